```python
import jax, jax.numpy as jnp
from jax import lax
import numpy as np

D_MODEL = 1024
BATCH = 16
SEQ = 4096
DEPTH = 2

CHUNK = 64
Q_BLOCK = 128
MEM_LEN = 256
MAX_STREAM_OFFSET = 4096
EPS = 1e-6
NEG_INF = -1e30

A_HEADS = 4
A_HEAD_DIM = 64
FORGET_BIAS_INIT = 3.0
B_GROUPS = 4
B_GROUP_DIM = 64
B_WINDOW = 128
C_HEADS = 8
C_NOPE_DIM = 64
C_ROPE_DIM = 32
C_V_DIM = 64
C_Q_RANK = 256
C_KV_RANK = 128
ROPE_THETA = 10000.0
M_HEADS = 4
M_HEAD_DIM = 128
D_FF = 2752
N_EXPERTS = 8
TOP_K = 2
D_FF_EXPERT = 1408

A_W = A_HEADS * A_HEAD_DIM
B_W = B_GROUPS * B_GROUP_DIM
C_W = C_HEADS * C_V_DIM
MIX_W = A_W + B_W + C_W
M_W = M_HEADS * M_HEAD_DIM
IN_SIZES = (A_W, A_W, A_W, A_HEADS, B_W, B_W, C_Q_RANK, C_KV_RANK, C_ROPE_DIM)
IN_W = int(sum(IN_SIZES))
IN_OFFSETS = tuple(int(v) for v in np.cumsum(IN_SIZES)[:-1])
N_DENSE = (DEPTH + 1) // 2
N_MOE = DEPTH // 2

kernel_name = 'hybrid_fox_gmlp_mla_moe_block'


def _rmsnorm(x, g):
    xf = x.astype(jnp.float32)
    y = xf * lax.rsqrt(jnp.mean(xf * xf, axis=-1, keepdims=True) + EPS)
    return (y * g.astype(jnp.float32)).astype(x.dtype)


def _layernorm(x, g):
    xf = x.astype(jnp.float32)
    mu = jnp.mean(xf, axis=-1, keepdims=True)
    var = jnp.mean(jnp.square(xf - mu), axis=-1, keepdims=True)
    return ((xf - mu) * lax.rsqrt(var + EPS) * g.astype(jnp.float32)).astype(x.dtype)


def _rope(x, pos):
    half = x.shape[-1] // 2
    inv = ROPE_THETA ** (-jnp.arange(half, dtype=jnp.float32) / half)
    ang = pos.astype(jnp.float32)[..., None] * inv
    ang = ang.reshape(ang.shape[:2] + (1,) * (x.ndim - 3) + (half,))
    cos, sin = jnp.cos(ang), jnp.sin(ang)
    x1 = x[..., :half].astype(jnp.float32)
    x2 = x[..., half:].astype(jnp.float32)
    return jnp.concatenate([x1 * cos - x2 * sin, x1 * sin + x2 * cos], axis=-1).astype(x.dtype)


def _swiglu(h, w_gate, w_up, w_down):
    return (jax.nn.silu(h @ w_gate) * (h @ w_up)) @ w_down


def _sweep_attention(q, k, v, scale, unit, log_decay):
    bn, nh, s_len, dk = q.shape
    nb = s_len // Q_BLOCK
    k_pos = jnp.arange(s_len)
    q_blocks = q.reshape(bn, nh, nb, Q_BLOCK, dk).transpose(2, 0, 1, 3, 4)
    if log_decay is None:
        xs = (q_blocks, jnp.arange(nb))
    else:
        xs = (q_blocks, jnp.arange(nb), log_decay.reshape(bn, nh, nb, Q_BLOCK).transpose(2, 0, 1, 3))

    def block(blk):
        qi, i = blk[0], blk[1]
        s = jnp.einsum('bhqd,bhkd->bhqk', qi, k, preferred_element_type=jnp.float32) * scale
        if log_decay is not None:
            s = s + blk[2][..., :, None] - log_decay[:, :, None, :]
        q_pos = i * Q_BLOCK + jnp.arange(Q_BLOCK)
        allowed = (k_pos[None, :] // unit) <= (q_pos[:, None] // unit)
        s = jnp.where(allowed, s, NEG_INF)
        p = jax.nn.softmax(s, axis=-1)
        return jnp.einsum('bhqk,bhkd->bhqd', p.astype(v.dtype), v)

    out = lax.map(block, xs)
    return out.transpose(1, 2, 0, 3, 4).reshape(bn, nh, s_len, v.shape[-1])


def _forgetting_attention(qa, ka, va, fa, b_forget, q_gain, k_gain):
    bn, s_len, _ = qa.shape
    q = _rmsnorm(qa.reshape(bn, s_len, A_HEADS, A_HEAD_DIM), q_gain)
    k = _rmsnorm(ka.reshape(bn, s_len, A_HEADS, A_HEAD_DIM), k_gain)
    v = va.reshape(bn, s_len, A_HEADS, A_HEAD_DIM)
    log_f = jax.nn.log_sigmoid((fa + b_forget).astype(jnp.float32))
    cum = jnp.cumsum(log_f, axis=1).transpose(0, 2, 1)
    o = _sweep_attention(q.transpose(0, 2, 1, 3), k.transpose(0, 2, 1, 3), v.transpose(0, 2, 1, 3),
                         A_HEAD_DIM ** -0.5, 1, cum)
    return o.transpose(0, 2, 1, 3).reshape(bn, s_len, A_W)


def _spatial_gating(ub, vb, v_gain, w_s, b_s):
    bn, s_len, _ = ub.shape
    u = jax.nn.gelu(ub, approximate=False)
    v = jax.nn.gelu(vb, approximate=False).reshape(bn, s_len, B_GROUPS, B_GROUP_DIM)
    v = _layernorm(v, v_gain.reshape(B_GROUPS, B_GROUP_DIM))
    v = v.reshape(bn, s_len // B_WINDOW, B_WINDOW, B_GROUPS, B_GROUP_DIM)
    pos = jnp.arange(B_WINDOW)
    mask = (pos[None, :] // CHUNK) <= (pos[:, None] // CHUNK)
    w = jnp.where(mask[None], w_s, 0.0)
    y = jnp.einsum('gts,bnsgd->bntgd', w, v) + b_s.T[None, None, :, :, None]
    return u * y.reshape(bn, s_len, B_W)


def _latent_attention(cq, ckv, kr, positions, cq_gain, w_uq, ckv_gain, w_ukv,
                      qn_g, qr_g, kn_g, kr_g):
    bn, s_len, _ = cq.shape
    q = jnp.einsum('bsr,rhd->bshd', _rmsnorm(cq, cq_gain), w_uq)
    q_nope = _rmsnorm(q[..., :C_NOPE_DIM], qn_g)
    q_rope = _rope(_rmsnorm(q[..., C_NOPE_DIM:], qr_g), positions)
    kv = jnp.einsum('bsr,rhd->bshd', _rmsnorm(ckv, ckv_gain), w_ukv)
    k_nope = _rmsnorm(kv[..., :C_NOPE_DIM], kn_g)
    v = kv[..., C_NOPE_DIM:]
    k_rope = _rope(_rmsnorm(kr, kr_g), positions)
    k = jnp.concatenate([k_nope, jnp.broadcast_to(k_rope[:, :, None, :],
                                                  (bn, s_len, C_HEADS, C_ROPE_DIM))], axis=-1)
    q = jnp.concatenate([q_nope, q_rope], axis=-1)
    o = _sweep_attention(q.transpose(0, 2, 1, 3), k.transpose(0, 2, 1, 3), v.transpose(0, 2, 1, 3),
                         (C_NOPE_DIM + C_ROPE_DIM) ** -0.5, CHUNK, None)
    return o.transpose(0, 2, 1, 3).reshape(bn, s_len, C_W)


def _memory_attention(h, mem_n, w_q, w_kv, q_g, k_g, w_o):
    bn, s_len, _ = h.shape
    m_len = mem_n.shape[1]
    q = _rmsnorm((h @ w_q).reshape(bn, s_len, M_HEADS, M_HEAD_DIM), q_g)
    kv = (mem_n @ w_kv).reshape(bn, m_len, 2, M_HEADS, M_HEAD_DIM)
    k = _rmsnorm(kv[:, :, 0], k_g)
    v = kv[:, :, 1]
    s = jnp.einsum('bshd,bmhd->bhsm', q, k, preferred_element_type=jnp.float32) * M_HEAD_DIM ** -0.5
    p = jax.nn.softmax(s, axis=-1)
    o = jnp.einsum('bhsm,bmhd->bshd', p.astype(v.dtype), v).reshape(bn, s_len, M_W)
    return o @ w_o


def _moe(h, w_router, b_router, w_gate, w_up, w_down):
    logits = (h @ w_router).astype(jnp.float32) + b_router.astype(jnp.float32)
    top_v, top_i = lax.top_k(logits, TOP_K)
    gates = jax.nn.softmax(top_v, axis=-1)
    combine = jnp.sum(jax.nn.one_hot(top_i, N_EXPERTS, dtype=jnp.float32) * gates[..., None],
                      axis=-2).astype(h.dtype)
    y = jnp.zeros_like(h)
    for e in range(N_EXPERTS):
        y = y + combine[..., e:e + 1] * _swiglu(h, w_gate[e], w_up[e], w_down[e])
    return y


def setup_inputs(seed: int = 0) -> dict:
    key = jax.random.key(seed)
    ks = iter(jax.random.split(key, 64))

    def nrm(shape, fan_in, mult=1.0):
        return mult * fan_in ** -0.5 * jax.random.normal(next(ks), shape, jnp.float32)

    def gain(shape):
        return 1.0 + 0.05 * jax.random.normal(next(ks), shape, jnp.float32)

    x = jax.random.normal(next(ks), (BATCH, SEQ, D_MODEL), jnp.float32)
    mem = jax.random.normal(next(ks), (BATCH, MEM_LEN, D_MODEL), jnp.float32)
    offsets = jax.random.randint(next(ks), (BATCH, 1), 0, MAX_STREAM_OFFSET, dtype=jnp.int32)
    positions = offsets + jnp.arange(SEQ, dtype=jnp.int32)[None, :]
    return {
        'x': x,
        'mem': mem,
        'positions': positions,
        'mix_norm': gain((DEPTH, D_MODEL)),
        'w_in': nrm((DEPTH, D_MODEL, IN_W), D_MODEL),
        'b_forget': FORGET_BIAS_INIT + 0.5 * jax.random.normal(next(ks), (DEPTH, A_HEADS), jnp.float32),
        'a_q_norm': gain((DEPTH, A_HEAD_DIM)),
        'a_k_norm': gain((DEPTH, A_HEAD_DIM)),
        'b_v_norm': gain((DEPTH, B_W)),
        'b_spatial_w': nrm((DEPTH, B_GROUPS, B_WINDOW, B_WINDOW), B_WINDOW, 0.5),
        'b_spatial_b': gain((DEPTH, B_GROUPS, B_WINDOW)),
        'c_q_lat_norm': gain((DEPTH, C_Q_RANK)),
        'c_w_uq': nrm((DEPTH, C_Q_RANK, C_HEADS, C_NOPE_DIM + C_ROPE_DIM), C_Q_RANK),
        'c_kv_lat_norm': gain((DEPTH, C_KV_RANK)),
        'c_w_ukv': nrm((DEPTH, C_KV_RANK, C_HEADS, C_NOPE_DIM + C_V_DIM), C_KV_RANK),
        'c_q_nope_norm': gain((DEPTH, C_NOPE_DIM)),
        'c_q_rope_norm': gain((DEPTH, C_ROPE_DIM)),
        'c_k_nope_norm': gain((DEPTH, C_NOPE_DIM)),
        'c_k_rope_norm': gain((DEPTH, C_ROPE_DIM)),
        'out_norm_a': gain((DEPTH, A_W)),
        'out_norm_b': gain((DEPTH, B_W)),
        'out_norm_c': gain((DEPTH, C_W)),
        'w_out': nrm((DEPTH, MIX_W, D_MODEL), MIX_W),
        'xattn_norm': gain((DEPTH, D_MODEL)),
        'mem_norm': gain((DEPTH, D_MODEL)),
        'w_mem_q': nrm((DEPTH, D_MODEL, M_W), D_MODEL),
        'w_mem_kv': nrm((DEPTH, D_MODEL, 2 * M_W), D_MODEL),
        'm_q_norm': gain((DEPTH, M_HEAD_DIM)),
        'm_k_norm': gain((DEPTH, M_HEAD_DIM)),
        'w_mem_out': nrm((DEPTH, M_W, D_MODEL), M_W),
        'ffn_norm': gain((DEPTH, D_MODEL)),
        'ffn_w_gate': nrm((N_DENSE, D_MODEL, D_FF), D_MODEL),
        'ffn_w_up': nrm((N_DENSE, D_MODEL, D_FF), D_MODEL),
        'ffn_w_down': nrm((N_DENSE, D_FF, D_MODEL), D_FF),
        'w_router': nrm((N_MOE, D_MODEL, N_EXPERTS), D_MODEL),
        'b_router': 0.01 * jax.random.normal(next(ks), (N_MOE, N_EXPERTS), jnp.float32),
        'moe_w_gate': nrm((N_MOE, N_EXPERTS, D_MODEL, D_FF_EXPERT), D_MODEL),
        'moe_w_up': nrm((N_MOE, N_EXPERTS, D_MODEL, D_FF_EXPERT), D_MODEL),
        'moe_w_down': nrm((N_MOE, N_EXPERTS, D_FF_EXPERT, D_MODEL), D_FF_EXPERT),
    }


def reference(x, mem, positions, mix_norm, w_in, b_forget, a_q_norm, a_k_norm, b_v_norm,
              b_spatial_w, b_spatial_b, c_q_lat_norm, c_w_uq, c_kv_lat_norm, c_w_ukv,
              c_q_nope_norm, c_q_rope_norm, c_k_nope_norm, c_k_rope_norm,
              out_norm_a, out_norm_b, out_norm_c, w_out, xattn_norm, mem_norm,
              w_mem_q, w_mem_kv, m_q_norm, m_k_norm, w_mem_out, ffn_norm,
              ffn_w_gate, ffn_w_up, ffn_w_down, w_router, b_router,
              moe_w_gate, moe_w_up, moe_w_down):
    for l in range(DEPTH):
        h = _rmsnorm(x, mix_norm[l])
        proj = h @ w_in[l]
        qa, ka, va, fa, ub, vb, cq, ckv, kr = jnp.split(proj, IN_OFFSETS, axis=-1)
        a = _forgetting_attention(qa, ka, va, fa, b_forget[l], a_q_norm[l], a_k_norm[l])
        b = _spatial_gating(ub, vb, b_v_norm[l], b_spatial_w[l], b_spatial_b[l])
        c = _latent_attention(cq, ckv, kr, positions, c_q_lat_norm[l], c_w_uq[l],
                              c_kv_lat_norm[l], c_w_ukv[l], c_q_nope_norm[l], c_q_rope_norm[l],
                              c_k_nope_norm[l], c_k_rope_norm[l])
        o = jnp.concatenate([_rmsnorm(a, out_norm_a[l]), _rmsnorm(b, out_norm_b[l]),
                             _rmsnorm(c, out_norm_c[l])], axis=-1)
        x = x + o @ w_out[l]
        x = x + _memory_attention(_rmsnorm(x, xattn_norm[l]), _rmsnorm(mem, mem_norm[l]),
                                  w_mem_q[l], w_mem_kv[l], m_q_norm[l], m_k_norm[l], w_mem_out[l])
        h = _rmsnorm(x, ffn_norm[l])
        if l % 2 == 0:
            x = x + _swiglu(h, ffn_w_gate[l // 2], ffn_w_up[l // 2], ffn_w_down[l // 2])
        else:
            m = l // 2
            x = x + _moe(h, w_router[m], b_router[m], moe_w_gate[m], moe_w_up[m], moe_w_down[m])
    return x
```

```python
import functools

import numpy as np
import jax
import jax.numpy as jnp
from jax import lax
from jax.experimental import pallas as pl
from jax.experimental.pallas import tpu as pltpu

F32 = jnp.float32
BF16 = jnp.bfloat16
HIGHEST = lax.Precision.HIGHEST

D_MODEL = 1024
CHUNK = 64
EPS = 1e-6
NEG_INF = -1e30
A_HEADS, A_HEAD_DIM = 4, 64
B_GROUPS, B_GROUP_DIM, B_WINDOW = 4, 64, 128
C_HEADS, C_NOPE_DIM, C_ROPE_DIM, C_V_DIM = 8, 64, 32, 64
C_Q_RANK, C_KV_RANK = 256, 128
ROPE_THETA = 10000.0
M_HEADS, M_HEAD_DIM = 4, 128
N_EXPERTS = 8
A_W = A_HEADS * A_HEAD_DIM
B_W = B_GROUPS * B_GROUP_DIM
C_W = C_HEADS * C_V_DIM
M_W = M_HEADS * M_HEAD_DIM

LANES = 128

SEG_Q, SEG_K, SEG_V, SEG_U, SEG_VB, SEG_CQ, SEG_CKV, SEG_MISC = 0, 256, 512, 768, 1024, 1280, 1536, 1664
IN_PAD_W = SEG_MISC + LANES
ROPE_LANE = C_NOPE_DIM
FORGET_LANE = 96
HEAD_SLAB = LANES

ROW_TILE = 512
ATTN_TILE = 512
VMEM_LIMIT = 56 * 1024 * 1024


def _cparams(*sem):
    return pltpu.CompilerParams(dimension_semantics=sem, vmem_limit_bytes=VMEM_LIMIT)


def _full(shape):
    n = len(shape)
    return pl.BlockSpec(shape, lambda *_: (0,) * n)


def _rms(x):
    return x * lax.rsqrt(jnp.mean(x * x, axis=-1, keepdims=True) + EPS)


def _lane_iota(n=LANES):
    return lax.broadcasted_iota(jnp.int32, (1, n), 1)


def _rope_table_kernel(pos_ref, inv_ref, sgn_ref, cos_ref, sin_ref):
    ang = pos_ref[...] * inv_ref[...]
    cos_ref[...] = jnp.cos(ang)
    sin_ref[...] = jnp.sin(ang) * sgn_ref[...]


def _rope_tables(pos_col):
    t = pos_col.shape[0]
    half = C_ROPE_DIM // 2
    inv = ROPE_THETA ** (-jnp.arange(half, dtype=F32) / half)
    inv_l = jnp.zeros((1, LANES), F32).at[0, ROPE_LANE:ROPE_LANE + C_ROPE_DIM].set(jnp.tile(inv, 2))
    sgn = np.zeros((1, LANES), np.float32)
    sgn[0, ROPE_LANE:ROPE_LANE + half] = -1.0
    sgn[0, ROPE_LANE + half:ROPE_LANE + C_ROPE_DIM] = 1.0
    tm = ROW_TILE
    return pl.pallas_call(
        _rope_table_kernel,
        grid=(t // tm,),
        in_specs=[pl.BlockSpec((tm, 1), lambda i: (i, 0)), _full((1, LANES)), _full((1, LANES))],
        out_specs=[pl.BlockSpec((tm, LANES), lambda i: (i, 0))] * 2,
        out_shape=[jax.ShapeDtypeStruct((t, LANES), F32)] * 2,
        compiler_params=_cparams("parallel"),
        name="rope_tables",
    )(pos_col, inv_l, jnp.asarray(sgn))


def _rotate(x, cos, sin_signed, lane):
    half = C_ROPE_DIM // 2
    partner = jnp.where(lane < ROPE_LANE + half,
                        pltpu.roll(x, LANES - half, 1), pltpu.roll(x, half, 1))
    return x * cos + partner * sin_signed


def _gelu(x):
    return 0.5 * x * (1.0 + lax.erf(x * np.float32(1.0 / np.sqrt(2.0))))


def _inproj_kernel(x_ref, g_ref, w_ref, aq_ref, ak_ref, fb_ref, bvg_ref, ws_ref, bs_ref, onb_ref,
                   cqg_ref, ckvg_ref, gm_ref, tri_ref,
                   qa_ref, ka_ref, va_ref, ft_ref, bn_ref, cq_ref, ckv_ref, misc_ref,
                   carry_ref, *, tiles_per_seq):
    i = pl.program_id(0)
    tm = x_ref.shape[0]
    h = _rms(x_ref[...]) * g_ref[...]
    proj = jnp.dot(h.astype(BF16), w_ref[...], preferred_element_type=F32)
    gm = gm_ref[...]

    def group_mean(v):
        return jnp.dot(v, gm, preferred_element_type=F32, precision=HIGHEST)

    q = proj[:, SEG_Q:SEG_Q + A_W]
    qa_ref[...] = (q * lax.rsqrt(group_mean(q * q) + EPS) * aq_ref[...]).astype(BF16)
    k = proj[:, SEG_K:SEG_K + A_W]
    ka_ref[...] = (k * lax.rsqrt(group_mean(k * k) + EPS) * ak_ref[...]).astype(BF16)
    va_ref[...] = proj[:, SEG_V:SEG_V + A_W].astype(BF16)

    misc = proj[:, SEG_MISC:SEG_MISC + LANES]
    misc_ref[...] = misc
    z = misc + fb_ref[...]
    log_f = jnp.minimum(z, 0.0) - jnp.log1p(jnp.exp(-jnp.abs(z)))

    @pl.when(i % tiles_per_seq == 0)
    def _():
        carry_ref[...] = jnp.zeros_like(carry_ref)

    cum = jnp.dot(tri_ref[...], log_f, preferred_element_type=F32, precision=HIGHEST) + carry_ref[...]
    carry_ref[...] = cum[tm - 1:tm, :]
    cum_t = cum.T
    ft_ref[0, 0] = cum_t[FORGET_LANE:FORGET_LANE + 8, :]
    ft_ref[0, 1] = cum_t[FORGET_LANE + 8:FORGET_LANE + 16, :]

    u = _gelu(proj[:, SEG_U:SEG_U + B_W])
    v = _gelu(proj[:, SEG_VB:SEG_VB + B_W])
    dv = v - group_mean(v)
    vn = dv * lax.rsqrt(group_mean(dv * dv) + EPS) * bvg_ref[...]
    group = lax.broadcasted_iota(jnp.int32, (1, B_W), 1) // B_GROUP_DIM
    for w in range(tm // B_WINDOW):
        rows = slice(w * B_WINDOW, (w + 1) * B_WINDOW)
        y_all = jnp.dot(ws_ref[...], vn[rows].astype(BF16), preferred_element_type=F32)
        y = bs_ref[...]
        for g in range(B_GROUPS):
            y = y + jnp.where(group == g, y_all[g * B_WINDOW:(g + 1) * B_WINDOW], 0.0)
        b = u[rows] * y
        bn_ref[rows, :] = (_rms(b) * onb_ref[...]).astype(BF16)

    cq_ref[...] = (_rms(proj[:, SEG_CQ:SEG_CQ + C_Q_RANK]) * cqg_ref[...]).astype(BF16)
    ckv_ref[...] = (_rms(proj[:, SEG_CKV:SEG_CKV + C_KV_RANK]) * ckvg_ref[...]).astype(BF16)


def _inproj(x2d, seq, p):
    t = x2d.shape[0]
    tm = ROW_TILE
    tps = seq // tm
    nb = t // seq
    row = lambda w: pl.BlockSpec((tm, w), lambda i: (i, 0))
    out_shape = [
        jax.ShapeDtypeStruct((t, A_W), BF16), jax.ShapeDtypeStruct((t, A_W), BF16),
        jax.ShapeDtypeStruct((t, A_W), BF16),
        jax.ShapeDtypeStruct((nb, 2, 8, seq), F32),
        jax.ShapeDtypeStruct((t, B_W), BF16),
        jax.ShapeDtypeStruct((t, C_Q_RANK), BF16), jax.ShapeDtypeStruct((t, C_KV_RANK), BF16),
        jax.ShapeDtypeStruct((t, LANES), F32),
    ]
    out_specs = [row(A_W), row(A_W), row(A_W),
                 pl.BlockSpec((1, 2, 8, tm), lambda i: (i // tps, 0, 0, i % tps)),
                 row(B_W), row(C_Q_RANK), row(C_KV_RANK), row(LANES)]
    consts = [p["mix_g"], p["w_in"], p["aq"], p["ak"], p["fb"], p["bvg"], p["ws"], p["bs"], p["onb"],
              p["cqg"], p["ckvg"], p["gm"], p["tri"]]
    return pl.pallas_call(
        functools.partial(_inproj_kernel, tiles_per_seq=tps),
        grid=(t // tm,),
        in_specs=[row(D_MODEL)] + [_full(c.shape) for c in consts],
        out_specs=out_specs,
        out_shape=out_shape,
        scratch_shapes=[pltpu.VMEM((1, LANES), F32)],
        compiler_params=_cparams("arbitrary"),
        name="in_proj",
    )(x2d, *consts)


def _mla_prep_kernel(cq_ref, ckv_ref, misc_ref, cos_ref, sin_ref, wuq_ref, wuk_ref, wuv_ref,
                     gq_ref, gkn_ref, gkr_ref, qc_ref, kc_ref, vc_ref):
    lane = _lane_iota()
    nope = lane < C_NOPE_DIM
    rope = (lane >= ROPE_LANE) & (lane < ROPE_LANE + C_ROPE_DIM)
    cos, sin = cos_ref[...], sin_ref[...]
    q = jnp.dot(cq_ref[...], wuq_ref[...], preferred_element_type=F32)
    kn = jnp.dot(ckv_ref[...], wuk_ref[...], preferred_element_type=F32)
    vc_ref[...] = jnp.dot(ckv_ref[...], wuv_ref[...], preferred_element_type=F32).astype(BF16)

    kr = jnp.where(rope, misc_ref[...], 0.0)
    kr = kr * lax.rsqrt(jnp.sum(kr * kr, axis=-1, keepdims=True) * (1.0 / C_ROPE_DIM) + EPS) * gkr_ref[...]
    kr = _rotate(kr, cos, sin, lane)

    for hd in range(C_HEADS):
        cols = slice(hd * HEAD_SLAB, (hd + 1) * HEAD_SLAB)
        qh = q[:, cols]
        sq = qh * qh
        r_n = lax.rsqrt(jnp.sum(jnp.where(nope, sq, 0.0), axis=-1, keepdims=True) * (1.0 / C_NOPE_DIM) + EPS)
        r_r = lax.rsqrt(jnp.sum(jnp.where(rope, sq, 0.0), axis=-1, keepdims=True) * (1.0 / C_ROPE_DIM) + EPS)
        qn = qh * jnp.where(nope, r_n, r_r) * gq_ref[...]
        qc_ref[:, cols] = _rotate(qn, cos, sin, lane).astype(BF16)
        kh = kn[:, cols]
        r_k = lax.rsqrt(jnp.sum(kh * kh, axis=-1, keepdims=True) * (1.0 / C_NOPE_DIM) + EPS)
        kc_ref[:, cols] = (kh * r_k * gkn_ref[...] + kr).astype(BF16)


def _mla_prep(cq, ckv, misc, cos, sin, p):
    t = cq.shape[0]
    tm = ROW_TILE
    row = lambda w: pl.BlockSpec((tm, w), lambda i: (i, 0))
    consts = [p["wuq"], p["wuk"], p["wuv"], p["gq"], p["gkn"], p["gkr"]]
    qk_w = C_HEADS * HEAD_SLAB
    return pl.pallas_call(
        _mla_prep_kernel,
        grid=(t // tm,),
        in_specs=[row(C_Q_RANK), row(C_KV_RANK), row(LANES), row(LANES), row(LANES)]
                 + [_full(c.shape) for c in consts],
        out_specs=[row(qk_w), row(qk_w), row(C_W)],
        out_shape=[jax.ShapeDtypeStruct((t, qk_w), BF16), jax.ShapeDtypeStruct((t, qk_w), BF16),
                   jax.ShapeDtypeStruct((t, C_W), BF16)],
        compiler_params=_cparams("parallel"),
        name="mla_prep",
    )(cq, ckv, misc, cos, sin, *consts)


def _attn_kernel(*refs, unit, decay, shared_slab):
    if decay:
        q_ref, k_ref, v_ref, ft_ref, o_ref, m_ref, l_ref, acc_ref = refs
    else:
        q_ref, k_ref, v_ref, o_ref, m_ref, l_ref, acc_ref = refs
        ft_ref = None
    i = pl.program_id(2)
    tq = q_ref.shape[0]
    tk = tq
    lane = _lane_iota()

    qs = []
    for hh in range(2):
        if shared_slab:
            qs.append(jnp.where(lane // A_HEAD_DIM == hh, q_ref[...], jnp.zeros((), BF16)))
        else:
            qs.append(q_ref[:, hh * HEAD_SLAB:(hh + 1) * HEAD_SLAB])
    kcols = [slice(0, LANES)] * 2 if shared_slab else [slice(0, HEAD_SLAB), slice(HEAD_SLAB, 2 * HEAD_SLAB)]

    m_ref[...] = jnp.full(m_ref.shape, NEG_INF, F32)
    l_ref[...] = jnp.zeros(l_ref.shape, F32)
    acc_ref[...] = jnp.zeros(acc_ref.shape, F32)

    q_start = pl.multiple_of(i * tq, tq)
    if decay:
        f0 = [ft_ref[0, 0, hh:hh + 1, pl.ds(q_start, LANES)][:, 0:1] for hh in range(2)]

    def step(j, masked):
        k_start = pl.multiple_of(j * tk, tk)
        v_blk = v_ref[pl.ds(k_start, tk), :]
        if masked:
            qpos = lax.broadcasted_iota(jnp.int32, (tq, tk), 0) // unit
            kpos = lax.broadcasted_iota(jnp.int32, (tq, tk), 1) // unit
            allowed = kpos <= qpos
        for hh in range(2):
            k_blk = k_ref[pl.ds(k_start, tk), kcols[hh]]
            s = lax.dot_general(qs[hh], k_blk, (((1,), (1,)), ((), ())), preferred_element_type=F32)
            if decay:
                s = s - (ft_ref[0, 0, hh:hh + 1, pl.ds(k_start, tk)] - f0[hh])
            if masked:
                s = jnp.where(allowed, s, NEG_INF)
            m_prev = m_ref[hh]
            m_new = jnp.maximum(m_prev, jnp.max(s, axis=1, keepdims=True))
            p = jnp.exp(s - m_new[:, 0:1])
            alpha = jnp.exp(m_prev - m_new)
            l_ref[hh] = alpha * l_ref[hh] + jnp.sum(p, axis=1, keepdims=True)
            acc_ref[hh] = alpha * acc_ref[hh] + jnp.dot(p.astype(BF16), v_blk, preferred_element_type=F32)
            m_ref[hh] = m_new

    def body(j, carry):
        step(j, False)
        return carry

    lax.fori_loop(0, i, body, 0)
    step(i, True)

    o0 = acc_ref[0] / l_ref[0]
    o1 = acc_ref[1] / l_ref[1]
    o_ref[...] = jnp.where(lane < C_V_DIM, o0, o1).astype(o_ref.dtype)


def _attention(q, k, v, ft, seq, *, unit, shared_slab):
    t = q.shape[0]
    nb = t // seq
    tq = ATTN_TILE
    nq = seq // tq
    n_pairs = v.shape[1] // LANES
    qk_w = LANES if shared_slab else 2 * HEAD_SLAB
    decay = ft is not None
    in_specs = [pl.BlockSpec((tq, qk_w), lambda b, p, i: (b * nq + i, p)),
                pl.BlockSpec((seq, qk_w), lambda b, p, i: (b, p)),
                pl.BlockSpec((seq, LANES), lambda b, p, i: (b, p))]
    args = [q, k, v]
    if decay:
        in_specs.append(pl.BlockSpec((1, 1, 8, seq), lambda b, p, i: (b, p, 0, 0)))
        args.append(ft)
    return pl.pallas_call(
        functools.partial(_attn_kernel, unit=unit, decay=decay, shared_slab=shared_slab),
        grid=(nb, n_pairs, nq),
        in_specs=in_specs,
        out_specs=pl.BlockSpec((tq, LANES), lambda b, p, i: (b * nq + i, p)),
        out_shape=jax.ShapeDtypeStruct((t, n_pairs * LANES), BF16),
        scratch_shapes=[pltpu.VMEM((2, tq, LANES), F32), pltpu.VMEM((2, tq, LANES), F32),
                        pltpu.VMEM((2, tq, LANES), F32)],
        compiler_params=_cparams("parallel", "parallel", "arbitrary"),
        name="attn_fox" if decay else "attn_mla",
    )(*args)


def _mem_kv_kernel(mem_ref, g_ref, w_ref, kg_ref, k_ref, v_ref):
    mn = (_rms(mem_ref[0]) * g_ref[...]).astype(BF16)
    kv = jnp.dot(mn, w_ref[...], preferred_element_type=F32)
    for hd in range(M_HEADS):
        cols = slice(hd * M_HEAD_DIM, (hd + 1) * M_HEAD_DIM)
        k_ref[0, :, cols] = (_rms(kv[:, cols]) * kg_ref[...]).astype(BF16)
    v_ref[0] = kv[:, M_W:].astype(BF16)


def _mem_kv(mem, p):
    nb, ml, _ = mem.shape
    consts = [p["mem_g"], p["w_mem_kv"], p["mkg"]]
    blk = pl.BlockSpec((1, ml, M_W), lambda b: (b, 0, 0))
    return pl.pallas_call(
        _mem_kv_kernel,
        grid=(nb,),
        in_specs=[pl.BlockSpec((1, ml, D_MODEL), lambda b: (b, 0, 0))] + [_full(c.shape) for c in consts],
        out_specs=[blk, blk],
        out_shape=[jax.ShapeDtypeStruct((nb, ml, M_W), BF16)] * 2,
        compiler_params=_cparams("parallel"),
        name="mem_kv",
    )(mem, *consts)


def _outproj_kernel(x_ref, a_ref, bn_ref, c_ref, ona_ref, onc_ref, wo_ref, xg_ref, wq_ref, mqg_ref,
                    km_ref, vm_ref, wmo_ref, o_ref):
    a_n = (_rms(a_ref[...].astype(F32)) * ona_ref[...]).astype(BF16)
    c_n = (_rms(c_ref[...].astype(F32)) * onc_ref[...]).astype(BF16)
    mix = jnp.concatenate([a_n, bn_ref[...], c_n], axis=-1)
    x1 = x_ref[...] + jnp.dot(mix, wo_ref[...], preferred_element_type=F32)

    h = (_rms(x1) * xg_ref[...]).astype(BF16)
    q = jnp.dot(h, wq_ref[...], preferred_element_type=F32)
    outs = []
    for hd in range(M_HEADS):
        cols = slice(hd * M_HEAD_DIM, (hd + 1) * M_HEAD_DIM)
        qh = (_rms(q[:, cols]) * mqg_ref[...]).astype(BF16)
        s = lax.dot_general(qh, km_ref[0, :, cols], (((1,), (1,)), ((), ())), preferred_element_type=F32)
        e = jnp.exp(s - jnp.max(s, axis=-1, keepdims=True))
        pr = e / jnp.sum(e, axis=-1, keepdims=True)
        outs.append(jnp.dot(pr.astype(BF16), vm_ref[0, :, cols], preferred_element_type=F32).astype(BF16))
    o_ref[...] = x1 + jnp.dot(jnp.concatenate(outs, axis=-1), wmo_ref[...], preferred_element_type=F32)


def _outproj(x2d, a, bn, c, km, vm, seq, p):
    t = x2d.shape[0]
    tm = ROW_TILE
    tps = seq // tm
    ml = km.shape[1]
    row = lambda w: pl.BlockSpec((tm, w), lambda i: (i, 0))
    memblk = pl.BlockSpec((1, ml, M_W), lambda i: (i // tps, 0, 0))
    c1 = [p["ona"], p["onc"], p["w_out"], p["xg"], p["w_mem_q"], p["mqg"]]
    return pl.pallas_call(
        _outproj_kernel,
        grid=(t // tm,),
        in_specs=[row(D_MODEL), row(A_W), row(B_W), row(C_W)] + [_full(c_.shape) for c_ in c1]
                 + [memblk, memblk, _full(p["w_mem_out"].shape)],
        out_specs=row(D_MODEL),
        out_shape=jax.ShapeDtypeStruct((t, D_MODEL), F32),
        compiler_params=_cparams("parallel"),
        name="out_proj_mem_attn",
    )(x2d, a, bn, c, *c1, km, vm, p["w_mem_out"])


def _silu(x):
    return x * jax.nn.sigmoid(x)


def _ffn_kernel(x_ref, g_ref, wg_ref, wu_ref, wd_ref, o_ref, *, n_chunks):
    x = x_ref[...]
    h = (_rms(x) * g_ref[...]).astype(BF16)
    fc = wg_ref.shape[1] // n_chunks
    acc = x
    for c in range(n_chunks):
        cols = slice(c * fc, (c + 1) * fc)
        act = _silu(jnp.dot(h, wg_ref[:, cols], preferred_element_type=F32)) * \
            jnp.dot(h, wu_ref[:, cols], preferred_element_type=F32)
        acc = acc + jnp.dot(act.astype(BF16), wd_ref[cols, :], preferred_element_type=F32)
    o_ref[...] = acc


def _ffn(x2d, g, wg, wu, wd):
    t = x2d.shape[0]
    tm = ROW_TILE
    row = pl.BlockSpec((tm, D_MODEL), lambda i: (i, 0))
    resident = lambda a: pl.BlockSpec(a.shape, lambda i: (0, 0), pipeline_mode=pl.Buffered(1))
    return pl.pallas_call(
        functools.partial(_ffn_kernel, n_chunks=2),
        grid=(t // tm,),
        in_specs=[row, _full(g.shape), resident(wg), resident(wu), resident(wd)],
        out_specs=row,
        out_shape=jax.ShapeDtypeStruct((t, D_MODEL), F32),
        compiler_params=_cparams("parallel"),
        name="ffn_dense",
    )(x2d, g, wg, wu, wd)


def _moe_kernel(x_ref, g_ref, wr_ref, br_ref, wg_ref, wu_ref, wd_ref, o_ref, h_ref, comb_ref):
    e = pl.program_id(1)
    lane = _lane_iota()

    @pl.when(e == 0)
    def _():
        x = x_ref[...]
        h = _rms(x) * g_ref[...]
        h_ref[...] = h.astype(BF16)
        logits = jnp.dot(h, wr_ref[...], preferred_element_type=F32, precision=HIGHEST) + br_ref[...]
        logits = jnp.where(lane < N_EXPERTS, logits, -jnp.inf)
        v1 = jnp.max(logits, axis=-1, keepdims=True)
        i1 = jnp.min(jnp.where(logits == v1, lane, LANES), axis=-1, keepdims=True)
        rest = jnp.where(lane == i1, -jnp.inf, logits)
        v2 = jnp.max(rest, axis=-1, keepdims=True)
        i2 = jnp.min(jnp.where(rest == v2, lane, LANES), axis=-1, keepdims=True)
        e2 = jnp.exp(v2 - v1)
        g1 = 1.0 / (1.0 + e2)
        comb_ref[...] = jnp.where(lane == i1, g1, 0.0) + jnp.where(lane == i2, e2 * g1, 0.0)
        o_ref[...] = x

    h = h_ref[...]
    act = _silu(jnp.dot(h, wg_ref[0], preferred_element_type=F32)) * \
        jnp.dot(h, wu_ref[0], preferred_element_type=F32)
    y = jnp.dot(act.astype(BF16), wd_ref[0], preferred_element_type=F32)
    w_e = jnp.sum(jnp.where(lane == e, comb_ref[...], 0.0), axis=-1, keepdims=True)
    o_ref[...] += w_e * y


def _moe(x2d, g, wr, br, wg, wu, wd):
    t = x2d.shape[0]
    tm = ROW_TILE
    ff = wg.shape[2]
    row = pl.BlockSpec((tm, D_MODEL), lambda i, e: (i, 0))
    return pl.pallas_call(
        _moe_kernel,
        grid=(t // tm, N_EXPERTS),
        in_specs=[row, _full(g.shape), _full(wr.shape), _full(br.shape),
                  pl.BlockSpec((1, D_MODEL, ff), lambda i, e: (e, 0, 0)),
                  pl.BlockSpec((1, D_MODEL, ff), lambda i, e: (e, 0, 0)),
                  pl.BlockSpec((1, ff, D_MODEL), lambda i, e: (e, 0, 0))],
        out_specs=row,
        out_shape=jax.ShapeDtypeStruct((t, D_MODEL), F32),
        scratch_shapes=[pltpu.VMEM((tm, D_MODEL), BF16), pltpu.VMEM((tm, LANES), F32)],
        compiler_params=_cparams("parallel", "arbitrary"),
        name="moe_dense",
    )(x2d, g, wr, br, wg, wu, wd)


def _pad_cols(w, width):
    return jnp.pad(w, ((0, 0), (0, width - w.shape[1])))


def _layer_params(l, mix_norm, w_in, b_forget, a_q_norm, a_k_norm, b_v_norm, b_spatial_w, b_spatial_b,
                  c_q_lat_norm, c_w_uq, c_kv_lat_norm, c_w_ukv, c_q_nope_norm, c_q_rope_norm,
                  c_k_nope_norm, c_k_rope_norm, out_norm_a, out_norm_b, out_norm_c, w_out,
                  xattn_norm, mem_norm, w_mem_q, w_mem_kv, m_q_norm, m_k_norm, w_mem_out):
    p = {}
    o = np.cumsum((0, A_W, A_W, A_W, A_HEADS, B_W, B_W, C_Q_RANK, C_KV_RANK, C_ROPE_DIM))
    w = w_in[l]
    seg = lambda n: w[:, o[n]:o[n + 1]]
    fa = seg(3)
    misc = jnp.zeros((D_MODEL, LANES), F32)
    misc = misc.at[:, ROPE_LANE:ROPE_LANE + C_ROPE_DIM].set(seg(8))
    fb = jnp.zeros((1, LANES), F32)
    for hd in range(A_HEADS):
        ln = FORGET_LANE + 8 * (hd // 2) + hd % 2
        misc = misc.at[:, ln].set(fa[:, hd])
        fb = fb.at[0, ln].set(b_forget[l, hd])
    p["w_in"] = jnp.concatenate([seg(0), seg(1), seg(2), seg(4), seg(5), seg(6), seg(7), misc], axis=1).astype(BF16)
    p["fb"] = fb
    p["mix_g"] = mix_norm[l][None]
    p["aq"] = jnp.tile(a_q_norm[l], A_HEADS)[None] * (A_HEAD_DIM ** -0.5)
    p["ak"] = jnp.tile(a_k_norm[l], A_HEADS)[None]
    p["bvg"] = b_v_norm[l][None]
    pos = np.arange(B_WINDOW)
    mask = (pos[None, :] // CHUNK) <= (pos[:, None] // CHUNK)
    p["ws"] = jnp.where(mask[None], b_spatial_w[l], 0.0).reshape(B_GROUPS * B_WINDOW, B_WINDOW).astype(BF16)
    p["bs"] = jnp.repeat(b_spatial_b[l].T, B_GROUP_DIM, axis=1)
    p["onb"] = out_norm_b[l][None]
    p["cqg"] = c_q_lat_norm[l][None]
    p["ckvg"] = c_kv_lat_norm[l][None]
    gidx = np.arange(A_W) // A_HEAD_DIM
    p["gm"] = jnp.asarray((gidx[:, None] == gidx[None, :]).astype(np.float32) / A_HEAD_DIM)
    p["tri"] = jnp.asarray(np.tril(np.ones((ROW_TILE, ROW_TILE), np.float32)))

    qd = C_NOPE_DIM + C_ROPE_DIM
    p["wuq"] = jnp.pad(c_w_uq[l], ((0, 0), (0, 0), (0, HEAD_SLAB - qd))).reshape(C_Q_RANK, -1).astype(BF16)
    wukv = c_w_ukv[l]
    p["wuk"] = jnp.pad(wukv[:, :, :C_NOPE_DIM], ((0, 0), (0, 0), (0, HEAD_SLAB - C_NOPE_DIM))
                       ).reshape(C_KV_RANK, -1).astype(BF16)
    p["wuv"] = wukv[:, :, C_NOPE_DIM:].reshape(C_KV_RANK, C_W).astype(BF16)
    gq = jnp.concatenate([c_q_nope_norm[l], c_q_rope_norm[l]]) * (qd ** -0.5)
    p["gq"] = _pad_cols(gq[None], LANES)
    p["gkn"] = _pad_cols(c_k_nope_norm[l][None], LANES)
    p["gkr"] = jnp.zeros((1, LANES), F32).at[0, ROPE_LANE:ROPE_LANE + C_ROPE_DIM].set(c_k_rope_norm[l])

    p["ona"] = out_norm_a[l][None]
    p["onc"] = out_norm_c[l][None]
    p["w_out"] = w_out[l].astype(BF16)
    p["xg"] = xattn_norm[l][None]
    p["w_mem_q"] = w_mem_q[l].astype(BF16)
    p["mqg"] = m_q_norm[l][None] * (M_HEAD_DIM ** -0.5)
    p["mem_g"] = mem_norm[l][None]
    p["w_mem_kv"] = w_mem_kv[l].astype(BF16)
    p["mkg"] = m_k_norm[l][None]
    p["w_mem_out"] = w_mem_out[l].astype(BF16)
    return p


def kernel(x, mem, positions, mix_norm, w_in, b_forget, a_q_norm, a_k_norm, b_v_norm, b_spatial_w, b_spatial_b, c_q_lat_norm, c_w_uq, c_kv_lat_norm, c_w_ukv, c_q_nope_norm, c_q_rope_norm, c_k_nope_norm, c_k_rope_norm, out_norm_a, out_norm_b, out_norm_c, w_out, xattn_norm, mem_norm, w_mem_q, w_mem_kv, m_q_norm, m_k_norm, w_mem_out, ffn_norm, ffn_w_gate, ffn_w_up, ffn_w_down, w_router, b_router, moe_w_gate, moe_w_up, moe_w_down):
    nb, seq, d = x.shape
    assert d == D_MODEL and seq % ROW_TILE == 0 and seq % ATTN_TILE == 0
    depth = w_in.shape[0]
    t = nb * seq
    x2d = x.reshape(t, d)
    cos, sin = _rope_tables(positions.reshape(t, 1).astype(F32))

    for l in range(depth):
        p = _layer_params(l, mix_norm, w_in, b_forget, a_q_norm, a_k_norm, b_v_norm, b_spatial_w,
                          b_spatial_b, c_q_lat_norm, c_w_uq, c_kv_lat_norm, c_w_ukv, c_q_nope_norm,
                          c_q_rope_norm, c_k_nope_norm, c_k_rope_norm, out_norm_a, out_norm_b,
                          out_norm_c, w_out, xattn_norm, mem_norm, w_mem_q, w_mem_kv, m_q_norm,
                          m_k_norm, w_mem_out)
        qa, ka, va, ft, bn, cq, ckv, misc = _inproj(x2d, seq, p)
        qc, kc, vc = _mla_prep(cq, ckv, misc, cos, sin, p)
        a = _attention(qa, ka, va, ft, seq, unit=1, shared_slab=True)
        c = _attention(qc, kc, vc, None, seq, unit=CHUNK, shared_slab=False)
        km, vm = _mem_kv(mem, p)
        x2d = _outproj(x2d, a, bn, c, km, vm, seq, p)
        g = ffn_norm[l][None]
        if l % 2 == 0:
            m = l // 2
            ff = ffn_w_gate.shape[2]
            ff_pad = -(-ff // (2 * LANES)) * (2 * LANES)
            wg = _pad_cols(ffn_w_gate[m], ff_pad).astype(BF16)
            wu = _pad_cols(ffn_w_up[m], ff_pad).astype(BF16)
            wd = jnp.pad(ffn_w_down[m], ((0, ff_pad - ff), (0, 0))).astype(BF16)
            x2d = _ffn(x2d, g, wg, wu, wd)
        else:
            m = l // 2
            wr = _pad_cols(w_router[m], LANES)
            br = _pad_cols(b_router[m][None], LANES)
            x2d = _moe(x2d, g, wr, br, moe_w_gate[m].astype(BF16), moe_w_up[m].astype(BF16),
                       moe_w_down[m].astype(BF16))
    return x2d.reshape(nb, seq, d)
```

```python
import functools

import numpy as np
import jax
import jax.numpy as jnp
from jax import lax
from jax.experimental import pallas as pl
from jax.experimental.pallas import tpu as pltpu

F32 = jnp.float32
BF16 = jnp.bfloat16
HIGHEST = lax.Precision.HIGHEST

D_MODEL = 1024
CHUNK = 64
EPS = 1e-6
NEG_INF = -1e30
A_HEADS, A_HEAD_DIM = 4, 64
B_GROUPS, B_GROUP_DIM, B_WINDOW = 4, 64, 128
C_HEADS, C_NOPE_DIM, C_ROPE_DIM, C_V_DIM = 8, 64, 32, 64
C_Q_RANK, C_KV_RANK = 256, 128
ROPE_THETA = 10000.0
M_HEADS, M_HEAD_DIM = 4, 128
N_EXPERTS = 8
A_W = A_HEADS * A_HEAD_DIM
B_W = B_GROUPS * B_GROUP_DIM
C_W = C_HEADS * C_V_DIM
M_W = M_HEADS * M_HEAD_DIM

LANES = 128

SEG_Q, SEG_K, SEG_V, SEG_U, SEG_VB, SEG_CQ, SEG_CKV, SEG_MISC = 0, 256, 512, 768, 1024, 1280, 1536, 1664
IN_PAD_W = SEG_MISC + LANES
ROPE_LANE = C_NOPE_DIM
FORGET_LANE = 96
HEAD_SLAB = LANES
VT_ROWS = 80
LOG2E = float(np.log2(np.e))

ROW_TILE = 512
ATTN_TILE = 512
VMEM_LIMIT = 56 * 1024 * 1024


def _cparams(*sem):
    return pltpu.CompilerParams(dimension_semantics=sem, vmem_limit_bytes=VMEM_LIMIT)


def _full(shape):
    n = len(shape)
    return pl.BlockSpec(shape, lambda *_: (0,) * n)


def _rms(x):
    return x * lax.rsqrt(jnp.mean(x * x, axis=-1, keepdims=True) + EPS)


def _lane_iota(n=LANES):
    return lax.broadcasted_iota(jnp.int32, (1, n), 1)


def _rope_table_kernel(pos_ref, inv_ref, sgn_ref, cos_ref, sin_ref):
    ang = pos_ref[...] * inv_ref[...]
    cos_ref[...] = jnp.cos(ang)
    sin_ref[...] = jnp.sin(ang) * sgn_ref[...]


def _rope_tables(pos_col):
    t = pos_col.shape[0]
    half = C_ROPE_DIM // 2
    inv = ROPE_THETA ** (-jnp.arange(half, dtype=F32) / half)
    inv_l = jnp.zeros((1, LANES), F32).at[0, ROPE_LANE:ROPE_LANE + C_ROPE_DIM].set(jnp.tile(inv, 2))
    sgn = np.zeros((1, LANES), np.float32)
    sgn[0, ROPE_LANE:ROPE_LANE + half] = -1.0
    sgn[0, ROPE_LANE + half:ROPE_LANE + C_ROPE_DIM] = 1.0
    tm = ROW_TILE
    return pl.pallas_call(
        _rope_table_kernel,
        grid=(t // tm,),
        in_specs=[pl.BlockSpec((tm, 1), lambda i: (i, 0)), _full((1, LANES)), _full((1, LANES))],
        out_specs=[pl.BlockSpec((tm, LANES), lambda i: (i, 0))] * 2,
        out_shape=[jax.ShapeDtypeStruct((t, LANES), F32)] * 2,
        compiler_params=_cparams("parallel"),
        name="rope_tables",
    )(pos_col, inv_l, jnp.asarray(sgn))


def _rotate(x, cos, sin_signed, lane):
    half = C_ROPE_DIM // 2
    partner = jnp.where(lane < ROPE_LANE + half,
                        pltpu.roll(x, LANES - half, 1), pltpu.roll(x, half, 1))
    return x * cos + partner * sin_signed


def _store_v_transposed(vt_ref, v, n_heads):
    tm = v.shape[0]
    v_t = v.T
    tail = jnp.where(lax.broadcasted_iota(jnp.int32, (VT_ROWS - C_V_DIM, tm), 0) == 0, 1.0, 0.0).astype(BF16)
    for hd in range(n_heads):
        vt_ref[0, hd, 0:C_V_DIM, :] = v_t[hd * C_V_DIM:(hd + 1) * C_V_DIM, :].astype(BF16)
        vt_ref[0, hd, C_V_DIM:VT_ROWS, :] = tail


def _gelu(x):
    return 0.5 * x * (1.0 + lax.erf(x * np.float32(1.0 / np.sqrt(2.0))))


def _inproj_kernel(x_ref, g_ref, w_ref, aq_ref, ak_ref, fb_ref, bvg_ref, ws_ref, bs_ref, onb_ref,
                   cqg_ref, ckvg_ref, gm_ref, tri_ref,
                   qa_ref, ka_ref, vt_ref, bn_ref, cq_ref, ckv_ref, misc_ref,
                   carry_ref, *, tiles_per_seq):
    i = pl.program_id(0)
    tm = x_ref.shape[0]
    h = _rms(x_ref[...]) * g_ref[...]
    proj = jnp.dot(h.astype(BF16), w_ref[...], preferred_element_type=F32)
    gm = gm_ref[...]

    def group_mean(v):
        return jnp.dot(v, gm, preferred_element_type=F32, precision=HIGHEST)

    misc = proj[:, SEG_MISC:SEG_MISC + LANES]
    misc_ref[...] = misc
    z = misc + fb_ref[...]
    log_f = jnp.minimum(z, 0.0) - jnp.log1p(jnp.exp(-jnp.abs(z)))

    @pl.when(i % tiles_per_seq == 0)
    def _():
        carry_ref[...] = jnp.zeros_like(carry_ref)

    cum = jnp.dot(tri_ref[...], log_f, preferred_element_type=F32, precision=HIGHEST) + carry_ref[...]
    carry_ref[...] = cum[tm - 1:tm, :]
    f_hi = (cum * LOG2E).astype(BF16).astype(F32)
    f_rem = cum * LOG2E - f_hi
    f_mid = f_rem.astype(BF16).astype(F32)
    f_lo = f_rem - f_mid

    q = proj[:, SEG_Q:SEG_Q + A_W]
    qn = q * lax.rsqrt(group_mean(q * q) + EPS) * aq_ref[...]
    k = proj[:, SEG_K:SEG_K + A_W]
    kn = k * lax.rsqrt(group_mean(k * k) + EPS) * ak_ref[...]
    lane = _lane_iota()
    for hd in range(A_HEADS):
        pair = slice((hd // 2) * LANES, (hd // 2 + 1) * LANES)
        slab = slice(hd * HEAD_SLAB, (hd + 1) * HEAD_SLAB)
        data = (lane < A_HEAD_DIM) if hd % 2 == 0 else (lane >= A_HEAD_DIM)
        e0 = A_HEAD_DIM if hd % 2 == 0 else 0
        fl = FORGET_LANE + 8 * (hd // 2) + hd % 2
        ones = jnp.where((lane >= e0) & (lane < e0 + 3), 1.0, 0.0)
        qa_ref[:, slab] = jnp.where(data, qn[:, pair], ones).astype(BF16)
        bias = jnp.where(lane == e0, -f_hi[:, fl:fl + 1],
                         jnp.where(lane == e0 + 1, -f_mid[:, fl:fl + 1],
                                   jnp.where(lane == e0 + 2, -f_lo[:, fl:fl + 1], 0.0)))
        ka_ref[:, slab] = jnp.where(data, kn[:, pair], bias).astype(BF16)
    _store_v_transposed(vt_ref, proj[:, SEG_V:SEG_V + A_W], A_HEADS)

    u = _gelu(proj[:, SEG_U:SEG_U + B_W])
    v = _gelu(proj[:, SEG_VB:SEG_VB + B_W])
    dv = v - group_mean(v)
    vn = dv * lax.rsqrt(group_mean(dv * dv) + EPS) * bvg_ref[...]
    group = lax.broadcasted_iota(jnp.int32, (1, B_W), 1) // B_GROUP_DIM
    for w in range(tm // B_WINDOW):
        rows = slice(w * B_WINDOW, (w + 1) * B_WINDOW)
        y_all = jnp.dot(ws_ref[...], vn[rows].astype(BF16), preferred_element_type=F32)
        y = bs_ref[...]
        for g in range(B_GROUPS):
            y = y + jnp.where(group == g, y_all[g * B_WINDOW:(g + 1) * B_WINDOW], 0.0)
        b = u[rows] * y
        bn_ref[rows, :] = (_rms(b) * onb_ref[...]).astype(BF16)

    cq_ref[...] = (_rms(proj[:, SEG_CQ:SEG_CQ + C_Q_RANK]) * cqg_ref[...]).astype(BF16)
    ckv_ref[...] = (_rms(proj[:, SEG_CKV:SEG_CKV + C_KV_RANK]) * ckvg_ref[...]).astype(BF16)


def _inproj(x2d, seq, p):
    t = x2d.shape[0]
    tm = ROW_TILE
    tps = seq // tm
    nb = t // seq
    row = lambda w: pl.BlockSpec((tm, w), lambda i: (i, 0))
    qk_w = A_HEADS * HEAD_SLAB
    out_shape = [
        jax.ShapeDtypeStruct((t, qk_w), BF16), jax.ShapeDtypeStruct((t, qk_w), BF16),
        jax.ShapeDtypeStruct((nb, A_HEADS, VT_ROWS, seq), BF16),
        jax.ShapeDtypeStruct((t, B_W), BF16),
        jax.ShapeDtypeStruct((t, C_Q_RANK), BF16), jax.ShapeDtypeStruct((t, C_KV_RANK), BF16),
        jax.ShapeDtypeStruct((t, LANES), F32),
    ]
    out_specs = [row(qk_w), row(qk_w),
                 pl.BlockSpec((1, A_HEADS, VT_ROWS, tm), lambda i: (i // tps, 0, 0, i % tps)),
                 row(B_W), row(C_Q_RANK), row(C_KV_RANK), row(LANES)]
    consts = [p["mix_g"], p["w_in"], p["aq"], p["ak"], p["fb"], p["bvg"], p["ws"], p["bs"], p["onb"],
              p["cqg"], p["ckvg"], p["gm"], p["tri"]]
    return pl.pallas_call(
        functools.partial(_inproj_kernel, tiles_per_seq=tps),
        grid=(t // tm,),
        in_specs=[row(D_MODEL)] + [_full(c.shape) for c in consts],
        out_specs=out_specs,
        out_shape=out_shape,
        scratch_shapes=[pltpu.VMEM((1, LANES), F32)],
        compiler_params=_cparams("arbitrary"),
        name="in_proj",
    )(x2d, *consts)


def _mla_prep_kernel(cq_ref, ckv_ref, misc_ref, cos_ref, sin_ref, wuq_ref, wuk_ref, wuv_ref,
                     gq_ref, gkn_ref, gkr_ref, qc_ref, kc_ref, vt_ref):
    lane = _lane_iota()
    nope = lane < C_NOPE_DIM
    rope = (lane >= ROPE_LANE) & (lane < ROPE_LANE + C_ROPE_DIM)
    cos, sin = cos_ref[...], sin_ref[...]
    q = jnp.dot(cq_ref[...], wuq_ref[...], preferred_element_type=F32)
    kn = jnp.dot(ckv_ref[...], wuk_ref[...], preferred_element_type=F32)
    _store_v_transposed(vt_ref, jnp.dot(ckv_ref[...], wuv_ref[...], preferred_element_type=F32), C_HEADS)

    kr = jnp.where(rope, misc_ref[...], 0.0)
    kr = kr * lax.rsqrt(jnp.sum(kr * kr, axis=-1, keepdims=True) * (1.0 / C_ROPE_DIM) + EPS) * gkr_ref[...]
    kr = _rotate(kr, cos, sin, lane)

    for hd in range(C_HEADS):
        cols = slice(hd * HEAD_SLAB, (hd + 1) * HEAD_SLAB)
        qh = q[:, cols]
        sq = qh * qh
        r_n = lax.rsqrt(jnp.sum(jnp.where(nope, sq, 0.0), axis=-1, keepdims=True) * (1.0 / C_NOPE_DIM) + EPS)
        r_r = lax.rsqrt(jnp.sum(jnp.where(rope, sq, 0.0), axis=-1, keepdims=True) * (1.0 / C_ROPE_DIM) + EPS)
        qn = qh * jnp.where(nope, r_n, r_r) * gq_ref[...]
        qc_ref[:, cols] = _rotate(qn, cos, sin, lane).astype(BF16)
        kh = kn[:, cols]
        r_k = lax.rsqrt(jnp.sum(kh * kh, axis=-1, keepdims=True) * (1.0 / C_NOPE_DIM) + EPS)
        kc_ref[:, cols] = (kh * r_k * gkn_ref[...] + kr).astype(BF16)


def _mla_prep(cq, ckv, misc, cos, sin, seq, p):
    t = cq.shape[0]
    tm = ROW_TILE
    tps = seq // tm
    row = lambda w: pl.BlockSpec((tm, w), lambda i: (i, 0))
    consts = [p["wuq"], p["wuk"], p["wuv"], p["gq"], p["gkn"], p["gkr"]]
    qk_w = C_HEADS * HEAD_SLAB
    return pl.pallas_call(
        _mla_prep_kernel,
        grid=(t // tm,),
        in_specs=[row(C_Q_RANK), row(C_KV_RANK), row(LANES), row(LANES), row(LANES)]
                 + [_full(c.shape) for c in consts],
        out_specs=[row(qk_w), row(qk_w),
                   pl.BlockSpec((1, C_HEADS, VT_ROWS, tm), lambda i: (i // tps, 0, 0, i % tps))],
        out_shape=[jax.ShapeDtypeStruct((t, qk_w), BF16), jax.ShapeDtypeStruct((t, qk_w), BF16),
                   jax.ShapeDtypeStruct((t // seq, C_HEADS, VT_ROWS, seq), BF16)],
        compiler_params=_cparams("parallel"),
        name="mla_prep",
    )(cq, ckv, misc, cos, sin, *consts)


def _attn_kernel(q_ref, k_ref, vt_ref, o_ref, s0_ref, s1_ref, mrun_ref, macc_ref, acc_ref, *, unit):
    i = pl.program_id(2)
    tq = q_ref.shape[0]
    tk = tq
    mrun_ref[...] = jnp.full(mrun_ref.shape, NEG_INF, F32)
    macc_ref[...] = jnp.full(macc_ref.shape, NEG_INF, F32)
    acc_ref[...] = jnp.zeros(acc_ref.shape, F32)

    def scores(j, s_ref, masked=False):
        k_start = pl.multiple_of(j * tk, tk)
        if masked:
            kpos = lax.broadcasted_iota(jnp.int32, (tk, tq), 0) // unit
            qpos = lax.broadcasted_iota(jnp.int32, (tk, tq), 1) // unit
            allowed = kpos <= qpos
        for hh in range(2):
            cols = slice(hh * HEAD_SLAB, (hh + 1) * HEAD_SLAB)
            s_t = lax.dot_general(k_ref[pl.ds(k_start, tk), cols], q_ref[:, cols],
                                  (((1,), (1,)), ((), ())), preferred_element_type=F32)
            if masked:
                s_t = jnp.where(allowed, s_t, NEG_INF)
            s_ref[hh] = s_t
            mrun_ref[hh] = jnp.maximum(mrun_ref[hh], jnp.max(s_t, axis=0, keepdims=True))

    def accumulate(j, s_ref):
        k_start = pl.multiple_of(j * tk, tk)
        for hh in range(2):
            m = mrun_ref[hh]
            p_t = jnp.exp2(s_ref[hh] - m).astype(BF16)
            pv = jnp.dot(vt_ref[0, hh, :, pl.ds(k_start, tk)], p_t, preferred_element_type=F32)
            acc_ref[hh] = jnp.exp2(macc_ref[hh] - m) * acc_ref[hh] + pv
            macc_ref[hh] = m

    scores(i, s1_ref, masked=True)
    n_pairs = i // 2

    def body(t, carry):
        j0 = 2 * t
        accumulate(jnp.where(t == 0, i, j0 - 1), s1_ref)
        scores(j0, s0_ref)
        accumulate(j0, s0_ref)
        scores(j0 + 1, s1_ref)
        return carry

    lax.fori_loop(0, n_pairs, body, 0)
    in_s1 = jnp.where(n_pairs == 0, i, 2 * n_pairs - 1)

    @pl.when(i % 2 == 1)
    def _():
        accumulate(in_s1, s1_ref)
        scores(i - 1, s0_ref)
        accumulate(i - 1, s0_ref)

    @pl.when(i % 2 == 0)
    def _():
        accumulate(in_s1, s1_ref)

    halves = [acc_ref[hh, 0:C_V_DIM, :] / acc_ref[hh, C_V_DIM:C_V_DIM + 1, :] for hh in range(2)]
    o_ref[...] = jnp.concatenate(halves, axis=0).T.astype(o_ref.dtype)


def _attention(q, k, vt, *, unit, name):
    t = q.shape[0]
    nb, n_heads, _, seq = vt.shape
    tq = ATTN_TILE
    nq = seq // tq
    return pl.pallas_call(
        functools.partial(_attn_kernel, unit=unit),
        grid=(nb, n_heads // 2, nq),
        in_specs=[pl.BlockSpec((tq, 2 * HEAD_SLAB), lambda b, p, i: (b * nq + i, p)),
                  pl.BlockSpec((seq, 2 * HEAD_SLAB), lambda b, p, i: (b, p)),
                  pl.BlockSpec((1, 2, VT_ROWS, seq), lambda b, p, i: (b, p, 0, 0))],
        out_specs=pl.BlockSpec((tq, 2 * C_V_DIM), lambda b, p, i: (b * nq + i, p)),
        out_shape=jax.ShapeDtypeStruct((t, n_heads * C_V_DIM), BF16),
        scratch_shapes=[pltpu.VMEM((2, tq, tq), F32), pltpu.VMEM((2, tq, tq), F32),
                        pltpu.VMEM((2, 1, tq), F32), pltpu.VMEM((2, 1, tq), F32),
                        pltpu.VMEM((2, VT_ROWS, tq), F32)],
        compiler_params=_cparams("parallel", "parallel", "arbitrary"),
        name=name,
    )(q, k, vt)


def _mem_kv_kernel(mem_ref, g_ref, w_ref, kg_ref, k_ref, v_ref):
    mn = (_rms(mem_ref[0]) * g_ref[...]).astype(BF16)
    kv = jnp.dot(mn, w_ref[...], preferred_element_type=F32)
    for hd in range(M_HEADS):
        cols = slice(hd * M_HEAD_DIM, (hd + 1) * M_HEAD_DIM)
        k_ref[0, :, cols] = (_rms(kv[:, cols]) * kg_ref[...]).astype(BF16)
    v_ref[0] = kv[:, M_W:].astype(BF16)


def _mem_kv(mem, p):
    nb, ml, _ = mem.shape
    consts = [p["mem_g"], p["w_mem_kv"], p["mkg"]]
    blk = pl.BlockSpec((1, ml, M_W), lambda b: (b, 0, 0))
    return pl.pallas_call(
        _mem_kv_kernel,
        grid=(nb,),
        in_specs=[pl.BlockSpec((1, ml, D_MODEL), lambda b: (b, 0, 0))] + [_full(c.shape) for c in consts],
        out_specs=[blk, blk],
        out_shape=[jax.ShapeDtypeStruct((nb, ml, M_W), BF16)] * 2,
        compiler_params=_cparams("parallel"),
        name="mem_kv",
    )(mem, *consts)


def _outproj_kernel(x_ref, a_ref, bn_ref, c_ref, ona_ref, onc_ref, wo_ref, xg_ref, wq_ref, mqg_ref,
                    km_ref, vm_ref, wmo_ref, o_ref):
    a_n = (_rms(a_ref[...].astype(F32)) * ona_ref[...]).astype(BF16)
    c_n = (_rms(c_ref[...].astype(F32)) * onc_ref[...]).astype(BF16)
    mix = jnp.concatenate([a_n, bn_ref[...], c_n], axis=-1)
    x1 = x_ref[...] + jnp.dot(mix, wo_ref[...], preferred_element_type=F32)

    h = (_rms(x1) * xg_ref[...]).astype(BF16)
    q = jnp.dot(h, wq_ref[...], preferred_element_type=F32)
    outs = []
    for hd in range(M_HEADS):
        cols = slice(hd * M_HEAD_DIM, (hd + 1) * M_HEAD_DIM)
        qh = (_rms(q[:, cols]) * mqg_ref[...]).astype(BF16)
        s = lax.dot_general(qh, km_ref[0, :, cols], (((1,), (1,)), ((), ())), preferred_element_type=F32)
        e = jnp.exp(s - jnp.max(s, axis=-1, keepdims=True))
        pr = e / jnp.sum(e, axis=-1, keepdims=True)
        outs.append(jnp.dot(pr.astype(BF16), vm_ref[0, :, cols], preferred_element_type=F32).astype(BF16))
    o_ref[...] = x1 + jnp.dot(jnp.concatenate(outs, axis=-1), wmo_ref[...], preferred_element_type=F32)


def _outproj(x2d, a, bn, c, km, vm, seq, p):
    t = x2d.shape[0]
    tm = ROW_TILE
    tps = seq // tm
    ml = km.shape[1]
    row = lambda w: pl.BlockSpec((tm, w), lambda i: (i, 0))
    memblk = pl.BlockSpec((1, ml, M_W), lambda i: (i // tps, 0, 0))
    c1 = [p["ona"], p["onc"], p["w_out"], p["xg"], p["w_mem_q"], p["mqg"]]
    return pl.pallas_call(
        _outproj_kernel,
        grid=(t // tm,),
        in_specs=[row(D_MODEL), row(A_W), row(B_W), row(C_W)] + [_full(c_.shape) for c_ in c1]
                 + [memblk, memblk, _full(p["w_mem_out"].shape)],
        out_specs=row(D_MODEL),
        out_shape=jax.ShapeDtypeStruct((t, D_MODEL), F32),
        compiler_params=_cparams("parallel"),
        name="out_proj_mem_attn",
    )(x2d, a, bn, c, *c1, km, vm, p["w_mem_out"])


def _silu(x):
    return x * jax.nn.sigmoid(x)


def _ffn_kernel(x_ref, g_ref, wg_ref, wu_ref, wd_ref, o_ref, *, n_chunks):
    x = x_ref[...]
    h = (_rms(x) * g_ref[...]).astype(BF16)
    fc = wg_ref.shape[1] // n_chunks
    acc = x
    for c in range(n_chunks):
        cols = slice(c * fc, (c + 1) * fc)
        act = _silu(jnp.dot(h, wg_ref[:, cols], preferred_element_type=F32)) * \
            jnp.dot(h, wu_ref[:, cols], preferred_element_type=F32)
        acc = acc + jnp.dot(act.astype(BF16), wd_ref[cols, :], preferred_element_type=F32)
    o_ref[...] = acc


def _ffn(x2d, g, wg, wu, wd):
    t = x2d.shape[0]
    tm = ROW_TILE
    row = pl.BlockSpec((tm, D_MODEL), lambda i: (i, 0))
    resident = lambda a: pl.BlockSpec(a.shape, lambda i: (0, 0), pipeline_mode=pl.Buffered(1))
    return pl.pallas_call(
        functools.partial(_ffn_kernel, n_chunks=2),
        grid=(t // tm,),
        in_specs=[row, _full(g.shape), resident(wg), resident(wu), resident(wd)],
        out_specs=row,
        out_shape=jax.ShapeDtypeStruct((t, D_MODEL), F32),
        compiler_params=_cparams("parallel"),
        name="ffn_dense",
    )(x2d, g, wg, wu, wd)


def _moe_kernel(x_ref, g_ref, wr_ref, br_ref, wg_ref, wu_ref, wd_ref, o_ref, h_ref, comb_ref):
    e = pl.program_id(1)
    lane = _lane_iota()

    @pl.when(e == 0)
    def _():
        x = x_ref[...]
        h = _rms(x) * g_ref[...]
        h_ref[...] = h.astype(BF16)
        logits = jnp.dot(h, wr_ref[...], preferred_element_type=F32, precision=HIGHEST) + br_ref[...]
        logits = jnp.where(lane < N_EXPERTS, logits, -jnp.inf)
        v1 = jnp.max(logits, axis=-1, keepdims=True)
        i1 = jnp.min(jnp.where(logits == v1, lane, LANES), axis=-1, keepdims=True)
        rest = jnp.where(lane == i1, -jnp.inf, logits)
        v2 = jnp.max(rest, axis=-1, keepdims=True)
        i2 = jnp.min(jnp.where(rest == v2, lane, LANES), axis=-1, keepdims=True)
        e2 = jnp.exp(v2 - v1)
        g1 = 1.0 / (1.0 + e2)
        comb_ref[...] = jnp.where(lane == i1, g1, 0.0) + jnp.where(lane == i2, e2 * g1, 0.0)
        o_ref[...] = x

    h = h_ref[...]
    act = _silu(jnp.dot(h, wg_ref[0], preferred_element_type=F32)) * \
        jnp.dot(h, wu_ref[0], preferred_element_type=F32)
    y = jnp.dot(act.astype(BF16), wd_ref[0], preferred_element_type=F32)
    w_e = jnp.sum(jnp.where(lane == e, comb_ref[...], 0.0), axis=-1, keepdims=True)
    o_ref[...] += w_e * y


def _moe(x2d, g, wr, br, wg, wu, wd):
    t = x2d.shape[0]
    tm = ROW_TILE
    ff = wg.shape[2]
    row = pl.BlockSpec((tm, D_MODEL), lambda i, e: (i, 0))
    return pl.pallas_call(
        _moe_kernel,
        grid=(t // tm, N_EXPERTS),
        in_specs=[row, _full(g.shape), _full(wr.shape), _full(br.shape),
                  pl.BlockSpec((1, D_MODEL, ff), lambda i, e: (e, 0, 0)),
                  pl.BlockSpec((1, D_MODEL, ff), lambda i, e: (e, 0, 0)),
                  pl.BlockSpec((1, ff, D_MODEL), lambda i, e: (e, 0, 0))],
        out_specs=row,
        out_shape=jax.ShapeDtypeStruct((t, D_MODEL), F32),
        scratch_shapes=[pltpu.VMEM((tm, D_MODEL), BF16), pltpu.VMEM((tm, LANES), F32)],
        compiler_params=_cparams("parallel", "arbitrary"),
        name="moe_dense",
    )(x2d, g, wr, br, wg, wu, wd)


def _pad_cols(w, width):
    return jnp.pad(w, ((0, 0), (0, width - w.shape[1])))


def _layer_params(l, mix_norm, w_in, b_forget, a_q_norm, a_k_norm, b_v_norm, b_spatial_w, b_spatial_b,
                  c_q_lat_norm, c_w_uq, c_kv_lat_norm, c_w_ukv, c_q_nope_norm, c_q_rope_norm,
                  c_k_nope_norm, c_k_rope_norm, out_norm_a, out_norm_b, out_norm_c, w_out,
                  xattn_norm, mem_norm, w_mem_q, w_mem_kv, m_q_norm, m_k_norm, w_mem_out):
    p = {}
    o = np.cumsum((0, A_W, A_W, A_W, A_HEADS, B_W, B_W, C_Q_RANK, C_KV_RANK, C_ROPE_DIM))
    w = w_in[l]
    seg = lambda n: w[:, o[n]:o[n + 1]]
    fa = seg(3)
    misc = jnp.zeros((D_MODEL, LANES), F32)
    misc = misc.at[:, ROPE_LANE:ROPE_LANE + C_ROPE_DIM].set(seg(8))
    fb = jnp.zeros((1, LANES), F32)
    for hd in range(A_HEADS):
        ln = FORGET_LANE + 8 * (hd // 2) + hd % 2
        misc = misc.at[:, ln].set(fa[:, hd])
        fb = fb.at[0, ln].set(b_forget[l, hd])
    p["w_in"] = jnp.concatenate([seg(0), seg(1), seg(2), seg(4), seg(5), seg(6), seg(7), misc], axis=1).astype(BF16)
    p["fb"] = fb
    p["mix_g"] = mix_norm[l][None]
    p["aq"] = jnp.tile(a_q_norm[l], A_HEADS)[None] * (A_HEAD_DIM ** -0.5 * LOG2E)
    p["ak"] = jnp.tile(a_k_norm[l], A_HEADS)[None]
    p["bvg"] = b_v_norm[l][None]
    pos = np.arange(B_WINDOW)
    mask = (pos[None, :] // CHUNK) <= (pos[:, None] // CHUNK)
    p["ws"] = jnp.where(mask[None], b_spatial_w[l], 0.0).reshape(B_GROUPS * B_WINDOW, B_WINDOW).astype(BF16)
    p["bs"] = jnp.repeat(b_spatial_b[l].T, B_GROUP_DIM, axis=1)
    p["onb"] = out_norm_b[l][None]
    p["cqg"] = c_q_lat_norm[l][None]
    p["ckvg"] = c_kv_lat_norm[l][None]
    gidx = np.arange(A_W) // A_HEAD_DIM
    p["gm"] = jnp.asarray((gidx[:, None] == gidx[None, :]).astype(np.float32) / A_HEAD_DIM)
    p["tri"] = jnp.asarray(np.tril(np.ones((ROW_TILE, ROW_TILE), np.float32)))

    qd = C_NOPE_DIM + C_ROPE_DIM
    p["wuq"] = jnp.pad(c_w_uq[l], ((0, 0), (0, 0), (0, HEAD_SLAB - qd))).reshape(C_Q_RANK, -1).astype(BF16)
    wukv = c_w_ukv[l]
    p["wuk"] = jnp.pad(wukv[:, :, :C_NOPE_DIM], ((0, 0), (0, 0), (0, HEAD_SLAB - C_NOPE_DIM))
                       ).reshape(C_KV_RANK, -1).astype(BF16)
    p["wuv"] = wukv[:, :, C_NOPE_DIM:].reshape(C_KV_RANK, C_W).astype(BF16)
    gq = jnp.concatenate([c_q_nope_norm[l], c_q_rope_norm[l]]) * (qd ** -0.5 * LOG2E)
    p["gq"] = _pad_cols(gq[None], LANES)
    p["gkn"] = _pad_cols(c_k_nope_norm[l][None], LANES)
    p["gkr"] = jnp.zeros((1, LANES), F32).at[0, ROPE_LANE:ROPE_LANE + C_ROPE_DIM].set(c_k_rope_norm[l])

    p["ona"] = out_norm_a[l][None]
    p["onc"] = out_norm_c[l][None]
    p["w_out"] = w_out[l].astype(BF16)
    p["xg"] = xattn_norm[l][None]
    p["w_mem_q"] = w_mem_q[l].astype(BF16)
    p["mqg"] = m_q_norm[l][None] * (M_HEAD_DIM ** -0.5)
    p["mem_g"] = mem_norm[l][None]
    p["w_mem_kv"] = w_mem_kv[l].astype(BF16)
    p["mkg"] = m_k_norm[l][None]
    p["w_mem_out"] = w_mem_out[l].astype(BF16)
    return p


def kernel(x, mem, positions, mix_norm, w_in, b_forget, a_q_norm, a_k_norm, b_v_norm, b_spatial_w, b_spatial_b, c_q_lat_norm, c_w_uq, c_kv_lat_norm, c_w_ukv, c_q_nope_norm, c_q_rope_norm, c_k_nope_norm, c_k_rope_norm, out_norm_a, out_norm_b, out_norm_c, w_out, xattn_norm, mem_norm, w_mem_q, w_mem_kv, m_q_norm, m_k_norm, w_mem_out, ffn_norm, ffn_w_gate, ffn_w_up, ffn_w_down, w_router, b_router, moe_w_gate, moe_w_up, moe_w_down):
    nb, seq, d = x.shape
    assert d == D_MODEL and seq % ROW_TILE == 0 and seq % ATTN_TILE == 0
    depth = w_in.shape[0]
    t = nb * seq
    x2d = x.reshape(t, d)
    cos, sin = _rope_tables(positions.reshape(t, 1).astype(F32))

    for l in range(depth):
        p = _layer_params(l, mix_norm, w_in, b_forget, a_q_norm, a_k_norm, b_v_norm, b_spatial_w,
                          b_spatial_b, c_q_lat_norm, c_w_uq, c_kv_lat_norm, c_w_ukv, c_q_nope_norm,
                          c_q_rope_norm, c_k_nope_norm, c_k_rope_norm, out_norm_a, out_norm_b,
                          out_norm_c, w_out, xattn_norm, mem_norm, w_mem_q, w_mem_kv, m_q_norm,
                          m_k_norm, w_mem_out)
        qa, ka, vta, bn, cq, ckv, misc = _inproj(x2d, seq, p)
        qc, kc, vtc = _mla_prep(cq, ckv, misc, cos, sin, seq, p)
        a = _attention(qa, ka, vta, unit=1, name="attn_fox")
        c = _attention(qc, kc, vtc, unit=CHUNK, name="attn_mla")
        km, vm = _mem_kv(mem, p)
        x2d = _outproj(x2d, a, bn, c, km, vm, seq, p)
        g = ffn_norm[l][None]
        if l % 2 == 0:
            m = l // 2
            ff = ffn_w_gate.shape[2]
            ff_pad = -(-ff // (2 * LANES)) * (2 * LANES)
            wg = _pad_cols(ffn_w_gate[m], ff_pad).astype(BF16)
            wu = _pad_cols(ffn_w_up[m], ff_pad).astype(BF16)
            wd = jnp.pad(ffn_w_down[m], ((0, ff_pad - ff), (0, 0))).astype(BF16)
            x2d = _ffn(x2d, g, wg, wu, wd)
        else:
            m = l // 2
            wr = _pad_cols(w_router[m], LANES)
            br = _pad_cols(b_router[m][None], LANES)
            x2d = _moe(x2d, g, wr, br, moe_w_gate[m].astype(BF16), moe_w_up[m].astype(BF16),
                       moe_w_down[m].astype(BF16))
    return x2d.reshape(nb, seq, d)
```

```python
import functools

import numpy as np
import jax
import jax.numpy as jnp
from jax import lax
from jax.experimental import pallas as pl
from jax.experimental.pallas import tpu as pltpu

F32 = jnp.float32
BF16 = jnp.bfloat16
HIGHEST = lax.Precision.HIGHEST

D_MODEL = 1024
CHUNK = 64
EPS = 1e-6
NEG_INF = -1e30
A_HEADS, A_HEAD_DIM = 4, 64
B_GROUPS, B_GROUP_DIM, B_WINDOW = 4, 64, 128
C_HEADS, C_NOPE_DIM, C_ROPE_DIM, C_V_DIM = 8, 64, 32, 64
C_Q_RANK, C_KV_RANK = 256, 128
ROPE_THETA = 10000.0
M_HEADS, M_HEAD_DIM = 4, 128
N_EXPERTS = 8
A_W = A_HEADS * A_HEAD_DIM
B_W = B_GROUPS * B_GROUP_DIM
C_W = C_HEADS * C_V_DIM
M_W = M_HEADS * M_HEAD_DIM

LANES = 128

SEG_Q, SEG_K, SEG_V, SEG_U, SEG_VB, SEG_CQ, SEG_CKV, SEG_MISC = 0, 256, 512, 768, 1024, 1280, 1536, 1664
IN_PAD_W = SEG_MISC + LANES
ROPE_LANE = C_NOPE_DIM
FORGET_LANE = 96
HEAD_SLAB = LANES
VT_ROWS = 80
LOG2E = float(np.log2(np.e))

ROW_TILE = 512
ATTN_TILE = 512
MOE_CHUNK = 1024
MOE_CAP = 320
VMEM_LIMIT = 56 * 1024 * 1024


def _cparams(*sem):
    return pltpu.CompilerParams(dimension_semantics=sem, vmem_limit_bytes=VMEM_LIMIT)


def _full(shape):
    n = len(shape)
    return pl.BlockSpec(shape, lambda *_: (0,) * n)


def _rms(x):
    return x * lax.rsqrt(jnp.mean(x * x, axis=-1, keepdims=True) + EPS)


def _lane_iota(n=LANES):
    return lax.broadcasted_iota(jnp.int32, (1, n), 1)


def _rope_table_kernel(pos_ref, inv_ref, sgn_ref, cos_ref, sin_ref):
    ang = pos_ref[...] * inv_ref[...]
    cos_ref[...] = jnp.cos(ang)
    sin_ref[...] = jnp.sin(ang) * sgn_ref[...]


def _rope_tables(pos_col):
    t = pos_col.shape[0]
    half = C_ROPE_DIM // 2
    inv = ROPE_THETA ** (-jnp.arange(half, dtype=F32) / half)
    inv_l = jnp.zeros((1, LANES), F32).at[0, ROPE_LANE:ROPE_LANE + C_ROPE_DIM].set(jnp.tile(inv, 2))
    sgn = np.zeros((1, LANES), np.float32)
    sgn[0, ROPE_LANE:ROPE_LANE + half] = -1.0
    sgn[0, ROPE_LANE + half:ROPE_LANE + C_ROPE_DIM] = 1.0
    tm = ROW_TILE
    return pl.pallas_call(
        _rope_table_kernel,
        grid=(t // tm,),
        in_specs=[pl.BlockSpec((tm, 1), lambda i: (i, 0)), _full((1, LANES)), _full((1, LANES))],
        out_specs=[pl.BlockSpec((tm, LANES), lambda i: (i, 0))] * 2,
        out_shape=[jax.ShapeDtypeStruct((t, LANES), F32)] * 2,
        compiler_params=_cparams("parallel"),
        name="rope_tables",
    )(pos_col, inv_l, jnp.asarray(sgn))


def _rotate(x, cos, sin_signed, lane):
    half = C_ROPE_DIM // 2
    partner = jnp.where(lane < ROPE_LANE + half,
                        pltpu.roll(x, LANES - half, 1), pltpu.roll(x, half, 1))
    return x * cos + partner * sin_signed


def _store_v_transposed(vt_ref, v, n_heads):
    tm = v.shape[0]
    v_t = v.T
    tail = jnp.where(lax.broadcasted_iota(jnp.int32, (VT_ROWS - C_V_DIM, tm), 0) == 0, 1.0, 0.0).astype(BF16)
    for hd in range(n_heads):
        vt_ref[0, hd, 0:C_V_DIM, :] = v_t[hd * C_V_DIM:(hd + 1) * C_V_DIM, :].astype(BF16)
        vt_ref[0, hd, C_V_DIM:VT_ROWS, :] = tail


def _gelu(x):
    return 0.5 * x * (1.0 + lax.erf(x * np.float32(1.0 / np.sqrt(2.0))))


def _inproj_kernel(x_ref, g_ref, w_ref, aq_ref, ak_ref, fb_ref, bvg_ref, ws_ref, bs_ref, onb_ref,
                   cqg_ref, ckvg_ref, gm_ref, tri_ref,
                   qa_ref, ka_ref, vt_ref, bn_ref, cq_ref, ckv_ref, misc_ref,
                   carry_ref, *, tiles_per_seq):
    i = pl.program_id(0)
    tm = x_ref.shape[0]
    h = _rms(x_ref[...]) * g_ref[...]
    proj = jnp.dot(h.astype(BF16), w_ref[...], preferred_element_type=F32)
    gm = gm_ref[...]

    def group_mean(v):
        return jnp.dot(v, gm, preferred_element_type=F32, precision=HIGHEST)

    misc = proj[:, SEG_MISC:SEG_MISC + LANES]
    misc_ref[...] = misc
    z = misc + fb_ref[...]
    log_f = jnp.minimum(z, 0.0) - jnp.log1p(jnp.exp(-jnp.abs(z)))

    @pl.when(i % tiles_per_seq == 0)
    def _():
        carry_ref[...] = jnp.zeros_like(carry_ref)

    cum = jnp.dot(tri_ref[...], log_f, preferred_element_type=F32, precision=HIGHEST) + carry_ref[...]
    carry_ref[...] = cum[tm - 1:tm, :]
    f_hi = (cum * LOG2E).astype(BF16).astype(F32)
    f_rem = cum * LOG2E - f_hi
    f_mid = f_rem.astype(BF16).astype(F32)
    f_lo = f_rem - f_mid

    q = proj[:, SEG_Q:SEG_Q + A_W]
    qn = q * lax.rsqrt(group_mean(q * q) + EPS) * aq_ref[...]
    k = proj[:, SEG_K:SEG_K + A_W]
    kn = k * lax.rsqrt(group_mean(k * k) + EPS) * ak_ref[...]
    lane = _lane_iota()
    for hd in range(A_HEADS):
        pair = slice((hd // 2) * LANES, (hd // 2 + 1) * LANES)
        slab = slice(hd * HEAD_SLAB, (hd + 1) * HEAD_SLAB)
        data = (lane < A_HEAD_DIM) if hd % 2 == 0 else (lane >= A_HEAD_DIM)
        e0 = A_HEAD_DIM if hd % 2 == 0 else 0
        fl = FORGET_LANE + 8 * (hd // 2) + hd % 2
        ones = jnp.where((lane >= e0) & (lane < e0 + 3), 1.0, 0.0)
        qa_ref[:, slab] = jnp.where(data, qn[:, pair], ones).astype(BF16)
        bias = jnp.where(lane == e0, -f_hi[:, fl:fl + 1],
                         jnp.where(lane == e0 + 1, -f_mid[:, fl:fl + 1],
                                   jnp.where(lane == e0 + 2, -f_lo[:, fl:fl + 1], 0.0)))
        ka_ref[:, slab] = jnp.where(data, kn[:, pair], bias).astype(BF16)
    _store_v_transposed(vt_ref, proj[:, SEG_V:SEG_V + A_W], A_HEADS)

    u = _gelu(proj[:, SEG_U:SEG_U + B_W])
    v = _gelu(proj[:, SEG_VB:SEG_VB + B_W])
    dv = v - group_mean(v)
    vn = dv * lax.rsqrt(group_mean(dv * dv) + EPS) * bvg_ref[...]
    group = lax.broadcasted_iota(jnp.int32, (1, B_W), 1) // B_GROUP_DIM
    for w in range(tm // B_WINDOW):
        rows = slice(w * B_WINDOW, (w + 1) * B_WINDOW)
        y_all = jnp.dot(ws_ref[...], vn[rows].astype(BF16), preferred_element_type=F32)
        y = bs_ref[...]
        for g in range(B_GROUPS):
            y = y + jnp.where(group == g, y_all[g * B_WINDOW:(g + 1) * B_WINDOW], 0.0)
        b = u[rows] * y
        bn_ref[rows, :] = (_rms(b) * onb_ref[...]).astype(BF16)

    cq_ref[...] = (_rms(proj[:, SEG_CQ:SEG_CQ + C_Q_RANK]) * cqg_ref[...]).astype(BF16)
    ckv_ref[...] = (_rms(proj[:, SEG_CKV:SEG_CKV + C_KV_RANK]) * ckvg_ref[...]).astype(BF16)


def _inproj(x2d, seq, p):
    t = x2d.shape[0]
    tm = ROW_TILE
    tps = seq // tm
    nb = t // seq
    row = lambda w: pl.BlockSpec((tm, w), lambda i: (i, 0))
    qk_w = A_HEADS * HEAD_SLAB
    out_shape = [
        jax.ShapeDtypeStruct((t, qk_w), BF16), jax.ShapeDtypeStruct((t, qk_w), BF16),
        jax.ShapeDtypeStruct((nb, A_HEADS, VT_ROWS, seq), BF16),
        jax.ShapeDtypeStruct((t, B_W), BF16),
        jax.ShapeDtypeStruct((t, C_Q_RANK), BF16), jax.ShapeDtypeStruct((t, C_KV_RANK), BF16),
        jax.ShapeDtypeStruct((t, LANES), F32),
    ]
    out_specs = [row(qk_w), row(qk_w),
                 pl.BlockSpec((1, A_HEADS, VT_ROWS, tm), lambda i: (i // tps, 0, 0, i % tps)),
                 row(B_W), row(C_Q_RANK), row(C_KV_RANK), row(LANES)]
    consts = [p["mix_g"], p["w_in"], p["aq"], p["ak"], p["fb"], p["bvg"], p["ws"], p["bs"], p["onb"],
              p["cqg"], p["ckvg"], p["gm"], p["tri"]]
    return pl.pallas_call(
        functools.partial(_inproj_kernel, tiles_per_seq=tps),
        grid=(t // tm,),
        in_specs=[row(D_MODEL)] + [_full(c.shape) for c in consts],
        out_specs=out_specs,
        out_shape=out_shape,
        scratch_shapes=[pltpu.VMEM((1, LANES), F32)],
        compiler_params=_cparams("arbitrary"),
        name="in_proj",
    )(x2d, *consts)


def _mla_prep_kernel(cq_ref, ckv_ref, misc_ref, cos_ref, sin_ref, wuq_ref, wuk_ref, wuv_ref,
                     gq_ref, gkn_ref, gkr_ref, qc_ref, kc_ref, vt_ref):
    lane = _lane_iota()
    nope = lane < C_NOPE_DIM
    rope = (lane >= ROPE_LANE) & (lane < ROPE_LANE + C_ROPE_DIM)
    cos, sin = cos_ref[...], sin_ref[...]
    q = jnp.dot(cq_ref[...], wuq_ref[...], preferred_element_type=F32)
    kn = jnp.dot(ckv_ref[...], wuk_ref[...], preferred_element_type=F32)
    _store_v_transposed(vt_ref, jnp.dot(ckv_ref[...], wuv_ref[...], preferred_element_type=F32), C_HEADS)

    kr = jnp.where(rope, misc_ref[...], 0.0)
    kr = kr * lax.rsqrt(jnp.sum(kr * kr, axis=-1, keepdims=True) * (1.0 / C_ROPE_DIM) + EPS) * gkr_ref[...]
    kr = _rotate(kr, cos, sin, lane)

    for hd in range(C_HEADS):
        cols = slice(hd * HEAD_SLAB, (hd + 1) * HEAD_SLAB)
        qh = q[:, cols]
        sq = qh * qh
        r_n = lax.rsqrt(jnp.sum(jnp.where(nope, sq, 0.0), axis=-1, keepdims=True) * (1.0 / C_NOPE_DIM) + EPS)
        r_r = lax.rsqrt(jnp.sum(jnp.where(rope, sq, 0.0), axis=-1, keepdims=True) * (1.0 / C_ROPE_DIM) + EPS)
        qn = qh * jnp.where(nope, r_n, r_r) * gq_ref[...]
        qc_ref[:, cols] = _rotate(qn, cos, sin, lane).astype(BF16)
        kh = kn[:, cols]
        r_k = lax.rsqrt(jnp.sum(kh * kh, axis=-1, keepdims=True) * (1.0 / C_NOPE_DIM) + EPS)
        kc_ref[:, cols] = (kh * r_k * gkn_ref[...] + kr).astype(BF16)


def _mla_prep(cq, ckv, misc, cos, sin, seq, p):
    t = cq.shape[0]
    tm = ROW_TILE
    tps = seq // tm
    row = lambda w: pl.BlockSpec((tm, w), lambda i: (i, 0))
    consts = [p["wuq"], p["wuk"], p["wuv"], p["gq"], p["gkn"], p["gkr"]]
    qk_w = C_HEADS * HEAD_SLAB
    return pl.pallas_call(
        _mla_prep_kernel,
        grid=(t // tm,),
        in_specs=[row(C_Q_RANK), row(C_KV_RANK), row(LANES), row(LANES), row(LANES)]
                 + [_full(c.shape) for c in consts],
        out_specs=[row(qk_w), row(qk_w),
                   pl.BlockSpec((1, C_HEADS, VT_ROWS, tm), lambda i: (i // tps, 0, 0, i % tps))],
        out_shape=[jax.ShapeDtypeStruct((t, qk_w), BF16), jax.ShapeDtypeStruct((t, qk_w), BF16),
                   jax.ShapeDtypeStruct((t // seq, C_HEADS, VT_ROWS, seq), BF16)],
        compiler_params=_cparams("parallel"),
        name="mla_prep",
    )(cq, ckv, misc, cos, sin, *consts)


def _attn_kernel(q_ref, k_ref, vt_ref, o_ref, s0_ref, s1_ref, mrun_ref, macc_ref, acc_ref, *, unit):
    i = pl.program_id(2)
    tq = q_ref.shape[0]
    tk = tq
    mrun_ref[...] = jnp.full(mrun_ref.shape, NEG_INF, F32)
    macc_ref[...] = jnp.full(macc_ref.shape, NEG_INF, F32)
    acc_ref[...] = jnp.zeros(acc_ref.shape, F32)

    def scores(j, s_ref, masked=False):
        k_start = pl.multiple_of(j * tk, tk)
        if masked:
            kpos = lax.broadcasted_iota(jnp.int32, (tk, tq), 0) // unit
            qpos = lax.broadcasted_iota(jnp.int32, (tk, tq), 1) // unit
            allowed = kpos <= qpos
        for hh in range(2):
            cols = slice(hh * HEAD_SLAB, (hh + 1) * HEAD_SLAB)
            s_t = lax.dot_general(k_ref[pl.ds(k_start, tk), cols], q_ref[:, cols],
                                  (((1,), (1,)), ((), ())), preferred_element_type=F32)
            if masked:
                s_t = jnp.where(allowed, s_t, NEG_INF)
            s_ref[hh] = s_t
            mrun_ref[hh] = jnp.maximum(mrun_ref[hh], jnp.max(s_t, axis=0, keepdims=True))

    def accumulate(j, s_ref):
        k_start = pl.multiple_of(j * tk, tk)
        for hh in range(2):
            m = mrun_ref[hh]
            p_t = jnp.exp2(s_ref[hh] - m).astype(BF16)
            pv = jnp.dot(vt_ref[0, hh, :, pl.ds(k_start, tk)], p_t, preferred_element_type=F32)
            acc_ref[hh] = jnp.exp2(macc_ref[hh] - m) * acc_ref[hh] + pv
            macc_ref[hh] = m

    scores(i, s1_ref, masked=True)
    n_pairs = i // 2

    def body(t, carry):
        j0 = 2 * t
        accumulate(jnp.where(t == 0, i, j0 - 1), s1_ref)
        scores(j0, s0_ref)
        accumulate(j0, s0_ref)
        scores(j0 + 1, s1_ref)
        return carry

    lax.fori_loop(0, n_pairs, body, 0)
    in_s1 = jnp.where(n_pairs == 0, i, 2 * n_pairs - 1)

    @pl.when(i % 2 == 1)
    def _():
        accumulate(in_s1, s1_ref)
        scores(i - 1, s0_ref)
        accumulate(i - 1, s0_ref)

    @pl.when(i % 2 == 0)
    def _():
        accumulate(in_s1, s1_ref)

    halves = [acc_ref[hh, 0:C_V_DIM, :] / acc_ref[hh, C_V_DIM:C_V_DIM + 1, :] for hh in range(2)]
    o_ref[...] = jnp.concatenate(halves, axis=0).T.astype(o_ref.dtype)


def _attention(q, k, vt, *, unit, name):
    t = q.shape[0]
    nb, n_heads, _, seq = vt.shape
    tq = ATTN_TILE
    nq = seq // tq
    return pl.pallas_call(
        functools.partial(_attn_kernel, unit=unit),
        grid=(nb, n_heads // 2, nq),
        in_specs=[pl.BlockSpec((tq, 2 * HEAD_SLAB), lambda b, p, i: (b * nq + i, p)),
                  pl.BlockSpec((seq, 2 * HEAD_SLAB), lambda b, p, i: (b, p)),
                  pl.BlockSpec((1, 2, VT_ROWS, seq), lambda b, p, i: (b, p, 0, 0))],
        out_specs=pl.BlockSpec((tq, 2 * C_V_DIM), lambda b, p, i: (b * nq + i, p)),
        out_shape=jax.ShapeDtypeStruct((t, n_heads * C_V_DIM), BF16),
        scratch_shapes=[pltpu.VMEM((2, tq, tq), F32), pltpu.VMEM((2, tq, tq), F32),
                        pltpu.VMEM((2, 1, tq), F32), pltpu.VMEM((2, 1, tq), F32),
                        pltpu.VMEM((2, VT_ROWS, tq), F32)],
        compiler_params=_cparams("parallel", "parallel", "arbitrary"),
        name=name,
    )(q, k, vt)


def _mem_kv_kernel(mem_ref, g_ref, w_ref, kg_ref, k_ref, v_ref):
    mn = (_rms(mem_ref[0]) * g_ref[...]).astype(BF16)
    kv = jnp.dot(mn, w_ref[...], preferred_element_type=F32)
    for hd in range(M_HEADS):
        cols = slice(hd * M_HEAD_DIM, (hd + 1) * M_HEAD_DIM)
        k_ref[0, :, cols] = (_rms(kv[:, cols]) * kg_ref[...]).astype(BF16)
    v_ref[0] = kv[:, M_W:].astype(BF16)


def _mem_kv(mem, p):
    nb, ml, _ = mem.shape
    consts = [p["mem_g"], p["w_mem_kv"], p["mkg"]]
    blk = pl.BlockSpec((1, ml, M_W), lambda b: (b, 0, 0))
    return pl.pallas_call(
        _mem_kv_kernel,
        grid=(nb,),
        in_specs=[pl.BlockSpec((1, ml, D_MODEL), lambda b: (b, 0, 0))] + [_full(c.shape) for c in consts],
        out_specs=[blk, blk],
        out_shape=[jax.ShapeDtypeStruct((nb, ml, M_W), BF16)] * 2,
        compiler_params=_cparams("parallel"),
        name="mem_kv",
    )(mem, *consts)


def _outproj_kernel(x_ref, a_ref, bn_ref, c_ref, ona_ref, onc_ref, wo_ref, xg_ref, wq_ref, mqg_ref,
                    km_ref, vm_ref, wmo_ref, o_ref):
    a_n = (_rms(a_ref[...].astype(F32)) * ona_ref[...]).astype(BF16)
    c_n = (_rms(c_ref[...].astype(F32)) * onc_ref[...]).astype(BF16)
    mix = jnp.concatenate([a_n, bn_ref[...], c_n], axis=-1)
    x1 = x_ref[...] + jnp.dot(mix, wo_ref[...], preferred_element_type=F32)

    h = (_rms(x1) * xg_ref[...]).astype(BF16)
    q = jnp.dot(h, wq_ref[...], preferred_element_type=F32)
    outs = []
    for hd in range(M_HEADS):
        cols = slice(hd * M_HEAD_DIM, (hd + 1) * M_HEAD_DIM)
        qh = (_rms(q[:, cols]) * mqg_ref[...]).astype(BF16)
        s = lax.dot_general(qh, km_ref[0, :, cols], (((1,), (1,)), ((), ())), preferred_element_type=F32)
        e = jnp.exp(s - jnp.max(s, axis=-1, keepdims=True))
        pr = e / jnp.sum(e, axis=-1, keepdims=True)
        outs.append(jnp.dot(pr.astype(BF16), vm_ref[0, :, cols], preferred_element_type=F32).astype(BF16))
    o_ref[...] = x1 + jnp.dot(jnp.concatenate(outs, axis=-1), wmo_ref[...], preferred_element_type=F32)


def _outproj(x2d, a, bn, c, km, vm, seq, p):
    t = x2d.shape[0]
    tm = ROW_TILE
    tps = seq // tm
    ml = km.shape[1]
    row = lambda w: pl.BlockSpec((tm, w), lambda i: (i, 0))
    memblk = pl.BlockSpec((1, ml, M_W), lambda i: (i // tps, 0, 0))
    c1 = [p["ona"], p["onc"], p["w_out"], p["xg"], p["w_mem_q"], p["mqg"]]
    return pl.pallas_call(
        _outproj_kernel,
        grid=(t // tm,),
        in_specs=[row(D_MODEL), row(A_W), row(B_W), row(C_W)] + [_full(c_.shape) for c_ in c1]
                 + [memblk, memblk, _full(p["w_mem_out"].shape)],
        out_specs=row(D_MODEL),
        out_shape=jax.ShapeDtypeStruct((t, D_MODEL), F32),
        compiler_params=_cparams("parallel"),
        name="out_proj_mem_attn",
    )(x2d, a, bn, c, *c1, km, vm, p["w_mem_out"])


def _silu(x):
    return x * jax.nn.sigmoid(x)


def _ffn_kernel(x_ref, g_ref, wg_ref, wu_ref, wd_ref, o_ref, *, n_chunks):
    x = x_ref[...]
    h = (_rms(x) * g_ref[...]).astype(BF16)
    fc = wg_ref.shape[1] // n_chunks
    acc = x
    for c in range(n_chunks):
        cols = slice(c * fc, (c + 1) * fc)
        act = _silu(jnp.dot(h, wg_ref[:, cols], preferred_element_type=F32)) * \
            jnp.dot(h, wu_ref[:, cols], preferred_element_type=F32)
        acc = acc + jnp.dot(act.astype(BF16), wd_ref[cols, :], preferred_element_type=F32)
    o_ref[...] = acc


def _ffn(x2d, g, wg, wu, wd):
    t = x2d.shape[0]
    tm = ROW_TILE
    row = pl.BlockSpec((tm, D_MODEL), lambda i: (i, 0))
    resident = lambda a: pl.BlockSpec(a.shape, lambda i: (0, 0), pipeline_mode=pl.Buffered(1))
    return pl.pallas_call(
        functools.partial(_ffn_kernel, n_chunks=2),
        grid=(t // tm,),
        in_specs=[row, _full(g.shape), resident(wg), resident(wu), resident(wd)],
        out_specs=row,
        out_shape=jax.ShapeDtypeStruct((t, D_MODEL), F32),
        compiler_params=_cparams("parallel"),
        name="ffn_dense",
    )(x2d, g, wg, wu, wd)


def _moe_kernel(x_ref, g_ref, wr_ref, br_ref, tri_ref, wg_ref, wu_ref, wd_ref, o_ref,
                h_ref, comb_ref, rank_ref, rank_t_ref):
    e = pl.program_id(1)
    lane = _lane_iota()
    cap = MOE_CAP

    @pl.when(e == 0)
    def _():
        x = x_ref[...]
        h = _rms(x) * g_ref[...]
        h_ref[...] = h.astype(BF16)
        logits = jnp.dot(h, wr_ref[...], preferred_element_type=F32, precision=HIGHEST) + br_ref[...]
        logits = jnp.where(lane < N_EXPERTS, logits, -jnp.inf)
        v1 = jnp.max(logits, axis=-1, keepdims=True)
        i1 = jnp.min(jnp.where(logits == v1, lane, LANES), axis=-1, keepdims=True)
        rest = jnp.where(lane == i1, -jnp.inf, logits)
        v2 = jnp.max(rest, axis=-1, keepdims=True)
        i2 = jnp.min(jnp.where(rest == v2, lane, LANES), axis=-1, keepdims=True)
        e2 = jnp.exp(v2 - v1)
        g1 = 1.0 / (1.0 + e2)
        comb_ref[...] = jnp.where(lane == i1, g1, 0.0) + jnp.where(lane == i2, e2 * g1, 0.0)
        hit = (lane == i1) | (lane == i2)
        before = jnp.dot(tri_ref[...], jnp.where(hit, 1.0, 0.0).astype(BF16), preferred_element_type=F32)
        rank = jnp.where(hit, before, -1.0)
        rank_ref[...] = rank
        rank_t_ref[...] = rank.T[0:N_EXPERTS, :]
        o_ref[...] = x

    pick = lane == e
    rank_col = jnp.sum(jnp.where(pick, rank_ref[...], 0.0), axis=-1, keepdims=True)
    gate_col = jnp.sum(jnp.where(pick, comb_ref[...], 0.0), axis=-1, keepdims=True)
    rank_row = rank_t_ref[pl.ds(e, 1), :]
    n_tokens = jnp.max(rank_col).astype(jnp.int32) + 1
    slot_col = lax.broadcasted_iota(jnp.int32, (cap, 1), 0).astype(F32)
    slot_row = lax.broadcasted_iota(jnp.int32, (1, cap), 1).astype(F32)

    def block(b, carry):
        r0 = (b * cap).astype(F32)
        sel = jnp.where(rank_row - r0 == slot_col, 1.0, 0.0).astype(BF16)
        sel_t = jnp.where(rank_col - r0 == slot_row, 1.0, 0.0).astype(BF16)
        xe = jnp.dot(sel, h_ref[...], preferred_element_type=F32).astype(BF16)
        act = _silu(jnp.dot(xe, wg_ref[0], preferred_element_type=F32)) * \
            jnp.dot(xe, wu_ref[0], preferred_element_type=F32)
        y = jnp.dot(act.astype(BF16), wd_ref[0], preferred_element_type=F32)
        o_ref[...] += gate_col * jnp.dot(sel_t, y.astype(BF16), preferred_element_type=F32)
        return carry

    lax.fori_loop(0, (n_tokens + cap - 1) // cap, block, 0)


def _moe(x2d, g, wr, br, wg, wu, wd):
    t = x2d.shape[0]
    tm = MOE_CHUNK
    ff = wg.shape[2]
    once = pl.Buffered(1)
    tri = jnp.asarray(np.tril(np.ones((tm, tm), np.float32), -1)).astype(BF16)
    return pl.pallas_call(
        _moe_kernel,
        grid=(t // tm, N_EXPERTS),
        in_specs=[pl.BlockSpec((tm, D_MODEL), lambda i, e: (i, 0), pipeline_mode=once),
                  _full(g.shape), _full(wr.shape), _full(br.shape),
                  pl.BlockSpec((tm, tm), lambda i, e: (0, 0), pipeline_mode=once),
                  pl.BlockSpec((1, D_MODEL, ff), lambda i, e: (e, 0, 0)),
                  pl.BlockSpec((1, D_MODEL, ff), lambda i, e: (e, 0, 0)),
                  pl.BlockSpec((1, ff, D_MODEL), lambda i, e: (e, 0, 0))],
        out_specs=pl.BlockSpec((tm, D_MODEL), lambda i, e: (i, 0)),
        out_shape=jax.ShapeDtypeStruct((t, D_MODEL), F32),
        scratch_shapes=[pltpu.VMEM((tm, D_MODEL), BF16), pltpu.VMEM((tm, LANES), F32),
                        pltpu.VMEM((tm, LANES), F32), pltpu.VMEM((N_EXPERTS, tm), F32)],
        compiler_params=_cparams("parallel", "arbitrary"),
        name="moe_routed",
    )(x2d, g, wr, br, tri, wg, wu, wd)


def _pad_cols(w, width):
    return jnp.pad(w, ((0, 0), (0, width - w.shape[1])))


def _layer_params(l, mix_norm, w_in, b_forget, a_q_norm, a_k_norm, b_v_norm, b_spatial_w, b_spatial_b,
                  c_q_lat_norm, c_w_uq, c_kv_lat_norm, c_w_ukv, c_q_nope_norm, c_q_rope_norm,
                  c_k_nope_norm, c_k_rope_norm, out_norm_a, out_norm_b, out_norm_c, w_out,
                  xattn_norm, mem_norm, w_mem_q, w_mem_kv, m_q_norm, m_k_norm, w_mem_out):
    p = {}
    o = np.cumsum((0, A_W, A_W, A_W, A_HEADS, B_W, B_W, C_Q_RANK, C_KV_RANK, C_ROPE_DIM))
    w = w_in[l]
    seg = lambda n: w[:, o[n]:o[n + 1]]
    fa = seg(3)
    misc = jnp.zeros((D_MODEL, LANES), F32)
    misc = misc.at[:, ROPE_LANE:ROPE_LANE + C_ROPE_DIM].set(seg(8))
    fb = jnp.zeros((1, LANES), F32)
    for hd in range(A_HEADS):
        ln = FORGET_LANE + 8 * (hd // 2) + hd % 2
        misc = misc.at[:, ln].set(fa[:, hd])
        fb = fb.at[0, ln].set(b_forget[l, hd])
    p["w_in"] = jnp.concatenate([seg(0), seg(1), seg(2), seg(4), seg(5), seg(6), seg(7), misc], axis=1).astype(BF16)
    p["fb"] = fb
    p["mix_g"] = mix_norm[l][None]
    p["aq"] = jnp.tile(a_q_norm[l], A_HEADS)[None] * (A_HEAD_DIM ** -0.5 * LOG2E)
    p["ak"] = jnp.tile(a_k_norm[l], A_HEADS)[None]
    p["bvg"] = b_v_norm[l][None]
    pos = np.arange(B_WINDOW)
    mask = (pos[None, :] // CHUNK) <= (pos[:, None] // CHUNK)
    p["ws"] = jnp.where(mask[None], b_spatial_w[l], 0.0).reshape(B_GROUPS * B_WINDOW, B_WINDOW).astype(BF16)
    p["bs"] = jnp.repeat(b_spatial_b[l].T, B_GROUP_DIM, axis=1)
    p["onb"] = out_norm_b[l][None]
    p["cqg"] = c_q_lat_norm[l][None]
    p["ckvg"] = c_kv_lat_norm[l][None]
    gidx = np.arange(A_W) // A_HEAD_DIM
    p["gm"] = jnp.asarray((gidx[:, None] == gidx[None, :]).astype(np.float32) / A_HEAD_DIM)
    p["tri"] = jnp.asarray(np.tril(np.ones((ROW_TILE, ROW_TILE), np.float32)))

    qd = C_NOPE_DIM + C_ROPE_DIM
    p["wuq"] = jnp.pad(c_w_uq[l], ((0, 0), (0, 0), (0, HEAD_SLAB - qd))).reshape(C_Q_RANK, -1).astype(BF16)
    wukv = c_w_ukv[l]
    p["wuk"] = jnp.pad(wukv[:, :, :C_NOPE_DIM], ((0, 0), (0, 0), (0, HEAD_SLAB - C_NOPE_DIM))
                       ).reshape(C_KV_RANK, -1).astype(BF16)
    p["wuv"] = wukv[:, :, C_NOPE_DIM:].reshape(C_KV_RANK, C_W).astype(BF16)
    gq = jnp.concatenate([c_q_nope_norm[l], c_q_rope_norm[l]]) * (qd ** -0.5 * LOG2E)
    p["gq"] = _pad_cols(gq[None], LANES)
    p["gkn"] = _pad_cols(c_k_nope_norm[l][None], LANES)
    p["gkr"] = jnp.zeros((1, LANES), F32).at[0, ROPE_LANE:ROPE_LANE + C_ROPE_DIM].set(c_k_rope_norm[l])

    p["ona"] = out_norm_a[l][None]
    p["onc"] = out_norm_c[l][None]
    p["w_out"] = w_out[l].astype(BF16)
    p["xg"] = xattn_norm[l][None]
    p["w_mem_q"] = w_mem_q[l].astype(BF16)
    p["mqg"] = m_q_norm[l][None] * (M_HEAD_DIM ** -0.5)
    p["mem_g"] = mem_norm[l][None]
    p["w_mem_kv"] = w_mem_kv[l].astype(BF16)
    p["mkg"] = m_k_norm[l][None]
    p["w_mem_out"] = w_mem_out[l].astype(BF16)
    return p


def kernel(x, mem, positions, mix_norm, w_in, b_forget, a_q_norm, a_k_norm, b_v_norm, b_spatial_w, b_spatial_b, c_q_lat_norm, c_w_uq, c_kv_lat_norm, c_w_ukv, c_q_nope_norm, c_q_rope_norm, c_k_nope_norm, c_k_rope_norm, out_norm_a, out_norm_b, out_norm_c, w_out, xattn_norm, mem_norm, w_mem_q, w_mem_kv, m_q_norm, m_k_norm, w_mem_out, ffn_norm, ffn_w_gate, ffn_w_up, ffn_w_down, w_router, b_router, moe_w_gate, moe_w_up, moe_w_down):
    nb, seq, d = x.shape
    assert d == D_MODEL and seq % ROW_TILE == 0 and seq % ATTN_TILE == 0 and (nb * seq) % MOE_CHUNK == 0
    depth = w_in.shape[0]
    t = nb * seq
    x2d = x.reshape(t, d)
    cos, sin = _rope_tables(positions.reshape(t, 1).astype(F32))

    for l in range(depth):
        p = _layer_params(l, mix_norm, w_in, b_forget, a_q_norm, a_k_norm, b_v_norm, b_spatial_w,
                          b_spatial_b, c_q_lat_norm, c_w_uq, c_kv_lat_norm, c_w_ukv, c_q_nope_norm,
                          c_q_rope_norm, c_k_nope_norm, c_k_rope_norm, out_norm_a, out_norm_b,
                          out_norm_c, w_out, xattn_norm, mem_norm, w_mem_q, w_mem_kv, m_q_norm,
                          m_k_norm, w_mem_out)
        qa, ka, vta, bn, cq, ckv, misc = _inproj(x2d, seq, p)
        qc, kc, vtc = _mla_prep(cq, ckv, misc, cos, sin, seq, p)
        a = _attention(qa, ka, vta, unit=1, name="attn_fox")
        c = _attention(qc, kc, vtc, unit=CHUNK, name="attn_mla")
        km, vm = _mem_kv(mem, p)
        x2d = _outproj(x2d, a, bn, c, km, vm, seq, p)
        g = ffn_norm[l][None]
        if l % 2 == 0:
            m = l // 2
            ff = ffn_w_gate.shape[2]
            ff_pad = -(-ff // (2 * LANES)) * (2 * LANES)
            wg = _pad_cols(ffn_w_gate[m], ff_pad).astype(BF16)
            wu = _pad_cols(ffn_w_up[m], ff_pad).astype(BF16)
            wd = jnp.pad(ffn_w_down[m], ((0, ff_pad - ff), (0, 0))).astype(BF16)
            x2d = _ffn(x2d, g, wg, wu, wd)
        else:
            m = l // 2
            wr = _pad_cols(w_router[m], LANES)
            br = _pad_cols(b_router[m][None], LANES)
            x2d = _moe(x2d, g, wr, br, moe_w_gate[m].astype(BF16), moe_w_up[m].astype(BF16),
                       moe_w_down[m].astype(BF16))
    return x2d.reshape(nb, seq, d)
```

```python
import functools

import numpy as np
import jax
import jax.numpy as jnp
from jax import lax
from jax.experimental import pallas as pl
from jax.experimental.pallas import tpu as pltpu

F32 = jnp.float32
BF16 = jnp.bfloat16
HIGHEST = lax.Precision.HIGHEST

D_MODEL = 1024
CHUNK = 64
EPS = 1e-6
NEG_INF = -1e30
A_HEADS, A_HEAD_DIM = 4, 64
B_GROUPS, B_GROUP_DIM, B_WINDOW = 4, 64, 128
C_HEADS, C_NOPE_DIM, C_ROPE_DIM, C_V_DIM = 8, 64, 32, 64
C_Q_RANK, C_KV_RANK = 256, 128
ROPE_THETA = 10000.0
M_HEADS, M_HEAD_DIM = 4, 128
N_EXPERTS = 8
A_W = A_HEADS * A_HEAD_DIM
B_W = B_GROUPS * B_GROUP_DIM
C_W = C_HEADS * C_V_DIM
M_W = M_HEADS * M_HEAD_DIM

LANES = 128

SEG_Q, SEG_K, SEG_V, SEG_U, SEG_VB, SEG_CQ, SEG_CKV, SEG_MISC = 0, 256, 512, 768, 1024, 1280, 1536, 1664
IN_PAD_W = SEG_MISC + LANES
ROPE_LANE = C_NOPE_DIM
FORGET_LANE = 96
HEAD_SLAB = LANES
VT_ROWS = 80
LOG2E = float(np.log2(np.e))

ROW_TILE = 512
ATTN_TILE = 512
MOE_CHUNK = 1024
MOE_CAP = 320
MOE_SUB = 2
VMEM_LIMIT = 56 * 1024 * 1024


def _cparams(*sem):
    return pltpu.CompilerParams(dimension_semantics=sem, vmem_limit_bytes=VMEM_LIMIT)


def _full(shape):
    n = len(shape)
    return pl.BlockSpec(shape, lambda *_: (0,) * n)


def _rms(x):
    return x * lax.rsqrt(jnp.mean(x * x, axis=-1, keepdims=True) + EPS)


def _lane_iota(n=LANES):
    return lax.broadcasted_iota(jnp.int32, (1, n), 1)


def _rope_table_kernel(pos_ref, inv_ref, sgn_ref, cos_ref, sin_ref):
    ang = pos_ref[...] * inv_ref[...]
    cos_ref[...] = jnp.cos(ang)
    sin_ref[...] = jnp.sin(ang) * sgn_ref[...]


def _rope_tables(pos_col):
    t = pos_col.shape[0]
    half = C_ROPE_DIM // 2
    inv = ROPE_THETA ** (-jnp.arange(half, dtype=F32) / half)
    inv_l = jnp.zeros((1, LANES), F32).at[0, ROPE_LANE:ROPE_LANE + C_ROPE_DIM].set(jnp.tile(inv, 2))
    sgn = np.zeros((1, LANES), np.float32)
    sgn[0, ROPE_LANE:ROPE_LANE + half] = -1.0
    sgn[0, ROPE_LANE + half:ROPE_LANE + C_ROPE_DIM] = 1.0
    tm = ROW_TILE
    return pl.pallas_call(
        _rope_table_kernel,
        grid=(t // tm,),
        in_specs=[pl.BlockSpec((tm, 1), lambda i: (i, 0)), _full((1, LANES)), _full((1, LANES))],
        out_specs=[pl.BlockSpec((tm, LANES), lambda i: (i, 0))] * 2,
        out_shape=[jax.ShapeDtypeStruct((t, LANES), F32)] * 2,
        compiler_params=_cparams("parallel"),
        name="rope_tables",
    )(pos_col, inv_l, jnp.asarray(sgn))


def _rotate(x, cos, sin_signed, lane):
    half = C_ROPE_DIM // 2
    partner = jnp.where(lane < ROPE_LANE + half,
                        pltpu.roll(x, LANES - half, 1), pltpu.roll(x, half, 1))
    return x * cos + partner * sin_signed


def _store_v_transposed(vt_ref, v, n_heads):
    tm = v.shape[0]
    v_t = v.T
    tail = jnp.where(lax.broadcasted_iota(jnp.int32, (VT_ROWS - C_V_DIM, tm), 0) == 0, 1.0, 0.0).astype(BF16)
    for hd in range(n_heads):
        vt_ref[0, hd, 0:C_V_DIM, :] = v_t[hd * C_V_DIM:(hd + 1) * C_V_DIM, :].astype(BF16)
        vt_ref[0, hd, C_V_DIM:VT_ROWS, :] = tail


def _gelu(x):
    return 0.5 * x * (1.0 + lax.erf(x * np.float32(1.0 / np.sqrt(2.0))))


def _inproj_kernel(x_ref, g_ref, w_ref, aq_ref, ak_ref, fb_ref, bvg_ref, ws_ref, bs_ref, onb_ref,
                   cqg_ref, ckvg_ref, gm_ref, tri_ref,
                   qa_ref, ka_ref, vt_ref, bn_ref, cq_ref, ckv_ref, misc_ref,
                   carry_ref, *, tiles_per_seq):
    i = pl.program_id(0)
    tm = x_ref.shape[0]
    h = _rms(x_ref[...]) * g_ref[...]
    proj = jnp.dot(h.astype(BF16), w_ref[...], preferred_element_type=F32)
    gm = gm_ref[...]

    def group_mean(v):
        return jnp.dot(v, gm, preferred_element_type=F32, precision=HIGHEST)

    misc = proj[:, SEG_MISC:SEG_MISC + LANES]
    misc_ref[...] = misc
    z = misc + fb_ref[...]
    log_f = jnp.minimum(z, 0.0) - jnp.log1p(jnp.exp(-jnp.abs(z)))

    @pl.when(i % tiles_per_seq == 0)
    def _():
        carry_ref[...] = jnp.zeros_like(carry_ref)

    cum = jnp.dot(tri_ref[...], log_f, preferred_element_type=F32, precision=HIGHEST) + carry_ref[...]
    carry_ref[...] = cum[tm - 1:tm, :]
    f_hi = (cum * LOG2E).astype(BF16).astype(F32)
    f_rem = cum * LOG2E - f_hi
    f_mid = f_rem.astype(BF16).astype(F32)
    f_lo = f_rem - f_mid

    q = proj[:, SEG_Q:SEG_Q + A_W]
    qn = q * lax.rsqrt(group_mean(q * q) + EPS) * aq_ref[...]
    k = proj[:, SEG_K:SEG_K + A_W]
    kn = k * lax.rsqrt(group_mean(k * k) + EPS) * ak_ref[...]
    lane = _lane_iota()
    for hd in range(A_HEADS):
        pair = slice((hd // 2) * LANES, (hd // 2 + 1) * LANES)
        slab = slice(hd * HEAD_SLAB, (hd + 1) * HEAD_SLAB)
        data = (lane < A_HEAD_DIM) if hd % 2 == 0 else (lane >= A_HEAD_DIM)
        e0 = A_HEAD_DIM if hd % 2 == 0 else 0
        fl = FORGET_LANE + 8 * (hd // 2) + hd % 2
        ones = jnp.where((lane >= e0) & (lane < e0 + 3), 1.0, 0.0)
        qa_ref[:, slab] = jnp.where(data, qn[:, pair], ones).astype(BF16)
        bias = jnp.where(lane == e0, -f_hi[:, fl:fl + 1],
                         jnp.where(lane == e0 + 1, -f_mid[:, fl:fl + 1],
                                   jnp.where(lane == e0 + 2, -f_lo[:, fl:fl + 1], 0.0)))
        ka_ref[:, slab] = jnp.where(data, kn[:, pair], bias).astype(BF16)
    _store_v_transposed(vt_ref, proj[:, SEG_V:SEG_V + A_W], A_HEADS)

    u = _gelu(proj[:, SEG_U:SEG_U + B_W])
    v = _gelu(proj[:, SEG_VB:SEG_VB + B_W])
    dv = v - group_mean(v)
    vn = dv * lax.rsqrt(group_mean(dv * dv) + EPS) * bvg_ref[...]
    group = lax.broadcasted_iota(jnp.int32, (1, B_W), 1) // B_GROUP_DIM
    for w in range(tm // B_WINDOW):
        rows = slice(w * B_WINDOW, (w + 1) * B_WINDOW)
        y_all = jnp.dot(ws_ref[...], vn[rows].astype(BF16), preferred_element_type=F32)
        y = bs_ref[...]
        for g in range(B_GROUPS):
            y = y + jnp.where(group == g, y_all[g * B_WINDOW:(g + 1) * B_WINDOW], 0.0)
        b = u[rows] * y
        bn_ref[rows, :] = (_rms(b) * onb_ref[...]).astype(BF16)

    cq_ref[...] = (_rms(proj[:, SEG_CQ:SEG_CQ + C_Q_RANK]) * cqg_ref[...]).astype(BF16)
    ckv_ref[...] = (_rms(proj[:, SEG_CKV:SEG_CKV + C_KV_RANK]) * ckvg_ref[...]).astype(BF16)


def _inproj(x2d, seq, p):
    t = x2d.shape[0]
    tm = ROW_TILE
    tps = seq // tm
    nb = t // seq
    row = lambda w: pl.BlockSpec((tm, w), lambda i: (i, 0))
    qk_w = A_HEADS * HEAD_SLAB
    out_shape = [
        jax.ShapeDtypeStruct((t, qk_w), BF16), jax.ShapeDtypeStruct((t, qk_w), BF16),
        jax.ShapeDtypeStruct((nb, A_HEADS, VT_ROWS, seq), BF16),
        jax.ShapeDtypeStruct((t, B_W), BF16),
        jax.ShapeDtypeStruct((t, C_Q_RANK), BF16), jax.ShapeDtypeStruct((t, C_KV_RANK), BF16),
        jax.ShapeDtypeStruct((t, LANES), F32),
    ]
    out_specs = [row(qk_w), row(qk_w),
                 pl.BlockSpec((1, A_HEADS, VT_ROWS, tm), lambda i: (i // tps, 0, 0, i % tps)),
                 row(B_W), row(C_Q_RANK), row(C_KV_RANK), row(LANES)]
    consts = [p["mix_g"], p["w_in"], p["aq"], p["ak"], p["fb"], p["bvg"], p["ws"], p["bs"], p["onb"],
              p["cqg"], p["ckvg"], p["gm"], p["tri"]]
    return pl.pallas_call(
        functools.partial(_inproj_kernel, tiles_per_seq=tps),
        grid=(t // tm,),
        in_specs=[row(D_MODEL)] + [_full(c.shape) for c in consts],
        out_specs=out_specs,
        out_shape=out_shape,
        scratch_shapes=[pltpu.VMEM((1, LANES), F32)],
        compiler_params=_cparams("arbitrary"),
        name="in_proj",
    )(x2d, *consts)


def _mla_prep_kernel(cq_ref, ckv_ref, misc_ref, cos_ref, sin_ref, wuq_ref, wuk_ref, wuv_ref,
                     gq_ref, gkn_ref, gkr_ref, qc_ref, kc_ref, vt_ref):
    lane = _lane_iota()
    nope = lane < C_NOPE_DIM
    rope = (lane >= ROPE_LANE) & (lane < ROPE_LANE + C_ROPE_DIM)
    cos, sin = cos_ref[...], sin_ref[...]
    q = jnp.dot(cq_ref[...], wuq_ref[...], preferred_element_type=F32)
    kn = jnp.dot(ckv_ref[...], wuk_ref[...], preferred_element_type=F32)
    _store_v_transposed(vt_ref, jnp.dot(ckv_ref[...], wuv_ref[...], preferred_element_type=F32), C_HEADS)

    kr = jnp.where(rope, misc_ref[...], 0.0)
    kr = kr * lax.rsqrt(jnp.sum(kr * kr, axis=-1, keepdims=True) * (1.0 / C_ROPE_DIM) + EPS) * gkr_ref[...]
    kr = _rotate(kr, cos, sin, lane)

    for hd in range(C_HEADS):
        cols = slice(hd * HEAD_SLAB, (hd + 1) * HEAD_SLAB)
        qh = q[:, cols]
        sq = qh * qh
        r_n = lax.rsqrt(jnp.sum(jnp.where(nope, sq, 0.0), axis=-1, keepdims=True) * (1.0 / C_NOPE_DIM) + EPS)
        r_r = lax.rsqrt(jnp.sum(jnp.where(rope, sq, 0.0), axis=-1, keepdims=True) * (1.0 / C_ROPE_DIM) + EPS)
        qn = qh * jnp.where(nope, r_n, r_r) * gq_ref[...]
        qc_ref[:, cols] = _rotate(qn, cos, sin, lane).astype(BF16)
        kh = kn[:, cols]
        r_k = lax.rsqrt(jnp.sum(kh * kh, axis=-1, keepdims=True) * (1.0 / C_NOPE_DIM) + EPS)
        kc_ref[:, cols] = (kh * r_k * gkn_ref[...] + kr).astype(BF16)


def _mla_prep(cq, ckv, misc, cos, sin, seq, p):
    t = cq.shape[0]
    tm = ROW_TILE
    tps = seq // tm
    row = lambda w: pl.BlockSpec((tm, w), lambda i: (i, 0))
    consts = [p["wuq"], p["wuk"], p["wuv"], p["gq"], p["gkn"], p["gkr"]]
    qk_w = C_HEADS * HEAD_SLAB
    return pl.pallas_call(
        _mla_prep_kernel,
        grid=(t // tm,),
        in_specs=[row(C_Q_RANK), row(C_KV_RANK), row(LANES), row(LANES), row(LANES)]
                 + [_full(c.shape) for c in consts],
        out_specs=[row(qk_w), row(qk_w),
                   pl.BlockSpec((1, C_HEADS, VT_ROWS, tm), lambda i: (i // tps, 0, 0, i % tps))],
        out_shape=[jax.ShapeDtypeStruct((t, qk_w), BF16), jax.ShapeDtypeStruct((t, qk_w), BF16),
                   jax.ShapeDtypeStruct((t // seq, C_HEADS, VT_ROWS, seq), BF16)],
        compiler_params=_cparams("parallel"),
        name="mla_prep",
    )(cq, ckv, misc, cos, sin, *consts)


def _attn_kernel(q_ref, k_ref, vt_ref, o_ref, s0_ref, s1_ref, mrun_ref, macc_ref, acc_ref, *, unit):
    i = pl.program_id(2)
    tq = q_ref.shape[0]
    tk = tq
    mrun_ref[...] = jnp.full(mrun_ref.shape, NEG_INF, F32)
    macc_ref[...] = jnp.full(macc_ref.shape, NEG_INF, F32)
    acc_ref[...] = jnp.zeros(acc_ref.shape, F32)

    def scores(j, s_ref, masked=False):
        k_start = pl.multiple_of(j * tk, tk)
        if masked:
            kpos = lax.broadcasted_iota(jnp.int32, (tk, tq), 0) // unit
            qpos = lax.broadcasted_iota(jnp.int32, (tk, tq), 1) // unit
            allowed = kpos <= qpos
        for hh in range(2):
            cols = slice(hh * HEAD_SLAB, (hh + 1) * HEAD_SLAB)
            s_t = lax.dot_general(k_ref[pl.ds(k_start, tk), cols], q_ref[:, cols],
                                  (((1,), (1,)), ((), ())), preferred_element_type=F32)
            if masked:
                s_t = jnp.where(allowed, s_t, NEG_INF)
            s_ref[hh] = s_t
            mrun_ref[hh] = jnp.maximum(mrun_ref[hh], jnp.max(s_t, axis=0, keepdims=True))

    def accumulate(j, s_ref):
        k_start = pl.multiple_of(j * tk, tk)
        for hh in range(2):
            m = mrun_ref[hh]
            p_t = jnp.exp2(s_ref[hh] - m).astype(BF16)
            pv = jnp.dot(vt_ref[0, hh, :, pl.ds(k_start, tk)], p_t, preferred_element_type=F32)
            acc_ref[hh] = jnp.exp2(macc_ref[hh] - m) * acc_ref[hh] + pv
            macc_ref[hh] = m

    scores(i, s1_ref, masked=True)
    n_pairs = i // 2

    def body(t, carry):
        j0 = 2 * t
        accumulate(jnp.where(t == 0, i, j0 - 1), s1_ref)
        scores(j0, s0_ref)
        accumulate(j0, s0_ref)
        scores(j0 + 1, s1_ref)
        return carry

    lax.fori_loop(0, n_pairs, body, 0)
    in_s1 = jnp.where(n_pairs == 0, i, 2 * n_pairs - 1)

    @pl.when(i % 2 == 1)
    def _():
        accumulate(in_s1, s1_ref)
        scores(i - 1, s0_ref)
        accumulate(i - 1, s0_ref)

    @pl.when(i % 2 == 0)
    def _():
        accumulate(in_s1, s1_ref)

    halves = [acc_ref[hh, 0:C_V_DIM, :] / acc_ref[hh, C_V_DIM:C_V_DIM + 1, :] for hh in range(2)]
    o_ref[...] = jnp.concatenate(halves, axis=0).T.astype(o_ref.dtype)


def _attention(q, k, vt, *, unit, name):
    t = q.shape[0]
    nb, n_heads, _, seq = vt.shape
    tq = ATTN_TILE
    nq = seq // tq
    return pl.pallas_call(
        functools.partial(_attn_kernel, unit=unit),
        grid=(nb, n_heads // 2, nq),
        in_specs=[pl.BlockSpec((tq, 2 * HEAD_SLAB), lambda b, p, i: (b * nq + i, p)),
                  pl.BlockSpec((seq, 2 * HEAD_SLAB), lambda b, p, i: (b, p)),
                  pl.BlockSpec((1, 2, VT_ROWS, seq), lambda b, p, i: (b, p, 0, 0))],
        out_specs=pl.BlockSpec((tq, 2 * C_V_DIM), lambda b, p, i: (b * nq + i, p)),
        out_shape=jax.ShapeDtypeStruct((t, n_heads * C_V_DIM), BF16),
        scratch_shapes=[pltpu.VMEM((2, tq, tq), F32), pltpu.VMEM((2, tq, tq), F32),
                        pltpu.VMEM((2, 1, tq), F32), pltpu.VMEM((2, 1, tq), F32),
                        pltpu.VMEM((2, VT_ROWS, tq), F32)],
        compiler_params=_cparams("parallel", "parallel", "arbitrary"),
        name=name,
    )(q, k, vt)


def _mem_kv_kernel(mem_ref, g_ref, w_ref, kg_ref, k_ref, v_ref):
    mn = (_rms(mem_ref[0]) * g_ref[...]).astype(BF16)
    kv = jnp.dot(mn, w_ref[...], preferred_element_type=F32)
    for hd in range(M_HEADS):
        cols = slice(hd * M_HEAD_DIM, (hd + 1) * M_HEAD_DIM)
        k_ref[0, :, cols] = (_rms(kv[:, cols]) * kg_ref[...]).astype(BF16)
    v_ref[0] = kv[:, M_W:].astype(BF16)


def _mem_kv(mem, p):
    nb, ml, _ = mem.shape
    consts = [p["mem_g"], p["w_mem_kv"], p["mkg"]]
    blk = pl.BlockSpec((1, ml, M_W), lambda b: (b, 0, 0))
    return pl.pallas_call(
        _mem_kv_kernel,
        grid=(nb,),
        in_specs=[pl.BlockSpec((1, ml, D_MODEL), lambda b: (b, 0, 0))] + [_full(c.shape) for c in consts],
        out_specs=[blk, blk],
        out_shape=[jax.ShapeDtypeStruct((nb, ml, M_W), BF16)] * 2,
        compiler_params=_cparams("parallel"),
        name="mem_kv",
    )(mem, *consts)


def _outproj_kernel(x_ref, a_ref, bn_ref, c_ref, ona_ref, onc_ref, wo_ref, xg_ref, wq_ref, mqg_ref,
                    km_ref, vm_ref, wmo_ref, o_ref):
    a_n = (_rms(a_ref[...].astype(F32)) * ona_ref[...]).astype(BF16)
    c_n = (_rms(c_ref[...].astype(F32)) * onc_ref[...]).astype(BF16)
    mix = jnp.concatenate([a_n, bn_ref[...], c_n], axis=-1)
    x1 = x_ref[...] + jnp.dot(mix, wo_ref[...], preferred_element_type=F32)

    h = (_rms(x1) * xg_ref[...]).astype(BF16)
    q = jnp.dot(h, wq_ref[...], preferred_element_type=F32)
    outs = []
    for hd in range(M_HEADS):
        cols = slice(hd * M_HEAD_DIM, (hd + 1) * M_HEAD_DIM)
        qh = (_rms(q[:, cols]) * mqg_ref[...]).astype(BF16)
        s = lax.dot_general(qh, km_ref[0, :, cols], (((1,), (1,)), ((), ())), preferred_element_type=F32)
        e = jnp.exp(s - jnp.max(s, axis=-1, keepdims=True))
        pr = e / jnp.sum(e, axis=-1, keepdims=True)
        outs.append(jnp.dot(pr.astype(BF16), vm_ref[0, :, cols], preferred_element_type=F32).astype(BF16))
    o_ref[...] = x1 + jnp.dot(jnp.concatenate(outs, axis=-1), wmo_ref[...], preferred_element_type=F32)


def _outproj(x2d, a, bn, c, km, vm, seq, p):
    t = x2d.shape[0]
    tm = ROW_TILE
    tps = seq // tm
    ml = km.shape[1]
    row = lambda w: pl.BlockSpec((tm, w), lambda i: (i, 0))
    memblk = pl.BlockSpec((1, ml, M_W), lambda i: (i // tps, 0, 0))
    c1 = [p["ona"], p["onc"], p["w_out"], p["xg"], p["w_mem_q"], p["mqg"]]
    return pl.pallas_call(
        _outproj_kernel,
        grid=(t // tm,),
        in_specs=[row(D_MODEL), row(A_W), row(B_W), row(C_W)] + [_full(c_.shape) for c_ in c1]
                 + [memblk, memblk, _full(p["w_mem_out"].shape)],
        out_specs=row(D_MODEL),
        out_shape=jax.ShapeDtypeStruct((t, D_MODEL), F32),
        compiler_params=_cparams("parallel"),
        name="out_proj_mem_attn",
    )(x2d, a, bn, c, *c1, km, vm, p["w_mem_out"])


def _silu(x):
    return x * jax.nn.sigmoid(x)


def _ffn_kernel(x_ref, g_ref, wg_ref, wu_ref, wd_ref, o_ref, *, n_chunks):
    x = x_ref[...]
    h = (_rms(x) * g_ref[...]).astype(BF16)
    fc = wg_ref.shape[1] // n_chunks
    acc = x
    for c in range(n_chunks):
        cols = slice(c * fc, (c + 1) * fc)
        act = _silu(jnp.dot(h, wg_ref[:, cols], preferred_element_type=F32)) * \
            jnp.dot(h, wu_ref[:, cols], preferred_element_type=F32)
        acc = acc + jnp.dot(act.astype(BF16), wd_ref[cols, :], preferred_element_type=F32)
    o_ref[...] = acc


def _ffn(x2d, g, wg, wu, wd):
    t = x2d.shape[0]
    tm = ROW_TILE
    row = pl.BlockSpec((tm, D_MODEL), lambda i: (i, 0))
    resident = lambda a: pl.BlockSpec(a.shape, lambda i: (0, 0), pipeline_mode=pl.Buffered(1))
    return pl.pallas_call(
        functools.partial(_ffn_kernel, n_chunks=2),
        grid=(t // tm,),
        in_specs=[row, _full(g.shape), resident(wg), resident(wu), resident(wd)],
        out_specs=row,
        out_shape=jax.ShapeDtypeStruct((t, D_MODEL), F32),
        compiler_params=_cparams("parallel"),
        name="ffn_dense",
    )(x2d, g, wg, wu, wd)


def _moe_kernel(x_hbm_ref, g_ref, wr_ref, br_ref, tri_ref, wg_ref, wu_ref, wd_ref, o_ref,
                h_ref, comb_ref, rank_ref, rank_t_ref, sem_ref):
    i = pl.program_id(0)
    e = pl.program_id(1)
    lane = _lane_iota()
    cap = MOE_CAP
    chunk = MOE_CHUNK
    n_sub = o_ref.shape[0] // chunk
    subs = [slice(sc * chunk, (sc + 1) * chunk) for sc in range(n_sub)]

    @pl.when(e == 0)
    def _():
        rows = o_ref.shape[0]
        load = pltpu.make_async_copy(x_hbm_ref.at[pl.ds(i * rows, rows)], o_ref, sem_ref)
        load.start()
        load.wait()
        for sc, rs in enumerate(subs):
            h = _rms(o_ref[rs, :]) * g_ref[...]
            h_ref[rs, :] = h.astype(BF16)
            logits = jnp.dot(h, wr_ref[...], preferred_element_type=F32, precision=HIGHEST) + br_ref[...]
            logits = jnp.where(lane < N_EXPERTS, logits, -jnp.inf)
            v1 = jnp.max(logits, axis=-1, keepdims=True)
            i1 = jnp.min(jnp.where(logits == v1, lane, LANES), axis=-1, keepdims=True)
            rest = jnp.where(lane == i1, -jnp.inf, logits)
            v2 = jnp.max(rest, axis=-1, keepdims=True)
            i2 = jnp.min(jnp.where(rest == v2, lane, LANES), axis=-1, keepdims=True)
            e2 = jnp.exp(v2 - v1)
            g1 = 1.0 / (1.0 + e2)
            comb_ref[rs, :] = jnp.where(lane == i1, g1, 0.0) + jnp.where(lane == i2, e2 * g1, 0.0)
            hit = (lane == i1) | (lane == i2)
            before = jnp.dot(tri_ref[...], jnp.where(hit, 1.0, 0.0).astype(BF16), preferred_element_type=F32)
            rank = jnp.where(hit, before, -1.0)
            rank_ref[rs, :] = rank
            rank_t_ref[sc] = rank.T[0:N_EXPERTS, :]

    pick = lane == e
    rank_cols = [jnp.sum(jnp.where(pick, rank_ref[rs, :], 0.0), axis=-1, keepdims=True) for rs in subs]
    gate_cols = [jnp.sum(jnp.where(pick, comb_ref[rs, :], 0.0), axis=-1, keepdims=True) for rs in subs]
    rank_rows = [rank_t_ref[sc, pl.ds(e, 1), :] for sc in range(n_sub)]
    n_tokens = functools.reduce(jnp.maximum, [jnp.max(r) for r in rank_cols]).astype(jnp.int32) + 1
    slot_col = lax.broadcasted_iota(jnp.int32, (cap, 1), 0).astype(F32)
    slot_row = lax.broadcasted_iota(jnp.int32, (1, cap), 1).astype(F32)

    def block(b, carry):
        r0 = (b * cap).astype(F32)
        xe = []
        for sc, rs in enumerate(subs):
            sel = jnp.where(rank_rows[sc] - r0 == slot_col, 1.0, 0.0).astype(BF16)
            xe.append(jnp.dot(sel, h_ref[rs, :], preferred_element_type=F32).astype(BF16))
        xe = jnp.concatenate(xe, axis=0)
        act = _silu(jnp.dot(xe, wg_ref[0], preferred_element_type=F32)) * \
            jnp.dot(xe, wu_ref[0], preferred_element_type=F32)
        y = jnp.dot(act.astype(BF16), wd_ref[0], preferred_element_type=F32).astype(BF16)
        for sc, rs in enumerate(subs):
            sel_t = jnp.where(rank_cols[sc] - r0 == slot_row, 1.0, 0.0).astype(BF16)
            o_ref[rs, :] += gate_cols[sc] * jnp.dot(sel_t, y[sc * cap:(sc + 1) * cap, :],
                                                    preferred_element_type=F32)
        return carry

    lax.fori_loop(0, (n_tokens + cap - 1) // cap, block, 0)


def _moe(x2d, g, wr, br, wg, wu, wd):
    t = x2d.shape[0]
    chunk = MOE_CHUNK
    tm = MOE_SUB * chunk
    ff = wg.shape[2]
    tri = jnp.asarray(np.tril(np.ones((chunk, chunk), np.float32), -1)).astype(BF16)
    return pl.pallas_call(
        _moe_kernel,
        grid=(t // tm, N_EXPERTS),
        in_specs=[pl.BlockSpec(memory_space=pl.ANY),
                  _full(g.shape), _full(wr.shape), _full(br.shape),
                  pl.BlockSpec((chunk, chunk), lambda i, e: (0, 0), pipeline_mode=pl.Buffered(1)),
                  pl.BlockSpec((1, D_MODEL, ff), lambda i, e: (e, 0, 0)),
                  pl.BlockSpec((1, D_MODEL, ff), lambda i, e: (e, 0, 0)),
                  pl.BlockSpec((1, ff, D_MODEL), lambda i, e: (e, 0, 0))],
        out_specs=pl.BlockSpec((tm, D_MODEL), lambda i, e: (i, 0)),
        out_shape=jax.ShapeDtypeStruct((t, D_MODEL), F32),
        scratch_shapes=[pltpu.VMEM((tm, D_MODEL), BF16), pltpu.VMEM((tm, LANES), F32),
                        pltpu.VMEM((tm, LANES), F32), pltpu.VMEM((MOE_SUB, N_EXPERTS, chunk), F32),
                        pltpu.SemaphoreType.DMA(())],
        compiler_params=_cparams("parallel", "arbitrary"),
        name="moe_routed",
    )(x2d, g, wr, br, tri, wg, wu, wd)


def _pad_cols(w, width):
    return jnp.pad(w, ((0, 0), (0, width - w.shape[1])))


def _layer_params(l, mix_norm, w_in, b_forget, a_q_norm, a_k_norm, b_v_norm, b_spatial_w, b_spatial_b,
                  c_q_lat_norm, c_w_uq, c_kv_lat_norm, c_w_ukv, c_q_nope_norm, c_q_rope_norm,
                  c_k_nope_norm, c_k_rope_norm, out_norm_a, out_norm_b, out_norm_c, w_out,
                  xattn_norm, mem_norm, w_mem_q, w_mem_kv, m_q_norm, m_k_norm, w_mem_out):
    p = {}
    o = np.cumsum((0, A_W, A_W, A_W, A_HEADS, B_W, B_W, C_Q_RANK, C_KV_RANK, C_ROPE_DIM))
    w = w_in[l]
    seg = lambda n: w[:, o[n]:o[n + 1]]
    fa = seg(3)
    misc = jnp.zeros((D_MODEL, LANES), F32)
    misc = misc.at[:, ROPE_LANE:ROPE_LANE + C_ROPE_DIM].set(seg(8))
    fb = jnp.zeros((1, LANES), F32)
    for hd in range(A_HEADS):
        ln = FORGET_LANE + 8 * (hd // 2) + hd % 2
        misc = misc.at[:, ln].set(fa[:, hd])
        fb = fb.at[0, ln].set(b_forget[l, hd])
    p["w_in"] = jnp.concatenate([seg(0), seg(1), seg(2), seg(4), seg(5), seg(6), seg(7), misc], axis=1).astype(BF16)
    p["fb"] = fb
    p["mix_g"] = mix_norm[l][None]
    p["aq"] = jnp.tile(a_q_norm[l], A_HEADS)[None] * (A_HEAD_DIM ** -0.5 * LOG2E)
    p["ak"] = jnp.tile(a_k_norm[l], A_HEADS)[None]
    p["bvg"] = b_v_norm[l][None]
    pos = np.arange(B_WINDOW)
    mask = (pos[None, :] // CHUNK) <= (pos[:, None] // CHUNK)
    p["ws"] = jnp.where(mask[None], b_spatial_w[l], 0.0).reshape(B_GROUPS * B_WINDOW, B_WINDOW).astype(BF16)
    p["bs"] = jnp.repeat(b_spatial_b[l].T, B_GROUP_DIM, axis=1)
    p["onb"] = out_norm_b[l][None]
    p["cqg"] = c_q_lat_norm[l][None]
    p["ckvg"] = c_kv_lat_norm[l][None]
    gidx = np.arange(A_W) // A_HEAD_DIM
    p["gm"] = jnp.asarray((gidx[:, None] == gidx[None, :]).astype(np.float32) / A_HEAD_DIM)
    p["tri"] = jnp.asarray(np.tril(np.ones((ROW_TILE, ROW_TILE), np.float32)))

    qd = C_NOPE_DIM + C_ROPE_DIM
    p["wuq"] = jnp.pad(c_w_uq[l], ((0, 0), (0, 0), (0, HEAD_SLAB - qd))).reshape(C_Q_RANK, -1).astype(BF16)
    wukv = c_w_ukv[l]
    p["wuk"] = jnp.pad(wukv[:, :, :C_NOPE_DIM], ((0, 0), (0, 0), (0, HEAD_SLAB - C_NOPE_DIM))
                       ).reshape(C_KV_RANK, -1).astype(BF16)
    p["wuv"] = wukv[:, :, C_NOPE_DIM:].reshape(C_KV_RANK, C_W).astype(BF16)
    gq = jnp.concatenate([c_q_nope_norm[l], c_q_rope_norm[l]]) * (qd ** -0.5 * LOG2E)
    p["gq"] = _pad_cols(gq[None], LANES)
    p["gkn"] = _pad_cols(c_k_nope_norm[l][None], LANES)
    p["gkr"] = jnp.zeros((1, LANES), F32).at[0, ROPE_LANE:ROPE_LANE + C_ROPE_DIM].set(c_k_rope_norm[l])

    p["ona"] = out_norm_a[l][None]
    p["onc"] = out_norm_c[l][None]
    p["w_out"] = w_out[l].astype(BF16)
    p["xg"] = xattn_norm[l][None]
    p["w_mem_q"] = w_mem_q[l].astype(BF16)
    p["mqg"] = m_q_norm[l][None] * (M_HEAD_DIM ** -0.5)
    p["mem_g"] = mem_norm[l][None]
    p["w_mem_kv"] = w_mem_kv[l].astype(BF16)
    p["mkg"] = m_k_norm[l][None]
    p["w_mem_out"] = w_mem_out[l].astype(BF16)
    return p


def kernel(x, mem, positions, mix_norm, w_in, b_forget, a_q_norm, a_k_norm, b_v_norm, b_spatial_w, b_spatial_b, c_q_lat_norm, c_w_uq, c_kv_lat_norm, c_w_ukv, c_q_nope_norm, c_q_rope_norm, c_k_nope_norm, c_k_rope_norm, out_norm_a, out_norm_b, out_norm_c, w_out, xattn_norm, mem_norm, w_mem_q, w_mem_kv, m_q_norm, m_k_norm, w_mem_out, ffn_norm, ffn_w_gate, ffn_w_up, ffn_w_down, w_router, b_router, moe_w_gate, moe_w_up, moe_w_down):
    nb, seq, d = x.shape
    assert d == D_MODEL and seq % ROW_TILE == 0 and seq % ATTN_TILE == 0 and (nb * seq) % (MOE_SUB * MOE_CHUNK) == 0
    depth = w_in.shape[0]
    t = nb * seq
    x2d = x.reshape(t, d)
    cos, sin = _rope_tables(positions.reshape(t, 1).astype(F32))

    for l in range(depth):
        p = _layer_params(l, mix_norm, w_in, b_forget, a_q_norm, a_k_norm, b_v_norm, b_spatial_w,
                          b_spatial_b, c_q_lat_norm, c_w_uq, c_kv_lat_norm, c_w_ukv, c_q_nope_norm,
                          c_q_rope_norm, c_k_nope_norm, c_k_rope_norm, out_norm_a, out_norm_b,
                          out_norm_c, w_out, xattn_norm, mem_norm, w_mem_q, w_mem_kv, m_q_norm,
                          m_k_norm, w_mem_out)
        qa, ka, vta, bn, cq, ckv, misc = _inproj(x2d, seq, p)
        qc, kc, vtc = _mla_prep(cq, ckv, misc, cos, sin, seq, p)
        a = _attention(qa, ka, vta, unit=1, name="attn_fox")
        c = _attention(qc, kc, vtc, unit=CHUNK, name="attn_mla")
        km, vm = _mem_kv(mem, p)
        x2d = _outproj(x2d, a, bn, c, km, vm, seq, p)
        g = ffn_norm[l][None]
        if l % 2 == 0:
            m = l // 2
            ff = ffn_w_gate.shape[2]
            ff_pad = -(-ff // (2 * LANES)) * (2 * LANES)
            wg = _pad_cols(ffn_w_gate[m], ff_pad).astype(BF16)
            wu = _pad_cols(ffn_w_up[m], ff_pad).astype(BF16)
            wd = jnp.pad(ffn_w_down[m], ((0, ff_pad - ff), (0, 0))).astype(BF16)
            x2d = _ffn(x2d, g, wg, wu, wd)
        else:
            m = l // 2
            wr = _pad_cols(w_router[m], LANES)
            br = _pad_cols(b_router[m][None], LANES)
            x2d = _moe(x2d, g, wr, br, moe_w_gate[m].astype(BF16), moe_w_up[m].astype(BF16),
                       moe_w_down[m].astype(BF16))
    return x2d.reshape(nb, seq, d)
```

```python
import functools

import numpy as np
import jax
import jax.numpy as jnp
from jax import lax
from jax.experimental import pallas as pl
from jax.experimental.pallas import tpu as pltpu

F32 = jnp.float32
BF16 = jnp.bfloat16
HIGHEST = lax.Precision.HIGHEST

D_MODEL = 1024
CHUNK = 64
EPS = 1e-6
NEG_INF = -1e30
A_HEADS, A_HEAD_DIM = 4, 64
B_GROUPS, B_GROUP_DIM, B_WINDOW = 4, 64, 128
C_HEADS, C_NOPE_DIM, C_ROPE_DIM, C_V_DIM = 8, 64, 32, 64
C_Q_RANK, C_KV_RANK = 256, 128
ROPE_THETA = 10000.0
M_HEADS, M_HEAD_DIM = 4, 128
N_EXPERTS = 8
A_W = A_HEADS * A_HEAD_DIM
B_W = B_GROUPS * B_GROUP_DIM
C_W = C_HEADS * C_V_DIM
M_W = M_HEADS * M_HEAD_DIM

LANES = 128

SEG_Q, SEG_K, SEG_V, SEG_U, SEG_VB, SEG_CQ, SEG_CKV, SEG_MISC = 0, 256, 512, 768, 1024, 1280, 1536, 1664
IN_PAD_W = SEG_MISC + LANES
ROPE_LANE = C_NOPE_DIM
FORGET_LANE = 96
HEAD_SLAB = LANES
VT_ROWS = 80
LOG2E = float(np.log2(np.e))

ROW_TILE = 512
ATTN_TILE = 512
MOE_CHUNK = 1024
MOE_CAP = 320
MOE_SUB = 1
VMEM_LIMIT = 56 * 1024 * 1024


def _cparams(*sem):
    return pltpu.CompilerParams(dimension_semantics=sem, vmem_limit_bytes=VMEM_LIMIT)


def _full(shape):
    n = len(shape)
    return pl.BlockSpec(shape, lambda *_: (0,) * n)


def _rms(x):
    return x * lax.rsqrt(jnp.mean(x * x, axis=-1, keepdims=True) + EPS)


def _dot_split(v, exact, pieces, lhs_is_exact=False):
    total = None
    rem = v
    for n in range(pieces):
        part = rem.astype(BF16)
        if n + 1 < pieces:
            rem = rem - part.astype(F32)
        term = (jnp.dot(exact, part, preferred_element_type=F32) if lhs_is_exact
                else jnp.dot(part, exact, preferred_element_type=F32))
        total = term if total is None else total + term
    return total


def _lane_iota(n=LANES):
    return lax.broadcasted_iota(jnp.int32, (1, n), 1)


def _rope_table_kernel(pos_ref, inv_ref, sgn_ref, cos_ref, sin_ref):
    ang = pos_ref[...] * inv_ref[...]
    cos_ref[...] = jnp.cos(ang)
    sin_ref[...] = jnp.sin(ang) * sgn_ref[...]


def _rope_tables(pos_col):
    t = pos_col.shape[0]
    half = C_ROPE_DIM // 2
    inv = ROPE_THETA ** (-jnp.arange(half, dtype=F32) / half)
    inv_l = jnp.zeros((1, LANES), F32).at[0, ROPE_LANE:ROPE_LANE + C_ROPE_DIM].set(jnp.tile(inv, 2))
    sgn = np.zeros((1, LANES), np.float32)
    sgn[0, ROPE_LANE:ROPE_LANE + half] = -1.0
    sgn[0, ROPE_LANE + half:ROPE_LANE + C_ROPE_DIM] = 1.0
    tm = ROW_TILE
    return pl.pallas_call(
        _rope_table_kernel,
        grid=(t // tm,),
        in_specs=[pl.BlockSpec((tm, 1), lambda i: (i, 0)), _full((1, LANES)), _full((1, LANES))],
        out_specs=[pl.BlockSpec((tm, LANES), lambda i: (i, 0))] * 2,
        out_shape=[jax.ShapeDtypeStruct((t, LANES), F32)] * 2,
        compiler_params=_cparams("parallel"),
        name="rope_tables",
    )(pos_col, inv_l, jnp.asarray(sgn))


def _rotate(x, cos, sin_signed, lane):
    half = C_ROPE_DIM // 2
    partner = jnp.where(lane < ROPE_LANE + half,
                        pltpu.roll(x, LANES - half, 1), pltpu.roll(x, half, 1))
    return x * cos + partner * sin_signed


def _store_v_transposed(vt_ref, v, n_heads):
    tm = v.shape[0]
    v_t = v.T
    tail = jnp.where(lax.broadcasted_iota(jnp.int32, (VT_ROWS - C_V_DIM, tm), 0) == 0, 1.0, 0.0).astype(BF16)
    for hd in range(n_heads):
        vt_ref[0, hd, 0:C_V_DIM, :] = v_t[hd * C_V_DIM:(hd + 1) * C_V_DIM, :].astype(BF16)
        vt_ref[0, hd, C_V_DIM:VT_ROWS, :] = tail


def _gelu(x):
    return 0.5 * x * (1.0 + lax.erf(x * np.float32(1.0 / np.sqrt(2.0))))


def _inproj_kernel(x_ref, g_ref, w_ref, aq_ref, ak_ref, fb_ref, bvg_ref, ws_ref, bs_ref, onb_ref,
                   cqg_ref, ckvg_ref, gm_ref, tri_ref,
                   qa_ref, ka_ref, vt_ref, bn_ref, cq_ref, ckv_ref, misc_ref,
                   carry_ref, *, tiles_per_seq):
    i = pl.program_id(0)
    tm = x_ref.shape[0]
    h = _rms(x_ref[...]) * g_ref[...]
    proj = jnp.dot(h.astype(BF16), w_ref[...], preferred_element_type=F32)
    gm = gm_ref[...]

    def group_mean(v):
        return _dot_split(v, gm, 2)

    misc = proj[:, SEG_MISC:SEG_MISC + LANES]
    misc_ref[...] = misc
    z = misc + fb_ref[...]
    log_f = jnp.minimum(z, 0.0) - jnp.log1p(jnp.exp(-jnp.abs(z)))

    @pl.when(i % tiles_per_seq == 0)
    def _():
        carry_ref[...] = jnp.zeros_like(carry_ref)

    cum = _dot_split(log_f, tri_ref[...], 3, lhs_is_exact=True) + carry_ref[...]
    carry_ref[...] = cum[tm - 1:tm, :]
    f_hi = (cum * LOG2E).astype(BF16).astype(F32)
    f_rem = cum * LOG2E - f_hi
    f_mid = f_rem.astype(BF16).astype(F32)
    f_lo = f_rem - f_mid

    q = proj[:, SEG_Q:SEG_Q + A_W]
    qn = q * lax.rsqrt(group_mean(q * q) + EPS) * aq_ref[...]
    k = proj[:, SEG_K:SEG_K + A_W]
    kn = k * lax.rsqrt(group_mean(k * k) + EPS) * ak_ref[...]
    lane = _lane_iota()
    for hd in range(A_HEADS):
        pair = slice((hd // 2) * LANES, (hd // 2 + 1) * LANES)
        slab = slice(hd * HEAD_SLAB, (hd + 1) * HEAD_SLAB)
        data = (lane < A_HEAD_DIM) if hd % 2 == 0 else (lane >= A_HEAD_DIM)
        e0 = A_HEAD_DIM if hd % 2 == 0 else 0
        fl = FORGET_LANE + 8 * (hd // 2) + hd % 2
        ones = jnp.where((lane >= e0) & (lane < e0 + 3), 1.0, 0.0)
        qa_ref[:, slab] = jnp.where(data, qn[:, pair], ones).astype(BF16)
        bias = jnp.where(lane == e0, -f_hi[:, fl:fl + 1],
                         jnp.where(lane == e0 + 1, -f_mid[:, fl:fl + 1],
                                   jnp.where(lane == e0 + 2, -f_lo[:, fl:fl + 1], 0.0)))
        ka_ref[:, slab] = jnp.where(data, kn[:, pair], bias).astype(BF16)
    _store_v_transposed(vt_ref, proj[:, SEG_V:SEG_V + A_W], A_HEADS)

    u = _gelu(proj[:, SEG_U:SEG_U + B_W])
    v = _gelu(proj[:, SEG_VB:SEG_VB + B_W])
    dv = v - group_mean(v)
    vn = dv * lax.rsqrt(group_mean(dv * dv) + EPS) * bvg_ref[...]
    group = lax.broadcasted_iota(jnp.int32, (1, B_W), 1) // B_GROUP_DIM
    for w in range(tm // B_WINDOW):
        rows = slice(w * B_WINDOW, (w + 1) * B_WINDOW)
        y_all = jnp.dot(ws_ref[...], vn[rows].astype(BF16), preferred_element_type=F32)
        y = bs_ref[...]
        for g in range(B_GROUPS):
            y = y + jnp.where(group == g, y_all[g * B_WINDOW:(g + 1) * B_WINDOW], 0.0)
        b = u[rows] * y
        bn_ref[rows, :] = (_rms(b) * onb_ref[...]).astype(BF16)

    cq_ref[...] = (_rms(proj[:, SEG_CQ:SEG_CQ + C_Q_RANK]) * cqg_ref[...]).astype(BF16)
    ckv_ref[...] = (_rms(proj[:, SEG_CKV:SEG_CKV + C_KV_RANK]) * ckvg_ref[...]).astype(BF16)


def _inproj(x2d, seq, p):
    t = x2d.shape[0]
    tm = ROW_TILE
    tps = seq // tm
    nb = t // seq
    row = lambda w: pl.BlockSpec((tm, w), lambda i: (i, 0))
    qk_w = A_HEADS * HEAD_SLAB
    out_shape = [
        jax.ShapeDtypeStruct((t, qk_w), BF16), jax.ShapeDtypeStruct((t, qk_w), BF16),
        jax.ShapeDtypeStruct((nb, A_HEADS, VT_ROWS, seq), BF16),
        jax.ShapeDtypeStruct((t, B_W), BF16),
        jax.ShapeDtypeStruct((t, C_Q_RANK), BF16), jax.ShapeDtypeStruct((t, C_KV_RANK), BF16),
        jax.ShapeDtypeStruct((t, LANES), F32),
    ]
    out_specs = [row(qk_w), row(qk_w),
                 pl.BlockSpec((1, A_HEADS, VT_ROWS, tm), lambda i: (i // tps, 0, 0, i % tps)),
                 row(B_W), row(C_Q_RANK), row(C_KV_RANK), row(LANES)]
    consts = [p["mix_g"], p["w_in"], p["aq"], p["ak"], p["fb"], p["bvg"], p["ws"], p["bs"], p["onb"],
              p["cqg"], p["ckvg"], p["gm"], p["tri"]]
    return pl.pallas_call(
        functools.partial(_inproj_kernel, tiles_per_seq=tps),
        grid=(t // tm,),
        in_specs=[row(D_MODEL)] + [_full(c.shape) for c in consts],
        out_specs=out_specs,
        out_shape=out_shape,
        scratch_shapes=[pltpu.VMEM((1, LANES), F32)],
        compiler_params=_cparams("arbitrary"),
        name="in_proj",
    )(x2d, *consts)


def _mla_prep_kernel(cq_ref, ckv_ref, misc_ref, cos_ref, sin_ref, wuq_ref, wuk_ref, wuv_ref,
                     gq_ref, gkn_ref, gkr_ref, qc_ref, kc_ref, vt_ref):
    lane = _lane_iota()
    nope = lane < C_NOPE_DIM
    rope = (lane >= ROPE_LANE) & (lane < ROPE_LANE + C_ROPE_DIM)
    cos, sin = cos_ref[...], sin_ref[...]
    q = jnp.dot(cq_ref[...], wuq_ref[0], preferred_element_type=F32)
    q_partner = jnp.dot(cq_ref[...], wuq_ref[1], preferred_element_type=F32)
    q_cos = gq_ref[0:1, :] * cos
    q_sin = gq_ref[1:2, :] * sin
    kn = jnp.dot(ckv_ref[...], wuk_ref[...], preferred_element_type=F32)
    _store_v_transposed(vt_ref, jnp.dot(ckv_ref[...], wuv_ref[...], preferred_element_type=F32), C_HEADS)

    kr = jnp.where(rope, misc_ref[...], 0.0)
    kr = kr * lax.rsqrt(jnp.sum(kr * kr, axis=-1, keepdims=True) * (1.0 / C_ROPE_DIM) + EPS) * gkr_ref[...]
    kr = _rotate(kr, cos, sin, lane)

    for hd in range(C_HEADS):
        cols = slice(hd * HEAD_SLAB, (hd + 1) * HEAD_SLAB)
        qh = q[:, cols]
        sq = qh * qh
        r_n = lax.rsqrt(jnp.sum(jnp.where(nope, sq, 0.0), axis=-1, keepdims=True) * (1.0 / C_NOPE_DIM) + EPS)
        r_r = lax.rsqrt(jnp.sum(jnp.where(rope, sq, 0.0), axis=-1, keepdims=True) * (1.0 / C_ROPE_DIM) + EPS)
        qc_ref[:, cols] = (jnp.where(nope, r_n, r_r)
                           * (qh * q_cos + q_partner[:, cols] * q_sin)).astype(BF16)
        kh = kn[:, cols]
        r_k = lax.rsqrt(jnp.sum(kh * kh, axis=-1, keepdims=True) * (1.0 / C_NOPE_DIM) + EPS)
        kc_ref[:, cols] = (kh * r_k * gkn_ref[...] + kr).astype(BF16)


def _mla_prep(cq, ckv, misc, cos, sin, seq, p):
    t = cq.shape[0]
    tm = ROW_TILE
    tps = seq // tm
    row = lambda w: pl.BlockSpec((tm, w), lambda i: (i, 0))
    consts = [p["wuq"], p["wuk"], p["wuv"], p["gq"], p["gkn"], p["gkr"]]
    qk_w = C_HEADS * HEAD_SLAB
    return pl.pallas_call(
        _mla_prep_kernel,
        grid=(t // tm,),
        in_specs=[row(C_Q_RANK), row(C_KV_RANK), row(LANES), row(LANES), row(LANES)]
                 + [_full(c.shape) for c in consts],
        out_specs=[row(qk_w), row(qk_w),
                   pl.BlockSpec((1, C_HEADS, VT_ROWS, tm), lambda i: (i // tps, 0, 0, i % tps))],
        out_shape=[jax.ShapeDtypeStruct((t, qk_w), BF16), jax.ShapeDtypeStruct((t, qk_w), BF16),
                   jax.ShapeDtypeStruct((t // seq, C_HEADS, VT_ROWS, seq), BF16)],
        compiler_params=_cparams("parallel"),
        name="mla_prep",
    )(cq, ckv, misc, cos, sin, *consts)


def _attn_kernel(q_ref, k_ref, vt_ref, o_ref, s0_ref, s1_ref, mrun_ref, macc_ref, acc_ref, *, unit):
    i = pl.program_id(2)
    tq = q_ref.shape[0]
    tk = tq
    mrun_ref[...] = jnp.full(mrun_ref.shape, NEG_INF, F32)
    macc_ref[...] = jnp.full(macc_ref.shape, NEG_INF, F32)
    acc_ref[...] = jnp.zeros(acc_ref.shape, F32)

    def scores(j, s_ref, masked=False):
        k_start = pl.multiple_of(j * tk, tk)
        if masked:
            kpos = lax.broadcasted_iota(jnp.int32, (tk, tq), 0) // unit
            qpos = lax.broadcasted_iota(jnp.int32, (tk, tq), 1) // unit
            allowed = kpos <= qpos
        for hh in range(2):
            cols = slice(hh * HEAD_SLAB, (hh + 1) * HEAD_SLAB)
            s_t = lax.dot_general(k_ref[pl.ds(k_start, tk), cols], q_ref[:, cols],
                                  (((1,), (1,)), ((), ())), preferred_element_type=F32)
            if masked:
                s_t = jnp.where(allowed, s_t, NEG_INF)
            s_ref[hh] = s_t
            mrun_ref[hh] = jnp.maximum(mrun_ref[hh], jnp.max(s_t, axis=0, keepdims=True))

    def accumulate(j, s_ref):
        k_start = pl.multiple_of(j * tk, tk)
        for hh in range(2):
            m = mrun_ref[hh]
            p_t = jnp.exp2(s_ref[hh] - m).astype(BF16)
            pv = jnp.dot(vt_ref[0, hh, :, pl.ds(k_start, tk)], p_t, preferred_element_type=F32)
            acc_ref[hh] = jnp.exp2(macc_ref[hh] - m) * acc_ref[hh] + pv
            macc_ref[hh] = m

    scores(i, s1_ref, masked=True)
    n_pairs = i // 2

    def body(t, carry):
        j0 = 2 * t
        accumulate(jnp.where(t == 0, i, j0 - 1), s1_ref)
        scores(j0, s0_ref)
        accumulate(j0, s0_ref)
        scores(j0 + 1, s1_ref)
        return carry

    lax.fori_loop(0, n_pairs, body, 0)
    in_s1 = jnp.where(n_pairs == 0, i, 2 * n_pairs - 1)

    @pl.when(i % 2 == 1)
    def _():
        accumulate(in_s1, s1_ref)
        scores(i - 1, s0_ref)
        accumulate(i - 1, s0_ref)

    @pl.when(i % 2 == 0)
    def _():
        accumulate(in_s1, s1_ref)

    halves = [acc_ref[hh, 0:C_V_DIM, :] / acc_ref[hh, C_V_DIM:C_V_DIM + 1, :] for hh in range(2)]
    o_ref[...] = jnp.concatenate(halves, axis=0).T.astype(o_ref.dtype)


def _attention(q, k, vt, *, unit, name):
    t = q.shape[0]
    nb, n_heads, _, seq = vt.shape
    tq = ATTN_TILE
    nq = seq // tq
    return pl.pallas_call(
        functools.partial(_attn_kernel, unit=unit),
        grid=(nb, n_heads // 2, nq),
        in_specs=[pl.BlockSpec((tq, 2 * HEAD_SLAB), lambda b, p, i: (b * nq + i, p)),
                  pl.BlockSpec((seq, 2 * HEAD_SLAB), lambda b, p, i: (b, p)),
                  pl.BlockSpec((1, 2, VT_ROWS, seq), lambda b, p, i: (b, p, 0, 0))],
        out_specs=pl.BlockSpec((tq, 2 * C_V_DIM), lambda b, p, i: (b * nq + i, p)),
        out_shape=jax.ShapeDtypeStruct((t, n_heads * C_V_DIM), BF16),
        scratch_shapes=[pltpu.VMEM((2, tq, tq), F32), pltpu.VMEM((2, tq, tq), F32),
                        pltpu.VMEM((2, 1, tq), F32), pltpu.VMEM((2, 1, tq), F32),
                        pltpu.VMEM((2, VT_ROWS, tq), F32)],
        compiler_params=_cparams("parallel", "parallel", "arbitrary"),
        name=name,
    )(q, k, vt)


def _mem_kv_kernel(mem_ref, g_ref, w_ref, kg_ref, k_ref, v_ref):
    mn = (_rms(mem_ref[0]) * g_ref[...]).astype(BF16)
    kv = jnp.dot(mn, w_ref[...], preferred_element_type=F32)
    for hd in range(M_HEADS):
        cols = slice(hd * M_HEAD_DIM, (hd + 1) * M_HEAD_DIM)
        k_ref[0, :, cols] = (_rms(kv[:, cols]) * kg_ref[...]).astype(BF16)
    v_ref[0] = kv[:, M_W:].astype(BF16)


def _mem_kv(mem, p):
    nb, ml, _ = mem.shape
    consts = [p["mem_g"], p["w_mem_kv"], p["mkg"]]
    blk = pl.BlockSpec((1, ml, M_W), lambda b: (b, 0, 0))
    return pl.pallas_call(
        _mem_kv_kernel,
        grid=(nb,),
        in_specs=[pl.BlockSpec((1, ml, D_MODEL), lambda b: (b, 0, 0))] + [_full(c.shape) for c in consts],
        out_specs=[blk, blk],
        out_shape=[jax.ShapeDtypeStruct((nb, ml, M_W), BF16)] * 2,
        compiler_params=_cparams("parallel"),
        name="mem_kv",
    )(mem, *consts)


def _outproj_kernel(x_ref, a_ref, bn_ref, c_ref, ona_ref, onc_ref, wo_ref, xg_ref, wq_ref, mqg_ref,
                    km_ref, vm_ref, wmo_ref, o_ref):
    a_n = (_rms(a_ref[...].astype(F32)) * ona_ref[...]).astype(BF16)
    c_n = (_rms(c_ref[...].astype(F32)) * onc_ref[...]).astype(BF16)
    mix = jnp.concatenate([a_n, bn_ref[...], c_n], axis=-1)
    x1 = x_ref[...] + jnp.dot(mix, wo_ref[...], preferred_element_type=F32)

    h = (_rms(x1) * xg_ref[...]).astype(BF16)
    q = jnp.dot(h, wq_ref[...], preferred_element_type=F32)
    outs = []
    for hd in range(M_HEADS):
        cols = slice(hd * M_HEAD_DIM, (hd + 1) * M_HEAD_DIM)
        qh = (_rms(q[:, cols]) * mqg_ref[...]).astype(BF16)
        s = lax.dot_general(qh, km_ref[0, :, cols], (((1,), (1,)), ((), ())), preferred_element_type=F32)
        e = jnp.exp(s - jnp.max(s, axis=-1, keepdims=True))
        pr = e / jnp.sum(e, axis=-1, keepdims=True)
        outs.append(jnp.dot(pr.astype(BF16), vm_ref[0, :, cols], preferred_element_type=F32).astype(BF16))
    o_ref[...] = x1 + jnp.dot(jnp.concatenate(outs, axis=-1), wmo_ref[...], preferred_element_type=F32)


def _outproj(x2d, a, bn, c, km, vm, seq, p):
    t = x2d.shape[0]
    tm = ROW_TILE
    tps = seq // tm
    ml = km.shape[1]
    row = lambda w: pl.BlockSpec((tm, w), lambda i: (i, 0))
    memblk = pl.BlockSpec((1, ml, M_W), lambda i: (i // tps, 0, 0))
    c1 = [p["ona"], p["onc"], p["w_out"], p["xg"], p["w_mem_q"], p["mqg"]]
    return pl.pallas_call(
        _outproj_kernel,
        grid=(t // tm,),
        in_specs=[row(D_MODEL), row(A_W), row(B_W), row(C_W)] + [_full(c_.shape) for c_ in c1]
                 + [memblk, memblk, _full(p["w_mem_out"].shape)],
        out_specs=row(D_MODEL),
        out_shape=jax.ShapeDtypeStruct((t, D_MODEL), F32),
        compiler_params=_cparams("parallel"),
        name="out_proj_mem_attn",
    )(x2d, a, bn, c, *c1, km, vm, p["w_mem_out"])


def _silu(x):
    return x * jax.nn.sigmoid(x)


def _ffn_kernel(x_ref, g_ref, wg_ref, wu_ref, wd_ref, o_ref, *, n_chunks):
    x = x_ref[...]
    h = (_rms(x) * g_ref[...]).astype(BF16)
    fc = wg_ref.shape[1] // n_chunks
    acc = x
    for c in range(n_chunks):
        cols = slice(c * fc, (c + 1) * fc)
        act = _silu(jnp.dot(h, wg_ref[:, cols], preferred_element_type=F32)) * \
            jnp.dot(h, wu_ref[:, cols], preferred_element_type=F32)
        acc = acc + jnp.dot(act.astype(BF16), wd_ref[cols, :], preferred_element_type=F32)
    o_ref[...] = acc


def _ffn(x2d, g, wg, wu, wd):
    t = x2d.shape[0]
    tm = ROW_TILE
    row = pl.BlockSpec((tm, D_MODEL), lambda i: (i, 0))
    resident = lambda a: pl.BlockSpec(a.shape, lambda i: (0, 0), pipeline_mode=pl.Buffered(1))
    return pl.pallas_call(
        functools.partial(_ffn_kernel, n_chunks=2),
        grid=(t // tm,),
        in_specs=[row, _full(g.shape), resident(wg), resident(wu), resident(wd)],
        out_specs=row,
        out_shape=jax.ShapeDtypeStruct((t, D_MODEL), F32),
        compiler_params=_cparams("parallel"),
        name="ffn_dense",
    )(x2d, g, wg, wu, wd)


def _moe_kernel(x_hbm_ref, g_ref, wr_ref, br_ref, tri_ref, wg_ref, wu_ref, wd_ref, o_ref,
                h_ref, comb_ref, rank_ref, rank_t_ref, sem_ref):
    i = pl.program_id(0)
    e = pl.program_id(1)
    lane = _lane_iota()
    cap = MOE_CAP
    chunk = MOE_CHUNK
    n_sub = o_ref.shape[0] // chunk
    subs = [slice(sc * chunk, (sc + 1) * chunk) for sc in range(n_sub)]

    @pl.when(e == 0)
    def _():
        rows = o_ref.shape[0]
        load = pltpu.make_async_copy(x_hbm_ref.at[pl.ds(i * rows, rows)], o_ref, sem_ref)
        load.start()
        load.wait()
        for sc, rs in enumerate(subs):
            h = _rms(o_ref[rs, :]) * g_ref[...]
            h_hi = h.astype(BF16)
            h_ref[rs, :] = h_hi
            h_lo = (h - h_hi.astype(F32)).astype(BF16)
            logits = (jnp.dot(h_hi, wr_ref[0], preferred_element_type=F32)
                      + jnp.dot(h_lo, wr_ref[0], preferred_element_type=F32)
                      + jnp.dot(h_hi, wr_ref[1], preferred_element_type=F32)) + br_ref[...]
            logits = jnp.where(lane < N_EXPERTS, logits, -jnp.inf)
            v1 = jnp.max(logits, axis=-1, keepdims=True)
            i1 = jnp.min(jnp.where(logits == v1, lane, LANES), axis=-1, keepdims=True)
            rest = jnp.where(lane == i1, -jnp.inf, logits)
            v2 = jnp.max(rest, axis=-1, keepdims=True)
            i2 = jnp.min(jnp.where(rest == v2, lane, LANES), axis=-1, keepdims=True)
            e2 = jnp.exp(v2 - v1)
            g1 = 1.0 / (1.0 + e2)
            comb_ref[rs, :] = jnp.where(lane == i1, g1, 0.0) + jnp.where(lane == i2, e2 * g1, 0.0)
            hit = (lane == i1) | (lane == i2)
            before = jnp.dot(tri_ref[...], jnp.where(hit, 1.0, 0.0).astype(BF16), preferred_element_type=F32)
            rank = jnp.where(hit, before, -1.0)
            rank_ref[rs, :] = rank
            rank_t_ref[sc] = rank.T[0:N_EXPERTS, :]

    pick = lane == e
    rank_cols = [jnp.sum(jnp.where(pick, rank_ref[rs, :], 0.0), axis=-1, keepdims=True) for rs in subs]
    gate_cols = [jnp.sum(jnp.where(pick, comb_ref[rs, :], 0.0), axis=-1, keepdims=True) for rs in subs]
    rank_rows = [rank_t_ref[sc, pl.ds(e, 1), :] for sc in range(n_sub)]
    n_tokens = functools.reduce(jnp.maximum, [jnp.max(r) for r in rank_cols]).astype(jnp.int32) + 1
    slot_col = lax.broadcasted_iota(jnp.int32, (cap, 1), 0).astype(F32)
    slot_row = lax.broadcasted_iota(jnp.int32, (1, cap), 1).astype(F32)

    def block(b, carry):
        r0 = (b * cap).astype(F32)
        xe = []
        for sc, rs in enumerate(subs):
            sel = jnp.where(rank_rows[sc] - r0 == slot_col, 1.0, 0.0).astype(BF16)
            xe.append(jnp.dot(sel, h_ref[rs, :], preferred_element_type=F32).astype(BF16))
        xe = jnp.concatenate(xe, axis=0)
        act = _silu(jnp.dot(xe, wg_ref[0], preferred_element_type=F32)) * \
            jnp.dot(xe, wu_ref[0], preferred_element_type=F32)
        y = jnp.dot(act.astype(BF16), wd_ref[0], preferred_element_type=F32).astype(BF16)
        for sc, rs in enumerate(subs):
            sel_t = jnp.where(rank_cols[sc] - r0 == slot_row, 1.0, 0.0).astype(BF16)
            o_ref[rs, :] += gate_cols[sc] * jnp.dot(sel_t, y[sc * cap:(sc + 1) * cap, :],
                                                    preferred_element_type=F32)
        return carry

    lax.fori_loop(0, (n_tokens + cap - 1) // cap, block, 0)


def _moe(x2d, g, wr, br, wg, wu, wd):
    t = x2d.shape[0]
    chunk = MOE_CHUNK
    tm = MOE_SUB * chunk
    ff = wg.shape[2]
    tri = jnp.asarray(np.tril(np.ones((chunk, chunk), np.float32), -1)).astype(BF16)
    return pl.pallas_call(
        _moe_kernel,
        grid=(t // tm, N_EXPERTS),
        in_specs=[pl.BlockSpec(memory_space=pl.ANY),
                  _full(g.shape), _full(wr.shape), _full(br.shape),
                  pl.BlockSpec((chunk, chunk), lambda i, e: (0, 0), pipeline_mode=pl.Buffered(1)),
                  pl.BlockSpec((1, D_MODEL, ff), lambda i, e: (e, 0, 0)),
                  pl.BlockSpec((1, D_MODEL, ff), lambda i, e: (e, 0, 0)),
                  pl.BlockSpec((1, ff, D_MODEL), lambda i, e: (e, 0, 0))],
        out_specs=pl.BlockSpec((tm, D_MODEL), lambda i, e: (i, 0)),
        out_shape=jax.ShapeDtypeStruct((t, D_MODEL), F32),
        scratch_shapes=[pltpu.VMEM((tm, D_MODEL), BF16), pltpu.VMEM((tm, LANES), F32),
                        pltpu.VMEM((tm, LANES), F32), pltpu.VMEM((MOE_SUB, N_EXPERTS, chunk), F32),
                        pltpu.SemaphoreType.DMA(())],
        compiler_params=_cparams("parallel", "arbitrary"),
        name="moe_routed",
    )(x2d, g, wr, br, tri, wg, wu, wd)


def _pad_cols(w, width):
    return jnp.pad(w, ((0, 0), (0, width - w.shape[1])))


def _layer_params(l, mix_norm, w_in, b_forget, a_q_norm, a_k_norm, b_v_norm, b_spatial_w, b_spatial_b,
                  c_q_lat_norm, c_w_uq, c_kv_lat_norm, c_w_ukv, c_q_nope_norm, c_q_rope_norm,
                  c_k_nope_norm, c_k_rope_norm, out_norm_a, out_norm_b, out_norm_c, w_out,
                  xattn_norm, mem_norm, w_mem_q, w_mem_kv, m_q_norm, m_k_norm, w_mem_out):
    p = {}
    o = np.cumsum((0, A_W, A_W, A_W, A_HEADS, B_W, B_W, C_Q_RANK, C_KV_RANK, C_ROPE_DIM))
    w = w_in[l]
    seg = lambda n: w[:, o[n]:o[n + 1]]
    fa = seg(3)
    misc = jnp.zeros((D_MODEL, LANES), F32)
    misc = misc.at[:, ROPE_LANE:ROPE_LANE + C_ROPE_DIM].set(seg(8))
    fb = jnp.zeros((1, LANES), F32)
    for hd in range(A_HEADS):
        ln = FORGET_LANE + 8 * (hd // 2) + hd % 2
        misc = misc.at[:, ln].set(fa[:, hd])
        fb = fb.at[0, ln].set(b_forget[l, hd])
    p["w_in"] = jnp.concatenate([seg(0), seg(1), seg(2), seg(4), seg(5), seg(6), seg(7), misc], axis=1).astype(BF16)
    p["fb"] = fb
    p["mix_g"] = mix_norm[l][None]
    p["aq"] = jnp.tile(a_q_norm[l], A_HEADS)[None] * (A_HEAD_DIM ** -0.5 * LOG2E)
    p["ak"] = jnp.tile(a_k_norm[l], A_HEADS)[None]
    p["bvg"] = b_v_norm[l][None]
    pos = np.arange(B_WINDOW)
    mask = (pos[None, :] // CHUNK) <= (pos[:, None] // CHUNK)
    p["ws"] = jnp.where(mask[None], b_spatial_w[l], 0.0).reshape(B_GROUPS * B_WINDOW, B_WINDOW).astype(BF16)
    p["bs"] = jnp.repeat(b_spatial_b[l].T, B_GROUP_DIM, axis=1)
    p["onb"] = out_norm_b[l][None]
    p["cqg"] = c_q_lat_norm[l][None]
    p["ckvg"] = c_kv_lat_norm[l][None]
    gidx = np.arange(A_W) // A_HEAD_DIM
    p["gm"] = jnp.asarray((gidx[:, None] == gidx[None, :]).astype(np.float32) / A_HEAD_DIM).astype(BF16)
    p["tri"] = jnp.asarray(np.tril(np.ones((ROW_TILE, ROW_TILE), np.float32))).astype(BF16)

    qd = C_NOPE_DIM + C_ROPE_DIM
    half = C_ROPE_DIM // 2
    wq = c_w_uq[l]
    wq_partner = jnp.concatenate([jnp.zeros_like(wq[:, :, :C_NOPE_DIM]), wq[:, :, C_NOPE_DIM + half:],
                                  wq[:, :, C_NOPE_DIM:C_NOPE_DIM + half]], axis=-1)
    p["wuq"] = jnp.pad(jnp.stack([wq, wq_partner]), ((0, 0), (0, 0), (0, 0), (0, HEAD_SLAB - qd))
                       ).reshape(2, C_Q_RANK, -1).astype(BF16)
    wukv = c_w_ukv[l]
    p["wuk"] = jnp.pad(wukv[:, :, :C_NOPE_DIM], ((0, 0), (0, 0), (0, HEAD_SLAB - C_NOPE_DIM))
                       ).reshape(C_KV_RANK, -1).astype(BF16)
    p["wuv"] = wukv[:, :, C_NOPE_DIM:].reshape(C_KV_RANK, C_W).astype(BF16)
    gq = jnp.concatenate([c_q_nope_norm[l], c_q_rope_norm[l]])
    gq_partner = jnp.concatenate([jnp.zeros_like(c_q_nope_norm[l]), c_q_rope_norm[l][half:],
                                  c_q_rope_norm[l][:half]])
    p["gq"] = _pad_cols(jnp.stack([gq, gq_partner]) * (qd ** -0.5 * LOG2E), LANES)
    p["gkn"] = _pad_cols(c_k_nope_norm[l][None], LANES)
    p["gkr"] = jnp.zeros((1, LANES), F32).at[0, ROPE_LANE:ROPE_LANE + C_ROPE_DIM].set(c_k_rope_norm[l])

    p["ona"] = out_norm_a[l][None]
    p["onc"] = out_norm_c[l][None]
    p["w_out"] = w_out[l].astype(BF16)
    p["xg"] = xattn_norm[l][None]
    p["w_mem_q"] = w_mem_q[l].astype(BF16)
    p["mqg"] = m_q_norm[l][None] * (M_HEAD_DIM ** -0.5)
    p["mem_g"] = mem_norm[l][None]
    p["w_mem_kv"] = w_mem_kv[l].astype(BF16)
    p["mkg"] = m_k_norm[l][None]
    p["w_mem_out"] = w_mem_out[l].astype(BF16)
    return p


def kernel(x, mem, positions, mix_norm, w_in, b_forget, a_q_norm, a_k_norm, b_v_norm, b_spatial_w, b_spatial_b, c_q_lat_norm, c_w_uq, c_kv_lat_norm, c_w_ukv, c_q_nope_norm, c_q_rope_norm, c_k_nope_norm, c_k_rope_norm, out_norm_a, out_norm_b, out_norm_c, w_out, xattn_norm, mem_norm, w_mem_q, w_mem_kv, m_q_norm, m_k_norm, w_mem_out, ffn_norm, ffn_w_gate, ffn_w_up, ffn_w_down, w_router, b_router, moe_w_gate, moe_w_up, moe_w_down):
    nb, seq, d = x.shape
    assert d == D_MODEL and seq % ROW_TILE == 0 and seq % ATTN_TILE == 0 and (nb * seq) % (MOE_SUB * MOE_CHUNK) == 0
    depth = w_in.shape[0]
    t = nb * seq
    x2d = x.reshape(t, d)
    cos, sin = _rope_tables(positions.reshape(t, 1).astype(F32))

    for l in range(depth):
        p = _layer_params(l, mix_norm, w_in, b_forget, a_q_norm, a_k_norm, b_v_norm, b_spatial_w,
                          b_spatial_b, c_q_lat_norm, c_w_uq, c_kv_lat_norm, c_w_ukv, c_q_nope_norm,
                          c_q_rope_norm, c_k_nope_norm, c_k_rope_norm, out_norm_a, out_norm_b,
                          out_norm_c, w_out, xattn_norm, mem_norm, w_mem_q, w_mem_kv, m_q_norm,
                          m_k_norm, w_mem_out)
        qa, ka, vta, bn, cq, ckv, misc = _inproj(x2d, seq, p)
        qc, kc, vtc = _mla_prep(cq, ckv, misc, cos, sin, seq, p)
        a = _attention(qa, ka, vta, unit=1, name="attn_fox")
        c = _attention(qc, kc, vtc, unit=CHUNK, name="attn_mla")
        km, vm = _mem_kv(mem, p)
        x2d = _outproj(x2d, a, bn, c, km, vm, seq, p)
        g = ffn_norm[l][None]
        if l % 2 == 0:
            m = l // 2
            ff = ffn_w_gate.shape[2]
            ff_pad = -(-ff // (2 * LANES)) * (2 * LANES)
            wg = _pad_cols(ffn_w_gate[m], ff_pad).astype(BF16)
            wu = _pad_cols(ffn_w_up[m], ff_pad).astype(BF16)
            wd = jnp.pad(ffn_w_down[m], ((0, ff_pad - ff), (0, 0))).astype(BF16)
            x2d = _ffn(x2d, g, wg, wu, wd)
        else:
            m = l // 2
            wr = _pad_cols(w_router[m], LANES)
            wr_hi = wr.astype(BF16)
            wr = jnp.stack([wr_hi, (wr - wr_hi.astype(F32)).astype(BF16)])
            br = _pad_cols(b_router[m][None], LANES)
            x2d = _moe(x2d, g, wr, br, moe_w_gate[m].astype(BF16), moe_w_up[m].astype(BF16),
                       moe_w_down[m].astype(BF16))
    return x2d.reshape(nb, seq, d)
```

```python
import functools

import numpy as np
import jax
import jax.numpy as jnp
from jax import lax
from jax.experimental import pallas as pl
from jax.experimental.pallas import tpu as pltpu

F32 = jnp.float32
BF16 = jnp.bfloat16
HIGHEST = lax.Precision.HIGHEST

D_MODEL = 1024
CHUNK = 64
EPS = 1e-6
NEG_INF = -1e30
A_HEADS, A_HEAD_DIM = 4, 64
B_GROUPS, B_GROUP_DIM, B_WINDOW = 4, 64, 128
C_HEADS, C_NOPE_DIM, C_ROPE_DIM, C_V_DIM = 8, 64, 32, 64
C_Q_RANK, C_KV_RANK = 256, 128
ROPE_THETA = 10000.0
M_HEADS, M_HEAD_DIM = 4, 128
N_EXPERTS = 8
A_W = A_HEADS * A_HEAD_DIM
B_W = B_GROUPS * B_GROUP_DIM
C_W = C_HEADS * C_V_DIM
M_W = M_HEADS * M_HEAD_DIM

LANES = 128

SEG_Q, SEG_K, SEG_V, SEG_U, SEG_VB, SEG_CQ, SEG_CKV, SEG_MISC = 0, 256, 512, 768, 1024, 1280, 1536, 1664
IN_PAD_W = SEG_MISC + LANES
ROPE_LANE = C_NOPE_DIM
FORGET_LANE = 96
HEAD_SLAB = LANES
VT_ROWS = 80
LOG2E = float(np.log2(np.e))

ROW_TILE = 512
ATTN_TILE = 512
MOE_CHUNK = 1024
MOE_CAP = 320
MOE_SUB = 1
VMEM_LIMIT = 56 * 1024 * 1024


def _cparams(*sem):
    return pltpu.CompilerParams(dimension_semantics=sem, vmem_limit_bytes=VMEM_LIMIT)


def _full(shape):
    n = len(shape)
    return pl.BlockSpec(shape, lambda *_: (0,) * n)


def _rms(x):
    return x * lax.rsqrt(jnp.mean(x * x, axis=-1, keepdims=True) + EPS)


def _dot_split(v, exact, pieces, lhs_is_exact=False):
    total = None
    rem = v
    for n in range(pieces):
        part = rem.astype(BF16)
        if n + 1 < pieces:
            rem = rem - part.astype(F32)
        term = (jnp.dot(exact, part, preferred_element_type=F32) if lhs_is_exact
                else jnp.dot(part, exact, preferred_element_type=F32))
        total = term if total is None else total + term
    return total


def _lane_iota(n=LANES):
    return lax.broadcasted_iota(jnp.int32, (1, n), 1)


def _rope_table_kernel(pos_ref, inv_ref, sgn_ref, cos_ref, sin_ref):
    ang = pos_ref[...] * inv_ref[...]
    cos_ref[...] = jnp.cos(ang)
    sin_ref[...] = jnp.sin(ang) * sgn_ref[...]


def _rope_tables(pos_col):
    t = pos_col.shape[0]
    half = C_ROPE_DIM // 2
    inv = ROPE_THETA ** (-jnp.arange(half, dtype=F32) / half)
    inv_l = jnp.zeros((1, LANES), F32).at[0, ROPE_LANE:ROPE_LANE + C_ROPE_DIM].set(jnp.tile(inv, 2))
    sgn = np.zeros((1, LANES), np.float32)
    sgn[0, ROPE_LANE:ROPE_LANE + half] = -1.0
    sgn[0, ROPE_LANE + half:ROPE_LANE + C_ROPE_DIM] = 1.0
    tm = ROW_TILE
    return pl.pallas_call(
        _rope_table_kernel,
        grid=(t // tm,),
        in_specs=[pl.BlockSpec((tm, 1), lambda i: (i, 0)), _full((1, LANES)), _full((1, LANES))],
        out_specs=[pl.BlockSpec((tm, LANES), lambda i: (i, 0))] * 2,
        out_shape=[jax.ShapeDtypeStruct((t, LANES), F32)] * 2,
        compiler_params=_cparams("parallel"),
        name="rope_tables",
    )(pos_col, inv_l, jnp.asarray(sgn))


def _rotate(x, cos, sin_signed, lane):
    half = C_ROPE_DIM // 2
    partner = jnp.where(lane < ROPE_LANE + half,
                        pltpu.roll(x, LANES - half, 1), pltpu.roll(x, half, 1))
    return x * cos + partner * sin_signed


def _store_v_transposed(vt_ref, v, n_heads):
    tm = v.shape[0]
    v_t = v.T
    tail = jnp.where(lax.broadcasted_iota(jnp.int32, (VT_ROWS - C_V_DIM, tm), 0) == 0, 1.0, 0.0).astype(BF16)
    for hd in range(n_heads):
        vt_ref[0, hd, 0:C_V_DIM, :] = v_t[hd * C_V_DIM:(hd + 1) * C_V_DIM, :].astype(BF16)
        vt_ref[0, hd, C_V_DIM:VT_ROWS, :] = tail


def _gelu(x):
    return 0.5 * x * (1.0 + lax.erf(x * np.float32(1.0 / np.sqrt(2.0))))


def _inproj_kernel(x_ref, g_ref, w_ref, aq_ref, ak_ref, fb_ref, bvg_ref, ws_ref, bs_ref, onb_ref,
                   cqg_ref, ckvg_ref, gm_ref, tri_ref,
                   qa_ref, ka_ref, vt_ref, bn_ref, cq_ref, ckv_ref, misc_ref,
                   carry_ref, *, tiles_per_seq):
    i = pl.program_id(0)
    tm = x_ref.shape[0]
    h = _rms(x_ref[...]) * g_ref[...]
    proj = jnp.dot(h.astype(BF16), w_ref[...], preferred_element_type=F32)
    gm = gm_ref[...]

    def group_mean(v):
        return _dot_split(v, gm, 2)

    misc = proj[:, SEG_MISC:SEG_MISC + LANES]
    misc_ref[...] = misc
    z = misc + fb_ref[...]
    log_f = jnp.minimum(z, 0.0) - jnp.log1p(jnp.exp(-jnp.abs(z)))

    @pl.when(i % tiles_per_seq == 0)
    def _():
        carry_ref[...] = jnp.zeros_like(carry_ref)

    cum = _dot_split(log_f, tri_ref[...], 3, lhs_is_exact=True) + carry_ref[...]
    carry_ref[...] = cum[tm - 1:tm, :]
    f_hi = (cum * LOG2E).astype(BF16).astype(F32)
    f_rem = cum * LOG2E - f_hi
    f_mid = f_rem.astype(BF16).astype(F32)
    f_lo = f_rem - f_mid

    q = proj[:, SEG_Q:SEG_Q + A_W]
    qn = q * lax.rsqrt(group_mean(q * q) + EPS) * aq_ref[...]
    k = proj[:, SEG_K:SEG_K + A_W]
    kn = k * lax.rsqrt(group_mean(k * k) + EPS) * ak_ref[...]
    lane = _lane_iota()
    for hd in range(A_HEADS):
        pair = slice((hd // 2) * LANES, (hd // 2 + 1) * LANES)
        slab = slice(hd * HEAD_SLAB, (hd + 1) * HEAD_SLAB)
        data = (lane < A_HEAD_DIM) if hd % 2 == 0 else (lane >= A_HEAD_DIM)
        e0 = A_HEAD_DIM if hd % 2 == 0 else 0
        fl = FORGET_LANE + 8 * (hd // 2) + hd % 2
        ones = jnp.where((lane >= e0) & (lane < e0 + 3), 1.0, 0.0)
        qa_ref[:, slab] = jnp.where(data, qn[:, pair], ones).astype(BF16)
        bias = jnp.where(lane == e0, -f_hi[:, fl:fl + 1],
                         jnp.where(lane == e0 + 1, -f_mid[:, fl:fl + 1],
                                   jnp.where(lane == e0 + 2, -f_lo[:, fl:fl + 1], 0.0)))
        ka_ref[:, slab] = jnp.where(data, kn[:, pair], bias).astype(BF16)
    _store_v_transposed(vt_ref, proj[:, SEG_V:SEG_V + A_W], A_HEADS)

    u = _gelu(proj[:, SEG_U:SEG_U + B_W])
    v = _gelu(proj[:, SEG_VB:SEG_VB + B_W])
    dv = v - group_mean(v)
    vn = dv * lax.rsqrt(group_mean(dv * dv) + EPS) * bvg_ref[...]
    group = lax.broadcasted_iota(jnp.int32, (1, B_W), 1) // B_GROUP_DIM
    for w in range(tm // B_WINDOW):
        rows = slice(w * B_WINDOW, (w + 1) * B_WINDOW)
        y_all = jnp.dot(ws_ref[...], vn[rows].astype(BF16), preferred_element_type=F32)
        y = bs_ref[...]
        for g in range(B_GROUPS):
            y = y + jnp.where(group == g, y_all[g * B_WINDOW:(g + 1) * B_WINDOW], 0.0)
        b = u[rows] * y
        bn_ref[rows, :] = (_rms(b) * onb_ref[...]).astype(BF16)

    cq_ref[...] = (_rms(proj[:, SEG_CQ:SEG_CQ + C_Q_RANK]) * cqg_ref[...]).astype(BF16)
    ckv_ref[...] = (_rms(proj[:, SEG_CKV:SEG_CKV + C_KV_RANK]) * ckvg_ref[...]).astype(BF16)


def _inproj(x2d, seq, p):
    t = x2d.shape[0]
    tm = ROW_TILE
    tps = seq // tm
    nb = t // seq
    row = lambda w: pl.BlockSpec((tm, w), lambda i: (i, 0))
    qk_w = A_HEADS * HEAD_SLAB
    out_shape = [
        jax.ShapeDtypeStruct((t, qk_w), BF16), jax.ShapeDtypeStruct((t, qk_w), BF16),
        jax.ShapeDtypeStruct((nb, A_HEADS, VT_ROWS, seq), BF16),
        jax.ShapeDtypeStruct((t, B_W), BF16),
        jax.ShapeDtypeStruct((t, C_Q_RANK), BF16), jax.ShapeDtypeStruct((t, C_KV_RANK), BF16),
        jax.ShapeDtypeStruct((t, LANES), F32),
    ]
    out_specs = [row(qk_w), row(qk_w),
                 pl.BlockSpec((1, A_HEADS, VT_ROWS, tm), lambda i: (i // tps, 0, 0, i % tps)),
                 row(B_W), row(C_Q_RANK), row(C_KV_RANK), row(LANES)]
    consts = [p["mix_g"], p["w_in"], p["aq"], p["ak"], p["fb"], p["bvg"], p["ws"], p["bs"], p["onb"],
              p["cqg"], p["ckvg"], p["gm"], p["tri"]]
    return pl.pallas_call(
        functools.partial(_inproj_kernel, tiles_per_seq=tps),
        grid=(t // tm,),
        in_specs=[row(D_MODEL)] + [_full(c.shape) for c in consts],
        out_specs=out_specs,
        out_shape=out_shape,
        scratch_shapes=[pltpu.VMEM((1, LANES), F32)],
        compiler_params=_cparams("arbitrary"),
        name="in_proj",
    )(x2d, *consts)


def _mla_prep_kernel(cq_ref, ckv_ref, misc_ref, cos_ref, sin_ref, wuq_ref, wuk_ref, wuv_ref,
                     gq_ref, gkn_ref, gkr_ref, qc_ref, kc_ref, vt_ref):
    lane = _lane_iota()
    nope = lane < C_NOPE_DIM
    rope = (lane >= ROPE_LANE) & (lane < ROPE_LANE + C_ROPE_DIM)
    cos, sin = cos_ref[...], sin_ref[...]
    q = jnp.dot(cq_ref[...], wuq_ref[0], preferred_element_type=F32)
    q_partner = jnp.dot(cq_ref[...], wuq_ref[1], preferred_element_type=F32)
    q_cos = gq_ref[0:1, :] * cos
    q_sin = gq_ref[1:2, :] * sin
    kn = jnp.dot(ckv_ref[...], wuk_ref[...], preferred_element_type=F32)
    _store_v_transposed(vt_ref, jnp.dot(ckv_ref[...], wuv_ref[...], preferred_element_type=F32), C_HEADS)

    kr = jnp.where(rope, misc_ref[...], 0.0)
    kr = kr * lax.rsqrt(jnp.sum(kr * kr, axis=-1, keepdims=True) * (1.0 / C_ROPE_DIM) + EPS) * gkr_ref[...]
    kr = _rotate(kr, cos, sin, lane)

    for hd in range(C_HEADS):
        cols = slice(hd * HEAD_SLAB, (hd + 1) * HEAD_SLAB)
        qh = q[:, cols]
        sq = qh * qh
        r_n = lax.rsqrt(jnp.sum(jnp.where(nope, sq, 0.0), axis=-1, keepdims=True) * (1.0 / C_NOPE_DIM) + EPS)
        r_r = lax.rsqrt(jnp.sum(jnp.where(rope, sq, 0.0), axis=-1, keepdims=True) * (1.0 / C_ROPE_DIM) + EPS)
        qc_ref[:, cols] = (jnp.where(nope, r_n, r_r)
                           * (qh * q_cos + q_partner[:, cols] * q_sin)).astype(BF16)
        kh = kn[:, cols]
        r_k = lax.rsqrt(jnp.sum(kh * kh, axis=-1, keepdims=True) * (1.0 / C_NOPE_DIM) + EPS)
        kc_ref[:, cols] = (kh * r_k * gkn_ref[...] + kr).astype(BF16)


def _mla_prep(cq, ckv, misc, cos, sin, seq, p):
    t = cq.shape[0]
    tm = ROW_TILE
    tps = seq // tm
    row = lambda w: pl.BlockSpec((tm, w), lambda i: (i, 0))
    consts = [p["wuq"], p["wuk"], p["wuv"], p["gq"], p["gkn"], p["gkr"]]
    qk_w = C_HEADS * HEAD_SLAB
    return pl.pallas_call(
        _mla_prep_kernel,
        grid=(t // tm,),
        in_specs=[row(C_Q_RANK), row(C_KV_RANK), row(LANES), row(LANES), row(LANES)]
                 + [_full(c.shape) for c in consts],
        out_specs=[row(qk_w), row(qk_w),
                   pl.BlockSpec((1, C_HEADS, VT_ROWS, tm), lambda i: (i // tps, 0, 0, i % tps))],
        out_shape=[jax.ShapeDtypeStruct((t, qk_w), BF16), jax.ShapeDtypeStruct((t, qk_w), BF16),
                   jax.ShapeDtypeStruct((t // seq, C_HEADS, VT_ROWS, seq), BF16)],
        compiler_params=_cparams("parallel"),
        name="mla_prep",
    )(cq, ckv, misc, cos, sin, *consts)


def _attn_items(nq):
    items = [(i, j) for i in range(nq) for j in range(i)] + [(i, i) for i in range(nq)]
    return np.array(items, np.int32).T


def _attn_kernel(items_ref, q_ref, k_ref, vt_ref, mask_ref, o_ref, s0_ref, s1_ref, p0_ref, p1_ref,
                 mp0_ref, mp1_ref, mrun_ref, macc_ref, acc_ref, *, n_items):
    tk, tq = mask_ref.shape
    nq = q_ref.shape[0] // tq
    s_bufs, p_bufs, mp_bufs = (s0_ref, s1_ref), (p0_ref, p1_ref), (mp0_ref, mp1_ref)
    mrun_ref[...] = jnp.full(mrun_ref.shape, NEG_INF, F32)
    macc_ref[...] = jnp.full(macc_ref.shape, NEG_INF, F32)
    acc_ref[...] = jnp.zeros(acc_ref.shape, F32)

    def scores(it, buf, masked):
        qi = items_ref[0, it]
        q_start = pl.multiple_of(qi * tq, tq)
        k_start = pl.multiple_of(items_ref[1, it] * tk, tk)
        for hh in range(2):
            cols = slice(hh * HEAD_SLAB, (hh + 1) * HEAD_SLAB)
            s_t = lax.dot_general(k_ref[pl.ds(k_start, tk), cols], q_ref[pl.ds(q_start, tq), cols],
                                  (((1,), (1,)), ((), ())), preferred_element_type=F32)
            if masked:
                s_t = s_t + mask_ref[...]
            s_bufs[buf][hh] = s_t
            mrun_ref[qi, hh] = jnp.maximum(mrun_ref[qi, hh], jnp.max(s_t, axis=0, keepdims=True))

    def exponentiate(it, buf):
        qi = items_ref[0, it]
        for hh in range(2):
            m = mrun_ref[qi, hh]
            p_bufs[buf][hh] = jnp.exp2(s_bufs[buf][hh] - m).astype(BF16)
            mp_bufs[buf][hh] = m

    def accumulate(it, buf):
        qi = items_ref[0, it]
        k_start = pl.multiple_of(items_ref[1, it] * tk, tk)
        for hh in range(2):
            m = mp_bufs[buf][hh]
            pv = jnp.dot(vt_ref[0, hh, :, pl.ds(k_start, tk)], p_bufs[buf][hh],
                         preferred_element_type=F32)
            acc_ref[qi, hh] = jnp.exp2(macc_ref[qi, hh] - m) * acc_ref[qi, hh] + pv
            macc_ref[qi, hh] = m

    def beat(it, par, masked):
        if not isinstance(it, int) or 2 <= it <= n_items + 1:
            accumulate(it - 2, par)
        if not isinstance(it, int) or 1 <= it <= n_items:
            exponentiate(it - 1, 1 - par)
        if not isinstance(it, int) or it < n_items:
            scores(it, par, masked)

    def run_beats(lo, hi, masked):
        lo = max(lo, 0)
        if lo < hi and (lo % 2 == 1 or lo < 2):
            for it in range(lo, min(hi, lo + 2 - lo % 2)):
                beat(it, it % 2, masked)
            lo = min(hi, lo + 2 - lo % 2)
        n_pairs = (hi - lo) // 2
        if n_pairs > 0:
            def pair(t, carry, lo=lo):
                beat(lo + 2 * t, 0, masked)
                beat(lo + 2 * t + 1, 1, masked)
                return carry
            lax.fori_loop(0, n_pairs, pair, 0)
        for it in range(lo + 2 * n_pairs, hi):
            beat(it, it % 2, masked)

    n_full = n_items - nq
    run_beats(0, n_full, False)
    run_beats(n_full, n_items, True)
    for it in (n_items, n_items + 1):
        beat(it, it % 2, True)

    for qi in range(nq):
        halves = [acc_ref[qi, hh, 0:C_V_DIM, :] / acc_ref[qi, hh, C_V_DIM:C_V_DIM + 1, :] for hh in range(2)]
        o_ref[qi * tq:(qi + 1) * tq, :] = jnp.concatenate(halves, axis=0).T.astype(o_ref.dtype)


def _attention(q, k, vt, *, unit, name):
    t = q.shape[0]
    nb, n_heads, _, seq = vt.shape
    tq = ATTN_TILE
    nq = seq // tq
    items = _attn_items(nq)
    pos = np.arange(tq)
    diag_mask = np.where((pos[:, None] // unit) <= (pos[None, :] // unit), 0.0, NEG_INF)
    mask = jnp.asarray(diag_mask.astype(np.float32))
    seq_blk = lambda w: pl.BlockSpec((seq, w), lambda b, p, items_ref: (b, p))
    grid_spec = pltpu.PrefetchScalarGridSpec(
        num_scalar_prefetch=1,
        grid=(nb, n_heads // 2),
        in_specs=[seq_blk(2 * HEAD_SLAB), seq_blk(2 * HEAD_SLAB),
                  pl.BlockSpec((1, 2, VT_ROWS, seq), lambda b, p, items_ref: (b, p, 0, 0)),
                  pl.BlockSpec((tq, tq), lambda b, p, items_ref: (0, 0), pipeline_mode=pl.Buffered(1))],
        out_specs=seq_blk(2 * C_V_DIM),
        scratch_shapes=[pltpu.VMEM((2, tq, tq), F32), pltpu.VMEM((2, tq, tq), F32),
                        pltpu.VMEM((2, tq, tq), BF16), pltpu.VMEM((2, tq, tq), BF16),
                        pltpu.VMEM((2, 1, tq), F32), pltpu.VMEM((2, 1, tq), F32),
                        pltpu.VMEM((nq, 2, 1, tq), F32), pltpu.VMEM((nq, 2, 1, tq), F32),
                        pltpu.VMEM((nq, 2, VT_ROWS, tq), F32)])
    return pl.pallas_call(
        functools.partial(_attn_kernel, n_items=items.shape[1]),
        grid_spec=grid_spec,
        out_shape=jax.ShapeDtypeStruct((t, n_heads * C_V_DIM), BF16),
        compiler_params=_cparams("parallel", "parallel"),
        name=name,
    )(jnp.asarray(items), q, k, vt, mask)


def _mem_kv_kernel(mem_ref, g_ref, w_ref, kg_ref, k_ref, v_ref):
    mn = (_rms(mem_ref[0]) * g_ref[...]).astype(BF16)
    kv = jnp.dot(mn, w_ref[...], preferred_element_type=F32)
    for hd in range(M_HEADS):
        cols = slice(hd * M_HEAD_DIM, (hd + 1) * M_HEAD_DIM)
        k_ref[0, :, cols] = (_rms(kv[:, cols]) * kg_ref[...]).astype(BF16)
    v_ref[0] = kv[:, M_W:].astype(BF16)


def _mem_kv(mem, p):
    nb, ml, _ = mem.shape
    consts = [p["mem_g"], p["w_mem_kv"], p["mkg"]]
    blk = pl.BlockSpec((1, ml, M_W), lambda b: (b, 0, 0))
    return pl.pallas_call(
        _mem_kv_kernel,
        grid=(nb,),
        in_specs=[pl.BlockSpec((1, ml, D_MODEL), lambda b: (b, 0, 0))] + [_full(c.shape) for c in consts],
        out_specs=[blk, blk],
        out_shape=[jax.ShapeDtypeStruct((nb, ml, M_W), BF16)] * 2,
        compiler_params=_cparams("parallel"),
        name="mem_kv",
    )(mem, *consts)


def _outproj_kernel(x_ref, a_ref, bn_ref, c_ref, ona_ref, onc_ref, wo_ref, xg_ref, wq_ref, mqg_ref,
                    km_ref, vm_ref, wmo_ref, o_ref):
    a_n = (_rms(a_ref[...].astype(F32)) * ona_ref[...]).astype(BF16)
    c_n = (_rms(c_ref[...].astype(F32)) * onc_ref[...]).astype(BF16)
    mix = jnp.concatenate([a_n, bn_ref[...], c_n], axis=-1)
    x1 = x_ref[...] + jnp.dot(mix, wo_ref[...], preferred_element_type=F32)

    h = (_rms(x1) * xg_ref[...]).astype(BF16)
    q = jnp.dot(h, wq_ref[...], preferred_element_type=F32)
    outs = []
    for hd in range(M_HEADS):
        cols = slice(hd * M_HEAD_DIM, (hd + 1) * M_HEAD_DIM)
        qh = (_rms(q[:, cols]) * mqg_ref[...]).astype(BF16)
        s = lax.dot_general(qh, km_ref[0, :, cols], (((1,), (1,)), ((), ())), preferred_element_type=F32)
        e = jnp.exp(s - jnp.max(s, axis=-1, keepdims=True))
        pr = e / jnp.sum(e, axis=-1, keepdims=True)
        outs.append(jnp.dot(pr.astype(BF16), vm_ref[0, :, cols], preferred_element_type=F32).astype(BF16))
    o_ref[...] = x1 + jnp.dot(jnp.concatenate(outs, axis=-1), wmo_ref[...], preferred_element_type=F32)


def _outproj(x2d, a, bn, c, km, vm, seq, p):
    t = x2d.shape[0]
    tm = ROW_TILE
    tps = seq // tm
    ml = km.shape[1]
    row = lambda w: pl.BlockSpec((tm, w), lambda i: (i, 0))
    memblk = pl.BlockSpec((1, ml, M_W), lambda i: (i // tps, 0, 0))
    c1 = [p["ona"], p["onc"], p["w_out"], p["xg"], p["w_mem_q"], p["mqg"]]
    return pl.pallas_call(
        _outproj_kernel,
        grid=(t // tm,),
        in_specs=[row(D_MODEL), row(A_W), row(B_W), row(C_W)] + [_full(c_.shape) for c_ in c1]
                 + [memblk, memblk, _full(p["w_mem_out"].shape)],
        out_specs=row(D_MODEL),
        out_shape=jax.ShapeDtypeStruct((t, D_MODEL), F32),
        compiler_params=_cparams("parallel"),
        name="out_proj_mem_attn",
    )(x2d, a, bn, c, *c1, km, vm, p["w_mem_out"])


def _silu(x):
    return x * jax.nn.sigmoid(x)


def _ffn_kernel(x_ref, g_ref, wg_ref, wu_ref, wd_ref, o_ref, *, n_chunks):
    x = x_ref[...]
    h = (_rms(x) * g_ref[...]).astype(BF16)
    fc = wg_ref.shape[1] // n_chunks
    acc = x
    for c in range(n_chunks):
        cols = slice(c * fc, (c + 1) * fc)
        act = _silu(jnp.dot(h, wg_ref[:, cols], preferred_element_type=F32)) * \
            jnp.dot(h, wu_ref[:, cols], preferred_element_type=F32)
        acc = acc + jnp.dot(act.astype(BF16), wd_ref[cols, :], preferred_element_type=F32)
    o_ref[...] = acc


def _ffn(x2d, g, wg, wu, wd):
    t = x2d.shape[0]
    tm = ROW_TILE
    row = pl.BlockSpec((tm, D_MODEL), lambda i: (i, 0))
    resident = lambda a: pl.BlockSpec(a.shape, lambda i: (0, 0), pipeline_mode=pl.Buffered(1))
    return pl.pallas_call(
        functools.partial(_ffn_kernel, n_chunks=2),
        grid=(t // tm,),
        in_specs=[row, _full(g.shape), resident(wg), resident(wu), resident(wd)],
        out_specs=row,
        out_shape=jax.ShapeDtypeStruct((t, D_MODEL), F32),
        compiler_params=_cparams("parallel"),
        name="ffn_dense",
    )(x2d, g, wg, wu, wd)


def _moe_kernel(x_hbm_ref, g_ref, wr_ref, br_ref, tri_ref, wg_ref, wu_ref, wd_ref, o_ref,
                h_ref, comb_ref, rank_ref, rank_t_ref, sem_ref):
    i = pl.program_id(0)
    e = pl.program_id(1)
    lane = _lane_iota()
    cap = MOE_CAP
    chunk = MOE_CHUNK
    n_sub = o_ref.shape[0] // chunk
    subs = [slice(sc * chunk, (sc + 1) * chunk) for sc in range(n_sub)]

    @pl.when(e == 0)
    def _():
        rows = o_ref.shape[0]
        load = pltpu.make_async_copy(x_hbm_ref.at[pl.ds(i * rows, rows)], o_ref, sem_ref)
        load.start()
        load.wait()
        for sc, rs in enumerate(subs):
            h = _rms(o_ref[rs, :]) * g_ref[...]
            h_hi = h.astype(BF16)
            h_ref[rs, :] = h_hi
            h_lo = (h - h_hi.astype(F32)).astype(BF16)
            logits = (jnp.dot(h_hi, wr_ref[0], preferred_element_type=F32)
                      + jnp.dot(h_lo, wr_ref[0], preferred_element_type=F32)
                      + jnp.dot(h_hi, wr_ref[1], preferred_element_type=F32)) + br_ref[...]
            logits = jnp.where(lane < N_EXPERTS, logits, -jnp.inf)
            v1 = jnp.max(logits, axis=-1, keepdims=True)
            i1 = jnp.min(jnp.where(logits == v1, lane, LANES), axis=-1, keepdims=True)
            rest = jnp.where(lane == i1, -jnp.inf, logits)
            v2 = jnp.max(rest, axis=-1, keepdims=True)
            i2 = jnp.min(jnp.where(rest == v2, lane, LANES), axis=-1, keepdims=True)
            e2 = jnp.exp(v2 - v1)
            g1 = 1.0 / (1.0 + e2)
            comb_ref[rs, :] = jnp.where(lane == i1, g1, 0.0) + jnp.where(lane == i2, e2 * g1, 0.0)
            hit = (lane == i1) | (lane == i2)
            before = jnp.dot(tri_ref[...], jnp.where(hit, 1.0, 0.0).astype(BF16), preferred_element_type=F32)
            rank = jnp.where(hit, before, -1.0)
            rank_ref[rs, :] = rank
            rank_t_ref[sc] = rank.T[0:N_EXPERTS, :]

    pick = lane == e
    rank_cols = [jnp.sum(jnp.where(pick, rank_ref[rs, :], 0.0), axis=-1, keepdims=True) for rs in subs]
    gate_cols = [jnp.sum(jnp.where(pick, comb_ref[rs, :], 0.0), axis=-1, keepdims=True) for rs in subs]
    rank_rows = [rank_t_ref[sc, pl.ds(e, 1), :] for sc in range(n_sub)]
    n_tokens = functools.reduce(jnp.maximum, [jnp.max(r) for r in rank_cols]).astype(jnp.int32) + 1
    slot_col = lax.broadcasted_iota(jnp.int32, (cap, 1), 0).astype(F32)
    slot_row = lax.broadcasted_iota(jnp.int32, (1, cap), 1).astype(F32)

    def block(b, carry):
        r0 = (b * cap).astype(F32)
        xe = []
        for sc, rs in enumerate(subs):
            sel = jnp.where(rank_rows[sc] - r0 == slot_col, 1.0, 0.0).astype(BF16)
            xe.append(jnp.dot(sel, h_ref[rs, :], preferred_element_type=F32).astype(BF16))
        xe = jnp.concatenate(xe, axis=0)
        act = _silu(jnp.dot(xe, wg_ref[0], preferred_element_type=F32)) * \
            jnp.dot(xe, wu_ref[0], preferred_element_type=F32)
        y = jnp.dot(act.astype(BF16), wd_ref[0], preferred_element_type=F32).astype(BF16)
        for sc, rs in enumerate(subs):
            sel_t = jnp.where(rank_cols[sc] - r0 == slot_row, 1.0, 0.0).astype(BF16)
            o_ref[rs, :] += gate_cols[sc] * jnp.dot(sel_t, y[sc * cap:(sc + 1) * cap, :],
                                                    preferred_element_type=F32)
        return carry

    lax.fori_loop(0, (n_tokens + cap - 1) // cap, block, 0)


def _moe(x2d, g, wr, br, wg, wu, wd):
    t = x2d.shape[0]
    chunk = MOE_CHUNK
    tm = MOE_SUB * chunk
    ff = wg.shape[2]
    tri = jnp.asarray(np.tril(np.ones((chunk, chunk), np.float32), -1)).astype(BF16)
    return pl.pallas_call(
        _moe_kernel,
        grid=(t // tm, N_EXPERTS),
        in_specs=[pl.BlockSpec(memory_space=pl.ANY),
                  _full(g.shape), _full(wr.shape), _full(br.shape),
                  pl.BlockSpec((chunk, chunk), lambda i, e: (0, 0), pipeline_mode=pl.Buffered(1)),
                  pl.BlockSpec((1, D_MODEL, ff), lambda i, e: (e, 0, 0)),
                  pl.BlockSpec((1, D_MODEL, ff), lambda i, e: (e, 0, 0)),
                  pl.BlockSpec((1, ff, D_MODEL), lambda i, e: (e, 0, 0))],
        out_specs=pl.BlockSpec((tm, D_MODEL), lambda i, e: (i, 0)),
        out_shape=jax.ShapeDtypeStruct((t, D_MODEL), F32),
        scratch_shapes=[pltpu.VMEM((tm, D_MODEL), BF16), pltpu.VMEM((tm, LANES), F32),
                        pltpu.VMEM((tm, LANES), F32), pltpu.VMEM((MOE_SUB, N_EXPERTS, chunk), F32),
                        pltpu.SemaphoreType.DMA(())],
        compiler_params=_cparams("parallel", "arbitrary"),
        name="moe_routed",
    )(x2d, g, wr, br, tri, wg, wu, wd)


def _pad_cols(w, width):
    return jnp.pad(w, ((0, 0), (0, width - w.shape[1])))


def _layer_params(l, mix_norm, w_in, b_forget, a_q_norm, a_k_norm, b_v_norm, b_spatial_w, b_spatial_b,
                  c_q_lat_norm, c_w_uq, c_kv_lat_norm, c_w_ukv, c_q_nope_norm, c_q_rope_norm,
                  c_k_nope_norm, c_k_rope_norm, out_norm_a, out_norm_b, out_norm_c, w_out,
                  xattn_norm, mem_norm, w_mem_q, w_mem_kv, m_q_norm, m_k_norm, w_mem_out):
    p = {}
    o = np.cumsum((0, A_W, A_W, A_W, A_HEADS, B_W, B_W, C_Q_RANK, C_KV_RANK, C_ROPE_DIM))
    w = w_in[l]
    seg = lambda n: w[:, o[n]:o[n + 1]]
    fa = seg(3)
    misc = jnp.zeros((D_MODEL, LANES), F32)
    misc = misc.at[:, ROPE_LANE:ROPE_LANE + C_ROPE_DIM].set(seg(8))
    fb = jnp.zeros((1, LANES), F32)
    for hd in range(A_HEADS):
        ln = FORGET_LANE + 8 * (hd // 2) + hd % 2
        misc = misc.at[:, ln].set(fa[:, hd])
        fb = fb.at[0, ln].set(b_forget[l, hd])
    p["w_in"] = jnp.concatenate([seg(0), seg(1), seg(2), seg(4), seg(5), seg(6), seg(7), misc], axis=1).astype(BF16)
    p["fb"] = fb
    p["mix_g"] = mix_norm[l][None]
    p["aq"] = jnp.tile(a_q_norm[l], A_HEADS)[None] * (A_HEAD_DIM ** -0.5 * LOG2E)
    p["ak"] = jnp.tile(a_k_norm[l], A_HEADS)[None]
    p["bvg"] = b_v_norm[l][None]
    pos = np.arange(B_WINDOW)
    mask = (pos[None, :] // CHUNK) <= (pos[:, None] // CHUNK)
    p["ws"] = jnp.where(mask[None], b_spatial_w[l], 0.0).reshape(B_GROUPS * B_WINDOW, B_WINDOW).astype(BF16)
    p["bs"] = jnp.repeat(b_spatial_b[l].T, B_GROUP_DIM, axis=1)
    p["onb"] = out_norm_b[l][None]
    p["cqg"] = c_q_lat_norm[l][None]
    p["ckvg"] = c_kv_lat_norm[l][None]
    gidx = np.arange(A_W) // A_HEAD_DIM
    p["gm"] = jnp.asarray((gidx[:, None] == gidx[None, :]).astype(np.float32) / A_HEAD_DIM).astype(BF16)
    p["tri"] = jnp.asarray(np.tril(np.ones((ROW_TILE, ROW_TILE), np.float32))).astype(BF16)

    qd = C_NOPE_DIM + C_ROPE_DIM
    half = C_ROPE_DIM // 2
    wq = c_w_uq[l]
    wq_partner = jnp.concatenate([jnp.zeros_like(wq[:, :, :C_NOPE_DIM]), wq[:, :, C_NOPE_DIM + half:],
                                  wq[:, :, C_NOPE_DIM:C_NOPE_DIM + half]], axis=-1)
    p["wuq"] = jnp.pad(jnp.stack([wq, wq_partner]), ((0, 0), (0, 0), (0, 0), (0, HEAD_SLAB - qd))
                       ).reshape(2, C_Q_RANK, -1).astype(BF16)
    wukv = c_w_ukv[l]
    p["wuk"] = jnp.pad(wukv[:, :, :C_NOPE_DIM], ((0, 0), (0, 0), (0, HEAD_SLAB - C_NOPE_DIM))
                       ).reshape(C_KV_RANK, -1).astype(BF16)
    p["wuv"] = wukv[:, :, C_NOPE_DIM:].reshape(C_KV_RANK, C_W).astype(BF16)
    gq = jnp.concatenate([c_q_nope_norm[l], c_q_rope_norm[l]])
    gq_partner = jnp.concatenate([jnp.zeros_like(c_q_nope_norm[l]), c_q_rope_norm[l][half:],
                                  c_q_rope_norm[l][:half]])
    p["gq"] = _pad_cols(jnp.stack([gq, gq_partner]) * (qd ** -0.5 * LOG2E), LANES)
    p["gkn"] = _pad_cols(c_k_nope_norm[l][None], LANES)
    p["gkr"] = jnp.zeros((1, LANES), F32).at[0, ROPE_LANE:ROPE_LANE + C_ROPE_DIM].set(c_k_rope_norm[l])

    p["ona"] = out_norm_a[l][None]
    p["onc"] = out_norm_c[l][None]
    p["w_out"] = w_out[l].astype(BF16)
    p["xg"] = xattn_norm[l][None]
    p["w_mem_q"] = w_mem_q[l].astype(BF16)
    p["mqg"] = m_q_norm[l][None] * (M_HEAD_DIM ** -0.5)
    p["mem_g"] = mem_norm[l][None]
    p["w_mem_kv"] = w_mem_kv[l].astype(BF16)
    p["mkg"] = m_k_norm[l][None]
    p["w_mem_out"] = w_mem_out[l].astype(BF16)
    return p


def kernel(x, mem, positions, mix_norm, w_in, b_forget, a_q_norm, a_k_norm, b_v_norm, b_spatial_w, b_spatial_b, c_q_lat_norm, c_w_uq, c_kv_lat_norm, c_w_ukv, c_q_nope_norm, c_q_rope_norm, c_k_nope_norm, c_k_rope_norm, out_norm_a, out_norm_b, out_norm_c, w_out, xattn_norm, mem_norm, w_mem_q, w_mem_kv, m_q_norm, m_k_norm, w_mem_out, ffn_norm, ffn_w_gate, ffn_w_up, ffn_w_down, w_router, b_router, moe_w_gate, moe_w_up, moe_w_down):
    nb, seq, d = x.shape
    assert d == D_MODEL and seq % ROW_TILE == 0 and seq % ATTN_TILE == 0 and (nb * seq) % (MOE_SUB * MOE_CHUNK) == 0
    depth = w_in.shape[0]
    t = nb * seq
    x2d = x.reshape(t, d)
    cos, sin = _rope_tables(positions.reshape(t, 1).astype(F32))

    for l in range(depth):
        p = _layer_params(l, mix_norm, w_in, b_forget, a_q_norm, a_k_norm, b_v_norm, b_spatial_w,
                          b_spatial_b, c_q_lat_norm, c_w_uq, c_kv_lat_norm, c_w_ukv, c_q_nope_norm,
                          c_q_rope_norm, c_k_nope_norm, c_k_rope_norm, out_norm_a, out_norm_b,
                          out_norm_c, w_out, xattn_norm, mem_norm, w_mem_q, w_mem_kv, m_q_norm,
                          m_k_norm, w_mem_out)
        qa, ka, vta, bn, cq, ckv, misc = _inproj(x2d, seq, p)
        qc, kc, vtc = _mla_prep(cq, ckv, misc, cos, sin, seq, p)
        a = _attention(qa, ka, vta, unit=1, name="attn_fox")
        c = _attention(qc, kc, vtc, unit=CHUNK, name="attn_mla")
        km, vm = _mem_kv(mem, p)
        x2d = _outproj(x2d, a, bn, c, km, vm, seq, p)
        g = ffn_norm[l][None]
        if l % 2 == 0:
            m = l // 2
            ff = ffn_w_gate.shape[2]
            ff_pad = -(-ff // (2 * LANES)) * (2 * LANES)
            wg = _pad_cols(ffn_w_gate[m], ff_pad).astype(BF16)
            wu = _pad_cols(ffn_w_up[m], ff_pad).astype(BF16)
            wd = jnp.pad(ffn_w_down[m], ((0, ff_pad - ff), (0, 0))).astype(BF16)
            x2d = _ffn(x2d, g, wg, wu, wd)
        else:
            m = l // 2
            wr = _pad_cols(w_router[m], LANES)
            wr_hi = wr.astype(BF16)
            wr = jnp.stack([wr_hi, (wr - wr_hi.astype(F32)).astype(BF16)])
            br = _pad_cols(b_router[m][None], LANES)
            x2d = _moe(x2d, g, wr, br, moe_w_gate[m].astype(BF16), moe_w_up[m].astype(BF16),
                       moe_w_down[m].astype(BF16))
    return x2d.reshape(nb, seq, d)
```

```python
import functools

import numpy as np
import jax
import jax.numpy as jnp
from jax import lax
from jax.experimental import pallas as pl
from jax.experimental.pallas import tpu as pltpu

F32 = jnp.float32
BF16 = jnp.bfloat16
HIGHEST = lax.Precision.HIGHEST

D_MODEL = 1024
CHUNK = 64
EPS = 1e-6
NEG_INF = -1e30
A_HEADS, A_HEAD_DIM = 4, 64
B_GROUPS, B_GROUP_DIM, B_WINDOW = 4, 64, 128
C_HEADS, C_NOPE_DIM, C_ROPE_DIM, C_V_DIM = 8, 64, 32, 64
C_Q_RANK, C_KV_RANK = 256, 128
ROPE_THETA = 10000.0
M_HEADS, M_HEAD_DIM = 4, 128
N_EXPERTS = 8
A_W = A_HEADS * A_HEAD_DIM
B_W = B_GROUPS * B_GROUP_DIM
C_W = C_HEADS * C_V_DIM
M_W = M_HEADS * M_HEAD_DIM

LANES = 128

SEG_Q, SEG_K, SEG_V, SEG_U, SEG_VB, SEG_CQ, SEG_CKV, SEG_MISC = 0, 256, 512, 768, 1024, 1280, 1536, 1664
IN_PAD_W = SEG_MISC + LANES
ROPE_LANE = C_NOPE_DIM
FORGET_LANE = 96
HEAD_SLAB = LANES
VT_ROWS = 80
LOG2E = float(np.log2(np.e))

ROW_TILE = 512
ATTN_TILE = 512
MOE_CHUNK = 1024
MOE_CAP = 256
MOE_TAIL = 128
VMEM_LIMIT = 56 * 1024 * 1024


def _cparams(*sem):
    return pltpu.CompilerParams(dimension_semantics=sem, vmem_limit_bytes=VMEM_LIMIT)


def _full(shape):
    n = len(shape)
    return pl.BlockSpec(shape, lambda *_: (0,) * n)


def _rms(x):
    return x * lax.rsqrt(jnp.mean(x * x, axis=-1, keepdims=True) + EPS)


def _dot_split(v, exact, pieces, lhs_is_exact=False):
    total = None
    rem = v
    for n in range(pieces):
        part = rem.astype(BF16)
        if n + 1 < pieces:
            rem = rem - part.astype(F32)
        term = (jnp.dot(exact, part, preferred_element_type=F32) if lhs_is_exact
                else jnp.dot(part, exact, preferred_element_type=F32))
        total = term if total is None else total + term
    return total


def _lane_iota(n=LANES):
    return lax.broadcasted_iota(jnp.int32, (1, n), 1)


def _rope_table_kernel(pos_ref, inv_ref, sgn_ref, cos_ref, sin_ref):
    ang = pos_ref[...] * inv_ref[...]
    cos_ref[...] = jnp.cos(ang)
    sin_ref[...] = jnp.sin(ang) * sgn_ref[...]


def _rope_tables(pos_col):
    t = pos_col.shape[0]
    half = C_ROPE_DIM // 2
    inv = ROPE_THETA ** (-jnp.arange(half, dtype=F32) / half)
    inv_l = jnp.zeros((1, LANES), F32).at[0, ROPE_LANE:ROPE_LANE + C_ROPE_DIM].set(jnp.tile(inv, 2))
    sgn = np.zeros((1, LANES), np.float32)
    sgn[0, ROPE_LANE:ROPE_LANE + half] = -1.0
    sgn[0, ROPE_LANE + half:ROPE_LANE + C_ROPE_DIM] = 1.0
    tm = ROW_TILE
    return pl.pallas_call(
        _rope_table_kernel,
        grid=(t // tm,),
        in_specs=[pl.BlockSpec((tm, 1), lambda i: (i, 0)), _full((1, LANES)), _full((1, LANES))],
        out_specs=[pl.BlockSpec((tm, LANES), lambda i: (i, 0))] * 2,
        out_shape=[jax.ShapeDtypeStruct((t, LANES), F32)] * 2,
        compiler_params=_cparams("parallel"),
        name="rope_tables",
    )(pos_col, inv_l, jnp.asarray(sgn))


def _rotate(x, cos, sin_signed, lane):
    half = C_ROPE_DIM // 2
    partner = jnp.where(lane < ROPE_LANE + half,
                        pltpu.roll(x, LANES - half, 1), pltpu.roll(x, half, 1))
    return x * cos + partner * sin_signed


def _store_v_transposed(vt_ref, v, n_heads):
    tm = v.shape[0]
    v_t = v.T
    tail = jnp.where(lax.broadcasted_iota(jnp.int32, (VT_ROWS - C_V_DIM, tm), 0) == 0, 1.0, 0.0).astype(BF16)
    for hd in range(n_heads):
        vt_ref[0, hd, 0:C_V_DIM, :] = v_t[hd * C_V_DIM:(hd + 1) * C_V_DIM, :].astype(BF16)
        vt_ref[0, hd, C_V_DIM:VT_ROWS, :] = tail


def _gelu(x):
    return 0.5 * x * (1.0 + lax.erf(x * np.float32(1.0 / np.sqrt(2.0))))


def _inproj_kernel(x_ref, g_ref, w_ref, aq_ref, ak_ref, fb_ref, bvg_ref, ws_ref, bs_ref, onb_ref,
                   cqg_ref, ckvg_ref, gm_ref, tri_ref,
                   qa_ref, ka_ref, vt_ref, bn_ref, cq_ref, ckv_ref, misc_ref,
                   carry_ref, *, tiles_per_seq):
    i = pl.program_id(0)
    tm = x_ref.shape[0]
    h = _rms(x_ref[...]) * g_ref[...]
    proj = jnp.dot(h.astype(BF16), w_ref[...], preferred_element_type=F32)
    gm = gm_ref[...]

    def group_mean(v):
        return _dot_split(v, gm, 2)

    misc = proj[:, SEG_MISC:SEG_MISC + LANES]
    misc_ref[...] = misc
    z = misc + fb_ref[...]
    log_f = jnp.minimum(z, 0.0) - jnp.log1p(jnp.exp(-jnp.abs(z)))

    @pl.when(i % tiles_per_seq == 0)
    def _():
        carry_ref[...] = jnp.zeros_like(carry_ref)

    cum = _dot_split(log_f, tri_ref[...], 3, lhs_is_exact=True) + carry_ref[...]
    carry_ref[...] = cum[tm - 1:tm, :]
    f_hi = (cum * LOG2E).astype(BF16).astype(F32)
    f_rem = cum * LOG2E - f_hi
    f_mid = f_rem.astype(BF16).astype(F32)
    f_lo = f_rem - f_mid

    q = proj[:, SEG_Q:SEG_Q + A_W]
    qn = q * lax.rsqrt(group_mean(q * q) + EPS) * aq_ref[...]
    k = proj[:, SEG_K:SEG_K + A_W]
    kn = k * lax.rsqrt(group_mean(k * k) + EPS) * ak_ref[...]
    lane = _lane_iota()
    for hd in range(A_HEADS):
        pair = slice((hd // 2) * LANES, (hd // 2 + 1) * LANES)
        slab = slice(hd * HEAD_SLAB, (hd + 1) * HEAD_SLAB)
        data = (lane < A_HEAD_DIM) if hd % 2 == 0 else (lane >= A_HEAD_DIM)
        e0 = A_HEAD_DIM if hd % 2 == 0 else 0
        fl = FORGET_LANE + 8 * (hd // 2) + hd % 2
        ones = jnp.where((lane >= e0) & (lane < e0 + 3), 1.0, 0.0)
        qa_ref[:, slab] = jnp.where(data, qn[:, pair], ones).astype(BF16)
        bias = jnp.where(lane == e0, -f_hi[:, fl:fl + 1],
                         jnp.where(lane == e0 + 1, -f_mid[:, fl:fl + 1],
                                   jnp.where(lane == e0 + 2, -f_lo[:, fl:fl + 1], 0.0)))
        ka_ref[:, slab] = jnp.where(data, kn[:, pair], bias).astype(BF16)
    _store_v_transposed(vt_ref, proj[:, SEG_V:SEG_V + A_W], A_HEADS)

    u = _gelu(proj[:, SEG_U:SEG_U + B_W])
    v = _gelu(proj[:, SEG_VB:SEG_VB + B_W])
    dv = v - group_mean(v)
    vn = dv * lax.rsqrt(group_mean(dv * dv) + EPS) * bvg_ref[...]
    group = lax.broadcasted_iota(jnp.int32, (1, B_W), 1) // B_GROUP_DIM
    for w in range(tm // B_WINDOW):
        rows = slice(w * B_WINDOW, (w + 1) * B_WINDOW)
        y_all = jnp.dot(ws_ref[...], vn[rows].astype(BF16), preferred_element_type=F32)
        y = bs_ref[...]
        for g in range(B_GROUPS):
            y = y + jnp.where(group == g, y_all[g * B_WINDOW:(g + 1) * B_WINDOW], 0.0)
        b = u[rows] * y
        bn_ref[rows, :] = (_rms(b) * onb_ref[...]).astype(BF16)

    cq_ref[...] = (_rms(proj[:, SEG_CQ:SEG_CQ + C_Q_RANK]) * cqg_ref[...]).astype(BF16)
    ckv_ref[...] = (_rms(proj[:, SEG_CKV:SEG_CKV + C_KV_RANK]) * ckvg_ref[...]).astype(BF16)


def _inproj(x2d, seq, p):
    t = x2d.shape[0]
    tm = ROW_TILE
    tps = seq // tm
    nb = t // seq
    row = lambda w: pl.BlockSpec((tm, w), lambda i: (i, 0))
    qk_w = A_HEADS * HEAD_SLAB
    out_shape = [
        jax.ShapeDtypeStruct((t, qk_w), BF16), jax.ShapeDtypeStruct((t, qk_w), BF16),
        jax.ShapeDtypeStruct((nb, A_HEADS, VT_ROWS, seq), BF16),
        jax.ShapeDtypeStruct((t, B_W), BF16),
        jax.ShapeDtypeStruct((t, C_Q_RANK), BF16), jax.ShapeDtypeStruct((t, C_KV_RANK), BF16),
        jax.ShapeDtypeStruct((t, LANES), F32),
    ]
    out_specs = [row(qk_w), row(qk_w),
                 pl.BlockSpec((1, A_HEADS, VT_ROWS, tm), lambda i: (i // tps, 0, 0, i % tps)),
                 row(B_W), row(C_Q_RANK), row(C_KV_RANK), row(LANES)]
    consts = [p["mix_g"], p["w_in"], p["aq"], p["ak"], p["fb"], p["bvg"], p["ws"], p["bs"], p["onb"],
              p["cqg"], p["ckvg"], p["gm"], p["tri"]]
    return pl.pallas_call(
        functools.partial(_inproj_kernel, tiles_per_seq=tps),
        grid=(t // tm,),
        in_specs=[row(D_MODEL)] + [_full(c.shape) for c in consts],
        out_specs=out_specs,
        out_shape=out_shape,
        scratch_shapes=[pltpu.VMEM((1, LANES), F32)],
        compiler_params=_cparams("arbitrary"),
        name="in_proj",
    )(x2d, *consts)


def _mla_prep_kernel(cq_ref, ckv_ref, misc_ref, cos_ref, sin_ref, wuq_ref, wuk_ref, wuv_ref,
                     gq_ref, gkn_ref, gkr_ref, qc_ref, kc_ref, vt_ref):
    lane = _lane_iota()
    nope = lane < C_NOPE_DIM
    rope = (lane >= ROPE_LANE) & (lane < ROPE_LANE + C_ROPE_DIM)
    cos, sin = cos_ref[...], sin_ref[...]
    q = jnp.dot(cq_ref[...], wuq_ref[0], preferred_element_type=F32)
    q_partner = jnp.dot(cq_ref[...], wuq_ref[1], preferred_element_type=F32)
    q_cos = gq_ref[0:1, :] * cos
    q_sin = gq_ref[1:2, :] * sin
    kn = jnp.dot(ckv_ref[...], wuk_ref[...], preferred_element_type=F32)
    _store_v_transposed(vt_ref, jnp.dot(ckv_ref[...], wuv_ref[...], preferred_element_type=F32), C_HEADS)

    kr = jnp.where(rope, misc_ref[...], 0.0)
    kr = kr * lax.rsqrt(jnp.sum(kr * kr, axis=-1, keepdims=True) * (1.0 / C_ROPE_DIM) + EPS) * gkr_ref[...]
    kr = _rotate(kr, cos, sin, lane)

    for hd in range(C_HEADS):
        cols = slice(hd * HEAD_SLAB, (hd + 1) * HEAD_SLAB)
        qh = q[:, cols]
        sq = qh * qh
        r_n = lax.rsqrt(jnp.sum(jnp.where(nope, sq, 0.0), axis=-1, keepdims=True) * (1.0 / C_NOPE_DIM) + EPS)
        r_r = lax.rsqrt(jnp.sum(jnp.where(rope, sq, 0.0), axis=-1, keepdims=True) * (1.0 / C_ROPE_DIM) + EPS)
        qc_ref[:, cols] = (jnp.where(nope, r_n, r_r)
                           * (qh * q_cos + q_partner[:, cols] * q_sin)).astype(BF16)
        kh = kn[:, cols]
        r_k = lax.rsqrt(jnp.sum(kh * kh, axis=-1, keepdims=True) * (1.0 / C_NOPE_DIM) + EPS)
        kc_ref[:, cols] = (kh * r_k * gkn_ref[...] + kr).astype(BF16)


def _mla_prep(cq, ckv, misc, cos, sin, seq, p):
    t = cq.shape[0]
    tm = ROW_TILE
    tps = seq // tm
    row = lambda w: pl.BlockSpec((tm, w), lambda i: (i, 0))
    consts = [p["wuq"], p["wuk"], p["wuv"], p["gq"], p["gkn"], p["gkr"]]
    qk_w = C_HEADS * HEAD_SLAB
    return pl.pallas_call(
        _mla_prep_kernel,
        grid=(t // tm,),
        in_specs=[row(C_Q_RANK), row(C_KV_RANK), row(LANES), row(LANES), row(LANES)]
                 + [_full(c.shape) for c in consts],
        out_specs=[row(qk_w), row(qk_w),
                   pl.BlockSpec((1, C_HEADS, VT_ROWS, tm), lambda i: (i // tps, 0, 0, i % tps))],
        out_shape=[jax.ShapeDtypeStruct((t, qk_w), BF16), jax.ShapeDtypeStruct((t, qk_w), BF16),
                   jax.ShapeDtypeStruct((t // seq, C_HEADS, VT_ROWS, seq), BF16)],
        compiler_params=_cparams("parallel"),
        name="mla_prep",
    )(cq, ckv, misc, cos, sin, *consts)


def _attn_items(nq):
    items = [(i, j) for i in range(nq) for j in range(i)] + [(i, i) for i in range(nq)]
    return np.array(items, np.int32).T


def _attn_kernel(items_ref, q_ref, k_ref, vt_ref, mask_ref, o_ref, s0_ref, s1_ref, p0_ref, p1_ref,
                 mp0_ref, mp1_ref, mrun_ref, macc_ref, acc_ref, *, n_items):
    tk, tq = mask_ref.shape
    nq = q_ref.shape[0] // tq
    s_bufs, p_bufs, mp_bufs = (s0_ref, s1_ref), (p0_ref, p1_ref), (mp0_ref, mp1_ref)
    mrun_ref[...] = jnp.full(mrun_ref.shape, NEG_INF, F32)
    macc_ref[...] = jnp.full(macc_ref.shape, NEG_INF, F32)
    acc_ref[...] = jnp.zeros(acc_ref.shape, F32)

    def scores(it, buf, masked):
        qi = items_ref[0, it]
        q_start = pl.multiple_of(qi * tq, tq)
        k_start = pl.multiple_of(items_ref[1, it] * tk, tk)
        for hh in range(2):
            cols = slice(hh * HEAD_SLAB, (hh + 1) * HEAD_SLAB)
            s_t = lax.dot_general(k_ref[pl.ds(k_start, tk), cols], q_ref[pl.ds(q_start, tq), cols],
                                  (((1,), (1,)), ((), ())), preferred_element_type=F32)
            if masked:
                s_t = s_t + mask_ref[...]
            s_bufs[buf][hh] = s_t
            mrun_ref[qi, hh] = jnp.maximum(mrun_ref[qi, hh], jnp.max(s_t, axis=0, keepdims=True))

    def exponentiate(it, buf):
        qi = items_ref[0, it]
        for hh in range(2):
            m = mrun_ref[qi, hh]
            p_bufs[buf][hh] = jnp.exp2(s_bufs[buf][hh] - m).astype(BF16)
            mp_bufs[buf][hh] = m

    def accumulate(it, buf):
        qi = items_ref[0, it]
        k_start = pl.multiple_of(items_ref[1, it] * tk, tk)
        for hh in range(2):
            m = mp_bufs[buf][hh]
            pv = jnp.dot(vt_ref[0, hh, :, pl.ds(k_start, tk)], p_bufs[buf][hh],
                         preferred_element_type=F32)
            acc_ref[qi, hh] = jnp.exp2(macc_ref[qi, hh] - m) * acc_ref[qi, hh] + pv
            macc_ref[qi, hh] = m

    def beat(it, par, masked):
        if not isinstance(it, int) or 2 <= it <= n_items + 1:
            accumulate(it - 2, par)
        if not isinstance(it, int) or 1 <= it <= n_items:
            exponentiate(it - 1, 1 - par)
        if not isinstance(it, int) or it < n_items:
            scores(it, par, masked)

    def run_beats(lo, hi, masked):
        lo = max(lo, 0)
        if lo < hi and (lo % 2 == 1 or lo < 2):
            for it in range(lo, min(hi, lo + 2 - lo % 2)):
                beat(it, it % 2, masked)
            lo = min(hi, lo + 2 - lo % 2)
        n_pairs = (hi - lo) // 2
        if n_pairs > 0:
            def pair(t, carry, lo=lo):
                beat(lo + 2 * t, 0, masked)
                beat(lo + 2 * t + 1, 1, masked)
                return carry
            lax.fori_loop(0, n_pairs, pair, 0)
        for it in range(lo + 2 * n_pairs, hi):
            beat(it, it % 2, masked)

    n_full = n_items - nq
    run_beats(0, n_full, False)
    run_beats(n_full, n_items, True)
    for it in (n_items, n_items + 1):
        beat(it, it % 2, True)

    for qi in range(nq):
        halves = [acc_ref[qi, hh, 0:C_V_DIM, :] / acc_ref[qi, hh, C_V_DIM:C_V_DIM + 1, :] for hh in range(2)]
        o_ref[qi * tq:(qi + 1) * tq, :] = jnp.concatenate(halves, axis=0).T.astype(o_ref.dtype)


def _attention(q, k, vt, *, unit, name):
    t = q.shape[0]
    nb, n_heads, _, seq = vt.shape
    tq = ATTN_TILE
    nq = seq // tq
    items = _attn_items(nq)
    pos = np.arange(tq)
    diag_mask = np.where((pos[:, None] // unit) <= (pos[None, :] // unit), 0.0, NEG_INF)
    mask = jnp.asarray(diag_mask.astype(np.float32))
    seq_blk = lambda w: pl.BlockSpec((seq, w), lambda b, p, items_ref: (b, p))
    grid_spec = pltpu.PrefetchScalarGridSpec(
        num_scalar_prefetch=1,
        grid=(nb, n_heads // 2),
        in_specs=[seq_blk(2 * HEAD_SLAB), seq_blk(2 * HEAD_SLAB),
                  pl.BlockSpec((1, 2, VT_ROWS, seq), lambda b, p, items_ref: (b, p, 0, 0)),
                  pl.BlockSpec((tq, tq), lambda b, p, items_ref: (0, 0), pipeline_mode=pl.Buffered(1))],
        out_specs=seq_blk(2 * C_V_DIM),
        scratch_shapes=[pltpu.VMEM((2, tq, tq), F32), pltpu.VMEM((2, tq, tq), F32),
                        pltpu.VMEM((2, tq, tq), BF16), pltpu.VMEM((2, tq, tq), BF16),
                        pltpu.VMEM((2, 1, tq), F32), pltpu.VMEM((2, 1, tq), F32),
                        pltpu.VMEM((nq, 2, 1, tq), F32), pltpu.VMEM((nq, 2, 1, tq), F32),
                        pltpu.VMEM((nq, 2, VT_ROWS, tq), F32)])
    return pl.pallas_call(
        functools.partial(_attn_kernel, n_items=items.shape[1]),
        grid_spec=grid_spec,
        out_shape=jax.ShapeDtypeStruct((t, n_heads * C_V_DIM), BF16),
        compiler_params=_cparams("parallel", "parallel"),
        name=name,
    )(jnp.asarray(items), q, k, vt, mask)


def _mem_kv_kernel(mem_ref, g_ref, w_ref, kg_ref, k_ref, v_ref):
    mn = (_rms(mem_ref[0]) * g_ref[...]).astype(BF16)
    kv = jnp.dot(mn, w_ref[...], preferred_element_type=F32)
    for hd in range(M_HEADS):
        cols = slice(hd * M_HEAD_DIM, (hd + 1) * M_HEAD_DIM)
        k_ref[0, :, cols] = (_rms(kv[:, cols]) * kg_ref[...]).astype(BF16)
    v_ref[0] = kv[:, M_W:].astype(BF16)


def _mem_kv(mem, p):
    nb, ml, _ = mem.shape
    consts = [p["mem_g"], p["w_mem_kv"], p["mkg"]]
    blk = pl.BlockSpec((1, ml, M_W), lambda b: (b, 0, 0))
    return pl.pallas_call(
        _mem_kv_kernel,
        grid=(nb,),
        in_specs=[pl.BlockSpec((1, ml, D_MODEL), lambda b: (b, 0, 0))] + [_full(c.shape) for c in consts],
        out_specs=[blk, blk],
        out_shape=[jax.ShapeDtypeStruct((nb, ml, M_W), BF16)] * 2,
        compiler_params=_cparams("parallel"),
        name="mem_kv",
    )(mem, *consts)


def _outproj_kernel(x_ref, a_ref, bn_ref, c_ref, ona_ref, onc_ref, wo_ref, xg_ref, wq_ref, mqg_ref,
                    km_ref, vm_ref, wmo_ref, o_ref):
    a_n = (_rms(a_ref[...].astype(F32)) * ona_ref[...]).astype(BF16)
    c_n = (_rms(c_ref[...].astype(F32)) * onc_ref[...]).astype(BF16)
    mix = jnp.concatenate([a_n, bn_ref[...], c_n], axis=-1)
    x1 = x_ref[...] + jnp.dot(mix, wo_ref[...], preferred_element_type=F32)

    h = (_rms(x1) * xg_ref[...]).astype(BF16)
    q = jnp.dot(h, wq_ref[...], preferred_element_type=F32)
    outs = []
    for hd in range(M_HEADS):
        cols = slice(hd * M_HEAD_DIM, (hd + 1) * M_HEAD_DIM)
        qh = (_rms(q[:, cols]) * mqg_ref[...]).astype(BF16)
        s = lax.dot_general(qh, km_ref[0, :, cols], (((1,), (1,)), ((), ())), preferred_element_type=F32)
        e = jnp.exp(s - jnp.max(s, axis=-1, keepdims=True))
        pr = e / jnp.sum(e, axis=-1, keepdims=True)
        outs.append(jnp.dot(pr.astype(BF16), vm_ref[0, :, cols], preferred_element_type=F32).astype(BF16))
    o_ref[...] = x1 + jnp.dot(jnp.concatenate(outs, axis=-1), wmo_ref[...], preferred_element_type=F32)


def _outproj(x2d, a, bn, c, km, vm, seq, p):
    t = x2d.shape[0]
    tm = ROW_TILE
    tps = seq // tm
    ml = km.shape[1]
    row = lambda w: pl.BlockSpec((tm, w), lambda i: (i, 0))
    memblk = pl.BlockSpec((1, ml, M_W), lambda i: (i // tps, 0, 0))
    c1 = [p["ona"], p["onc"], p["w_out"], p["xg"], p["w_mem_q"], p["mqg"]]
    return pl.pallas_call(
        _outproj_kernel,
        grid=(t // tm,),
        in_specs=[row(D_MODEL), row(A_W), row(B_W), row(C_W)] + [_full(c_.shape) for c_ in c1]
                 + [memblk, memblk, _full(p["w_mem_out"].shape)],
        out_specs=row(D_MODEL),
        out_shape=jax.ShapeDtypeStruct((t, D_MODEL), F32),
        compiler_params=_cparams("parallel"),
        name="out_proj_mem_attn",
    )(x2d, a, bn, c, *c1, km, vm, p["w_mem_out"])


def _silu(x):
    return x * jax.nn.sigmoid(x)


def _ffn_kernel(x_ref, g_ref, wg_ref, wu_ref, wd_ref, o_ref, *, n_chunks):
    x = x_ref[...]
    h = (_rms(x) * g_ref[...]).astype(BF16)
    fc = wg_ref.shape[1] // n_chunks
    acc = x
    for c in range(n_chunks):
        cols = slice(c * fc, (c + 1) * fc)
        act = _silu(jnp.dot(h, wg_ref[:, cols], preferred_element_type=F32)) * \
            jnp.dot(h, wu_ref[:, cols], preferred_element_type=F32)
        acc = acc + jnp.dot(act.astype(BF16), wd_ref[cols, :], preferred_element_type=F32)
    o_ref[...] = acc


def _ffn(x2d, g, wg, wu, wd):
    t = x2d.shape[0]
    tm = ROW_TILE
    row = pl.BlockSpec((tm, D_MODEL), lambda i: (i, 0))
    resident = lambda a: pl.BlockSpec(a.shape, lambda i: (0, 0), pipeline_mode=pl.Buffered(1))
    return pl.pallas_call(
        functools.partial(_ffn_kernel, n_chunks=2),
        grid=(t // tm,),
        in_specs=[row, _full(g.shape), resident(wg), resident(wu), resident(wd)],
        out_specs=row,
        out_shape=jax.ShapeDtypeStruct((t, D_MODEL), F32),
        compiler_params=_cparams("parallel"),
        name="ffn_dense",
    )(x2d, g, wg, wu, wd)


def _moe_kernel(x_hbm_ref, g_ref, wr_ref, br_ref, tri_ref, wg_ref, wu_ref, wd_ref, o_ref,
                h_ref, comb_ref, rank_ref, rank_t_ref, sem_ref):
    i = pl.program_id(0)
    e = pl.program_id(1)
    lane = _lane_iota()
    cap = MOE_CAP

    @pl.when(e == 0)
    def _():
        rows = o_ref.shape[0]
        load = pltpu.make_async_copy(x_hbm_ref.at[pl.ds(i * rows, rows)], o_ref, sem_ref)
        load.start()
        load.wait()
        h = _rms(o_ref[...]) * g_ref[...]
        h_hi = h.astype(BF16)
        h_ref[...] = h_hi
        h_lo = (h - h_hi.astype(F32)).astype(BF16)
        logits = (jnp.dot(h_hi, wr_ref[0], preferred_element_type=F32)
                  + jnp.dot(h_lo, wr_ref[0], preferred_element_type=F32)
                  + jnp.dot(h_hi, wr_ref[1], preferred_element_type=F32)) + br_ref[...]
        logits = jnp.where(lane < N_EXPERTS, logits, -jnp.inf)
        v1 = jnp.max(logits, axis=-1, keepdims=True)
        i1 = jnp.min(jnp.where(logits == v1, lane, LANES), axis=-1, keepdims=True)
        rest = jnp.where(lane == i1, -jnp.inf, logits)
        v2 = jnp.max(rest, axis=-1, keepdims=True)
        i2 = jnp.min(jnp.where(rest == v2, lane, LANES), axis=-1, keepdims=True)
        e2 = jnp.exp(v2 - v1)
        g1 = 1.0 / (1.0 + e2)
        comb_ref[...] = jnp.where(lane == i1, g1, 0.0) + jnp.where(lane == i2, e2 * g1, 0.0)
        hit = (lane == i1) | (lane == i2)
        before = jnp.dot(tri_ref[...], jnp.where(hit, 1.0, 0.0).astype(BF16), preferred_element_type=F32)
        rank = jnp.where(hit, before, -1.0)
        rank_ref[...] = rank
        rank_t_ref[...] = rank.T[0:N_EXPERTS, :]

    pick = lane == e
    rank_col = jnp.sum(jnp.where(pick, rank_ref[...], 0.0), axis=-1, keepdims=True)
    gate_col = jnp.sum(jnp.where(pick, comb_ref[...], 0.0), axis=-1, keepdims=True)
    rank_row = rank_t_ref[pl.ds(e, 1), :]
    n_tokens = jnp.max(rank_col).astype(jnp.int32) + 1

    def block(b, rows):
        r0 = (b * cap).astype(F32)
        slot_col = lax.broadcasted_iota(jnp.int32, (rows, 1), 0).astype(F32)
        slot_row = lax.broadcasted_iota(jnp.int32, (1, rows), 1).astype(F32)
        sel = jnp.where(rank_row - r0 == slot_col, 1.0, 0.0).astype(BF16)
        sel_t = jnp.where(rank_col - r0 == slot_row, 1.0, 0.0).astype(BF16)
        xe = jnp.dot(sel, h_ref[...], preferred_element_type=F32).astype(BF16)
        act = _silu(jnp.dot(xe, wg_ref[0], preferred_element_type=F32)) * \
            jnp.dot(xe, wu_ref[0], preferred_element_type=F32)
        y = jnp.dot(act.astype(BF16), wd_ref[0], preferred_element_type=F32).astype(BF16)
        o_ref[...] += gate_col * jnp.dot(sel_t, y, preferred_element_type=F32)

    n_full = n_tokens // cap
    left = n_tokens - n_full * cap

    def full_block(b, carry):
        block(b, cap)
        return carry

    lax.fori_loop(0, n_full, full_block, 0)

    @pl.when(left > MOE_TAIL)
    def _():
        block(n_full, cap)

    @pl.when((left > 0) & (left <= MOE_TAIL))
    def _():
        block(n_full, MOE_TAIL)


def _moe(x2d, g, wr, br, wg, wu, wd):
    t = x2d.shape[0]
    chunk = tm = MOE_CHUNK
    ff = wg.shape[2]
    tri = jnp.asarray(np.tril(np.ones((chunk, chunk), np.float32), -1)).astype(BF16)
    return pl.pallas_call(
        _moe_kernel,
        grid=(t // tm, N_EXPERTS),
        in_specs=[pl.BlockSpec(memory_space=pl.ANY),
                  _full(g.shape), _full(wr.shape), _full(br.shape),
                  pl.BlockSpec((chunk, chunk), lambda i, e: (0, 0), pipeline_mode=pl.Buffered(1)),
                  pl.BlockSpec((1, D_MODEL, ff), lambda i, e: (e, 0, 0)),
                  pl.BlockSpec((1, D_MODEL, ff), lambda i, e: (e, 0, 0)),
                  pl.BlockSpec((1, ff, D_MODEL), lambda i, e: (e, 0, 0))],
        out_specs=pl.BlockSpec((tm, D_MODEL), lambda i, e: (i, 0)),
        out_shape=jax.ShapeDtypeStruct((t, D_MODEL), F32),
        scratch_shapes=[pltpu.VMEM((tm, D_MODEL), BF16), pltpu.VMEM((tm, LANES), F32),
                        pltpu.VMEM((tm, LANES), F32), pltpu.VMEM((N_EXPERTS, chunk), F32),
                        pltpu.SemaphoreType.DMA(())],
        compiler_params=_cparams("parallel", "arbitrary"),
        name="moe_routed",
    )(x2d, g, wr, br, tri, wg, wu, wd)


def _pad_cols(w, width):
    return jnp.pad(w, ((0, 0), (0, width - w.shape[1])))


def _layer_params(l, mix_norm, w_in, b_forget, a_q_norm, a_k_norm, b_v_norm, b_spatial_w, b_spatial_b,
                  c_q_lat_norm, c_w_uq, c_kv_lat_norm, c_w_ukv, c_q_nope_norm, c_q_rope_norm,
                  c_k_nope_norm, c_k_rope_norm, out_norm_a, out_norm_b, out_norm_c, w_out,
                  xattn_norm, mem_norm, w_mem_q, w_mem_kv, m_q_norm, m_k_norm, w_mem_out):
    p = {}
    o = np.cumsum((0, A_W, A_W, A_W, A_HEADS, B_W, B_W, C_Q_RANK, C_KV_RANK, C_ROPE_DIM))
    w = w_in[l]
    seg = lambda n: w[:, o[n]:o[n + 1]]
    fa = seg(3)
    misc = jnp.zeros((D_MODEL, LANES), F32)
    misc = misc.at[:, ROPE_LANE:ROPE_LANE + C_ROPE_DIM].set(seg(8))
    fb = jnp.zeros((1, LANES), F32)
    for hd in range(A_HEADS):
        ln = FORGET_LANE + 8 * (hd // 2) + hd % 2
        misc = misc.at[:, ln].set(fa[:, hd])
        fb = fb.at[0, ln].set(b_forget[l, hd])
    p["w_in"] = jnp.concatenate([seg(0), seg(1), seg(2), seg(4), seg(5), seg(6), seg(7), misc], axis=1).astype(BF16)
    p["fb"] = fb
    p["mix_g"] = mix_norm[l][None]
    p["aq"] = jnp.tile(a_q_norm[l], A_HEADS)[None] * (A_HEAD_DIM ** -0.5 * LOG2E)
    p["ak"] = jnp.tile(a_k_norm[l], A_HEADS)[None]
    p["bvg"] = b_v_norm[l][None]
    pos = np.arange(B_WINDOW)
    mask = (pos[None, :] // CHUNK) <= (pos[:, None] // CHUNK)
    p["ws"] = jnp.where(mask[None], b_spatial_w[l], 0.0).reshape(B_GROUPS * B_WINDOW, B_WINDOW).astype(BF16)
    p["bs"] = jnp.repeat(b_spatial_b[l].T, B_GROUP_DIM, axis=1)
    p["onb"] = out_norm_b[l][None]
    p["cqg"] = c_q_lat_norm[l][None]
    p["ckvg"] = c_kv_lat_norm[l][None]
    gidx = np.arange(A_W) // A_HEAD_DIM
    p["gm"] = jnp.asarray((gidx[:, None] == gidx[None, :]).astype(np.float32) / A_HEAD_DIM).astype(BF16)
    p["tri"] = jnp.asarray(np.tril(np.ones((ROW_TILE, ROW_TILE), np.float32))).astype(BF16)

    qd = C_NOPE_DIM + C_ROPE_DIM
    half = C_ROPE_DIM // 2
    wq = c_w_uq[l]
    wq_partner = jnp.concatenate([jnp.zeros_like(wq[:, :, :C_NOPE_DIM]), wq[:, :, C_NOPE_DIM + half:],
                                  wq[:, :, C_NOPE_DIM:C_NOPE_DIM + half]], axis=-1)
    p["wuq"] = jnp.pad(jnp.stack([wq, wq_partner]), ((0, 0), (0, 0), (0, 0), (0, HEAD_SLAB - qd))
                       ).reshape(2, C_Q_RANK, -1).astype(BF16)
    wukv = c_w_ukv[l]
    p["wuk"] = jnp.pad(wukv[:, :, :C_NOPE_DIM], ((0, 0), (0, 0), (0, HEAD_SLAB - C_NOPE_DIM))
                       ).reshape(C_KV_RANK, -1).astype(BF16)
    p["wuv"] = wukv[:, :, C_NOPE_DIM:].reshape(C_KV_RANK, C_W).astype(BF16)
    gq = jnp.concatenate([c_q_nope_norm[l], c_q_rope_norm[l]])
    gq_partner = jnp.concatenate([jnp.zeros_like(c_q_nope_norm[l]), c_q_rope_norm[l][half:],
                                  c_q_rope_norm[l][:half]])
    p["gq"] = _pad_cols(jnp.stack([gq, gq_partner]) * (qd ** -0.5 * LOG2E), LANES)
    p["gkn"] = _pad_cols(c_k_nope_norm[l][None], LANES)
    p["gkr"] = jnp.zeros((1, LANES), F32).at[0, ROPE_LANE:ROPE_LANE + C_ROPE_DIM].set(c_k_rope_norm[l])

    p["ona"] = out_norm_a[l][None]
    p["onc"] = out_norm_c[l][None]
    p["w_out"] = w_out[l].astype(BF16)
    p["xg"] = xattn_norm[l][None]
    p["w_mem_q"] = w_mem_q[l].astype(BF16)
    p["mqg"] = m_q_norm[l][None] * (M_HEAD_DIM ** -0.5)
    p["mem_g"] = mem_norm[l][None]
    p["w_mem_kv"] = w_mem_kv[l].astype(BF16)
    p["mkg"] = m_k_norm[l][None]
    p["w_mem_out"] = w_mem_out[l].astype(BF16)
    return p


def kernel(x, mem, positions, mix_norm, w_in, b_forget, a_q_norm, a_k_norm, b_v_norm, b_spatial_w, b_spatial_b, c_q_lat_norm, c_w_uq, c_kv_lat_norm, c_w_ukv, c_q_nope_norm, c_q_rope_norm, c_k_nope_norm, c_k_rope_norm, out_norm_a, out_norm_b, out_norm_c, w_out, xattn_norm, mem_norm, w_mem_q, w_mem_kv, m_q_norm, m_k_norm, w_mem_out, ffn_norm, ffn_w_gate, ffn_w_up, ffn_w_down, w_router, b_router, moe_w_gate, moe_w_up, moe_w_down):
    nb, seq, d = x.shape
    assert d == D_MODEL and seq % ROW_TILE == 0 and seq % ATTN_TILE == 0 and (nb * seq) % MOE_CHUNK == 0
    depth = w_in.shape[0]
    t = nb * seq
    x2d = x.reshape(t, d)
    cos, sin = _rope_tables(positions.reshape(t, 1).astype(F32))

    for l in range(depth):
        p = _layer_params(l, mix_norm, w_in, b_forget, a_q_norm, a_k_norm, b_v_norm, b_spatial_w,
                          b_spatial_b, c_q_lat_norm, c_w_uq, c_kv_lat_norm, c_w_ukv, c_q_nope_norm,
                          c_q_rope_norm, c_k_nope_norm, c_k_rope_norm, out_norm_a, out_norm_b,
                          out_norm_c, w_out, xattn_norm, mem_norm, w_mem_q, w_mem_kv, m_q_norm,
                          m_k_norm, w_mem_out)
        qa, ka, vta, bn, cq, ckv, misc = _inproj(x2d, seq, p)
        qc, kc, vtc = _mla_prep(cq, ckv, misc, cos, sin, seq, p)
        a = _attention(qa, ka, vta, unit=1, name="attn_fox")
        c = _attention(qc, kc, vtc, unit=CHUNK, name="attn_mla")
        km, vm = _mem_kv(mem, p)
        x2d = _outproj(x2d, a, bn, c, km, vm, seq, p)
        g = ffn_norm[l][None]
        if l % 2 == 0:
            m = l // 2
            ff = ffn_w_gate.shape[2]
            ff_pad = -(-ff // (2 * LANES)) * (2 * LANES)
            wg = _pad_cols(ffn_w_gate[m], ff_pad).astype(BF16)
            wu = _pad_cols(ffn_w_up[m], ff_pad).astype(BF16)
            wd = jnp.pad(ffn_w_down[m], ((0, ff_pad - ff), (0, 0))).astype(BF16)
            x2d = _ffn(x2d, g, wg, wu, wd)
        else:
            m = l // 2
            wr = _pad_cols(w_router[m], LANES)
            wr_hi = wr.astype(BF16)
            wr = jnp.stack([wr_hi, (wr - wr_hi.astype(F32)).astype(BF16)])
            br = _pad_cols(b_router[m][None], LANES)
            x2d = _moe(x2d, g, wr, br, moe_w_gate[m].astype(BF16), moe_w_up[m].astype(BF16),
                       moe_w_down[m].astype(BF16))
    return x2d.reshape(nb, seq, d)
```

```python
import functools

import numpy as np
import jax
import jax.numpy as jnp
from jax import lax
from jax.experimental import pallas as pl
from jax.experimental.pallas import tpu as pltpu
from jax.experimental.pallas import tpu_sc as plsc

F32 = jnp.float32
BF16 = jnp.bfloat16
HIGHEST = lax.Precision.HIGHEST

D_MODEL = 1024
CHUNK = 64
EPS = 1e-6
NEG_INF = -1e30
A_HEADS, A_HEAD_DIM = 4, 64
B_GROUPS, B_GROUP_DIM, B_WINDOW = 4, 64, 128
C_HEADS, C_NOPE_DIM, C_ROPE_DIM, C_V_DIM = 8, 64, 32, 64
C_Q_RANK, C_KV_RANK = 256, 128
ROPE_THETA = 10000.0
M_HEADS, M_HEAD_DIM = 4, 128
N_EXPERTS = 8
A_W = A_HEADS * A_HEAD_DIM
B_W = B_GROUPS * B_GROUP_DIM
C_W = C_HEADS * C_V_DIM
M_W = M_HEADS * M_HEAD_DIM

LANES = 128

SEG_Q, SEG_K, SEG_V, SEG_U, SEG_VB, SEG_CQ, SEG_CKV, SEG_MISC = 0, 256, 512, 768, 1024, 1280, 1536, 1664
IN_PAD_W = SEG_MISC + LANES
ROPE_LANE = C_NOPE_DIM
FORGET_LANE = 96
HEAD_SLAB = LANES
VT_ROWS = 80
LOG2E = float(np.log2(np.e))

ROW_TILE = 512
ATTN_TILE = 512
MOE_CHUNK = 1024
MOE_CAP = 256
MOE_TAIL = 128
MOE_BLOCK = 512
VMEM_LIMIT = 56 * 1024 * 1024


def _cparams(*sem):
    return pltpu.CompilerParams(dimension_semantics=sem, vmem_limit_bytes=VMEM_LIMIT)


def _full(shape):
    n = len(shape)
    return pl.BlockSpec(shape, lambda *_: (0,) * n)


def _rms(x):
    return x * lax.rsqrt(jnp.mean(x * x, axis=-1, keepdims=True) + EPS)


def _dot_split(v, exact, pieces, lhs_is_exact=False):
    total = None
    rem = v
    for n in range(pieces):
        part = rem.astype(BF16)
        if n + 1 < pieces:
            rem = rem - part.astype(F32)
        term = (jnp.dot(exact, part, preferred_element_type=F32) if lhs_is_exact
                else jnp.dot(part, exact, preferred_element_type=F32))
        total = term if total is None else total + term
    return total


def _lane_iota(n=LANES):
    return lax.broadcasted_iota(jnp.int32, (1, n), 1)


def _rope_table_kernel(pos_ref, inv_ref, sgn_ref, cos_ref, sin_ref):
    ang = pos_ref[...] * inv_ref[...]
    cos_ref[...] = jnp.cos(ang)
    sin_ref[...] = jnp.sin(ang) * sgn_ref[...]


def _rope_tables(pos_col):
    t = pos_col.shape[0]
    half = C_ROPE_DIM // 2
    inv = ROPE_THETA ** (-jnp.arange(half, dtype=F32) / half)
    inv_l = jnp.zeros((1, LANES), F32).at[0, ROPE_LANE:ROPE_LANE + C_ROPE_DIM].set(jnp.tile(inv, 2))
    sgn = np.zeros((1, LANES), np.float32)
    sgn[0, ROPE_LANE:ROPE_LANE + half] = -1.0
    sgn[0, ROPE_LANE + half:ROPE_LANE + C_ROPE_DIM] = 1.0
    tm = ROW_TILE
    return pl.pallas_call(
        _rope_table_kernel,
        grid=(t // tm,),
        in_specs=[pl.BlockSpec((tm, 1), lambda i: (i, 0)), _full((1, LANES)), _full((1, LANES))],
        out_specs=[pl.BlockSpec((tm, LANES), lambda i: (i, 0))] * 2,
        out_shape=[jax.ShapeDtypeStruct((t, LANES), F32)] * 2,
        compiler_params=_cparams("parallel"),
        name="rope_tables",
    )(pos_col, inv_l, jnp.asarray(sgn))


def _rotate(x, cos, sin_signed, lane):
    half = C_ROPE_DIM // 2
    partner = jnp.where(lane < ROPE_LANE + half,
                        pltpu.roll(x, LANES - half, 1), pltpu.roll(x, half, 1))
    return x * cos + partner * sin_signed


def _store_v_transposed(vt_ref, v, n_heads):
    tm = v.shape[0]
    v_t = v.T
    tail = jnp.where(lax.broadcasted_iota(jnp.int32, (VT_ROWS - C_V_DIM, tm), 0) == 0, 1.0, 0.0).astype(BF16)
    for hd in range(n_heads):
        vt_ref[0, hd, 0:C_V_DIM, :] = v_t[hd * C_V_DIM:(hd + 1) * C_V_DIM, :].astype(BF16)
        vt_ref[0, hd, C_V_DIM:VT_ROWS, :] = tail


def _gelu(x):
    return 0.5 * x * (1.0 + lax.erf(x * np.float32(1.0 / np.sqrt(2.0))))


def _inproj_kernel(x_ref, g_ref, w_ref, aq_ref, ak_ref, fb_ref, bvg_ref, ws_ref, bs_ref, onb_ref,
                   cqg_ref, ckvg_ref, gm_ref, tri_ref,
                   qa_ref, ka_ref, vt_ref, bn_ref, cq_ref, ckv_ref, misc_ref,
                   carry_ref, *, tiles_per_seq):
    i = pl.program_id(0)
    tm = x_ref.shape[0]
    h = _rms(x_ref[...]) * g_ref[...]
    proj = jnp.dot(h.astype(BF16), w_ref[...], preferred_element_type=F32)
    gm = gm_ref[...]

    def group_mean(v):
        return _dot_split(v, gm, 2)

    misc = proj[:, SEG_MISC:SEG_MISC + LANES]
    misc_ref[...] = misc
    z = misc + fb_ref[...]
    log_f = jnp.minimum(z, 0.0) - jnp.log1p(jnp.exp(-jnp.abs(z)))

    @pl.when(i % tiles_per_seq == 0)
    def _():
        carry_ref[...] = jnp.zeros_like(carry_ref)

    cum = _dot_split(log_f, tri_ref[...], 3, lhs_is_exact=True) + carry_ref[...]
    carry_ref[...] = cum[tm - 1:tm, :]
    f_hi = (cum * LOG2E).astype(BF16).astype(F32)
    f_rem = cum * LOG2E - f_hi
    f_mid = f_rem.astype(BF16).astype(F32)
    f_lo = f_rem - f_mid

    q = proj[:, SEG_Q:SEG_Q + A_W]
    qn = q * lax.rsqrt(group_mean(q * q) + EPS) * aq_ref[...]
    k = proj[:, SEG_K:SEG_K + A_W]
    kn = k * lax.rsqrt(group_mean(k * k) + EPS) * ak_ref[...]
    lane = _lane_iota()
    for hd in range(A_HEADS):
        pair = slice((hd // 2) * LANES, (hd // 2 + 1) * LANES)
        slab = slice(hd * HEAD_SLAB, (hd + 1) * HEAD_SLAB)
        data = (lane < A_HEAD_DIM) if hd % 2 == 0 else (lane >= A_HEAD_DIM)
        e0 = A_HEAD_DIM if hd % 2 == 0 else 0
        fl = FORGET_LANE + 8 * (hd // 2) + hd % 2
        ones = jnp.where((lane >= e0) & (lane < e0 + 3), 1.0, 0.0)
        qa_ref[:, slab] = jnp.where(data, qn[:, pair], ones).astype(BF16)
        bias = jnp.where(lane == e0, -f_hi[:, fl:fl + 1],
                         jnp.where(lane == e0 + 1, -f_mid[:, fl:fl + 1],
                                   jnp.where(lane == e0 + 2, -f_lo[:, fl:fl + 1], 0.0)))
        ka_ref[:, slab] = jnp.where(data, kn[:, pair], bias).astype(BF16)
    _store_v_transposed(vt_ref, proj[:, SEG_V:SEG_V + A_W], A_HEADS)

    u = _gelu(proj[:, SEG_U:SEG_U + B_W])
    v = _gelu(proj[:, SEG_VB:SEG_VB + B_W])
    dv = v - group_mean(v)
    vn = dv * lax.rsqrt(group_mean(dv * dv) + EPS) * bvg_ref[...]
    group = lax.broadcasted_iota(jnp.int32, (1, B_W), 1) // B_GROUP_DIM
    for w in range(tm // B_WINDOW):
        rows = slice(w * B_WINDOW, (w + 1) * B_WINDOW)
        y_all = jnp.dot(ws_ref[...], vn[rows].astype(BF16), preferred_element_type=F32)
        y = bs_ref[...]
        for g in range(B_GROUPS):
            y = y + jnp.where(group == g, y_all[g * B_WINDOW:(g + 1) * B_WINDOW], 0.0)
        b = u[rows] * y
        bn_ref[rows, :] = (_rms(b) * onb_ref[...]).astype(BF16)

    cq_ref[...] = (_rms(proj[:, SEG_CQ:SEG_CQ + C_Q_RANK]) * cqg_ref[...]).astype(BF16)
    ckv_ref[...] = (_rms(proj[:, SEG_CKV:SEG_CKV + C_KV_RANK]) * ckvg_ref[...]).astype(BF16)


def _inproj(x2d, seq, p):
    t = x2d.shape[0]
    tm = ROW_TILE
    tps = seq // tm
    nb = t // seq
    row = lambda w: pl.BlockSpec((tm, w), lambda i: (i, 0))
    qk_w = A_HEADS * HEAD_SLAB
    out_shape = [
        jax.ShapeDtypeStruct((t, qk_w), BF16), jax.ShapeDtypeStruct((t, qk_w), BF16),
        jax.ShapeDtypeStruct((nb, A_HEADS, VT_ROWS, seq), BF16),
        jax.ShapeDtypeStruct((t, B_W), BF16),
        jax.ShapeDtypeStruct((t, C_Q_RANK), BF16), jax.ShapeDtypeStruct((t, C_KV_RANK), BF16),
        jax.ShapeDtypeStruct((t, LANES), F32),
    ]
    out_specs = [row(qk_w), row(qk_w),
                 pl.BlockSpec((1, A_HEADS, VT_ROWS, tm), lambda i: (i // tps, 0, 0, i % tps)),
                 row(B_W), row(C_Q_RANK), row(C_KV_RANK), row(LANES)]
    consts = [p["mix_g"], p["w_in"], p["aq"], p["ak"], p["fb"], p["bvg"], p["ws"], p["bs"], p["onb"],
              p["cqg"], p["ckvg"], p["gm"], p["tri"]]
    return pl.pallas_call(
        functools.partial(_inproj_kernel, tiles_per_seq=tps),
        grid=(t // tm,),
        in_specs=[row(D_MODEL)] + [_full(c.shape) for c in consts],
        out_specs=out_specs,
        out_shape=out_shape,
        scratch_shapes=[pltpu.VMEM((1, LANES), F32)],
        compiler_params=_cparams("arbitrary"),
        name="in_proj",
    )(x2d, *consts)


def _mla_prep_kernel(cq_ref, ckv_ref, misc_ref, cos_ref, sin_ref, wuq_ref, wuk_ref, wuv_ref,
                     gq_ref, gkn_ref, gkr_ref, qc_ref, kc_ref, vt_ref):
    lane = _lane_iota()
    nope = lane < C_NOPE_DIM
    rope = (lane >= ROPE_LANE) & (lane < ROPE_LANE + C_ROPE_DIM)
    cos, sin = cos_ref[...], sin_ref[...]
    q = jnp.dot(cq_ref[...], wuq_ref[0], preferred_element_type=F32)
    q_partner = jnp.dot(cq_ref[...], wuq_ref[1], preferred_element_type=F32)
    q_cos = gq_ref[0:1, :] * cos
    q_sin = gq_ref[1:2, :] * sin
    kn = jnp.dot(ckv_ref[...], wuk_ref[...], preferred_element_type=F32)
    _store_v_transposed(vt_ref, jnp.dot(ckv_ref[...], wuv_ref[...], preferred_element_type=F32), C_HEADS)

    kr = jnp.where(rope, misc_ref[...], 0.0)
    kr = kr * lax.rsqrt(jnp.sum(kr * kr, axis=-1, keepdims=True) * (1.0 / C_ROPE_DIM) + EPS) * gkr_ref[...]
    kr = _rotate(kr, cos, sin, lane)

    for hd in range(C_HEADS):
        cols = slice(hd * HEAD_SLAB, (hd + 1) * HEAD_SLAB)
        qh = q[:, cols]
        sq = qh * qh
        r_n = lax.rsqrt(jnp.sum(jnp.where(nope, sq, 0.0), axis=-1, keepdims=True) * (1.0 / C_NOPE_DIM) + EPS)
        r_r = lax.rsqrt(jnp.sum(jnp.where(rope, sq, 0.0), axis=-1, keepdims=True) * (1.0 / C_ROPE_DIM) + EPS)
        qc_ref[:, cols] = (jnp.where(nope, r_n, r_r)
                           * (qh * q_cos + q_partner[:, cols] * q_sin)).astype(BF16)
        kh = kn[:, cols]
        r_k = lax.rsqrt(jnp.sum(kh * kh, axis=-1, keepdims=True) * (1.0 / C_NOPE_DIM) + EPS)
        kc_ref[:, cols] = (kh * r_k * gkn_ref[...] + kr).astype(BF16)


def _mla_prep(cq, ckv, misc, cos, sin, seq, p):
    t = cq.shape[0]
    tm = ROW_TILE
    tps = seq // tm
    row = lambda w: pl.BlockSpec((tm, w), lambda i: (i, 0))
    consts = [p["wuq"], p["wuk"], p["wuv"], p["gq"], p["gkn"], p["gkr"]]
    qk_w = C_HEADS * HEAD_SLAB
    return pl.pallas_call(
        _mla_prep_kernel,
        grid=(t // tm,),
        in_specs=[row(C_Q_RANK), row(C_KV_RANK), row(LANES), row(LANES), row(LANES)]
                 + [_full(c.shape) for c in consts],
        out_specs=[row(qk_w), row(qk_w),
                   pl.BlockSpec((1, C_HEADS, VT_ROWS, tm), lambda i: (i // tps, 0, 0, i % tps))],
        out_shape=[jax.ShapeDtypeStruct((t, qk_w), BF16), jax.ShapeDtypeStruct((t, qk_w), BF16),
                   jax.ShapeDtypeStruct((t // seq, C_HEADS, VT_ROWS, seq), BF16)],
        compiler_params=_cparams("parallel"),
        name="mla_prep",
    )(cq, ckv, misc, cos, sin, *consts)


def _attn_items(nq):
    items = [(i, j) for i in range(nq) for j in range(i)] + [(i, i) for i in range(nq)]
    return np.array(items, np.int32).T


def _attn_kernel(items_ref, q_ref, k_ref, vt_ref, mask_ref, o_ref, s0_ref, s1_ref, p0_ref, p1_ref,
                 mp0_ref, mp1_ref, mrun_ref, macc_ref, acc_ref, *, n_items):
    tk, tq = mask_ref.shape
    nq = q_ref.shape[0] // tq
    s_bufs, p_bufs, mp_bufs = (s0_ref, s1_ref), (p0_ref, p1_ref), (mp0_ref, mp1_ref)
    mrun_ref[...] = jnp.full(mrun_ref.shape, NEG_INF, F32)
    macc_ref[...] = jnp.full(macc_ref.shape, NEG_INF, F32)
    acc_ref[...] = jnp.zeros(acc_ref.shape, F32)

    def scores(it, buf, masked):
        qi = items_ref[0, it]
        q_start = pl.multiple_of(qi * tq, tq)
        k_start = pl.multiple_of(items_ref[1, it] * tk, tk)
        for hh in range(2):
            cols = slice(hh * HEAD_SLAB, (hh + 1) * HEAD_SLAB)
            s_t = lax.dot_general(k_ref[pl.ds(k_start, tk), cols], q_ref[pl.ds(q_start, tq), cols],
                                  (((1,), (1,)), ((), ())), preferred_element_type=F32)
            if masked:
                s_t = s_t + mask_ref[...]
            s_bufs[buf][hh] = s_t
            mrun_ref[qi, hh] = jnp.maximum(mrun_ref[qi, hh], jnp.max(s_t, axis=0, keepdims=True))

    def exponentiate(it, buf):
        qi = items_ref[0, it]
        for hh in range(2):
            m = mrun_ref[qi, hh]
            p_bufs[buf][hh] = jnp.exp2(s_bufs[buf][hh] - m).astype(BF16)
            mp_bufs[buf][hh] = m

    def accumulate(it, buf):
        qi = items_ref[0, it]
        k_start = pl.multiple_of(items_ref[1, it] * tk, tk)
        for hh in range(2):
            m = mp_bufs[buf][hh]
            pv = jnp.dot(vt_ref[0, hh, :, pl.ds(k_start, tk)], p_bufs[buf][hh],
                         preferred_element_type=F32)
            acc_ref[qi, hh] = jnp.exp2(macc_ref[qi, hh] - m) * acc_ref[qi, hh] + pv
            macc_ref[qi, hh] = m

    def beat(it, par, masked):
        if not isinstance(it, int) or 2 <= it <= n_items + 1:
            accumulate(it - 2, par)
        if not isinstance(it, int) or 1 <= it <= n_items:
            exponentiate(it - 1, 1 - par)
        if not isinstance(it, int) or it < n_items:
            scores(it, par, masked)

    def run_beats(lo, hi, masked):
        lo = max(lo, 0)
        if lo < hi and (lo % 2 == 1 or lo < 2):
            for it in range(lo, min(hi, lo + 2 - lo % 2)):
                beat(it, it % 2, masked)
            lo = min(hi, lo + 2 - lo % 2)
        n_pairs = (hi - lo) // 2
        if n_pairs > 0:
            def pair(t, carry, lo=lo):
                beat(lo + 2 * t, 0, masked)
                beat(lo + 2 * t + 1, 1, masked)
                return carry
            lax.fori_loop(0, n_pairs, pair, 0)
        for it in range(lo + 2 * n_pairs, hi):
            beat(it, it % 2, masked)

    n_full = n_items - nq
    run_beats(0, n_full, False)
    run_beats(n_full, n_items, True)
    for it in (n_items, n_items + 1):
        beat(it, it % 2, True)

    for qi in range(nq):
        halves = [acc_ref[qi, hh, 0:C_V_DIM, :] / acc_ref[qi, hh, C_V_DIM:C_V_DIM + 1, :] for hh in range(2)]
        o_ref[qi * tq:(qi + 1) * tq, :] = jnp.concatenate(halves, axis=0).T.astype(o_ref.dtype)


def _attention(q, k, vt, *, unit, name):
    t = q.shape[0]
    nb, n_heads, _, seq = vt.shape
    tq = ATTN_TILE
    nq = seq // tq
    items = _attn_items(nq)
    pos = np.arange(tq)
    diag_mask = np.where((pos[:, None] // unit) <= (pos[None, :] // unit), 0.0, NEG_INF)
    mask = jnp.asarray(diag_mask.astype(np.float32))
    seq_blk = lambda w: pl.BlockSpec((seq, w), lambda b, p, items_ref: (b, p))
    grid_spec = pltpu.PrefetchScalarGridSpec(
        num_scalar_prefetch=1,
        grid=(nb, n_heads // 2),
        in_specs=[seq_blk(2 * HEAD_SLAB), seq_blk(2 * HEAD_SLAB),
                  pl.BlockSpec((1, 2, VT_ROWS, seq), lambda b, p, items_ref: (b, p, 0, 0)),
                  pl.BlockSpec((tq, tq), lambda b, p, items_ref: (0, 0), pipeline_mode=pl.Buffered(1))],
        out_specs=seq_blk(2 * C_V_DIM),
        scratch_shapes=[pltpu.VMEM((2, tq, tq), F32), pltpu.VMEM((2, tq, tq), F32),
                        pltpu.VMEM((2, tq, tq), BF16), pltpu.VMEM((2, tq, tq), BF16),
                        pltpu.VMEM((2, 1, tq), F32), pltpu.VMEM((2, 1, tq), F32),
                        pltpu.VMEM((nq, 2, 1, tq), F32), pltpu.VMEM((nq, 2, 1, tq), F32),
                        pltpu.VMEM((nq, 2, VT_ROWS, tq), F32)])
    return pl.pallas_call(
        functools.partial(_attn_kernel, n_items=items.shape[1]),
        grid_spec=grid_spec,
        out_shape=jax.ShapeDtypeStruct((t, n_heads * C_V_DIM), BF16),
        compiler_params=_cparams("parallel", "parallel"),
        name=name,
    )(jnp.asarray(items), q, k, vt, mask)


def _mem_kv_kernel(mem_ref, g_ref, w_ref, kg_ref, k_ref, v_ref):
    mn = (_rms(mem_ref[0]) * g_ref[...]).astype(BF16)
    kv = jnp.dot(mn, w_ref[...], preferred_element_type=F32)
    for hd in range(M_HEADS):
        cols = slice(hd * M_HEAD_DIM, (hd + 1) * M_HEAD_DIM)
        k_ref[0, :, cols] = (_rms(kv[:, cols]) * kg_ref[...]).astype(BF16)
    v_ref[0] = kv[:, M_W:].astype(BF16)


def _mem_kv(mem, p):
    nb, ml, _ = mem.shape
    consts = [p["mem_g"], p["w_mem_kv"], p["mkg"]]
    blk = pl.BlockSpec((1, ml, M_W), lambda b: (b, 0, 0))
    return pl.pallas_call(
        _mem_kv_kernel,
        grid=(nb,),
        in_specs=[pl.BlockSpec((1, ml, D_MODEL), lambda b: (b, 0, 0))] + [_full(c.shape) for c in consts],
        out_specs=[blk, blk],
        out_shape=[jax.ShapeDtypeStruct((nb, ml, M_W), BF16)] * 2,
        compiler_params=_cparams("parallel"),
        name="mem_kv",
    )(mem, *consts)


def _outproj_kernel(x_ref, a_ref, bn_ref, c_ref, ona_ref, onc_ref, wo_ref, xg_ref, wq_ref, mqg_ref,
                    km_ref, vm_ref, wmo_ref, o_ref):
    a_n = (_rms(a_ref[...].astype(F32)) * ona_ref[...]).astype(BF16)
    c_n = (_rms(c_ref[...].astype(F32)) * onc_ref[...]).astype(BF16)
    mix = jnp.concatenate([a_n, bn_ref[...], c_n], axis=-1)
    x1 = x_ref[...] + jnp.dot(mix, wo_ref[...], preferred_element_type=F32)

    h = (_rms(x1) * xg_ref[...]).astype(BF16)
    q = jnp.dot(h, wq_ref[...], preferred_element_type=F32)
    outs = []
    for hd in range(M_HEADS):
        cols = slice(hd * M_HEAD_DIM, (hd + 1) * M_HEAD_DIM)
        qh = (_rms(q[:, cols]) * mqg_ref[...]).astype(BF16)
        s = lax.dot_general(qh, km_ref[0, :, cols], (((1,), (1,)), ((), ())), preferred_element_type=F32)
        e = jnp.exp(s - jnp.max(s, axis=-1, keepdims=True))
        pr = e / jnp.sum(e, axis=-1, keepdims=True)
        outs.append(jnp.dot(pr.astype(BF16), vm_ref[0, :, cols], preferred_element_type=F32).astype(BF16))
    o_ref[...] = x1 + jnp.dot(jnp.concatenate(outs, axis=-1), wmo_ref[...], preferred_element_type=F32)


def _outproj(x2d, a, bn, c, km, vm, seq, p):
    t = x2d.shape[0]
    tm = ROW_TILE
    tps = seq // tm
    ml = km.shape[1]
    row = lambda w: pl.BlockSpec((tm, w), lambda i: (i, 0))
    memblk = pl.BlockSpec((1, ml, M_W), lambda i: (i // tps, 0, 0))
    c1 = [p["ona"], p["onc"], p["w_out"], p["xg"], p["w_mem_q"], p["mqg"]]
    return pl.pallas_call(
        _outproj_kernel,
        grid=(t // tm,),
        in_specs=[row(D_MODEL), row(A_W), row(B_W), row(C_W)] + [_full(c_.shape) for c_ in c1]
                 + [memblk, memblk, _full(p["w_mem_out"].shape)],
        out_specs=row(D_MODEL),
        out_shape=jax.ShapeDtypeStruct((t, D_MODEL), F32),
        compiler_params=_cparams("parallel"),
        name="out_proj_mem_attn",
    )(x2d, a, bn, c, *c1, km, vm, p["w_mem_out"])


def _silu(x):
    return x * jax.nn.sigmoid(x)


def _ffn_kernel(x_ref, g_ref, wg_ref, wu_ref, wd_ref, o_ref, *, n_chunks):
    x = x_ref[...]
    h = (_rms(x) * g_ref[...]).astype(BF16)
    fc = wg_ref.shape[1] // n_chunks
    acc = x
    for c in range(n_chunks):
        cols = slice(c * fc, (c + 1) * fc)
        act = _silu(jnp.dot(h, wg_ref[:, cols], preferred_element_type=F32)) * \
            jnp.dot(h, wu_ref[:, cols], preferred_element_type=F32)
        acc = acc + jnp.dot(act.astype(BF16), wd_ref[cols, :], preferred_element_type=F32)
    o_ref[...] = acc


def _ffn(x2d, g, wg, wu, wd):
    t = x2d.shape[0]
    tm = ROW_TILE
    row = pl.BlockSpec((tm, D_MODEL), lambda i: (i, 0))
    resident = lambda a: pl.BlockSpec(a.shape, lambda i: (0, 0), pipeline_mode=pl.Buffered(1))
    return pl.pallas_call(
        functools.partial(_ffn_kernel, n_chunks=2),
        grid=(t // tm,),
        in_specs=[row, _full(g.shape), resident(wg), resident(wu), resident(wd)],
        out_specs=row,
        out_shape=jax.ShapeDtypeStruct((t, D_MODEL), F32),
        compiler_params=_cparams("parallel"),
        name="ffn_dense",
    )(x2d, g, wg, wu, wd)


def _moe_kernel(x_hbm_ref, g_ref, wr_ref, br_ref, tri_ref, wg_ref, wu_ref, wd_ref, o_ref,
                h_ref, comb_ref, rank_ref, rank_t_ref, sem_ref):
    i = pl.program_id(0)
    e = pl.program_id(1)
    lane = _lane_iota()
    cap = MOE_CAP

    @pl.when(e == 0)
    def _():
        rows = o_ref.shape[0]
        load = pltpu.make_async_copy(x_hbm_ref.at[pl.ds(i * rows, rows)], o_ref, sem_ref)
        load.start()
        load.wait()
        h = _rms(o_ref[...]) * g_ref[...]
        h_hi = h.astype(BF16)
        h_ref[...] = h_hi
        h_lo = (h - h_hi.astype(F32)).astype(BF16)
        logits = (jnp.dot(h_hi, wr_ref[0], preferred_element_type=F32)
                  + jnp.dot(h_lo, wr_ref[0], preferred_element_type=F32)
                  + jnp.dot(h_hi, wr_ref[1], preferred_element_type=F32)) + br_ref[...]
        logits = jnp.where(lane < N_EXPERTS, logits, -jnp.inf)
        v1 = jnp.max(logits, axis=-1, keepdims=True)
        i1 = jnp.min(jnp.where(logits == v1, lane, LANES), axis=-1, keepdims=True)
        rest = jnp.where(lane == i1, -jnp.inf, logits)
        v2 = jnp.max(rest, axis=-1, keepdims=True)
        i2 = jnp.min(jnp.where(rest == v2, lane, LANES), axis=-1, keepdims=True)
        e2 = jnp.exp(v2 - v1)
        g1 = 1.0 / (1.0 + e2)
        comb_ref[...] = jnp.where(lane == i1, g1, 0.0) + jnp.where(lane == i2, e2 * g1, 0.0)
        hit = (lane == i1) | (lane == i2)
        before = jnp.dot(tri_ref[...], jnp.where(hit, 1.0, 0.0).astype(BF16), preferred_element_type=F32)
        rank = jnp.where(hit, before, -1.0)
        rank_ref[...] = rank
        rank_t_ref[...] = rank.T[0:N_EXPERTS, :]

    pick = lane == e
    rank_col = jnp.sum(jnp.where(pick, rank_ref[...], 0.0), axis=-1, keepdims=True)
    gate_col = jnp.sum(jnp.where(pick, comb_ref[...], 0.0), axis=-1, keepdims=True)
    rank_row = rank_t_ref[pl.ds(e, 1), :]
    n_tokens = jnp.max(rank_col).astype(jnp.int32) + 1

    def block(b, rows):
        r0 = (b * cap).astype(F32)
        slot_col = lax.broadcasted_iota(jnp.int32, (rows, 1), 0).astype(F32)
        slot_row = lax.broadcasted_iota(jnp.int32, (1, rows), 1).astype(F32)
        sel = jnp.where(rank_row - r0 == slot_col, 1.0, 0.0).astype(BF16)
        sel_t = jnp.where(rank_col - r0 == slot_row, 1.0, 0.0).astype(BF16)
        xe = jnp.dot(sel, h_ref[...], preferred_element_type=F32).astype(BF16)
        act = _silu(jnp.dot(xe, wg_ref[0], preferred_element_type=F32)) * \
            jnp.dot(xe, wu_ref[0], preferred_element_type=F32)
        y = jnp.dot(act.astype(BF16), wd_ref[0], preferred_element_type=F32).astype(BF16)
        o_ref[...] += gate_col * jnp.dot(sel_t, y, preferred_element_type=F32)

    n_full = n_tokens // cap
    left = n_tokens - n_full * cap

    def full_block(b, carry):
        block(b, cap)
        return carry

    lax.fori_loop(0, n_full, full_block, 0)

    @pl.when(left > MOE_TAIL)
    def _():
        block(n_full, cap)

    @pl.when((left > 0) & (left <= MOE_TAIL))
    def _():
        block(n_full, MOE_TAIL)


def _moe(x2d, g, wr, br, wg, wu, wd):
    t = x2d.shape[0]
    chunk = tm = MOE_CHUNK
    ff = wg.shape[2]
    tri = jnp.asarray(np.tril(np.ones((chunk, chunk), np.float32), -1)).astype(BF16)
    return pl.pallas_call(
        _moe_kernel,
        grid=(t // tm, N_EXPERTS),
        in_specs=[pl.BlockSpec(memory_space=pl.ANY),
                  _full(g.shape), _full(wr.shape), _full(br.shape),
                  pl.BlockSpec((chunk, chunk), lambda i, e: (0, 0), pipeline_mode=pl.Buffered(1)),
                  pl.BlockSpec((1, D_MODEL, ff), lambda i, e: (e, 0, 0)),
                  pl.BlockSpec((1, D_MODEL, ff), lambda i, e: (e, 0, 0)),
                  pl.BlockSpec((1, ff, D_MODEL), lambda i, e: (e, 0, 0))],
        out_specs=pl.BlockSpec((tm, D_MODEL), lambda i, e: (i, 0)),
        out_shape=jax.ShapeDtypeStruct((t, D_MODEL), F32),
        scratch_shapes=[pltpu.VMEM((tm, D_MODEL), BF16), pltpu.VMEM((tm, LANES), F32),
                        pltpu.VMEM((tm, LANES), F32), pltpu.VMEM((N_EXPERTS, chunk), F32),
                        pltpu.SemaphoreType.DMA(())],
        compiler_params=_cparams("parallel", "arbitrary"),
        name="moe_routed",
    )(x2d, g, wr, br, tri, wg, wu, wd)


def _pack_bf16_pairs(lo, hi):
    lo_bits = pltpu.bitcast(lo.astype(BF16).astype(F32), jnp.uint32)
    hi_bits = pltpu.bitcast(hi.astype(BF16).astype(F32), jnp.uint32)
    return lax.shift_right_logical(lo_bits, jnp.uint32(16)) | (hi_bits & jnp.uint32(0xFFFF0000))


def _unpack_bf16_pairs(words):
    lo = pltpu.bitcast(lax.shift_left(words, jnp.uint32(16)), F32)
    hi = pltpu.bitcast(words & jnp.uint32(0xFFFF0000), F32)
    return lo, hi


def _pack_rows(v):
    q = D_MODEL // 4
    return _pack_bf16_pairs(v[:, 0:q], v[:, q:2 * q]), _pack_bf16_pairs(v[:, 2 * q:3 * q], v[:, 3 * q:])


def _unpack_rows(a, b):
    return jnp.concatenate([*_unpack_bf16_pairs(a), *_unpack_bf16_pairs(b)], axis=-1)


def _route_kernel(x_ref, g_ref, wr_ref, br_ref, tri_ref, ha_ref, hb_ref, route_ref, cnt_ref):
    lane = _lane_iota()
    h = _rms(x_ref[...]) * g_ref[...]
    ha_ref[...], hb_ref[...] = _pack_rows(h)
    h_hi = h.astype(BF16)
    h_lo = (h - h_hi.astype(F32)).astype(BF16)
    logits = (jnp.dot(h_hi, wr_ref[0], preferred_element_type=F32)
              + jnp.dot(h_lo, wr_ref[0], preferred_element_type=F32)
              + jnp.dot(h_hi, wr_ref[1], preferred_element_type=F32)) + br_ref[...]
    logits = jnp.where(lane < N_EXPERTS, logits, -jnp.inf)
    v1 = jnp.max(logits, axis=-1, keepdims=True)
    i1 = jnp.min(jnp.where(logits == v1, lane, LANES), axis=-1, keepdims=True)
    rest = jnp.where(lane == i1, -jnp.inf, logits)
    v2 = jnp.max(rest, axis=-1, keepdims=True)
    i2 = jnp.min(jnp.where(rest == v2, lane, LANES), axis=-1, keepdims=True)
    e2 = jnp.exp(v2 - v1)
    g1 = 1.0 / (1.0 + e2)
    hit1, hit2 = lane == i1, lane == i2
    ones = jnp.where(hit1 | hit2, 1.0, 0.0)
    before = jnp.dot(tri_ref[...], ones.astype(BF16), preferred_element_type=F32)
    r1 = jnp.sum(jnp.where(hit1, before, 0.0), axis=-1, keepdims=True)
    r2 = jnp.sum(jnp.where(hit2, before, 0.0), axis=-1, keepdims=True)
    cols = [i1.astype(F32), i2.astype(F32), g1, e2 * g1, r1, r2]
    route = jnp.zeros(route_ref.shape, F32)
    for n, c in enumerate(cols):
        route = jnp.where(lane == n, c, route)
    route_ref[...] = route
    cnt_ref[0] = jnp.broadcast_to(jnp.sum(ones, axis=0, keepdims=True), cnt_ref.shape[1:])


def _route(x2d, g, wr, br):
    t = x2d.shape[0]
    tm = MOE_CHUNK
    tri = jnp.asarray(np.tril(np.ones((tm, tm), np.float32), -1)).astype(BF16)
    row = lambda w: pl.BlockSpec((tm, w), lambda i: (i, 0))
    q = D_MODEL // 4
    return pl.pallas_call(
        _route_kernel,
        grid=(t // tm,),
        in_specs=[row(D_MODEL), _full(g.shape), _full(wr.shape), _full(br.shape), _full(tri.shape)],
        out_specs=[row(q), row(q), row(LANES), pl.BlockSpec((1, 8, LANES), lambda i: (i, 0, 0))],
        out_shape=[jax.ShapeDtypeStruct((t, q), jnp.uint32), jax.ShapeDtypeStruct((t, q), jnp.uint32),
                   jax.ShapeDtypeStruct((t, LANES), F32), jax.ShapeDtypeStruct((t // tm, 8, LANES), F32)],
        compiler_params=_cparams("parallel"),
        name="moe_route",
    )(x2d, g, wr, br, tri)


SC_WINDOW = 128


def _sc_mesh():
    return plsc.VectorSubcoreMesh(core_axis_name="core", subcore_axis_name="subcore")


def _sc_scatter_rows(x, idx, n_out):
    n, d = x.shape
    m = idx.shape[1]
    nblk = n // SC_WINDOW

    @pl.kernel(out_type=jax.ShapeDtypeStruct((n_out, d), x.dtype), mesh=_sc_mesh(), name="moe_sc_scatter")
    def scatter(x_hbm, i_hbm, o_hbm):
        def body(x_vmem, i_vmem):
            pltpu.sync_copy(x_vmem, o_hbm.at[i_vmem.at[0]])

        pltpu.emit_pipeline(
            body,
            grid=(m // SC_WINDOW,),
            in_specs=[pl.BlockSpec((SC_WINDOW, d), lambda j: (j % nblk, 0)),
                      pl.BlockSpec((1, SC_WINDOW), lambda j: (0, j))],
            out_specs=[],
            core_axis_name="subcore",
            dimension_semantics=(pltpu.PARALLEL,),
        )(x_hbm, i_hbm)

    return scatter(x, idx)


def _sc_gather_rows(table, idx):
    d = table.shape[1]
    m = idx.shape[1]

    @pl.kernel(out_type=jax.ShapeDtypeStruct((m, d), table.dtype), mesh=_sc_mesh(), name="moe_sc_gather")
    def gather(t_hbm, i_hbm, o_hbm):
        def body(i_vmem, o_vmem):
            pltpu.sync_copy(t_hbm.at[i_vmem.at[0]], o_vmem)

        pltpu.emit_pipeline(
            body,
            grid=(m // SC_WINDOW,),
            in_specs=[pl.BlockSpec((1, SC_WINDOW), lambda j: (0, j))],
            out_specs=[pl.BlockSpec((SC_WINDOW, d), lambda j: (j, 0))],
            core_axis_name="subcore",
            dimension_semantics=(pltpu.PARALLEL,),
        )(i_hbm, o_hbm)

    return gather(table, idx)


def _expert_kernel(blk_expert_ref, n_used_ref, xa_ref, xb_ref, wg_ref, wu_ref, wd_ref, ya_ref, yb_ref):
    del blk_expert_ref

    @pl.when(pl.program_id(0) < n_used_ref[0])
    def _():
        xe = _unpack_rows(xa_ref[...], xb_ref[...]).astype(BF16)
        act = _silu(jnp.dot(xe, wg_ref[0], preferred_element_type=F32)) * \
            jnp.dot(xe, wu_ref[0], preferred_element_type=F32)
        y = jnp.dot(act.astype(BF16), wd_ref[0], preferred_element_type=F32)
        ya_ref[...], yb_ref[...] = _pack_rows(y)

    @pl.when(pl.program_id(0) >= n_used_ref[0])
    def _():
        ya_ref[...] = jnp.zeros(ya_ref.shape, ya_ref.dtype)
        yb_ref[...] = jnp.zeros(yb_ref.shape, yb_ref.dtype)


def _experts(blk_expert, n_used, xa, xb, wg, wu, wd):
    n_rows, q = xa.shape
    ff = wg.shape[2]
    blk = MOE_BLOCK
    row = pl.BlockSpec((blk, q), lambda b, be, nu: (b, 0))
    grid_spec = pltpu.PrefetchScalarGridSpec(
        num_scalar_prefetch=2,
        grid=(n_rows // blk,),
        in_specs=[row, row,
                  pl.BlockSpec((1, D_MODEL, ff), lambda b, be, nu: (be[b], 0, 0)),
                  pl.BlockSpec((1, D_MODEL, ff), lambda b, be, nu: (be[b], 0, 0)),
                  pl.BlockSpec((1, ff, D_MODEL), lambda b, be, nu: (be[b], 0, 0))],
        out_specs=[row, row])
    return pl.pallas_call(
        _expert_kernel,
        grid_spec=grid_spec,
        out_shape=[jax.ShapeDtypeStruct((n_rows, q), jnp.uint32)] * 2,
        compiler_params=_cparams("arbitrary"),
        name="moe_experts",
    )(blk_expert, n_used, xa, xb, wg, wu, wd)


def _combine_kernel(x_ref, route_ref, a1_ref, b1_ref, a2_ref, b2_ref, o_ref):
    g1 = route_ref[:, 2:3]
    g2 = route_ref[:, 3:4]
    o_ref[...] = x_ref[...] + g1 * _unpack_rows(a1_ref[...], b1_ref[...]) \
        + g2 * _unpack_rows(a2_ref[...], b2_ref[...])


def _combine(x2d, route, ya, yb):
    t = x2d.shape[0]
    tm = ROW_TILE
    nt = t // tm
    q = ya.shape[1]
    row = lambda w: pl.BlockSpec((tm, w), lambda i: (i, 0))
    first = pl.BlockSpec((tm, q), lambda i: (i, 0))
    second = pl.BlockSpec((tm, q), lambda i: (nt + i, 0))
    return pl.pallas_call(
        _combine_kernel,
        grid=(nt,),
        in_specs=[row(D_MODEL), row(LANES), first, first, second, second],
        out_specs=row(D_MODEL),
        out_shape=jax.ShapeDtypeStruct((t, D_MODEL), F32),
        compiler_params=_cparams("parallel"),
        name="moe_combine",
    )(x2d, route, ya, yb, ya, yb)


def _moe_sorted(x2d, g, wr, br, wg, wu, wd):
    t = x2d.shape[0]
    blk = MOE_BLOCK
    n_blocks = 2 * t // blk + N_EXPERTS
    ha, hb, route, cnt = _route(x2d, g, wr, br)

    cnt = cnt[:, 0, :N_EXPERTS].astype(jnp.int32)
    before_chunk = jnp.cumsum(cnt, axis=0) - cnt
    seg_blocks = (jnp.sum(cnt, axis=0) + blk - 1) // blk
    seg_end_blk = jnp.cumsum(seg_blocks)
    seg_start = (seg_end_blk - seg_blocks) * blk
    chunk_of = jnp.arange(t, dtype=jnp.int32) // MOE_CHUNK

    def dest(e_col, r_col):
        e = route[:, e_col].astype(jnp.int32)
        return seg_start[e] + before_chunk[chunk_of, e] + route[:, r_col].astype(jnp.int32)

    idx = jnp.concatenate([dest(0, 4), dest(1, 5)])[None]
    blk_expert = jnp.minimum(jnp.searchsorted(seg_end_blk, jnp.arange(n_blocks, dtype=jnp.int32), side="right"),
                             N_EXPERTS - 1).astype(jnp.int32)
    n_used = seg_end_blk[-1:].astype(jnp.int32)

    xa = _sc_scatter_rows(ha, idx, n_blocks * blk)
    xb = _sc_scatter_rows(hb, idx, n_blocks * blk)
    ya, yb = _experts(blk_expert, n_used, xa, xb, wg, wu, wd)
    return _combine(x2d, route, _sc_gather_rows(ya, idx), _sc_gather_rows(yb, idx))


def _pad_cols(w, width):
    return jnp.pad(w, ((0, 0), (0, width - w.shape[1])))


def _layer_params(l, mix_norm, w_in, b_forget, a_q_norm, a_k_norm, b_v_norm, b_spatial_w, b_spatial_b,
                  c_q_lat_norm, c_w_uq, c_kv_lat_norm, c_w_ukv, c_q_nope_norm, c_q_rope_norm,
                  c_k_nope_norm, c_k_rope_norm, out_norm_a, out_norm_b, out_norm_c, w_out,
                  xattn_norm, mem_norm, w_mem_q, w_mem_kv, m_q_norm, m_k_norm, w_mem_out):
    p = {}
    o = np.cumsum((0, A_W, A_W, A_W, A_HEADS, B_W, B_W, C_Q_RANK, C_KV_RANK, C_ROPE_DIM))
    w = w_in[l]
    seg = lambda n: w[:, o[n]:o[n + 1]]
    fa = seg(3)
    misc = jnp.zeros((D_MODEL, LANES), F32)
    misc = misc.at[:, ROPE_LANE:ROPE_LANE + C_ROPE_DIM].set(seg(8))
    fb = jnp.zeros((1, LANES), F32)
    for hd in range(A_HEADS):
        ln = FORGET_LANE + 8 * (hd // 2) + hd % 2
        misc = misc.at[:, ln].set(fa[:, hd])
        fb = fb.at[0, ln].set(b_forget[l, hd])
    p["w_in"] = jnp.concatenate([seg(0), seg(1), seg(2), seg(4), seg(5), seg(6), seg(7), misc], axis=1).astype(BF16)
    p["fb"] = fb
    p["mix_g"] = mix_norm[l][None]
    p["aq"] = jnp.tile(a_q_norm[l], A_HEADS)[None] * (A_HEAD_DIM ** -0.5 * LOG2E)
    p["ak"] = jnp.tile(a_k_norm[l], A_HEADS)[None]
    p["bvg"] = b_v_norm[l][None]
    pos = np.arange(B_WINDOW)
    mask = (pos[None, :] // CHUNK) <= (pos[:, None] // CHUNK)
    p["ws"] = jnp.where(mask[None], b_spatial_w[l], 0.0).reshape(B_GROUPS * B_WINDOW, B_WINDOW).astype(BF16)
    p["bs"] = jnp.repeat(b_spatial_b[l].T, B_GROUP_DIM, axis=1)
    p["onb"] = out_norm_b[l][None]
    p["cqg"] = c_q_lat_norm[l][None]
    p["ckvg"] = c_kv_lat_norm[l][None]
    gidx = np.arange(A_W) // A_HEAD_DIM
    p["gm"] = jnp.asarray((gidx[:, None] == gidx[None, :]).astype(np.float32) / A_HEAD_DIM).astype(BF16)
    p["tri"] = jnp.asarray(np.tril(np.ones((ROW_TILE, ROW_TILE), np.float32))).astype(BF16)

    qd = C_NOPE_DIM + C_ROPE_DIM
    half = C_ROPE_DIM // 2
    wq = c_w_uq[l]
    wq_partner = jnp.concatenate([jnp.zeros_like(wq[:, :, :C_NOPE_DIM]), wq[:, :, C_NOPE_DIM + half:],
                                  wq[:, :, C_NOPE_DIM:C_NOPE_DIM + half]], axis=-1)
    p["wuq"] = jnp.pad(jnp.stack([wq, wq_partner]), ((0, 0), (0, 0), (0, 0), (0, HEAD_SLAB - qd))
                       ).reshape(2, C_Q_RANK, -1).astype(BF16)
    wukv = c_w_ukv[l]
    p["wuk"] = jnp.pad(wukv[:, :, :C_NOPE_DIM], ((0, 0), (0, 0), (0, HEAD_SLAB - C_NOPE_DIM))
                       ).reshape(C_KV_RANK, -1).astype(BF16)
    p["wuv"] = wukv[:, :, C_NOPE_DIM:].reshape(C_KV_RANK, C_W).astype(BF16)
    gq = jnp.concatenate([c_q_nope_norm[l], c_q_rope_norm[l]])
    gq_partner = jnp.concatenate([jnp.zeros_like(c_q_nope_norm[l]), c_q_rope_norm[l][half:],
                                  c_q_rope_norm[l][:half]])
    p["gq"] = _pad_cols(jnp.stack([gq, gq_partner]) * (qd ** -0.5 * LOG2E), LANES)
    p["gkn"] = _pad_cols(c_k_nope_norm[l][None], LANES)
    p["gkr"] = jnp.zeros((1, LANES), F32).at[0, ROPE_LANE:ROPE_LANE + C_ROPE_DIM].set(c_k_rope_norm[l])

    p["ona"] = out_norm_a[l][None]
    p["onc"] = out_norm_c[l][None]
    p["w_out"] = w_out[l].astype(BF16)
    p["xg"] = xattn_norm[l][None]
    p["w_mem_q"] = w_mem_q[l].astype(BF16)
    p["mqg"] = m_q_norm[l][None] * (M_HEAD_DIM ** -0.5)
    p["mem_g"] = mem_norm[l][None]
    p["w_mem_kv"] = w_mem_kv[l].astype(BF16)
    p["mkg"] = m_k_norm[l][None]
    p["w_mem_out"] = w_mem_out[l].astype(BF16)
    return p


def kernel(x, mem, positions, mix_norm, w_in, b_forget, a_q_norm, a_k_norm, b_v_norm, b_spatial_w, b_spatial_b, c_q_lat_norm, c_w_uq, c_kv_lat_norm, c_w_ukv, c_q_nope_norm, c_q_rope_norm, c_k_nope_norm, c_k_rope_norm, out_norm_a, out_norm_b, out_norm_c, w_out, xattn_norm, mem_norm, w_mem_q, w_mem_kv, m_q_norm, m_k_norm, w_mem_out, ffn_norm, ffn_w_gate, ffn_w_up, ffn_w_down, w_router, b_router, moe_w_gate, moe_w_up, moe_w_down):
    nb, seq, d = x.shape
    assert d == D_MODEL and seq % ROW_TILE == 0 and seq % ATTN_TILE == 0 and (nb * seq) % MOE_CHUNK == 0
    depth = w_in.shape[0]
    t = nb * seq
    x2d = x.reshape(t, d)
    cos, sin = _rope_tables(positions.reshape(t, 1).astype(F32))

    for l in range(depth):
        p = _layer_params(l, mix_norm, w_in, b_forget, a_q_norm, a_k_norm, b_v_norm, b_spatial_w,
                          b_spatial_b, c_q_lat_norm, c_w_uq, c_kv_lat_norm, c_w_ukv, c_q_nope_norm,
                          c_q_rope_norm, c_k_nope_norm, c_k_rope_norm, out_norm_a, out_norm_b,
                          out_norm_c, w_out, xattn_norm, mem_norm, w_mem_q, w_mem_kv, m_q_norm,
                          m_k_norm, w_mem_out)
        qa, ka, vta, bn, cq, ckv, misc = _inproj(x2d, seq, p)
        qc, kc, vtc = _mla_prep(cq, ckv, misc, cos, sin, seq, p)
        a = _attention(qa, ka, vta, unit=1, name="attn_fox")
        c = _attention(qc, kc, vtc, unit=CHUNK, name="attn_mla")
        km, vm = _mem_kv(mem, p)
        x2d = _outproj(x2d, a, bn, c, km, vm, seq, p)
        g = ffn_norm[l][None]
        if l % 2 == 0:
            m = l // 2
            ff = ffn_w_gate.shape[2]
            ff_pad = -(-ff // (2 * LANES)) * (2 * LANES)
            wg = _pad_cols(ffn_w_gate[m], ff_pad).astype(BF16)
            wu = _pad_cols(ffn_w_up[m], ff_pad).astype(BF16)
            wd = jnp.pad(ffn_w_down[m], ((0, ff_pad - ff), (0, 0))).astype(BF16)
            x2d = _ffn(x2d, g, wg, wu, wd)
        else:
            m = l // 2
            wr = _pad_cols(w_router[m], LANES)
            wr_hi = wr.astype(BF16)
            wr = jnp.stack([wr_hi, (wr - wr_hi.astype(F32)).astype(BF16)])
            br = _pad_cols(b_router[m][None], LANES)
            x2d = _moe_sorted(x2d, g, wr, br, moe_w_gate[m].astype(BF16), moe_w_up[m].astype(BF16),
                       moe_w_down[m].astype(BF16))
    return x2d.reshape(nb, seq, d)
```

```python
import functools

import numpy as np
import jax
import jax.numpy as jnp
from jax import lax
from jax.experimental import pallas as pl
from jax.experimental.pallas import tpu as pltpu
from jax.experimental.pallas import tpu_sc as plsc

F32 = jnp.float32
BF16 = jnp.bfloat16
HIGHEST = lax.Precision.HIGHEST

D_MODEL = 1024
CHUNK = 64
EPS = 1e-6
NEG_INF = -1e30
A_HEADS, A_HEAD_DIM = 4, 64
B_GROUPS, B_GROUP_DIM, B_WINDOW = 4, 64, 128
C_HEADS, C_NOPE_DIM, C_ROPE_DIM, C_V_DIM = 8, 64, 32, 64
C_Q_RANK, C_KV_RANK = 256, 128
ROPE_THETA = 10000.0
M_HEADS, M_HEAD_DIM = 4, 128
N_EXPERTS = 8
A_W = A_HEADS * A_HEAD_DIM
B_W = B_GROUPS * B_GROUP_DIM
C_W = C_HEADS * C_V_DIM
M_W = M_HEADS * M_HEAD_DIM

LANES = 128

SEG_Q, SEG_K, SEG_V, SEG_U, SEG_VB, SEG_CQ, SEG_CKV, SEG_MISC = 0, 256, 512, 768, 1024, 1280, 1536, 1664
IN_PAD_W = SEG_MISC + LANES
ROPE_LANE = C_NOPE_DIM
FORGET_LANE = 96
HEAD_SLAB = LANES
VT_ROWS = 80
LOG2E = float(np.log2(np.e))

ROW_TILE = 512
ATTN_TILE = 512
MOE_CHUNK = 1024
MOE_CAP = 256
MOE_TAIL = 128
MOE_BLOCK = 512
VMEM_LIMIT = 56 * 1024 * 1024


def _cparams(*sem):
    return pltpu.CompilerParams(dimension_semantics=sem, vmem_limit_bytes=VMEM_LIMIT)


def _full(shape):
    n = len(shape)
    return pl.BlockSpec(shape, lambda *_: (0,) * n)


def _rms(x):
    return x * lax.rsqrt(jnp.mean(x * x, axis=-1, keepdims=True) + EPS)


def _dot_split(v, exact, pieces, lhs_is_exact=False):
    total = None
    rem = v
    for n in range(pieces):
        part = rem.astype(BF16)
        if n + 1 < pieces:
            rem = rem - part.astype(F32)
        term = (jnp.dot(exact, part, preferred_element_type=F32) if lhs_is_exact
                else jnp.dot(part, exact, preferred_element_type=F32))
        total = term if total is None else total + term
    return total


def _lane_iota(n=LANES):
    return lax.broadcasted_iota(jnp.int32, (1, n), 1)


def _rope_table_kernel(pos_ref, inv_ref, sgn_ref, cos_ref, sin_ref):
    ang = pos_ref[...] * inv_ref[...]
    cos_ref[...] = jnp.cos(ang)
    sin_ref[...] = jnp.sin(ang) * sgn_ref[...]


def _rope_tables(pos_col):
    t = pos_col.shape[0]
    half = C_ROPE_DIM // 2
    inv = ROPE_THETA ** (-jnp.arange(half, dtype=F32) / half)
    inv_l = jnp.zeros((1, LANES), F32).at[0, ROPE_LANE:ROPE_LANE + C_ROPE_DIM].set(jnp.tile(inv, 2))
    sgn = np.zeros((1, LANES), np.float32)
    sgn[0, ROPE_LANE:ROPE_LANE + half] = -1.0
    sgn[0, ROPE_LANE + half:ROPE_LANE + C_ROPE_DIM] = 1.0
    tm = ROW_TILE
    return pl.pallas_call(
        _rope_table_kernel,
        grid=(t // tm,),
        in_specs=[pl.BlockSpec((tm, 1), lambda i: (i, 0)), _full((1, LANES)), _full((1, LANES))],
        out_specs=[pl.BlockSpec((tm, LANES), lambda i: (i, 0))] * 2,
        out_shape=[jax.ShapeDtypeStruct((t, LANES), F32)] * 2,
        compiler_params=_cparams("parallel"),
        name="rope_tables",
    )(pos_col, inv_l, jnp.asarray(sgn))


def _rotate(x, cos, sin_signed, lane):
    half = C_ROPE_DIM // 2
    partner = jnp.where(lane < ROPE_LANE + half,
                        pltpu.roll(x, LANES - half, 1), pltpu.roll(x, half, 1))
    return x * cos + partner * sin_signed


def _store_v_transposed(vt_ref, v, n_heads):
    tm = v.shape[0]
    v_t = v.T
    tail = jnp.where(lax.broadcasted_iota(jnp.int32, (VT_ROWS - C_V_DIM, tm), 0) == 0, 1.0, 0.0).astype(BF16)
    for hd in range(n_heads):
        vt_ref[0, hd, 0:C_V_DIM, :] = v_t[hd * C_V_DIM:(hd + 1) * C_V_DIM, :].astype(BF16)
        vt_ref[0, hd, C_V_DIM:VT_ROWS, :] = tail


def _gelu(x):
    return 0.5 * x * (1.0 + lax.erf(x * np.float32(1.0 / np.sqrt(2.0))))


def _inproj_kernel(x_ref, g_ref, w_ref, aq_ref, ak_ref, fb_ref, bvg_ref, ws_ref, bs_ref, onb_ref,
                   cqg_ref, ckvg_ref, gm_ref, tri_ref,
                   qa_ref, ka_ref, vt_ref, bn_ref, cq_ref, ckv_ref, misc_ref,
                   carry_ref, *, tiles_per_seq):
    i = pl.program_id(0)
    tm = x_ref.shape[0]
    h = _rms(x_ref[...]) * g_ref[...]
    proj = jnp.dot(h.astype(BF16), w_ref[...], preferred_element_type=F32)
    gm = gm_ref[...]

    def group_mean(v):
        return _dot_split(v, gm, 2)

    misc = proj[:, SEG_MISC:SEG_MISC + LANES]
    misc_ref[...] = misc
    z = misc + fb_ref[...]
    log_f = jnp.minimum(z, 0.0) - jnp.log1p(jnp.exp(-jnp.abs(z)))

    @pl.when(i % tiles_per_seq == 0)
    def _():
        carry_ref[...] = jnp.zeros_like(carry_ref)

    cum = _dot_split(log_f, tri_ref[...], 3, lhs_is_exact=True) + carry_ref[...]
    carry_ref[...] = cum[tm - 1:tm, :]
    f_hi = (cum * LOG2E).astype(BF16).astype(F32)
    f_rem = cum * LOG2E - f_hi
    f_mid = f_rem.astype(BF16).astype(F32)
    f_lo = f_rem - f_mid

    q = proj[:, SEG_Q:SEG_Q + A_W]
    qn = q * lax.rsqrt(group_mean(q * q) + EPS) * aq_ref[...]
    k = proj[:, SEG_K:SEG_K + A_W]
    kn = k * lax.rsqrt(group_mean(k * k) + EPS) * ak_ref[...]
    lane = _lane_iota()
    for hd in range(A_HEADS):
        pair = slice((hd // 2) * LANES, (hd // 2 + 1) * LANES)
        slab = slice(hd * HEAD_SLAB, (hd + 1) * HEAD_SLAB)
        data = (lane < A_HEAD_DIM) if hd % 2 == 0 else (lane >= A_HEAD_DIM)
        e0 = A_HEAD_DIM if hd % 2 == 0 else 0
        fl = FORGET_LANE + 8 * (hd // 2) + hd % 2
        ones = jnp.where((lane >= e0) & (lane < e0 + 3), 1.0, 0.0)
        qa_ref[:, slab] = jnp.where(data, qn[:, pair], ones).astype(BF16)
        bias = jnp.where(lane == e0, -f_hi[:, fl:fl + 1],
                         jnp.where(lane == e0 + 1, -f_mid[:, fl:fl + 1],
                                   jnp.where(lane == e0 + 2, -f_lo[:, fl:fl + 1], 0.0)))
        ka_ref[:, slab] = jnp.where(data, kn[:, pair], bias).astype(BF16)
    _store_v_transposed(vt_ref, proj[:, SEG_V:SEG_V + A_W], A_HEADS)

    u = _gelu(proj[:, SEG_U:SEG_U + B_W])
    v = _gelu(proj[:, SEG_VB:SEG_VB + B_W])
    dv = v - group_mean(v)
    vn = dv * lax.rsqrt(group_mean(dv * dv) + EPS) * bvg_ref[...]
    group = lax.broadcasted_iota(jnp.int32, (1, B_W), 1) // B_GROUP_DIM
    for w in range(tm // B_WINDOW):
        rows = slice(w * B_WINDOW, (w + 1) * B_WINDOW)
        y_all = jnp.dot(ws_ref[...], vn[rows].astype(BF16), preferred_element_type=F32)
        y = bs_ref[...]
        for g in range(B_GROUPS):
            y = y + jnp.where(group == g, y_all[g * B_WINDOW:(g + 1) * B_WINDOW], 0.0)
        b = u[rows] * y
        bn_ref[rows, :] = (_rms(b) * onb_ref[...]).astype(BF16)

    cq_ref[...] = (_rms(proj[:, SEG_CQ:SEG_CQ + C_Q_RANK]) * cqg_ref[...]).astype(BF16)
    ckv_ref[...] = (_rms(proj[:, SEG_CKV:SEG_CKV + C_KV_RANK]) * ckvg_ref[...]).astype(BF16)


def _inproj(x2d, seq, p):
    t = x2d.shape[0]
    tm = ROW_TILE
    tps = seq // tm
    nb = t // seq
    row = lambda w: pl.BlockSpec((tm, w), lambda i: (i, 0))
    qk_w = A_HEADS * HEAD_SLAB
    out_shape = [
        jax.ShapeDtypeStruct((t, qk_w), BF16), jax.ShapeDtypeStruct((t, qk_w), BF16),
        jax.ShapeDtypeStruct((nb, A_HEADS, VT_ROWS, seq), BF16),
        jax.ShapeDtypeStruct((t, B_W), BF16),
        jax.ShapeDtypeStruct((t, C_Q_RANK), BF16), jax.ShapeDtypeStruct((t, C_KV_RANK), BF16),
        jax.ShapeDtypeStruct((t, LANES), F32),
    ]
    out_specs = [row(qk_w), row(qk_w),
                 pl.BlockSpec((1, A_HEADS, VT_ROWS, tm), lambda i: (i // tps, 0, 0, i % tps)),
                 row(B_W), row(C_Q_RANK), row(C_KV_RANK), row(LANES)]
    consts = [p["mix_g"], p["w_in"], p["aq"], p["ak"], p["fb"], p["bvg"], p["ws"], p["bs"], p["onb"],
              p["cqg"], p["ckvg"], p["gm"], p["tri"]]
    return pl.pallas_call(
        functools.partial(_inproj_kernel, tiles_per_seq=tps),
        grid=(t // tm,),
        in_specs=[row(D_MODEL)] + [_full(c.shape) for c in consts],
        out_specs=out_specs,
        out_shape=out_shape,
        scratch_shapes=[pltpu.VMEM((1, LANES), F32)],
        compiler_params=_cparams("arbitrary"),
        name="in_proj",
    )(x2d, *consts)


def _mla_prep_kernel(cq_ref, ckv_ref, misc_ref, cos_ref, sin_ref, wuq_ref, wuk_ref, wuv_ref,
                     gq_ref, gkn_ref, gkr_ref, qc_ref, kc_ref, vt_ref):
    lane = _lane_iota()
    nope = lane < C_NOPE_DIM
    rope = (lane >= ROPE_LANE) & (lane < ROPE_LANE + C_ROPE_DIM)
    cos, sin = cos_ref[...], sin_ref[...]
    q = jnp.dot(cq_ref[...], wuq_ref[0], preferred_element_type=F32)
    q_partner = jnp.dot(cq_ref[...], wuq_ref[1], preferred_element_type=F32)
    q_cos = gq_ref[0:1, :] * cos
    q_sin = gq_ref[1:2, :] * sin
    kn = jnp.dot(ckv_ref[...], wuk_ref[...], preferred_element_type=F32)
    _store_v_transposed(vt_ref, jnp.dot(ckv_ref[...], wuv_ref[...], preferred_element_type=F32), C_HEADS)

    kr = jnp.where(rope, misc_ref[...], 0.0)
    kr = kr * lax.rsqrt(jnp.sum(kr * kr, axis=-1, keepdims=True) * (1.0 / C_ROPE_DIM) + EPS) * gkr_ref[...]
    kr = _rotate(kr, cos, sin, lane)

    for hd in range(C_HEADS):
        cols = slice(hd * HEAD_SLAB, (hd + 1) * HEAD_SLAB)
        qh = q[:, cols]
        sq = qh * qh
        r_n = lax.rsqrt(jnp.sum(jnp.where(nope, sq, 0.0), axis=-1, keepdims=True) * (1.0 / C_NOPE_DIM) + EPS)
        r_r = lax.rsqrt(jnp.sum(jnp.where(rope, sq, 0.0), axis=-1, keepdims=True) * (1.0 / C_ROPE_DIM) + EPS)
        qc_ref[:, cols] = (jnp.where(nope, r_n, r_r)
                           * (qh * q_cos + q_partner[:, cols] * q_sin)).astype(BF16)
        kh = kn[:, cols]
        r_k = lax.rsqrt(jnp.sum(kh * kh, axis=-1, keepdims=True) * (1.0 / C_NOPE_DIM) + EPS)
        kc_ref[:, cols] = (kh * r_k * gkn_ref[...] + kr).astype(BF16)


def _mla_prep(cq, ckv, misc, cos, sin, seq, p):
    t = cq.shape[0]
    tm = ROW_TILE
    tps = seq // tm
    row = lambda w: pl.BlockSpec((tm, w), lambda i: (i, 0))
    consts = [p["wuq"], p["wuk"], p["wuv"], p["gq"], p["gkn"], p["gkr"]]
    qk_w = C_HEADS * HEAD_SLAB
    return pl.pallas_call(
        _mla_prep_kernel,
        grid=(t // tm,),
        in_specs=[row(C_Q_RANK), row(C_KV_RANK), row(LANES), row(LANES), row(LANES)]
                 + [_full(c.shape) for c in consts],
        out_specs=[row(qk_w), row(qk_w),
                   pl.BlockSpec((1, C_HEADS, VT_ROWS, tm), lambda i: (i // tps, 0, 0, i % tps))],
        out_shape=[jax.ShapeDtypeStruct((t, qk_w), BF16), jax.ShapeDtypeStruct((t, qk_w), BF16),
                   jax.ShapeDtypeStruct((t // seq, C_HEADS, VT_ROWS, seq), BF16)],
        compiler_params=_cparams("parallel"),
        name="mla_prep",
    )(cq, ckv, misc, cos, sin, *consts)


def _attn_items(nq):
    items = [(i, j) for i in range(nq) for j in range(i)] + [(i, i) for i in range(nq)]
    return np.array(items, np.int32).T


def _attn_kernel(items_ref, q_ref, k_ref, vt_ref, mask_ref, o_ref, s0_ref, s1_ref, p0_ref, p1_ref,
                 mp0_ref, mp1_ref, mrun_ref, macc_ref, acc_ref, *, n_items):
    tk, tq = mask_ref.shape
    nq = q_ref.shape[0] // tq
    s_bufs, p_bufs, mp_bufs = (s0_ref, s1_ref), (p0_ref, p1_ref), (mp0_ref, mp1_ref)
    mrun_ref[...] = jnp.full(mrun_ref.shape, NEG_INF, F32)
    macc_ref[...] = jnp.full(macc_ref.shape, NEG_INF, F32)
    acc_ref[...] = jnp.zeros(acc_ref.shape, F32)

    def scores(it, buf, masked):
        qi = items_ref[0, it]
        q_start = pl.multiple_of(qi * tq, tq)
        k_start = pl.multiple_of(items_ref[1, it] * tk, tk)
        for hh in range(2):
            cols = slice(hh * HEAD_SLAB, (hh + 1) * HEAD_SLAB)
            s_t = lax.dot_general(k_ref[pl.ds(k_start, tk), cols], q_ref[pl.ds(q_start, tq), cols],
                                  (((1,), (1,)), ((), ())), preferred_element_type=F32)
            if masked:
                s_t = s_t + mask_ref[...]
            s_bufs[buf][hh] = s_t
            mrun_ref[qi, hh] = jnp.maximum(mrun_ref[qi, hh], jnp.max(s_t, axis=0, keepdims=True))

    def exponentiate(it, buf):
        qi = items_ref[0, it]
        for hh in range(2):
            m = mrun_ref[qi, hh]
            p_bufs[buf][hh] = jnp.exp2(s_bufs[buf][hh] - m).astype(BF16)
            mp_bufs[buf][hh] = m

    def accumulate(it, buf):
        qi = items_ref[0, it]
        k_start = pl.multiple_of(items_ref[1, it] * tk, tk)
        for hh in range(2):
            m = mp_bufs[buf][hh]
            pv = jnp.dot(vt_ref[0, hh, :, pl.ds(k_start, tk)], p_bufs[buf][hh],
                         preferred_element_type=F32)
            acc_ref[qi, hh] = jnp.exp2(macc_ref[qi, hh] - m) * acc_ref[qi, hh] + pv
            macc_ref[qi, hh] = m

    def beat(it, par, masked):
        if not isinstance(it, int) or 2 <= it <= n_items + 1:
            accumulate(it - 2, par)
        if not isinstance(it, int) or 1 <= it <= n_items:
            exponentiate(it - 1, 1 - par)
        if not isinstance(it, int) or it < n_items:
            scores(it, par, masked)

    def run_beats(lo, hi, masked):
        lo = max(lo, 0)
        if lo < hi and (lo % 2 == 1 or lo < 2):
            for it in range(lo, min(hi, lo + 2 - lo % 2)):
                beat(it, it % 2, masked)
            lo = min(hi, lo + 2 - lo % 2)
        n_pairs = (hi - lo) // 2
        if n_pairs > 0:
            def pair(t, carry, lo=lo):
                beat(lo + 2 * t, 0, masked)
                beat(lo + 2 * t + 1, 1, masked)
                return carry
            lax.fori_loop(0, n_pairs, pair, 0)
        for it in range(lo + 2 * n_pairs, hi):
            beat(it, it % 2, masked)

    n_full = n_items - nq
    run_beats(0, n_full, False)
    run_beats(n_full, n_items, True)
    for it in (n_items, n_items + 1):
        beat(it, it % 2, True)

    for qi in range(nq):
        halves = [acc_ref[qi, hh, 0:C_V_DIM, :] / acc_ref[qi, hh, C_V_DIM:C_V_DIM + 1, :] for hh in range(2)]
        o_ref[qi * tq:(qi + 1) * tq, :] = jnp.concatenate(halves, axis=0).T.astype(o_ref.dtype)


def _attention(q, k, vt, *, unit, name):
    t = q.shape[0]
    nb, n_heads, _, seq = vt.shape
    tq = ATTN_TILE
    nq = seq // tq
    items = _attn_items(nq)
    pos = np.arange(tq)
    diag_mask = np.where((pos[:, None] // unit) <= (pos[None, :] // unit), 0.0, NEG_INF)
    mask = jnp.asarray(diag_mask.astype(np.float32))
    seq_blk = lambda w: pl.BlockSpec((seq, w), lambda b, p, items_ref: (b, p))
    grid_spec = pltpu.PrefetchScalarGridSpec(
        num_scalar_prefetch=1,
        grid=(nb, n_heads // 2),
        in_specs=[seq_blk(2 * HEAD_SLAB), seq_blk(2 * HEAD_SLAB),
                  pl.BlockSpec((1, 2, VT_ROWS, seq), lambda b, p, items_ref: (b, p, 0, 0)),
                  pl.BlockSpec((tq, tq), lambda b, p, items_ref: (0, 0), pipeline_mode=pl.Buffered(1))],
        out_specs=seq_blk(2 * C_V_DIM),
        scratch_shapes=[pltpu.VMEM((2, tq, tq), F32), pltpu.VMEM((2, tq, tq), F32),
                        pltpu.VMEM((2, tq, tq), BF16), pltpu.VMEM((2, tq, tq), BF16),
                        pltpu.VMEM((2, 1, tq), F32), pltpu.VMEM((2, 1, tq), F32),
                        pltpu.VMEM((nq, 2, 1, tq), F32), pltpu.VMEM((nq, 2, 1, tq), F32),
                        pltpu.VMEM((nq, 2, VT_ROWS, tq), F32)])
    return pl.pallas_call(
        functools.partial(_attn_kernel, n_items=items.shape[1]),
        grid_spec=grid_spec,
        out_shape=jax.ShapeDtypeStruct((t, n_heads * C_V_DIM), BF16),
        compiler_params=_cparams("parallel", "parallel"),
        name=name,
    )(jnp.asarray(items), q, k, vt, mask)


def _mem_kv_kernel(mem_ref, g_ref, w_ref, kg_ref, k_ref, v_ref):
    mn = (_rms(mem_ref[0]) * g_ref[...]).astype(BF16)
    kv = jnp.dot(mn, w_ref[...], preferred_element_type=F32)
    for hd in range(M_HEADS):
        cols = slice(hd * M_HEAD_DIM, (hd + 1) * M_HEAD_DIM)
        k_ref[0, :, cols] = (_rms(kv[:, cols]) * kg_ref[...]).astype(BF16)
    v_ref[0] = kv[:, M_W:].astype(BF16)


def _mem_kv(mem, p):
    nb, ml, _ = mem.shape
    consts = [p["mem_g"], p["w_mem_kv"], p["mkg"]]
    blk = pl.BlockSpec((1, ml, M_W), lambda b: (b, 0, 0))
    return pl.pallas_call(
        _mem_kv_kernel,
        grid=(nb,),
        in_specs=[pl.BlockSpec((1, ml, D_MODEL), lambda b: (b, 0, 0))] + [_full(c.shape) for c in consts],
        out_specs=[blk, blk],
        out_shape=[jax.ShapeDtypeStruct((nb, ml, M_W), BF16)] * 2,
        compiler_params=_cparams("parallel"),
        name="mem_kv",
    )(mem, *consts)


def _outproj_kernel(x_ref, a_ref, bn_ref, c_ref, ona_ref, onc_ref, wo_ref, xg_ref, wq_ref, mqg_ref,
                    km_ref, vm_ref, wmo_ref, o_ref):
    a_n = (_rms(a_ref[...].astype(F32)) * ona_ref[...]).astype(BF16)
    c_n = (_rms(c_ref[...].astype(F32)) * onc_ref[...]).astype(BF16)
    mix = jnp.concatenate([a_n, bn_ref[...], c_n], axis=-1)
    x1 = x_ref[...] + jnp.dot(mix, wo_ref[...], preferred_element_type=F32)

    h = (_rms(x1) * xg_ref[...]).astype(BF16)
    q = jnp.dot(h, wq_ref[...], preferred_element_type=F32)
    outs = []
    for hd in range(M_HEADS):
        cols = slice(hd * M_HEAD_DIM, (hd + 1) * M_HEAD_DIM)
        qh = (_rms(q[:, cols]) * mqg_ref[...]).astype(BF16)
        s = lax.dot_general(qh, km_ref[0, :, cols], (((1,), (1,)), ((), ())), preferred_element_type=F32)
        e = jnp.exp(s - jnp.max(s, axis=-1, keepdims=True))
        pr = e / jnp.sum(e, axis=-1, keepdims=True)
        outs.append(jnp.dot(pr.astype(BF16), vm_ref[0, :, cols], preferred_element_type=F32).astype(BF16))
    o_ref[...] = x1 + jnp.dot(jnp.concatenate(outs, axis=-1), wmo_ref[...], preferred_element_type=F32)


def _outproj(x2d, a, bn, c, km, vm, seq, p):
    t = x2d.shape[0]
    tm = ROW_TILE
    tps = seq // tm
    ml = km.shape[1]
    row = lambda w: pl.BlockSpec((tm, w), lambda i: (i, 0))
    memblk = pl.BlockSpec((1, ml, M_W), lambda i: (i // tps, 0, 0))
    c1 = [p["ona"], p["onc"], p["w_out"], p["xg"], p["w_mem_q"], p["mqg"]]
    return pl.pallas_call(
        _outproj_kernel,
        grid=(t // tm,),
        in_specs=[row(D_MODEL), row(A_W), row(B_W), row(C_W)] + [_full(c_.shape) for c_ in c1]
                 + [memblk, memblk, _full(p["w_mem_out"].shape)],
        out_specs=row(D_MODEL),
        out_shape=jax.ShapeDtypeStruct((t, D_MODEL), F32),
        compiler_params=_cparams("parallel"),
        name="out_proj_mem_attn",
    )(x2d, a, bn, c, *c1, km, vm, p["w_mem_out"])


def _silu(x):
    return x * jax.nn.sigmoid(x)


def _ffn_kernel(x_ref, g_ref, wg_ref, wu_ref, wd_ref, o_ref, *, n_chunks):
    x = x_ref[...]
    h = (_rms(x) * g_ref[...]).astype(BF16)
    fc = wg_ref.shape[1] // n_chunks
    acc = x
    for c in range(n_chunks):
        cols = slice(c * fc, (c + 1) * fc)
        act = _silu(jnp.dot(h, wg_ref[:, cols], preferred_element_type=F32)) * \
            jnp.dot(h, wu_ref[:, cols], preferred_element_type=F32)
        acc = acc + jnp.dot(act.astype(BF16), wd_ref[cols, :], preferred_element_type=F32)
    o_ref[...] = acc


def _ffn(x2d, g, wg, wu, wd):
    t = x2d.shape[0]
    tm = ROW_TILE
    row = pl.BlockSpec((tm, D_MODEL), lambda i: (i, 0))
    resident = lambda a: pl.BlockSpec(a.shape, lambda i: (0, 0), pipeline_mode=pl.Buffered(1))
    return pl.pallas_call(
        functools.partial(_ffn_kernel, n_chunks=2),
        grid=(t // tm,),
        in_specs=[row, _full(g.shape), resident(wg), resident(wu), resident(wd)],
        out_specs=row,
        out_shape=jax.ShapeDtypeStruct((t, D_MODEL), F32),
        compiler_params=_cparams("parallel"),
        name="ffn_dense",
    )(x2d, g, wg, wu, wd)


def _moe_kernel(x_hbm_ref, g_ref, wr_ref, br_ref, tri_ref, wg_ref, wu_ref, wd_ref, o_ref,
                h_ref, comb_ref, rank_ref, rank_t_ref, sem_ref):
    i = pl.program_id(0)
    e = pl.program_id(1)
    lane = _lane_iota()
    cap = MOE_CAP

    @pl.when(e == 0)
    def _():
        rows = o_ref.shape[0]
        load = pltpu.make_async_copy(x_hbm_ref.at[pl.ds(i * rows, rows)], o_ref, sem_ref)
        load.start()
        load.wait()
        h = _rms(o_ref[...]) * g_ref[...]
        h_hi = h.astype(BF16)
        h_ref[...] = h_hi
        h_lo = (h - h_hi.astype(F32)).astype(BF16)
        logits = (jnp.dot(h_hi, wr_ref[0], preferred_element_type=F32)
                  + jnp.dot(h_lo, wr_ref[0], preferred_element_type=F32)
                  + jnp.dot(h_hi, wr_ref[1], preferred_element_type=F32)) + br_ref[...]
        logits = jnp.where(lane < N_EXPERTS, logits, -jnp.inf)
        v1 = jnp.max(logits, axis=-1, keepdims=True)
        i1 = jnp.min(jnp.where(logits == v1, lane, LANES), axis=-1, keepdims=True)
        rest = jnp.where(lane == i1, -jnp.inf, logits)
        v2 = jnp.max(rest, axis=-1, keepdims=True)
        i2 = jnp.min(jnp.where(rest == v2, lane, LANES), axis=-1, keepdims=True)
        e2 = jnp.exp(v2 - v1)
        g1 = 1.0 / (1.0 + e2)
        comb_ref[...] = jnp.where(lane == i1, g1, 0.0) + jnp.where(lane == i2, e2 * g1, 0.0)
        hit = (lane == i1) | (lane == i2)
        before = jnp.dot(tri_ref[...], jnp.where(hit, 1.0, 0.0).astype(BF16), preferred_element_type=F32)
        rank = jnp.where(hit, before, -1.0)
        rank_ref[...] = rank
        rank_t_ref[...] = rank.T[0:N_EXPERTS, :]

    pick = lane == e
    rank_col = jnp.sum(jnp.where(pick, rank_ref[...], 0.0), axis=-1, keepdims=True)
    gate_col = jnp.sum(jnp.where(pick, comb_ref[...], 0.0), axis=-1, keepdims=True)
    rank_row = rank_t_ref[pl.ds(e, 1), :]
    n_tokens = jnp.max(rank_col).astype(jnp.int32) + 1

    def block(b, rows):
        r0 = (b * cap).astype(F32)
        slot_col = lax.broadcasted_iota(jnp.int32, (rows, 1), 0).astype(F32)
        slot_row = lax.broadcasted_iota(jnp.int32, (1, rows), 1).astype(F32)
        sel = jnp.where(rank_row - r0 == slot_col, 1.0, 0.0).astype(BF16)
        sel_t = jnp.where(rank_col - r0 == slot_row, 1.0, 0.0).astype(BF16)
        xe = jnp.dot(sel, h_ref[...], preferred_element_type=F32).astype(BF16)
        act = _silu(jnp.dot(xe, wg_ref[0], preferred_element_type=F32)) * \
            jnp.dot(xe, wu_ref[0], preferred_element_type=F32)
        y = jnp.dot(act.astype(BF16), wd_ref[0], preferred_element_type=F32).astype(BF16)
        o_ref[...] += gate_col * jnp.dot(sel_t, y, preferred_element_type=F32)

    n_full = n_tokens // cap
    left = n_tokens - n_full * cap

    def full_block(b, carry):
        block(b, cap)
        return carry

    lax.fori_loop(0, n_full, full_block, 0)

    @pl.when(left > MOE_TAIL)
    def _():
        block(n_full, cap)

    @pl.when((left > 0) & (left <= MOE_TAIL))
    def _():
        block(n_full, MOE_TAIL)


def _moe(x2d, g, wr, br, wg, wu, wd):
    t = x2d.shape[0]
    chunk = tm = MOE_CHUNK
    ff = wg.shape[2]
    tri = jnp.asarray(np.tril(np.ones((chunk, chunk), np.float32), -1)).astype(BF16)
    return pl.pallas_call(
        _moe_kernel,
        grid=(t // tm, N_EXPERTS),
        in_specs=[pl.BlockSpec(memory_space=pl.ANY),
                  _full(g.shape), _full(wr.shape), _full(br.shape),
                  pl.BlockSpec((chunk, chunk), lambda i, e: (0, 0), pipeline_mode=pl.Buffered(1)),
                  pl.BlockSpec((1, D_MODEL, ff), lambda i, e: (e, 0, 0)),
                  pl.BlockSpec((1, D_MODEL, ff), lambda i, e: (e, 0, 0)),
                  pl.BlockSpec((1, ff, D_MODEL), lambda i, e: (e, 0, 0))],
        out_specs=pl.BlockSpec((tm, D_MODEL), lambda i, e: (i, 0)),
        out_shape=jax.ShapeDtypeStruct((t, D_MODEL), F32),
        scratch_shapes=[pltpu.VMEM((tm, D_MODEL), BF16), pltpu.VMEM((tm, LANES), F32),
                        pltpu.VMEM((tm, LANES), F32), pltpu.VMEM((N_EXPERTS, chunk), F32),
                        pltpu.SemaphoreType.DMA(())],
        compiler_params=_cparams("parallel", "arbitrary"),
        name="moe_routed",
    )(x2d, g, wr, br, tri, wg, wu, wd)


def _pack_bf16_pairs(lo, hi):
    lo_bits = pltpu.bitcast(lo.astype(BF16).astype(F32), jnp.uint32)
    hi_bits = pltpu.bitcast(hi.astype(BF16).astype(F32), jnp.uint32)
    return lax.shift_right_logical(lo_bits, jnp.uint32(16)) | (hi_bits & jnp.uint32(0xFFFF0000))


def _unpack_bf16_pairs(words):
    lo = pltpu.bitcast(lax.shift_left(words, jnp.uint32(16)), F32)
    hi = pltpu.bitcast(words & jnp.uint32(0xFFFF0000), F32)
    return lo, hi


def _pack_rows(v):
    q = D_MODEL // 4
    return _pack_bf16_pairs(v[:, 0:q], v[:, q:2 * q]), _pack_bf16_pairs(v[:, 2 * q:3 * q], v[:, 3 * q:])


def _unpack_rows(a, b):
    return jnp.concatenate([*_unpack_bf16_pairs(a), *_unpack_bf16_pairs(b)], axis=-1)


def _route_kernel(x_ref, g_ref, wr_ref, br_ref, tri_ref, ha_ref, hb_ref, route_ref, cnt_ref):
    lane = _lane_iota()
    h = _rms(x_ref[...]) * g_ref[...]
    ha_ref[...], hb_ref[...] = _pack_rows(h)
    h_hi = h.astype(BF16)
    h_lo = (h - h_hi.astype(F32)).astype(BF16)
    logits = (jnp.dot(h_hi, wr_ref[0], preferred_element_type=F32)
              + jnp.dot(h_lo, wr_ref[0], preferred_element_type=F32)
              + jnp.dot(h_hi, wr_ref[1], preferred_element_type=F32)) + br_ref[...]
    logits = jnp.where(lane < N_EXPERTS, logits, -jnp.inf)
    v1 = jnp.max(logits, axis=-1, keepdims=True)
    i1 = jnp.min(jnp.where(logits == v1, lane, LANES), axis=-1, keepdims=True)
    rest = jnp.where(lane == i1, -jnp.inf, logits)
    v2 = jnp.max(rest, axis=-1, keepdims=True)
    i2 = jnp.min(jnp.where(rest == v2, lane, LANES), axis=-1, keepdims=True)
    e2 = jnp.exp(v2 - v1)
    g1 = 1.0 / (1.0 + e2)
    hit1, hit2 = lane == i1, lane == i2
    ones = jnp.where(hit1 | hit2, 1.0, 0.0)
    before = jnp.dot(tri_ref[...], ones.astype(BF16), preferred_element_type=F32)
    r1 = jnp.sum(jnp.where(hit1, before, 0.0), axis=-1, keepdims=True)
    r2 = jnp.sum(jnp.where(hit2, before, 0.0), axis=-1, keepdims=True)
    cols = [i1.astype(F32), i2.astype(F32), g1, e2 * g1, r1, r2]
    route = jnp.zeros(route_ref.shape, F32)
    for n, c in enumerate(cols):
        route = jnp.where(lane == n, c, route)
    route_ref[...] = route
    cnt_ref[0] = jnp.broadcast_to(jnp.sum(ones, axis=0, keepdims=True), cnt_ref.shape[1:])


def _route(x2d, g, wr, br):
    t = x2d.shape[0]
    tm = MOE_CHUNK
    tri = jnp.asarray(np.tril(np.ones((tm, tm), np.float32), -1)).astype(BF16)
    row = lambda w: pl.BlockSpec((tm, w), lambda i: (i, 0))
    q = D_MODEL // 4
    return pl.pallas_call(
        _route_kernel,
        grid=(t // tm,),
        in_specs=[row(D_MODEL), _full(g.shape), _full(wr.shape), _full(br.shape), _full(tri.shape)],
        out_specs=[row(q), row(q), row(LANES), pl.BlockSpec((1, 8, LANES), lambda i: (i, 0, 0))],
        out_shape=[jax.ShapeDtypeStruct((t, q), jnp.uint32), jax.ShapeDtypeStruct((t, q), jnp.uint32),
                   jax.ShapeDtypeStruct((t, LANES), F32), jax.ShapeDtypeStruct((t // tm, 8, LANES), F32)],
        compiler_params=_cparams("parallel"),
        name="moe_route",
    )(x2d, g, wr, br, tri)


def _dest_kernel(route_ref, base_ref, o_ref):
    lane = _lane_iota()
    r = route_ref[...]
    base = base_ref[0, 0:1, :]
    lane_f = lane.astype(F32)
    d1 = jnp.sum(jnp.where(lane_f == r[:, 0:1], base, 0.0), axis=-1, keepdims=True) + r[:, 4:5]
    d2 = jnp.sum(jnp.where(lane_f == r[:, 1:2], base, 0.0), axis=-1, keepdims=True) + r[:, 5:6]
    both = jnp.where(lane == 0, d1, jnp.where(lane == 1, d2, 0.0))
    o_ref[...] = both.T[0:8, :].astype(jnp.int32)


def _destinations(route, base):
    t = route.shape[0]
    tm = MOE_CHUNK
    out = pl.pallas_call(
        _dest_kernel,
        grid=(t // tm,),
        in_specs=[pl.BlockSpec((tm, LANES), lambda i: (i, 0)), pl.BlockSpec((1, 8, LANES), lambda i: (i, 0, 0))],
        out_specs=pl.BlockSpec((8, tm), lambda i: (0, i)),
        out_shape=jax.ShapeDtypeStruct((8, t), jnp.int32),
        compiler_params=_cparams("parallel"),
        name="moe_dest",
    )(route, base)
    return out[0:2].reshape(1, 2 * t)


SC_WINDOW = 128


def _sc_mesh():
    return plsc.VectorSubcoreMesh(core_axis_name="core", subcore_axis_name="subcore")


def _sc_scatter_rows(x, idx, n_out):
    n, d = x.shape
    m = idx.shape[1]
    nblk = n // SC_WINDOW

    @pl.kernel(out_type=jax.ShapeDtypeStruct((n_out, d), x.dtype), mesh=_sc_mesh(), name="moe_sc_scatter")
    def scatter(x_hbm, i_hbm, o_hbm):
        def body(x_vmem, i_vmem):
            pltpu.sync_copy(x_vmem, o_hbm.at[i_vmem.at[0]])

        pltpu.emit_pipeline(
            body,
            grid=(m // SC_WINDOW,),
            in_specs=[pl.BlockSpec((SC_WINDOW, d), lambda j: (j % nblk, 0)),
                      pl.BlockSpec((1, SC_WINDOW), lambda j: (0, j))],
            out_specs=[],
            core_axis_name="subcore",
            dimension_semantics=(pltpu.PARALLEL,),
        )(x_hbm, i_hbm)

    return scatter(x, idx)


def _sc_gather_rows(table, idx):
    d = table.shape[1]
    m = idx.shape[1]

    @pl.kernel(out_type=jax.ShapeDtypeStruct((m, d), table.dtype), mesh=_sc_mesh(), name="moe_sc_gather")
    def gather(t_hbm, i_hbm, o_hbm):
        def body(i_vmem, o_vmem):
            pltpu.sync_copy(t_hbm.at[i_vmem.at[0]], o_vmem)

        pltpu.emit_pipeline(
            body,
            grid=(m // SC_WINDOW,),
            in_specs=[pl.BlockSpec((1, SC_WINDOW), lambda j: (0, j))],
            out_specs=[pl.BlockSpec((SC_WINDOW, d), lambda j: (j, 0))],
            core_axis_name="subcore",
            dimension_semantics=(pltpu.PARALLEL,),
        )(i_hbm, o_hbm)

    return gather(table, idx)


def _expert_kernel(blk_expert_ref, n_used_ref, xa_ref, xb_ref, wg_ref, wu_ref, wd_ref, ya_ref, yb_ref):
    del blk_expert_ref

    @pl.when(pl.program_id(0) < n_used_ref[0])
    def _():
        xe = _unpack_rows(xa_ref[...], xb_ref[...]).astype(BF16)
        act = _silu(jnp.dot(xe, wg_ref[0], preferred_element_type=F32)) * \
            jnp.dot(xe, wu_ref[0], preferred_element_type=F32)
        y = jnp.dot(act.astype(BF16), wd_ref[0], preferred_element_type=F32)
        ya_ref[...], yb_ref[...] = _pack_rows(y)

    @pl.when(pl.program_id(0) >= n_used_ref[0])
    def _():
        ya_ref[...] = jnp.zeros(ya_ref.shape, ya_ref.dtype)
        yb_ref[...] = jnp.zeros(yb_ref.shape, yb_ref.dtype)


def _experts(blk_expert, n_used, xa, xb, wg, wu, wd):
    n_rows, q = xa.shape
    ff = wg.shape[2]
    blk = MOE_BLOCK
    row = pl.BlockSpec((blk, q), lambda b, be, nu: (b, 0))
    grid_spec = pltpu.PrefetchScalarGridSpec(
        num_scalar_prefetch=2,
        grid=(n_rows // blk,),
        in_specs=[row, row,
                  pl.BlockSpec((1, D_MODEL, ff), lambda b, be, nu: (be[b], 0, 0)),
                  pl.BlockSpec((1, D_MODEL, ff), lambda b, be, nu: (be[b], 0, 0)),
                  pl.BlockSpec((1, ff, D_MODEL), lambda b, be, nu: (be[b], 0, 0))],
        out_specs=[row, row])
    return pl.pallas_call(
        _expert_kernel,
        grid_spec=grid_spec,
        out_shape=[jax.ShapeDtypeStruct((n_rows, q), jnp.uint32)] * 2,
        compiler_params=_cparams("arbitrary"),
        name="moe_experts",
    )(blk_expert, n_used, xa, xb, wg, wu, wd)


def _combine_kernel(x_ref, route_ref, a1_ref, b1_ref, a2_ref, b2_ref, o_ref):
    g1 = route_ref[:, 2:3]
    g2 = route_ref[:, 3:4]
    o_ref[...] = x_ref[...] + g1 * _unpack_rows(a1_ref[...], b1_ref[...]) \
        + g2 * _unpack_rows(a2_ref[...], b2_ref[...])


def _combine(x2d, route, ya, yb):
    t = x2d.shape[0]
    tm = ROW_TILE
    nt = t // tm
    q = ya.shape[1]
    row = lambda w: pl.BlockSpec((tm, w), lambda i: (i, 0))
    first = pl.BlockSpec((tm, q), lambda i: (i, 0))
    second = pl.BlockSpec((tm, q), lambda i: (nt + i, 0))
    return pl.pallas_call(
        _combine_kernel,
        grid=(nt,),
        in_specs=[row(D_MODEL), row(LANES), first, first, second, second],
        out_specs=row(D_MODEL),
        out_shape=jax.ShapeDtypeStruct((t, D_MODEL), F32),
        compiler_params=_cparams("parallel"),
        name="moe_combine",
    )(x2d, route, ya, yb, ya, yb)


def _moe_sorted(x2d, g, wr, br, wg, wu, wd):
    t = x2d.shape[0]
    blk = MOE_BLOCK
    n_blocks = 2 * t // blk + N_EXPERTS
    ha, hb, route, cnt = _route(x2d, g, wr, br)

    cnt = cnt[:, 0, :N_EXPERTS].astype(jnp.int32)
    before_chunk = jnp.cumsum(cnt, axis=0) - cnt
    seg_blocks = (jnp.sum(cnt, axis=0) + blk - 1) // blk
    seg_end_blk = jnp.cumsum(seg_blocks)
    seg_start = (seg_end_blk - seg_blocks) * blk
    base = jnp.pad((seg_start[None, :] + before_chunk).astype(F32), ((0, 0), (0, LANES - N_EXPERTS)))
    idx = _destinations(route, jnp.broadcast_to(base[:, None, :], (base.shape[0], 8, LANES)))
    blk_expert = jnp.minimum(jnp.searchsorted(seg_end_blk, jnp.arange(n_blocks, dtype=jnp.int32), side="right"),
                             N_EXPERTS - 1).astype(jnp.int32)
    n_used = seg_end_blk[-1:].astype(jnp.int32)

    xa = _sc_scatter_rows(ha, idx, n_blocks * blk)
    xb = _sc_scatter_rows(hb, idx, n_blocks * blk)
    ya, yb = _experts(blk_expert, n_used, xa, xb, wg, wu, wd)
    return _combine(x2d, route, _sc_gather_rows(ya, idx), _sc_gather_rows(yb, idx))


def _pad_cols(w, width):
    return jnp.pad(w, ((0, 0), (0, width - w.shape[1])))


def _layer_params(l, mix_norm, w_in, b_forget, a_q_norm, a_k_norm, b_v_norm, b_spatial_w, b_spatial_b,
                  c_q_lat_norm, c_w_uq, c_kv_lat_norm, c_w_ukv, c_q_nope_norm, c_q_rope_norm,
                  c_k_nope_norm, c_k_rope_norm, out_norm_a, out_norm_b, out_norm_c, w_out,
                  xattn_norm, mem_norm, w_mem_q, w_mem_kv, m_q_norm, m_k_norm, w_mem_out):
    p = {}
    o = np.cumsum((0, A_W, A_W, A_W, A_HEADS, B_W, B_W, C_Q_RANK, C_KV_RANK, C_ROPE_DIM))
    w = w_in[l]
    seg = lambda n: w[:, o[n]:o[n + 1]]
    fa = seg(3)
    misc = jnp.zeros((D_MODEL, LANES), F32)
    misc = misc.at[:, ROPE_LANE:ROPE_LANE + C_ROPE_DIM].set(seg(8))
    fb = jnp.zeros((1, LANES), F32)
    for hd in range(A_HEADS):
        ln = FORGET_LANE + 8 * (hd // 2) + hd % 2
        misc = misc.at[:, ln].set(fa[:, hd])
        fb = fb.at[0, ln].set(b_forget[l, hd])
    p["w_in"] = jnp.concatenate([seg(0), seg(1), seg(2), seg(4), seg(5), seg(6), seg(7), misc], axis=1).astype(BF16)
    p["fb"] = fb
    p["mix_g"] = mix_norm[l][None]
    p["aq"] = jnp.tile(a_q_norm[l], A_HEADS)[None] * (A_HEAD_DIM ** -0.5 * LOG2E)
    p["ak"] = jnp.tile(a_k_norm[l], A_HEADS)[None]
    p["bvg"] = b_v_norm[l][None]
    pos = np.arange(B_WINDOW)
    mask = (pos[None, :] // CHUNK) <= (pos[:, None] // CHUNK)
    p["ws"] = jnp.where(mask[None], b_spatial_w[l], 0.0).reshape(B_GROUPS * B_WINDOW, B_WINDOW).astype(BF16)
    p["bs"] = jnp.repeat(b_spatial_b[l].T, B_GROUP_DIM, axis=1)
    p["onb"] = out_norm_b[l][None]
    p["cqg"] = c_q_lat_norm[l][None]
    p["ckvg"] = c_kv_lat_norm[l][None]
    gidx = np.arange(A_W) // A_HEAD_DIM
    p["gm"] = jnp.asarray((gidx[:, None] == gidx[None, :]).astype(np.float32) / A_HEAD_DIM).astype(BF16)
    p["tri"] = jnp.asarray(np.tril(np.ones((ROW_TILE, ROW_TILE), np.float32))).astype(BF16)

    qd = C_NOPE_DIM + C_ROPE_DIM
    half = C_ROPE_DIM // 2
    wq = c_w_uq[l]
    wq_partner = jnp.concatenate([jnp.zeros_like(wq[:, :, :C_NOPE_DIM]), wq[:, :, C_NOPE_DIM + half:],
                                  wq[:, :, C_NOPE_DIM:C_NOPE_DIM + half]], axis=-1)
    p["wuq"] = jnp.pad(jnp.stack([wq, wq_partner]), ((0, 0), (0, 0), (0, 0), (0, HEAD_SLAB - qd))
                       ).reshape(2, C_Q_RANK, -1).astype(BF16)
    wukv = c_w_ukv[l]
    p["wuk"] = jnp.pad(wukv[:, :, :C_NOPE_DIM], ((0, 0), (0, 0), (0, HEAD_SLAB - C_NOPE_DIM))
                       ).reshape(C_KV_RANK, -1).astype(BF16)
    p["wuv"] = wukv[:, :, C_NOPE_DIM:].reshape(C_KV_RANK, C_W).astype(BF16)
    gq = jnp.concatenate([c_q_nope_norm[l], c_q_rope_norm[l]])
    gq_partner = jnp.concatenate([jnp.zeros_like(c_q_nope_norm[l]), c_q_rope_norm[l][half:],
                                  c_q_rope_norm[l][:half]])
    p["gq"] = _pad_cols(jnp.stack([gq, gq_partner]) * (qd ** -0.5 * LOG2E), LANES)
    p["gkn"] = _pad_cols(c_k_nope_norm[l][None], LANES)
    p["gkr"] = jnp.zeros((1, LANES), F32).at[0, ROPE_LANE:ROPE_LANE + C_ROPE_DIM].set(c_k_rope_norm[l])

    p["ona"] = out_norm_a[l][None]
    p["onc"] = out_norm_c[l][None]
    p["w_out"] = w_out[l].astype(BF16)
    p["xg"] = xattn_norm[l][None]
    p["w_mem_q"] = w_mem_q[l].astype(BF16)
    p["mqg"] = m_q_norm[l][None] * (M_HEAD_DIM ** -0.5)
    p["mem_g"] = mem_norm[l][None]
    p["w_mem_kv"] = w_mem_kv[l].astype(BF16)
    p["mkg"] = m_k_norm[l][None]
    p["w_mem_out"] = w_mem_out[l].astype(BF16)
    return p


def kernel(x, mem, positions, mix_norm, w_in, b_forget, a_q_norm, a_k_norm, b_v_norm, b_spatial_w, b_spatial_b, c_q_lat_norm, c_w_uq, c_kv_lat_norm, c_w_ukv, c_q_nope_norm, c_q_rope_norm, c_k_nope_norm, c_k_rope_norm, out_norm_a, out_norm_b, out_norm_c, w_out, xattn_norm, mem_norm, w_mem_q, w_mem_kv, m_q_norm, m_k_norm, w_mem_out, ffn_norm, ffn_w_gate, ffn_w_up, ffn_w_down, w_router, b_router, moe_w_gate, moe_w_up, moe_w_down):
    nb, seq, d = x.shape
    assert d == D_MODEL and seq % ROW_TILE == 0 and seq % ATTN_TILE == 0 and (nb * seq) % MOE_CHUNK == 0
    depth = w_in.shape[0]
    t = nb * seq
    x2d = x.reshape(t, d)
    cos, sin = _rope_tables(positions.reshape(t, 1).astype(F32))

    for l in range(depth):
        p = _layer_params(l, mix_norm, w_in, b_forget, a_q_norm, a_k_norm, b_v_norm, b_spatial_w,
                          b_spatial_b, c_q_lat_norm, c_w_uq, c_kv_lat_norm, c_w_ukv, c_q_nope_norm,
                          c_q_rope_norm, c_k_nope_norm, c_k_rope_norm, out_norm_a, out_norm_b,
                          out_norm_c, w_out, xattn_norm, mem_norm, w_mem_q, w_mem_kv, m_q_norm,
                          m_k_norm, w_mem_out)
        qa, ka, vta, bn, cq, ckv, misc = _inproj(x2d, seq, p)
        qc, kc, vtc = _mla_prep(cq, ckv, misc, cos, sin, seq, p)
        a = _attention(qa, ka, vta, unit=1, name="attn_fox")
        c = _attention(qc, kc, vtc, unit=CHUNK, name="attn_mla")
        km, vm = _mem_kv(mem, p)
        x2d = _outproj(x2d, a, bn, c, km, vm, seq, p)
        g = ffn_norm[l][None]
        if l % 2 == 0:
            m = l // 2
            ff = ffn_w_gate.shape[2]
            ff_pad = -(-ff // (2 * LANES)) * (2 * LANES)
            wg = _pad_cols(ffn_w_gate[m], ff_pad).astype(BF16)
            wu = _pad_cols(ffn_w_up[m], ff_pad).astype(BF16)
            wd = jnp.pad(ffn_w_down[m], ((0, ff_pad - ff), (0, 0))).astype(BF16)
            x2d = _ffn(x2d, g, wg, wu, wd)
        else:
            m = l // 2
            wr = _pad_cols(w_router[m], LANES)
            wr_hi = wr.astype(BF16)
            wr = jnp.stack([wr_hi, (wr - wr_hi.astype(F32)).astype(BF16)])
            br = _pad_cols(b_router[m][None], LANES)
            x2d = _moe_sorted(x2d, g, wr, br, moe_w_gate[m].astype(BF16), moe_w_up[m].astype(BF16),
                       moe_w_down[m].astype(BF16))
    return x2d.reshape(nb, seq, d)
```

```python
import functools

import numpy as np
import jax
import jax.numpy as jnp
from jax import lax
from jax.experimental import pallas as pl
from jax.experimental.pallas import tpu as pltpu
from jax.experimental.pallas import tpu_sc as plsc

F32 = jnp.float32
BF16 = jnp.bfloat16
HIGHEST = lax.Precision.HIGHEST

D_MODEL = 1024
CHUNK = 64
EPS = 1e-6
NEG_INF = -1e30
A_HEADS, A_HEAD_DIM = 4, 64
B_GROUPS, B_GROUP_DIM, B_WINDOW = 4, 64, 128
C_HEADS, C_NOPE_DIM, C_ROPE_DIM, C_V_DIM = 8, 64, 32, 64
C_Q_RANK, C_KV_RANK = 256, 128
ROPE_THETA = 10000.0
M_HEADS, M_HEAD_DIM = 4, 128
N_EXPERTS = 8
A_W = A_HEADS * A_HEAD_DIM
B_W = B_GROUPS * B_GROUP_DIM
C_W = C_HEADS * C_V_DIM
M_W = M_HEADS * M_HEAD_DIM

LANES = 128

SEG_Q, SEG_K, SEG_V, SEG_U, SEG_VB, SEG_CQ, SEG_CKV, SEG_MISC = 0, 256, 512, 768, 1024, 1280, 1536, 1664
IN_PAD_W = SEG_MISC + LANES
ROPE_LANE = C_NOPE_DIM
FORGET_LANE = 96
HEAD_SLAB = LANES
VT_ROWS = 80
LOG2E = float(np.log2(np.e))

ROW_TILE = 512
ATTN_TILE = 512
ATTN_UNROLL = 2
MOE_CHUNK = 1024
MOE_BLOCK = 512
VMEM_LIMIT = 56 * 1024 * 1024


def _cparams(*sem):
    return pltpu.CompilerParams(dimension_semantics=sem, vmem_limit_bytes=VMEM_LIMIT)


def _full(shape):
    n = len(shape)
    return pl.BlockSpec(shape, lambda *_: (0,) * n)


def _rms(x):
    return x * lax.rsqrt(jnp.mean(x * x, axis=-1, keepdims=True) + EPS)


def _dot_split(v, exact, pieces, lhs_is_exact=False):
    total = None
    rem = v
    for n in range(pieces):
        part = rem.astype(BF16)
        if n + 1 < pieces:
            rem = rem - part.astype(F32)
        term = (jnp.dot(exact, part, preferred_element_type=F32) if lhs_is_exact
                else jnp.dot(part, exact, preferred_element_type=F32))
        total = term if total is None else total + term
    return total


def _lane_iota(n=LANES):
    return lax.broadcasted_iota(jnp.int32, (1, n), 1)


def _rope_table_kernel(pos_ref, inv_ref, sgn_ref, cos_ref, sin_ref):
    ang = pos_ref[...] * inv_ref[...]
    cos_ref[...] = jnp.cos(ang)
    sin_ref[...] = jnp.sin(ang) * sgn_ref[...]


def _rope_tables(pos_col):
    t = pos_col.shape[0]
    half = C_ROPE_DIM // 2
    inv = ROPE_THETA ** (-jnp.arange(half, dtype=F32) / half)
    inv_l = jnp.zeros((1, LANES), F32).at[0, ROPE_LANE:ROPE_LANE + C_ROPE_DIM].set(jnp.tile(inv, 2))
    sgn = np.zeros((1, LANES), np.float32)
    sgn[0, ROPE_LANE:ROPE_LANE + half] = -1.0
    sgn[0, ROPE_LANE + half:ROPE_LANE + C_ROPE_DIM] = 1.0
    tm = ROW_TILE
    return pl.pallas_call(
        _rope_table_kernel,
        grid=(t // tm,),
        in_specs=[pl.BlockSpec((tm, 1), lambda i: (i, 0)), _full((1, LANES)), _full((1, LANES))],
        out_specs=[pl.BlockSpec((tm, LANES), lambda i: (i, 0))] * 2,
        out_shape=[jax.ShapeDtypeStruct((t, LANES), F32)] * 2,
        compiler_params=_cparams("parallel"),
        name="rope_tables",
    )(pos_col, inv_l, jnp.asarray(sgn))


def _rotate(x, cos, sin_signed, lane):
    half = C_ROPE_DIM // 2
    partner = jnp.where(lane < ROPE_LANE + half,
                        pltpu.roll(x, LANES - half, 1), pltpu.roll(x, half, 1))
    return x * cos + partner * sin_signed


def _store_v_transposed(vt_ref, v, n_heads):
    tm = v.shape[0]
    v_t = v.T
    tail = jnp.where(lax.broadcasted_iota(jnp.int32, (VT_ROWS - C_V_DIM, tm), 0) == 0, 1.0, 0.0).astype(BF16)
    for hd in range(n_heads):
        vt_ref[0, hd, 0:C_V_DIM, :] = v_t[hd * C_V_DIM:(hd + 1) * C_V_DIM, :].astype(BF16)
        vt_ref[0, hd, C_V_DIM:VT_ROWS, :] = tail


def _gelu(x):
    return 0.5 * x * (1.0 + lax.erf(x * np.float32(1.0 / np.sqrt(2.0))))


def _inproj_kernel(x_ref, g_ref, w_ref, aq_ref, ak_ref, fb_ref, bvg_ref, ws_ref, bs_ref, onb_ref,
                   cqg_ref, ckvg_ref, gm_ref, tri_ref,
                   qa_ref, ka_ref, vt_ref, bn_ref, cq_ref, ckv_ref, misc_ref,
                   carry_ref, *, tiles_per_seq):
    i = pl.program_id(0)
    tm = x_ref.shape[0]
    h = _rms(x_ref[...]) * g_ref[...]
    proj = jnp.dot(h.astype(BF16), w_ref[...], preferred_element_type=F32)
    gm = gm_ref[...]

    def group_mean(v):
        return _dot_split(v, gm, 2)

    misc = proj[:, SEG_MISC:SEG_MISC + LANES]
    misc_ref[...] = misc
    z = misc + fb_ref[...]
    log_f = jnp.minimum(z, 0.0) - jnp.log1p(jnp.exp(-jnp.abs(z)))

    @pl.when(i % tiles_per_seq == 0)
    def _():
        carry_ref[...] = jnp.zeros_like(carry_ref)

    cum = _dot_split(log_f, tri_ref[...], 3, lhs_is_exact=True) + carry_ref[...]
    carry_ref[...] = cum[tm - 1:tm, :]
    f_hi = (cum * LOG2E).astype(BF16).astype(F32)
    f_rem = cum * LOG2E - f_hi
    f_mid = f_rem.astype(BF16).astype(F32)
    f_lo = f_rem - f_mid

    q = proj[:, SEG_Q:SEG_Q + A_W]
    qn = q * lax.rsqrt(group_mean(q * q) + EPS) * aq_ref[...]
    k = proj[:, SEG_K:SEG_K + A_W]
    kn = k * lax.rsqrt(group_mean(k * k) + EPS) * ak_ref[...]
    lane = _lane_iota()
    for hd in range(A_HEADS):
        pair = slice((hd // 2) * LANES, (hd // 2 + 1) * LANES)
        slab = slice(hd * HEAD_SLAB, (hd + 1) * HEAD_SLAB)
        data = (lane < A_HEAD_DIM) if hd % 2 == 0 else (lane >= A_HEAD_DIM)
        e0 = A_HEAD_DIM if hd % 2 == 0 else 0
        fl = FORGET_LANE + 8 * (hd // 2) + hd % 2
        ones = jnp.where((lane >= e0) & (lane < e0 + 3), 1.0, 0.0)
        qa_ref[:, slab] = jnp.where(data, qn[:, pair], ones).astype(BF16)
        bias = jnp.where(lane == e0, -f_hi[:, fl:fl + 1],
                         jnp.where(lane == e0 + 1, -f_mid[:, fl:fl + 1],
                                   jnp.where(lane == e0 + 2, -f_lo[:, fl:fl + 1], 0.0)))
        ka_ref[:, slab] = jnp.where(data, kn[:, pair], bias).astype(BF16)
    _store_v_transposed(vt_ref, proj[:, SEG_V:SEG_V + A_W], A_HEADS)

    u = _gelu(proj[:, SEG_U:SEG_U + B_W])
    v = _gelu(proj[:, SEG_VB:SEG_VB + B_W])
    dv = v - group_mean(v)
    vn = dv * lax.rsqrt(group_mean(dv * dv) + EPS) * bvg_ref[...]
    group = lax.broadcasted_iota(jnp.int32, (1, B_W), 1) // B_GROUP_DIM
    for w in range(tm // B_WINDOW):
        rows = slice(w * B_WINDOW, (w + 1) * B_WINDOW)
        y_all = jnp.dot(ws_ref[...], vn[rows].astype(BF16), preferred_element_type=F32)
        y = bs_ref[...]
        for g in range(B_GROUPS):
            y = y + jnp.where(group == g, y_all[g * B_WINDOW:(g + 1) * B_WINDOW], 0.0)
        b = u[rows] * y
        bn_ref[rows, :] = (_rms(b) * onb_ref[...]).astype(BF16)

    cq_ref[...] = (_rms(proj[:, SEG_CQ:SEG_CQ + C_Q_RANK]) * cqg_ref[...]).astype(BF16)
    ckv_ref[...] = (_rms(proj[:, SEG_CKV:SEG_CKV + C_KV_RANK]) * ckvg_ref[...]).astype(BF16)


def _inproj(x2d, seq, p):
    t = x2d.shape[0]
    tm = ROW_TILE
    tps = seq // tm
    nb = t // seq
    row = lambda w: pl.BlockSpec((tm, w), lambda i: (i, 0))
    qk_w = A_HEADS * HEAD_SLAB
    out_shape = [
        jax.ShapeDtypeStruct((t, qk_w), BF16), jax.ShapeDtypeStruct((t, qk_w), BF16),
        jax.ShapeDtypeStruct((nb, A_HEADS, VT_ROWS, seq), BF16),
        jax.ShapeDtypeStruct((t, B_W), BF16),
        jax.ShapeDtypeStruct((t, C_Q_RANK), BF16), jax.ShapeDtypeStruct((t, C_KV_RANK), BF16),
        jax.ShapeDtypeStruct((t, LANES), F32),
    ]
    out_specs = [row(qk_w), row(qk_w),
                 pl.BlockSpec((1, A_HEADS, VT_ROWS, tm), lambda i: (i // tps, 0, 0, i % tps)),
                 row(B_W), row(C_Q_RANK), row(C_KV_RANK), row(LANES)]
    consts = [p["mix_g"], p["w_in"], p["aq"], p["ak"], p["fb"], p["bvg"], p["ws"], p["bs"], p["onb"],
              p["cqg"], p["ckvg"], p["gm"], p["tri"]]
    return pl.pallas_call(
        functools.partial(_inproj_kernel, tiles_per_seq=tps),
        grid=(t // tm,),
        in_specs=[row(D_MODEL)] + [_full(c.shape) for c in consts],
        out_specs=out_specs,
        out_shape=out_shape,
        scratch_shapes=[pltpu.VMEM((1, LANES), F32)],
        compiler_params=_cparams("arbitrary"),
        name="in_proj",
    )(x2d, *consts)


def _mla_prep_kernel(cq_ref, ckv_ref, misc_ref, cos_ref, sin_ref, wuq_ref, wuk_ref, wuv_ref,
                     gq_ref, gkn_ref, gkr_ref, qc_ref, kc_ref, vt_ref):
    lane = _lane_iota()
    nope = lane < C_NOPE_DIM
    rope = (lane >= ROPE_LANE) & (lane < ROPE_LANE + C_ROPE_DIM)
    cos, sin = cos_ref[...], sin_ref[...]
    q = jnp.dot(cq_ref[...], wuq_ref[0], preferred_element_type=F32)
    q_partner = jnp.dot(cq_ref[...], wuq_ref[1], preferred_element_type=F32)
    q_cos = gq_ref[0:1, :] * cos
    q_sin = gq_ref[1:2, :] * sin
    kn = jnp.dot(ckv_ref[...], wuk_ref[...], preferred_element_type=F32)
    _store_v_transposed(vt_ref, jnp.dot(ckv_ref[...], wuv_ref[...], preferred_element_type=F32), C_HEADS)

    kr = jnp.where(rope, misc_ref[...], 0.0)
    kr = kr * lax.rsqrt(jnp.sum(kr * kr, axis=-1, keepdims=True) * (1.0 / C_ROPE_DIM) + EPS) * gkr_ref[...]
    kr = _rotate(kr, cos, sin, lane)

    for hd in range(C_HEADS):
        cols = slice(hd * HEAD_SLAB, (hd + 1) * HEAD_SLAB)
        qh = q[:, cols]
        sq = qh * qh
        r_n = lax.rsqrt(jnp.sum(jnp.where(nope, sq, 0.0), axis=-1, keepdims=True) * (1.0 / C_NOPE_DIM) + EPS)
        r_r = lax.rsqrt(jnp.sum(jnp.where(rope, sq, 0.0), axis=-1, keepdims=True) * (1.0 / C_ROPE_DIM) + EPS)
        qc_ref[:, cols] = (jnp.where(nope, r_n, r_r)
                           * (qh * q_cos + q_partner[:, cols] * q_sin)).astype(BF16)
        kh = kn[:, cols]
        r_k = lax.rsqrt(jnp.sum(kh * kh, axis=-1, keepdims=True) * (1.0 / C_NOPE_DIM) + EPS)
        kc_ref[:, cols] = (kh * r_k * gkn_ref[...] + kr).astype(BF16)


def _mla_prep(cq, ckv, misc, cos, sin, seq, p):
    t = cq.shape[0]
    tm = ROW_TILE
    tps = seq // tm
    row = lambda w: pl.BlockSpec((tm, w), lambda i: (i, 0))
    consts = [p["wuq"], p["wuk"], p["wuv"], p["gq"], p["gkn"], p["gkr"]]
    qk_w = C_HEADS * HEAD_SLAB
    return pl.pallas_call(
        _mla_prep_kernel,
        grid=(t // tm,),
        in_specs=[row(C_Q_RANK), row(C_KV_RANK), row(LANES), row(LANES), row(LANES)]
                 + [_full(c.shape) for c in consts],
        out_specs=[row(qk_w), row(qk_w),
                   pl.BlockSpec((1, C_HEADS, VT_ROWS, tm), lambda i: (i // tps, 0, 0, i % tps))],
        out_shape=[jax.ShapeDtypeStruct((t, qk_w), BF16), jax.ShapeDtypeStruct((t, qk_w), BF16),
                   jax.ShapeDtypeStruct((t // seq, C_HEADS, VT_ROWS, seq), BF16)],
        compiler_params=_cparams("parallel"),
        name="mla_prep",
    )(cq, ckv, misc, cos, sin, *consts)


def _attn_items(nq):
    items = [(i, j) for i in range(nq) for j in range(i)] + [(i, i) for i in range(nq)]
    return np.array(items, np.int32).T


def _attn_kernel(items_ref, q_ref, k_ref, vt_ref, mask_ref, o_ref, s0_ref, s1_ref, p0_ref, p1_ref,
                 mp0_ref, mp1_ref, mrun_ref, macc_ref, acc_ref, *, n_items):
    tk, tq = mask_ref.shape
    nq = q_ref.shape[0] // tq
    s_bufs, p_bufs, mp_bufs = (s0_ref, s1_ref), (p0_ref, p1_ref), (mp0_ref, mp1_ref)
    mrun_ref[...] = jnp.full(mrun_ref.shape, NEG_INF, F32)
    macc_ref[...] = jnp.full(macc_ref.shape, NEG_INF, F32)
    acc_ref[...] = jnp.zeros(acc_ref.shape, F32)

    def scores(it, buf, masked):
        qi = items_ref[0, it]
        q_start = pl.multiple_of(qi * tq, tq)
        k_start = pl.multiple_of(items_ref[1, it] * tk, tk)
        for hh in range(2):
            cols = slice(hh * HEAD_SLAB, (hh + 1) * HEAD_SLAB)
            s_t = lax.dot_general(k_ref[pl.ds(k_start, tk), cols], q_ref[pl.ds(q_start, tq), cols],
                                  (((1,), (1,)), ((), ())), preferred_element_type=F32)
            if masked:
                s_t = s_t + mask_ref[...]
            s_bufs[buf][hh] = s_t
            mrun_ref[qi, hh] = jnp.maximum(mrun_ref[qi, hh], jnp.max(s_t, axis=0, keepdims=True))

    def exponentiate(it, buf):
        qi = items_ref[0, it]
        for hh in range(2):
            m = mrun_ref[qi, hh]
            p_bufs[buf][hh] = jnp.exp2(s_bufs[buf][hh] - m).astype(BF16)
            mp_bufs[buf][hh] = m

    def accumulate(it, buf):
        qi = items_ref[0, it]
        k_start = pl.multiple_of(items_ref[1, it] * tk, tk)
        for hh in range(2):
            m = mp_bufs[buf][hh]
            pv = jnp.dot(vt_ref[0, hh, :, pl.ds(k_start, tk)], p_bufs[buf][hh],
                         preferred_element_type=F32)
            acc_ref[qi, hh] = jnp.exp2(macc_ref[qi, hh] - m) * acc_ref[qi, hh] + pv
            macc_ref[qi, hh] = m

    def beat(it, par, masked):
        if not isinstance(it, int) or 2 <= it <= n_items + 1:
            accumulate(it - 2, par)
        if not isinstance(it, int) or 1 <= it <= n_items:
            exponentiate(it - 1, 1 - par)
        if not isinstance(it, int) or it < n_items:
            scores(it, par, masked)

    def run_beats(lo, hi, masked):
        lo = max(lo, 0)
        if lo < hi and (lo % 2 == 1 or lo < 2):
            for it in range(lo, min(hi, lo + 2 - lo % 2)):
                beat(it, it % 2, masked)
            lo = min(hi, lo + 2 - lo % 2)
        n_groups = (hi - lo) // ATTN_UNROLL
        if n_groups > 0:
            def group(t, carry, lo=lo):
                for u in range(ATTN_UNROLL):
                    beat(lo + ATTN_UNROLL * t + u, u % 2, masked)
                return carry
            lax.fori_loop(0, n_groups, group, 0)
        for it in range(lo + ATTN_UNROLL * n_groups, hi):
            beat(it, it % 2, masked)

    n_full = n_items - nq
    run_beats(0, n_full, False)
    run_beats(n_full, n_items, True)
    for it in (n_items, n_items + 1):
        beat(it, it % 2, True)

    for qi in range(nq):
        halves = [acc_ref[qi, hh, 0:C_V_DIM, :] / acc_ref[qi, hh, C_V_DIM:C_V_DIM + 1, :] for hh in range(2)]
        o_ref[qi * tq:(qi + 1) * tq, :] = jnp.concatenate(halves, axis=0).T.astype(o_ref.dtype)


def _attention(q, k, vt, *, unit, name):
    t = q.shape[0]
    nb, n_heads, _, seq = vt.shape
    tq = ATTN_TILE
    nq = seq // tq
    items = _attn_items(nq)
    pos = np.arange(tq)
    diag_mask = np.where((pos[:, None] // unit) <= (pos[None, :] // unit), 0.0, NEG_INF)
    mask = jnp.asarray(diag_mask.astype(np.float32))
    seq_blk = lambda w: pl.BlockSpec((seq, w), lambda b, p, items_ref: (b, p))
    grid_spec = pltpu.PrefetchScalarGridSpec(
        num_scalar_prefetch=1,
        grid=(nb, n_heads // 2),
        in_specs=[seq_blk(2 * HEAD_SLAB), seq_blk(2 * HEAD_SLAB),
                  pl.BlockSpec((1, 2, VT_ROWS, seq), lambda b, p, items_ref: (b, p, 0, 0)),
                  pl.BlockSpec((tq, tq), lambda b, p, items_ref: (0, 0), pipeline_mode=pl.Buffered(1))],
        out_specs=seq_blk(2 * C_V_DIM),
        scratch_shapes=[pltpu.VMEM((2, tq, tq), F32), pltpu.VMEM((2, tq, tq), F32),
                        pltpu.VMEM((2, tq, tq), BF16), pltpu.VMEM((2, tq, tq), BF16),
                        pltpu.VMEM((2, 1, tq), F32), pltpu.VMEM((2, 1, tq), F32),
                        pltpu.VMEM((nq, 2, 1, tq), F32), pltpu.VMEM((nq, 2, 1, tq), F32),
                        pltpu.VMEM((nq, 2, VT_ROWS, tq), F32)])
    return pl.pallas_call(
        functools.partial(_attn_kernel, n_items=items.shape[1]),
        grid_spec=grid_spec,
        out_shape=jax.ShapeDtypeStruct((t, n_heads * C_V_DIM), BF16),
        compiler_params=_cparams("parallel", "parallel"),
        name=name,
    )(jnp.asarray(items), q, k, vt, mask)


def _mem_kv_kernel(mem_ref, g_ref, w_ref, kg_ref, k_ref, v_ref):
    mn = (_rms(mem_ref[0]) * g_ref[...]).astype(BF16)
    kv = jnp.dot(mn, w_ref[...], preferred_element_type=F32)
    for hd in range(M_HEADS):
        cols = slice(hd * M_HEAD_DIM, (hd + 1) * M_HEAD_DIM)
        k_ref[0, :, cols] = (_rms(kv[:, cols]) * kg_ref[...]).astype(BF16)
    v_ref[0] = kv[:, M_W:].astype(BF16)


def _mem_kv(mem, p):
    nb, ml, _ = mem.shape
    consts = [p["mem_g"], p["w_mem_kv"], p["mkg"]]
    blk = pl.BlockSpec((1, ml, M_W), lambda b: (b, 0, 0))
    return pl.pallas_call(
        _mem_kv_kernel,
        grid=(nb,),
        in_specs=[pl.BlockSpec((1, ml, D_MODEL), lambda b: (b, 0, 0))] + [_full(c.shape) for c in consts],
        out_specs=[blk, blk],
        out_shape=[jax.ShapeDtypeStruct((nb, ml, M_W), BF16)] * 2,
        compiler_params=_cparams("parallel"),
        name="mem_kv",
    )(mem, *consts)


def _outproj_kernel(x_ref, a_ref, bn_ref, c_ref, ona_ref, onc_ref, wo_ref, xg_ref, wq_ref, mqg_ref,
                    km_ref, vm_ref, wmo_ref, o_ref):
    a_n = (_rms(a_ref[...].astype(F32)) * ona_ref[...]).astype(BF16)
    c_n = (_rms(c_ref[...].astype(F32)) * onc_ref[...]).astype(BF16)
    mix = jnp.concatenate([a_n, bn_ref[...], c_n], axis=-1)
    x1 = x_ref[...] + jnp.dot(mix, wo_ref[...], preferred_element_type=F32)

    h = (_rms(x1) * xg_ref[...]).astype(BF16)
    q = jnp.dot(h, wq_ref[...], preferred_element_type=F32)
    outs = []
    for hd in range(M_HEADS):
        cols = slice(hd * M_HEAD_DIM, (hd + 1) * M_HEAD_DIM)
        qh = (_rms(q[:, cols]) * mqg_ref[...]).astype(BF16)
        s = lax.dot_general(qh, km_ref[0, :, cols], (((1,), (1,)), ((), ())), preferred_element_type=F32)
        e = jnp.exp(s - jnp.max(s, axis=-1, keepdims=True))
        pr = e / jnp.sum(e, axis=-1, keepdims=True)
        outs.append(jnp.dot(pr.astype(BF16), vm_ref[0, :, cols], preferred_element_type=F32).astype(BF16))
    o_ref[...] = x1 + jnp.dot(jnp.concatenate(outs, axis=-1), wmo_ref[...], preferred_element_type=F32)


def _outproj(x2d, a, bn, c, km, vm, seq, p):
    t = x2d.shape[0]
    tm = ROW_TILE
    tps = seq // tm
    ml = km.shape[1]
    row = lambda w: pl.BlockSpec((tm, w), lambda i: (i, 0))
    memblk = pl.BlockSpec((1, ml, M_W), lambda i: (i // tps, 0, 0))
    c1 = [p["ona"], p["onc"], p["w_out"], p["xg"], p["w_mem_q"], p["mqg"]]
    return pl.pallas_call(
        _outproj_kernel,
        grid=(t // tm,),
        in_specs=[row(D_MODEL), row(A_W), row(B_W), row(C_W)] + [_full(c_.shape) for c_ in c1]
                 + [memblk, memblk, _full(p["w_mem_out"].shape)],
        out_specs=row(D_MODEL),
        out_shape=jax.ShapeDtypeStruct((t, D_MODEL), F32),
        compiler_params=_cparams("parallel"),
        name="out_proj_mem_attn",
    )(x2d, a, bn, c, *c1, km, vm, p["w_mem_out"])


def _silu(x):
    return x * jax.nn.sigmoid(x)


def _ffn_kernel(x_ref, g_ref, wg_ref, wu_ref, wd_ref, o_ref, *, n_chunks):
    x = x_ref[...]
    h = (_rms(x) * g_ref[...]).astype(BF16)
    fc = wg_ref.shape[1] // n_chunks
    acc = x
    for c in range(n_chunks):
        cols = slice(c * fc, (c + 1) * fc)
        act = _silu(jnp.dot(h, wg_ref[:, cols], preferred_element_type=F32)) * \
            jnp.dot(h, wu_ref[:, cols], preferred_element_type=F32)
        acc = acc + jnp.dot(act.astype(BF16), wd_ref[cols, :], preferred_element_type=F32)
    o_ref[...] = acc


def _ffn(x2d, g, wg, wu, wd):
    t = x2d.shape[0]
    tm = ROW_TILE
    row = pl.BlockSpec((tm, D_MODEL), lambda i: (i, 0))
    resident = lambda a: pl.BlockSpec(a.shape, lambda i: (0, 0), pipeline_mode=pl.Buffered(1))
    return pl.pallas_call(
        functools.partial(_ffn_kernel, n_chunks=2),
        grid=(t // tm,),
        in_specs=[row, _full(g.shape), resident(wg), resident(wu), resident(wd)],
        out_specs=row,
        out_shape=jax.ShapeDtypeStruct((t, D_MODEL), F32),
        compiler_params=_cparams("parallel"),
        name="ffn_dense",
    )(x2d, g, wg, wu, wd)


def _pack_bf16_pairs(lo, hi):
    lo_bits = pltpu.bitcast(lo.astype(BF16).astype(F32), jnp.uint32)
    hi_bits = pltpu.bitcast(hi.astype(BF16).astype(F32), jnp.uint32)
    return lax.shift_right_logical(lo_bits, jnp.uint32(16)) | (hi_bits & jnp.uint32(0xFFFF0000))


def _unpack_bf16_pairs(words):
    lo = pltpu.bitcast(lax.shift_left(words, jnp.uint32(16)), F32)
    hi = pltpu.bitcast(words & jnp.uint32(0xFFFF0000), F32)
    return lo, hi


def _pack_rows(v):
    q = D_MODEL // 4
    return _pack_bf16_pairs(v[:, 0:q], v[:, q:2 * q]), _pack_bf16_pairs(v[:, 2 * q:3 * q], v[:, 3 * q:])


def _unpack_rows(a, b):
    return jnp.concatenate([*_unpack_bf16_pairs(a), *_unpack_bf16_pairs(b)], axis=-1)


def _route_kernel(x_ref, g_ref, wr_ref, br_ref, tri_ref, ha_ref, hb_ref, route_ref, cnt_ref):
    lane = _lane_iota()
    h = _rms(x_ref[...]) * g_ref[...]
    ha_ref[...], hb_ref[...] = _pack_rows(h)
    h_hi = h.astype(BF16)
    h_lo = (h - h_hi.astype(F32)).astype(BF16)
    logits = (jnp.dot(h_hi, wr_ref[0], preferred_element_type=F32)
              + jnp.dot(h_lo, wr_ref[0], preferred_element_type=F32)
              + jnp.dot(h_hi, wr_ref[1], preferred_element_type=F32)) + br_ref[...]
    logits = jnp.where(lane < N_EXPERTS, logits, -jnp.inf)
    v1 = jnp.max(logits, axis=-1, keepdims=True)
    i1 = jnp.min(jnp.where(logits == v1, lane, LANES), axis=-1, keepdims=True)
    rest = jnp.where(lane == i1, -jnp.inf, logits)
    v2 = jnp.max(rest, axis=-1, keepdims=True)
    i2 = jnp.min(jnp.where(rest == v2, lane, LANES), axis=-1, keepdims=True)
    e2 = jnp.exp(v2 - v1)
    g1 = 1.0 / (1.0 + e2)
    hit1, hit2 = lane == i1, lane == i2
    ones = jnp.where(hit1 | hit2, 1.0, 0.0)
    before = jnp.dot(tri_ref[...], ones.astype(BF16), preferred_element_type=F32)
    r1 = jnp.sum(jnp.where(hit1, before, 0.0), axis=-1, keepdims=True)
    r2 = jnp.sum(jnp.where(hit2, before, 0.0), axis=-1, keepdims=True)
    cols = [i1.astype(F32), i2.astype(F32), g1, e2 * g1, r1, r2]
    route = jnp.zeros(route_ref.shape, F32)
    for n, c in enumerate(cols):
        route = jnp.where(lane == n, c, route)
    route_ref[...] = route
    cnt_ref[0] = jnp.broadcast_to(jnp.sum(ones, axis=0, keepdims=True), cnt_ref.shape[1:])


def _route(x2d, g, wr, br):
    t = x2d.shape[0]
    tm = MOE_CHUNK
    tri = jnp.asarray(np.tril(np.ones((tm, tm), np.float32), -1)).astype(BF16)
    row = lambda w: pl.BlockSpec((tm, w), lambda i: (i, 0))
    q = D_MODEL // 4
    return pl.pallas_call(
        _route_kernel,
        grid=(t // tm,),
        in_specs=[row(D_MODEL), _full(g.shape), _full(wr.shape), _full(br.shape), _full(tri.shape)],
        out_specs=[row(q), row(q), row(LANES), pl.BlockSpec((1, 8, LANES), lambda i: (i, 0, 0))],
        out_shape=[jax.ShapeDtypeStruct((t, q), jnp.uint32), jax.ShapeDtypeStruct((t, q), jnp.uint32),
                   jax.ShapeDtypeStruct((t, LANES), F32), jax.ShapeDtypeStruct((t // tm, 8, LANES), F32)],
        compiler_params=_cparams("parallel"),
        name="moe_route",
    )(x2d, g, wr, br, tri)


def _dest_kernel(route_ref, base_ref, o_ref):
    lane = _lane_iota()
    r = route_ref[...]
    base = base_ref[0, 0:1, :]
    lane_f = lane.astype(F32)
    d1 = jnp.sum(jnp.where(lane_f == r[:, 0:1], base, 0.0), axis=-1, keepdims=True) + r[:, 4:5]
    d2 = jnp.sum(jnp.where(lane_f == r[:, 1:2], base, 0.0), axis=-1, keepdims=True) + r[:, 5:6]
    both = jnp.where(lane == 0, d1, jnp.where(lane == 1, d2, 0.0))
    o_ref[...] = both.T[0:8, :].astype(jnp.int32)


def _destinations(route, base):
    t = route.shape[0]
    tm = MOE_CHUNK
    out = pl.pallas_call(
        _dest_kernel,
        grid=(t // tm,),
        in_specs=[pl.BlockSpec((tm, LANES), lambda i: (i, 0)), pl.BlockSpec((1, 8, LANES), lambda i: (i, 0, 0))],
        out_specs=pl.BlockSpec((8, tm), lambda i: (0, i)),
        out_shape=jax.ShapeDtypeStruct((8, t), jnp.int32),
        compiler_params=_cparams("parallel"),
        name="moe_dest",
    )(route, base)
    return out[0:2].reshape(1, 2 * t)


SC_WINDOW = 128


def _sc_mesh():
    return plsc.VectorSubcoreMesh(core_axis_name="core", subcore_axis_name="subcore")


def _sc_scatter_rows(x, idx, n_out):
    n, d = x.shape
    m = idx.shape[1]
    nblk = n // SC_WINDOW

    @pl.kernel(out_type=jax.ShapeDtypeStruct((n_out, d), x.dtype), mesh=_sc_mesh(), name="moe_sc_scatter")
    def scatter(x_hbm, i_hbm, o_hbm):
        def body(x_vmem, i_vmem):
            pltpu.sync_copy(x_vmem, o_hbm.at[i_vmem.at[0]])

        half = m // SC_WINDOW // 2
        pltpu.emit_pipeline(
            body,
            grid=(2, half),
            in_specs=[pl.BlockSpec((SC_WINDOW, d), lambda c, j: ((c * half + j) % nblk, 0)),
                      pl.BlockSpec((1, SC_WINDOW), lambda c, j: (0, c * half + j))],
            out_specs=[],
            core_axis_name=("core", "subcore"),
            dimension_semantics=(pltpu.PARALLEL, pltpu.PARALLEL),
        )(x_hbm, i_hbm)

    return scatter(x, idx)


def _sc_gather_rows(table, idx):
    d = table.shape[1]
    m = idx.shape[1]

    @pl.kernel(out_type=jax.ShapeDtypeStruct((m, d), table.dtype), mesh=_sc_mesh(), name="moe_sc_gather")
    def gather(t_hbm, i_hbm, o_hbm):
        def body(i_vmem, o_vmem):
            pltpu.sync_copy(t_hbm.at[i_vmem.at[0]], o_vmem)

        half = m // SC_WINDOW // 2
        pltpu.emit_pipeline(
            body,
            grid=(2, half),
            in_specs=[pl.BlockSpec((1, SC_WINDOW), lambda c, j: (0, c * half + j))],
            out_specs=[pl.BlockSpec((SC_WINDOW, d), lambda c, j: (c * half + j, 0))],
            core_axis_name=("core", "subcore"),
            dimension_semantics=(pltpu.PARALLEL, pltpu.PARALLEL),
        )(i_hbm, o_hbm)

    return gather(table, idx)


def _expert_kernel(blk_expert_ref, n_used_ref, xa_ref, xb_ref, wg_ref, wu_ref, wd_ref, ya_ref, yb_ref):
    del blk_expert_ref

    @pl.when(pl.program_id(0) < n_used_ref[0])
    def _():
        xe = _unpack_rows(xa_ref[...], xb_ref[...]).astype(BF16)
        act = _silu(jnp.dot(xe, wg_ref[0], preferred_element_type=F32)) * \
            jnp.dot(xe, wu_ref[0], preferred_element_type=F32)
        y = jnp.dot(act.astype(BF16), wd_ref[0], preferred_element_type=F32)
        ya_ref[...], yb_ref[...] = _pack_rows(y)

    @pl.when(pl.program_id(0) >= n_used_ref[0])
    def _():
        ya_ref[...] = jnp.zeros(ya_ref.shape, ya_ref.dtype)
        yb_ref[...] = jnp.zeros(yb_ref.shape, yb_ref.dtype)


def _experts(blk_expert, n_used, xa, xb, wg, wu, wd):
    n_rows, q = xa.shape
    ff = wg.shape[2]
    blk = MOE_BLOCK
    row = pl.BlockSpec((blk, q), lambda b, be, nu: (b, 0))
    grid_spec = pltpu.PrefetchScalarGridSpec(
        num_scalar_prefetch=2,
        grid=(n_rows // blk,),
        in_specs=[row, row,
                  pl.BlockSpec((1, D_MODEL, ff), lambda b, be, nu: (be[b], 0, 0)),
                  pl.BlockSpec((1, D_MODEL, ff), lambda b, be, nu: (be[b], 0, 0)),
                  pl.BlockSpec((1, ff, D_MODEL), lambda b, be, nu: (be[b], 0, 0))],
        out_specs=[row, row])
    return pl.pallas_call(
        _expert_kernel,
        grid_spec=grid_spec,
        out_shape=[jax.ShapeDtypeStruct((n_rows, q), jnp.uint32)] * 2,
        compiler_params=_cparams("arbitrary"),
        name="moe_experts",
    )(blk_expert, n_used, xa, xb, wg, wu, wd)


def _combine_kernel(x_ref, route_ref, a1_ref, b1_ref, a2_ref, b2_ref, o_ref):
    g1 = route_ref[:, 2:3]
    g2 = route_ref[:, 3:4]
    o_ref[...] = x_ref[...] + g1 * _unpack_rows(a1_ref[...], b1_ref[...]) \
        + g2 * _unpack_rows(a2_ref[...], b2_ref[...])


def _combine(x2d, route, ya, yb):
    t = x2d.shape[0]
    tm = ROW_TILE
    nt = t // tm
    q = ya.shape[1]
    row = lambda w: pl.BlockSpec((tm, w), lambda i: (i, 0))
    first = pl.BlockSpec((tm, q), lambda i: (i, 0))
    second = pl.BlockSpec((tm, q), lambda i: (nt + i, 0))
    return pl.pallas_call(
        _combine_kernel,
        grid=(nt,),
        in_specs=[row(D_MODEL), row(LANES), first, first, second, second],
        out_specs=row(D_MODEL),
        out_shape=jax.ShapeDtypeStruct((t, D_MODEL), F32),
        compiler_params=_cparams("parallel"),
        name="moe_combine",
    )(x2d, route, ya, yb, ya, yb)


def _moe_sorted(x2d, g, wr, br, wg, wu, wd):
    t = x2d.shape[0]
    blk = MOE_BLOCK
    n_blocks = 2 * t // blk + N_EXPERTS
    ha, hb, route, cnt = _route(x2d, g, wr, br)

    cnt = cnt[:, 0, :N_EXPERTS].astype(jnp.int32)
    before_chunk = jnp.cumsum(cnt, axis=0) - cnt
    seg_blocks = (jnp.sum(cnt, axis=0) + blk - 1) // blk
    seg_end_blk = jnp.cumsum(seg_blocks)
    seg_start = (seg_end_blk - seg_blocks) * blk
    base = jnp.pad((seg_start[None, :] + before_chunk).astype(F32), ((0, 0), (0, LANES - N_EXPERTS)))
    idx = _destinations(route, jnp.broadcast_to(base[:, None, :], (base.shape[0], 8, LANES)))
    blk_expert = jnp.minimum(jnp.searchsorted(seg_end_blk, jnp.arange(n_blocks, dtype=jnp.int32), side="right"),
                             N_EXPERTS - 1).astype(jnp.int32)
    n_used = seg_end_blk[-1:].astype(jnp.int32)

    xa = _sc_scatter_rows(ha, idx, n_blocks * blk)
    xb = _sc_scatter_rows(hb, idx, n_blocks * blk)
    ya, yb = _experts(blk_expert, n_used, xa, xb, wg, wu, wd)
    return _combine(x2d, route, _sc_gather_rows(ya, idx), _sc_gather_rows(yb, idx))


def _pad_cols(w, width):
    return jnp.pad(w, ((0, 0), (0, width - w.shape[1])))


def _layer_params(l, mix_norm, w_in, b_forget, a_q_norm, a_k_norm, b_v_norm, b_spatial_w, b_spatial_b,
                  c_q_lat_norm, c_w_uq, c_kv_lat_norm, c_w_ukv, c_q_nope_norm, c_q_rope_norm,
                  c_k_nope_norm, c_k_rope_norm, out_norm_a, out_norm_b, out_norm_c, w_out,
                  xattn_norm, mem_norm, w_mem_q, w_mem_kv, m_q_norm, m_k_norm, w_mem_out):
    p = {}
    o = np.cumsum((0, A_W, A_W, A_W, A_HEADS, B_W, B_W, C_Q_RANK, C_KV_RANK, C_ROPE_DIM))
    w = w_in[l]
    seg = lambda n: w[:, o[n]:o[n + 1]]
    fa = seg(3)
    misc = jnp.zeros((D_MODEL, LANES), F32)
    misc = misc.at[:, ROPE_LANE:ROPE_LANE + C_ROPE_DIM].set(seg(8))
    fb = jnp.zeros((1, LANES), F32)
    for hd in range(A_HEADS):
        ln = FORGET_LANE + 8 * (hd // 2) + hd % 2
        misc = misc.at[:, ln].set(fa[:, hd])
        fb = fb.at[0, ln].set(b_forget[l, hd])
    p["w_in"] = jnp.concatenate([seg(0), seg(1), seg(2), seg(4), seg(5), seg(6), seg(7), misc], axis=1).astype(BF16)
    p["fb"] = fb
    p["mix_g"] = mix_norm[l][None]
    p["aq"] = jnp.tile(a_q_norm[l], A_HEADS)[None] * (A_HEAD_DIM ** -0.5 * LOG2E)
    p["ak"] = jnp.tile(a_k_norm[l], A_HEADS)[None]
    p["bvg"] = b_v_norm[l][None]
    pos = np.arange(B_WINDOW)
    mask = (pos[None, :] // CHUNK) <= (pos[:, None] // CHUNK)
    p["ws"] = jnp.where(mask[None], b_spatial_w[l], 0.0).reshape(B_GROUPS * B_WINDOW, B_WINDOW).astype(BF16)
    p["bs"] = jnp.repeat(b_spatial_b[l].T, B_GROUP_DIM, axis=1)
    p["onb"] = out_norm_b[l][None]
    p["cqg"] = c_q_lat_norm[l][None]
    p["ckvg"] = c_kv_lat_norm[l][None]
    gidx = np.arange(A_W) // A_HEAD_DIM
    p["gm"] = jnp.asarray((gidx[:, None] == gidx[None, :]).astype(np.float32) / A_HEAD_DIM).astype(BF16)
    p["tri"] = jnp.asarray(np.tril(np.ones((ROW_TILE, ROW_TILE), np.float32))).astype(BF16)

    qd = C_NOPE_DIM + C_ROPE_DIM
    half = C_ROPE_DIM // 2
    wq = c_w_uq[l]
    wq_partner = jnp.concatenate([jnp.zeros_like(wq[:, :, :C_NOPE_DIM]), wq[:, :, C_NOPE_DIM + half:],
                                  wq[:, :, C_NOPE_DIM:C_NOPE_DIM + half]], axis=-1)
    p["wuq"] = jnp.pad(jnp.stack([wq, wq_partner]), ((0, 0), (0, 0), (0, 0), (0, HEAD_SLAB - qd))
                       ).reshape(2, C_Q_RANK, -1).astype(BF16)
    wukv = c_w_ukv[l]
    p["wuk"] = jnp.pad(wukv[:, :, :C_NOPE_DIM], ((0, 0), (0, 0), (0, HEAD_SLAB - C_NOPE_DIM))
                       ).reshape(C_KV_RANK, -1).astype(BF16)
    p["wuv"] = wukv[:, :, C_NOPE_DIM:].reshape(C_KV_RANK, C_W).astype(BF16)
    gq = jnp.concatenate([c_q_nope_norm[l], c_q_rope_norm[l]])
    gq_partner = jnp.concatenate([jnp.zeros_like(c_q_nope_norm[l]), c_q_rope_norm[l][half:],
                                  c_q_rope_norm[l][:half]])
    p["gq"] = _pad_cols(jnp.stack([gq, gq_partner]) * (qd ** -0.5 * LOG2E), LANES)
    p["gkn"] = _pad_cols(c_k_nope_norm[l][None], LANES)
    p["gkr"] = jnp.zeros((1, LANES), F32).at[0, ROPE_LANE:ROPE_LANE + C_ROPE_DIM].set(c_k_rope_norm[l])

    p["ona"] = out_norm_a[l][None]
    p["onc"] = out_norm_c[l][None]
    p["w_out"] = w_out[l].astype(BF16)
    p["xg"] = xattn_norm[l][None]
    p["w_mem_q"] = w_mem_q[l].astype(BF16)
    p["mqg"] = m_q_norm[l][None] * (M_HEAD_DIM ** -0.5)
    p["mem_g"] = mem_norm[l][None]
    p["w_mem_kv"] = w_mem_kv[l].astype(BF16)
    p["mkg"] = m_k_norm[l][None]
    p["w_mem_out"] = w_mem_out[l].astype(BF16)
    return p


def kernel(x, mem, positions, mix_norm, w_in, b_forget, a_q_norm, a_k_norm, b_v_norm, b_spatial_w, b_spatial_b, c_q_lat_norm, c_w_uq, c_kv_lat_norm, c_w_ukv, c_q_nope_norm, c_q_rope_norm, c_k_nope_norm, c_k_rope_norm, out_norm_a, out_norm_b, out_norm_c, w_out, xattn_norm, mem_norm, w_mem_q, w_mem_kv, m_q_norm, m_k_norm, w_mem_out, ffn_norm, ffn_w_gate, ffn_w_up, ffn_w_down, w_router, b_router, moe_w_gate, moe_w_up, moe_w_down):
    nb, seq, d = x.shape
    assert d == D_MODEL and seq % ROW_TILE == 0 and seq % ATTN_TILE == 0 and (nb * seq) % MOE_CHUNK == 0
    depth = w_in.shape[0]
    t = nb * seq
    x2d = x.reshape(t, d)
    cos, sin = _rope_tables(positions.reshape(t, 1).astype(F32))

    for l in range(depth):
        p = _layer_params(l, mix_norm, w_in, b_forget, a_q_norm, a_k_norm, b_v_norm, b_spatial_w,
                          b_spatial_b, c_q_lat_norm, c_w_uq, c_kv_lat_norm, c_w_ukv, c_q_nope_norm,
                          c_q_rope_norm, c_k_nope_norm, c_k_rope_norm, out_norm_a, out_norm_b,
                          out_norm_c, w_out, xattn_norm, mem_norm, w_mem_q, w_mem_kv, m_q_norm,
                          m_k_norm, w_mem_out)
        qa, ka, vta, bn, cq, ckv, misc = _inproj(x2d, seq, p)
        qc, kc, vtc = _mla_prep(cq, ckv, misc, cos, sin, seq, p)
        a = _attention(qa, ka, vta, unit=1, name="attn_fox")
        c = _attention(qc, kc, vtc, unit=CHUNK, name="attn_mla")
        km, vm = _mem_kv(mem, p)
        x2d = _outproj(x2d, a, bn, c, km, vm, seq, p)
        g = ffn_norm[l][None]
        if l % 2 == 0:
            m = l // 2
            ff = ffn_w_gate.shape[2]
            ff_pad = -(-ff // (2 * LANES)) * (2 * LANES)
            wg = _pad_cols(ffn_w_gate[m], ff_pad).astype(BF16)
            wu = _pad_cols(ffn_w_up[m], ff_pad).astype(BF16)
            wd = jnp.pad(ffn_w_down[m], ((0, ff_pad - ff), (0, 0))).astype(BF16)
            x2d = _ffn(x2d, g, wg, wu, wd)
        else:
            m = l // 2
            wr = _pad_cols(w_router[m], LANES)
            wr_hi = wr.astype(BF16)
            wr = jnp.stack([wr_hi, (wr - wr_hi.astype(F32)).astype(BF16)])
            br = _pad_cols(b_router[m][None], LANES)
            x2d = _moe_sorted(x2d, g, wr, br, moe_w_gate[m].astype(BF16), moe_w_up[m].astype(BF16),
                       moe_w_down[m].astype(BF16))
    return x2d.reshape(nb, seq, d)
```

```python
import functools

import numpy as np
import jax
import jax.numpy as jnp
from jax import lax
from jax.experimental import pallas as pl
from jax.experimental.pallas import tpu as pltpu
from jax.experimental.pallas import tpu_sc as plsc

F32 = jnp.float32
BF16 = jnp.bfloat16
HIGHEST = lax.Precision.HIGHEST

D_MODEL = 1024
CHUNK = 64
EPS = 1e-6
NEG_INF = -1e30
A_HEADS, A_HEAD_DIM = 4, 64
B_GROUPS, B_GROUP_DIM, B_WINDOW = 4, 64, 128
C_HEADS, C_NOPE_DIM, C_ROPE_DIM, C_V_DIM = 8, 64, 32, 64
C_Q_RANK, C_KV_RANK = 256, 128
ROPE_THETA = 10000.0
M_HEADS, M_HEAD_DIM = 4, 128
N_EXPERTS = 8
A_W = A_HEADS * A_HEAD_DIM
B_W = B_GROUPS * B_GROUP_DIM
C_W = C_HEADS * C_V_DIM
M_W = M_HEADS * M_HEAD_DIM

LANES = 128

SEG_Q, SEG_K, SEG_V, SEG_U, SEG_VB, SEG_CQ, SEG_CKV, SEG_MISC = 0, 256, 512, 768, 1024, 1280, 1536, 1664
IN_PAD_W = SEG_MISC + LANES
ROPE_LANE = C_NOPE_DIM
FORGET_LANE = 96
HEAD_SLAB = LANES
VT_ROWS = 80
LOG2E = float(np.log2(np.e))

ROW_TILE = 512
FFN_TILE = 1024
ATTN_TILE = 512
ATTN_UNROLL = 2
MOE_CHUNK = 1024
MOE_BLOCK = 512
VMEM_LIMIT = 56 * 1024 * 1024


def _cparams(*sem):
    return pltpu.CompilerParams(dimension_semantics=sem, vmem_limit_bytes=VMEM_LIMIT)


def _full(shape):
    n = len(shape)
    return pl.BlockSpec(shape, lambda *_: (0,) * n)


def _rms(x):
    return x * lax.rsqrt(jnp.mean(x * x, axis=-1, keepdims=True) + EPS)


def _dot_split(v, exact, pieces, lhs_is_exact=False):
    total = None
    rem = v
    for n in range(pieces):
        part = rem.astype(BF16)
        if n + 1 < pieces:
            rem = rem - part.astype(F32)
        term = (jnp.dot(exact, part, preferred_element_type=F32) if lhs_is_exact
                else jnp.dot(part, exact, preferred_element_type=F32))
        total = term if total is None else total + term
    return total


def _lane_iota(n=LANES):
    return lax.broadcasted_iota(jnp.int32, (1, n), 1)


def _rope_table_kernel(pos_ref, inv_ref, sgn_ref, cos_ref, sin_ref):
    ang = pos_ref[...] * inv_ref[...]
    cos_ref[...] = jnp.cos(ang)
    sin_ref[...] = jnp.sin(ang) * sgn_ref[...]


def _rope_tables(pos_col):
    t = pos_col.shape[0]
    half = C_ROPE_DIM // 2
    inv = ROPE_THETA ** (-jnp.arange(half, dtype=F32) / half)
    inv_l = jnp.zeros((1, LANES), F32).at[0, ROPE_LANE:ROPE_LANE + C_ROPE_DIM].set(jnp.tile(inv, 2))
    sgn = np.zeros((1, LANES), np.float32)
    sgn[0, ROPE_LANE:ROPE_LANE + half] = -1.0
    sgn[0, ROPE_LANE + half:ROPE_LANE + C_ROPE_DIM] = 1.0
    tm = ROW_TILE
    return pl.pallas_call(
        _rope_table_kernel,
        grid=(t // tm,),
        in_specs=[pl.BlockSpec((tm, 1), lambda i: (i, 0)), _full((1, LANES)), _full((1, LANES))],
        out_specs=[pl.BlockSpec((tm, LANES), lambda i: (i, 0))] * 2,
        out_shape=[jax.ShapeDtypeStruct((t, LANES), F32)] * 2,
        compiler_params=_cparams("parallel"),
        name="rope_tables",
    )(pos_col, inv_l, jnp.asarray(sgn))


def _rotate(x, cos, sin_signed, lane):
    half = C_ROPE_DIM // 2
    partner = jnp.where(lane < ROPE_LANE + half,
                        pltpu.roll(x, LANES - half, 1), pltpu.roll(x, half, 1))
    return x * cos + partner * sin_signed


def _store_v_transposed(vt_ref, v, n_heads):
    tm = v.shape[0]
    v_t = v.T
    tail = jnp.where(lax.broadcasted_iota(jnp.int32, (VT_ROWS - C_V_DIM, tm), 0) == 0, 1.0, 0.0).astype(BF16)
    for hd in range(n_heads):
        vt_ref[0, hd, 0:C_V_DIM, :] = v_t[hd * C_V_DIM:(hd + 1) * C_V_DIM, :].astype(BF16)
        vt_ref[0, hd, C_V_DIM:VT_ROWS, :] = tail


def _gelu(x):
    return 0.5 * x * (1.0 + lax.erf(x * np.float32(1.0 / np.sqrt(2.0))))


def _inproj_kernel(x_ref, g_ref, w_ref, aq_ref, ak_ref, fb_ref, bvg_ref, ws_ref, bs_ref, onb_ref,
                   cqg_ref, ckvg_ref, gm_ref, tri_ref,
                   qa_ref, ka_ref, vt_ref, bn_ref, cq_ref, ckv_ref, misc_ref,
                   carry_ref, *, tiles_per_seq):
    i = pl.program_id(0)
    tm = x_ref.shape[0]
    h = _rms(x_ref[...]) * g_ref[...]
    proj = jnp.dot(h.astype(BF16), w_ref[...], preferred_element_type=F32)
    gm = gm_ref[...]

    def group_mean(v):
        return _dot_split(v, gm, 2)

    misc = proj[:, SEG_MISC:SEG_MISC + LANES]
    misc_ref[...] = misc
    z = misc + fb_ref[...]
    log_f = jnp.minimum(z, 0.0) - jnp.log1p(jnp.exp(-jnp.abs(z)))

    @pl.when(i % tiles_per_seq == 0)
    def _():
        carry_ref[...] = jnp.zeros_like(carry_ref)

    cum = _dot_split(log_f, tri_ref[...], 3, lhs_is_exact=True) + carry_ref[...]
    carry_ref[...] = cum[tm - 1:tm, :]
    f_hi = (cum * LOG2E).astype(BF16).astype(F32)
    f_rem = cum * LOG2E - f_hi
    f_mid = f_rem.astype(BF16).astype(F32)
    f_lo = f_rem - f_mid

    q = proj[:, SEG_Q:SEG_Q + A_W]
    qn = q * lax.rsqrt(group_mean(q * q) + EPS) * aq_ref[...]
    k = proj[:, SEG_K:SEG_K + A_W]
    kn = k * lax.rsqrt(group_mean(k * k) + EPS) * ak_ref[...]
    lane = _lane_iota()
    for hd in range(A_HEADS):
        pair = slice((hd // 2) * LANES, (hd // 2 + 1) * LANES)
        slab = slice(hd * HEAD_SLAB, (hd + 1) * HEAD_SLAB)
        data = (lane < A_HEAD_DIM) if hd % 2 == 0 else (lane >= A_HEAD_DIM)
        e0 = A_HEAD_DIM if hd % 2 == 0 else 0
        fl = FORGET_LANE + 8 * (hd // 2) + hd % 2
        ones = jnp.where((lane >= e0) & (lane < e0 + 3), 1.0, 0.0)
        qa_ref[:, slab] = jnp.where(data, qn[:, pair], ones).astype(BF16)
        bias = jnp.where(lane == e0, -f_hi[:, fl:fl + 1],
                         jnp.where(lane == e0 + 1, -f_mid[:, fl:fl + 1],
                                   jnp.where(lane == e0 + 2, -f_lo[:, fl:fl + 1], 0.0)))
        ka_ref[:, slab] = jnp.where(data, kn[:, pair], bias).astype(BF16)
    _store_v_transposed(vt_ref, proj[:, SEG_V:SEG_V + A_W], A_HEADS)

    u = _gelu(proj[:, SEG_U:SEG_U + B_W])
    v = _gelu(proj[:, SEG_VB:SEG_VB + B_W])
    dv = v - group_mean(v)
    vn = dv * lax.rsqrt(group_mean(dv * dv) + EPS) * bvg_ref[...]
    group = lax.broadcasted_iota(jnp.int32, (1, B_W), 1) // B_GROUP_DIM
    for w in range(tm // B_WINDOW):
        rows = slice(w * B_WINDOW, (w + 1) * B_WINDOW)
        y_all = jnp.dot(ws_ref[...], vn[rows].astype(BF16), preferred_element_type=F32)
        y = bs_ref[...]
        for g in range(B_GROUPS):
            y = y + jnp.where(group == g, y_all[g * B_WINDOW:(g + 1) * B_WINDOW], 0.0)
        b = u[rows] * y
        bn_ref[rows, :] = (_rms(b) * onb_ref[...]).astype(BF16)

    cq_ref[...] = (_rms(proj[:, SEG_CQ:SEG_CQ + C_Q_RANK]) * cqg_ref[...]).astype(BF16)
    ckv_ref[...] = (_rms(proj[:, SEG_CKV:SEG_CKV + C_KV_RANK]) * ckvg_ref[...]).astype(BF16)


def _inproj(x2d, seq, p):
    t = x2d.shape[0]
    tm = ROW_TILE
    tps = seq // tm
    nb = t // seq
    row = lambda w: pl.BlockSpec((tm, w), lambda i: (i, 0))
    qk_w = A_HEADS * HEAD_SLAB
    out_shape = [
        jax.ShapeDtypeStruct((t, qk_w), BF16), jax.ShapeDtypeStruct((t, qk_w), BF16),
        jax.ShapeDtypeStruct((nb, A_HEADS, VT_ROWS, seq), BF16),
        jax.ShapeDtypeStruct((t, B_W), BF16),
        jax.ShapeDtypeStruct((t, C_Q_RANK), BF16), jax.ShapeDtypeStruct((t, C_KV_RANK), BF16),
        jax.ShapeDtypeStruct((t, LANES), F32),
    ]
    out_specs = [row(qk_w), row(qk_w),
                 pl.BlockSpec((1, A_HEADS, VT_ROWS, tm), lambda i: (i // tps, 0, 0, i % tps)),
                 row(B_W), row(C_Q_RANK), row(C_KV_RANK), row(LANES)]
    consts = [p["mix_g"], p["w_in"], p["aq"], p["ak"], p["fb"], p["bvg"], p["ws"], p["bs"], p["onb"],
              p["cqg"], p["ckvg"], p["gm"], p["tri"]]
    return pl.pallas_call(
        functools.partial(_inproj_kernel, tiles_per_seq=tps),
        grid=(t // tm,),
        in_specs=[row(D_MODEL)] + [_full(c.shape) for c in consts],
        out_specs=out_specs,
        out_shape=out_shape,
        scratch_shapes=[pltpu.VMEM((1, LANES), F32)],
        compiler_params=_cparams("arbitrary"),
        name="in_proj",
    )(x2d, *consts)


def _mla_prep_kernel(cq_ref, ckv_ref, misc_ref, cos_ref, sin_ref, wuq_ref, wuk_ref, wuv_ref,
                     gq_ref, gkn_ref, gkr_ref, qc_ref, kc_ref, vt_ref):
    lane = _lane_iota()
    nope = lane < C_NOPE_DIM
    rope = (lane >= ROPE_LANE) & (lane < ROPE_LANE + C_ROPE_DIM)
    cos, sin = cos_ref[...], sin_ref[...]
    q = jnp.dot(cq_ref[...], wuq_ref[0], preferred_element_type=F32)
    q_partner = jnp.dot(cq_ref[...], wuq_ref[1], preferred_element_type=F32)
    q_cos = gq_ref[0:1, :] * cos
    q_sin = gq_ref[1:2, :] * sin
    kn = jnp.dot(ckv_ref[...], wuk_ref[...], preferred_element_type=F32)
    _store_v_transposed(vt_ref, jnp.dot(ckv_ref[...], wuv_ref[...], preferred_element_type=F32), C_HEADS)

    kr = jnp.where(rope, misc_ref[...], 0.0)
    kr = kr * lax.rsqrt(jnp.sum(kr * kr, axis=-1, keepdims=True) * (1.0 / C_ROPE_DIM) + EPS) * gkr_ref[...]
    kr = _rotate(kr, cos, sin, lane)

    for hd in range(C_HEADS):
        cols = slice(hd * HEAD_SLAB, (hd + 1) * HEAD_SLAB)
        qh = q[:, cols]
        sq = qh * qh
        r_n = lax.rsqrt(jnp.sum(jnp.where(nope, sq, 0.0), axis=-1, keepdims=True) * (1.0 / C_NOPE_DIM) + EPS)
        r_r = lax.rsqrt(jnp.sum(jnp.where(rope, sq, 0.0), axis=-1, keepdims=True) * (1.0 / C_ROPE_DIM) + EPS)
        qc_ref[:, cols] = (jnp.where(nope, r_n, r_r)
                           * (qh * q_cos + q_partner[:, cols] * q_sin)).astype(BF16)
        kh = kn[:, cols]
        r_k = lax.rsqrt(jnp.sum(kh * kh, axis=-1, keepdims=True) * (1.0 / C_NOPE_DIM) + EPS)
        kc_ref[:, cols] = (kh * r_k * gkn_ref[...] + kr).astype(BF16)


def _mla_prep(cq, ckv, misc, cos, sin, seq, p):
    t = cq.shape[0]
    tm = ROW_TILE
    tps = seq // tm
    row = lambda w: pl.BlockSpec((tm, w), lambda i: (i, 0))
    consts = [p["wuq"], p["wuk"], p["wuv"], p["gq"], p["gkn"], p["gkr"]]
    qk_w = C_HEADS * HEAD_SLAB
    return pl.pallas_call(
        _mla_prep_kernel,
        grid=(t // tm,),
        in_specs=[row(C_Q_RANK), row(C_KV_RANK), row(LANES), row(LANES), row(LANES)]
                 + [_full(c.shape) for c in consts],
        out_specs=[row(qk_w), row(qk_w),
                   pl.BlockSpec((1, C_HEADS, VT_ROWS, tm), lambda i: (i // tps, 0, 0, i % tps))],
        out_shape=[jax.ShapeDtypeStruct((t, qk_w), BF16), jax.ShapeDtypeStruct((t, qk_w), BF16),
                   jax.ShapeDtypeStruct((t // seq, C_HEADS, VT_ROWS, seq), BF16)],
        compiler_params=_cparams("parallel"),
        name="mla_prep",
    )(cq, ckv, misc, cos, sin, *consts)


def _attn_items(nq):
    items = [(i, j) for i in range(nq) for j in range(i)] + [(i, i) for i in range(nq)]
    return np.array(items, np.int32).T


def _attn_kernel(items_ref, q_ref, k_ref, vt_ref, mask_ref, o_ref, s0_ref, s1_ref, p0_ref, p1_ref,
                 mp0_ref, mp1_ref, mrun_ref, macc_ref, acc_ref, *, n_items):
    tk, tq = mask_ref.shape
    nq = q_ref.shape[0] // tq
    s_bufs, p_bufs, mp_bufs = (s0_ref, s1_ref), (p0_ref, p1_ref), (mp0_ref, mp1_ref)
    mrun_ref[...] = jnp.full(mrun_ref.shape, NEG_INF, F32)
    macc_ref[...] = jnp.full(macc_ref.shape, NEG_INF, F32)
    acc_ref[...] = jnp.zeros(acc_ref.shape, F32)

    def scores(it, buf, masked):
        qi = items_ref[0, it]
        q_start = pl.multiple_of(qi * tq, tq)
        k_start = pl.multiple_of(items_ref[1, it] * tk, tk)
        for hh in range(2):
            cols = slice(hh * HEAD_SLAB, (hh + 1) * HEAD_SLAB)
            s_t = lax.dot_general(k_ref[pl.ds(k_start, tk), cols], q_ref[pl.ds(q_start, tq), cols],
                                  (((1,), (1,)), ((), ())), preferred_element_type=F32)
            if masked:
                s_t = s_t + mask_ref[...]
            s_bufs[buf][hh] = s_t
            mrun_ref[qi, hh] = jnp.maximum(mrun_ref[qi, hh], jnp.max(s_t, axis=0, keepdims=True))

    def exponentiate(it, buf):
        qi = items_ref[0, it]
        for hh in range(2):
            m = mrun_ref[qi, hh]
            p_bufs[buf][hh] = jnp.exp2(s_bufs[buf][hh] - m).astype(BF16)
            mp_bufs[buf][hh] = m

    def accumulate(it, buf):
        qi = items_ref[0, it]
        k_start = pl.multiple_of(items_ref[1, it] * tk, tk)
        for hh in range(2):
            m = mp_bufs[buf][hh]
            pv = jnp.dot(vt_ref[0, hh, :, pl.ds(k_start, tk)], p_bufs[buf][hh],
                         preferred_element_type=F32)
            acc_ref[qi, hh] = jnp.exp2(macc_ref[qi, hh] - m) * acc_ref[qi, hh] + pv
            macc_ref[qi, hh] = m

    def beat(it, par, masked):
        if not isinstance(it, int) or 2 <= it <= n_items + 1:
            accumulate(it - 2, par)
        if not isinstance(it, int) or 1 <= it <= n_items:
            exponentiate(it - 1, 1 - par)
        if not isinstance(it, int) or it < n_items:
            scores(it, par, masked)

    def run_beats(lo, hi, masked):
        lo = max(lo, 0)
        if lo < hi and (lo % 2 == 1 or lo < 2):
            for it in range(lo, min(hi, lo + 2 - lo % 2)):
                beat(it, it % 2, masked)
            lo = min(hi, lo + 2 - lo % 2)
        n_groups = (hi - lo) // ATTN_UNROLL
        if n_groups > 0:
            def group(t, carry, lo=lo):
                for u in range(ATTN_UNROLL):
                    beat(lo + ATTN_UNROLL * t + u, u % 2, masked)
                return carry
            lax.fori_loop(0, n_groups, group, 0)
        for it in range(lo + ATTN_UNROLL * n_groups, hi):
            beat(it, it % 2, masked)

    n_full = n_items - nq
    run_beats(0, n_full, False)
    run_beats(n_full, n_items, True)
    for it in (n_items, n_items + 1):
        beat(it, it % 2, True)

    for qi in range(nq):
        halves = [acc_ref[qi, hh, 0:C_V_DIM, :] / acc_ref[qi, hh, C_V_DIM:C_V_DIM + 1, :] for hh in range(2)]
        o_ref[qi * tq:(qi + 1) * tq, :] = jnp.concatenate(halves, axis=0).T.astype(o_ref.dtype)


def _attention(q, k, vt, *, unit, name):
    t = q.shape[0]
    nb, n_heads, _, seq = vt.shape
    tq = ATTN_TILE
    nq = seq // tq
    items = _attn_items(nq)
    pos = np.arange(tq)
    diag_mask = np.where((pos[:, None] // unit) <= (pos[None, :] // unit), 0.0, NEG_INF)
    mask = jnp.asarray(diag_mask.astype(np.float32))
    seq_blk = lambda w: pl.BlockSpec((seq, w), lambda b, p, items_ref: (b, p))
    grid_spec = pltpu.PrefetchScalarGridSpec(
        num_scalar_prefetch=1,
        grid=(nb, n_heads // 2),
        in_specs=[seq_blk(2 * HEAD_SLAB), seq_blk(2 * HEAD_SLAB),
                  pl.BlockSpec((1, 2, VT_ROWS, seq), lambda b, p, items_ref: (b, p, 0, 0)),
                  pl.BlockSpec((tq, tq), lambda b, p, items_ref: (0, 0), pipeline_mode=pl.Buffered(1))],
        out_specs=seq_blk(2 * C_V_DIM),
        scratch_shapes=[pltpu.VMEM((2, tq, tq), F32), pltpu.VMEM((2, tq, tq), F32),
                        pltpu.VMEM((2, tq, tq), BF16), pltpu.VMEM((2, tq, tq), BF16),
                        pltpu.VMEM((2, 1, tq), F32), pltpu.VMEM((2, 1, tq), F32),
                        pltpu.VMEM((nq, 2, 1, tq), F32), pltpu.VMEM((nq, 2, 1, tq), F32),
                        pltpu.VMEM((nq, 2, VT_ROWS, tq), F32)])
    return pl.pallas_call(
        functools.partial(_attn_kernel, n_items=items.shape[1]),
        grid_spec=grid_spec,
        out_shape=jax.ShapeDtypeStruct((t, n_heads * C_V_DIM), BF16),
        compiler_params=_cparams("parallel", "parallel"),
        name=name,
    )(jnp.asarray(items), q, k, vt, mask)


def _mem_kv_kernel(mem_ref, g_ref, w_ref, kg_ref, k_ref, v_ref):
    mn = (_rms(mem_ref[0]) * g_ref[...]).astype(BF16)
    kv = jnp.dot(mn, w_ref[...], preferred_element_type=F32)
    for hd in range(M_HEADS):
        cols = slice(hd * M_HEAD_DIM, (hd + 1) * M_HEAD_DIM)
        k_ref[0, :, cols] = (_rms(kv[:, cols]) * kg_ref[...]).astype(BF16)
    v_ref[0] = kv[:, M_W:].astype(BF16)


def _mem_kv(mem, p):
    nb, ml, _ = mem.shape
    consts = [p["mem_g"], p["w_mem_kv"], p["mkg"]]
    blk = pl.BlockSpec((1, ml, M_W), lambda b: (b, 0, 0))
    return pl.pallas_call(
        _mem_kv_kernel,
        grid=(nb,),
        in_specs=[pl.BlockSpec((1, ml, D_MODEL), lambda b: (b, 0, 0))] + [_full(c.shape) for c in consts],
        out_specs=[blk, blk],
        out_shape=[jax.ShapeDtypeStruct((nb, ml, M_W), BF16)] * 2,
        compiler_params=_cparams("parallel"),
        name="mem_kv",
    )(mem, *consts)


def _outproj_kernel(x_ref, a_ref, bn_ref, c_ref, ona_ref, onc_ref, wo_ref, xg_ref, wq_ref, mqg_ref,
                    km_ref, vm_ref, wmo_ref, o_ref):
    a_n = (_rms(a_ref[...].astype(F32)) * ona_ref[...]).astype(BF16)
    c_n = (_rms(c_ref[...].astype(F32)) * onc_ref[...]).astype(BF16)
    mix = jnp.concatenate([a_n, bn_ref[...], c_n], axis=-1)
    x1 = x_ref[...] + jnp.dot(mix, wo_ref[...], preferred_element_type=F32)

    h = (_rms(x1) * xg_ref[...]).astype(BF16)
    q = jnp.dot(h, wq_ref[...], preferred_element_type=F32)
    outs = []
    for hd in range(M_HEADS):
        cols = slice(hd * M_HEAD_DIM, (hd + 1) * M_HEAD_DIM)
        qh = (_rms(q[:, cols]) * mqg_ref[...]).astype(BF16)
        s = lax.dot_general(qh, km_ref[0, :, cols], (((1,), (1,)), ((), ())), preferred_element_type=F32)
        e = jnp.exp2(s - jnp.max(s, axis=-1, keepdims=True))
        pv = jnp.dot(e.astype(BF16), vm_ref[0, :, cols], preferred_element_type=F32)
        outs.append((pv / jnp.sum(e, axis=-1, keepdims=True)).astype(BF16))
    o_ref[...] = x1 + jnp.dot(jnp.concatenate(outs, axis=-1), wmo_ref[...], preferred_element_type=F32)


def _outproj(x2d, a, bn, c, km, vm, seq, p):
    t = x2d.shape[0]
    tm = ROW_TILE
    tps = seq // tm
    ml = km.shape[1]
    row = lambda w: pl.BlockSpec((tm, w), lambda i: (i, 0))
    memblk = pl.BlockSpec((1, ml, M_W), lambda i: (i // tps, 0, 0))
    c1 = [p["ona"], p["onc"], p["w_out"], p["xg"], p["w_mem_q"], p["mqg"]]
    return pl.pallas_call(
        _outproj_kernel,
        grid=(t // tm,),
        in_specs=[row(D_MODEL), row(A_W), row(B_W), row(C_W)] + [_full(c_.shape) for c_ in c1]
                 + [memblk, memblk, _full(p["w_mem_out"].shape)],
        out_specs=row(D_MODEL),
        out_shape=jax.ShapeDtypeStruct((t, D_MODEL), F32),
        compiler_params=_cparams("parallel"),
        name="out_proj_mem_attn",
    )(x2d, a, bn, c, *c1, km, vm, p["w_mem_out"])


def _silu(x):
    return x * jax.nn.sigmoid(x)


def _ffn_kernel(x_ref, g_ref, wg_ref, wu_ref, wd_ref, o_ref, *, n_chunks):
    x = x_ref[...]
    h = (_rms(x) * g_ref[...]).astype(BF16)
    fc = wg_ref.shape[1] // n_chunks
    acc = x
    for c in range(n_chunks):
        cols = slice(c * fc, (c + 1) * fc)
        act = _silu(jnp.dot(h, wg_ref[:, cols], preferred_element_type=F32)) * \
            jnp.dot(h, wu_ref[:, cols], preferred_element_type=F32)
        acc = acc + jnp.dot(act.astype(BF16), wd_ref[cols, :], preferred_element_type=F32)
    o_ref[...] = acc


def _ffn(x2d, g, wg, wu, wd):
    t = x2d.shape[0]
    tm = FFN_TILE
    row = pl.BlockSpec((tm, D_MODEL), lambda i: (i, 0))
    resident = lambda a: pl.BlockSpec(a.shape, lambda i: (0, 0), pipeline_mode=pl.Buffered(1))
    return pl.pallas_call(
        functools.partial(_ffn_kernel, n_chunks=2),
        grid=(t // tm,),
        in_specs=[row, _full(g.shape), resident(wg), resident(wu), resident(wd)],
        out_specs=row,
        out_shape=jax.ShapeDtypeStruct((t, D_MODEL), F32),
        compiler_params=_cparams("parallel"),
        name="ffn_dense",
    )(x2d, g, wg, wu, wd)


def _pack_bf16_pairs(lo, hi):
    lo_bits = pltpu.bitcast(lo.astype(BF16).astype(F32), jnp.uint32)
    hi_bits = pltpu.bitcast(hi.astype(BF16).astype(F32), jnp.uint32)
    return lax.shift_right_logical(lo_bits, jnp.uint32(16)) | (hi_bits & jnp.uint32(0xFFFF0000))


def _unpack_bf16_pairs(words):
    lo = pltpu.bitcast(lax.shift_left(words, jnp.uint32(16)), F32)
    hi = pltpu.bitcast(words & jnp.uint32(0xFFFF0000), F32)
    return lo, hi


def _pack_rows(v):
    q = D_MODEL // 4
    return _pack_bf16_pairs(v[:, 0:q], v[:, q:2 * q]), _pack_bf16_pairs(v[:, 2 * q:3 * q], v[:, 3 * q:])


def _unpack_rows(a, b):
    return jnp.concatenate([*_unpack_bf16_pairs(a), *_unpack_bf16_pairs(b)], axis=-1)


def _route_kernel(x_ref, g_ref, wr_ref, br_ref, tri_ref, ha_ref, hb_ref, route_ref, cnt_ref):
    lane = _lane_iota()
    h = _rms(x_ref[...]) * g_ref[...]
    ha_ref[...], hb_ref[...] = _pack_rows(h)
    h_hi = h.astype(BF16)
    h_lo = (h - h_hi.astype(F32)).astype(BF16)
    logits = (jnp.dot(h_hi, wr_ref[0], preferred_element_type=F32)
              + jnp.dot(h_lo, wr_ref[0], preferred_element_type=F32)
              + jnp.dot(h_hi, wr_ref[1], preferred_element_type=F32)) + br_ref[...]
    logits = jnp.where(lane < N_EXPERTS, logits, -jnp.inf)
    v1 = jnp.max(logits, axis=-1, keepdims=True)
    i1 = jnp.min(jnp.where(logits == v1, lane, LANES), axis=-1, keepdims=True)
    rest = jnp.where(lane == i1, -jnp.inf, logits)
    v2 = jnp.max(rest, axis=-1, keepdims=True)
    i2 = jnp.min(jnp.where(rest == v2, lane, LANES), axis=-1, keepdims=True)
    e2 = jnp.exp(v2 - v1)
    g1 = 1.0 / (1.0 + e2)
    hit1, hit2 = lane == i1, lane == i2
    ones = jnp.where(hit1 | hit2, 1.0, 0.0)
    before = jnp.dot(tri_ref[...], ones.astype(BF16), preferred_element_type=F32)
    r1 = jnp.sum(jnp.where(hit1, before, 0.0), axis=-1, keepdims=True)
    r2 = jnp.sum(jnp.where(hit2, before, 0.0), axis=-1, keepdims=True)
    cols = [i1.astype(F32), i2.astype(F32), g1, e2 * g1, r1, r2]
    route = jnp.zeros(route_ref.shape, F32)
    for n, c in enumerate(cols):
        route = jnp.where(lane == n, c, route)
    route_ref[...] = route
    cnt_ref[0] = jnp.broadcast_to(jnp.sum(ones, axis=0, keepdims=True), cnt_ref.shape[1:])


def _route(x2d, g, wr, br):
    t = x2d.shape[0]
    tm = MOE_CHUNK
    tri = jnp.asarray(np.tril(np.ones((tm, tm), np.float32), -1)).astype(BF16)
    row = lambda w: pl.BlockSpec((tm, w), lambda i: (i, 0))
    q = D_MODEL // 4
    return pl.pallas_call(
        _route_kernel,
        grid=(t // tm,),
        in_specs=[row(D_MODEL), _full(g.shape), _full(wr.shape), _full(br.shape), _full(tri.shape)],
        out_specs=[row(q), row(q), row(LANES), pl.BlockSpec((1, 8, LANES), lambda i: (i, 0, 0))],
        out_shape=[jax.ShapeDtypeStruct((t, q), jnp.uint32), jax.ShapeDtypeStruct((t, q), jnp.uint32),
                   jax.ShapeDtypeStruct((t, LANES), F32), jax.ShapeDtypeStruct((t // tm, 8, LANES), F32)],
        compiler_params=_cparams("parallel"),
        name="moe_route",
    )(x2d, g, wr, br, tri)


def _dest_kernel(route_ref, base_ref, o_ref):
    lane = _lane_iota()
    r = route_ref[...]
    base = base_ref[0, 0:1, :]
    lane_f = lane.astype(F32)
    d1 = jnp.sum(jnp.where(lane_f == r[:, 0:1], base, 0.0), axis=-1, keepdims=True) + r[:, 4:5]
    d2 = jnp.sum(jnp.where(lane_f == r[:, 1:2], base, 0.0), axis=-1, keepdims=True) + r[:, 5:6]
    both = jnp.where(lane == 0, d1, jnp.where(lane == 1, d2, 0.0))
    o_ref[...] = both.T[0:8, :].astype(jnp.int32)


def _destinations(route, base):
    t = route.shape[0]
    tm = MOE_CHUNK
    out = pl.pallas_call(
        _dest_kernel,
        grid=(t // tm,),
        in_specs=[pl.BlockSpec((tm, LANES), lambda i: (i, 0)), pl.BlockSpec((1, 8, LANES), lambda i: (i, 0, 0))],
        out_specs=pl.BlockSpec((8, tm), lambda i: (0, i)),
        out_shape=jax.ShapeDtypeStruct((8, t), jnp.int32),
        compiler_params=_cparams("parallel"),
        name="moe_dest",
    )(route, base)
    return out[0:2].reshape(1, 2 * t)


SC_WINDOW = 128


def _sc_mesh():
    return plsc.VectorSubcoreMesh(core_axis_name="core", subcore_axis_name="subcore")


def _sc_scatter_rows(x, idx, n_out):
    n, d = x.shape
    m = idx.shape[1]
    nblk = n // SC_WINDOW

    @pl.kernel(out_type=jax.ShapeDtypeStruct((n_out, d), x.dtype), mesh=_sc_mesh(), name="moe_sc_scatter")
    def scatter(x_hbm, i_hbm, o_hbm):
        def body(x_vmem, i_vmem):
            pltpu.sync_copy(x_vmem, o_hbm.at[i_vmem.at[0]])

        half = m // SC_WINDOW // 2
        pltpu.emit_pipeline(
            body,
            grid=(2, half),
            in_specs=[pl.BlockSpec((SC_WINDOW, d), lambda c, j: ((c * half + j) % nblk, 0)),
                      pl.BlockSpec((1, SC_WINDOW), lambda c, j: (0, c * half + j))],
            out_specs=[],
            core_axis_name=("core", "subcore"),
            dimension_semantics=(pltpu.PARALLEL, pltpu.PARALLEL),
        )(x_hbm, i_hbm)

    return scatter(x, idx)


def _sc_gather_rows(table, idx):
    d = table.shape[1]
    m = idx.shape[1]

    @pl.kernel(out_type=jax.ShapeDtypeStruct((m, d), table.dtype), mesh=_sc_mesh(), name="moe_sc_gather")
    def gather(t_hbm, i_hbm, o_hbm):
        def body(i_vmem, o_vmem):
            pltpu.sync_copy(t_hbm.at[i_vmem.at[0]], o_vmem)

        half = m // SC_WINDOW // 2
        pltpu.emit_pipeline(
            body,
            grid=(2, half),
            in_specs=[pl.BlockSpec((1, SC_WINDOW), lambda c, j: (0, c * half + j))],
            out_specs=[pl.BlockSpec((SC_WINDOW, d), lambda c, j: (c * half + j, 0))],
            core_axis_name=("core", "subcore"),
            dimension_semantics=(pltpu.PARALLEL, pltpu.PARALLEL),
        )(i_hbm, o_hbm)

    return gather(table, idx)


def _expert_kernel(blk_expert_ref, n_used_ref, xa_ref, xb_ref, wg_ref, wu_ref, wd_ref, ya_ref, yb_ref,
                   wg16_ref, wu16_ref, wd16_ref):
    b = pl.program_id(0)

    @pl.when((b == 0) | (blk_expert_ref[b] != blk_expert_ref[jnp.maximum(b - 1, 0)]))
    def _():
        wg16_ref[...] = wg_ref[0].astype(BF16)
        wu16_ref[...] = wu_ref[0].astype(BF16)
        wd16_ref[...] = wd_ref[0].astype(BF16)

    @pl.when(b < n_used_ref[0])
    def _():
        xe = _unpack_rows(xa_ref[...], xb_ref[...]).astype(BF16)
        act = _silu(jnp.dot(xe, wg16_ref[...], preferred_element_type=F32)) * \
            jnp.dot(xe, wu16_ref[...], preferred_element_type=F32)
        y = jnp.dot(act.astype(BF16), wd16_ref[...], preferred_element_type=F32)
        ya_ref[...], yb_ref[...] = _pack_rows(y)

    @pl.when(pl.program_id(0) >= n_used_ref[0])
    def _():
        ya_ref[...] = jnp.zeros(ya_ref.shape, ya_ref.dtype)
        yb_ref[...] = jnp.zeros(yb_ref.shape, yb_ref.dtype)


def _experts(blk_expert, n_used, xa, xb, wg, wu, wd):
    n_rows, q = xa.shape
    ff = wg.shape[2]
    blk = MOE_BLOCK
    row = pl.BlockSpec((blk, q), lambda b, be, nu: (b, 0))
    grid_spec = pltpu.PrefetchScalarGridSpec(
        num_scalar_prefetch=2,
        grid=(n_rows // blk,),
        in_specs=[row, row,
                  pl.BlockSpec((1, D_MODEL, ff), lambda b, be, nu: (be[b], 0, 0)),
                  pl.BlockSpec((1, D_MODEL, ff), lambda b, be, nu: (be[b], 0, 0)),
                  pl.BlockSpec((1, ff, D_MODEL), lambda b, be, nu: (be[b], 0, 0))],
        out_specs=[row, row],
        scratch_shapes=[pltpu.VMEM((D_MODEL, ff), BF16), pltpu.VMEM((D_MODEL, ff), BF16),
                        pltpu.VMEM((ff, D_MODEL), BF16)])
    return pl.pallas_call(
        _expert_kernel,
        grid_spec=grid_spec,
        out_shape=[jax.ShapeDtypeStruct((n_rows, q), jnp.uint32)] * 2,
        compiler_params=_cparams("arbitrary"),
        name="moe_experts",
    )(blk_expert, n_used, xa, xb, wg, wu, wd)


def _combine_kernel(x_ref, route_ref, a1_ref, b1_ref, a2_ref, b2_ref, o_ref):
    g1 = route_ref[:, 2:3]
    g2 = route_ref[:, 3:4]
    o_ref[...] = x_ref[...] + g1 * _unpack_rows(a1_ref[...], b1_ref[...]) \
        + g2 * _unpack_rows(a2_ref[...], b2_ref[...])


def _combine(x2d, route, ya, yb):
    t = x2d.shape[0]
    tm = ROW_TILE
    nt = t // tm
    q = ya.shape[1]
    row = lambda w: pl.BlockSpec((tm, w), lambda i: (i, 0))
    first = pl.BlockSpec((tm, q), lambda i: (i, 0))
    second = pl.BlockSpec((tm, q), lambda i: (nt + i, 0))
    return pl.pallas_call(
        _combine_kernel,
        grid=(nt,),
        in_specs=[row(D_MODEL), row(LANES), first, first, second, second],
        out_specs=row(D_MODEL),
        out_shape=jax.ShapeDtypeStruct((t, D_MODEL), F32),
        compiler_params=_cparams("parallel"),
        name="moe_combine",
    )(x2d, route, ya, yb, ya, yb)


def _moe_sorted(x2d, g, wr, br, wg, wu, wd):
    t = x2d.shape[0]
    blk = MOE_BLOCK
    n_blocks = 2 * t // blk + N_EXPERTS
    ha, hb, route, cnt = _route(x2d, g, wr, br)

    cnt = cnt[:, 0, :N_EXPERTS].astype(jnp.int32)
    before_chunk = jnp.cumsum(cnt, axis=0) - cnt
    seg_blocks = (jnp.sum(cnt, axis=0) + blk - 1) // blk
    seg_end_blk = jnp.cumsum(seg_blocks)
    seg_start = (seg_end_blk - seg_blocks) * blk
    base = jnp.pad((seg_start[None, :] + before_chunk).astype(F32), ((0, 0), (0, LANES - N_EXPERTS)))
    idx = _destinations(route, jnp.broadcast_to(base[:, None, :], (base.shape[0], 8, LANES)))
    blk_expert = jnp.minimum(jnp.searchsorted(seg_end_blk, jnp.arange(n_blocks, dtype=jnp.int32), side="right"),
                             N_EXPERTS - 1).astype(jnp.int32)
    n_used = seg_end_blk[-1:].astype(jnp.int32)

    xa = _sc_scatter_rows(ha, idx, n_blocks * blk)
    xb = _sc_scatter_rows(hb, idx, n_blocks * blk)
    ya, yb = _experts(blk_expert, n_used, xa, xb, wg, wu, wd)
    return _combine(x2d, route, _sc_gather_rows(ya, idx), _sc_gather_rows(yb, idx))


def _pad_cols(w, width):
    return jnp.pad(w, ((0, 0), (0, width - w.shape[1])))


def _layer_params(l, mix_norm, w_in, b_forget, a_q_norm, a_k_norm, b_v_norm, b_spatial_w, b_spatial_b,
                  c_q_lat_norm, c_w_uq, c_kv_lat_norm, c_w_ukv, c_q_nope_norm, c_q_rope_norm,
                  c_k_nope_norm, c_k_rope_norm, out_norm_a, out_norm_b, out_norm_c, w_out,
                  xattn_norm, mem_norm, w_mem_q, w_mem_kv, m_q_norm, m_k_norm, w_mem_out):
    p = {}
    o = np.cumsum((0, A_W, A_W, A_W, A_HEADS, B_W, B_W, C_Q_RANK, C_KV_RANK, C_ROPE_DIM))
    w = w_in[l]
    seg = lambda n: w[:, o[n]:o[n + 1]]
    fa = seg(3)
    misc = jnp.zeros((D_MODEL, LANES), F32)
    misc = misc.at[:, ROPE_LANE:ROPE_LANE + C_ROPE_DIM].set(seg(8))
    fb = jnp.zeros((1, LANES), F32)
    for hd in range(A_HEADS):
        ln = FORGET_LANE + 8 * (hd // 2) + hd % 2
        misc = misc.at[:, ln].set(fa[:, hd])
        fb = fb.at[0, ln].set(b_forget[l, hd])
    p["w_in"] = jnp.concatenate([seg(0), seg(1), seg(2), seg(4), seg(5), seg(6), seg(7), misc], axis=1).astype(BF16)
    p["fb"] = fb
    p["mix_g"] = mix_norm[l][None]
    p["aq"] = jnp.tile(a_q_norm[l], A_HEADS)[None] * (A_HEAD_DIM ** -0.5 * LOG2E)
    p["ak"] = jnp.tile(a_k_norm[l], A_HEADS)[None]
    p["bvg"] = b_v_norm[l][None]
    pos = np.arange(B_WINDOW)
    mask = (pos[None, :] // CHUNK) <= (pos[:, None] // CHUNK)
    p["ws"] = jnp.where(mask[None], b_spatial_w[l], 0.0).reshape(B_GROUPS * B_WINDOW, B_WINDOW).astype(BF16)
    p["bs"] = jnp.repeat(b_spatial_b[l].T, B_GROUP_DIM, axis=1)
    p["onb"] = out_norm_b[l][None]
    p["cqg"] = c_q_lat_norm[l][None]
    p["ckvg"] = c_kv_lat_norm[l][None]
    gidx = np.arange(A_W) // A_HEAD_DIM
    p["gm"] = jnp.asarray((gidx[:, None] == gidx[None, :]).astype(np.float32) / A_HEAD_DIM).astype(BF16)
    p["tri"] = jnp.asarray(np.tril(np.ones((ROW_TILE, ROW_TILE), np.float32))).astype(BF16)

    qd = C_NOPE_DIM + C_ROPE_DIM
    half = C_ROPE_DIM // 2
    wq = c_w_uq[l]
    wq_partner = jnp.concatenate([jnp.zeros_like(wq[:, :, :C_NOPE_DIM]), wq[:, :, C_NOPE_DIM + half:],
                                  wq[:, :, C_NOPE_DIM:C_NOPE_DIM + half]], axis=-1)
    p["wuq"] = jnp.pad(jnp.stack([wq, wq_partner]), ((0, 0), (0, 0), (0, 0), (0, HEAD_SLAB - qd))
                       ).reshape(2, C_Q_RANK, -1).astype(BF16)
    wukv = c_w_ukv[l]
    p["wuk"] = jnp.pad(wukv[:, :, :C_NOPE_DIM], ((0, 0), (0, 0), (0, HEAD_SLAB - C_NOPE_DIM))
                       ).reshape(C_KV_RANK, -1).astype(BF16)
    p["wuv"] = wukv[:, :, C_NOPE_DIM:].reshape(C_KV_RANK, C_W).astype(BF16)
    gq = jnp.concatenate([c_q_nope_norm[l], c_q_rope_norm[l]])
    gq_partner = jnp.concatenate([jnp.zeros_like(c_q_nope_norm[l]), c_q_rope_norm[l][half:],
                                  c_q_rope_norm[l][:half]])
    p["gq"] = _pad_cols(jnp.stack([gq, gq_partner]) * (qd ** -0.5 * LOG2E), LANES)
    p["gkn"] = _pad_cols(c_k_nope_norm[l][None], LANES)
    p["gkr"] = jnp.zeros((1, LANES), F32).at[0, ROPE_LANE:ROPE_LANE + C_ROPE_DIM].set(c_k_rope_norm[l])

    p["ona"] = out_norm_a[l][None]
    p["onc"] = out_norm_c[l][None]
    p["w_out"] = w_out[l].astype(BF16)
    p["xg"] = xattn_norm[l][None]
    p["w_mem_q"] = w_mem_q[l].astype(BF16)
    p["mqg"] = m_q_norm[l][None] * (M_HEAD_DIM ** -0.5 * LOG2E)
    p["mem_g"] = mem_norm[l][None]
    p["w_mem_kv"] = w_mem_kv[l].astype(BF16)
    p["mkg"] = m_k_norm[l][None]
    p["w_mem_out"] = w_mem_out[l].astype(BF16)
    return p


def kernel(x, mem, positions, mix_norm, w_in, b_forget, a_q_norm, a_k_norm, b_v_norm, b_spatial_w, b_spatial_b, c_q_lat_norm, c_w_uq, c_kv_lat_norm, c_w_ukv, c_q_nope_norm, c_q_rope_norm, c_k_nope_norm, c_k_rope_norm, out_norm_a, out_norm_b, out_norm_c, w_out, xattn_norm, mem_norm, w_mem_q, w_mem_kv, m_q_norm, m_k_norm, w_mem_out, ffn_norm, ffn_w_gate, ffn_w_up, ffn_w_down, w_router, b_router, moe_w_gate, moe_w_up, moe_w_down):
    nb, seq, d = x.shape
    assert d == D_MODEL and seq % ROW_TILE == 0 and seq % ATTN_TILE == 0 and (nb * seq) % MOE_CHUNK == 0
    depth = w_in.shape[0]
    t = nb * seq
    x2d = x.reshape(t, d)
    cos, sin = _rope_tables(positions.reshape(t, 1).astype(F32))

    for l in range(depth):
        p = _layer_params(l, mix_norm, w_in, b_forget, a_q_norm, a_k_norm, b_v_norm, b_spatial_w,
                          b_spatial_b, c_q_lat_norm, c_w_uq, c_kv_lat_norm, c_w_ukv, c_q_nope_norm,
                          c_q_rope_norm, c_k_nope_norm, c_k_rope_norm, out_norm_a, out_norm_b,
                          out_norm_c, w_out, xattn_norm, mem_norm, w_mem_q, w_mem_kv, m_q_norm,
                          m_k_norm, w_mem_out)
        qa, ka, vta, bn, cq, ckv, misc = _inproj(x2d, seq, p)
        qc, kc, vtc = _mla_prep(cq, ckv, misc, cos, sin, seq, p)
        a = _attention(qa, ka, vta, unit=1, name="attn_fox")
        c = _attention(qc, kc, vtc, unit=CHUNK, name="attn_mla")
        km, vm = _mem_kv(mem, p)
        x2d = _outproj(x2d, a, bn, c, km, vm, seq, p)
        g = ffn_norm[l][None]
        if l % 2 == 0:
            m = l // 2
            ff = ffn_w_gate.shape[2]
            ff_pad = -(-ff // (2 * LANES)) * (2 * LANES)
            wg = _pad_cols(ffn_w_gate[m], ff_pad).astype(BF16)
            wu = _pad_cols(ffn_w_up[m], ff_pad).astype(BF16)
            wd = jnp.pad(ffn_w_down[m], ((0, ff_pad - ff), (0, 0))).astype(BF16)
            x2d = _ffn(x2d, g, wg, wu, wd)
        else:
            m = l // 2
            wr = _pad_cols(w_router[m], LANES)
            wr_hi = wr.astype(BF16)
            wr = jnp.stack([wr_hi, (wr - wr_hi.astype(F32)).astype(BF16)])
            br = _pad_cols(b_router[m][None], LANES)
            x2d = _moe_sorted(x2d, g, wr, br, moe_w_gate[m], moe_w_up[m], moe_w_down[m])
    return x2d.reshape(nb, seq, d)
```

```python
import functools

import numpy as np
import jax
import jax.numpy as jnp
from jax import lax
from jax.experimental import pallas as pl
from jax.experimental.pallas import tpu as pltpu
from jax.experimental.pallas import tpu_sc as plsc

F32 = jnp.float32
BF16 = jnp.bfloat16
HIGHEST = lax.Precision.HIGHEST

D_MODEL = 1024
CHUNK = 64
EPS = 1e-6
NEG_INF = -1e30
A_HEADS, A_HEAD_DIM = 4, 64
B_GROUPS, B_GROUP_DIM, B_WINDOW = 4, 64, 128
C_HEADS, C_NOPE_DIM, C_ROPE_DIM, C_V_DIM = 8, 64, 32, 64
C_Q_RANK, C_KV_RANK = 256, 128
ROPE_THETA = 10000.0
M_HEADS, M_HEAD_DIM = 4, 128
N_EXPERTS = 8
A_W = A_HEADS * A_HEAD_DIM
B_W = B_GROUPS * B_GROUP_DIM
C_W = C_HEADS * C_V_DIM
M_W = M_HEADS * M_HEAD_DIM

LANES = 128

SEG_Q, SEG_K, SEG_V, SEG_U, SEG_VB, SEG_CQ, SEG_CKV, SEG_MISC = 0, 256, 512, 768, 1024, 1280, 1536, 1664
IN_PAD_W = SEG_MISC + LANES
ROPE_LANE = C_NOPE_DIM
FORGET_LANE = 96
HEAD_SLAB = LANES
VT_ROWS = 80
LOG2E = float(np.log2(np.e))

ROW_TILE = 512
ATTN_TILE = 512
ATTN_UNROLL = 2
MOE_CHUNK = 1024
MOE_BLOCK = 512
VMEM_LIMIT = 56 * 1024 * 1024


def _cparams(*sem):
    return pltpu.CompilerParams(dimension_semantics=sem, vmem_limit_bytes=VMEM_LIMIT)


def _full(shape):
    n = len(shape)
    return pl.BlockSpec(shape, lambda *_: (0,) * n)


def _rms(x):
    return x * lax.rsqrt(jnp.mean(x * x, axis=-1, keepdims=True) + EPS)


def _dot_split(v, exact, pieces, lhs_is_exact=False):
    total = None
    rem = v
    for n in range(pieces):
        part = rem.astype(BF16)
        if n + 1 < pieces:
            rem = rem - part.astype(F32)
        term = (jnp.dot(exact, part, preferred_element_type=F32) if lhs_is_exact
                else jnp.dot(part, exact, preferred_element_type=F32))
        total = term if total is None else total + term
    return total


def _lane_iota(n=LANES):
    return lax.broadcasted_iota(jnp.int32, (1, n), 1)


def _rope_table_kernel(pos_ref, inv_ref, sgn_ref, cos_ref, sin_ref):
    ang = pos_ref[...] * inv_ref[...]
    cos_ref[...] = jnp.cos(ang)
    sin_ref[...] = jnp.sin(ang) * sgn_ref[...]


def _rope_tables(pos_col):
    t = pos_col.shape[0]
    half = C_ROPE_DIM // 2
    inv = ROPE_THETA ** (-jnp.arange(half, dtype=F32) / half)
    inv_l = jnp.zeros((1, LANES), F32).at[0, ROPE_LANE:ROPE_LANE + C_ROPE_DIM].set(jnp.tile(inv, 2))
    sgn = np.zeros((1, LANES), np.float32)
    sgn[0, ROPE_LANE:ROPE_LANE + half] = -1.0
    sgn[0, ROPE_LANE + half:ROPE_LANE + C_ROPE_DIM] = 1.0
    tm = ROW_TILE
    return pl.pallas_call(
        _rope_table_kernel,
        grid=(t // tm,),
        in_specs=[pl.BlockSpec((tm, 1), lambda i: (i, 0)), _full((1, LANES)), _full((1, LANES))],
        out_specs=[pl.BlockSpec((tm, LANES), lambda i: (i, 0))] * 2,
        out_shape=[jax.ShapeDtypeStruct((t, LANES), F32)] * 2,
        compiler_params=_cparams("parallel"),
        name="rope_tables",
    )(pos_col, inv_l, jnp.asarray(sgn))


def _rotate(x, cos, sin_signed, lane):
    half = C_ROPE_DIM // 2
    partner = jnp.where(lane < ROPE_LANE + half,
                        pltpu.roll(x, LANES - half, 1), pltpu.roll(x, half, 1))
    return x * cos + partner * sin_signed


def _store_v_transposed(vt_ref, v, n_heads, lanes=slice(None)):
    tm = v.shape[0]
    v_t = v.T
    tail = jnp.where(lax.broadcasted_iota(jnp.int32, (VT_ROWS - C_V_DIM, tm), 0) == 0, 1.0, 0.0).astype(BF16)
    for hd in range(n_heads):
        vt_ref[0, hd, 0:C_V_DIM, lanes] = v_t[hd * C_V_DIM:(hd + 1) * C_V_DIM, :].astype(BF16)
        vt_ref[0, hd, C_V_DIM:VT_ROWS, lanes] = tail


def _gelu(x):
    return 0.5 * x * (1.0 + lax.erf(x * np.float32(1.0 / np.sqrt(2.0))))


def _inproj_kernel(x0_ref, xa_ref, xb_ref, g_ref, w_ref, aq_ref, ak_ref, fb_ref, bvg_ref, ws_ref, bs_ref,
                   onb_ref, cqg_ref, ckvg_ref, gm_ref, tri_ref,
                   qa_ref, ka_ref, vt_ref, bn_ref, cq_ref, ckv_ref, misc_ref,
                   carry_ref, buf0_ref, buf1_ref, *, tiles_per_seq):
    i = pl.program_id(0)
    tm = xa_ref.shape[0]

    def projection_parts(x_ref, buf_ref):
        h = (_rms(x_ref[...]) * g_ref[...]).astype(BF16)

        def part(lo, hi):
            def run():
                buf_ref[:, lo:hi] = jnp.dot(h, w_ref[:, lo:hi], preferred_element_type=F32)
            return run
        return [part(SEG_Q, SEG_U), part(SEG_U, SEG_CQ), part(SEG_CQ, IN_PAD_W)]

    @pl.when(i == 0)
    def _():
        for run in projection_parts(x0_ref, buf0_ref):
            run()

    refs = (aq_ref, ak_ref, fb_ref, bvg_ref, ws_ref, bs_ref, onb_ref, cqg_ref, ckvg_ref, gm_ref, tri_ref,
            qa_ref, ka_ref, vt_ref, bn_ref, cq_ref, ckv_ref, misc_ref, carry_ref)
    _mixer_prologues(buf0_ref, 0, 2 * i, tm, tiles_per_seq, projection_parts(xa_ref, buf1_ref), *refs)
    _mixer_prologues(buf1_ref, 1, 2 * i + 1, tm, tiles_per_seq, projection_parts(xb_ref, buf0_ref), *refs)


def _mixer_prologues(proj, half, tile, tm, tiles_per_seq, between, aq_ref, ak_ref, fb_ref, bvg_ref, ws_ref,
                     bs_ref, onb_ref, cqg_ref, ckvg_ref, gm_ref, tri_ref, qa_ref, ka_ref, vt_ref, bn_ref,
                     cq_ref, ckv_ref, misc_ref, carry_ref):
    out_rows = slice(half * tm, (half + 1) * tm)
    gm = gm_ref[...]

    def group_mean(v):
        return _dot_split(v, gm, 2)

    misc = proj[:, SEG_MISC:SEG_MISC + LANES]
    misc_ref[out_rows, :] = misc
    z = misc + fb_ref[...]
    log_f = jnp.minimum(z, 0.0) - jnp.log1p(jnp.exp(-jnp.abs(z)))
    carry = jnp.where(tile % tiles_per_seq == 0, 0.0, carry_ref[...])
    cum = _dot_split(log_f, tri_ref[...], 3, lhs_is_exact=True) + carry
    carry_ref[...] = cum[tm - 1:tm, :]
    f_hi = (cum * LOG2E).astype(BF16).astype(F32)
    f_rem = cum * LOG2E - f_hi
    f_mid = f_rem.astype(BF16).astype(F32)
    f_lo = f_rem - f_mid

    between[0]()
    q = proj[:, SEG_Q:SEG_Q + A_W]
    qn = q * lax.rsqrt(group_mean(q * q) + EPS) * aq_ref[...]
    k = proj[:, SEG_K:SEG_K + A_W]
    kn = k * lax.rsqrt(group_mean(k * k) + EPS) * ak_ref[...]
    lane = _lane_iota()
    for hd in range(A_HEADS):
        pair = slice((hd // 2) * LANES, (hd // 2 + 1) * LANES)
        slab = slice(hd * HEAD_SLAB, (hd + 1) * HEAD_SLAB)
        data = (lane < A_HEAD_DIM) if hd % 2 == 0 else (lane >= A_HEAD_DIM)
        e0 = A_HEAD_DIM if hd % 2 == 0 else 0
        fl = FORGET_LANE + 8 * (hd // 2) + hd % 2
        ones = jnp.where((lane >= e0) & (lane < e0 + 3), 1.0, 0.0)
        qa_ref[out_rows, slab] = jnp.where(data, qn[:, pair], ones).astype(BF16)
        bias = jnp.where(lane == e0, -f_hi[:, fl:fl + 1],
                         jnp.where(lane == e0 + 1, -f_mid[:, fl:fl + 1],
                                   jnp.where(lane == e0 + 2, -f_lo[:, fl:fl + 1], 0.0)))
        ka_ref[out_rows, slab] = jnp.where(data, kn[:, pair], bias).astype(BF16)
    _store_v_transposed(vt_ref, proj[:, SEG_V:SEG_V + A_W], A_HEADS, out_rows)

    between[1]()
    u = _gelu(proj[:, SEG_U:SEG_U + B_W])
    v = _gelu(proj[:, SEG_VB:SEG_VB + B_W])
    dv = v - group_mean(v)
    vn = dv * lax.rsqrt(group_mean(dv * dv) + EPS) * bvg_ref[...]
    group = lax.broadcasted_iota(jnp.int32, (1, B_W), 1) // B_GROUP_DIM
    for w in range(tm // B_WINDOW):
        rows = slice(w * B_WINDOW, (w + 1) * B_WINDOW)
        y_all = jnp.dot(ws_ref[...], vn[rows].astype(BF16), preferred_element_type=F32)
        y = bs_ref[...]
        for g in range(B_GROUPS):
            y = y + jnp.where(group == g, y_all[g * B_WINDOW:(g + 1) * B_WINDOW], 0.0)
        b = u[rows] * y
        bn_ref[half * tm + w * B_WINDOW:half * tm + (w + 1) * B_WINDOW, :] = (_rms(b) * onb_ref[...]).astype(BF16)

    between[2]()
    cq_ref[out_rows, :] = (_rms(proj[:, SEG_CQ:SEG_CQ + C_Q_RANK]) * cqg_ref[...]).astype(BF16)
    ckv_ref[out_rows, :] = (_rms(proj[:, SEG_CKV:SEG_CKV + C_KV_RANK]) * ckvg_ref[...]).astype(BF16)


def _inproj(x2d, seq, p):
    t = x2d.shape[0]
    tm = ROW_TILE
    tps = seq // tm
    nb = t // seq
    n_tiles = t // tm
    steps_per_seq = tps // 2
    row = lambda w: pl.BlockSpec((2 * tm, w), lambda i: (i, 0))
    x_tile = lambda index: pl.BlockSpec((tm, D_MODEL), lambda i: (index(i), 0))
    qk_w = A_HEADS * HEAD_SLAB
    out_shape = [
        jax.ShapeDtypeStruct((t, qk_w), BF16), jax.ShapeDtypeStruct((t, qk_w), BF16),
        jax.ShapeDtypeStruct((nb, A_HEADS, VT_ROWS, seq), BF16),
        jax.ShapeDtypeStruct((t, B_W), BF16),
        jax.ShapeDtypeStruct((t, C_Q_RANK), BF16), jax.ShapeDtypeStruct((t, C_KV_RANK), BF16),
        jax.ShapeDtypeStruct((t, LANES), F32),
    ]
    out_specs = [row(qk_w), row(qk_w),
                 pl.BlockSpec((1, A_HEADS, VT_ROWS, 2 * tm),
                              lambda i: (i // steps_per_seq, 0, 0, i % steps_per_seq)),
                 row(B_W), row(C_Q_RANK), row(C_KV_RANK), row(LANES)]
    consts = [p["mix_g"], p["w_in"], p["aq"], p["ak"], p["fb"], p["bvg"], p["ws"], p["bs"], p["onb"],
              p["cqg"], p["ckvg"], p["gm"], p["tri"]]
    return pl.pallas_call(
        functools.partial(_inproj_kernel, tiles_per_seq=tps),
        grid=(n_tiles // 2,),
        in_specs=[x_tile(lambda i: 0), x_tile(lambda i: 2 * i + 1),
                  x_tile(lambda i: jnp.minimum(2 * i + 2, n_tiles - 1))] + [_full(c.shape) for c in consts],
        out_specs=out_specs,
        out_shape=out_shape,
        scratch_shapes=[pltpu.VMEM((1, LANES), F32), pltpu.VMEM((tm, IN_PAD_W), F32),
                        pltpu.VMEM((tm, IN_PAD_W), F32)],
        compiler_params=_cparams("arbitrary"),
        name="in_proj",
    )(x2d, x2d, x2d, *consts)


def _mla_prep_kernel(cq_ref, ckv_ref, misc_ref, cos_ref, sin_ref, wuq_ref, wuk_ref, wuv_ref,
                     gq_ref, gkn_ref, gkr_ref, qc_ref, kc_ref, vt_ref):
    lane = _lane_iota()
    nope = lane < C_NOPE_DIM
    rope = (lane >= ROPE_LANE) & (lane < ROPE_LANE + C_ROPE_DIM)
    cos, sin = cos_ref[...], sin_ref[...]
    q = jnp.dot(cq_ref[...], wuq_ref[0], preferred_element_type=F32)
    q_partner = jnp.dot(cq_ref[...], wuq_ref[1], preferred_element_type=F32)
    q_cos = gq_ref[0:1, :] * cos
    q_sin = gq_ref[1:2, :] * sin
    kn = jnp.dot(ckv_ref[...], wuk_ref[...], preferred_element_type=F32)
    _store_v_transposed(vt_ref, jnp.dot(ckv_ref[...], wuv_ref[...], preferred_element_type=F32), C_HEADS)

    kr = jnp.where(rope, misc_ref[...], 0.0)
    kr = kr * lax.rsqrt(jnp.sum(kr * kr, axis=-1, keepdims=True) * (1.0 / C_ROPE_DIM) + EPS) * gkr_ref[...]
    kr = _rotate(kr, cos, sin, lane)

    for hd in range(C_HEADS):
        cols = slice(hd * HEAD_SLAB, (hd + 1) * HEAD_SLAB)
        qh = q[:, cols]
        sq = qh * qh
        r_n = lax.rsqrt(jnp.sum(jnp.where(nope, sq, 0.0), axis=-1, keepdims=True) * (1.0 / C_NOPE_DIM) + EPS)
        r_r = lax.rsqrt(jnp.sum(jnp.where(rope, sq, 0.0), axis=-1, keepdims=True) * (1.0 / C_ROPE_DIM) + EPS)
        qc_ref[:, cols] = (jnp.where(nope, r_n, r_r)
                           * (qh * q_cos + q_partner[:, cols] * q_sin)).astype(BF16)
        kh = kn[:, cols]
        r_k = lax.rsqrt(jnp.sum(kh * kh, axis=-1, keepdims=True) * (1.0 / C_NOPE_DIM) + EPS)
        kc_ref[:, cols] = (kh * r_k * gkn_ref[...] + kr).astype(BF16)


def _mla_prep(cq, ckv, misc, cos, sin, seq, p):
    t = cq.shape[0]
    tm = ROW_TILE
    tps = seq // tm
    row = lambda w: pl.BlockSpec((tm, w), lambda i: (i, 0))
    consts = [p["wuq"], p["wuk"], p["wuv"], p["gq"], p["gkn"], p["gkr"]]
    qk_w = C_HEADS * HEAD_SLAB
    return pl.pallas_call(
        _mla_prep_kernel,
        grid=(t // tm,),
        in_specs=[row(C_Q_RANK), row(C_KV_RANK), row(LANES), row(LANES), row(LANES)]
                 + [_full(c.shape) for c in consts],
        out_specs=[row(qk_w), row(qk_w),
                   pl.BlockSpec((1, C_HEADS, VT_ROWS, tm), lambda i: (i // tps, 0, 0, i % tps))],
        out_shape=[jax.ShapeDtypeStruct((t, qk_w), BF16), jax.ShapeDtypeStruct((t, qk_w), BF16),
                   jax.ShapeDtypeStruct((t // seq, C_HEADS, VT_ROWS, seq), BF16)],
        compiler_params=_cparams("parallel"),
        name="mla_prep",
    )(cq, ckv, misc, cos, sin, *consts)


def _attn_items(nq):
    items = [(i, j) for i in range(nq) for j in range(i)] + [(i, i) for i in range(nq)]
    return np.array(items, np.int32).T


def _attn_kernel(items_ref, q_ref, k_ref, vt_ref, mask_ref, o_ref, s0_ref, s1_ref, p0_ref, p1_ref,
                 mp0_ref, mp1_ref, mrun_ref, macc_ref, acc_ref, *, n_items):
    tk, tq = mask_ref.shape
    nq = q_ref.shape[0] // tq
    s_bufs, p_bufs, mp_bufs = (s0_ref, s1_ref), (p0_ref, p1_ref), (mp0_ref, mp1_ref)
    mrun_ref[...] = jnp.full(mrun_ref.shape, NEG_INF, F32)
    macc_ref[...] = jnp.full(macc_ref.shape, NEG_INF, F32)
    acc_ref[...] = jnp.zeros(acc_ref.shape, F32)

    def scores(it, buf, masked):
        qi = items_ref[0, it]
        q_start = pl.multiple_of(qi * tq, tq)
        k_start = pl.multiple_of(items_ref[1, it] * tk, tk)
        for hh in range(2):
            cols = slice(hh * HEAD_SLAB, (hh + 1) * HEAD_SLAB)
            s_t = lax.dot_general(k_ref[pl.ds(k_start, tk), cols], q_ref[pl.ds(q_start, tq), cols],
                                  (((1,), (1,)), ((), ())), preferred_element_type=F32)
            if masked:
                s_t = s_t + mask_ref[...]
            s_bufs[buf][hh] = s_t
            mrun_ref[qi, hh] = jnp.maximum(mrun_ref[qi, hh], jnp.max(s_t, axis=0, keepdims=True))

    def exponentiate(it, buf):
        qi = items_ref[0, it]
        for hh in range(2):
            m = mrun_ref[qi, hh]
            p_bufs[buf][hh] = jnp.exp2(s_bufs[buf][hh] - m).astype(BF16)
            mp_bufs[buf][hh] = m

    def accumulate(it, buf):
        qi = items_ref[0, it]
        k_start = pl.multiple_of(items_ref[1, it] * tk, tk)
        for hh in range(2):
            m = mp_bufs[buf][hh]
            pv = jnp.dot(vt_ref[0, hh, :, pl.ds(k_start, tk)], p_bufs[buf][hh],
                         preferred_element_type=F32)
            acc_ref[qi, hh] = jnp.exp2(macc_ref[qi, hh] - m) * acc_ref[qi, hh] + pv
            macc_ref[qi, hh] = m

    def beat(it, par, masked):
        if not isinstance(it, int) or 2 <= it <= n_items + 1:
            accumulate(it - 2, par)
        if not isinstance(it, int) or 1 <= it <= n_items:
            exponentiate(it - 1, 1 - par)
        if not isinstance(it, int) or it < n_items:
            scores(it, par, masked)

    def run_beats(lo, hi, masked):
        lo = max(lo, 0)
        if lo < hi and (lo % 2 == 1 or lo < 2):
            for it in range(lo, min(hi, lo + 2 - lo % 2)):
                beat(it, it % 2, masked)
            lo = min(hi, lo + 2 - lo % 2)
        n_groups = (hi - lo) // ATTN_UNROLL
        if n_groups > 0:
            def group(t, carry, lo=lo):
                for u in range(ATTN_UNROLL):
                    beat(lo + ATTN_UNROLL * t + u, u % 2, masked)
                return carry
            lax.fori_loop(0, n_groups, group, 0)
        for it in range(lo + ATTN_UNROLL * n_groups, hi):
            beat(it, it % 2, masked)

    n_full = n_items - nq
    run_beats(0, n_full, False)
    run_beats(n_full, n_items, True)
    for it in (n_items, n_items + 1):
        beat(it, it % 2, True)

    for qi in range(nq):
        halves = [acc_ref[qi, hh, 0:C_V_DIM, :] / acc_ref[qi, hh, C_V_DIM:C_V_DIM + 1, :] for hh in range(2)]
        o_ref[qi * tq:(qi + 1) * tq, :] = jnp.concatenate(halves, axis=0).T.astype(o_ref.dtype)


def _attention(q, k, vt, *, unit, name):
    t = q.shape[0]
    nb, n_heads, _, seq = vt.shape
    tq = ATTN_TILE
    nq = seq // tq
    items = _attn_items(nq)
    pos = np.arange(tq)
    diag_mask = np.where((pos[:, None] // unit) <= (pos[None, :] // unit), 0.0, NEG_INF)
    mask = jnp.asarray(diag_mask.astype(np.float32))
    seq_blk = lambda w: pl.BlockSpec((seq, w), lambda b, p, items_ref: (b, p))
    grid_spec = pltpu.PrefetchScalarGridSpec(
        num_scalar_prefetch=1,
        grid=(nb, n_heads // 2),
        in_specs=[seq_blk(2 * HEAD_SLAB), seq_blk(2 * HEAD_SLAB),
                  pl.BlockSpec((1, 2, VT_ROWS, seq), lambda b, p, items_ref: (b, p, 0, 0)),
                  pl.BlockSpec((tq, tq), lambda b, p, items_ref: (0, 0), pipeline_mode=pl.Buffered(1))],
        out_specs=seq_blk(2 * C_V_DIM),
        scratch_shapes=[pltpu.VMEM((2, tq, tq), F32), pltpu.VMEM((2, tq, tq), F32),
                        pltpu.VMEM((2, tq, tq), BF16), pltpu.VMEM((2, tq, tq), BF16),
                        pltpu.VMEM((2, 1, tq), F32), pltpu.VMEM((2, 1, tq), F32),
                        pltpu.VMEM((nq, 2, 1, tq), F32), pltpu.VMEM((nq, 2, 1, tq), F32),
                        pltpu.VMEM((nq, 2, VT_ROWS, tq), F32)])
    return pl.pallas_call(
        functools.partial(_attn_kernel, n_items=items.shape[1]),
        grid_spec=grid_spec,
        out_shape=jax.ShapeDtypeStruct((t, n_heads * C_V_DIM), BF16),
        compiler_params=_cparams("parallel", "parallel"),
        name=name,
    )(jnp.asarray(items), q, k, vt, mask)


def _mem_kv_kernel(mem_ref, g_ref, w_ref, kg_ref, k_ref, v_ref):
    mn = (_rms(mem_ref[0]) * g_ref[...]).astype(BF16)
    kv = jnp.dot(mn, w_ref[...], preferred_element_type=F32)
    for hd in range(M_HEADS):
        cols = slice(hd * M_HEAD_DIM, (hd + 1) * M_HEAD_DIM)
        k_ref[0, :, cols] = (_rms(kv[:, cols]) * kg_ref[...]).astype(BF16)
    v_ref[0] = kv[:, M_W:].astype(BF16)


def _mem_kv(mem, p):
    nb, ml, _ = mem.shape
    consts = [p["mem_g"], p["w_mem_kv"], p["mkg"]]
    blk = pl.BlockSpec((1, ml, M_W), lambda b: (b, 0, 0))
    return pl.pallas_call(
        _mem_kv_kernel,
        grid=(nb,),
        in_specs=[pl.BlockSpec((1, ml, D_MODEL), lambda b: (b, 0, 0))] + [_full(c.shape) for c in consts],
        out_specs=[blk, blk],
        out_shape=[jax.ShapeDtypeStruct((nb, ml, M_W), BF16)] * 2,
        compiler_params=_cparams("parallel"),
        name="mem_kv",
    )(mem, *consts)


def _outproj_kernel(x_ref, a_ref, bn_ref, c_ref, ona_ref, onc_ref, wo_ref, xg_ref, wq_ref, mqg_ref,
                    km_ref, vm_ref, wmo_ref, o_ref):
    a_n = (_rms(a_ref[...].astype(F32)) * ona_ref[...]).astype(BF16)
    c_n = (_rms(c_ref[...].astype(F32)) * onc_ref[...]).astype(BF16)
    mix = jnp.concatenate([a_n, bn_ref[...], c_n], axis=-1)
    x1 = x_ref[...] + jnp.dot(mix, wo_ref[...], preferred_element_type=F32)

    h = (_rms(x1) * xg_ref[...]).astype(BF16)
    q = jnp.dot(h, wq_ref[...], preferred_element_type=F32)
    outs = []
    for hd in range(M_HEADS):
        cols = slice(hd * M_HEAD_DIM, (hd + 1) * M_HEAD_DIM)
        qh = (_rms(q[:, cols]) * mqg_ref[...]).astype(BF16)
        s = lax.dot_general(qh, km_ref[0, :, cols], (((1,), (1,)), ((), ())), preferred_element_type=F32)
        e = jnp.exp(s - jnp.max(s, axis=-1, keepdims=True))
        pr = e / jnp.sum(e, axis=-1, keepdims=True)
        outs.append(jnp.dot(pr.astype(BF16), vm_ref[0, :, cols], preferred_element_type=F32).astype(BF16))
    o_ref[...] = x1 + jnp.dot(jnp.concatenate(outs, axis=-1), wmo_ref[...], preferred_element_type=F32)


def _outproj(x2d, a, bn, c, km, vm, seq, p):
    t = x2d.shape[0]
    tm = ROW_TILE
    tps = seq // tm
    ml = km.shape[1]
    row = lambda w: pl.BlockSpec((tm, w), lambda i: (i, 0))
    memblk = pl.BlockSpec((1, ml, M_W), lambda i: (i // tps, 0, 0))
    c1 = [p["ona"], p["onc"], p["w_out"], p["xg"], p["w_mem_q"], p["mqg"]]
    return pl.pallas_call(
        _outproj_kernel,
        grid=(t // tm,),
        in_specs=[row(D_MODEL), row(A_W), row(B_W), row(C_W)] + [_full(c_.shape) for c_ in c1]
                 + [memblk, memblk, _full(p["w_mem_out"].shape)],
        out_specs=row(D_MODEL),
        out_shape=jax.ShapeDtypeStruct((t, D_MODEL), F32),
        compiler_params=_cparams("parallel"),
        name="out_proj_mem_attn",
    )(x2d, a, bn, c, *c1, km, vm, p["w_mem_out"])


def _silu(x):
    return x * jax.nn.sigmoid(x)


def _ffn_kernel(x_ref, g_ref, wg_ref, wu_ref, wd_ref, o_ref, *, n_chunks):
    x = x_ref[...]
    h = (_rms(x) * g_ref[...]).astype(BF16)
    fc = wg_ref.shape[1] // n_chunks
    acc = x
    for c in range(n_chunks):
        cols = slice(c * fc, (c + 1) * fc)
        act = _silu(jnp.dot(h, wg_ref[:, cols], preferred_element_type=F32)) * \
            jnp.dot(h, wu_ref[:, cols], preferred_element_type=F32)
        acc = acc + jnp.dot(act.astype(BF16), wd_ref[cols, :], preferred_element_type=F32)
    o_ref[...] = acc


def _ffn(x2d, g, wg, wu, wd):
    t = x2d.shape[0]
    tm = ROW_TILE
    row = pl.BlockSpec((tm, D_MODEL), lambda i: (i, 0))
    resident = lambda a: pl.BlockSpec(a.shape, lambda i: (0, 0), pipeline_mode=pl.Buffered(1))
    return pl.pallas_call(
        functools.partial(_ffn_kernel, n_chunks=2),
        grid=(t // tm,),
        in_specs=[row, _full(g.shape), resident(wg), resident(wu), resident(wd)],
        out_specs=row,
        out_shape=jax.ShapeDtypeStruct((t, D_MODEL), F32),
        compiler_params=_cparams("parallel"),
        name="ffn_dense",
    )(x2d, g, wg, wu, wd)


def _pack_bf16_pairs(lo, hi):
    lo_bits = pltpu.bitcast(lo.astype(BF16).astype(F32), jnp.uint32)
    hi_bits = pltpu.bitcast(hi.astype(BF16).astype(F32), jnp.uint32)
    return lax.shift_right_logical(lo_bits, jnp.uint32(16)) | (hi_bits & jnp.uint32(0xFFFF0000))


def _unpack_bf16_pairs(words):
    lo = pltpu.bitcast(lax.shift_left(words, jnp.uint32(16)), F32)
    hi = pltpu.bitcast(words & jnp.uint32(0xFFFF0000), F32)
    return lo, hi


def _pack_rows(v):
    q = D_MODEL // 4
    return _pack_bf16_pairs(v[:, 0:q], v[:, q:2 * q]), _pack_bf16_pairs(v[:, 2 * q:3 * q], v[:, 3 * q:])


def _unpack_rows(a, b):
    return jnp.concatenate([*_unpack_bf16_pairs(a), *_unpack_bf16_pairs(b)], axis=-1)


def _route_kernel(x_ref, g_ref, wr_ref, br_ref, tri_ref, ha_ref, hb_ref, route_ref, cnt_ref):
    lane = _lane_iota()
    h = _rms(x_ref[...]) * g_ref[...]
    ha_ref[...], hb_ref[...] = _pack_rows(h)
    h_hi = h.astype(BF16)
    h_lo = (h - h_hi.astype(F32)).astype(BF16)
    logits = (jnp.dot(h_hi, wr_ref[0], preferred_element_type=F32)
              + jnp.dot(h_lo, wr_ref[0], preferred_element_type=F32)
              + jnp.dot(h_hi, wr_ref[1], preferred_element_type=F32)) + br_ref[...]
    logits = jnp.where(lane < N_EXPERTS, logits, -jnp.inf)
    v1 = jnp.max(logits, axis=-1, keepdims=True)
    i1 = jnp.min(jnp.where(logits == v1, lane, LANES), axis=-1, keepdims=True)
    rest = jnp.where(lane == i1, -jnp.inf, logits)
    v2 = jnp.max(rest, axis=-1, keepdims=True)
    i2 = jnp.min(jnp.where(rest == v2, lane, LANES), axis=-1, keepdims=True)
    e2 = jnp.exp(v2 - v1)
    g1 = 1.0 / (1.0 + e2)
    hit1, hit2 = lane == i1, lane == i2
    ones = jnp.where(hit1 | hit2, 1.0, 0.0)
    before = jnp.dot(tri_ref[...], ones.astype(BF16), preferred_element_type=F32)
    r1 = jnp.sum(jnp.where(hit1, before, 0.0), axis=-1, keepdims=True)
    r2 = jnp.sum(jnp.where(hit2, before, 0.0), axis=-1, keepdims=True)
    cols = [i1.astype(F32), i2.astype(F32), g1, e2 * g1, r1, r2]
    route = jnp.zeros(route_ref.shape, F32)
    for n, c in enumerate(cols):
        route = jnp.where(lane == n, c, route)
    route_ref[...] = route
    cnt_ref[0] = jnp.broadcast_to(jnp.sum(ones, axis=0, keepdims=True), cnt_ref.shape[1:])


def _route(x2d, g, wr, br):
    t = x2d.shape[0]
    tm = MOE_CHUNK
    tri = jnp.asarray(np.tril(np.ones((tm, tm), np.float32), -1)).astype(BF16)
    row = lambda w: pl.BlockSpec((tm, w), lambda i: (i, 0))
    q = D_MODEL // 4
    return pl.pallas_call(
        _route_kernel,
        grid=(t // tm,),
        in_specs=[row(D_MODEL), _full(g.shape), _full(wr.shape), _full(br.shape), _full(tri.shape)],
        out_specs=[row(q), row(q), row(LANES), pl.BlockSpec((1, 8, LANES), lambda i: (i, 0, 0))],
        out_shape=[jax.ShapeDtypeStruct((t, q), jnp.uint32), jax.ShapeDtypeStruct((t, q), jnp.uint32),
                   jax.ShapeDtypeStruct((t, LANES), F32), jax.ShapeDtypeStruct((t // tm, 8, LANES), F32)],
        compiler_params=_cparams("parallel"),
        name="moe_route",
    )(x2d, g, wr, br, tri)


def _dest_kernel(route_ref, base_ref, o_ref):
    lane = _lane_iota()
    r = route_ref[...]
    base = base_ref[0, 0:1, :]
    lane_f = lane.astype(F32)
    d1 = jnp.sum(jnp.where(lane_f == r[:, 0:1], base, 0.0), axis=-1, keepdims=True) + r[:, 4:5]
    d2 = jnp.sum(jnp.where(lane_f == r[:, 1:2], base, 0.0), axis=-1, keepdims=True) + r[:, 5:6]
    both = jnp.where(lane == 0, d1, jnp.where(lane == 1, d2, 0.0))
    o_ref[...] = both.T[0:8, :].astype(jnp.int32)


def _destinations(route, base):
    t = route.shape[0]
    tm = MOE_CHUNK
    out = pl.pallas_call(
        _dest_kernel,
        grid=(t // tm,),
        in_specs=[pl.BlockSpec((tm, LANES), lambda i: (i, 0)), pl.BlockSpec((1, 8, LANES), lambda i: (i, 0, 0))],
        out_specs=pl.BlockSpec((8, tm), lambda i: (0, i)),
        out_shape=jax.ShapeDtypeStruct((8, t), jnp.int32),
        compiler_params=_cparams("parallel"),
        name="moe_dest",
    )(route, base)
    return out[0:2].reshape(1, 2 * t)


SC_WINDOW = 128


def _sc_mesh():
    return plsc.VectorSubcoreMesh(core_axis_name="core", subcore_axis_name="subcore")


def _sc_scatter_rows(x, idx, n_out):
    n, d = x.shape
    m = idx.shape[1]
    nblk = n // SC_WINDOW

    @pl.kernel(out_type=jax.ShapeDtypeStruct((n_out, d), x.dtype), mesh=_sc_mesh(), name="moe_sc_scatter")
    def scatter(x_hbm, i_hbm, o_hbm):
        def body(x_vmem, i_vmem):
            pltpu.sync_copy(x_vmem, o_hbm.at[i_vmem.at[0]])

        half = m // SC_WINDOW // 2
        pltpu.emit_pipeline(
            body,
            grid=(2, half),
            in_specs=[pl.BlockSpec((SC_WINDOW, d), lambda c, j: ((c * half + j) % nblk, 0)),
                      pl.BlockSpec((1, SC_WINDOW), lambda c, j: (0, c * half + j))],
            out_specs=[],
            core_axis_name=("core", "subcore"),
            dimension_semantics=(pltpu.PARALLEL, pltpu.PARALLEL),
        )(x_hbm, i_hbm)

    return scatter(x, idx)


def _sc_gather_rows(table, idx):
    d = table.shape[1]
    m = idx.shape[1]

    @pl.kernel(out_type=jax.ShapeDtypeStruct((m, d), table.dtype), mesh=_sc_mesh(), name="moe_sc_gather")
    def gather(t_hbm, i_hbm, o_hbm):
        def body(i_vmem, o_vmem):
            pltpu.sync_copy(t_hbm.at[i_vmem.at[0]], o_vmem)

        half = m // SC_WINDOW // 2
        pltpu.emit_pipeline(
            body,
            grid=(2, half),
            in_specs=[pl.BlockSpec((1, SC_WINDOW), lambda c, j: (0, c * half + j))],
            out_specs=[pl.BlockSpec((SC_WINDOW, d), lambda c, j: (c * half + j, 0))],
            core_axis_name=("core", "subcore"),
            dimension_semantics=(pltpu.PARALLEL, pltpu.PARALLEL),
        )(i_hbm, o_hbm)

    return gather(table, idx)


def _expert_kernel(blk_expert_ref, n_used_ref, xa_ref, xb_ref, wg_ref, wu_ref, wd_ref, ya_ref, yb_ref):
    del blk_expert_ref

    @pl.when(pl.program_id(0) < n_used_ref[0])
    def _():
        xe = _unpack_rows(xa_ref[...], xb_ref[...]).astype(BF16)
        act = _silu(jnp.dot(xe, wg_ref[0], preferred_element_type=F32)) * \
            jnp.dot(xe, wu_ref[0], preferred_element_type=F32)
        y = jnp.dot(act.astype(BF16), wd_ref[0], preferred_element_type=F32)
        ya_ref[...], yb_ref[...] = _pack_rows(y)

    @pl.when(pl.program_id(0) >= n_used_ref[0])
    def _():
        ya_ref[...] = jnp.zeros(ya_ref.shape, ya_ref.dtype)
        yb_ref[...] = jnp.zeros(yb_ref.shape, yb_ref.dtype)


def _experts(blk_expert, n_used, xa, xb, wg, wu, wd):
    n_rows, q = xa.shape
    ff = wg.shape[2]
    blk = MOE_BLOCK
    row = pl.BlockSpec((blk, q), lambda b, be, nu: (b, 0))
    grid_spec = pltpu.PrefetchScalarGridSpec(
        num_scalar_prefetch=2,
        grid=(n_rows // blk,),
        in_specs=[row, row,
                  pl.BlockSpec((1, D_MODEL, ff), lambda b, be, nu: (be[b], 0, 0)),
                  pl.BlockSpec((1, D_MODEL, ff), lambda b, be, nu: (be[b], 0, 0)),
                  pl.BlockSpec((1, ff, D_MODEL), lambda b, be, nu: (be[b], 0, 0))],
        out_specs=[row, row])
    return pl.pallas_call(
        _expert_kernel,
        grid_spec=grid_spec,
        out_shape=[jax.ShapeDtypeStruct((n_rows, q), jnp.uint32)] * 2,
        compiler_params=_cparams("arbitrary"),
        name="moe_experts",
    )(blk_expert, n_used, xa, xb, wg, wu, wd)


def _combine_kernel(x_ref, route_ref, a1_ref, b1_ref, a2_ref, b2_ref, o_ref):
    g1 = route_ref[:, 2:3]
    g2 = route_ref[:, 3:4]
    o_ref[...] = x_ref[...] + g1 * _unpack_rows(a1_ref[...], b1_ref[...]) \
        + g2 * _unpack_rows(a2_ref[...], b2_ref[...])


def _combine(x2d, route, ya, yb):
    t = x2d.shape[0]
    tm = ROW_TILE
    nt = t // tm
    q = ya.shape[1]
    row = lambda w: pl.BlockSpec((tm, w), lambda i: (i, 0))
    first = pl.BlockSpec((tm, q), lambda i: (i, 0))
    second = pl.BlockSpec((tm, q), lambda i: (nt + i, 0))
    return pl.pallas_call(
        _combine_kernel,
        grid=(nt,),
        in_specs=[row(D_MODEL), row(LANES), first, first, second, second],
        out_specs=row(D_MODEL),
        out_shape=jax.ShapeDtypeStruct((t, D_MODEL), F32),
        compiler_params=_cparams("parallel"),
        name="moe_combine",
    )(x2d, route, ya, yb, ya, yb)


def _moe_sorted(x2d, g, wr, br, wg, wu, wd):
    t = x2d.shape[0]
    blk = MOE_BLOCK
    n_blocks = 2 * t // blk + N_EXPERTS
    ha, hb, route, cnt = _route(x2d, g, wr, br)

    cnt = cnt[:, 0, :N_EXPERTS].astype(jnp.int32)
    before_chunk = jnp.cumsum(cnt, axis=0) - cnt
    seg_blocks = (jnp.sum(cnt, axis=0) + blk - 1) // blk
    seg_end_blk = jnp.cumsum(seg_blocks)
    seg_start = (seg_end_blk - seg_blocks) * blk
    base = jnp.pad((seg_start[None, :] + before_chunk).astype(F32), ((0, 0), (0, LANES - N_EXPERTS)))
    idx = _destinations(route, jnp.broadcast_to(base[:, None, :], (base.shape[0], 8, LANES)))
    blk_expert = jnp.minimum(jnp.searchsorted(seg_end_blk, jnp.arange(n_blocks, dtype=jnp.int32), side="right"),
                             N_EXPERTS - 1).astype(jnp.int32)
    n_used = seg_end_blk[-1:].astype(jnp.int32)

    xa = _sc_scatter_rows(ha, idx, n_blocks * blk)
    xb = _sc_scatter_rows(hb, idx, n_blocks * blk)
    ya, yb = _experts(blk_expert, n_used, xa, xb, wg, wu, wd)
    return _combine(x2d, route, _sc_gather_rows(ya, idx), _sc_gather_rows(yb, idx))


def _pad_cols(w, width):
    return jnp.pad(w, ((0, 0), (0, width - w.shape[1])))


def _layer_params(l, mix_norm, w_in, b_forget, a_q_norm, a_k_norm, b_v_norm, b_spatial_w, b_spatial_b,
                  c_q_lat_norm, c_w_uq, c_kv_lat_norm, c_w_ukv, c_q_nope_norm, c_q_rope_norm,
                  c_k_nope_norm, c_k_rope_norm, out_norm_a, out_norm_b, out_norm_c, w_out,
                  xattn_norm, mem_norm, w_mem_q, w_mem_kv, m_q_norm, m_k_norm, w_mem_out):
    p = {}
    o = np.cumsum((0, A_W, A_W, A_W, A_HEADS, B_W, B_W, C_Q_RANK, C_KV_RANK, C_ROPE_DIM))
    w = w_in[l]
    seg = lambda n: w[:, o[n]:o[n + 1]]
    fa = seg(3)
    misc = jnp.zeros((D_MODEL, LANES), F32)
    misc = misc.at[:, ROPE_LANE:ROPE_LANE + C_ROPE_DIM].set(seg(8))
    fb = jnp.zeros((1, LANES), F32)
    for hd in range(A_HEADS):
        ln = FORGET_LANE + 8 * (hd // 2) + hd % 2
        misc = misc.at[:, ln].set(fa[:, hd])
        fb = fb.at[0, ln].set(b_forget[l, hd])
    p["w_in"] = jnp.concatenate([seg(0), seg(1), seg(2), seg(4), seg(5), seg(6), seg(7), misc], axis=1).astype(BF16)
    p["fb"] = fb
    p["mix_g"] = mix_norm[l][None]
    p["aq"] = jnp.tile(a_q_norm[l], A_HEADS)[None] * (A_HEAD_DIM ** -0.5 * LOG2E)
    p["ak"] = jnp.tile(a_k_norm[l], A_HEADS)[None]
    p["bvg"] = b_v_norm[l][None]
    pos = np.arange(B_WINDOW)
    mask = (pos[None, :] // CHUNK) <= (pos[:, None] // CHUNK)
    p["ws"] = jnp.where(mask[None], b_spatial_w[l], 0.0).reshape(B_GROUPS * B_WINDOW, B_WINDOW).astype(BF16)
    p["bs"] = jnp.repeat(b_spatial_b[l].T, B_GROUP_DIM, axis=1)
    p["onb"] = out_norm_b[l][None]
    p["cqg"] = c_q_lat_norm[l][None]
    p["ckvg"] = c_kv_lat_norm[l][None]
    gidx = np.arange(A_W) // A_HEAD_DIM
    p["gm"] = jnp.asarray((gidx[:, None] == gidx[None, :]).astype(np.float32) / A_HEAD_DIM).astype(BF16)
    p["tri"] = jnp.asarray(np.tril(np.ones((ROW_TILE, ROW_TILE), np.float32))).astype(BF16)

    qd = C_NOPE_DIM + C_ROPE_DIM
    half = C_ROPE_DIM // 2
    wq = c_w_uq[l]
    wq_partner = jnp.concatenate([jnp.zeros_like(wq[:, :, :C_NOPE_DIM]), wq[:, :, C_NOPE_DIM + half:],
                                  wq[:, :, C_NOPE_DIM:C_NOPE_DIM + half]], axis=-1)
    p["wuq"] = jnp.pad(jnp.stack([wq, wq_partner]), ((0, 0), (0, 0), (0, 0), (0, HEAD_SLAB - qd))
                       ).reshape(2, C_Q_RANK, -1).astype(BF16)
    wukv = c_w_ukv[l]
    p["wuk"] = jnp.pad(wukv[:, :, :C_NOPE_DIM], ((0, 0), (0, 0), (0, HEAD_SLAB - C_NOPE_DIM))
                       ).reshape(C_KV_RANK, -1).astype(BF16)
    p["wuv"] = wukv[:, :, C_NOPE_DIM:].reshape(C_KV_RANK, C_W).astype(BF16)
    gq = jnp.concatenate([c_q_nope_norm[l], c_q_rope_norm[l]])
    gq_partner = jnp.concatenate([jnp.zeros_like(c_q_nope_norm[l]), c_q_rope_norm[l][half:],
                                  c_q_rope_norm[l][:half]])
    p["gq"] = _pad_cols(jnp.stack([gq, gq_partner]) * (qd ** -0.5 * LOG2E), LANES)
    p["gkn"] = _pad_cols(c_k_nope_norm[l][None], LANES)
    p["gkr"] = jnp.zeros((1, LANES), F32).at[0, ROPE_LANE:ROPE_LANE + C_ROPE_DIM].set(c_k_rope_norm[l])

    p["ona"] = out_norm_a[l][None]
    p["onc"] = out_norm_c[l][None]
    p["w_out"] = w_out[l].astype(BF16)
    p["xg"] = xattn_norm[l][None]
    p["w_mem_q"] = w_mem_q[l].astype(BF16)
    p["mqg"] = m_q_norm[l][None] * (M_HEAD_DIM ** -0.5)
    p["mem_g"] = mem_norm[l][None]
    p["w_mem_kv"] = w_mem_kv[l].astype(BF16)
    p["mkg"] = m_k_norm[l][None]
    p["w_mem_out"] = w_mem_out[l].astype(BF16)
    return p


def kernel(x, mem, positions, mix_norm, w_in, b_forget, a_q_norm, a_k_norm, b_v_norm, b_spatial_w, b_spatial_b, c_q_lat_norm, c_w_uq, c_kv_lat_norm, c_w_ukv, c_q_nope_norm, c_q_rope_norm, c_k_nope_norm, c_k_rope_norm, out_norm_a, out_norm_b, out_norm_c, w_out, xattn_norm, mem_norm, w_mem_q, w_mem_kv, m_q_norm, m_k_norm, w_mem_out, ffn_norm, ffn_w_gate, ffn_w_up, ffn_w_down, w_router, b_router, moe_w_gate, moe_w_up, moe_w_down):
    nb, seq, d = x.shape
    assert d == D_MODEL and seq % ROW_TILE == 0 and seq % ATTN_TILE == 0 and (nb * seq) % MOE_CHUNK == 0
    depth = w_in.shape[0]
    t = nb * seq
    x2d = x.reshape(t, d)
    cos, sin = _rope_tables(positions.reshape(t, 1).astype(F32))

    for l in range(depth):
        p = _layer_params(l, mix_norm, w_in, b_forget, a_q_norm, a_k_norm, b_v_norm, b_spatial_w,
                          b_spatial_b, c_q_lat_norm, c_w_uq, c_kv_lat_norm, c_w_ukv, c_q_nope_norm,
                          c_q_rope_norm, c_k_nope_norm, c_k_rope_norm, out_norm_a, out_norm_b,
                          out_norm_c, w_out, xattn_norm, mem_norm, w_mem_q, w_mem_kv, m_q_norm,
                          m_k_norm, w_mem_out)
        qa, ka, vta, bn, cq, ckv, misc = _inproj(x2d, seq, p)
        qc, kc, vtc = _mla_prep(cq, ckv, misc, cos, sin, seq, p)
        a = _attention(qa, ka, vta, unit=1, name="attn_fox")
        c = _attention(qc, kc, vtc, unit=CHUNK, name="attn_mla")
        km, vm = _mem_kv(mem, p)
        x2d = _outproj(x2d, a, bn, c, km, vm, seq, p)
        g = ffn_norm[l][None]
        if l % 2 == 0:
            m = l // 2
            ff = ffn_w_gate.shape[2]
            ff_pad = -(-ff // (2 * LANES)) * (2 * LANES)
            wg = _pad_cols(ffn_w_gate[m], ff_pad).astype(BF16)
            wu = _pad_cols(ffn_w_up[m], ff_pad).astype(BF16)
            wd = jnp.pad(ffn_w_down[m], ((0, ff_pad - ff), (0, 0))).astype(BF16)
            x2d = _ffn(x2d, g, wg, wu, wd)
        else:
            m = l // 2
            wr = _pad_cols(w_router[m], LANES)
            wr_hi = wr.astype(BF16)
            wr = jnp.stack([wr_hi, (wr - wr_hi.astype(F32)).astype(BF16)])
            br = _pad_cols(b_router[m][None], LANES)
            x2d = _moe_sorted(x2d, g, wr, br, moe_w_gate[m].astype(BF16), moe_w_up[m].astype(BF16),
                              moe_w_down[m].astype(BF16))
    return x2d.reshape(nb, seq, d)
```

```python
import functools

import numpy as np
import jax
import jax.numpy as jnp
from jax import lax
from jax.experimental import pallas as pl
from jax.experimental.pallas import tpu as pltpu
from jax.experimental.pallas import tpu_sc as plsc

F32 = jnp.float32
BF16 = jnp.bfloat16
HIGHEST = lax.Precision.HIGHEST

D_MODEL = 1024
CHUNK = 64
EPS = 1e-6
NEG_INF = -1e30
A_HEADS, A_HEAD_DIM = 4, 64
B_GROUPS, B_GROUP_DIM, B_WINDOW = 4, 64, 128
C_HEADS, C_NOPE_DIM, C_ROPE_DIM, C_V_DIM = 8, 64, 32, 64
C_Q_RANK, C_KV_RANK = 256, 128
ROPE_THETA = 10000.0
M_HEADS, M_HEAD_DIM = 4, 128
N_EXPERTS = 8
A_W = A_HEADS * A_HEAD_DIM
B_W = B_GROUPS * B_GROUP_DIM
C_W = C_HEADS * C_V_DIM
M_W = M_HEADS * M_HEAD_DIM

LANES = 128

SEG_Q, SEG_K, SEG_V, SEG_U, SEG_VB, SEG_CQ, SEG_CKV, SEG_MISC = 0, 256, 512, 768, 1024, 1280, 1536, 1664
IN_PAD_W = SEG_MISC + LANES
ROPE_LANE = C_NOPE_DIM
FORGET_LANE = 96
HEAD_SLAB = LANES
VT_ROWS = 80
LOG2E = float(np.log2(np.e))

ROW_TILE = 512
ATTN_TILE = 512
ATTN_UNROLL = 2
MOE_CHUNK = 1024
MOE_BLOCK = 512
VMEM_LIMIT = 56 * 1024 * 1024


def _cparams(*sem):
    return pltpu.CompilerParams(dimension_semantics=sem, vmem_limit_bytes=VMEM_LIMIT)


def _full(shape):
    n = len(shape)
    return pl.BlockSpec(shape, lambda *_: (0,) * n)


def _rms(x):
    return x * lax.rsqrt(jnp.mean(x * x, axis=-1, keepdims=True) + EPS)


def _dot_split(v, exact, pieces, lhs_is_exact=False):
    total = None
    rem = v
    for n in range(pieces):
        part = rem.astype(BF16)
        if n + 1 < pieces:
            rem = rem - part.astype(F32)
        term = (jnp.dot(exact, part, preferred_element_type=F32) if lhs_is_exact
                else jnp.dot(part, exact, preferred_element_type=F32))
        total = term if total is None else total + term
    return total


def _lane_iota(n=LANES):
    return lax.broadcasted_iota(jnp.int32, (1, n), 1)


def _rope_table_kernel(pos_ref, inv_ref, sgn_ref, cos_ref, sin_ref):
    ang = pos_ref[...] * inv_ref[...]
    cos_ref[...] = jnp.cos(ang)
    sin_ref[...] = jnp.sin(ang) * sgn_ref[...]


def _rope_tables(pos_col):
    t = pos_col.shape[0]
    half = C_ROPE_DIM // 2
    inv = ROPE_THETA ** (-jnp.arange(half, dtype=F32) / half)
    inv_l = jnp.zeros((1, LANES), F32).at[0, ROPE_LANE:ROPE_LANE + C_ROPE_DIM].set(jnp.tile(inv, 2))
    sgn = np.zeros((1, LANES), np.float32)
    sgn[0, ROPE_LANE:ROPE_LANE + half] = -1.0
    sgn[0, ROPE_LANE + half:ROPE_LANE + C_ROPE_DIM] = 1.0
    tm = ROW_TILE
    return pl.pallas_call(
        _rope_table_kernel,
        grid=(t // tm,),
        in_specs=[pl.BlockSpec((tm, 1), lambda i: (i, 0)), _full((1, LANES)), _full((1, LANES))],
        out_specs=[pl.BlockSpec((tm, LANES), lambda i: (i, 0))] * 2,
        out_shape=[jax.ShapeDtypeStruct((t, LANES), F32)] * 2,
        compiler_params=_cparams("parallel"),
        name="rope_tables",
    )(pos_col, inv_l, jnp.asarray(sgn))


def _rotate(x, cos, sin_signed, lane):
    half = C_ROPE_DIM // 2
    partner = jnp.where(lane < ROPE_LANE + half,
                        pltpu.roll(x, LANES - half, 1), pltpu.roll(x, half, 1))
    return x * cos + partner * sin_signed


def _store_v_transposed(vt_ref, v, n_heads, lanes=slice(None)):
    tm = v.shape[0]
    v_t = v.T
    tail = jnp.where(lax.broadcasted_iota(jnp.int32, (VT_ROWS - C_V_DIM, tm), 0) == 0, 1.0, 0.0).astype(BF16)
    for hd in range(n_heads):
        vt_ref[0, hd, 0:C_V_DIM, lanes] = v_t[hd * C_V_DIM:(hd + 1) * C_V_DIM, :].astype(BF16)
        vt_ref[0, hd, C_V_DIM:VT_ROWS, lanes] = tail


def _gelu(x):
    return 0.5 * x * (1.0 + lax.erf(x * np.float32(1.0 / np.sqrt(2.0))))


def _inproj_kernel(x0_ref, xa_ref, xb_ref, g_ref, w_ref, aq_ref, ak_ref, fb_ref, bvg_ref, ws_ref, bs_ref,
                   onb_ref, cqg_ref, ckvg_ref, gm_ref, tri_ref,
                   qa_ref, ka_ref, vt_ref, bn_ref, cq_ref, ckv_ref, misc_ref,
                   carry_ref, buf0_ref, buf1_ref, *, tiles_per_seq):
    i = pl.program_id(0)
    tm = xa_ref.shape[0]

    def projection_parts(x_ref, buf_ref):
        h = (_rms(x_ref[...]) * g_ref[...]).astype(BF16)

        def part(lo, hi):
            def run():
                buf_ref[:, lo:hi] = jnp.dot(h, w_ref[:, lo:hi], preferred_element_type=F32)
            return run
        return [part(SEG_Q, SEG_U), part(SEG_U, SEG_CQ), part(SEG_CQ, IN_PAD_W)]

    @pl.when(i == 0)
    def _():
        for run in projection_parts(x0_ref, buf0_ref):
            run()

    refs = (aq_ref, ak_ref, fb_ref, bvg_ref, ws_ref, bs_ref, onb_ref, cqg_ref, ckvg_ref, gm_ref, tri_ref,
            qa_ref, ka_ref, vt_ref, bn_ref, cq_ref, ckv_ref, misc_ref, carry_ref)
    _mixer_prologues(buf0_ref, 0, 2 * i, tm, tiles_per_seq, projection_parts(xa_ref, buf1_ref), *refs)
    _mixer_prologues(buf1_ref, 1, 2 * i + 1, tm, tiles_per_seq, projection_parts(xb_ref, buf0_ref), *refs)


def _mixer_prologues(proj, half, tile, tm, tiles_per_seq, between, aq_ref, ak_ref, fb_ref, bvg_ref, ws_ref,
                     bs_ref, onb_ref, cqg_ref, ckvg_ref, gm_ref, tri_ref, qa_ref, ka_ref, vt_ref, bn_ref,
                     cq_ref, ckv_ref, misc_ref, carry_ref):
    out_rows = slice(half * tm, (half + 1) * tm)
    gm = gm_ref[...]

    def group_mean(v):
        return _dot_split(v, gm, 2)

    misc = proj[:, SEG_MISC:SEG_MISC + LANES]
    misc_ref[out_rows, :] = misc
    z = misc + fb_ref[...]
    log_f = jnp.minimum(z, 0.0) - jnp.log1p(jnp.exp(-jnp.abs(z)))
    carry = jnp.where(tile % tiles_per_seq == 0, 0.0, carry_ref[...])
    cum = _dot_split(log_f, tri_ref[...], 3, lhs_is_exact=True) + carry
    carry_ref[...] = cum[tm - 1:tm, :]
    f_hi = (cum * LOG2E).astype(BF16).astype(F32)
    f_rem = cum * LOG2E - f_hi
    f_mid = f_rem.astype(BF16).astype(F32)
    f_lo = f_rem - f_mid

    between[0]()
    q = proj[:, SEG_Q:SEG_Q + A_W]
    qn = q * lax.rsqrt(group_mean(q * q) + EPS) * aq_ref[...]
    k = proj[:, SEG_K:SEG_K + A_W]
    kn = k * lax.rsqrt(group_mean(k * k) + EPS) * ak_ref[...]
    lane = _lane_iota()
    for hd in range(A_HEADS):
        pair = slice((hd // 2) * LANES, (hd // 2 + 1) * LANES)
        slab = slice(hd * HEAD_SLAB, (hd + 1) * HEAD_SLAB)
        data = (lane < A_HEAD_DIM) if hd % 2 == 0 else (lane >= A_HEAD_DIM)
        e0 = A_HEAD_DIM if hd % 2 == 0 else 0
        fl = FORGET_LANE + 8 * (hd // 2) + hd % 2
        ones = jnp.where((lane >= e0) & (lane < e0 + 3), 1.0, 0.0)
        qa_ref[out_rows, slab] = jnp.where(data, qn[:, pair], ones).astype(BF16)
        bias = jnp.where(lane == e0, -f_hi[:, fl:fl + 1],
                         jnp.where(lane == e0 + 1, -f_mid[:, fl:fl + 1],
                                   jnp.where(lane == e0 + 2, -f_lo[:, fl:fl + 1], 0.0)))
        ka_ref[out_rows, slab] = jnp.where(data, kn[:, pair], bias).astype(BF16)
    _store_v_transposed(vt_ref, proj[:, SEG_V:SEG_V + A_W], A_HEADS, out_rows)

    between[1]()
    u = _gelu(proj[:, SEG_U:SEG_U + B_W])
    v = _gelu(proj[:, SEG_VB:SEG_VB + B_W])
    dv = v - group_mean(v)
    vn = dv * lax.rsqrt(group_mean(dv * dv) + EPS) * bvg_ref[...]
    group = lax.broadcasted_iota(jnp.int32, (1, B_W), 1) // B_GROUP_DIM
    for w in range(tm // B_WINDOW):
        rows = slice(w * B_WINDOW, (w + 1) * B_WINDOW)
        y_all = jnp.dot(ws_ref[...], vn[rows].astype(BF16), preferred_element_type=F32)
        y = bs_ref[...]
        for g in range(B_GROUPS):
            y = y + jnp.where(group == g, y_all[g * B_WINDOW:(g + 1) * B_WINDOW], 0.0)
        b = u[rows] * y
        bn_ref[half * tm + w * B_WINDOW:half * tm + (w + 1) * B_WINDOW, :] = (_rms(b) * onb_ref[...]).astype(BF16)

    between[2]()
    cq_ref[out_rows, :] = (_rms(proj[:, SEG_CQ:SEG_CQ + C_Q_RANK]) * cqg_ref[...]).astype(BF16)
    ckv_ref[out_rows, :] = (_rms(proj[:, SEG_CKV:SEG_CKV + C_KV_RANK]) * ckvg_ref[...]).astype(BF16)


def _inproj(x2d, seq, p):
    t = x2d.shape[0]
    tm = ROW_TILE
    tps = seq // tm
    nb = t // seq
    n_tiles = t // tm
    steps_per_seq = tps // 2
    row = lambda w: pl.BlockSpec((2 * tm, w), lambda i: (i, 0))
    x_tile = lambda index: pl.BlockSpec((tm, D_MODEL), lambda i: (index(i), 0))
    qk_w = A_HEADS * HEAD_SLAB
    out_shape = [
        jax.ShapeDtypeStruct((t, qk_w), BF16), jax.ShapeDtypeStruct((t, qk_w), BF16),
        jax.ShapeDtypeStruct((nb, A_HEADS, VT_ROWS, seq), BF16),
        jax.ShapeDtypeStruct((t, B_W), BF16),
        jax.ShapeDtypeStruct((t, C_Q_RANK), BF16), jax.ShapeDtypeStruct((t, C_KV_RANK), BF16),
        jax.ShapeDtypeStruct((t, LANES), F32),
    ]
    out_specs = [row(qk_w), row(qk_w),
                 pl.BlockSpec((1, A_HEADS, VT_ROWS, 2 * tm),
                              lambda i: (i // steps_per_seq, 0, 0, i % steps_per_seq)),
                 row(B_W), row(C_Q_RANK), row(C_KV_RANK), row(LANES)]
    consts = [p["mix_g"], p["w_in"], p["aq"], p["ak"], p["fb"], p["bvg"], p["ws"], p["bs"], p["onb"],
              p["cqg"], p["ckvg"], p["gm"], p["tri"]]
    return pl.pallas_call(
        functools.partial(_inproj_kernel, tiles_per_seq=tps),
        grid=(n_tiles // 2,),
        in_specs=[x_tile(lambda i: 0), x_tile(lambda i: 2 * i + 1),
                  x_tile(lambda i: jnp.minimum(2 * i + 2, n_tiles - 1))] + [_full(c.shape) for c in consts],
        out_specs=out_specs,
        out_shape=out_shape,
        scratch_shapes=[pltpu.VMEM((1, LANES), F32), pltpu.VMEM((tm, IN_PAD_W), F32),
                        pltpu.VMEM((tm, IN_PAD_W), F32)],
        compiler_params=_cparams("arbitrary"),
        name="in_proj",
    )(x2d, x2d, x2d, *consts)


def _mla_prep_kernel(cq_ref, ckv_ref, misc_ref, cos_ref, sin_ref, wuq_ref, wuk_ref, wuv_ref,
                     gq_ref, gkn_ref, gkr_ref, seg_ref, qc_ref, kc_ref, vt_ref):
    lane = _lane_iota()
    rope = (lane >= ROPE_LANE) & (lane < ROPE_LANE + C_ROPE_DIM)
    cos, sin = cos_ref[...], sin_ref[...]
    q = jnp.dot(cq_ref[...], wuq_ref[0], preferred_element_type=F32)
    q_partner = jnp.dot(cq_ref[...], wuq_ref[1], preferred_element_type=F32)
    q_cos = gq_ref[0:1, :] * cos
    q_sin = gq_ref[1:2, :] * sin
    kn = jnp.dot(ckv_ref[...], wuk_ref[...], preferred_element_type=F32)
    _store_v_transposed(vt_ref, jnp.dot(ckv_ref[...], wuv_ref[...], preferred_element_type=F32), C_HEADS)

    seg = seg_ref[...]

    def inv_rms(v):
        return lax.rsqrt(jnp.dot((v * v).astype(BF16), seg, preferred_element_type=F32) + EPS)

    kr = jnp.where(rope, misc_ref[...], 0.0)
    kr = _rotate(kr * inv_rms(kr) * gkr_ref[...], cos, sin, lane)

    for hd in range(C_HEADS):
        cols = slice(hd * HEAD_SLAB, (hd + 1) * HEAD_SLAB)
        qh = q[:, cols]
        qc_ref[:, cols] = (inv_rms(qh) * (qh * q_cos + q_partner[:, cols] * q_sin)).astype(BF16)
        kh = kn[:, cols]
        kc_ref[:, cols] = (kh * inv_rms(kh) * gkn_ref[...] + kr).astype(BF16)


def _mla_prep(cq, ckv, misc, cos, sin, seq, p):
    t = cq.shape[0]
    tm = ROW_TILE
    tps = seq // tm
    row = lambda w: pl.BlockSpec((tm, w), lambda i: (i, 0))
    consts = [p["wuq"], p["wuk"], p["wuv"], p["gq"], p["gkn"], p["gkr"], p["seg"]]
    qk_w = C_HEADS * HEAD_SLAB
    return pl.pallas_call(
        _mla_prep_kernel,
        grid=(t // tm,),
        in_specs=[row(C_Q_RANK), row(C_KV_RANK), row(LANES), row(LANES), row(LANES)]
                 + [_full(c.shape) for c in consts],
        out_specs=[row(qk_w), row(qk_w),
                   pl.BlockSpec((1, C_HEADS, VT_ROWS, tm), lambda i: (i // tps, 0, 0, i % tps))],
        out_shape=[jax.ShapeDtypeStruct((t, qk_w), BF16), jax.ShapeDtypeStruct((t, qk_w), BF16),
                   jax.ShapeDtypeStruct((t // seq, C_HEADS, VT_ROWS, seq), BF16)],
        compiler_params=_cparams("parallel"),
        name="mla_prep",
    )(cq, ckv, misc, cos, sin, *consts)


def _attn_items(nq):
    items = [(i, j) for i in range(nq) for j in range(i)] + [(i, i) for i in range(nq)]
    return np.array(items, np.int32).T


def _attn_kernel(items_ref, q_ref, k_ref, vt_ref, mask_ref, o_ref, s0_ref, s1_ref, p0_ref, p1_ref,
                 mp0_ref, mp1_ref, mrun_ref, macc_ref, acc_ref, *, n_items):
    tk, tq = mask_ref.shape
    nq = q_ref.shape[0] // tq
    s_bufs, p_bufs, mp_bufs = (s0_ref, s1_ref), (p0_ref, p1_ref), (mp0_ref, mp1_ref)
    mrun_ref[...] = jnp.full(mrun_ref.shape, NEG_INF, F32)
    macc_ref[...] = jnp.full(macc_ref.shape, NEG_INF, F32)
    acc_ref[...] = jnp.zeros(acc_ref.shape, F32)

    def scores(it, buf, masked):
        qi = items_ref[0, it]
        q_start = pl.multiple_of(qi * tq, tq)
        k_start = pl.multiple_of(items_ref[1, it] * tk, tk)
        for hh in range(2):
            cols = slice(hh * HEAD_SLAB, (hh + 1) * HEAD_SLAB)
            s_t = lax.dot_general(k_ref[pl.ds(k_start, tk), cols], q_ref[pl.ds(q_start, tq), cols],
                                  (((1,), (1,)), ((), ())), preferred_element_type=F32)
            if masked:
                s_t = s_t + mask_ref[...]
            s_bufs[buf][hh] = s_t
            mrun_ref[qi, hh] = jnp.maximum(mrun_ref[qi, hh], jnp.max(s_t, axis=0, keepdims=True))

    def exponentiate(it, buf):
        qi = items_ref[0, it]
        for hh in range(2):
            m = mrun_ref[qi, hh]
            p_bufs[buf][hh] = jnp.exp2(s_bufs[buf][hh] - m).astype(BF16)
            mp_bufs[buf][hh] = m

    def accumulate(it, buf):
        qi = items_ref[0, it]
        k_start = pl.multiple_of(items_ref[1, it] * tk, tk)
        for hh in range(2):
            m = mp_bufs[buf][hh]
            pv = jnp.dot(vt_ref[0, hh, :, pl.ds(k_start, tk)], p_bufs[buf][hh],
                         preferred_element_type=F32)
            acc_ref[qi, hh] = jnp.exp2(macc_ref[qi, hh] - m) * acc_ref[qi, hh] + pv
            macc_ref[qi, hh] = m

    def beat(it, par, masked):
        if not isinstance(it, int) or 2 <= it <= n_items + 1:
            accumulate(it - 2, par)
        if not isinstance(it, int) or 1 <= it <= n_items:
            exponentiate(it - 1, 1 - par)
        if not isinstance(it, int) or it < n_items:
            scores(it, par, masked)

    def run_beats(lo, hi, masked):
        lo = max(lo, 0)
        if lo < hi and (lo % 2 == 1 or lo < 2):
            for it in range(lo, min(hi, lo + 2 - lo % 2)):
                beat(it, it % 2, masked)
            lo = min(hi, lo + 2 - lo % 2)
        n_groups = (hi - lo) // ATTN_UNROLL
        if n_groups > 0:
            def group(t, carry, lo=lo):
                for u in range(ATTN_UNROLL):
                    beat(lo + ATTN_UNROLL * t + u, u % 2, masked)
                return carry
            lax.fori_loop(0, n_groups, group, 0)
        for it in range(lo + ATTN_UNROLL * n_groups, hi):
            beat(it, it % 2, masked)

    n_full = n_items - nq
    run_beats(0, n_full, False)
    run_beats(n_full, n_items, True)
    for it in (n_items, n_items + 1):
        beat(it, it % 2, True)

    for qi in range(nq):
        halves = [acc_ref[qi, hh, 0:C_V_DIM, :] / acc_ref[qi, hh, C_V_DIM:C_V_DIM + 1, :] for hh in range(2)]
        o_ref[qi * tq:(qi + 1) * tq, :] = jnp.concatenate(halves, axis=0).T.astype(o_ref.dtype)


def _attention(q, k, vt, *, unit, name):
    t = q.shape[0]
    nb, n_heads, _, seq = vt.shape
    tq = ATTN_TILE
    nq = seq // tq
    items = _attn_items(nq)
    pos = np.arange(tq)
    diag_mask = np.where((pos[:, None] // unit) <= (pos[None, :] // unit), 0.0, NEG_INF)
    mask = jnp.asarray(diag_mask.astype(np.float32))
    seq_blk = lambda w: pl.BlockSpec((seq, w), lambda b, p, items_ref: (b, p))
    grid_spec = pltpu.PrefetchScalarGridSpec(
        num_scalar_prefetch=1,
        grid=(nb, n_heads // 2),
        in_specs=[seq_blk(2 * HEAD_SLAB), seq_blk(2 * HEAD_SLAB),
                  pl.BlockSpec((1, 2, VT_ROWS, seq), lambda b, p, items_ref: (b, p, 0, 0)),
                  pl.BlockSpec((tq, tq), lambda b, p, items_ref: (0, 0), pipeline_mode=pl.Buffered(1))],
        out_specs=seq_blk(2 * C_V_DIM),
        scratch_shapes=[pltpu.VMEM((2, tq, tq), F32), pltpu.VMEM((2, tq, tq), F32),
                        pltpu.VMEM((2, tq, tq), BF16), pltpu.VMEM((2, tq, tq), BF16),
                        pltpu.VMEM((2, 1, tq), F32), pltpu.VMEM((2, 1, tq), F32),
                        pltpu.VMEM((nq, 2, 1, tq), F32), pltpu.VMEM((nq, 2, 1, tq), F32),
                        pltpu.VMEM((nq, 2, VT_ROWS, tq), F32)])
    return pl.pallas_call(
        functools.partial(_attn_kernel, n_items=items.shape[1]),
        grid_spec=grid_spec,
        out_shape=jax.ShapeDtypeStruct((t, n_heads * C_V_DIM), BF16),
        compiler_params=_cparams("parallel", "parallel"),
        name=name,
    )(jnp.asarray(items), q, k, vt, mask)


def _mem_kv_kernel(mem_ref, g_ref, w_ref, kg_ref, k_ref, v_ref):
    mn = (_rms(mem_ref[0]) * g_ref[...]).astype(BF16)
    kv = jnp.dot(mn, w_ref[...], preferred_element_type=F32)
    for hd in range(M_HEADS):
        cols = slice(hd * M_HEAD_DIM, (hd + 1) * M_HEAD_DIM)
        k_ref[0, :, cols] = (_rms(kv[:, cols]) * kg_ref[...]).astype(BF16)
    v_ref[0] = kv[:, M_W:].astype(BF16)


def _mem_kv(mem, p):
    nb, ml, _ = mem.shape
    consts = [p["mem_g"], p["w_mem_kv"], p["mkg"]]
    blk = pl.BlockSpec((1, ml, M_W), lambda b: (b, 0, 0))
    return pl.pallas_call(
        _mem_kv_kernel,
        grid=(nb,),
        in_specs=[pl.BlockSpec((1, ml, D_MODEL), lambda b: (b, 0, 0))] + [_full(c.shape) for c in consts],
        out_specs=[blk, blk],
        out_shape=[jax.ShapeDtypeStruct((nb, ml, M_W), BF16)] * 2,
        compiler_params=_cparams("parallel"),
        name="mem_kv",
    )(mem, *consts)


def _outproj_kernel(x_ref, a_ref, bn_ref, c_ref, ona_ref, onc_ref, wo_ref, xg_ref, wq_ref, mqg_ref,
                    km_ref, vm_ref, wmo_ref, o_ref):
    a_n = (_rms(a_ref[...].astype(F32)) * ona_ref[...]).astype(BF16)
    c_n = (_rms(c_ref[...].astype(F32)) * onc_ref[...]).astype(BF16)
    mix = jnp.concatenate([a_n, bn_ref[...], c_n], axis=-1)
    x1 = x_ref[...] + jnp.dot(mix, wo_ref[...], preferred_element_type=F32)

    h = (_rms(x1) * xg_ref[...]).astype(BF16)
    q = jnp.dot(h, wq_ref[...], preferred_element_type=F32)
    outs = []
    for hd in range(M_HEADS):
        cols = slice(hd * M_HEAD_DIM, (hd + 1) * M_HEAD_DIM)
        qh = (_rms(q[:, cols]) * mqg_ref[...]).astype(BF16)
        s = lax.dot_general(qh, km_ref[0, :, cols], (((1,), (1,)), ((), ())), preferred_element_type=F32)
        e = jnp.exp(s - jnp.max(s, axis=-1, keepdims=True))
        pr = e / jnp.sum(e, axis=-1, keepdims=True)
        outs.append(jnp.dot(pr.astype(BF16), vm_ref[0, :, cols], preferred_element_type=F32).astype(BF16))
    o_ref[...] = x1 + jnp.dot(jnp.concatenate(outs, axis=-1), wmo_ref[...], preferred_element_type=F32)


def _outproj(x2d, a, bn, c, km, vm, seq, p):
    t = x2d.shape[0]
    tm = ROW_TILE
    tps = seq // tm
    ml = km.shape[1]
    row = lambda w: pl.BlockSpec((tm, w), lambda i: (i, 0))
    memblk = pl.BlockSpec((1, ml, M_W), lambda i: (i // tps, 0, 0))
    c1 = [p["ona"], p["onc"], p["w_out"], p["xg"], p["w_mem_q"], p["mqg"]]
    return pl.pallas_call(
        _outproj_kernel,
        grid=(t // tm,),
        in_specs=[row(D_MODEL), row(A_W), row(B_W), row(C_W)] + [_full(c_.shape) for c_ in c1]
                 + [memblk, memblk, _full(p["w_mem_out"].shape)],
        out_specs=row(D_MODEL),
        out_shape=jax.ShapeDtypeStruct((t, D_MODEL), F32),
        compiler_params=_cparams("parallel"),
        name="out_proj_mem_attn",
    )(x2d, a, bn, c, *c1, km, vm, p["w_mem_out"])


def _silu(x):
    return x * jax.nn.sigmoid(x)


def _ffn_kernel(x_ref, g_ref, wg_ref, wu_ref, wd_ref, o_ref, *, n_chunks):
    x = x_ref[...]
    h = (_rms(x) * g_ref[...]).astype(BF16)
    fc = wg_ref.shape[1] // n_chunks
    acc = x
    for c in range(n_chunks):
        cols = slice(c * fc, (c + 1) * fc)
        act = _silu(jnp.dot(h, wg_ref[:, cols], preferred_element_type=F32)) * \
            jnp.dot(h, wu_ref[:, cols], preferred_element_type=F32)
        acc = acc + jnp.dot(act.astype(BF16), wd_ref[cols, :], preferred_element_type=F32)
    o_ref[...] = acc


def _ffn(x2d, g, wg, wu, wd):
    t = x2d.shape[0]
    tm = ROW_TILE
    row = pl.BlockSpec((tm, D_MODEL), lambda i: (i, 0))
    resident = lambda a: pl.BlockSpec(a.shape, lambda i: (0, 0), pipeline_mode=pl.Buffered(1))
    return pl.pallas_call(
        functools.partial(_ffn_kernel, n_chunks=2),
        grid=(t // tm,),
        in_specs=[row, _full(g.shape), resident(wg), resident(wu), resident(wd)],
        out_specs=row,
        out_shape=jax.ShapeDtypeStruct((t, D_MODEL), F32),
        compiler_params=_cparams("parallel"),
        name="ffn_dense",
    )(x2d, g, wg, wu, wd)


def _pack_bf16_pairs(lo, hi):
    lo_bits = pltpu.bitcast(lo.astype(BF16).astype(F32), jnp.uint32)
    hi_bits = pltpu.bitcast(hi.astype(BF16).astype(F32), jnp.uint32)
    return lax.shift_right_logical(lo_bits, jnp.uint32(16)) | (hi_bits & jnp.uint32(0xFFFF0000))


def _unpack_bf16_pairs(words):
    lo = pltpu.bitcast(lax.shift_left(words, jnp.uint32(16)), F32)
    hi = pltpu.bitcast(words & jnp.uint32(0xFFFF0000), F32)
    return lo, hi


def _pack_rows(v):
    q = D_MODEL // 4
    return _pack_bf16_pairs(v[:, 0:q], v[:, q:2 * q]), _pack_bf16_pairs(v[:, 2 * q:3 * q], v[:, 3 * q:])


def _unpack_rows(a, b):
    return jnp.concatenate([*_unpack_bf16_pairs(a), *_unpack_bf16_pairs(b)], axis=-1)


def _route_kernel(x_ref, g_ref, wr_ref, br_ref, tri_ref, ha_ref, hb_ref, route_ref, cnt_ref):
    lane = _lane_iota()
    h = _rms(x_ref[...]) * g_ref[...]
    ha_ref[...], hb_ref[...] = _pack_rows(h)
    h_hi = h.astype(BF16)
    h_lo = (h - h_hi.astype(F32)).astype(BF16)
    logits = (jnp.dot(h_hi, wr_ref[0], preferred_element_type=F32)
              + jnp.dot(h_lo, wr_ref[0], preferred_element_type=F32)
              + jnp.dot(h_hi, wr_ref[1], preferred_element_type=F32)) + br_ref[...]
    logits = jnp.where(lane < N_EXPERTS, logits, -jnp.inf)
    v1 = jnp.max(logits, axis=-1, keepdims=True)
    i1 = jnp.min(jnp.where(logits == v1, lane, LANES), axis=-1, keepdims=True)
    rest = jnp.where(lane == i1, -jnp.inf, logits)
    v2 = jnp.max(rest, axis=-1, keepdims=True)
    i2 = jnp.min(jnp.where(rest == v2, lane, LANES), axis=-1, keepdims=True)
    e2 = jnp.exp(v2 - v1)
    g1 = 1.0 / (1.0 + e2)
    hit1, hit2 = lane == i1, lane == i2
    ones = jnp.where(hit1 | hit2, 1.0, 0.0)
    before = jnp.dot(tri_ref[...], ones.astype(BF16), preferred_element_type=F32)
    r1 = jnp.sum(jnp.where(hit1, before, 0.0), axis=-1, keepdims=True)
    r2 = jnp.sum(jnp.where(hit2, before, 0.0), axis=-1, keepdims=True)
    cols = [i1.astype(F32), i2.astype(F32), g1, e2 * g1, r1, r2]
    route = jnp.zeros(route_ref.shape, F32)
    for n, c in enumerate(cols):
        route = jnp.where(lane == n, c, route)
    route_ref[...] = route
    cnt_ref[0] = jnp.broadcast_to(jnp.sum(ones, axis=0, keepdims=True), cnt_ref.shape[1:])


def _route(x2d, g, wr, br):
    t = x2d.shape[0]
    tm = MOE_CHUNK
    tri = jnp.asarray(np.tril(np.ones((tm, tm), np.float32), -1)).astype(BF16)
    row = lambda w: pl.BlockSpec((tm, w), lambda i: (i, 0))
    q = D_MODEL // 4
    return pl.pallas_call(
        _route_kernel,
        grid=(t // tm,),
        in_specs=[row(D_MODEL), _full(g.shape), _full(wr.shape), _full(br.shape), _full(tri.shape)],
        out_specs=[row(q), row(q), row(LANES), pl.BlockSpec((1, 8, LANES), lambda i: (i, 0, 0))],
        out_shape=[jax.ShapeDtypeStruct((t, q), jnp.uint32), jax.ShapeDtypeStruct((t, q), jnp.uint32),
                   jax.ShapeDtypeStruct((t, LANES), F32), jax.ShapeDtypeStruct((t // tm, 8, LANES), F32)],
        compiler_params=_cparams("parallel"),
        name="moe_route",
    )(x2d, g, wr, br, tri)


def _dest_kernel(route_ref, base_ref, o_ref):
    lane = _lane_iota()
    r = route_ref[...]
    base = base_ref[0, 0:1, :]
    lane_f = lane.astype(F32)
    d1 = jnp.sum(jnp.where(lane_f == r[:, 0:1], base, 0.0), axis=-1, keepdims=True) + r[:, 4:5]
    d2 = jnp.sum(jnp.where(lane_f == r[:, 1:2], base, 0.0), axis=-1, keepdims=True) + r[:, 5:6]
    both = jnp.where(lane == 0, d1, jnp.where(lane == 1, d2, 0.0))
    o_ref[...] = both.T[0:8, :].astype(jnp.int32)


def _destinations(route, base):
    t = route.shape[0]
    tm = MOE_CHUNK
    out = pl.pallas_call(
        _dest_kernel,
        grid=(t // tm,),
        in_specs=[pl.BlockSpec((tm, LANES), lambda i: (i, 0)), pl.BlockSpec((1, 8, LANES), lambda i: (i, 0, 0))],
        out_specs=pl.BlockSpec((8, tm), lambda i: (0, i)),
        out_shape=jax.ShapeDtypeStruct((8, t), jnp.int32),
        compiler_params=_cparams("parallel"),
        name="moe_dest",
    )(route, base)
    return out[0:2].reshape(1, 2 * t)


SC_WINDOW = 128


def _sc_mesh():
    return plsc.VectorSubcoreMesh(core_axis_name="core", subcore_axis_name="subcore")


def _sc_scatter_rows(x, idx, n_out):
    n, d = x.shape
    m = idx.shape[1]
    nblk = n // SC_WINDOW

    @pl.kernel(out_type=jax.ShapeDtypeStruct((n_out, d), x.dtype), mesh=_sc_mesh(), name="moe_sc_scatter")
    def scatter(x_hbm, i_hbm, o_hbm):
        def body(x_vmem, i_vmem):
            pltpu.sync_copy(x_vmem, o_hbm.at[i_vmem.at[0]])

        half = m // SC_WINDOW // 2
        pltpu.emit_pipeline(
            body,
            grid=(2, half),
            in_specs=[pl.BlockSpec((SC_WINDOW, d), lambda c, j: ((c * half + j) % nblk, 0)),
                      pl.BlockSpec((1, SC_WINDOW), lambda c, j: (0, c * half + j))],
            out_specs=[],
            core_axis_name=("core", "subcore"),
            dimension_semantics=(pltpu.PARALLEL, pltpu.PARALLEL),
        )(x_hbm, i_hbm)

    return scatter(x, idx)


def _sc_gather_rows(table, idx):
    d = table.shape[1]
    m = idx.shape[1]

    @pl.kernel(out_type=jax.ShapeDtypeStruct((m, d), table.dtype), mesh=_sc_mesh(), name="moe_sc_gather")
    def gather(t_hbm, i_hbm, o_hbm):
        def body(i_vmem, o_vmem):
            pltpu.sync_copy(t_hbm.at[i_vmem.at[0]], o_vmem)

        half = m // SC_WINDOW // 2
        pltpu.emit_pipeline(
            body,
            grid=(2, half),
            in_specs=[pl.BlockSpec((1, SC_WINDOW), lambda c, j: (0, c * half + j))],
            out_specs=[pl.BlockSpec((SC_WINDOW, d), lambda c, j: (c * half + j, 0))],
            core_axis_name=("core", "subcore"),
            dimension_semantics=(pltpu.PARALLEL, pltpu.PARALLEL),
        )(i_hbm, o_hbm)

    return gather(table, idx)


def _expert_kernel(blk_expert_ref, n_used_ref, xa_ref, xb_ref, wg_ref, wu_ref, wd_ref, ya_ref, yb_ref):
    del blk_expert_ref

    @pl.when(pl.program_id(0) < n_used_ref[0])
    def _():
        xe = _unpack_rows(xa_ref[...], xb_ref[...]).astype(BF16)
        act = _silu(jnp.dot(xe, wg_ref[0], preferred_element_type=F32)) * \
            jnp.dot(xe, wu_ref[0], preferred_element_type=F32)
        y = jnp.dot(act.astype(BF16), wd_ref[0], preferred_element_type=F32)
        ya_ref[...], yb_ref[...] = _pack_rows(y)

    @pl.when(pl.program_id(0) >= n_used_ref[0])
    def _():
        ya_ref[...] = jnp.zeros(ya_ref.shape, ya_ref.dtype)
        yb_ref[...] = jnp.zeros(yb_ref.shape, yb_ref.dtype)


def _experts(blk_expert, n_used, xa, xb, wg, wu, wd):
    n_rows, q = xa.shape
    ff = wg.shape[2]
    blk = MOE_BLOCK
    row = pl.BlockSpec((blk, q), lambda b, be, nu: (b, 0))
    grid_spec = pltpu.PrefetchScalarGridSpec(
        num_scalar_prefetch=2,
        grid=(n_rows // blk,),
        in_specs=[row, row,
                  pl.BlockSpec((1, D_MODEL, ff), lambda b, be, nu: (be[b], 0, 0)),
                  pl.BlockSpec((1, D_MODEL, ff), lambda b, be, nu: (be[b], 0, 0)),
                  pl.BlockSpec((1, ff, D_MODEL), lambda b, be, nu: (be[b], 0, 0))],
        out_specs=[row, row])
    return pl.pallas_call(
        _expert_kernel,
        grid_spec=grid_spec,
        out_shape=[jax.ShapeDtypeStruct((n_rows, q), jnp.uint32)] * 2,
        compiler_params=_cparams("arbitrary"),
        name="moe_experts",
    )(blk_expert, n_used, xa, xb, wg, wu, wd)


def _combine_kernel(x_ref, route_ref, a1_ref, b1_ref, a2_ref, b2_ref, o_ref):
    g1 = route_ref[:, 2:3]
    g2 = route_ref[:, 3:4]
    o_ref[...] = x_ref[...] + g1 * _unpack_rows(a1_ref[...], b1_ref[...]) \
        + g2 * _unpack_rows(a2_ref[...], b2_ref[...])


def _combine(x2d, route, ya, yb):
    t = x2d.shape[0]
    tm = ROW_TILE
    nt = t // tm
    q = ya.shape[1]
    row = lambda w: pl.BlockSpec((tm, w), lambda i: (i, 0))
    first = pl.BlockSpec((tm, q), lambda i: (i, 0))
    second = pl.BlockSpec((tm, q), lambda i: (nt + i, 0))
    return pl.pallas_call(
        _combine_kernel,
        grid=(nt,),
        in_specs=[row(D_MODEL), row(LANES), first, first, second, second],
        out_specs=row(D_MODEL),
        out_shape=jax.ShapeDtypeStruct((t, D_MODEL), F32),
        compiler_params=_cparams("parallel"),
        name="moe_combine",
    )(x2d, route, ya, yb, ya, yb)


def _moe_sorted(x2d, g, wr, br, wg, wu, wd):
    t = x2d.shape[0]
    blk = MOE_BLOCK
    n_blocks = 2 * t // blk + N_EXPERTS
    ha, hb, route, cnt = _route(x2d, g, wr, br)

    cnt = cnt[:, 0, :N_EXPERTS].astype(jnp.int32)
    before_chunk = jnp.cumsum(cnt, axis=0) - cnt
    seg_blocks = (jnp.sum(cnt, axis=0) + blk - 1) // blk
    seg_end_blk = jnp.cumsum(seg_blocks)
    seg_start = (seg_end_blk - seg_blocks) * blk
    base = jnp.pad((seg_start[None, :] + before_chunk).astype(F32), ((0, 0), (0, LANES - N_EXPERTS)))
    idx = _destinations(route, jnp.broadcast_to(base[:, None, :], (base.shape[0], 8, LANES)))
    past_end = jnp.arange(n_blocks, dtype=jnp.int32)[:, None] >= seg_end_blk[None, :]
    blk_expert = jnp.minimum(jnp.sum(past_end, axis=1), N_EXPERTS - 1).astype(jnp.int32)
    n_used = seg_end_blk[-1:].astype(jnp.int32)

    xa = _sc_scatter_rows(ha, idx, n_blocks * blk)
    xb = _sc_scatter_rows(hb, idx, n_blocks * blk)
    ya, yb = _experts(blk_expert, n_used, xa, xb, wg, wu, wd)
    return _combine(x2d, route, _sc_gather_rows(ya, idx), _sc_gather_rows(yb, idx))


def _pad_cols(w, width):
    return jnp.pad(w, ((0, 0), (0, width - w.shape[1])))


def _layer_params(l, mix_norm, w_in, b_forget, a_q_norm, a_k_norm, b_v_norm, b_spatial_w, b_spatial_b,
                  c_q_lat_norm, c_w_uq, c_kv_lat_norm, c_w_ukv, c_q_nope_norm, c_q_rope_norm,
                  c_k_nope_norm, c_k_rope_norm, out_norm_a, out_norm_b, out_norm_c, w_out,
                  xattn_norm, mem_norm, w_mem_q, w_mem_kv, m_q_norm, m_k_norm, w_mem_out):
    p = {}
    o = np.cumsum((0, A_W, A_W, A_W, A_HEADS, B_W, B_W, C_Q_RANK, C_KV_RANK, C_ROPE_DIM))
    w = w_in[l]
    seg = lambda n: w[:, o[n]:o[n + 1]]
    fa = seg(3)
    misc = jnp.zeros((D_MODEL, LANES), F32)
    misc = misc.at[:, ROPE_LANE:ROPE_LANE + C_ROPE_DIM].set(seg(8))
    fb = jnp.zeros((1, LANES), F32)
    for hd in range(A_HEADS):
        ln = FORGET_LANE + 8 * (hd // 2) + hd % 2
        misc = misc.at[:, ln].set(fa[:, hd])
        fb = fb.at[0, ln].set(b_forget[l, hd])
    p["w_in"] = jnp.concatenate([seg(0), seg(1), seg(2), seg(4), seg(5), seg(6), seg(7), misc], axis=1).astype(BF16)
    p["fb"] = fb
    p["mix_g"] = mix_norm[l][None]
    p["aq"] = jnp.tile(a_q_norm[l], A_HEADS)[None] * (A_HEAD_DIM ** -0.5 * LOG2E)
    p["ak"] = jnp.tile(a_k_norm[l], A_HEADS)[None]
    p["bvg"] = b_v_norm[l][None]
    pos = np.arange(B_WINDOW)
    mask = (pos[None, :] // CHUNK) <= (pos[:, None] // CHUNK)
    p["ws"] = jnp.where(mask[None], b_spatial_w[l], 0.0).reshape(B_GROUPS * B_WINDOW, B_WINDOW).astype(BF16)
    p["bs"] = jnp.repeat(b_spatial_b[l].T, B_GROUP_DIM, axis=1)
    p["onb"] = out_norm_b[l][None]
    p["cqg"] = c_q_lat_norm[l][None]
    p["ckvg"] = c_kv_lat_norm[l][None]
    gidx = np.arange(A_W) // A_HEAD_DIM
    p["gm"] = jnp.asarray((gidx[:, None] == gidx[None, :]).astype(np.float32) / A_HEAD_DIM).astype(BF16)
    p["tri"] = jnp.asarray(np.tril(np.ones((ROW_TILE, ROW_TILE), np.float32))).astype(BF16)

    qd = C_NOPE_DIM + C_ROPE_DIM
    half = C_ROPE_DIM // 2
    wq = c_w_uq[l]
    wq_partner = jnp.concatenate([jnp.zeros_like(wq[:, :, :C_NOPE_DIM]), wq[:, :, C_NOPE_DIM + half:],
                                  wq[:, :, C_NOPE_DIM:C_NOPE_DIM + half]], axis=-1)
    p["wuq"] = jnp.pad(jnp.stack([wq, wq_partner]), ((0, 0), (0, 0), (0, 0), (0, HEAD_SLAB - qd))
                       ).reshape(2, C_Q_RANK, -1).astype(BF16)
    wukv = c_w_ukv[l]
    p["wuk"] = jnp.pad(wukv[:, :, :C_NOPE_DIM], ((0, 0), (0, 0), (0, HEAD_SLAB - C_NOPE_DIM))
                       ).reshape(C_KV_RANK, -1).astype(BF16)
    p["wuv"] = wukv[:, :, C_NOPE_DIM:].reshape(C_KV_RANK, C_W).astype(BF16)
    gq = jnp.concatenate([c_q_nope_norm[l], c_q_rope_norm[l]])
    gq_partner = jnp.concatenate([jnp.zeros_like(c_q_nope_norm[l]), c_q_rope_norm[l][half:],
                                  c_q_rope_norm[l][:half]])
    p["gq"] = _pad_cols(jnp.stack([gq, gq_partner]) * (qd ** -0.5 * LOG2E), LANES)
    p["gkn"] = _pad_cols(c_k_nope_norm[l][None], LANES)
    seg = np.zeros((LANES, LANES), np.float32)
    seg[:C_NOPE_DIM, :C_NOPE_DIM] = 1.0 / C_NOPE_DIM
    seg[ROPE_LANE:ROPE_LANE + C_ROPE_DIM, ROPE_LANE:ROPE_LANE + C_ROPE_DIM] = 1.0 / C_ROPE_DIM
    p["seg"] = jnp.asarray(seg).astype(BF16)
    p["gkr"] = jnp.zeros((1, LANES), F32).at[0, ROPE_LANE:ROPE_LANE + C_ROPE_DIM].set(c_k_rope_norm[l])

    p["ona"] = out_norm_a[l][None]
    p["onc"] = out_norm_c[l][None]
    p["w_out"] = w_out[l].astype(BF16)
    p["xg"] = xattn_norm[l][None]
    p["w_mem_q"] = w_mem_q[l].astype(BF16)
    p["mqg"] = m_q_norm[l][None] * (M_HEAD_DIM ** -0.5)
    p["mem_g"] = mem_norm[l][None]
    p["w_mem_kv"] = w_mem_kv[l].astype(BF16)
    p["mkg"] = m_k_norm[l][None]
    p["w_mem_out"] = w_mem_out[l].astype(BF16)
    return p


def kernel(x, mem, positions, mix_norm, w_in, b_forget, a_q_norm, a_k_norm, b_v_norm, b_spatial_w, b_spatial_b, c_q_lat_norm, c_w_uq, c_kv_lat_norm, c_w_ukv, c_q_nope_norm, c_q_rope_norm, c_k_nope_norm, c_k_rope_norm, out_norm_a, out_norm_b, out_norm_c, w_out, xattn_norm, mem_norm, w_mem_q, w_mem_kv, m_q_norm, m_k_norm, w_mem_out, ffn_norm, ffn_w_gate, ffn_w_up, ffn_w_down, w_router, b_router, moe_w_gate, moe_w_up, moe_w_down):
    nb, seq, d = x.shape
    assert d == D_MODEL and seq % ROW_TILE == 0 and seq % ATTN_TILE == 0 and (nb * seq) % MOE_CHUNK == 0
    depth = w_in.shape[0]
    t = nb * seq
    x2d = x.reshape(t, d)
    cos, sin = _rope_tables(positions.reshape(t, 1).astype(F32))

    for l in range(depth):
        p = _layer_params(l, mix_norm, w_in, b_forget, a_q_norm, a_k_norm, b_v_norm, b_spatial_w,
                          b_spatial_b, c_q_lat_norm, c_w_uq, c_kv_lat_norm, c_w_ukv, c_q_nope_norm,
                          c_q_rope_norm, c_k_nope_norm, c_k_rope_norm, out_norm_a, out_norm_b,
                          out_norm_c, w_out, xattn_norm, mem_norm, w_mem_q, w_mem_kv, m_q_norm,
                          m_k_norm, w_mem_out)
        qa, ka, vta, bn, cq, ckv, misc = _inproj(x2d, seq, p)
        qc, kc, vtc = _mla_prep(cq, ckv, misc, cos, sin, seq, p)
        a = _attention(qa, ka, vta, unit=1, name="attn_fox")
        c = _attention(qc, kc, vtc, unit=CHUNK, name="attn_mla")
        km, vm = _mem_kv(mem, p)
        x2d = _outproj(x2d, a, bn, c, km, vm, seq, p)
        g = ffn_norm[l][None]
        if l % 2 == 0:
            m = l // 2
            ff = ffn_w_gate.shape[2]
            ff_pad = -(-ff // (2 * LANES)) * (2 * LANES)
            wg = _pad_cols(ffn_w_gate[m], ff_pad).astype(BF16)
            wu = _pad_cols(ffn_w_up[m], ff_pad).astype(BF16)
            wd = jnp.pad(ffn_w_down[m], ((0, ff_pad - ff), (0, 0))).astype(BF16)
            x2d = _ffn(x2d, g, wg, wu, wd)
        else:
            m = l // 2
            wr = _pad_cols(w_router[m], LANES)
            wr_hi = wr.astype(BF16)
            wr = jnp.stack([wr_hi, (wr - wr_hi.astype(F32)).astype(BF16)])
            br = _pad_cols(b_router[m][None], LANES)
            x2d = _moe_sorted(x2d, g, wr, br, moe_w_gate[m].astype(BF16), moe_w_up[m].astype(BF16),
                              moe_w_down[m].astype(BF16))
    return x2d.reshape(nb, seq, d)
```

```python
import functools

import numpy as np
import jax
import jax.numpy as jnp
from jax import lax
from jax.experimental import pallas as pl
from jax.experimental.pallas import tpu as pltpu
from jax.experimental.pallas import tpu_sc as plsc

F32 = jnp.float32
BF16 = jnp.bfloat16
HIGHEST = lax.Precision.HIGHEST

D_MODEL = 1024
CHUNK = 64
EPS = 1e-6
NEG_INF = -1e30
A_HEADS, A_HEAD_DIM = 4, 64
B_GROUPS, B_GROUP_DIM, B_WINDOW = 4, 64, 128
C_HEADS, C_NOPE_DIM, C_ROPE_DIM, C_V_DIM = 8, 64, 32, 64
C_Q_RANK, C_KV_RANK = 256, 128
ROPE_THETA = 10000.0
M_HEADS, M_HEAD_DIM = 4, 128
N_EXPERTS = 8
A_W = A_HEADS * A_HEAD_DIM
B_W = B_GROUPS * B_GROUP_DIM
C_W = C_HEADS * C_V_DIM
M_W = M_HEADS * M_HEAD_DIM

LANES = 128

SEG_Q, SEG_K, SEG_V, SEG_U, SEG_VB, SEG_CQ, SEG_CKV, SEG_MISC = 0, 256, 512, 768, 1024, 1280, 1536, 1664
IN_PAD_W = SEG_MISC + LANES
ROPE_LANE = C_NOPE_DIM
FORGET_LANE = 96
HEAD_SLAB = LANES
VT_ROWS = 80
LOG2E = float(np.log2(np.e))

ROW_TILE = 512
ATTN_TILE = 512
MOE_CHUNK = 1024
MOE_BLOCK = 512
VMEM_LIMIT = 56 * 1024 * 1024


def _cparams(*sem):
    return pltpu.CompilerParams(dimension_semantics=sem, vmem_limit_bytes=VMEM_LIMIT)


def _full(shape):
    n = len(shape)
    return pl.BlockSpec(shape, lambda *_: (0,) * n)


def _rms(x):
    return x * lax.rsqrt(jnp.mean(x * x, axis=-1, keepdims=True) + EPS)


def _dot_split(v, exact, pieces, lhs_is_exact=False):
    total = None
    rem = v
    for n in range(pieces):
        part = rem.astype(BF16)
        if n + 1 < pieces:
            rem = rem - part.astype(F32)
        term = (jnp.dot(exact, part, preferred_element_type=F32) if lhs_is_exact
                else jnp.dot(part, exact, preferred_element_type=F32))
        total = term if total is None else total + term
    return total


def _lane_iota(n=LANES):
    return lax.broadcasted_iota(jnp.int32, (1, n), 1)


def _rope_table_kernel(pos_ref, inv_ref, sgn_ref, cos_ref, sin_ref):
    ang = pos_ref[...] * inv_ref[...]
    cos_ref[...] = jnp.cos(ang)
    sin_ref[...] = jnp.sin(ang) * sgn_ref[...]


def _rope_tables(pos_col):
    t = pos_col.shape[0]
    half = C_ROPE_DIM // 2
    inv = ROPE_THETA ** (-jnp.arange(half, dtype=F32) / half)
    inv_l = jnp.zeros((1, LANES), F32).at[0, ROPE_LANE:ROPE_LANE + C_ROPE_DIM].set(jnp.tile(inv, 2))
    sgn = np.zeros((1, LANES), np.float32)
    sgn[0, ROPE_LANE:ROPE_LANE + half] = -1.0
    sgn[0, ROPE_LANE + half:ROPE_LANE + C_ROPE_DIM] = 1.0
    tm = ROW_TILE
    return pl.pallas_call(
        _rope_table_kernel,
        grid=(t // tm,),
        in_specs=[pl.BlockSpec((tm, 1), lambda i: (i, 0)), _full((1, LANES)), _full((1, LANES))],
        out_specs=[pl.BlockSpec((tm, LANES), lambda i: (i, 0))] * 2,
        out_shape=[jax.ShapeDtypeStruct((t, LANES), F32)] * 2,
        compiler_params=_cparams("parallel"),
        name="rope_tables",
    )(pos_col, inv_l, jnp.asarray(sgn))


def _rotate(x, cos, sin_signed, lane):
    half = C_ROPE_DIM // 2
    partner = jnp.where(lane < ROPE_LANE + half,
                        pltpu.roll(x, LANES - half, 1), pltpu.roll(x, half, 1))
    return x * cos + partner * sin_signed


def _store_v_transposed(vt_ref, v, n_heads, lanes=slice(None)):
    tm = v.shape[0]
    v_t = v.T
    tail = jnp.where(lax.broadcasted_iota(jnp.int32, (VT_ROWS - C_V_DIM, tm), 0) == 0, 1.0, 0.0).astype(BF16)
    for hd in range(n_heads):
        vt_ref[0, hd, 0:C_V_DIM, lanes] = v_t[hd * C_V_DIM:(hd + 1) * C_V_DIM, :].astype(BF16)
        vt_ref[0, hd, C_V_DIM:VT_ROWS, lanes] = tail


def _gelu(x):
    return 0.5 * x * (1.0 + lax.erf(x * np.float32(1.0 / np.sqrt(2.0))))


def _inproj_kernel(x0_ref, xa_ref, xb_ref, g_ref, w_ref, aq_ref, ak_ref, fb_ref, bvg_ref, ws_ref, bs_ref,
                   onb_ref, cqg_ref, ckvg_ref, gm_ref, tri_ref,
                   qa_ref, ka_ref, vt_ref, bn_ref, cq_ref, ckv_ref, misc_ref,
                   carry_ref, buf0_ref, buf1_ref, *, tiles_per_seq):
    i = pl.program_id(0)
    tm = xa_ref.shape[0]

    def projection_parts(x_ref, buf_ref):
        h = (_rms(x_ref[...]) * g_ref[...]).astype(BF16)

        def part(lo, hi):
            def run():
                buf_ref[:, lo:hi] = jnp.dot(h, w_ref[:, lo:hi], preferred_element_type=F32)
            return run
        return [part(SEG_Q, SEG_U), part(SEG_U, SEG_CQ), part(SEG_CQ, IN_PAD_W)]

    @pl.when(i == 0)
    def _():
        for run in projection_parts(x0_ref, buf0_ref):
            run()

    refs = (aq_ref, ak_ref, fb_ref, bvg_ref, ws_ref, bs_ref, onb_ref, cqg_ref, ckvg_ref, gm_ref, tri_ref,
            qa_ref, ka_ref, vt_ref, bn_ref, cq_ref, ckv_ref, misc_ref, carry_ref)
    _mixer_prologues(buf0_ref, 0, 2 * i, tm, tiles_per_seq, projection_parts(xa_ref, buf1_ref), *refs)
    _mixer_prologues(buf1_ref, 1, 2 * i + 1, tm, tiles_per_seq, projection_parts(xb_ref, buf0_ref), *refs)


def _mixer_prologues(proj, half, tile, tm, tiles_per_seq, between, aq_ref, ak_ref, fb_ref, bvg_ref, ws_ref,
                     bs_ref, onb_ref, cqg_ref, ckvg_ref, gm_ref, tri_ref, qa_ref, ka_ref, vt_ref, bn_ref,
                     cq_ref, ckv_ref, misc_ref, carry_ref):
    out_rows = slice(half * tm, (half + 1) * tm)
    gm = gm_ref[...]

    def group_mean(v):
        return _dot_split(v, gm, 2)

    misc = proj[:, SEG_MISC:SEG_MISC + LANES]
    misc_ref[out_rows, :] = misc
    z = misc + fb_ref[...]
    log_f = jnp.minimum(z, 0.0) - jnp.log1p(jnp.exp(-jnp.abs(z)))
    carry = jnp.where(tile % tiles_per_seq == 0, 0.0, carry_ref[...])
    cum = _dot_split(log_f, tri_ref[...], 3, lhs_is_exact=True) + carry
    carry_ref[...] = cum[tm - 1:tm, :]
    f_hi = (cum * LOG2E).astype(BF16).astype(F32)
    f_rem = cum * LOG2E - f_hi
    f_mid = f_rem.astype(BF16).astype(F32)
    f_lo = f_rem - f_mid

    between[0]()
    q = proj[:, SEG_Q:SEG_Q + A_W]
    qn = q * lax.rsqrt(group_mean(q * q) + EPS) * aq_ref[...]
    k = proj[:, SEG_K:SEG_K + A_W]
    kn = k * lax.rsqrt(group_mean(k * k) + EPS) * ak_ref[...]
    lane = _lane_iota()
    for hd in range(A_HEADS):
        pair = slice((hd // 2) * LANES, (hd // 2 + 1) * LANES)
        slab = slice(hd * HEAD_SLAB, (hd + 1) * HEAD_SLAB)
        data = (lane < A_HEAD_DIM) if hd % 2 == 0 else (lane >= A_HEAD_DIM)
        e0 = A_HEAD_DIM if hd % 2 == 0 else 0
        fl = FORGET_LANE + 8 * (hd // 2) + hd % 2
        ones = jnp.where((lane >= e0) & (lane < e0 + 3), 1.0, 0.0)
        qa_ref[out_rows, slab] = jnp.where(data, qn[:, pair], ones).astype(BF16)
        bias = jnp.where(lane == e0, -f_hi[:, fl:fl + 1],
                         jnp.where(lane == e0 + 1, -f_mid[:, fl:fl + 1],
                                   jnp.where(lane == e0 + 2, -f_lo[:, fl:fl + 1], 0.0)))
        ka_ref[out_rows, slab] = jnp.where(data, kn[:, pair], bias).astype(BF16)
    _store_v_transposed(vt_ref, proj[:, SEG_V:SEG_V + A_W], A_HEADS, out_rows)

    between[1]()
    u = _gelu(proj[:, SEG_U:SEG_U + B_W])
    v = _gelu(proj[:, SEG_VB:SEG_VB + B_W])
    dv = v - group_mean(v)
    vn = dv * lax.rsqrt(group_mean(dv * dv) + EPS) * bvg_ref[...]
    group = lax.broadcasted_iota(jnp.int32, (1, B_W), 1) // B_GROUP_DIM
    for w in range(tm // B_WINDOW):
        rows = slice(w * B_WINDOW, (w + 1) * B_WINDOW)
        y_all = jnp.dot(ws_ref[...], vn[rows].astype(BF16), preferred_element_type=F32)
        y = bs_ref[...]
        for g in range(B_GROUPS):
            y = y + jnp.where(group == g, y_all[g * B_WINDOW:(g + 1) * B_WINDOW], 0.0)
        b = u[rows] * y
        bn_ref[half * tm + w * B_WINDOW:half * tm + (w + 1) * B_WINDOW, :] = (_rms(b) * onb_ref[...]).astype(BF16)

    between[2]()
    cq_ref[out_rows, :] = (_rms(proj[:, SEG_CQ:SEG_CQ + C_Q_RANK]) * cqg_ref[...]).astype(BF16)
    ckv_ref[out_rows, :] = (_rms(proj[:, SEG_CKV:SEG_CKV + C_KV_RANK]) * ckvg_ref[...]).astype(BF16)


def _inproj(x2d, seq, p):
    t = x2d.shape[0]
    tm = ROW_TILE
    tps = seq // tm
    nb = t // seq
    n_tiles = t // tm
    steps_per_seq = tps // 2
    row = lambda w: pl.BlockSpec((2 * tm, w), lambda i: (i, 0))
    x_tile = lambda index: pl.BlockSpec((tm, D_MODEL), lambda i: (index(i), 0))
    qk_w = A_HEADS * HEAD_SLAB
    out_shape = [
        jax.ShapeDtypeStruct((t, qk_w), BF16), jax.ShapeDtypeStruct((t, qk_w), BF16),
        jax.ShapeDtypeStruct((nb, A_HEADS, VT_ROWS, seq), BF16),
        jax.ShapeDtypeStruct((t, B_W), BF16),
        jax.ShapeDtypeStruct((t, C_Q_RANK), BF16), jax.ShapeDtypeStruct((t, C_KV_RANK), BF16),
        jax.ShapeDtypeStruct((t, LANES), F32),
    ]
    out_specs = [row(qk_w), row(qk_w),
                 pl.BlockSpec((1, A_HEADS, VT_ROWS, 2 * tm),
                              lambda i: (i // steps_per_seq, 0, 0, i % steps_per_seq)),
                 row(B_W), row(C_Q_RANK), row(C_KV_RANK), row(LANES)]
    consts = [p["mix_g"], p["w_in"], p["aq"], p["ak"], p["fb"], p["bvg"], p["ws"], p["bs"], p["onb"],
              p["cqg"], p["ckvg"], p["gm"], p["tri"]]
    return pl.pallas_call(
        functools.partial(_inproj_kernel, tiles_per_seq=tps),
        grid=(n_tiles // 2,),
        in_specs=[x_tile(lambda i: 0), x_tile(lambda i: 2 * i + 1),
                  x_tile(lambda i: jnp.minimum(2 * i + 2, n_tiles - 1))] + [_full(c.shape) for c in consts],
        out_specs=out_specs,
        out_shape=out_shape,
        scratch_shapes=[pltpu.VMEM((1, LANES), F32), pltpu.VMEM((tm, IN_PAD_W), F32),
                        pltpu.VMEM((tm, IN_PAD_W), F32)],
        compiler_params=_cparams("arbitrary"),
        name="in_proj",
    )(x2d, x2d, x2d, *consts)


def _mla_prep_kernel(cq_ref, ckv_ref, misc_ref, cos_ref, sin_ref, wuq_ref, wuk_ref, wuv_ref,
                     gq_ref, gkn_ref, gkr_ref, seg_ref, qc_ref, kc_ref, vt_ref):
    lane = _lane_iota()
    rope = (lane >= ROPE_LANE) & (lane < ROPE_LANE + C_ROPE_DIM)
    cos, sin = cos_ref[...], sin_ref[...]
    q = jnp.dot(cq_ref[...], wuq_ref[0], preferred_element_type=F32)
    q_partner = jnp.dot(cq_ref[...], wuq_ref[1], preferred_element_type=F32)
    q_cos = gq_ref[0:1, :] * cos
    q_sin = gq_ref[1:2, :] * sin
    kn = jnp.dot(ckv_ref[...], wuk_ref[...], preferred_element_type=F32)
    _store_v_transposed(vt_ref, jnp.dot(ckv_ref[...], wuv_ref[...], preferred_element_type=F32), C_HEADS)

    seg = seg_ref[...]

    def inv_rms(v):
        return lax.rsqrt(jnp.dot((v * v).astype(BF16), seg, preferred_element_type=F32) + EPS)

    kr = jnp.where(rope, misc_ref[...], 0.0)
    kr = _rotate(kr * inv_rms(kr) * gkr_ref[...], cos, sin, lane)

    for hd in range(C_HEADS):
        cols = slice(hd * HEAD_SLAB, (hd + 1) * HEAD_SLAB)
        qh = q[:, cols]
        qc_ref[:, cols] = (inv_rms(qh) * (qh * q_cos + q_partner[:, cols] * q_sin)).astype(BF16)
        kh = kn[:, cols]
        kc_ref[:, cols] = (kh * inv_rms(kh) * gkn_ref[...] + kr).astype(BF16)


def _mla_prep(cq, ckv, misc, cos, sin, seq, p):
    t = cq.shape[0]
    tm = ROW_TILE
    tps = seq // tm
    row = lambda w: pl.BlockSpec((tm, w), lambda i: (i, 0))
    consts = [p["wuq"], p["wuk"], p["wuv"], p["gq"], p["gkn"], p["gkr"], p["seg"]]
    qk_w = C_HEADS * HEAD_SLAB
    return pl.pallas_call(
        _mla_prep_kernel,
        grid=(t // tm,),
        in_specs=[row(C_Q_RANK), row(C_KV_RANK), row(LANES), row(LANES), row(LANES)]
                 + [_full(c.shape) for c in consts],
        out_specs=[row(qk_w), row(qk_w),
                   pl.BlockSpec((1, C_HEADS, VT_ROWS, tm), lambda i: (i // tps, 0, 0, i % tps))],
        out_shape=[jax.ShapeDtypeStruct((t, qk_w), BF16), jax.ShapeDtypeStruct((t, qk_w), BF16),
                   jax.ShapeDtypeStruct((t // seq, C_HEADS, VT_ROWS, seq), BF16)],
        compiler_params=_cparams("parallel"),
        name="mla_prep",
    )(cq, ckv, misc, cos, sin, *consts)


def _attn_items(nq):
    return [(i, j) for i in range(nq) for j in range(i)] + [(i, i) for i in range(nq)]


def _attn_kernel(q_ref, k_ref, vt_ref, mask_ref, o_ref, s0_ref, s1_ref, p0_ref, p1_ref,
                 mp0_ref, mp1_ref, mrun_ref, macc_ref, acc_ref):
    tk, tq = mask_ref.shape
    nq = q_ref.shape[0] // tq
    items = _attn_items(nq)
    n_items = len(items)
    s_bufs, p_bufs, mp_bufs = (s0_ref, s1_ref), (p0_ref, p1_ref), (mp0_ref, mp1_ref)
    mrun_ref[...] = jnp.full(mrun_ref.shape, NEG_INF, F32)
    macc_ref[...] = jnp.full(macc_ref.shape, NEG_INF, F32)
    acc_ref[...] = jnp.zeros(acc_ref.shape, F32)

    def scores(it, buf):
        qi, kj = items[it]
        for hh in range(2):
            cols = slice(hh * HEAD_SLAB, (hh + 1) * HEAD_SLAB)
            s_t = lax.dot_general(k_ref[kj * tk:(kj + 1) * tk, cols], q_ref[qi * tq:(qi + 1) * tq, cols],
                                  (((1,), (1,)), ((), ())), preferred_element_type=F32)
            if qi == kj:
                s_t = s_t + mask_ref[...]
            s_bufs[buf][hh] = s_t
            mrun_ref[qi, hh] = jnp.maximum(mrun_ref[qi, hh], jnp.max(s_t, axis=0, keepdims=True))

    def exponentiate(it, buf):
        qi, _ = items[it]
        for hh in range(2):
            m = mrun_ref[qi, hh]
            p_bufs[buf][hh] = jnp.exp2(s_bufs[buf][hh] - m).astype(BF16)
            mp_bufs[buf][hh] = m

    def accumulate(it, buf):
        qi, kj = items[it]
        for hh in range(2):
            m = mp_bufs[buf][hh]
            pv = jnp.dot(vt_ref[0, hh, :, kj * tk:(kj + 1) * tk], p_bufs[buf][hh],
                         preferred_element_type=F32)
            acc_ref[qi, hh] = jnp.exp2(macc_ref[qi, hh] - m) * acc_ref[qi, hh] + pv
            macc_ref[qi, hh] = m

    for it in range(n_items + 2):
        par = it % 2
        if 2 <= it:
            accumulate(it - 2, par)
        if 1 <= it <= n_items:
            exponentiate(it - 1, 1 - par)
        if it < n_items:
            scores(it, par)

    for qi in range(nq):
        halves = [acc_ref[qi, hh, 0:C_V_DIM, :] / acc_ref[qi, hh, C_V_DIM:C_V_DIM + 1, :] for hh in range(2)]
        o_ref[qi * tq:(qi + 1) * tq, :] = jnp.concatenate(halves, axis=0).T.astype(o_ref.dtype)


def _attention(q, k, vt, *, unit, name):
    t = q.shape[0]
    nb, n_heads, _, seq = vt.shape
    tq = ATTN_TILE
    nq = seq // tq
    pos = np.arange(tq)
    diag_mask = np.where((pos[:, None] // unit) <= (pos[None, :] // unit), 0.0, NEG_INF)
    mask = jnp.asarray(diag_mask.astype(np.float32))
    seq_blk = lambda w: pl.BlockSpec((seq, w), lambda b, p: (b, p))
    return pl.pallas_call(
        _attn_kernel,
        grid=(nb, n_heads // 2),
        in_specs=[seq_blk(2 * HEAD_SLAB), seq_blk(2 * HEAD_SLAB),
                  pl.BlockSpec((1, 2, VT_ROWS, seq), lambda b, p: (b, p, 0, 0)),
                  pl.BlockSpec((tq, tq), lambda b, p: (0, 0), pipeline_mode=pl.Buffered(1))],
        out_specs=seq_blk(2 * C_V_DIM),
        out_shape=jax.ShapeDtypeStruct((t, n_heads * C_V_DIM), BF16),
        scratch_shapes=[pltpu.VMEM((2, tq, tq), F32), pltpu.VMEM((2, tq, tq), F32),
                        pltpu.VMEM((2, tq, tq), BF16), pltpu.VMEM((2, tq, tq), BF16),
                        pltpu.VMEM((2, 1, tq), F32), pltpu.VMEM((2, 1, tq), F32),
                        pltpu.VMEM((nq, 2, 1, tq), F32), pltpu.VMEM((nq, 2, 1, tq), F32),
                        pltpu.VMEM((nq, 2, VT_ROWS, tq), F32)],
        compiler_params=_cparams("parallel", "parallel"),
        name=name,
    )(q, k, vt, mask)


def _mem_kv_kernel(mem_ref, g_ref, w_ref, kg_ref, k_ref, v_ref):
    mn = (_rms(mem_ref[0]) * g_ref[...]).astype(BF16)
    kv = jnp.dot(mn, w_ref[...], preferred_element_type=F32)
    for hd in range(M_HEADS):
        cols = slice(hd * M_HEAD_DIM, (hd + 1) * M_HEAD_DIM)
        k_ref[0, :, cols] = (_rms(kv[:, cols]) * kg_ref[...]).astype(BF16)
    v_ref[0] = kv[:, M_W:].astype(BF16)


def _mem_kv(mem, p):
    nb, ml, _ = mem.shape
    consts = [p["mem_g"], p["w_mem_kv"], p["mkg"]]
    blk = pl.BlockSpec((1, ml, M_W), lambda b: (b, 0, 0))
    return pl.pallas_call(
        _mem_kv_kernel,
        grid=(nb,),
        in_specs=[pl.BlockSpec((1, ml, D_MODEL), lambda b: (b, 0, 0))] + [_full(c.shape) for c in consts],
        out_specs=[blk, blk],
        out_shape=[jax.ShapeDtypeStruct((nb, ml, M_W), BF16)] * 2,
        compiler_params=_cparams("parallel"),
        name="mem_kv",
    )(mem, *consts)


def _outproj_kernel(x_ref, a_ref, bn_ref, c_ref, ona_ref, onc_ref, wo_ref, xg_ref, wq_ref, mqg_ref,
                    km_ref, vm_ref, wmo_ref, o_ref):
    a_n = (_rms(a_ref[...].astype(F32)) * ona_ref[...]).astype(BF16)
    c_n = (_rms(c_ref[...].astype(F32)) * onc_ref[...]).astype(BF16)
    mix = jnp.concatenate([a_n, bn_ref[...], c_n], axis=-1)
    x1 = x_ref[...] + jnp.dot(mix, wo_ref[...], preferred_element_type=F32)

    h = (_rms(x1) * xg_ref[...]).astype(BF16)
    q = jnp.dot(h, wq_ref[...], preferred_element_type=F32)
    outs = []
    for hd in range(M_HEADS):
        cols = slice(hd * M_HEAD_DIM, (hd + 1) * M_HEAD_DIM)
        qh = (_rms(q[:, cols]) * mqg_ref[...]).astype(BF16)
        s = lax.dot_general(qh, km_ref[0, :, cols], (((1,), (1,)), ((), ())), preferred_element_type=F32)
        e = jnp.exp(s - jnp.max(s, axis=-1, keepdims=True))
        pr = e / jnp.sum(e, axis=-1, keepdims=True)
        outs.append(jnp.dot(pr.astype(BF16), vm_ref[0, :, cols], preferred_element_type=F32).astype(BF16))
    o_ref[...] = x1 + jnp.dot(jnp.concatenate(outs, axis=-1), wmo_ref[...], preferred_element_type=F32)


def _outproj(x2d, a, bn, c, km, vm, seq, p):
    t = x2d.shape[0]
    tm = ROW_TILE
    tps = seq // tm
    ml = km.shape[1]
    row = lambda w: pl.BlockSpec((tm, w), lambda i: (i, 0))
    memblk = pl.BlockSpec((1, ml, M_W), lambda i: (i // tps, 0, 0))
    c1 = [p["ona"], p["onc"], p["w_out"], p["xg"], p["w_mem_q"], p["mqg"]]
    return pl.pallas_call(
        _outproj_kernel,
        grid=(t // tm,),
        in_specs=[row(D_MODEL), row(A_W), row(B_W), row(C_W)] + [_full(c_.shape) for c_ in c1]
                 + [memblk, memblk, _full(p["w_mem_out"].shape)],
        out_specs=row(D_MODEL),
        out_shape=jax.ShapeDtypeStruct((t, D_MODEL), F32),
        compiler_params=_cparams("parallel"),
        name="out_proj_mem_attn",
    )(x2d, a, bn, c, *c1, km, vm, p["w_mem_out"])


def _silu(x):
    return x * jax.nn.sigmoid(x)


def _ffn_kernel(x_ref, g_ref, wg_ref, wu_ref, wd_ref, o_ref, *, n_chunks):
    x = x_ref[...]
    h = (_rms(x) * g_ref[...]).astype(BF16)
    fc = wg_ref.shape[1] // n_chunks
    acc = x
    for c in range(n_chunks):
        cols = slice(c * fc, (c + 1) * fc)
        act = _silu(jnp.dot(h, wg_ref[:, cols], preferred_element_type=F32)) * \
            jnp.dot(h, wu_ref[:, cols], preferred_element_type=F32)
        acc = acc + jnp.dot(act.astype(BF16), wd_ref[cols, :], preferred_element_type=F32)
    o_ref[...] = acc


def _ffn(x2d, g, wg, wu, wd):
    t = x2d.shape[0]
    tm = ROW_TILE
    row = pl.BlockSpec((tm, D_MODEL), lambda i: (i, 0))
    resident = lambda a: pl.BlockSpec(a.shape, lambda i: (0, 0), pipeline_mode=pl.Buffered(1))
    return pl.pallas_call(
        functools.partial(_ffn_kernel, n_chunks=2),
        grid=(t // tm,),
        in_specs=[row, _full(g.shape), resident(wg), resident(wu), resident(wd)],
        out_specs=row,
        out_shape=jax.ShapeDtypeStruct((t, D_MODEL), F32),
        compiler_params=_cparams("parallel"),
        name="ffn_dense",
    )(x2d, g, wg, wu, wd)


def _pack_bf16_pairs(lo, hi):
    lo_bits = pltpu.bitcast(lo.astype(BF16).astype(F32), jnp.uint32)
    hi_bits = pltpu.bitcast(hi.astype(BF16).astype(F32), jnp.uint32)
    return lax.shift_right_logical(lo_bits, jnp.uint32(16)) | (hi_bits & jnp.uint32(0xFFFF0000))


def _unpack_bf16_pairs(words):
    lo = pltpu.bitcast(lax.shift_left(words, jnp.uint32(16)), F32)
    hi = pltpu.bitcast(words & jnp.uint32(0xFFFF0000), F32)
    return lo, hi


def _pack_rows(v):
    q = D_MODEL // 4
    return _pack_bf16_pairs(v[:, 0:q], v[:, q:2 * q]), _pack_bf16_pairs(v[:, 2 * q:3 * q], v[:, 3 * q:])


def _unpack_rows(a, b):
    return jnp.concatenate([*_unpack_bf16_pairs(a), *_unpack_bf16_pairs(b)], axis=-1)


def _route_kernel(x_ref, g_ref, wr_ref, br_ref, tri_ref, ha_ref, hb_ref, route_ref, cnt_ref):
    lane = _lane_iota()
    h = _rms(x_ref[...]) * g_ref[...]
    ha_ref[...], hb_ref[...] = _pack_rows(h)
    h_hi = h.astype(BF16)
    h_lo = (h - h_hi.astype(F32)).astype(BF16)
    logits = (jnp.dot(h_hi, wr_ref[0], preferred_element_type=F32)
              + jnp.dot(h_lo, wr_ref[0], preferred_element_type=F32)
              + jnp.dot(h_hi, wr_ref[1], preferred_element_type=F32)) + br_ref[...]
    logits = jnp.where(lane < N_EXPERTS, logits, -jnp.inf)
    v1 = jnp.max(logits, axis=-1, keepdims=True)
    i1 = jnp.min(jnp.where(logits == v1, lane, LANES), axis=-1, keepdims=True)
    rest = jnp.where(lane == i1, -jnp.inf, logits)
    v2 = jnp.max(rest, axis=-1, keepdims=True)
    i2 = jnp.min(jnp.where(rest == v2, lane, LANES), axis=-1, keepdims=True)
    e2 = jnp.exp(v2 - v1)
    g1 = 1.0 / (1.0 + e2)
    hit1, hit2 = lane == i1, lane == i2
    ones = jnp.where(hit1 | hit2, 1.0, 0.0)
    before = jnp.dot(tri_ref[...], ones.astype(BF16), preferred_element_type=F32)
    r1 = jnp.sum(jnp.where(hit1, before, 0.0), axis=-1, keepdims=True)
    r2 = jnp.sum(jnp.where(hit2, before, 0.0), axis=-1, keepdims=True)
    cols = [i1.astype(F32), i2.astype(F32), g1, e2 * g1, r1, r2]
    route = jnp.zeros(route_ref.shape, F32)
    for n, c in enumerate(cols):
        route = jnp.where(lane == n, c, route)
    route_ref[...] = route
    cnt_ref[0] = jnp.broadcast_to(jnp.sum(ones, axis=0, keepdims=True), cnt_ref.shape[1:])


def _route(x2d, g, wr, br):
    t = x2d.shape[0]
    tm = MOE_CHUNK
    tri = jnp.asarray(np.tril(np.ones((tm, tm), np.float32), -1)).astype(BF16)
    row = lambda w: pl.BlockSpec((tm, w), lambda i: (i, 0))
    q = D_MODEL // 4
    return pl.pallas_call(
        _route_kernel,
        grid=(t // tm,),
        in_specs=[row(D_MODEL), _full(g.shape), _full(wr.shape), _full(br.shape), _full(tri.shape)],
        out_specs=[row(q), row(q), row(LANES), pl.BlockSpec((1, 8, LANES), lambda i: (i, 0, 0))],
        out_shape=[jax.ShapeDtypeStruct((t, q), jnp.uint32), jax.ShapeDtypeStruct((t, q), jnp.uint32),
                   jax.ShapeDtypeStruct((t, LANES), F32), jax.ShapeDtypeStruct((t // tm, 8, LANES), F32)],
        compiler_params=_cparams("parallel"),
        name="moe_route",
    )(x2d, g, wr, br, tri)


def _dest_kernel(route_ref, base_ref, o_ref):
    lane = _lane_iota()
    r = route_ref[...]
    base = base_ref[0, 0:1, :]
    lane_f = lane.astype(F32)
    d1 = jnp.sum(jnp.where(lane_f == r[:, 0:1], base, 0.0), axis=-1, keepdims=True) + r[:, 4:5]
    d2 = jnp.sum(jnp.where(lane_f == r[:, 1:2], base, 0.0), axis=-1, keepdims=True) + r[:, 5:6]
    both = jnp.where(lane == 0, d1, jnp.where(lane == 1, d2, 0.0))
    o_ref[...] = both.T[0:8, :].astype(jnp.int32)


def _destinations(route, base):
    t = route.shape[0]
    tm = MOE_CHUNK
    out = pl.pallas_call(
        _dest_kernel,
        grid=(t // tm,),
        in_specs=[pl.BlockSpec((tm, LANES), lambda i: (i, 0)), pl.BlockSpec((1, 8, LANES), lambda i: (i, 0, 0))],
        out_specs=pl.BlockSpec((8, tm), lambda i: (0, i)),
        out_shape=jax.ShapeDtypeStruct((8, t), jnp.int32),
        compiler_params=_cparams("parallel"),
        name="moe_dest",
    )(route, base)
    return out[0:2].reshape(1, 2 * t)


SC_WINDOW = 128


def _sc_mesh():
    return plsc.VectorSubcoreMesh(core_axis_name="core", subcore_axis_name="subcore")


def _sc_scatter_rows(x, idx, n_out):
    n, d = x.shape
    m = idx.shape[1]
    nblk = n // SC_WINDOW

    @pl.kernel(out_type=jax.ShapeDtypeStruct((n_out, d), x.dtype), mesh=_sc_mesh(), name="moe_sc_scatter")
    def scatter(x_hbm, i_hbm, o_hbm):
        def body(x_vmem, i_vmem):
            pltpu.sync_copy(x_vmem, o_hbm.at[i_vmem.at[0]])

        half = m // SC_WINDOW // 2
        pltpu.emit_pipeline(
            body,
            grid=(2, half),
            in_specs=[pl.BlockSpec((SC_WINDOW, d), lambda c, j: ((c * half + j) % nblk, 0)),
                      pl.BlockSpec((1, SC_WINDOW), lambda c, j: (0, c * half + j))],
            out_specs=[],
            core_axis_name=("core", "subcore"),
            dimension_semantics=(pltpu.PARALLEL, pltpu.PARALLEL),
        )(x_hbm, i_hbm)

    return scatter(x, idx)


def _sc_gather_rows(table, idx):
    d = table.shape[1]
    m = idx.shape[1]

    @pl.kernel(out_type=jax.ShapeDtypeStruct((m, d), table.dtype), mesh=_sc_mesh(), name="moe_sc_gather")
    def gather(t_hbm, i_hbm, o_hbm):
        def body(i_vmem, o_vmem):
            pltpu.sync_copy(t_hbm.at[i_vmem.at[0]], o_vmem)

        half = m // SC_WINDOW // 2
        pltpu.emit_pipeline(
            body,
            grid=(2, half),
            in_specs=[pl.BlockSpec((1, SC_WINDOW), lambda c, j: (0, c * half + j))],
            out_specs=[pl.BlockSpec((SC_WINDOW, d), lambda c, j: (c * half + j, 0))],
            core_axis_name=("core", "subcore"),
            dimension_semantics=(pltpu.PARALLEL, pltpu.PARALLEL),
        )(i_hbm, o_hbm)

    return gather(table, idx)


def _expert_kernel(blk_expert_ref, n_used_ref, xa_ref, xb_ref, wg_ref, wu_ref, wd_ref, ya_ref, yb_ref):
    del blk_expert_ref

    @pl.when(pl.program_id(0) < n_used_ref[0])
    def _():
        xe = _unpack_rows(xa_ref[...], xb_ref[...]).astype(BF16)
        act = _silu(jnp.dot(xe, wg_ref[0], preferred_element_type=F32)) * \
            jnp.dot(xe, wu_ref[0], preferred_element_type=F32)
        y = jnp.dot(act.astype(BF16), wd_ref[0], preferred_element_type=F32)
        ya_ref[...], yb_ref[...] = _pack_rows(y)

    @pl.when(pl.program_id(0) >= n_used_ref[0])
    def _():
        ya_ref[...] = jnp.zeros(ya_ref.shape, ya_ref.dtype)
        yb_ref[...] = jnp.zeros(yb_ref.shape, yb_ref.dtype)


def _experts(blk_expert, n_used, xa, xb, wg, wu, wd):
    n_rows, q = xa.shape
    ff = wg.shape[2]
    blk = MOE_BLOCK
    row = pl.BlockSpec((blk, q), lambda b, be, nu: (b, 0))
    grid_spec = pltpu.PrefetchScalarGridSpec(
        num_scalar_prefetch=2,
        grid=(n_rows // blk,),
        in_specs=[row, row,
                  pl.BlockSpec((1, D_MODEL, ff), lambda b, be, nu: (be[b], 0, 0)),
                  pl.BlockSpec((1, D_MODEL, ff), lambda b, be, nu: (be[b], 0, 0)),
                  pl.BlockSpec((1, ff, D_MODEL), lambda b, be, nu: (be[b], 0, 0))],
        out_specs=[row, row])
    return pl.pallas_call(
        _expert_kernel,
        grid_spec=grid_spec,
        out_shape=[jax.ShapeDtypeStruct((n_rows, q), jnp.uint32)] * 2,
        compiler_params=_cparams("arbitrary"),
        name="moe_experts",
    )(blk_expert, n_used, xa, xb, wg, wu, wd)


def _combine_kernel(x_ref, route_ref, a1_ref, b1_ref, a2_ref, b2_ref, o_ref):
    g1 = route_ref[:, 2:3]
    g2 = route_ref[:, 3:4]
    o_ref[...] = x_ref[...] + g1 * _unpack_rows(a1_ref[...], b1_ref[...]) \
        + g2 * _unpack_rows(a2_ref[...], b2_ref[...])


def _combine(x2d, route, ya, yb):
    t = x2d.shape[0]
    tm = ROW_TILE
    nt = t // tm
    q = ya.shape[1]
    row = lambda w: pl.BlockSpec((tm, w), lambda i: (i, 0))
    first = pl.BlockSpec((tm, q), lambda i: (i, 0))
    second = pl.BlockSpec((tm, q), lambda i: (nt + i, 0))
    return pl.pallas_call(
        _combine_kernel,
        grid=(nt,),
        in_specs=[row(D_MODEL), row(LANES), first, first, second, second],
        out_specs=row(D_MODEL),
        out_shape=jax.ShapeDtypeStruct((t, D_MODEL), F32),
        compiler_params=_cparams("parallel"),
        name="moe_combine",
    )(x2d, route, ya, yb, ya, yb)


def _moe_sorted(x2d, g, wr, br, wg, wu, wd):
    t = x2d.shape[0]
    blk = MOE_BLOCK
    n_blocks = 2 * t // blk + N_EXPERTS
    ha, hb, route, cnt = _route(x2d, g, wr, br)

    cnt = cnt[:, 0, :N_EXPERTS].astype(jnp.int32)
    before_chunk = jnp.cumsum(cnt, axis=0) - cnt
    seg_blocks = (jnp.sum(cnt, axis=0) + blk - 1) // blk
    seg_end_blk = jnp.cumsum(seg_blocks)
    seg_start = (seg_end_blk - seg_blocks) * blk
    base = jnp.pad((seg_start[None, :] + before_chunk).astype(F32), ((0, 0), (0, LANES - N_EXPERTS)))
    idx = _destinations(route, jnp.broadcast_to(base[:, None, :], (base.shape[0], 8, LANES)))
    past_end = jnp.arange(n_blocks, dtype=jnp.int32)[:, None] >= seg_end_blk[None, :]
    blk_expert = jnp.minimum(jnp.sum(past_end, axis=1), N_EXPERTS - 1).astype(jnp.int32)
    n_used = seg_end_blk[-1:].astype(jnp.int32)

    xa = _sc_scatter_rows(ha, idx, n_blocks * blk)
    xb = _sc_scatter_rows(hb, idx, n_blocks * blk)
    ya, yb = _experts(blk_expert, n_used, xa, xb, wg, wu, wd)
    return _combine(x2d, route, _sc_gather_rows(ya, idx), _sc_gather_rows(yb, idx))


def _pad_cols(w, width):
    return jnp.pad(w, ((0, 0), (0, width - w.shape[1])))


def _layer_params(l, mix_norm, w_in, b_forget, a_q_norm, a_k_norm, b_v_norm, b_spatial_w, b_spatial_b,
                  c_q_lat_norm, c_w_uq, c_kv_lat_norm, c_w_ukv, c_q_nope_norm, c_q_rope_norm,
                  c_k_nope_norm, c_k_rope_norm, out_norm_a, out_norm_b, out_norm_c, w_out,
                  xattn_norm, mem_norm, w_mem_q, w_mem_kv, m_q_norm, m_k_norm, w_mem_out):
    p = {}
    o = np.cumsum((0, A_W, A_W, A_W, A_HEADS, B_W, B_W, C_Q_RANK, C_KV_RANK, C_ROPE_DIM))
    w = w_in[l]
    seg = lambda n: w[:, o[n]:o[n + 1]]
    fa = seg(3)
    misc = jnp.zeros((D_MODEL, LANES), F32)
    misc = misc.at[:, ROPE_LANE:ROPE_LANE + C_ROPE_DIM].set(seg(8))
    fb = jnp.zeros((1, LANES), F32)
    for hd in range(A_HEADS):
        ln = FORGET_LANE + 8 * (hd // 2) + hd % 2
        misc = misc.at[:, ln].set(fa[:, hd])
        fb = fb.at[0, ln].set(b_forget[l, hd])
    p["w_in"] = jnp.concatenate([seg(0), seg(1), seg(2), seg(4), seg(5), seg(6), seg(7), misc], axis=1).astype(BF16)
    p["fb"] = fb
    p["mix_g"] = mix_norm[l][None]
    p["aq"] = jnp.tile(a_q_norm[l], A_HEADS)[None] * (A_HEAD_DIM ** -0.5 * LOG2E)
    p["ak"] = jnp.tile(a_k_norm[l], A_HEADS)[None]
    p["bvg"] = b_v_norm[l][None]
    pos = np.arange(B_WINDOW)
    mask = (pos[None, :] // CHUNK) <= (pos[:, None] // CHUNK)
    p["ws"] = jnp.where(mask[None], b_spatial_w[l], 0.0).reshape(B_GROUPS * B_WINDOW, B_WINDOW).astype(BF16)
    p["bs"] = jnp.repeat(b_spatial_b[l].T, B_GROUP_DIM, axis=1)
    p["onb"] = out_norm_b[l][None]
    p["cqg"] = c_q_lat_norm[l][None]
    p["ckvg"] = c_kv_lat_norm[l][None]
    gidx = np.arange(A_W) // A_HEAD_DIM
    p["gm"] = jnp.asarray((gidx[:, None] == gidx[None, :]).astype(np.float32) / A_HEAD_DIM).astype(BF16)
    p["tri"] = jnp.asarray(np.tril(np.ones((ROW_TILE, ROW_TILE), np.float32))).astype(BF16)

    qd = C_NOPE_DIM + C_ROPE_DIM
    half = C_ROPE_DIM // 2
    wq = c_w_uq[l]
    wq_partner = jnp.concatenate([jnp.zeros_like(wq[:, :, :C_NOPE_DIM]), wq[:, :, C_NOPE_DIM + half:],
                                  wq[:, :, C_NOPE_DIM:C_NOPE_DIM + half]], axis=-1)
    p["wuq"] = jnp.pad(jnp.stack([wq, wq_partner]), ((0, 0), (0, 0), (0, 0), (0, HEAD_SLAB - qd))
                       ).reshape(2, C_Q_RANK, -1).astype(BF16)
    wukv = c_w_ukv[l]
    p["wuk"] = jnp.pad(wukv[:, :, :C_NOPE_DIM], ((0, 0), (0, 0), (0, HEAD_SLAB - C_NOPE_DIM))
                       ).reshape(C_KV_RANK, -1).astype(BF16)
    p["wuv"] = wukv[:, :, C_NOPE_DIM:].reshape(C_KV_RANK, C_W).astype(BF16)
    gq = jnp.concatenate([c_q_nope_norm[l], c_q_rope_norm[l]])
    gq_partner = jnp.concatenate([jnp.zeros_like(c_q_nope_norm[l]), c_q_rope_norm[l][half:],
                                  c_q_rope_norm[l][:half]])
    p["gq"] = _pad_cols(jnp.stack([gq, gq_partner]) * (qd ** -0.5 * LOG2E), LANES)
    p["gkn"] = _pad_cols(c_k_nope_norm[l][None], LANES)
    seg = np.zeros((LANES, LANES), np.float32)
    seg[:C_NOPE_DIM, :C_NOPE_DIM] = 1.0 / C_NOPE_DIM
    seg[ROPE_LANE:ROPE_LANE + C_ROPE_DIM, ROPE_LANE:ROPE_LANE + C_ROPE_DIM] = 1.0 / C_ROPE_DIM
    p["seg"] = jnp.asarray(seg).astype(BF16)
    p["gkr"] = jnp.zeros((1, LANES), F32).at[0, ROPE_LANE:ROPE_LANE + C_ROPE_DIM].set(c_k_rope_norm[l])

    p["ona"] = out_norm_a[l][None]
    p["onc"] = out_norm_c[l][None]
    p["w_out"] = w_out[l].astype(BF16)
    p["xg"] = xattn_norm[l][None]
    p["w_mem_q"] = w_mem_q[l].astype(BF16)
    p["mqg"] = m_q_norm[l][None] * (M_HEAD_DIM ** -0.5)
    p["mem_g"] = mem_norm[l][None]
    p["w_mem_kv"] = w_mem_kv[l].astype(BF16)
    p["mkg"] = m_k_norm[l][None]
    p["w_mem_out"] = w_mem_out[l].astype(BF16)
    return p


def kernel(x, mem, positions, mix_norm, w_in, b_forget, a_q_norm, a_k_norm, b_v_norm, b_spatial_w, b_spatial_b, c_q_lat_norm, c_w_uq, c_kv_lat_norm, c_w_ukv, c_q_nope_norm, c_q_rope_norm, c_k_nope_norm, c_k_rope_norm, out_norm_a, out_norm_b, out_norm_c, w_out, xattn_norm, mem_norm, w_mem_q, w_mem_kv, m_q_norm, m_k_norm, w_mem_out, ffn_norm, ffn_w_gate, ffn_w_up, ffn_w_down, w_router, b_router, moe_w_gate, moe_w_up, moe_w_down):
    nb, seq, d = x.shape
    assert d == D_MODEL and seq % ROW_TILE == 0 and seq % ATTN_TILE == 0 and (nb * seq) % MOE_CHUNK == 0
    depth = w_in.shape[0]
    t = nb * seq
    x2d = x.reshape(t, d)
    cos, sin = _rope_tables(positions.reshape(t, 1).astype(F32))

    for l in range(depth):
        p = _layer_params(l, mix_norm, w_in, b_forget, a_q_norm, a_k_norm, b_v_norm, b_spatial_w,
                          b_spatial_b, c_q_lat_norm, c_w_uq, c_kv_lat_norm, c_w_ukv, c_q_nope_norm,
                          c_q_rope_norm, c_k_nope_norm, c_k_rope_norm, out_norm_a, out_norm_b,
                          out_norm_c, w_out, xattn_norm, mem_norm, w_mem_q, w_mem_kv, m_q_norm,
                          m_k_norm, w_mem_out)
        qa, ka, vta, bn, cq, ckv, misc = _inproj(x2d, seq, p)
        qc, kc, vtc = _mla_prep(cq, ckv, misc, cos, sin, seq, p)
        a = _attention(qa, ka, vta, unit=1, name="attn_fox")
        c = _attention(qc, kc, vtc, unit=CHUNK, name="attn_mla")
        km, vm = _mem_kv(mem, p)
        x2d = _outproj(x2d, a, bn, c, km, vm, seq, p)
        g = ffn_norm[l][None]
        if l % 2 == 0:
            m = l // 2
            ff = ffn_w_gate.shape[2]
            ff_pad = -(-ff // (2 * LANES)) * (2 * LANES)
            wg = _pad_cols(ffn_w_gate[m], ff_pad).astype(BF16)
            wu = _pad_cols(ffn_w_up[m], ff_pad).astype(BF16)
            wd = jnp.pad(ffn_w_down[m], ((0, ff_pad - ff), (0, 0))).astype(BF16)
            x2d = _ffn(x2d, g, wg, wu, wd)
        else:
            m = l // 2
            wr = _pad_cols(w_router[m], LANES)
            wr_hi = wr.astype(BF16)
            wr = jnp.stack([wr_hi, (wr - wr_hi.astype(F32)).astype(BF16)])
            br = _pad_cols(b_router[m][None], LANES)
            x2d = _moe_sorted(x2d, g, wr, br, moe_w_gate[m].astype(BF16), moe_w_up[m].astype(BF16),
                              moe_w_down[m].astype(BF16))
    return x2d.reshape(nb, seq, d)
```

```python
import functools

import numpy as np
import jax
import jax.numpy as jnp
from jax import lax
from jax.experimental import pallas as pl
from jax.experimental.pallas import tpu as pltpu
from jax.experimental.pallas import tpu_sc as plsc

F32 = jnp.float32
BF16 = jnp.bfloat16
HIGHEST = lax.Precision.HIGHEST

D_MODEL = 1024
CHUNK = 64
EPS = 1e-6
NEG_INF = -1e30
A_HEADS, A_HEAD_DIM = 4, 64
B_GROUPS, B_GROUP_DIM, B_WINDOW = 4, 64, 128
C_HEADS, C_NOPE_DIM, C_ROPE_DIM, C_V_DIM = 8, 64, 32, 64
C_Q_RANK, C_KV_RANK = 256, 128
ROPE_THETA = 10000.0
M_HEADS, M_HEAD_DIM = 4, 128
N_EXPERTS = 8
A_W = A_HEADS * A_HEAD_DIM
B_W = B_GROUPS * B_GROUP_DIM
C_W = C_HEADS * C_V_DIM
M_W = M_HEADS * M_HEAD_DIM

LANES = 128

SEG_Q, SEG_K, SEG_V, SEG_U, SEG_VB, SEG_CQ, SEG_CKV, SEG_MISC = 0, 256, 512, 768, 1024, 1280, 1536, 1664
IN_PAD_W = SEG_MISC + LANES
ROPE_LANE = C_NOPE_DIM
FORGET_LANE = 96
HEAD_SLAB = LANES
VT_ROWS = 80
LOG2E = float(np.log2(np.e))

ROW_TILE = 512
ATTN_TILE = 512
MOE_CHUNK = 1024
MOE_BLOCK = 512
VMEM_LIMIT = 56 * 1024 * 1024


def _cparams(*sem):
    return pltpu.CompilerParams(dimension_semantics=sem, vmem_limit_bytes=VMEM_LIMIT)


def _full(shape):
    n = len(shape)
    return pl.BlockSpec(shape, lambda *_: (0,) * n)


def _rms(x):
    return x * lax.rsqrt(jnp.mean(x * x, axis=-1, keepdims=True) + EPS)


def _dot_split(v, exact, pieces, lhs_is_exact=False):
    total = None
    rem = v
    for n in range(pieces):
        part = rem.astype(BF16)
        if n + 1 < pieces:
            rem = rem - part.astype(F32)
        term = (jnp.dot(exact, part, preferred_element_type=F32) if lhs_is_exact
                else jnp.dot(part, exact, preferred_element_type=F32))
        total = term if total is None else total + term
    return total


def _lane_iota(n=LANES):
    return lax.broadcasted_iota(jnp.int32, (1, n), 1)


def _rope_table_kernel(pos_ref, inv_ref, sgn_ref, cos_ref, sin_ref):
    ang = pos_ref[...] * inv_ref[...]
    cos_ref[...] = jnp.cos(ang)
    sin_ref[...] = jnp.sin(ang) * sgn_ref[...]


def _rope_tables(pos_col):
    t = pos_col.shape[0]
    half = C_ROPE_DIM // 2
    inv = ROPE_THETA ** (-jnp.arange(half, dtype=F32) / half)
    inv_l = jnp.zeros((1, LANES), F32).at[0, ROPE_LANE:ROPE_LANE + C_ROPE_DIM].set(jnp.tile(inv, 2))
    sgn = np.zeros((1, LANES), np.float32)
    sgn[0, ROPE_LANE:ROPE_LANE + half] = -1.0
    sgn[0, ROPE_LANE + half:ROPE_LANE + C_ROPE_DIM] = 1.0
    tm = ROW_TILE
    return pl.pallas_call(
        _rope_table_kernel,
        grid=(t // tm,),
        in_specs=[pl.BlockSpec((tm, 1), lambda i: (i, 0)), _full((1, LANES)), _full((1, LANES))],
        out_specs=[pl.BlockSpec((tm, LANES), lambda i: (i, 0))] * 2,
        out_shape=[jax.ShapeDtypeStruct((t, LANES), F32)] * 2,
        compiler_params=_cparams("parallel"),
        name="rope_tables",
    )(pos_col, inv_l, jnp.asarray(sgn))


def _rotate(x, cos, sin_signed, lane):
    half = C_ROPE_DIM // 2
    partner = jnp.where(lane < ROPE_LANE + half,
                        pltpu.roll(x, LANES - half, 1), pltpu.roll(x, half, 1))
    return x * cos + partner * sin_signed


def _store_v_transposed(vt_ref, v, n_heads, lanes=slice(None)):
    tm = v.shape[0]
    v_t = v.T
    tail = jnp.where(lax.broadcasted_iota(jnp.int32, (VT_ROWS - C_V_DIM, tm), 0) == 0, 1.0, 0.0).astype(BF16)
    for hd in range(n_heads):
        vt_ref[0, hd, 0:C_V_DIM, lanes] = v_t[hd * C_V_DIM:(hd + 1) * C_V_DIM, :].astype(BF16)
        vt_ref[0, hd, C_V_DIM:VT_ROWS, lanes] = tail


def _gelu(x):
    return 0.5 * x * (1.0 + lax.erf(x * np.float32(1.0 / np.sqrt(2.0))))


def _inproj_kernel(x0_ref, xa_ref, xb_ref, g_ref, w_ref, aq_ref, ak_ref, fb_ref, bvg_ref, ws_ref, bs_ref,
                   onb_ref, cqg_ref, ckvg_ref, gm_ref, tri_ref,
                   qa_ref, ka_ref, vt_ref, bn_ref, cq_ref, ckv_ref, misc_ref,
                   carry_ref, buf0_ref, buf1_ref, *, tiles_per_seq):
    i = pl.program_id(0)
    tm = xa_ref.shape[0]

    def projection_parts(x_ref, buf_ref):
        h = (_rms(x_ref[...]) * g_ref[...]).astype(BF16)

        def part(lo, hi):
            def run():
                buf_ref[:, lo:hi] = jnp.dot(h, w_ref[:, lo:hi], preferred_element_type=F32)
            return run
        return [part(SEG_Q, SEG_U), part(SEG_U, SEG_CQ), part(SEG_CQ, IN_PAD_W)]

    @pl.when(i == 0)
    def _():
        for run in projection_parts(x0_ref, buf0_ref):
            run()

    refs = (aq_ref, ak_ref, fb_ref, bvg_ref, ws_ref, bs_ref, onb_ref, cqg_ref, ckvg_ref, gm_ref, tri_ref,
            qa_ref, ka_ref, vt_ref, bn_ref, cq_ref, ckv_ref, misc_ref, carry_ref)
    _mixer_prologues(buf0_ref, 0, 2 * i, tm, tiles_per_seq, projection_parts(xa_ref, buf1_ref), *refs)
    _mixer_prologues(buf1_ref, 1, 2 * i + 1, tm, tiles_per_seq, projection_parts(xb_ref, buf0_ref), *refs)


def _mixer_prologues(proj, half, tile, tm, tiles_per_seq, between, aq_ref, ak_ref, fb_ref, bvg_ref, ws_ref,
                     bs_ref, onb_ref, cqg_ref, ckvg_ref, gm_ref, tri_ref, qa_ref, ka_ref, vt_ref, bn_ref,
                     cq_ref, ckv_ref, misc_ref, carry_ref):
    out_rows = slice(half * tm, (half + 1) * tm)
    gm = gm_ref[...]

    def group_mean(v):
        return _dot_split(v, gm, 2)

    misc = proj[:, SEG_MISC:SEG_MISC + LANES]
    misc_ref[out_rows, :] = misc
    z = misc + fb_ref[...]
    log_f = jnp.minimum(z, 0.0) - jnp.log1p(jnp.exp(-jnp.abs(z)))
    carry = jnp.where(tile % tiles_per_seq == 0, 0.0, carry_ref[...])
    cum = _dot_split(log_f, tri_ref[...], 3, lhs_is_exact=True) + carry
    carry_ref[...] = cum[tm - 1:tm, :]
    f_hi = (cum * LOG2E).astype(BF16).astype(F32)
    f_rem = cum * LOG2E - f_hi
    f_mid = f_rem.astype(BF16).astype(F32)
    f_lo = f_rem - f_mid

    between[0]()
    q = proj[:, SEG_Q:SEG_Q + A_W]
    qn = q * lax.rsqrt(group_mean(q * q) + EPS) * aq_ref[...]
    k = proj[:, SEG_K:SEG_K + A_W]
    kn = k * lax.rsqrt(group_mean(k * k) + EPS) * ak_ref[...]
    lane = _lane_iota()
    for hd in range(A_HEADS):
        pair = slice((hd // 2) * LANES, (hd // 2 + 1) * LANES)
        slab = slice(hd * HEAD_SLAB, (hd + 1) * HEAD_SLAB)
        data = (lane < A_HEAD_DIM) if hd % 2 == 0 else (lane >= A_HEAD_DIM)
        e0 = A_HEAD_DIM if hd % 2 == 0 else 0
        fl = FORGET_LANE + 8 * (hd // 2) + hd % 2
        ones = jnp.where((lane >= e0) & (lane < e0 + 3), 1.0, 0.0)
        qa_ref[out_rows, slab] = jnp.where(data, qn[:, pair], ones).astype(BF16)
        bias = jnp.where(lane == e0, -f_hi[:, fl:fl + 1],
                         jnp.where(lane == e0 + 1, -f_mid[:, fl:fl + 1],
                                   jnp.where(lane == e0 + 2, -f_lo[:, fl:fl + 1], 0.0)))
        ka_ref[out_rows, slab] = jnp.where(data, kn[:, pair], bias).astype(BF16)
    _store_v_transposed(vt_ref, proj[:, SEG_V:SEG_V + A_W], A_HEADS, out_rows)

    between[1]()
    u = _gelu(proj[:, SEG_U:SEG_U + B_W])
    v = _gelu(proj[:, SEG_VB:SEG_VB + B_W])
    dv = v - group_mean(v)
    vn = dv * lax.rsqrt(group_mean(dv * dv) + EPS) * bvg_ref[...]
    group = lax.broadcasted_iota(jnp.int32, (1, B_W), 1) // B_GROUP_DIM
    for w in range(tm // B_WINDOW):
        rows = slice(w * B_WINDOW, (w + 1) * B_WINDOW)
        y_all = jnp.dot(ws_ref[...], vn[rows].astype(BF16), preferred_element_type=F32)
        y = bs_ref[...]
        for g in range(B_GROUPS):
            y = y + jnp.where(group == g, y_all[g * B_WINDOW:(g + 1) * B_WINDOW], 0.0)
        b = u[rows] * y
        bn_ref[half * tm + w * B_WINDOW:half * tm + (w + 1) * B_WINDOW, :] = (_rms(b) * onb_ref[...]).astype(BF16)

    between[2]()
    cq_ref[out_rows, :] = (_rms(proj[:, SEG_CQ:SEG_CQ + C_Q_RANK]) * cqg_ref[...]).astype(BF16)
    ckv_ref[out_rows, :] = (_rms(proj[:, SEG_CKV:SEG_CKV + C_KV_RANK]) * ckvg_ref[...]).astype(BF16)


def _inproj(x2d, seq, p):
    t = x2d.shape[0]
    tm = ROW_TILE
    tps = seq // tm
    nb = t // seq
    n_tiles = t // tm
    steps_per_seq = tps // 2
    row = lambda w: pl.BlockSpec((2 * tm, w), lambda i: (i, 0))
    x_tile = lambda index: pl.BlockSpec((tm, D_MODEL), lambda i: (index(i), 0))
    qk_w = A_HEADS * HEAD_SLAB
    out_shape = [
        jax.ShapeDtypeStruct((t, qk_w), BF16), jax.ShapeDtypeStruct((t, qk_w), BF16),
        jax.ShapeDtypeStruct((nb, A_HEADS, VT_ROWS, seq), BF16),
        jax.ShapeDtypeStruct((t, B_W), BF16),
        jax.ShapeDtypeStruct((t, C_Q_RANK), BF16), jax.ShapeDtypeStruct((t, C_KV_RANK), BF16),
        jax.ShapeDtypeStruct((t, LANES), F32),
    ]
    out_specs = [row(qk_w), row(qk_w),
                 pl.BlockSpec((1, A_HEADS, VT_ROWS, 2 * tm),
                              lambda i: (i // steps_per_seq, 0, 0, i % steps_per_seq)),
                 row(B_W), row(C_Q_RANK), row(C_KV_RANK), row(LANES)]
    consts = [p["mix_g"], p["w_in"], p["aq"], p["ak"], p["fb"], p["bvg"], p["ws"], p["bs"], p["onb"],
              p["cqg"], p["ckvg"], p["gm"], p["tri"]]
    return pl.pallas_call(
        functools.partial(_inproj_kernel, tiles_per_seq=tps),
        grid=(n_tiles // 2,),
        in_specs=[x_tile(lambda i: 0), x_tile(lambda i: 2 * i + 1),
                  x_tile(lambda i: jnp.minimum(2 * i + 2, n_tiles - 1))] + [_full(c.shape) for c in consts],
        out_specs=out_specs,
        out_shape=out_shape,
        scratch_shapes=[pltpu.VMEM((1, LANES), F32), pltpu.VMEM((tm, IN_PAD_W), F32),
                        pltpu.VMEM((tm, IN_PAD_W), F32)],
        compiler_params=_cparams("arbitrary"),
        name="in_proj",
    )(x2d, x2d, x2d, *consts)


def _mla_prep_kernel(cq_ref, ckv_ref, misc_ref, cos_ref, sin_ref, wuq_ref, wuk_ref, wuv_ref,
                     gq_ref, gkn_ref, gkr_ref, seg_ref, qc_ref, kc_ref, vt_ref):
    lane = _lane_iota()
    rope = (lane >= ROPE_LANE) & (lane < ROPE_LANE + C_ROPE_DIM)
    cos, sin = cos_ref[...], sin_ref[...]
    q = jnp.dot(cq_ref[...], wuq_ref[0], preferred_element_type=F32)
    q_partner = jnp.dot(cq_ref[...], wuq_ref[1], preferred_element_type=F32)
    q_cos = gq_ref[0:1, :] * cos
    q_sin = gq_ref[1:2, :] * sin
    kn = jnp.dot(ckv_ref[...], wuk_ref[...], preferred_element_type=F32)
    _store_v_transposed(vt_ref, jnp.dot(ckv_ref[...], wuv_ref[...], preferred_element_type=F32), C_HEADS)

    seg = seg_ref[...]

    def inv_rms(v):
        return lax.rsqrt(jnp.dot((v * v).astype(BF16), seg, preferred_element_type=F32) + EPS)

    kr = jnp.where(rope, misc_ref[...], 0.0)
    kr = _rotate(kr * inv_rms(kr) * gkr_ref[...], cos, sin, lane)

    for hd in range(C_HEADS):
        cols = slice(hd * HEAD_SLAB, (hd + 1) * HEAD_SLAB)
        qh = q[:, cols]
        qc_ref[:, cols] = (inv_rms(qh) * (qh * q_cos + q_partner[:, cols] * q_sin)).astype(BF16)
        kh = kn[:, cols]
        kc_ref[:, cols] = (kh * inv_rms(kh) * gkn_ref[...] + kr).astype(BF16)


def _mla_prep(cq, ckv, misc, cos, sin, seq, p):
    t = cq.shape[0]
    tm = ROW_TILE
    tps = seq // tm
    row = lambda w: pl.BlockSpec((tm, w), lambda i: (i, 0))
    consts = [p["wuq"], p["wuk"], p["wuv"], p["gq"], p["gkn"], p["gkr"], p["seg"]]
    qk_w = C_HEADS * HEAD_SLAB
    return pl.pallas_call(
        _mla_prep_kernel,
        grid=(t // tm,),
        in_specs=[row(C_Q_RANK), row(C_KV_RANK), row(LANES), row(LANES), row(LANES)]
                 + [_full(c.shape) for c in consts],
        out_specs=[row(qk_w), row(qk_w),
                   pl.BlockSpec((1, C_HEADS, VT_ROWS, tm), lambda i: (i // tps, 0, 0, i % tps))],
        out_shape=[jax.ShapeDtypeStruct((t, qk_w), BF16), jax.ShapeDtypeStruct((t, qk_w), BF16),
                   jax.ShapeDtypeStruct((t // seq, C_HEADS, VT_ROWS, seq), BF16)],
        compiler_params=_cparams("parallel"),
        name="mla_prep",
    )(cq, ckv, misc, cos, sin, *consts)


def _attn_items(nq):
    return [(i, j) for i in range(nq) for j in range(i + 1)]


def _attn_kernel(q_ref, k_ref, vt_ref, mask_ref, o_ref, s0_ref, s1_ref, p0_ref, p1_ref,
                 mp0_ref, mp1_ref, mrun_ref, macc_ref, acc_ref):
    tk, tq = mask_ref.shape
    nq = q_ref.shape[0] // tq
    items = _attn_items(nq)
    n_items = len(items)
    s_bufs, p_bufs, mp_bufs = (s0_ref, s1_ref), (p0_ref, p1_ref), (mp0_ref, mp1_ref)
    mrun_ref[...] = jnp.full(mrun_ref.shape, NEG_INF, F32)
    macc_ref[...] = jnp.full(macc_ref.shape, NEG_INF, F32)
    acc_ref[...] = jnp.zeros(acc_ref.shape, F32)

    def scores(it, buf):
        qi, kj = items[it]
        for hh in range(2):
            cols = slice(hh * HEAD_SLAB, (hh + 1) * HEAD_SLAB)
            s_t = lax.dot_general(k_ref[kj * tk:(kj + 1) * tk, cols], q_ref[qi * tq:(qi + 1) * tq, cols],
                                  (((1,), (1,)), ((), ())), preferred_element_type=F32)
            if qi == kj:
                s_t = s_t + mask_ref[...]
            s_bufs[buf][hh] = s_t
            mrun_ref[qi, hh] = jnp.maximum(mrun_ref[qi, hh], jnp.max(s_t, axis=0, keepdims=True))

    def exponentiate(it, buf):
        qi, _ = items[it]
        for hh in range(2):
            m = mrun_ref[qi, hh]
            p_bufs[buf][hh] = jnp.exp2(s_bufs[buf][hh] - m).astype(BF16)
            mp_bufs[buf][hh] = m

    def accumulate(it, buf):
        qi, kj = items[it]
        for hh in range(2):
            m = mp_bufs[buf][hh]
            pv = jnp.dot(vt_ref[0, hh, :, kj * tk:(kj + 1) * tk], p_bufs[buf][hh],
                         preferred_element_type=F32)
            acc_ref[qi, hh] = jnp.exp2(macc_ref[qi, hh] - m) * acc_ref[qi, hh] + pv
            macc_ref[qi, hh] = m

    for it in range(n_items + 2):
        par = it % 2
        if 2 <= it:
            accumulate(it - 2, par)
            qi, kj = items[it - 2]
            if qi == kj:
                halves = [acc_ref[qi, hh, 0:C_V_DIM, :] / acc_ref[qi, hh, C_V_DIM:C_V_DIM + 1, :]
                          for hh in range(2)]
                o_ref[qi * tq:(qi + 1) * tq, :] = jnp.concatenate(halves, axis=0).T.astype(o_ref.dtype)
        if 1 <= it <= n_items:
            exponentiate(it - 1, 1 - par)
        if it < n_items:
            scores(it, par)


def _attention(q, k, vt, *, unit, name):
    t = q.shape[0]
    nb, n_heads, _, seq = vt.shape
    tq = ATTN_TILE
    nq = seq // tq
    pos = np.arange(tq)
    diag_mask = np.where((pos[:, None] // unit) <= (pos[None, :] // unit), 0.0, NEG_INF)
    mask = jnp.asarray(diag_mask.astype(np.float32))
    seq_blk = lambda w: pl.BlockSpec((seq, w), lambda b, p: (b, p))
    return pl.pallas_call(
        _attn_kernel,
        grid=(nb, n_heads // 2),
        in_specs=[seq_blk(2 * HEAD_SLAB), seq_blk(2 * HEAD_SLAB),
                  pl.BlockSpec((1, 2, VT_ROWS, seq), lambda b, p: (b, p, 0, 0)),
                  pl.BlockSpec((tq, tq), lambda b, p: (0, 0), pipeline_mode=pl.Buffered(1))],
        out_specs=seq_blk(2 * C_V_DIM),
        out_shape=jax.ShapeDtypeStruct((t, n_heads * C_V_DIM), BF16),
        scratch_shapes=[pltpu.VMEM((2, tq, tq), F32), pltpu.VMEM((2, tq, tq), F32),
                        pltpu.VMEM((2, tq, tq), BF16), pltpu.VMEM((2, tq, tq), BF16),
                        pltpu.VMEM((2, 1, tq), F32), pltpu.VMEM((2, 1, tq), F32),
                        pltpu.VMEM((nq, 2, 1, tq), F32), pltpu.VMEM((nq, 2, 1, tq), F32),
                        pltpu.VMEM((nq, 2, VT_ROWS, tq), F32)],
        compiler_params=_cparams("parallel", "parallel"),
        name=name,
    )(q, k, vt, mask)


def _mem_kv_kernel(mem_ref, g_ref, w_ref, kg_ref, k_ref, v_ref):
    mn = (_rms(mem_ref[0]) * g_ref[...]).astype(BF16)
    kv = jnp.dot(mn, w_ref[...], preferred_element_type=F32)
    for hd in range(M_HEADS):
        cols = slice(hd * M_HEAD_DIM, (hd + 1) * M_HEAD_DIM)
        k_ref[0, :, cols] = (_rms(kv[:, cols]) * kg_ref[...]).astype(BF16)
    v_ref[0] = kv[:, M_W:].astype(BF16)


def _mem_kv(mem, p):
    nb, ml, _ = mem.shape
    consts = [p["mem_g"], p["w_mem_kv"], p["mkg"]]
    blk = pl.BlockSpec((1, ml, M_W), lambda b: (b, 0, 0))
    return pl.pallas_call(
        _mem_kv_kernel,
        grid=(nb,),
        in_specs=[pl.BlockSpec((1, ml, D_MODEL), lambda b: (b, 0, 0))] + [_full(c.shape) for c in consts],
        out_specs=[blk, blk],
        out_shape=[jax.ShapeDtypeStruct((nb, ml, M_W), BF16)] * 2,
        compiler_params=_cparams("parallel"),
        name="mem_kv",
    )(mem, *consts)


def _outproj_kernel(x_ref, a_ref, bn_ref, c_ref, ona_ref, onc_ref, wo_ref, xg_ref, wq_ref, mqg_ref,
                    km_ref, vm_ref, wmo_ref, o_ref):
    a_n = (_rms(a_ref[...].astype(F32)) * ona_ref[...]).astype(BF16)
    c_n = (_rms(c_ref[...].astype(F32)) * onc_ref[...]).astype(BF16)
    mix = jnp.concatenate([a_n, bn_ref[...], c_n], axis=-1)
    x1 = x_ref[...] + jnp.dot(mix, wo_ref[...], preferred_element_type=F32)

    h = (_rms(x1) * xg_ref[...]).astype(BF16)
    q = jnp.dot(h, wq_ref[...], preferred_element_type=F32)
    outs = []
    for hd in range(M_HEADS):
        cols = slice(hd * M_HEAD_DIM, (hd + 1) * M_HEAD_DIM)
        qh = (_rms(q[:, cols]) * mqg_ref[...]).astype(BF16)
        s = lax.dot_general(qh, km_ref[0, :, cols], (((1,), (1,)), ((), ())), preferred_element_type=F32)
        e = jnp.exp(s - jnp.max(s, axis=-1, keepdims=True))
        pr = e / jnp.sum(e, axis=-1, keepdims=True)
        outs.append(jnp.dot(pr.astype(BF16), vm_ref[0, :, cols], preferred_element_type=F32).astype(BF16))
    o_ref[...] = x1 + jnp.dot(jnp.concatenate(outs, axis=-1), wmo_ref[...], preferred_element_type=F32)


def _outproj(x2d, a, bn, c, km, vm, seq, p):
    t = x2d.shape[0]
    tm = ROW_TILE
    tps = seq // tm
    ml = km.shape[1]
    row = lambda w: pl.BlockSpec((tm, w), lambda i: (i, 0))
    memblk = pl.BlockSpec((1, ml, M_W), lambda i: (i // tps, 0, 0))
    c1 = [p["ona"], p["onc"], p["w_out"], p["xg"], p["w_mem_q"], p["mqg"]]
    return pl.pallas_call(
        _outproj_kernel,
        grid=(t // tm,),
        in_specs=[row(D_MODEL), row(A_W), row(B_W), row(C_W)] + [_full(c_.shape) for c_ in c1]
                 + [memblk, memblk, _full(p["w_mem_out"].shape)],
        out_specs=row(D_MODEL),
        out_shape=jax.ShapeDtypeStruct((t, D_MODEL), F32),
        compiler_params=_cparams("parallel"),
        name="out_proj_mem_attn",
    )(x2d, a, bn, c, *c1, km, vm, p["w_mem_out"])


def _silu(x):
    return x * jax.nn.sigmoid(x)


def _ffn_kernel(x_ref, g_ref, wg_ref, wu_ref, wd_ref, o_ref, *, n_chunks):
    x = x_ref[...]
    h = (_rms(x) * g_ref[...]).astype(BF16)
    fc = wg_ref.shape[1] // n_chunks
    acc = x
    for c in range(n_chunks):
        cols = slice(c * fc, (c + 1) * fc)
        act = _silu(jnp.dot(h, wg_ref[:, cols], preferred_element_type=F32)) * \
            jnp.dot(h, wu_ref[:, cols], preferred_element_type=F32)
        acc = acc + jnp.dot(act.astype(BF16), wd_ref[cols, :], preferred_element_type=F32)
    o_ref[...] = acc


def _ffn(x2d, g, wg, wu, wd):
    t = x2d.shape[0]
    tm = ROW_TILE
    row = pl.BlockSpec((tm, D_MODEL), lambda i: (i, 0))
    resident = lambda a: pl.BlockSpec(a.shape, lambda i: (0, 0), pipeline_mode=pl.Buffered(1))
    return pl.pallas_call(
        functools.partial(_ffn_kernel, n_chunks=2),
        grid=(t // tm,),
        in_specs=[row, _full(g.shape), resident(wg), resident(wu), resident(wd)],
        out_specs=row,
        out_shape=jax.ShapeDtypeStruct((t, D_MODEL), F32),
        compiler_params=_cparams("parallel"),
        name="ffn_dense",
    )(x2d, g, wg, wu, wd)


def _pack_bf16_pairs(lo, hi):
    lo_bits = pltpu.bitcast(lo.astype(BF16).astype(F32), jnp.uint32)
    hi_bits = pltpu.bitcast(hi.astype(BF16).astype(F32), jnp.uint32)
    return lax.shift_right_logical(lo_bits, jnp.uint32(16)) | (hi_bits & jnp.uint32(0xFFFF0000))


def _unpack_bf16_pairs(words):
    lo = pltpu.bitcast(lax.shift_left(words, jnp.uint32(16)), F32)
    hi = pltpu.bitcast(words & jnp.uint32(0xFFFF0000), F32)
    return lo, hi


def _pack_rows(v):
    q = D_MODEL // 4
    return _pack_bf16_pairs(v[:, 0:q], v[:, q:2 * q]), _pack_bf16_pairs(v[:, 2 * q:3 * q], v[:, 3 * q:])


def _unpack_rows(a, b):
    return jnp.concatenate([*_unpack_bf16_pairs(a), *_unpack_bf16_pairs(b)], axis=-1)


def _route_kernel(x_ref, g_ref, wr_ref, br_ref, tri_ref, ha_ref, hb_ref, route_ref, cnt_ref):
    lane = _lane_iota()
    h = _rms(x_ref[...]) * g_ref[...]
    ha_ref[...], hb_ref[...] = _pack_rows(h)
    h_hi = h.astype(BF16)
    h_lo = (h - h_hi.astype(F32)).astype(BF16)
    logits = (jnp.dot(h_hi, wr_ref[0], preferred_element_type=F32)
              + jnp.dot(h_lo, wr_ref[0], preferred_element_type=F32)
              + jnp.dot(h_hi, wr_ref[1], preferred_element_type=F32)) + br_ref[...]
    logits = jnp.where(lane < N_EXPERTS, logits, -jnp.inf)
    v1 = jnp.max(logits, axis=-1, keepdims=True)
    i1 = jnp.min(jnp.where(logits == v1, lane, LANES), axis=-1, keepdims=True)
    rest = jnp.where(lane == i1, -jnp.inf, logits)
    v2 = jnp.max(rest, axis=-1, keepdims=True)
    i2 = jnp.min(jnp.where(rest == v2, lane, LANES), axis=-1, keepdims=True)
    e2 = jnp.exp(v2 - v1)
    g1 = 1.0 / (1.0 + e2)
    hit1, hit2 = lane == i1, lane == i2
    ones = jnp.where(hit1 | hit2, 1.0, 0.0)
    before = jnp.dot(tri_ref[...], ones.astype(BF16), preferred_element_type=F32)
    r1 = jnp.sum(jnp.where(hit1, before, 0.0), axis=-1, keepdims=True)
    r2 = jnp.sum(jnp.where(hit2, before, 0.0), axis=-1, keepdims=True)
    cols = [i1.astype(F32), i2.astype(F32), g1, e2 * g1, r1, r2]
    route = jnp.zeros(route_ref.shape, F32)
    for n, c in enumerate(cols):
        route = jnp.where(lane == n, c, route)
    route_ref[...] = route
    cnt_ref[0] = jnp.broadcast_to(jnp.sum(ones, axis=0, keepdims=True), cnt_ref.shape[1:])


def _route(x2d, g, wr, br):
    t = x2d.shape[0]
    tm = MOE_CHUNK
    tri = jnp.asarray(np.tril(np.ones((tm, tm), np.float32), -1)).astype(BF16)
    row = lambda w: pl.BlockSpec((tm, w), lambda i: (i, 0))
    q = D_MODEL // 4
    return pl.pallas_call(
        _route_kernel,
        grid=(t // tm,),
        in_specs=[row(D_MODEL), _full(g.shape), _full(wr.shape), _full(br.shape), _full(tri.shape)],
        out_specs=[row(q), row(q), row(LANES), pl.BlockSpec((1, 8, LANES), lambda i: (i, 0, 0))],
        out_shape=[jax.ShapeDtypeStruct((t, q), jnp.uint32), jax.ShapeDtypeStruct((t, q), jnp.uint32),
                   jax.ShapeDtypeStruct((t, LANES), F32), jax.ShapeDtypeStruct((t // tm, 8, LANES), F32)],
        compiler_params=_cparams("parallel"),
        name="moe_route",
    )(x2d, g, wr, br, tri)


def _dest_kernel(route_ref, base_ref, o_ref):
    lane = _lane_iota()
    r = route_ref[...]
    base = base_ref[0, 0:1, :]
    lane_f = lane.astype(F32)
    d1 = jnp.sum(jnp.where(lane_f == r[:, 0:1], base, 0.0), axis=-1, keepdims=True) + r[:, 4:5]
    d2 = jnp.sum(jnp.where(lane_f == r[:, 1:2], base, 0.0), axis=-1, keepdims=True) + r[:, 5:6]
    both = jnp.where(lane == 0, d1, jnp.where(lane == 1, d2, 0.0))
    o_ref[...] = both.T[0:8, :].astype(jnp.int32)


def _destinations(route, base):
    t = route.shape[0]
    tm = MOE_CHUNK
    out = pl.pallas_call(
        _dest_kernel,
        grid=(t // tm,),
        in_specs=[pl.BlockSpec((tm, LANES), lambda i: (i, 0)), pl.BlockSpec((1, 8, LANES), lambda i: (i, 0, 0))],
        out_specs=pl.BlockSpec((8, tm), lambda i: (0, i)),
        out_shape=jax.ShapeDtypeStruct((8, t), jnp.int32),
        compiler_params=_cparams("parallel"),
        name="moe_dest",
    )(route, base)
    return out[0:2].reshape(1, 2 * t)


SC_WINDOW = 128


def _sc_mesh():
    return plsc.VectorSubcoreMesh(core_axis_name="core", subcore_axis_name="subcore")


def _sc_scatter_rows(x, idx, n_out):
    n, d = x.shape
    m = idx.shape[1]
    nblk = n // SC_WINDOW

    @pl.kernel(out_type=jax.ShapeDtypeStruct((n_out, d), x.dtype), mesh=_sc_mesh(), name="moe_sc_scatter")
    def scatter(x_hbm, i_hbm, o_hbm):
        def body(x_vmem, i_vmem):
            pltpu.sync_copy(x_vmem, o_hbm.at[i_vmem.at[0]])

        half = m // SC_WINDOW // 2
        pltpu.emit_pipeline(
            body,
            grid=(2, half),
            in_specs=[pl.BlockSpec((SC_WINDOW, d), lambda c, j: ((c * half + j) % nblk, 0)),
                      pl.BlockSpec((1, SC_WINDOW), lambda c, j: (0, c * half + j))],
            out_specs=[],
            core_axis_name=("core", "subcore"),
            dimension_semantics=(pltpu.PARALLEL, pltpu.PARALLEL),
        )(x_hbm, i_hbm)

    return scatter(x, idx)


def _sc_gather_rows(table, idx):
    d = table.shape[1]
    m = idx.shape[1]

    @pl.kernel(out_type=jax.ShapeDtypeStruct((m, d), table.dtype), mesh=_sc_mesh(), name="moe_sc_gather")
    def gather(t_hbm, i_hbm, o_hbm):
        def body(i_vmem, o_vmem):
            pltpu.sync_copy(t_hbm.at[i_vmem.at[0]], o_vmem)

        half = m // SC_WINDOW // 2
        pltpu.emit_pipeline(
            body,
            grid=(2, half),
            in_specs=[pl.BlockSpec((1, SC_WINDOW), lambda c, j: (0, c * half + j))],
            out_specs=[pl.BlockSpec((SC_WINDOW, d), lambda c, j: (c * half + j, 0))],
            core_axis_name=("core", "subcore"),
            dimension_semantics=(pltpu.PARALLEL, pltpu.PARALLEL),
        )(i_hbm, o_hbm)

    return gather(table, idx)


def _expert_kernel(blk_expert_ref, n_used_ref, xa_ref, xb_ref, wg_ref, wu_ref, wd_ref, ya_ref, yb_ref):
    del blk_expert_ref

    @pl.when(pl.program_id(0) < n_used_ref[0])
    def _():
        xe = _unpack_rows(xa_ref[...], xb_ref[...]).astype(BF16)
        act = _silu(jnp.dot(xe, wg_ref[0], preferred_element_type=F32)) * \
            jnp.dot(xe, wu_ref[0], preferred_element_type=F32)
        y = jnp.dot(act.astype(BF16), wd_ref[0], preferred_element_type=F32)
        ya_ref[...], yb_ref[...] = _pack_rows(y)

    @pl.when(pl.program_id(0) >= n_used_ref[0])
    def _():
        ya_ref[...] = jnp.zeros(ya_ref.shape, ya_ref.dtype)
        yb_ref[...] = jnp.zeros(yb_ref.shape, yb_ref.dtype)


def _experts(blk_expert, n_used, xa, xb, wg, wu, wd):
    n_rows, q = xa.shape
    ff = wg.shape[2]
    blk = MOE_BLOCK
    row = pl.BlockSpec((blk, q), lambda b, be, nu: (b, 0))
    grid_spec = pltpu.PrefetchScalarGridSpec(
        num_scalar_prefetch=2,
        grid=(n_rows // blk,),
        in_specs=[row, row,
                  pl.BlockSpec((1, D_MODEL, ff), lambda b, be, nu: (be[b], 0, 0)),
                  pl.BlockSpec((1, D_MODEL, ff), lambda b, be, nu: (be[b], 0, 0)),
                  pl.BlockSpec((1, ff, D_MODEL), lambda b, be, nu: (be[b], 0, 0))],
        out_specs=[row, row])
    return pl.pallas_call(
        _expert_kernel,
        grid_spec=grid_spec,
        out_shape=[jax.ShapeDtypeStruct((n_rows, q), jnp.uint32)] * 2,
        compiler_params=_cparams("arbitrary"),
        name="moe_experts",
    )(blk_expert, n_used, xa, xb, wg, wu, wd)


def _combine_kernel(x_ref, route_ref, a1_ref, b1_ref, a2_ref, b2_ref, o_ref):
    g1 = route_ref[:, 2:3]
    g2 = route_ref[:, 3:4]
    o_ref[...] = x_ref[...] + g1 * _unpack_rows(a1_ref[...], b1_ref[...]) \
        + g2 * _unpack_rows(a2_ref[...], b2_ref[...])


def _combine(x2d, route, ya, yb):
    t = x2d.shape[0]
    tm = ROW_TILE
    nt = t // tm
    q = ya.shape[1]
    row = lambda w: pl.BlockSpec((tm, w), lambda i: (i, 0))
    first = pl.BlockSpec((tm, q), lambda i: (i, 0))
    second = pl.BlockSpec((tm, q), lambda i: (nt + i, 0))
    return pl.pallas_call(
        _combine_kernel,
        grid=(nt,),
        in_specs=[row(D_MODEL), row(LANES), first, first, second, second],
        out_specs=row(D_MODEL),
        out_shape=jax.ShapeDtypeStruct((t, D_MODEL), F32),
        compiler_params=_cparams("parallel"),
        name="moe_combine",
    )(x2d, route, ya, yb, ya, yb)


def _moe_sorted(x2d, g, wr, br, wg, wu, wd):
    t = x2d.shape[0]
    blk = MOE_BLOCK
    n_blocks = 2 * t // blk + N_EXPERTS
    ha, hb, route, cnt = _route(x2d, g, wr, br)

    cnt = cnt[:, 0, :N_EXPERTS].astype(jnp.int32)
    before_chunk = jnp.cumsum(cnt, axis=0) - cnt
    seg_blocks = (jnp.sum(cnt, axis=0) + blk - 1) // blk
    seg_end_blk = jnp.cumsum(seg_blocks)
    seg_start = (seg_end_blk - seg_blocks) * blk
    base = jnp.pad((seg_start[None, :] + before_chunk).astype(F32), ((0, 0), (0, LANES - N_EXPERTS)))
    idx = _destinations(route, jnp.broadcast_to(base[:, None, :], (base.shape[0], 8, LANES)))
    past_end = jnp.arange(n_blocks, dtype=jnp.int32)[:, None] >= seg_end_blk[None, :]
    blk_expert = jnp.minimum(jnp.sum(past_end, axis=1), N_EXPERTS - 1).astype(jnp.int32)
    n_used = seg_end_blk[-1:].astype(jnp.int32)

    xa = _sc_scatter_rows(ha, idx, n_blocks * blk)
    xb = _sc_scatter_rows(hb, idx, n_blocks * blk)
    ya, yb = _experts(blk_expert, n_used, xa, xb, wg, wu, wd)
    return _combine(x2d, route, _sc_gather_rows(ya, idx), _sc_gather_rows(yb, idx))


def _pad_cols(w, width):
    return jnp.pad(w, ((0, 0), (0, width - w.shape[1])))


def _layer_params(l, mix_norm, w_in, b_forget, a_q_norm, a_k_norm, b_v_norm, b_spatial_w, b_spatial_b,
                  c_q_lat_norm, c_w_uq, c_kv_lat_norm, c_w_ukv, c_q_nope_norm, c_q_rope_norm,
                  c_k_nope_norm, c_k_rope_norm, out_norm_a, out_norm_b, out_norm_c, w_out,
                  xattn_norm, mem_norm, w_mem_q, w_mem_kv, m_q_norm, m_k_norm, w_mem_out):
    p = {}
    o = np.cumsum((0, A_W, A_W, A_W, A_HEADS, B_W, B_W, C_Q_RANK, C_KV_RANK, C_ROPE_DIM))
    w = w_in[l]
    seg = lambda n: w[:, o[n]:o[n + 1]]
    fa = seg(3)
    misc = jnp.zeros((D_MODEL, LANES), F32)
    misc = misc.at[:, ROPE_LANE:ROPE_LANE + C_ROPE_DIM].set(seg(8))
    fb = jnp.zeros((1, LANES), F32)
    for hd in range(A_HEADS):
        ln = FORGET_LANE + 8 * (hd // 2) + hd % 2
        misc = misc.at[:, ln].set(fa[:, hd])
        fb = fb.at[0, ln].set(b_forget[l, hd])
    p["w_in"] = jnp.concatenate([seg(0), seg(1), seg(2), seg(4), seg(5), seg(6), seg(7), misc], axis=1).astype(BF16)
    p["fb"] = fb
    p["mix_g"] = mix_norm[l][None]
    p["aq"] = jnp.tile(a_q_norm[l], A_HEADS)[None] * (A_HEAD_DIM ** -0.5 * LOG2E)
    p["ak"] = jnp.tile(a_k_norm[l], A_HEADS)[None]
    p["bvg"] = b_v_norm[l][None]
    pos = np.arange(B_WINDOW)
    mask = (pos[None, :] // CHUNK) <= (pos[:, None] // CHUNK)
    p["ws"] = jnp.where(mask[None], b_spatial_w[l], 0.0).reshape(B_GROUPS * B_WINDOW, B_WINDOW).astype(BF16)
    p["bs"] = jnp.repeat(b_spatial_b[l].T, B_GROUP_DIM, axis=1)
    p["onb"] = out_norm_b[l][None]
    p["cqg"] = c_q_lat_norm[l][None]
    p["ckvg"] = c_kv_lat_norm[l][None]
    gidx = np.arange(A_W) // A_HEAD_DIM
    p["gm"] = jnp.asarray((gidx[:, None] == gidx[None, :]).astype(np.float32) / A_HEAD_DIM).astype(BF16)
    p["tri"] = jnp.asarray(np.tril(np.ones((ROW_TILE, ROW_TILE), np.float32))).astype(BF16)

    qd = C_NOPE_DIM + C_ROPE_DIM
    half = C_ROPE_DIM // 2
    wq = c_w_uq[l]
    wq_partner = jnp.concatenate([jnp.zeros_like(wq[:, :, :C_NOPE_DIM]), wq[:, :, C_NOPE_DIM + half:],
                                  wq[:, :, C_NOPE_DIM:C_NOPE_DIM + half]], axis=-1)
    p["wuq"] = jnp.pad(jnp.stack([wq, wq_partner]), ((0, 0), (0, 0), (0, 0), (0, HEAD_SLAB - qd))
                       ).reshape(2, C_Q_RANK, -1).astype(BF16)
    wukv = c_w_ukv[l]
    p["wuk"] = jnp.pad(wukv[:, :, :C_NOPE_DIM], ((0, 0), (0, 0), (0, HEAD_SLAB - C_NOPE_DIM))
                       ).reshape(C_KV_RANK, -1).astype(BF16)
    p["wuv"] = wukv[:, :, C_NOPE_DIM:].reshape(C_KV_RANK, C_W).astype(BF16)
    gq = jnp.concatenate([c_q_nope_norm[l], c_q_rope_norm[l]])
    gq_partner = jnp.concatenate([jnp.zeros_like(c_q_nope_norm[l]), c_q_rope_norm[l][half:],
                                  c_q_rope_norm[l][:half]])
    p["gq"] = _pad_cols(jnp.stack([gq, gq_partner]) * (qd ** -0.5 * LOG2E), LANES)
    p["gkn"] = _pad_cols(c_k_nope_norm[l][None], LANES)
    seg = np.zeros((LANES, LANES), np.float32)
    seg[:C_NOPE_DIM, :C_NOPE_DIM] = 1.0 / C_NOPE_DIM
    seg[ROPE_LANE:ROPE_LANE + C_ROPE_DIM, ROPE_LANE:ROPE_LANE + C_ROPE_DIM] = 1.0 / C_ROPE_DIM
    p["seg"] = jnp.asarray(seg).astype(BF16)
    p["gkr"] = jnp.zeros((1, LANES), F32).at[0, ROPE_LANE:ROPE_LANE + C_ROPE_DIM].set(c_k_rope_norm[l])

    p["ona"] = out_norm_a[l][None]
    p["onc"] = out_norm_c[l][None]
    p["w_out"] = w_out[l].astype(BF16)
    p["xg"] = xattn_norm[l][None]
    p["w_mem_q"] = w_mem_q[l].astype(BF16)
    p["mqg"] = m_q_norm[l][None] * (M_HEAD_DIM ** -0.5)
    p["mem_g"] = mem_norm[l][None]
    p["w_mem_kv"] = w_mem_kv[l].astype(BF16)
    p["mkg"] = m_k_norm[l][None]
    p["w_mem_out"] = w_mem_out[l].astype(BF16)
    return p


def kernel(x, mem, positions, mix_norm, w_in, b_forget, a_q_norm, a_k_norm, b_v_norm, b_spatial_w, b_spatial_b, c_q_lat_norm, c_w_uq, c_kv_lat_norm, c_w_ukv, c_q_nope_norm, c_q_rope_norm, c_k_nope_norm, c_k_rope_norm, out_norm_a, out_norm_b, out_norm_c, w_out, xattn_norm, mem_norm, w_mem_q, w_mem_kv, m_q_norm, m_k_norm, w_mem_out, ffn_norm, ffn_w_gate, ffn_w_up, ffn_w_down, w_router, b_router, moe_w_gate, moe_w_up, moe_w_down):
    nb, seq, d = x.shape
    assert d == D_MODEL and seq % ROW_TILE == 0 and seq % ATTN_TILE == 0 and (nb * seq) % MOE_CHUNK == 0
    depth = w_in.shape[0]
    t = nb * seq
    x2d = x.reshape(t, d)
    cos, sin = _rope_tables(positions.reshape(t, 1).astype(F32))

    for l in range(depth):
        p = _layer_params(l, mix_norm, w_in, b_forget, a_q_norm, a_k_norm, b_v_norm, b_spatial_w,
                          b_spatial_b, c_q_lat_norm, c_w_uq, c_kv_lat_norm, c_w_ukv, c_q_nope_norm,
                          c_q_rope_norm, c_k_nope_norm, c_k_rope_norm, out_norm_a, out_norm_b,
                          out_norm_c, w_out, xattn_norm, mem_norm, w_mem_q, w_mem_kv, m_q_norm,
                          m_k_norm, w_mem_out)
        qa, ka, vta, bn, cq, ckv, misc = _inproj(x2d, seq, p)
        qc, kc, vtc = _mla_prep(cq, ckv, misc, cos, sin, seq, p)
        a = _attention(qa, ka, vta, unit=1, name="attn_fox")
        c = _attention(qc, kc, vtc, unit=CHUNK, name="attn_mla")
        km, vm = _mem_kv(mem, p)
        x2d = _outproj(x2d, a, bn, c, km, vm, seq, p)
        g = ffn_norm[l][None]
        if l % 2 == 0:
            m = l // 2
            ff = ffn_w_gate.shape[2]
            ff_pad = -(-ff // (2 * LANES)) * (2 * LANES)
            wg = _pad_cols(ffn_w_gate[m], ff_pad).astype(BF16)
            wu = _pad_cols(ffn_w_up[m], ff_pad).astype(BF16)
            wd = jnp.pad(ffn_w_down[m], ((0, ff_pad - ff), (0, 0))).astype(BF16)
            x2d = _ffn(x2d, g, wg, wu, wd)
        else:
            m = l // 2
            wr = _pad_cols(w_router[m], LANES)
            wr_hi = wr.astype(BF16)
            wr = jnp.stack([wr_hi, (wr - wr_hi.astype(F32)).astype(BF16)])
            br = _pad_cols(b_router[m][None], LANES)
            x2d = _moe_sorted(x2d, g, wr, br, moe_w_gate[m].astype(BF16), moe_w_up[m].astype(BF16),
                              moe_w_down[m].astype(BF16))
    return x2d.reshape(nb, seq, d)
```

```python
import functools

import numpy as np
import jax
import jax.numpy as jnp
from jax import lax
from jax.experimental import pallas as pl
from jax.experimental.pallas import tpu as pltpu
from jax.experimental.pallas import tpu_sc as plsc

F32 = jnp.float32
BF16 = jnp.bfloat16
HIGHEST = lax.Precision.HIGHEST

D_MODEL = 1024
CHUNK = 64
EPS = 1e-6
NEG_INF = -1e30
A_HEADS, A_HEAD_DIM = 4, 64
B_GROUPS, B_GROUP_DIM, B_WINDOW = 4, 64, 128
C_HEADS, C_NOPE_DIM, C_ROPE_DIM, C_V_DIM = 8, 64, 32, 64
C_Q_RANK, C_KV_RANK = 256, 128
ROPE_THETA = 10000.0
M_HEADS, M_HEAD_DIM = 4, 128
N_EXPERTS = 8
A_W = A_HEADS * A_HEAD_DIM
B_W = B_GROUPS * B_GROUP_DIM
C_W = C_HEADS * C_V_DIM
M_W = M_HEADS * M_HEAD_DIM

LANES = 128

SEG_Q, SEG_K, SEG_V, SEG_U, SEG_VB, SEG_CQ, SEG_CKV, SEG_MISC = 0, 256, 512, 768, 1024, 1280, 1536, 1664
IN_PAD_W = SEG_MISC + LANES
ROPE_LANE = C_NOPE_DIM
FORGET_LANE = 96
HEAD_SLAB = LANES
VT_ROWS = 80
LOG2E = float(np.log2(np.e))

ROW_TILE = 512
ATTN_TILE = 512
MOE_CHUNK = 1024
MOE_BLOCK = 512
VMEM_LIMIT = 56 * 1024 * 1024


def _cparams(*sem):
    return pltpu.CompilerParams(dimension_semantics=sem, vmem_limit_bytes=VMEM_LIMIT)


def _full(shape):
    n = len(shape)
    return pl.BlockSpec(shape, lambda *_: (0,) * n)


def _rms(x):
    return x * lax.rsqrt(jnp.mean(x * x, axis=-1, keepdims=True) + EPS)


def _dot_split(v, exact, pieces, lhs_is_exact=False):
    total = None
    rem = v
    for n in range(pieces):
        part = rem.astype(BF16)
        if n + 1 < pieces:
            rem = rem - part.astype(F32)
        term = (jnp.dot(exact, part, preferred_element_type=F32) if lhs_is_exact
                else jnp.dot(part, exact, preferred_element_type=F32))
        total = term if total is None else total + term
    return total


def _lane_iota(n=LANES):
    return lax.broadcasted_iota(jnp.int32, (1, n), 1)


def _rope_table_kernel(pos_ref, inv_ref, sgn_ref, cos_ref, sin_ref):
    ang = pos_ref[...] * inv_ref[...]
    cos_ref[...] = jnp.cos(ang)
    sin_ref[...] = jnp.sin(ang) * sgn_ref[...]


def _rope_tables(pos_col):
    t = pos_col.shape[0]
    half = C_ROPE_DIM // 2
    inv = ROPE_THETA ** (-jnp.arange(half, dtype=F32) / half)
    inv_l = jnp.zeros((1, LANES), F32).at[0, ROPE_LANE:ROPE_LANE + C_ROPE_DIM].set(jnp.tile(inv, 2))
    sgn = np.zeros((1, LANES), np.float32)
    sgn[0, ROPE_LANE:ROPE_LANE + half] = -1.0
    sgn[0, ROPE_LANE + half:ROPE_LANE + C_ROPE_DIM] = 1.0
    tm = ROW_TILE
    return pl.pallas_call(
        _rope_table_kernel,
        grid=(t // tm,),
        in_specs=[pl.BlockSpec((tm, 1), lambda i: (i, 0)), _full((1, LANES)), _full((1, LANES))],
        out_specs=[pl.BlockSpec((tm, LANES), lambda i: (i, 0))] * 2,
        out_shape=[jax.ShapeDtypeStruct((t, LANES), F32)] * 2,
        compiler_params=_cparams("parallel"),
        name="rope_tables",
    )(pos_col, inv_l, jnp.asarray(sgn))


def _rotate(x, cos, sin_signed, lane):
    half = C_ROPE_DIM // 2
    partner = jnp.where(lane < ROPE_LANE + half,
                        pltpu.roll(x, LANES - half, 1), pltpu.roll(x, half, 1))
    return x * cos + partner * sin_signed


def _store_v_transposed(vt_ref, v, n_heads, lanes=slice(None)):
    tm = v.shape[0]
    v_t = v.T
    tail = jnp.where(lax.broadcasted_iota(jnp.int32, (VT_ROWS - C_V_DIM, tm), 0) == 0, 1.0, 0.0).astype(BF16)
    for hd in range(n_heads):
        vt_ref[0, hd, 0:C_V_DIM, lanes] = v_t[hd * C_V_DIM:(hd + 1) * C_V_DIM, :].astype(BF16)
        vt_ref[0, hd, C_V_DIM:VT_ROWS, lanes] = tail


def _gelu(x):
    return 0.5 * x * (1.0 + lax.erf(x * np.float32(1.0 / np.sqrt(2.0))))


def _inproj_kernel(x0_ref, xa_ref, xb_ref, g_ref, w_ref, aq_ref, ak_ref, fb_ref, bvg_ref, ws_ref, bs_ref,
                   onb_ref, cqg_ref, ckvg_ref, gm_ref, tri_ref,
                   qa_ref, ka_ref, vt_ref, bn_ref, cq_ref, ckv_ref, misc_ref,
                   carry_ref, buf0_ref, buf1_ref, *, tiles_per_seq):
    i = pl.program_id(0)
    tm = xa_ref.shape[0]

    def projection_parts(x_ref, buf_ref):
        h = (_rms(x_ref[...]) * g_ref[...]).astype(BF16)

        def part(lo, hi):
            def run():
                buf_ref[:, lo:hi] = jnp.dot(h, w_ref[:, lo:hi], preferred_element_type=F32)
            return run
        return [part(SEG_Q, SEG_U), part(SEG_U, SEG_CQ), part(SEG_CQ, IN_PAD_W)]

    @pl.when(i == 0)
    def _():
        for run in projection_parts(x0_ref, buf0_ref):
            run()

    refs = (aq_ref, ak_ref, fb_ref, bvg_ref, ws_ref, bs_ref, onb_ref, cqg_ref, ckvg_ref, gm_ref, tri_ref,
            qa_ref, ka_ref, vt_ref, bn_ref, cq_ref, ckv_ref, misc_ref, carry_ref)
    _mixer_prologues(buf0_ref, 0, 2 * i, tm, tiles_per_seq, projection_parts(xa_ref, buf1_ref), *refs)
    _mixer_prologues(buf1_ref, 1, 2 * i + 1, tm, tiles_per_seq, projection_parts(xb_ref, buf0_ref), *refs)


def _mixer_prologues(proj, half, tile, tm, tiles_per_seq, between, aq_ref, ak_ref, fb_ref, bvg_ref, ws_ref,
                     bs_ref, onb_ref, cqg_ref, ckvg_ref, gm_ref, tri_ref, qa_ref, ka_ref, vt_ref, bn_ref,
                     cq_ref, ckv_ref, misc_ref, carry_ref):
    out_rows = slice(half * tm, (half + 1) * tm)
    gm = gm_ref[...]

    def group_mean(v):
        return _dot_split(v, gm, 2)

    misc = proj[:, SEG_MISC:SEG_MISC + LANES]
    misc_ref[out_rows, :] = misc
    z = misc + fb_ref[...]
    log_f = jnp.minimum(z, 0.0) - jnp.log1p(jnp.exp(-jnp.abs(z)))
    carry = jnp.where(tile % tiles_per_seq == 0, 0.0, carry_ref[...])
    cum = _dot_split(log_f, tri_ref[...], 3, lhs_is_exact=True) + carry
    carry_ref[...] = cum[tm - 1:tm, :]
    f_hi = (cum * LOG2E).astype(BF16).astype(F32)
    f_rem = cum * LOG2E - f_hi
    f_mid = f_rem.astype(BF16).astype(F32)
    f_lo = f_rem - f_mid

    between[0]()
    q = proj[:, SEG_Q:SEG_Q + A_W]
    qn = q * lax.rsqrt(group_mean(q * q) + EPS) * aq_ref[...]
    k = proj[:, SEG_K:SEG_K + A_W]
    kn = k * lax.rsqrt(group_mean(k * k) + EPS) * ak_ref[...]
    lane = _lane_iota()
    for hd in range(A_HEADS):
        pair = slice((hd // 2) * LANES, (hd // 2 + 1) * LANES)
        slab = slice(hd * HEAD_SLAB, (hd + 1) * HEAD_SLAB)
        data = (lane < A_HEAD_DIM) if hd % 2 == 0 else (lane >= A_HEAD_DIM)
        e0 = A_HEAD_DIM if hd % 2 == 0 else 0
        fl = FORGET_LANE + 8 * (hd // 2) + hd % 2
        ones = jnp.where((lane >= e0) & (lane < e0 + 3), 1.0, 0.0)
        qa_ref[out_rows, slab] = jnp.where(data, qn[:, pair], ones).astype(BF16)
        bias = jnp.where(lane == e0, -f_hi[:, fl:fl + 1],
                         jnp.where(lane == e0 + 1, -f_mid[:, fl:fl + 1],
                                   jnp.where(lane == e0 + 2, -f_lo[:, fl:fl + 1], 0.0)))
        ka_ref[out_rows, slab] = jnp.where(data, kn[:, pair], bias).astype(BF16)
    _store_v_transposed(vt_ref, proj[:, SEG_V:SEG_V + A_W], A_HEADS, out_rows)

    between[1]()
    u = _gelu(proj[:, SEG_U:SEG_U + B_W])
    v = _gelu(proj[:, SEG_VB:SEG_VB + B_W])
    dv = v - group_mean(v)
    vn = dv * lax.rsqrt(group_mean(dv * dv) + EPS) * bvg_ref[...]
    group = lax.broadcasted_iota(jnp.int32, (1, B_W), 1) // B_GROUP_DIM
    for w in range(tm // B_WINDOW):
        rows = slice(w * B_WINDOW, (w + 1) * B_WINDOW)
        y_all = jnp.dot(ws_ref[...], vn[rows].astype(BF16), preferred_element_type=F32)
        y = bs_ref[...]
        for g in range(B_GROUPS):
            y = y + jnp.where(group == g, y_all[g * B_WINDOW:(g + 1) * B_WINDOW], 0.0)
        b = u[rows] * y
        bn_ref[half * tm + w * B_WINDOW:half * tm + (w + 1) * B_WINDOW, :] = (_rms(b) * onb_ref[...]).astype(BF16)

    between[2]()
    cq_ref[out_rows, :] = (_rms(proj[:, SEG_CQ:SEG_CQ + C_Q_RANK]) * cqg_ref[...]).astype(BF16)
    ckv_ref[out_rows, :] = (_rms(proj[:, SEG_CKV:SEG_CKV + C_KV_RANK]) * ckvg_ref[...]).astype(BF16)


def _inproj(x2d, seq, p):
    t = x2d.shape[0]
    tm = ROW_TILE
    tps = seq // tm
    nb = t // seq
    n_tiles = t // tm
    steps_per_seq = tps // 2
    row = lambda w: pl.BlockSpec((2 * tm, w), lambda i: (i, 0))
    x_tile = lambda index: pl.BlockSpec((tm, D_MODEL), lambda i: (index(i), 0))
    qk_w = A_HEADS * HEAD_SLAB
    out_shape = [
        jax.ShapeDtypeStruct((t, qk_w), BF16), jax.ShapeDtypeStruct((t, qk_w), BF16),
        jax.ShapeDtypeStruct((nb, A_HEADS, VT_ROWS, seq), BF16),
        jax.ShapeDtypeStruct((t, B_W), BF16),
        jax.ShapeDtypeStruct((t, C_Q_RANK), BF16), jax.ShapeDtypeStruct((t, C_KV_RANK), BF16),
        jax.ShapeDtypeStruct((t, LANES), F32),
    ]
    out_specs = [row(qk_w), row(qk_w),
                 pl.BlockSpec((1, A_HEADS, VT_ROWS, 2 * tm),
                              lambda i: (i // steps_per_seq, 0, 0, i % steps_per_seq)),
                 row(B_W), row(C_Q_RANK), row(C_KV_RANK), row(LANES)]
    consts = [p["mix_g"], p["w_in"], p["aq"], p["ak"], p["fb"], p["bvg"], p["ws"], p["bs"], p["onb"],
              p["cqg"], p["ckvg"], p["gm"], p["tri"]]
    return pl.pallas_call(
        functools.partial(_inproj_kernel, tiles_per_seq=tps),
        grid=(n_tiles // 2,),
        in_specs=[x_tile(lambda i: 0), x_tile(lambda i: 2 * i + 1),
                  x_tile(lambda i: jnp.minimum(2 * i + 2, n_tiles - 1))] + [_full(c.shape) for c in consts],
        out_specs=out_specs,
        out_shape=out_shape,
        scratch_shapes=[pltpu.VMEM((1, LANES), F32), pltpu.VMEM((tm, IN_PAD_W), F32),
                        pltpu.VMEM((tm, IN_PAD_W), F32)],
        compiler_params=_cparams("arbitrary"),
        name="in_proj",
    )(x2d, x2d, x2d, *consts)


def _mla_prep_kernel(cq_ref, ckv_ref, misc_ref, cos_ref, sin_ref, wuq_ref, wuk_ref, wuv_ref,
                     gq_ref, gkn_ref, gkr_ref, seg_ref, qc_ref, kc_ref, vt_ref):
    lane = _lane_iota()
    rope = (lane >= ROPE_LANE) & (lane < ROPE_LANE + C_ROPE_DIM)
    cos, sin = cos_ref[...], sin_ref[...]
    q = jnp.dot(cq_ref[...], wuq_ref[0], preferred_element_type=F32)
    q_partner = jnp.dot(cq_ref[...], wuq_ref[1], preferred_element_type=F32)
    q_cos = gq_ref[0:1, :] * cos
    q_sin = gq_ref[1:2, :] * sin
    kn = jnp.dot(ckv_ref[...], wuk_ref[...], preferred_element_type=F32)
    _store_v_transposed(vt_ref, jnp.dot(ckv_ref[...], wuv_ref[...], preferred_element_type=F32), C_HEADS)

    seg = seg_ref[...]

    def inv_rms(v):
        return lax.rsqrt(jnp.dot((v * v).astype(BF16), seg, preferred_element_type=F32) + EPS)

    kr = jnp.where(rope, misc_ref[...], 0.0)
    kr = _rotate(kr * inv_rms(kr) * gkr_ref[...], cos, sin, lane)

    for hd in range(C_HEADS):
        cols = slice(hd * HEAD_SLAB, (hd + 1) * HEAD_SLAB)
        qh = q[:, cols]
        qc_ref[:, cols] = (inv_rms(qh) * (qh * q_cos + q_partner[:, cols] * q_sin)).astype(BF16)
        kh = kn[:, cols]
        kc_ref[:, cols] = (kh * inv_rms(kh) * gkn_ref[...] + kr).astype(BF16)


def _mla_prep(cq, ckv, misc, cos, sin, seq, p):
    t = cq.shape[0]
    tm = ROW_TILE
    tps = seq // tm
    row = lambda w: pl.BlockSpec((tm, w), lambda i: (i, 0))
    consts = [p["wuq"], p["wuk"], p["wuv"], p["gq"], p["gkn"], p["gkr"], p["seg"]]
    qk_w = C_HEADS * HEAD_SLAB
    return pl.pallas_call(
        _mla_prep_kernel,
        grid=(t // tm,),
        in_specs=[row(C_Q_RANK), row(C_KV_RANK), row(LANES), row(LANES), row(LANES)]
                 + [_full(c.shape) for c in consts],
        out_specs=[row(qk_w), row(qk_w),
                   pl.BlockSpec((1, C_HEADS, VT_ROWS, tm), lambda i: (i // tps, 0, 0, i % tps))],
        out_shape=[jax.ShapeDtypeStruct((t, qk_w), BF16), jax.ShapeDtypeStruct((t, qk_w), BF16),
                   jax.ShapeDtypeStruct((t // seq, C_HEADS, VT_ROWS, seq), BF16)],
        compiler_params=_cparams("parallel"),
        name="mla_prep",
    )(cq, ckv, misc, cos, sin, *consts)


def _attn_items(nq):
    return [(i, j) for i in range(nq) for j in range(i + 1)]


def _attn_kernel(q_ref, k_ref, vt_ref, mask_ref, o_ref, s0_ref, s1_ref, p0_ref, p1_ref,
                 mp0_ref, mp1_ref, mrun_ref, macc_ref, acc_ref):
    tk, tq = mask_ref.shape
    nq = q_ref.shape[0] // tq
    items = _attn_items(nq)
    n_items = len(items)
    s_bufs, p_bufs, mp_bufs = (s0_ref, s1_ref), (p0_ref, p1_ref), (mp0_ref, mp1_ref)
    mrun_ref[...] = jnp.full(mrun_ref.shape, NEG_INF, F32)
    macc_ref[...] = jnp.full(macc_ref.shape, NEG_INF, F32)
    acc_ref[...] = jnp.zeros(acc_ref.shape, F32)

    half = tk // 2
    nt_dims = (((1,), (1,)), ((), ()))
    top, bot, left, right = slice(0, half), slice(half, tk), slice(0, half), slice(half, tq)

    def scores(it, buf):
        qi, kj = items[it]
        k_rows, q_rows = kj * tk, qi * tq
        for hh in range(2):
            cols = slice(hh * HEAD_SLAB, (hh + 1) * HEAD_SLAB)
            if qi != kj:
                s_t = lax.dot_general(k_ref[k_rows:k_rows + tk, cols], q_ref[q_rows:q_rows + tq, cols],
                                      nt_dims, preferred_element_type=F32)
                s_bufs[buf][hh] = s_t
                col_max = jnp.max(s_t, axis=0, keepdims=True)
            else:
                s_top = lax.dot_general(k_ref[k_rows:k_rows + half, cols], q_ref[q_rows:q_rows + tq, cols],
                                        nt_dims, preferred_element_type=F32)
                s_top = jnp.concatenate([s_top[:, left] + mask_ref[top, left], s_top[:, right]], axis=1)
                s_br = lax.dot_general(k_ref[k_rows + half:k_rows + tk, cols],
                                       q_ref[q_rows + half:q_rows + tq, cols],
                                       nt_dims, preferred_element_type=F32) + mask_ref[bot, right]
                s_bufs[buf][hh, top, :] = s_top
                s_bufs[buf][hh, bot, right] = s_br
                max_top = jnp.max(s_top, axis=0, keepdims=True)
                col_max = jnp.concatenate(
                    [max_top[:, left], jnp.maximum(max_top[:, right], jnp.max(s_br, axis=0, keepdims=True))],
                    axis=1)
            mrun_ref[qi, hh] = jnp.maximum(mrun_ref[qi, hh], col_max)

    def exponentiate(it, buf):
        qi, kj = items[it]
        for hh in range(2):
            m = mrun_ref[qi, hh]
            if qi != kj:
                p_bufs[buf][hh] = jnp.exp2(s_bufs[buf][hh] - m).astype(BF16)
            else:
                p_bufs[buf][hh, top, :] = jnp.exp2(s_bufs[buf][hh, top, :] - m).astype(BF16)
                p_bufs[buf][hh, bot, right] = jnp.exp2(s_bufs[buf][hh, bot, right] - m[:, right]).astype(BF16)
            mp_bufs[buf][hh] = m

    def accumulate(it, buf):
        qi, kj = items[it]
        k_rows = kj * tk
        for hh in range(2):
            m = mp_bufs[buf][hh]
            if qi != kj:
                pv = jnp.dot(vt_ref[0, hh, :, k_rows:k_rows + tk], p_bufs[buf][hh],
                             preferred_element_type=F32)
            else:
                pv = jnp.concatenate(
                    [jnp.dot(vt_ref[0, hh, :, k_rows:k_rows + half], p_bufs[buf][hh, top, left],
                             preferred_element_type=F32),
                     jnp.dot(vt_ref[0, hh, :, k_rows:k_rows + tk], p_bufs[buf][hh, :, right],
                             preferred_element_type=F32)], axis=1)
            acc_ref[qi, hh] = jnp.exp2(macc_ref[qi, hh] - m) * acc_ref[qi, hh] + pv
            macc_ref[qi, hh] = m

    for it in range(n_items + 2):
        par = it % 2
        if 2 <= it:
            accumulate(it - 2, par)
            qi, kj = items[it - 2]
            if qi == kj:
                halves = [acc_ref[qi, hh, 0:C_V_DIM, :] / acc_ref[qi, hh, C_V_DIM:C_V_DIM + 1, :]
                          for hh in range(2)]
                o_ref[qi * tq:(qi + 1) * tq, :] = jnp.concatenate(halves, axis=0).T.astype(o_ref.dtype)
        if 1 <= it <= n_items:
            exponentiate(it - 1, 1 - par)
        if it < n_items:
            scores(it, par)


def _attention(q, k, vt, *, unit, name):
    t = q.shape[0]
    nb, n_heads, _, seq = vt.shape
    tq = ATTN_TILE
    nq = seq // tq
    assert (tq // 2) % unit == 0
    pos = np.arange(tq)
    diag_mask = np.where((pos[:, None] // unit) <= (pos[None, :] // unit), 0.0, NEG_INF)
    mask = jnp.asarray(diag_mask.astype(np.float32))
    seq_blk = lambda w: pl.BlockSpec((seq, w), lambda b, p: (b, p))
    return pl.pallas_call(
        _attn_kernel,
        grid=(nb, n_heads // 2),
        in_specs=[seq_blk(2 * HEAD_SLAB), seq_blk(2 * HEAD_SLAB),
                  pl.BlockSpec((1, 2, VT_ROWS, seq), lambda b, p: (b, p, 0, 0)),
                  pl.BlockSpec((tq, tq), lambda b, p: (0, 0), pipeline_mode=pl.Buffered(1))],
        out_specs=seq_blk(2 * C_V_DIM),
        out_shape=jax.ShapeDtypeStruct((t, n_heads * C_V_DIM), BF16),
        scratch_shapes=[pltpu.VMEM((2, tq, tq), F32), pltpu.VMEM((2, tq, tq), F32),
                        pltpu.VMEM((2, tq, tq), BF16), pltpu.VMEM((2, tq, tq), BF16),
                        pltpu.VMEM((2, 1, tq), F32), pltpu.VMEM((2, 1, tq), F32),
                        pltpu.VMEM((nq, 2, 1, tq), F32), pltpu.VMEM((nq, 2, 1, tq), F32),
                        pltpu.VMEM((nq, 2, VT_ROWS, tq), F32)],
        compiler_params=_cparams("parallel", "parallel"),
        name=name,
    )(q, k, vt, mask)


def _mem_kv_kernel(mem_ref, g_ref, w_ref, kg_ref, k_ref, v_ref):
    mn = (_rms(mem_ref[0]) * g_ref[...]).astype(BF16)
    kv = jnp.dot(mn, w_ref[...], preferred_element_type=F32)
    for hd in range(M_HEADS):
        cols = slice(hd * M_HEAD_DIM, (hd + 1) * M_HEAD_DIM)
        k_ref[0, :, cols] = (_rms(kv[:, cols]) * kg_ref[...]).astype(BF16)
    v_ref[0] = kv[:, M_W:].astype(BF16)


def _mem_kv(mem, p):
    nb, ml, _ = mem.shape
    consts = [p["mem_g"], p["w_mem_kv"], p["mkg"]]
    blk = pl.BlockSpec((1, ml, M_W), lambda b: (b, 0, 0))
    return pl.pallas_call(
        _mem_kv_kernel,
        grid=(nb,),
        in_specs=[pl.BlockSpec((1, ml, D_MODEL), lambda b: (b, 0, 0))] + [_full(c.shape) for c in consts],
        out_specs=[blk, blk],
        out_shape=[jax.ShapeDtypeStruct((nb, ml, M_W), BF16)] * 2,
        compiler_params=_cparams("parallel"),
        name="mem_kv",
    )(mem, *consts)


def _outproj_kernel(x_ref, a_ref, bn_ref, c_ref, ona_ref, onc_ref, wo_ref, xg_ref, wq_ref, mqg_ref,
                    km_ref, vm_ref, wmo_ref, o_ref):
    a_n = (_rms(a_ref[...].astype(F32)) * ona_ref[...]).astype(BF16)
    c_n = (_rms(c_ref[...].astype(F32)) * onc_ref[...]).astype(BF16)
    mix = jnp.concatenate([a_n, bn_ref[...], c_n], axis=-1)
    x1 = x_ref[...] + jnp.dot(mix, wo_ref[...], preferred_element_type=F32)

    h = (_rms(x1) * xg_ref[...]).astype(BF16)
    q = jnp.dot(h, wq_ref[...], preferred_element_type=F32)
    outs = []
    for hd in range(M_HEADS):
        cols = slice(hd * M_HEAD_DIM, (hd + 1) * M_HEAD_DIM)
        qh = (_rms(q[:, cols]) * mqg_ref[...]).astype(BF16)
        s = lax.dot_general(qh, km_ref[0, :, cols], (((1,), (1,)), ((), ())), preferred_element_type=F32)
        e = jnp.exp(s - jnp.max(s, axis=-1, keepdims=True))
        pr = e / jnp.sum(e, axis=-1, keepdims=True)
        outs.append(jnp.dot(pr.astype(BF16), vm_ref[0, :, cols], preferred_element_type=F32).astype(BF16))
    o_ref[...] = x1 + jnp.dot(jnp.concatenate(outs, axis=-1), wmo_ref[...], preferred_element_type=F32)


def _outproj(x2d, a, bn, c, km, vm, seq, p):
    t = x2d.shape[0]
    tm = ROW_TILE
    tps = seq // tm
    ml = km.shape[1]
    row = lambda w: pl.BlockSpec((tm, w), lambda i: (i, 0))
    memblk = pl.BlockSpec((1, ml, M_W), lambda i: (i // tps, 0, 0))
    c1 = [p["ona"], p["onc"], p["w_out"], p["xg"], p["w_mem_q"], p["mqg"]]
    return pl.pallas_call(
        _outproj_kernel,
        grid=(t // tm,),
        in_specs=[row(D_MODEL), row(A_W), row(B_W), row(C_W)] + [_full(c_.shape) for c_ in c1]
                 + [memblk, memblk, _full(p["w_mem_out"].shape)],
        out_specs=row(D_MODEL),
        out_shape=jax.ShapeDtypeStruct((t, D_MODEL), F32),
        compiler_params=_cparams("parallel"),
        name="out_proj_mem_attn",
    )(x2d, a, bn, c, *c1, km, vm, p["w_mem_out"])


def _silu(x):
    return x * jax.nn.sigmoid(x)


def _ffn_kernel(x_ref, g_ref, wg_ref, wu_ref, wd_ref, o_ref, *, n_chunks):
    x = x_ref[...]
    h = (_rms(x) * g_ref[...]).astype(BF16)
    fc = wg_ref.shape[1] // n_chunks
    acc = x
    for c in range(n_chunks):
        cols = slice(c * fc, (c + 1) * fc)
        act = _silu(jnp.dot(h, wg_ref[:, cols], preferred_element_type=F32)) * \
            jnp.dot(h, wu_ref[:, cols], preferred_element_type=F32)
        acc = acc + jnp.dot(act.astype(BF16), wd_ref[cols, :], preferred_element_type=F32)
    o_ref[...] = acc


def _ffn(x2d, g, wg, wu, wd):
    t = x2d.shape[0]
    tm = ROW_TILE
    row = pl.BlockSpec((tm, D_MODEL), lambda i: (i, 0))
    resident = lambda a: pl.BlockSpec(a.shape, lambda i: (0, 0), pipeline_mode=pl.Buffered(1))
    return pl.pallas_call(
        functools.partial(_ffn_kernel, n_chunks=2),
        grid=(t // tm,),
        in_specs=[row, _full(g.shape), resident(wg), resident(wu), resident(wd)],
        out_specs=row,
        out_shape=jax.ShapeDtypeStruct((t, D_MODEL), F32),
        compiler_params=_cparams("parallel"),
        name="ffn_dense",
    )(x2d, g, wg, wu, wd)


def _pack_bf16_pairs(lo, hi):
    lo_bits = pltpu.bitcast(lo.astype(BF16).astype(F32), jnp.uint32)
    hi_bits = pltpu.bitcast(hi.astype(BF16).astype(F32), jnp.uint32)
    return lax.shift_right_logical(lo_bits, jnp.uint32(16)) | (hi_bits & jnp.uint32(0xFFFF0000))


def _unpack_bf16_pairs(words):
    lo = pltpu.bitcast(lax.shift_left(words, jnp.uint32(16)), F32)
    hi = pltpu.bitcast(words & jnp.uint32(0xFFFF0000), F32)
    return lo, hi


def _pack_rows(v):
    q = D_MODEL // 4
    return _pack_bf16_pairs(v[:, 0:q], v[:, q:2 * q]), _pack_bf16_pairs(v[:, 2 * q:3 * q], v[:, 3 * q:])


def _unpack_rows(a, b):
    return jnp.concatenate([*_unpack_bf16_pairs(a), *_unpack_bf16_pairs(b)], axis=-1)


def _route_kernel(x_ref, g_ref, wr_ref, br_ref, tri_ref, ha_ref, hb_ref, route_ref, cnt_ref):
    lane = _lane_iota()
    h = _rms(x_ref[...]) * g_ref[...]
    ha_ref[...], hb_ref[...] = _pack_rows(h)
    h_hi = h.astype(BF16)
    h_lo = (h - h_hi.astype(F32)).astype(BF16)
    logits = (jnp.dot(h_hi, wr_ref[0], preferred_element_type=F32)
              + jnp.dot(h_lo, wr_ref[0], preferred_element_type=F32)
              + jnp.dot(h_hi, wr_ref[1], preferred_element_type=F32)) + br_ref[...]
    logits = jnp.where(lane < N_EXPERTS, logits, -jnp.inf)
    v1 = jnp.max(logits, axis=-1, keepdims=True)
    i1 = jnp.min(jnp.where(logits == v1, lane, LANES), axis=-1, keepdims=True)
    rest = jnp.where(lane == i1, -jnp.inf, logits)
    v2 = jnp.max(rest, axis=-1, keepdims=True)
    i2 = jnp.min(jnp.where(rest == v2, lane, LANES), axis=-1, keepdims=True)
    e2 = jnp.exp(v2 - v1)
    g1 = 1.0 / (1.0 + e2)
    hit1, hit2 = lane == i1, lane == i2
    ones = jnp.where(hit1 | hit2, 1.0, 0.0)
    before = jnp.dot(tri_ref[...], ones.astype(BF16), preferred_element_type=F32)
    r1 = jnp.sum(jnp.where(hit1, before, 0.0), axis=-1, keepdims=True)
    r2 = jnp.sum(jnp.where(hit2, before, 0.0), axis=-1, keepdims=True)
    cols = [i1.astype(F32), i2.astype(F32), g1, e2 * g1, r1, r2]
    route = jnp.zeros(route_ref.shape, F32)
    for n, c in enumerate(cols):
        route = jnp.where(lane == n, c, route)
    route_ref[...] = route
    cnt_ref[0] = jnp.broadcast_to(jnp.sum(ones, axis=0, keepdims=True), cnt_ref.shape[1:])


def _route(x2d, g, wr, br):
    t = x2d.shape[0]
    tm = MOE_CHUNK
    tri = jnp.asarray(np.tril(np.ones((tm, tm), np.float32), -1)).astype(BF16)
    row = lambda w: pl.BlockSpec((tm, w), lambda i: (i, 0))
    q = D_MODEL // 4
    return pl.pallas_call(
        _route_kernel,
        grid=(t // tm,),
        in_specs=[row(D_MODEL), _full(g.shape), _full(wr.shape), _full(br.shape), _full(tri.shape)],
        out_specs=[row(q), row(q), row(LANES), pl.BlockSpec((1, 8, LANES), lambda i: (i, 0, 0))],
        out_shape=[jax.ShapeDtypeStruct((t, q), jnp.uint32), jax.ShapeDtypeStruct((t, q), jnp.uint32),
                   jax.ShapeDtypeStruct((t, LANES), F32), jax.ShapeDtypeStruct((t // tm, 8, LANES), F32)],
        compiler_params=_cparams("parallel"),
        name="moe_route",
    )(x2d, g, wr, br, tri)


def _dest_kernel(route_ref, base_ref, o_ref):
    lane = _lane_iota()
    r = route_ref[...]
    base = base_ref[0, 0:1, :]
    lane_f = lane.astype(F32)
    d1 = jnp.sum(jnp.where(lane_f == r[:, 0:1], base, 0.0), axis=-1, keepdims=True) + r[:, 4:5]
    d2 = jnp.sum(jnp.where(lane_f == r[:, 1:2], base, 0.0), axis=-1, keepdims=True) + r[:, 5:6]
    both = jnp.where(lane == 0, d1, jnp.where(lane == 1, d2, 0.0))
    o_ref[...] = both.T[0:8, :].astype(jnp.int32)


def _destinations(route, base):
    t = route.shape[0]
    tm = MOE_CHUNK
    out = pl.pallas_call(
        _dest_kernel,
        grid=(t // tm,),
        in_specs=[pl.BlockSpec((tm, LANES), lambda i: (i, 0)), pl.BlockSpec((1, 8, LANES), lambda i: (i, 0, 0))],
        out_specs=pl.BlockSpec((8, tm), lambda i: (0, i)),
        out_shape=jax.ShapeDtypeStruct((8, t), jnp.int32),
        compiler_params=_cparams("parallel"),
        name="moe_dest",
    )(route, base)
    return out[0:2].reshape(1, 2 * t)


SC_WINDOW = 128


def _sc_mesh():
    return plsc.VectorSubcoreMesh(core_axis_name="core", subcore_axis_name="subcore")


def _sc_scatter_rows(x, idx, n_out):
    n, d = x.shape
    m = idx.shape[1]
    nblk = n // SC_WINDOW

    @pl.kernel(out_type=jax.ShapeDtypeStruct((n_out, d), x.dtype), mesh=_sc_mesh(), name="moe_sc_scatter")
    def scatter(x_hbm, i_hbm, o_hbm):
        def body(x_vmem, i_vmem):
            pltpu.sync_copy(x_vmem, o_hbm.at[i_vmem.at[0]])

        half = m // SC_WINDOW // 2
        pltpu.emit_pipeline(
            body,
            grid=(2, half),
            in_specs=[pl.BlockSpec((SC_WINDOW, d), lambda c, j: ((c * half + j) % nblk, 0)),
                      pl.BlockSpec((1, SC_WINDOW), lambda c, j: (0, c * half + j))],
            out_specs=[],
            core_axis_name=("core", "subcore"),
            dimension_semantics=(pltpu.PARALLEL, pltpu.PARALLEL),
        )(x_hbm, i_hbm)

    return scatter(x, idx)


def _sc_gather_rows(table, idx):
    d = table.shape[1]
    m = idx.shape[1]

    @pl.kernel(out_type=jax.ShapeDtypeStruct((m, d), table.dtype), mesh=_sc_mesh(), name="moe_sc_gather")
    def gather(t_hbm, i_hbm, o_hbm):
        def body(i_vmem, o_vmem):
            pltpu.sync_copy(t_hbm.at[i_vmem.at[0]], o_vmem)

        half = m // SC_WINDOW // 2
        pltpu.emit_pipeline(
            body,
            grid=(2, half),
            in_specs=[pl.BlockSpec((1, SC_WINDOW), lambda c, j: (0, c * half + j))],
            out_specs=[pl.BlockSpec((SC_WINDOW, d), lambda c, j: (c * half + j, 0))],
            core_axis_name=("core", "subcore"),
            dimension_semantics=(pltpu.PARALLEL, pltpu.PARALLEL),
        )(i_hbm, o_hbm)

    return gather(table, idx)


def _expert_kernel(blk_expert_ref, n_used_ref, xa_ref, xb_ref, wg_ref, wu_ref, wd_ref, ya_ref, yb_ref):
    del blk_expert_ref

    @pl.when(pl.program_id(0) < n_used_ref[0])
    def _():
        xe = _unpack_rows(xa_ref[...], xb_ref[...]).astype(BF16)
        act = _silu(jnp.dot(xe, wg_ref[0], preferred_element_type=F32)) * \
            jnp.dot(xe, wu_ref[0], preferred_element_type=F32)
        y = jnp.dot(act.astype(BF16), wd_ref[0], preferred_element_type=F32)
        ya_ref[...], yb_ref[...] = _pack_rows(y)

    @pl.when(pl.program_id(0) >= n_used_ref[0])
    def _():
        ya_ref[...] = jnp.zeros(ya_ref.shape, ya_ref.dtype)
        yb_ref[...] = jnp.zeros(yb_ref.shape, yb_ref.dtype)


def _experts(blk_expert, n_used, xa, xb, wg, wu, wd):
    n_rows, q = xa.shape
    ff = wg.shape[2]
    blk = MOE_BLOCK
    row = pl.BlockSpec((blk, q), lambda b, be, nu: (b, 0))
    grid_spec = pltpu.PrefetchScalarGridSpec(
        num_scalar_prefetch=2,
        grid=(n_rows // blk,),
        in_specs=[row, row,
                  pl.BlockSpec((1, D_MODEL, ff), lambda b, be, nu: (be[b], 0, 0)),
                  pl.BlockSpec((1, D_MODEL, ff), lambda b, be, nu: (be[b], 0, 0)),
                  pl.BlockSpec((1, ff, D_MODEL), lambda b, be, nu: (be[b], 0, 0))],
        out_specs=[row, row])
    return pl.pallas_call(
        _expert_kernel,
        grid_spec=grid_spec,
        out_shape=[jax.ShapeDtypeStruct((n_rows, q), jnp.uint32)] * 2,
        compiler_params=_cparams("arbitrary"),
        name="moe_experts",
    )(blk_expert, n_used, xa, xb, wg, wu, wd)


def _combine_kernel(x_ref, route_ref, a1_ref, b1_ref, a2_ref, b2_ref, o_ref):
    g1 = route_ref[:, 2:3]
    g2 = route_ref[:, 3:4]
    o_ref[...] = x_ref[...] + g1 * _unpack_rows(a1_ref[...], b1_ref[...]) \
        + g2 * _unpack_rows(a2_ref[...], b2_ref[...])


def _combine(x2d, route, ya, yb):
    t = x2d.shape[0]
    tm = ROW_TILE
    nt = t // tm
    q = ya.shape[1]
    row = lambda w: pl.BlockSpec((tm, w), lambda i: (i, 0))
    first = pl.BlockSpec((tm, q), lambda i: (i, 0))
    second = pl.BlockSpec((tm, q), lambda i: (nt + i, 0))
    return pl.pallas_call(
        _combine_kernel,
        grid=(nt,),
        in_specs=[row(D_MODEL), row(LANES), first, first, second, second],
        out_specs=row(D_MODEL),
        out_shape=jax.ShapeDtypeStruct((t, D_MODEL), F32),
        compiler_params=_cparams("parallel"),
        name="moe_combine",
    )(x2d, route, ya, yb, ya, yb)


def _moe_sorted(x2d, g, wr, br, wg, wu, wd):
    t = x2d.shape[0]
    blk = MOE_BLOCK
    n_blocks = 2 * t // blk + N_EXPERTS
    ha, hb, route, cnt = _route(x2d, g, wr, br)

    cnt = cnt[:, 0, :N_EXPERTS].astype(jnp.int32)
    before_chunk = jnp.cumsum(cnt, axis=0) - cnt
    seg_blocks = (jnp.sum(cnt, axis=0) + blk - 1) // blk
    seg_end_blk = jnp.cumsum(seg_blocks)
    seg_start = (seg_end_blk - seg_blocks) * blk
    base = jnp.pad((seg_start[None, :] + before_chunk).astype(F32), ((0, 0), (0, LANES - N_EXPERTS)))
    idx = _destinations(route, jnp.broadcast_to(base[:, None, :], (base.shape[0], 8, LANES)))
    past_end = jnp.arange(n_blocks, dtype=jnp.int32)[:, None] >= seg_end_blk[None, :]
    blk_expert = jnp.minimum(jnp.sum(past_end, axis=1), N_EXPERTS - 1).astype(jnp.int32)
    n_used = seg_end_blk[-1:].astype(jnp.int32)

    xa = _sc_scatter_rows(ha, idx, n_blocks * blk)
    xb = _sc_scatter_rows(hb, idx, n_blocks * blk)
    ya, yb = _experts(blk_expert, n_used, xa, xb, wg, wu, wd)
    return _combine(x2d, route, _sc_gather_rows(ya, idx), _sc_gather_rows(yb, idx))


def _pad_cols(w, width):
    return jnp.pad(w, ((0, 0), (0, width - w.shape[1])))


def _layer_params(l, mix_norm, w_in, b_forget, a_q_norm, a_k_norm, b_v_norm, b_spatial_w, b_spatial_b,
                  c_q_lat_norm, c_w_uq, c_kv_lat_norm, c_w_ukv, c_q_nope_norm, c_q_rope_norm,
                  c_k_nope_norm, c_k_rope_norm, out_norm_a, out_norm_b, out_norm_c, w_out,
                  xattn_norm, mem_norm, w_mem_q, w_mem_kv, m_q_norm, m_k_norm, w_mem_out):
    p = {}
    o = np.cumsum((0, A_W, A_W, A_W, A_HEADS, B_W, B_W, C_Q_RANK, C_KV_RANK, C_ROPE_DIM))
    w = w_in[l]
    seg = lambda n: w[:, o[n]:o[n + 1]]
    fa = seg(3)
    misc = jnp.zeros((D_MODEL, LANES), F32)
    misc = misc.at[:, ROPE_LANE:ROPE_LANE + C_ROPE_DIM].set(seg(8))
    fb = jnp.zeros((1, LANES), F32)
    for hd in range(A_HEADS):
        ln = FORGET_LANE + 8 * (hd // 2) + hd % 2
        misc = misc.at[:, ln].set(fa[:, hd])
        fb = fb.at[0, ln].set(b_forget[l, hd])
    p["w_in"] = jnp.concatenate([seg(0), seg(1), seg(2), seg(4), seg(5), seg(6), seg(7), misc], axis=1).astype(BF16)
    p["fb"] = fb
    p["mix_g"] = mix_norm[l][None]
    p["aq"] = jnp.tile(a_q_norm[l], A_HEADS)[None] * (A_HEAD_DIM ** -0.5 * LOG2E)
    p["ak"] = jnp.tile(a_k_norm[l], A_HEADS)[None]
    p["bvg"] = b_v_norm[l][None]
    pos = np.arange(B_WINDOW)
    mask = (pos[None, :] // CHUNK) <= (pos[:, None] // CHUNK)
    p["ws"] = jnp.where(mask[None], b_spatial_w[l], 0.0).reshape(B_GROUPS * B_WINDOW, B_WINDOW).astype(BF16)
    p["bs"] = jnp.repeat(b_spatial_b[l].T, B_GROUP_DIM, axis=1)
    p["onb"] = out_norm_b[l][None]
    p["cqg"] = c_q_lat_norm[l][None]
    p["ckvg"] = c_kv_lat_norm[l][None]
    gidx = np.arange(A_W) // A_HEAD_DIM
    p["gm"] = jnp.asarray((gidx[:, None] == gidx[None, :]).astype(np.float32) / A_HEAD_DIM).astype(BF16)
    p["tri"] = jnp.asarray(np.tril(np.ones((ROW_TILE, ROW_TILE), np.float32))).astype(BF16)

    qd = C_NOPE_DIM + C_ROPE_DIM
    half = C_ROPE_DIM // 2
    wq = c_w_uq[l]
    wq_partner = jnp.concatenate([jnp.zeros_like(wq[:, :, :C_NOPE_DIM]), wq[:, :, C_NOPE_DIM + half:],
                                  wq[:, :, C_NOPE_DIM:C_NOPE_DIM + half]], axis=-1)
    p["wuq"] = jnp.pad(jnp.stack([wq, wq_partner]), ((0, 0), (0, 0), (0, 0), (0, HEAD_SLAB - qd))
                       ).reshape(2, C_Q_RANK, -1).astype(BF16)
    wukv = c_w_ukv[l]
    p["wuk"] = jnp.pad(wukv[:, :, :C_NOPE_DIM], ((0, 0), (0, 0), (0, HEAD_SLAB - C_NOPE_DIM))
                       ).reshape(C_KV_RANK, -1).astype(BF16)
    p["wuv"] = wukv[:, :, C_NOPE_DIM:].reshape(C_KV_RANK, C_W).astype(BF16)
    gq = jnp.concatenate([c_q_nope_norm[l], c_q_rope_norm[l]])
    gq_partner = jnp.concatenate([jnp.zeros_like(c_q_nope_norm[l]), c_q_rope_norm[l][half:],
                                  c_q_rope_norm[l][:half]])
    p["gq"] = _pad_cols(jnp.stack([gq, gq_partner]) * (qd ** -0.5 * LOG2E), LANES)
    p["gkn"] = _pad_cols(c_k_nope_norm[l][None], LANES)
    seg = np.zeros((LANES, LANES), np.float32)
    seg[:C_NOPE_DIM, :C_NOPE_DIM] = 1.0 / C_NOPE_DIM
    seg[ROPE_LANE:ROPE_LANE + C_ROPE_DIM, ROPE_LANE:ROPE_LANE + C_ROPE_DIM] = 1.0 / C_ROPE_DIM
    p["seg"] = jnp.asarray(seg).astype(BF16)
    p["gkr"] = jnp.zeros((1, LANES), F32).at[0, ROPE_LANE:ROPE_LANE + C_ROPE_DIM].set(c_k_rope_norm[l])

    p["ona"] = out_norm_a[l][None]
    p["onc"] = out_norm_c[l][None]
    p["w_out"] = w_out[l].astype(BF16)
    p["xg"] = xattn_norm[l][None]
    p["w_mem_q"] = w_mem_q[l].astype(BF16)
    p["mqg"] = m_q_norm[l][None] * (M_HEAD_DIM ** -0.5)
    p["mem_g"] = mem_norm[l][None]
    p["w_mem_kv"] = w_mem_kv[l].astype(BF16)
    p["mkg"] = m_k_norm[l][None]
    p["w_mem_out"] = w_mem_out[l].astype(BF16)
    return p


def kernel(x, mem, positions, mix_norm, w_in, b_forget, a_q_norm, a_k_norm, b_v_norm, b_spatial_w, b_spatial_b, c_q_lat_norm, c_w_uq, c_kv_lat_norm, c_w_ukv, c_q_nope_norm, c_q_rope_norm, c_k_nope_norm, c_k_rope_norm, out_norm_a, out_norm_b, out_norm_c, w_out, xattn_norm, mem_norm, w_mem_q, w_mem_kv, m_q_norm, m_k_norm, w_mem_out, ffn_norm, ffn_w_gate, ffn_w_up, ffn_w_down, w_router, b_router, moe_w_gate, moe_w_up, moe_w_down):
    nb, seq, d = x.shape
    assert d == D_MODEL and seq % ROW_TILE == 0 and seq % ATTN_TILE == 0 and (nb * seq) % MOE_CHUNK == 0
    depth = w_in.shape[0]
    t = nb * seq
    x2d = x.reshape(t, d)
    cos, sin = _rope_tables(positions.reshape(t, 1).astype(F32))

    for l in range(depth):
        p = _layer_params(l, mix_norm, w_in, b_forget, a_q_norm, a_k_norm, b_v_norm, b_spatial_w,
                          b_spatial_b, c_q_lat_norm, c_w_uq, c_kv_lat_norm, c_w_ukv, c_q_nope_norm,
                          c_q_rope_norm, c_k_nope_norm, c_k_rope_norm, out_norm_a, out_norm_b,
                          out_norm_c, w_out, xattn_norm, mem_norm, w_mem_q, w_mem_kv, m_q_norm,
                          m_k_norm, w_mem_out)
        qa, ka, vta, bn, cq, ckv, misc = _inproj(x2d, seq, p)
        qc, kc, vtc = _mla_prep(cq, ckv, misc, cos, sin, seq, p)
        a = _attention(qa, ka, vta, unit=1, name="attn_fox")
        c = _attention(qc, kc, vtc, unit=CHUNK, name="attn_mla")
        km, vm = _mem_kv(mem, p)
        x2d = _outproj(x2d, a, bn, c, km, vm, seq, p)
        g = ffn_norm[l][None]
        if l % 2 == 0:
            m = l // 2
            ff = ffn_w_gate.shape[2]
            ff_pad = -(-ff // (2 * LANES)) * (2 * LANES)
            wg = _pad_cols(ffn_w_gate[m], ff_pad).astype(BF16)
            wu = _pad_cols(ffn_w_up[m], ff_pad).astype(BF16)
            wd = jnp.pad(ffn_w_down[m], ((0, ff_pad - ff), (0, 0))).astype(BF16)
            x2d = _ffn(x2d, g, wg, wu, wd)
        else:
            m = l // 2
            wr = _pad_cols(w_router[m], LANES)
            wr_hi = wr.astype(BF16)
            wr = jnp.stack([wr_hi, (wr - wr_hi.astype(F32)).astype(BF16)])
            br = _pad_cols(b_router[m][None], LANES)
            x2d = _moe_sorted(x2d, g, wr, br, moe_w_gate[m].astype(BF16), moe_w_up[m].astype(BF16),
                              moe_w_down[m].astype(BF16))
    return x2d.reshape(nb, seq, d)
```

```python
import functools

import numpy as np
import jax
import jax.numpy as jnp
from jax import lax
from jax.experimental import pallas as pl
from jax.experimental.pallas import tpu as pltpu
from jax.experimental.pallas import tpu_sc as plsc

F32 = jnp.float32
BF16 = jnp.bfloat16

D_MODEL = 1024
CHUNK = 64
EPS = 1e-6
NEG_INF = -1e30
A_HEADS, A_HEAD_DIM = 4, 64
B_GROUPS, B_GROUP_DIM, B_WINDOW = 4, 64, 128
C_HEADS, C_NOPE_DIM, C_ROPE_DIM, C_V_DIM = 8, 64, 32, 64
C_Q_RANK, C_KV_RANK = 256, 128
ROPE_THETA = 10000.0
M_HEADS, M_HEAD_DIM = 4, 128
N_EXPERTS = 8
A_W = A_HEADS * A_HEAD_DIM
B_W = B_GROUPS * B_GROUP_DIM
C_W = C_HEADS * C_V_DIM
M_W = M_HEADS * M_HEAD_DIM

LANES = 128

SEG_Q, SEG_K, SEG_V, SEG_U, SEG_VB, SEG_CQ, SEG_CKV, SEG_MISC = 0, 256, 512, 768, 1024, 1280, 1536, 1664
IN_PAD_W = SEG_MISC + LANES
ROPE_LANE = C_NOPE_DIM
FORGET_LANE = 96
HEAD_SLAB = LANES
VT_ROWS = 80
LOG2E = float(np.log2(np.e))

ROW_TILE = 512
ATTN_TILE = 512
MOE_CHUNK = 1024
MOE_BLOCK = 512
VMEM_LIMIT = 56 * 1024 * 1024


def _cparams(*sem):
    return pltpu.CompilerParams(dimension_semantics=sem, vmem_limit_bytes=VMEM_LIMIT)


def _full(shape):
    n = len(shape)
    return pl.BlockSpec(shape, lambda *_: (0,) * n)


def _rms(x):
    return x * lax.rsqrt(jnp.mean(x * x, axis=-1, keepdims=True) + EPS)


def _dot_split(v, exact, pieces, lhs_is_exact=False):
    total = None
    rem = v
    for n in range(pieces):
        part = rem.astype(BF16)
        if n + 1 < pieces:
            rem = rem - part.astype(F32)
        term = (jnp.dot(exact, part, preferred_element_type=F32) if lhs_is_exact
                else jnp.dot(part, exact, preferred_element_type=F32))
        total = term if total is None else total + term
    return total


def _lane_iota(n=LANES):
    return lax.broadcasted_iota(jnp.int32, (1, n), 1)


def _rope_table_kernel(pos_ref, inv_ref, sgn_ref, cos_ref, sin_ref):
    ang = pos_ref[...] * inv_ref[...]
    cos_ref[...] = jnp.cos(ang)
    sin_ref[...] = jnp.sin(ang) * sgn_ref[...]


def _rope_tables(pos_col):
    t = pos_col.shape[0]
    half = C_ROPE_DIM // 2
    inv = ROPE_THETA ** (-jnp.arange(half, dtype=F32) / half)
    inv_l = jnp.zeros((1, LANES), F32).at[0, ROPE_LANE:ROPE_LANE + C_ROPE_DIM].set(jnp.tile(inv, 2))
    sgn = np.zeros((1, LANES), np.float32)
    sgn[0, ROPE_LANE:ROPE_LANE + half] = -1.0
    sgn[0, ROPE_LANE + half:ROPE_LANE + C_ROPE_DIM] = 1.0
    tm = ROW_TILE
    return pl.pallas_call(
        _rope_table_kernel,
        grid=(t // tm,),
        in_specs=[pl.BlockSpec((tm, 1), lambda i: (i, 0)), _full((1, LANES)), _full((1, LANES))],
        out_specs=[pl.BlockSpec((tm, LANES), lambda i: (i, 0))] * 2,
        out_shape=[jax.ShapeDtypeStruct((t, LANES), F32)] * 2,
        compiler_params=_cparams("parallel"),
        name="rope_tables",
    )(pos_col, inv_l, jnp.asarray(sgn))


def _rotate(x, cos, sin_signed, lane):
    half = C_ROPE_DIM // 2
    partner = jnp.where(lane < ROPE_LANE + half,
                        pltpu.roll(x, LANES - half, 1), pltpu.roll(x, half, 1))
    return x * cos + partner * sin_signed


def _store_v_transposed(vt_ref, v, n_heads, lanes=slice(None)):
    tm = v.shape[0]
    v_t = v.T
    tail = jnp.where(lax.broadcasted_iota(jnp.int32, (VT_ROWS - C_V_DIM, tm), 0) == 0, 1.0, 0.0).astype(BF16)
    for hd in range(n_heads):
        vt_ref[0, hd, 0:C_V_DIM, lanes] = v_t[hd * C_V_DIM:(hd + 1) * C_V_DIM, :].astype(BF16)
        vt_ref[0, hd, C_V_DIM:VT_ROWS, lanes] = tail


def _gelu(x):
    return 0.5 * x * (1.0 + lax.erf(x * np.float32(1.0 / np.sqrt(2.0))))


def _inproj_kernel(x0_ref, xa_ref, xb_ref, g_ref, w_ref, aq_ref, ak_ref, fb_ref, bvg_ref, ws_ref, bs_ref,
                   onb_ref, cqg_ref, ckvg_ref, gm_ref, tri_ref,
                   qa_ref, ka_ref, vt_ref, bn_ref, cq_ref, ckv_ref, misc_ref,
                   carry_ref, buf0_ref, buf1_ref, *, tiles_per_seq):
    i = pl.program_id(0)
    tm = xa_ref.shape[0]

    def projection_parts(x_ref, buf_ref):
        h = (_rms(x_ref[...]) * g_ref[...]).astype(BF16)

        def part(lo, hi):
            def run():
                buf_ref[:, lo:hi] = jnp.dot(h, w_ref[:, lo:hi], preferred_element_type=F32)
            return run
        return [part(SEG_Q, SEG_U), part(SEG_U, SEG_CQ), part(SEG_CQ, IN_PAD_W)]

    @pl.when(i == 0)
    def _():
        for run in projection_parts(x0_ref, buf0_ref):
            run()

    refs = (aq_ref, ak_ref, fb_ref, bvg_ref, ws_ref, bs_ref, onb_ref, cqg_ref, ckvg_ref, gm_ref, tri_ref,
            qa_ref, ka_ref, vt_ref, bn_ref, cq_ref, ckv_ref, misc_ref, carry_ref)
    _mixer_prologues(buf0_ref, 0, 2 * i, tm, tiles_per_seq, projection_parts(xa_ref, buf1_ref), *refs)
    _mixer_prologues(buf1_ref, 1, 2 * i + 1, tm, tiles_per_seq, projection_parts(xb_ref, buf0_ref), *refs)


def _mixer_prologues(proj, half, tile, tm, tiles_per_seq, between, aq_ref, ak_ref, fb_ref, bvg_ref, ws_ref,
                     bs_ref, onb_ref, cqg_ref, ckvg_ref, gm_ref, tri_ref, qa_ref, ka_ref, vt_ref, bn_ref,
                     cq_ref, ckv_ref, misc_ref, carry_ref):
    out_rows = slice(half * tm, (half + 1) * tm)
    gm = gm_ref[...]

    def group_mean(v):
        return _dot_split(v, gm, 2)

    misc = proj[:, SEG_MISC:SEG_MISC + LANES]
    misc_ref[out_rows, :] = misc
    z = misc + fb_ref[...]
    log_f = jnp.minimum(z, 0.0) - jnp.log1p(jnp.exp(-jnp.abs(z)))
    carry = jnp.where(tile % tiles_per_seq == 0, 0.0, carry_ref[...])
    cum = _dot_split(log_f, tri_ref[...], 3, lhs_is_exact=True) + carry
    carry_ref[...] = cum[tm - 1:tm, :]
    f_hi = (cum * LOG2E).astype(BF16).astype(F32)
    f_rem = cum * LOG2E - f_hi
    f_mid = f_rem.astype(BF16).astype(F32)
    f_lo = f_rem - f_mid

    between[0]()
    q = proj[:, SEG_Q:SEG_Q + A_W]
    qn = q * lax.rsqrt(group_mean(q * q) + EPS) * aq_ref[...]
    k = proj[:, SEG_K:SEG_K + A_W]
    kn = k * lax.rsqrt(group_mean(k * k) + EPS) * ak_ref[...]
    lane = _lane_iota()
    for hd in range(A_HEADS):
        pair = slice((hd // 2) * LANES, (hd // 2 + 1) * LANES)
        slab = slice(hd * HEAD_SLAB, (hd + 1) * HEAD_SLAB)
        data = (lane < A_HEAD_DIM) if hd % 2 == 0 else (lane >= A_HEAD_DIM)
        e0 = A_HEAD_DIM if hd % 2 == 0 else 0
        fl = FORGET_LANE + 8 * (hd // 2) + hd % 2
        ones = jnp.where((lane >= e0) & (lane < e0 + 3), 1.0, 0.0)
        qa_ref[out_rows, slab] = jnp.where(data, qn[:, pair], ones).astype(BF16)
        bias = jnp.where(lane == e0, -f_hi[:, fl:fl + 1],
                         jnp.where(lane == e0 + 1, -f_mid[:, fl:fl + 1],
                                   jnp.where(lane == e0 + 2, -f_lo[:, fl:fl + 1], 0.0)))
        ka_ref[out_rows, slab] = jnp.where(data, kn[:, pair], bias).astype(BF16)
    _store_v_transposed(vt_ref, proj[:, SEG_V:SEG_V + A_W], A_HEADS, out_rows)

    between[1]()
    u = _gelu(proj[:, SEG_U:SEG_U + B_W])
    v = _gelu(proj[:, SEG_VB:SEG_VB + B_W])
    dv = v - group_mean(v)
    vn = dv * lax.rsqrt(group_mean(dv * dv) + EPS) * bvg_ref[...]
    group = lax.broadcasted_iota(jnp.int32, (1, B_W), 1) // B_GROUP_DIM
    for w in range(tm // B_WINDOW):
        rows = slice(w * B_WINDOW, (w + 1) * B_WINDOW)
        y_all = jnp.dot(ws_ref[...], vn[rows].astype(BF16), preferred_element_type=F32)
        y = bs_ref[...]
        for g in range(B_GROUPS):
            y = y + jnp.where(group == g, y_all[g * B_WINDOW:(g + 1) * B_WINDOW], 0.0)
        b = u[rows] * y
        bn_ref[half * tm + w * B_WINDOW:half * tm + (w + 1) * B_WINDOW, :] = (_rms(b) * onb_ref[...]).astype(BF16)

    between[2]()
    cq_ref[out_rows, :] = (_rms(proj[:, SEG_CQ:SEG_CQ + C_Q_RANK]) * cqg_ref[...]).astype(BF16)
    ckv_ref[out_rows, :] = (_rms(proj[:, SEG_CKV:SEG_CKV + C_KV_RANK]) * ckvg_ref[...]).astype(BF16)


def _inproj(x2d, seq, p):
    t = x2d.shape[0]
    tm = ROW_TILE
    tps = seq // tm
    nb = t // seq
    n_tiles = t // tm
    steps_per_seq = tps // 2
    row = lambda w: pl.BlockSpec((2 * tm, w), lambda i: (i, 0))
    x_tile = lambda index: pl.BlockSpec((tm, D_MODEL), lambda i: (index(i), 0))
    qk_w = A_HEADS * HEAD_SLAB
    out_shape = [
        jax.ShapeDtypeStruct((t, qk_w), BF16), jax.ShapeDtypeStruct((t, qk_w), BF16),
        jax.ShapeDtypeStruct((nb, A_HEADS, VT_ROWS, seq), BF16),
        jax.ShapeDtypeStruct((t, B_W), BF16),
        jax.ShapeDtypeStruct((t, C_Q_RANK), BF16), jax.ShapeDtypeStruct((t, C_KV_RANK), BF16),
        jax.ShapeDtypeStruct((t, LANES), F32),
    ]
    out_specs = [row(qk_w), row(qk_w),
                 pl.BlockSpec((1, A_HEADS, VT_ROWS, 2 * tm),
                              lambda i: (i // steps_per_seq, 0, 0, i % steps_per_seq)),
                 row(B_W), row(C_Q_RANK), row(C_KV_RANK), row(LANES)]
    consts = [p["mix_g"], p["w_in"], p["aq"], p["ak"], p["fb"], p["bvg"], p["ws"], p["bs"], p["onb"],
              p["cqg"], p["ckvg"], p["gm"], p["tri"]]
    return pl.pallas_call(
        functools.partial(_inproj_kernel, tiles_per_seq=tps),
        grid=(n_tiles // 2,),
        in_specs=[x_tile(lambda i: 0), x_tile(lambda i: 2 * i + 1),
                  x_tile(lambda i: jnp.minimum(2 * i + 2, n_tiles - 1))] + [_full(c.shape) for c in consts],
        out_specs=out_specs,
        out_shape=out_shape,
        scratch_shapes=[pltpu.VMEM((1, LANES), F32), pltpu.VMEM((tm, IN_PAD_W), F32),
                        pltpu.VMEM((tm, IN_PAD_W), F32)],
        compiler_params=_cparams("arbitrary"),
        name="in_proj",
    )(x2d, x2d, x2d, *consts)


def _mla_prep_kernel(cq_ref, ckv_ref, misc_ref, cos_ref, sin_ref, wuq_ref, wuk_ref, wuv_ref,
                     gq_ref, gkn_ref, gkr_ref, seg_ref, qc_ref, kc_ref, vt_ref):
    lane = _lane_iota()
    rope = (lane >= ROPE_LANE) & (lane < ROPE_LANE + C_ROPE_DIM)
    cos, sin = cos_ref[...], sin_ref[...]
    q = jnp.dot(cq_ref[...], wuq_ref[0], preferred_element_type=F32)
    q_partner = jnp.dot(cq_ref[...], wuq_ref[1], preferred_element_type=F32)
    q_cos = gq_ref[0:1, :] * cos
    q_sin = gq_ref[1:2, :] * sin
    kn = jnp.dot(ckv_ref[...], wuk_ref[...], preferred_element_type=F32)
    _store_v_transposed(vt_ref, jnp.dot(ckv_ref[...], wuv_ref[...], preferred_element_type=F32), C_HEADS)

    seg = seg_ref[...]

    def inv_rms(v):
        return lax.rsqrt(jnp.dot((v * v).astype(BF16), seg, preferred_element_type=F32) + EPS)

    kr = jnp.where(rope, misc_ref[...], 0.0)
    kr = _rotate(kr * inv_rms(kr) * gkr_ref[...], cos, sin, lane)

    for hd in range(C_HEADS):
        cols = slice(hd * HEAD_SLAB, (hd + 1) * HEAD_SLAB)
        qh = q[:, cols]
        qc_ref[:, cols] = (inv_rms(qh) * (qh * q_cos + q_partner[:, cols] * q_sin)).astype(BF16)
        kh = kn[:, cols]
        kc_ref[:, cols] = (kh * inv_rms(kh) * gkn_ref[...] + kr).astype(BF16)


def _mla_prep(cq, ckv, misc, cos, sin, seq, p):
    t = cq.shape[0]
    tm = ROW_TILE
    tps = seq // tm
    row = lambda w: pl.BlockSpec((tm, w), lambda i: (i, 0))
    consts = [p["wuq"], p["wuk"], p["wuv"], p["gq"], p["gkn"], p["gkr"], p["seg"]]
    qk_w = C_HEADS * HEAD_SLAB
    return pl.pallas_call(
        _mla_prep_kernel,
        grid=(t // tm,),
        in_specs=[row(C_Q_RANK), row(C_KV_RANK), row(LANES), row(LANES), row(LANES)]
                 + [_full(c.shape) for c in consts],
        out_specs=[row(qk_w), row(qk_w),
                   pl.BlockSpec((1, C_HEADS, VT_ROWS, tm), lambda i: (i // tps, 0, 0, i % tps))],
        out_shape=[jax.ShapeDtypeStruct((t, qk_w), BF16), jax.ShapeDtypeStruct((t, qk_w), BF16),
                   jax.ShapeDtypeStruct((t // seq, C_HEADS, VT_ROWS, seq), BF16)],
        compiler_params=_cparams("parallel"),
        name="mla_prep",
    )(cq, ckv, misc, cos, sin, *consts)


def _attn_items(nq):
    return [(i, j) for i in range(nq) for j in range(i + 1)]


def _attn_kernel(q_ref, k_ref, vt_ref, mask_ref, o_ref, s0_ref, s1_ref, p0_ref, p1_ref,
                 mp0_ref, mp1_ref, mrun_ref, macc_ref, acc_ref):
    tk, tq = mask_ref.shape
    nq = q_ref.shape[0] // tq
    items = _attn_items(nq)
    n_items = len(items)
    s_bufs, p_bufs, mp_bufs = (s0_ref, s1_ref), (p0_ref, p1_ref), (mp0_ref, mp1_ref)
    mrun_ref[...] = jnp.full(mrun_ref.shape, NEG_INF, F32)
    macc_ref[...] = jnp.full(macc_ref.shape, NEG_INF, F32)
    acc_ref[...] = jnp.zeros(acc_ref.shape, F32)

    half = tk // 2
    nt_dims = (((1,), (1,)), ((), ()))
    top, bot, left, right = slice(0, half), slice(half, tk), slice(0, half), slice(half, tq)

    def scores(it, buf):
        qi, kj = items[it]
        k_rows, q_rows = kj * tk, qi * tq
        for hh in range(2):
            cols = slice(hh * HEAD_SLAB, (hh + 1) * HEAD_SLAB)
            if qi != kj:
                s_t = lax.dot_general(k_ref[k_rows:k_rows + tk, cols], q_ref[q_rows:q_rows + tq, cols],
                                      nt_dims, preferred_element_type=F32)
                s_bufs[buf][hh] = s_t
                col_max = jnp.max(s_t, axis=0, keepdims=True)
            else:
                s_top = lax.dot_general(k_ref[k_rows:k_rows + half, cols], q_ref[q_rows:q_rows + tq, cols],
                                        nt_dims, preferred_element_type=F32)
                s_top = jnp.concatenate([s_top[:, left] + mask_ref[top, left], s_top[:, right]], axis=1)
                s_br = lax.dot_general(k_ref[k_rows + half:k_rows + tk, cols],
                                       q_ref[q_rows + half:q_rows + tq, cols],
                                       nt_dims, preferred_element_type=F32) + mask_ref[bot, right]
                s_bufs[buf][hh, top, :] = s_top
                s_bufs[buf][hh, bot, right] = s_br
                max_top = jnp.max(s_top, axis=0, keepdims=True)
                col_max = jnp.concatenate(
                    [max_top[:, left], jnp.maximum(max_top[:, right], jnp.max(s_br, axis=0, keepdims=True))],
                    axis=1)
            mrun_ref[qi, hh] = jnp.maximum(mrun_ref[qi, hh], col_max)

    def exponentiate(it, buf):
        qi, kj = items[it]
        for hh in range(2):
            m = mrun_ref[qi, hh]
            if qi != kj:
                p_bufs[buf][hh] = jnp.exp2(s_bufs[buf][hh] - m).astype(BF16)
            else:
                p_bufs[buf][hh, top, :] = jnp.exp2(s_bufs[buf][hh, top, :] - m).astype(BF16)
                p_bufs[buf][hh, bot, right] = jnp.exp2(s_bufs[buf][hh, bot, right] - m[:, right]).astype(BF16)
            mp_bufs[buf][hh] = m

    def accumulate(it, buf):
        qi, kj = items[it]
        k_rows = kj * tk
        for hh in range(2):
            m = mp_bufs[buf][hh]
            if qi != kj:
                pv = jnp.dot(vt_ref[0, hh, :, k_rows:k_rows + tk], p_bufs[buf][hh],
                             preferred_element_type=F32)
            else:
                pv = jnp.concatenate(
                    [jnp.dot(vt_ref[0, hh, :, k_rows:k_rows + half], p_bufs[buf][hh, top, left],
                             preferred_element_type=F32),
                     jnp.dot(vt_ref[0, hh, :, k_rows:k_rows + tk], p_bufs[buf][hh, :, right],
                             preferred_element_type=F32)], axis=1)
            acc_ref[qi, hh] = jnp.exp2(macc_ref[qi, hh] - m) * acc_ref[qi, hh] + pv
            macc_ref[qi, hh] = m

    for it in range(n_items + 2):
        par = it % 2
        if 2 <= it:
            accumulate(it - 2, par)
            qi, kj = items[it - 2]
            if qi == kj:
                halves = [acc_ref[qi, hh, 0:C_V_DIM, :] / acc_ref[qi, hh, C_V_DIM:C_V_DIM + 1, :]
                          for hh in range(2)]
                o_ref[qi * tq:(qi + 1) * tq, :] = jnp.concatenate(halves, axis=0).T.astype(o_ref.dtype)
        if 1 <= it <= n_items:
            exponentiate(it - 1, 1 - par)
        if it < n_items:
            scores(it, par)


def _attention(q, k, vt, *, unit, name):
    t = q.shape[0]
    nb, n_heads, _, seq = vt.shape
    tq = ATTN_TILE
    nq = seq // tq
    assert (tq // 2) % unit == 0
    pos = np.arange(tq)
    diag_mask = np.where((pos[:, None] // unit) <= (pos[None, :] // unit), 0.0, NEG_INF)
    mask = jnp.asarray(diag_mask.astype(np.float32))
    seq_blk = lambda w: pl.BlockSpec((seq, w), lambda b, p: (b, p))
    return pl.pallas_call(
        _attn_kernel,
        grid=(nb, n_heads // 2),
        in_specs=[seq_blk(2 * HEAD_SLAB), seq_blk(2 * HEAD_SLAB),
                  pl.BlockSpec((1, 2, VT_ROWS, seq), lambda b, p: (b, p, 0, 0)),
                  pl.BlockSpec((tq, tq), lambda b, p: (0, 0), pipeline_mode=pl.Buffered(1))],
        out_specs=seq_blk(2 * C_V_DIM),
        out_shape=jax.ShapeDtypeStruct((t, n_heads * C_V_DIM), BF16),
        scratch_shapes=[pltpu.VMEM((2, tq, tq), F32), pltpu.VMEM((2, tq, tq), F32),
                        pltpu.VMEM((2, tq, tq), BF16), pltpu.VMEM((2, tq, tq), BF16),
                        pltpu.VMEM((2, 1, tq), F32), pltpu.VMEM((2, 1, tq), F32),
                        pltpu.VMEM((nq, 2, 1, tq), F32), pltpu.VMEM((nq, 2, 1, tq), F32),
                        pltpu.VMEM((nq, 2, VT_ROWS, tq), F32)],
        compiler_params=_cparams("parallel", "parallel"),
        name=name,
    )(q, k, vt, mask)


def _mem_kv_kernel(mem_ref, g_ref, w_ref, kg_ref, k_ref, v_ref):
    mn = (_rms(mem_ref[0]) * g_ref[...]).astype(BF16)
    kv = jnp.dot(mn, w_ref[...], preferred_element_type=F32)
    for hd in range(M_HEADS):
        cols = slice(hd * M_HEAD_DIM, (hd + 1) * M_HEAD_DIM)
        k_ref[0, :, cols] = (_rms(kv[:, cols]) * kg_ref[...]).astype(BF16)
    v_ref[0] = kv[:, M_W:].astype(BF16)


def _mem_kv(mem, p):
    nb, ml, _ = mem.shape
    consts = [p["mem_g"], p["w_mem_kv"], p["mkg"]]
    blk = pl.BlockSpec((1, ml, M_W), lambda b: (b, 0, 0))
    return pl.pallas_call(
        _mem_kv_kernel,
        grid=(nb,),
        in_specs=[pl.BlockSpec((1, ml, D_MODEL), lambda b: (b, 0, 0))] + [_full(c.shape) for c in consts],
        out_specs=[blk, blk],
        out_shape=[jax.ShapeDtypeStruct((nb, ml, M_W), BF16)] * 2,
        compiler_params=_cparams("parallel"),
        name="mem_kv",
    )(mem, *consts)


def _outproj_kernel(x_ref, a_ref, bn_ref, c_ref, ona_ref, onc_ref, wo_ref, xg_ref, wq_ref, mqg_ref,
                    km_ref, vm_ref, wmo_ref, o_ref):
    tm = x_ref.shape[0] // 2
    tiles = [slice(0, tm), slice(tm, 2 * tm)]
    x1, q, outs = [], [], [[], []]
    for rows in tiles:
        a_n = (_rms(a_ref[rows, :].astype(F32)) * ona_ref[...]).astype(BF16)
        c_n = (_rms(c_ref[rows, :].astype(F32)) * onc_ref[...]).astype(BF16)
        mix = jnp.concatenate([a_n, bn_ref[rows, :], c_n], axis=-1)
        x1.append(x_ref[rows, :] + jnp.dot(mix, wo_ref[...], preferred_element_type=F32))
    for t in range(2):
        h = (_rms(x1[t]) * xg_ref[...]).astype(BF16)
        q.append(jnp.dot(h, wq_ref[...], preferred_element_type=F32))
    for hd in range(M_HEADS):
        cols = slice(hd * M_HEAD_DIM, (hd + 1) * M_HEAD_DIM)
        for t in range(2):
            qh = (_rms(q[t][:, cols]) * mqg_ref[...]).astype(BF16)
            s = lax.dot_general(qh, km_ref[0, :, cols], (((1,), (1,)), ((), ())), preferred_element_type=F32)
            e = jnp.exp(s - jnp.max(s, axis=-1, keepdims=True))
            pr = e / jnp.sum(e, axis=-1, keepdims=True)
            outs[t].append(jnp.dot(pr.astype(BF16), vm_ref[0, :, cols], preferred_element_type=F32).astype(BF16))
    for t, rows in enumerate(tiles):
        o_ref[rows, :] = x1[t] + jnp.dot(jnp.concatenate(outs[t], axis=-1), wmo_ref[...],
                                         preferred_element_type=F32)


def _outproj(x2d, a, bn, c, km, vm, seq, p):
    t = x2d.shape[0]
    tm = 2 * ROW_TILE
    tps = seq // tm
    ml = km.shape[1]
    row = lambda w: pl.BlockSpec((tm, w), lambda i: (i, 0))
    memblk = pl.BlockSpec((1, ml, M_W), lambda i: (i // tps, 0, 0))
    c1 = [p["ona"], p["onc"], p["w_out"], p["xg"], p["w_mem_q"], p["mqg"]]
    return pl.pallas_call(
        _outproj_kernel,
        grid=(t // tm,),
        in_specs=[row(D_MODEL), row(A_W), row(B_W), row(C_W)] + [_full(c_.shape) for c_ in c1]
                 + [memblk, memblk, _full(p["w_mem_out"].shape)],
        out_specs=row(D_MODEL),
        out_shape=jax.ShapeDtypeStruct((t, D_MODEL), F32),
        compiler_params=_cparams("parallel"),
        name="out_proj_mem_attn",
    )(x2d, a, bn, c, *c1, km, vm, p["w_mem_out"])


def _silu(x):
    return x * jax.nn.sigmoid(x)


def _ffn_kernel(x_ref, g_ref, wg_ref, wu_ref, wd_ref, o_ref, *, n_chunks):
    x = x_ref[...]
    h = (_rms(x) * g_ref[...]).astype(BF16)
    fc = wg_ref.shape[1] // n_chunks
    acc = x
    for c in range(n_chunks):
        cols = slice(c * fc, (c + 1) * fc)
        act = _silu(jnp.dot(h, wg_ref[:, cols], preferred_element_type=F32)) * \
            jnp.dot(h, wu_ref[:, cols], preferred_element_type=F32)
        acc = acc + jnp.dot(act.astype(BF16), wd_ref[cols, :], preferred_element_type=F32)
    o_ref[...] = acc


def _ffn(x2d, g, wg, wu, wd):
    t = x2d.shape[0]
    tm = ROW_TILE
    row = pl.BlockSpec((tm, D_MODEL), lambda i: (i, 0))
    resident = lambda a: pl.BlockSpec(a.shape, lambda i: (0, 0), pipeline_mode=pl.Buffered(1))
    return pl.pallas_call(
        functools.partial(_ffn_kernel, n_chunks=2),
        grid=(t // tm,),
        in_specs=[row, _full(g.shape), resident(wg), resident(wu), resident(wd)],
        out_specs=row,
        out_shape=jax.ShapeDtypeStruct((t, D_MODEL), F32),
        compiler_params=_cparams("parallel"),
        name="ffn_dense",
    )(x2d, g, wg, wu, wd)


def _pack_bf16_pairs(lo, hi):
    lo_bits = pltpu.bitcast(lo.astype(BF16).astype(F32), jnp.uint32)
    hi_bits = pltpu.bitcast(hi.astype(BF16).astype(F32), jnp.uint32)
    return lax.shift_right_logical(lo_bits, jnp.uint32(16)) | (hi_bits & jnp.uint32(0xFFFF0000))


def _unpack_bf16_pairs(words):
    lo = pltpu.bitcast(lax.shift_left(words, jnp.uint32(16)), F32)
    hi = pltpu.bitcast(words & jnp.uint32(0xFFFF0000), F32)
    return lo, hi


def _pack_rows(v):
    q = D_MODEL // 4
    return _pack_bf16_pairs(v[:, 0:q], v[:, q:2 * q]), _pack_bf16_pairs(v[:, 2 * q:3 * q], v[:, 3 * q:])


def _unpack_rows(a, b):
    return jnp.concatenate([*_unpack_bf16_pairs(a), *_unpack_bf16_pairs(b)], axis=-1)


def _route_kernel(x_ref, g_ref, wr_ref, br_ref, tri_ref, ha_ref, hb_ref, route_ref, cnt_ref):
    lane = _lane_iota()
    h = _rms(x_ref[...]) * g_ref[...]
    ha_ref[...], hb_ref[...] = _pack_rows(h)
    h_hi = h.astype(BF16)
    h_lo = (h - h_hi.astype(F32)).astype(BF16)
    logits = (jnp.dot(h_hi, wr_ref[0], preferred_element_type=F32)
              + jnp.dot(h_lo, wr_ref[0], preferred_element_type=F32)
              + jnp.dot(h_hi, wr_ref[1], preferred_element_type=F32)) + br_ref[...]
    logits = jnp.where(lane < N_EXPERTS, logits, -jnp.inf)
    v1 = jnp.max(logits, axis=-1, keepdims=True)
    i1 = jnp.min(jnp.where(logits == v1, lane, LANES), axis=-1, keepdims=True)
    rest = jnp.where(lane == i1, -jnp.inf, logits)
    v2 = jnp.max(rest, axis=-1, keepdims=True)
    i2 = jnp.min(jnp.where(rest == v2, lane, LANES), axis=-1, keepdims=True)
    e2 = jnp.exp(v2 - v1)
    g1 = 1.0 / (1.0 + e2)
    hit1, hit2 = lane == i1, lane == i2
    ones = jnp.where(hit1 | hit2, 1.0, 0.0)
    before = jnp.dot(tri_ref[...], ones.astype(BF16), preferred_element_type=F32)
    r1 = jnp.sum(jnp.where(hit1, before, 0.0), axis=-1, keepdims=True)
    r2 = jnp.sum(jnp.where(hit2, before, 0.0), axis=-1, keepdims=True)
    cols = [i1.astype(F32), i2.astype(F32), g1, e2 * g1, r1, r2]
    route = jnp.zeros(route_ref.shape, F32)
    for n, c in enumerate(cols):
        route = jnp.where(lane == n, c, route)
    route_ref[...] = route
    cnt_ref[0] = jnp.broadcast_to(jnp.sum(ones, axis=0, keepdims=True), cnt_ref.shape[1:])


def _route(x2d, g, wr, br):
    t = x2d.shape[0]
    tm = MOE_CHUNK
    tri = jnp.asarray(np.tril(np.ones((tm, tm), np.float32), -1)).astype(BF16)
    row = lambda w: pl.BlockSpec((tm, w), lambda i: (i, 0))
    q = D_MODEL // 4
    return pl.pallas_call(
        _route_kernel,
        grid=(t // tm,),
        in_specs=[row(D_MODEL), _full(g.shape), _full(wr.shape), _full(br.shape), _full(tri.shape)],
        out_specs=[row(q), row(q), row(LANES), pl.BlockSpec((1, 8, LANES), lambda i: (i, 0, 0))],
        out_shape=[jax.ShapeDtypeStruct((t, q), jnp.uint32), jax.ShapeDtypeStruct((t, q), jnp.uint32),
                   jax.ShapeDtypeStruct((t, LANES), F32), jax.ShapeDtypeStruct((t // tm, 8, LANES), F32)],
        compiler_params=_cparams("parallel"),
        name="moe_route",
    )(x2d, g, wr, br, tri)


def _dest_kernel(route_ref, base_ref, o_ref):
    lane = _lane_iota()
    r = route_ref[...]
    base = base_ref[0, 0:1, :]
    lane_f = lane.astype(F32)
    d1 = jnp.sum(jnp.where(lane_f == r[:, 0:1], base, 0.0), axis=-1, keepdims=True) + r[:, 4:5]
    d2 = jnp.sum(jnp.where(lane_f == r[:, 1:2], base, 0.0), axis=-1, keepdims=True) + r[:, 5:6]
    both = jnp.where(lane == 0, d1, jnp.where(lane == 1, d2, 0.0))
    o_ref[...] = both.T[0:8, :].astype(jnp.int32)


def _destinations(route, base):
    t = route.shape[0]
    tm = MOE_CHUNK
    out = pl.pallas_call(
        _dest_kernel,
        grid=(t // tm,),
        in_specs=[pl.BlockSpec((tm, LANES), lambda i: (i, 0)), pl.BlockSpec((1, 8, LANES), lambda i: (i, 0, 0))],
        out_specs=pl.BlockSpec((8, tm), lambda i: (0, i)),
        out_shape=jax.ShapeDtypeStruct((8, t), jnp.int32),
        compiler_params=_cparams("parallel"),
        name="moe_dest",
    )(route, base)
    return out[0:2].reshape(1, 2 * t)


SC_WINDOW = 128


def _sc_mesh():
    return plsc.VectorSubcoreMesh(core_axis_name="core", subcore_axis_name="subcore")


def _sc_scatter_rows(x, idx, n_out):
    n, d = x.shape
    m = idx.shape[1]
    nblk = n // SC_WINDOW

    @pl.kernel(out_type=jax.ShapeDtypeStruct((n_out, d), x.dtype), mesh=_sc_mesh(), name="moe_sc_scatter")
    def scatter(x_hbm, i_hbm, o_hbm):
        def body(x_vmem, i_vmem):
            pltpu.sync_copy(x_vmem, o_hbm.at[i_vmem.at[0]])

        half = m // SC_WINDOW // 2
        pltpu.emit_pipeline(
            body,
            grid=(2, half),
            in_specs=[pl.BlockSpec((SC_WINDOW, d), lambda c, j: ((c * half + j) % nblk, 0)),
                      pl.BlockSpec((1, SC_WINDOW), lambda c, j: (0, c * half + j))],
            out_specs=[],
            core_axis_name=("core", "subcore"),
            dimension_semantics=(pltpu.PARALLEL, pltpu.PARALLEL),
        )(x_hbm, i_hbm)

    return scatter(x, idx)


def _sc_gather_rows(table, idx):
    d = table.shape[1]
    m = idx.shape[1]

    @pl.kernel(out_type=jax.ShapeDtypeStruct((m, d), table.dtype), mesh=_sc_mesh(), name="moe_sc_gather")
    def gather(t_hbm, i_hbm, o_hbm):
        def body(i_vmem, o_vmem):
            pltpu.sync_copy(t_hbm.at[i_vmem.at[0]], o_vmem)

        half = m // SC_WINDOW // 2
        pltpu.emit_pipeline(
            body,
            grid=(2, half),
            in_specs=[pl.BlockSpec((1, SC_WINDOW), lambda c, j: (0, c * half + j))],
            out_specs=[pl.BlockSpec((SC_WINDOW, d), lambda c, j: (c * half + j, 0))],
            core_axis_name=("core", "subcore"),
            dimension_semantics=(pltpu.PARALLEL, pltpu.PARALLEL),
        )(i_hbm, o_hbm)

    return gather(table, idx)


def _expert_kernel(blk_expert_ref, n_used_ref, xa_ref, xb_ref, wg_ref, wu_ref, wd_ref, ya_ref, yb_ref):
    del blk_expert_ref

    @pl.when(pl.program_id(0) < n_used_ref[0])
    def _():
        xe = _unpack_rows(xa_ref[...], xb_ref[...]).astype(BF16)
        act = _silu(jnp.dot(xe, wg_ref[0], preferred_element_type=F32)) * \
            jnp.dot(xe, wu_ref[0], preferred_element_type=F32)
        y = jnp.dot(act.astype(BF16), wd_ref[0], preferred_element_type=F32)
        ya_ref[...], yb_ref[...] = _pack_rows(y)

    @pl.when(pl.program_id(0) >= n_used_ref[0])
    def _():
        ya_ref[...] = jnp.zeros(ya_ref.shape, ya_ref.dtype)
        yb_ref[...] = jnp.zeros(yb_ref.shape, yb_ref.dtype)


def _experts(blk_expert, n_used, xa, xb, wg, wu, wd):
    n_rows, q = xa.shape
    ff = wg.shape[2]
    blk = MOE_BLOCK
    row = pl.BlockSpec((blk, q), lambda b, be, nu: (b, 0))
    grid_spec = pltpu.PrefetchScalarGridSpec(
        num_scalar_prefetch=2,
        grid=(n_rows // blk,),
        in_specs=[row, row,
                  pl.BlockSpec((1, D_MODEL, ff), lambda b, be, nu: (be[b], 0, 0)),
                  pl.BlockSpec((1, D_MODEL, ff), lambda b, be, nu: (be[b], 0, 0)),
                  pl.BlockSpec((1, ff, D_MODEL), lambda b, be, nu: (be[b], 0, 0))],
        out_specs=[row, row])
    return pl.pallas_call(
        _expert_kernel,
        grid_spec=grid_spec,
        out_shape=[jax.ShapeDtypeStruct((n_rows, q), jnp.uint32)] * 2,
        compiler_params=_cparams("arbitrary"),
        name="moe_experts",
    )(blk_expert, n_used, xa, xb, wg, wu, wd)


def _combine_kernel(x_ref, route_ref, a1_ref, b1_ref, a2_ref, b2_ref, o_ref):
    g1 = route_ref[:, 2:3]
    g2 = route_ref[:, 3:4]
    o_ref[...] = x_ref[...] + g1 * _unpack_rows(a1_ref[...], b1_ref[...]) \
        + g2 * _unpack_rows(a2_ref[...], b2_ref[...])


def _combine(x2d, route, ya, yb):
    t = x2d.shape[0]
    tm = ROW_TILE
    nt = t // tm
    q = ya.shape[1]
    row = lambda w: pl.BlockSpec((tm, w), lambda i: (i, 0))
    first = pl.BlockSpec((tm, q), lambda i: (i, 0))
    second = pl.BlockSpec((tm, q), lambda i: (nt + i, 0))
    return pl.pallas_call(
        _combine_kernel,
        grid=(nt,),
        in_specs=[row(D_MODEL), row(LANES), first, first, second, second],
        out_specs=row(D_MODEL),
        out_shape=jax.ShapeDtypeStruct((t, D_MODEL), F32),
        compiler_params=_cparams("parallel"),
        name="moe_combine",
    )(x2d, route, ya, yb, ya, yb)


def _moe_sorted(x2d, g, wr, br, wg, wu, wd):
    t = x2d.shape[0]
    blk = MOE_BLOCK
    n_blocks = 2 * t // blk + N_EXPERTS
    ha, hb, route, cnt = _route(x2d, g, wr, br)

    cnt = cnt[:, 0, :N_EXPERTS].astype(jnp.int32)
    before_chunk = jnp.cumsum(cnt, axis=0) - cnt
    seg_blocks = (jnp.sum(cnt, axis=0) + blk - 1) // blk
    seg_end_blk = jnp.cumsum(seg_blocks)
    seg_start = (seg_end_blk - seg_blocks) * blk
    base = jnp.pad((seg_start[None, :] + before_chunk).astype(F32), ((0, 0), (0, LANES - N_EXPERTS)))
    idx = _destinations(route, jnp.broadcast_to(base[:, None, :], (base.shape[0], 8, LANES)))
    past_end = jnp.arange(n_blocks, dtype=jnp.int32)[:, None] >= seg_end_blk[None, :]
    blk_expert = jnp.minimum(jnp.sum(past_end, axis=1), N_EXPERTS - 1).astype(jnp.int32)
    n_used = seg_end_blk[-1:].astype(jnp.int32)

    xa = _sc_scatter_rows(ha, idx, n_blocks * blk)
    xb = _sc_scatter_rows(hb, idx, n_blocks * blk)
    ya, yb = _experts(blk_expert, n_used, xa, xb, wg, wu, wd)
    return _combine(x2d, route, _sc_gather_rows(ya, idx), _sc_gather_rows(yb, idx))


def _pad_cols(w, width):
    return jnp.pad(w, ((0, 0), (0, width - w.shape[1])))


def _layer_params(l, mix_norm, w_in, b_forget, a_q_norm, a_k_norm, b_v_norm, b_spatial_w, b_spatial_b,
                  c_q_lat_norm, c_w_uq, c_kv_lat_norm, c_w_ukv, c_q_nope_norm, c_q_rope_norm,
                  c_k_nope_norm, c_k_rope_norm, out_norm_a, out_norm_b, out_norm_c, w_out,
                  xattn_norm, mem_norm, w_mem_q, w_mem_kv, m_q_norm, m_k_norm, w_mem_out):
    p = {}
    o = np.cumsum((0, A_W, A_W, A_W, A_HEADS, B_W, B_W, C_Q_RANK, C_KV_RANK, C_ROPE_DIM))
    w = w_in[l]
    seg = lambda n: w[:, o[n]:o[n + 1]]
    fa = seg(3)
    misc = jnp.zeros((D_MODEL, LANES), F32)
    misc = misc.at[:, ROPE_LANE:ROPE_LANE + C_ROPE_DIM].set(seg(8))
    fb = jnp.zeros((1, LANES), F32)
    for hd in range(A_HEADS):
        ln = FORGET_LANE + 8 * (hd // 2) + hd % 2
        misc = misc.at[:, ln].set(fa[:, hd])
        fb = fb.at[0, ln].set(b_forget[l, hd])
    p["w_in"] = jnp.concatenate([seg(0), seg(1), seg(2), seg(4), seg(5), seg(6), seg(7), misc], axis=1).astype(BF16)
    p["fb"] = fb
    p["mix_g"] = mix_norm[l][None]
    p["aq"] = jnp.tile(a_q_norm[l], A_HEADS)[None] * (A_HEAD_DIM ** -0.5 * LOG2E)
    p["ak"] = jnp.tile(a_k_norm[l], A_HEADS)[None]
    p["bvg"] = b_v_norm[l][None]
    pos = np.arange(B_WINDOW)
    mask = (pos[None, :] // CHUNK) <= (pos[:, None] // CHUNK)
    p["ws"] = jnp.where(mask[None], b_spatial_w[l], 0.0).reshape(B_GROUPS * B_WINDOW, B_WINDOW).astype(BF16)
    p["bs"] = jnp.repeat(b_spatial_b[l].T, B_GROUP_DIM, axis=1)
    p["onb"] = out_norm_b[l][None]
    p["cqg"] = c_q_lat_norm[l][None]
    p["ckvg"] = c_kv_lat_norm[l][None]
    gidx = np.arange(A_W) // A_HEAD_DIM
    p["gm"] = jnp.asarray((gidx[:, None] == gidx[None, :]).astype(np.float32) / A_HEAD_DIM).astype(BF16)
    p["tri"] = jnp.asarray(np.tril(np.ones((ROW_TILE, ROW_TILE), np.float32))).astype(BF16)

    qd = C_NOPE_DIM + C_ROPE_DIM
    half = C_ROPE_DIM // 2
    wq = c_w_uq[l]
    wq_partner = jnp.concatenate([jnp.zeros_like(wq[:, :, :C_NOPE_DIM]), wq[:, :, C_NOPE_DIM + half:],
                                  wq[:, :, C_NOPE_DIM:C_NOPE_DIM + half]], axis=-1)
    p["wuq"] = jnp.pad(jnp.stack([wq, wq_partner]), ((0, 0), (0, 0), (0, 0), (0, HEAD_SLAB - qd))
                       ).reshape(2, C_Q_RANK, -1).astype(BF16)
    wukv = c_w_ukv[l]
    p["wuk"] = jnp.pad(wukv[:, :, :C_NOPE_DIM], ((0, 0), (0, 0), (0, HEAD_SLAB - C_NOPE_DIM))
                       ).reshape(C_KV_RANK, -1).astype(BF16)
    p["wuv"] = wukv[:, :, C_NOPE_DIM:].reshape(C_KV_RANK, C_W).astype(BF16)
    gq = jnp.concatenate([c_q_nope_norm[l], c_q_rope_norm[l]])
    gq_partner = jnp.concatenate([jnp.zeros_like(c_q_nope_norm[l]), c_q_rope_norm[l][half:],
                                  c_q_rope_norm[l][:half]])
    p["gq"] = _pad_cols(jnp.stack([gq, gq_partner]) * (qd ** -0.5 * LOG2E), LANES)
    p["gkn"] = _pad_cols(c_k_nope_norm[l][None], LANES)
    seg = np.zeros((LANES, LANES), np.float32)
    seg[:C_NOPE_DIM, :C_NOPE_DIM] = 1.0 / C_NOPE_DIM
    seg[ROPE_LANE:ROPE_LANE + C_ROPE_DIM, ROPE_LANE:ROPE_LANE + C_ROPE_DIM] = 1.0 / C_ROPE_DIM
    p["seg"] = jnp.asarray(seg).astype(BF16)
    p["gkr"] = jnp.zeros((1, LANES), F32).at[0, ROPE_LANE:ROPE_LANE + C_ROPE_DIM].set(c_k_rope_norm[l])

    p["ona"] = out_norm_a[l][None]
    p["onc"] = out_norm_c[l][None]
    p["w_out"] = w_out[l].astype(BF16)
    p["xg"] = xattn_norm[l][None]
    p["w_mem_q"] = w_mem_q[l].astype(BF16)
    p["mqg"] = m_q_norm[l][None] * (M_HEAD_DIM ** -0.5)
    p["mem_g"] = mem_norm[l][None]
    p["w_mem_kv"] = w_mem_kv[l].astype(BF16)
    p["mkg"] = m_k_norm[l][None]
    p["w_mem_out"] = w_mem_out[l].astype(BF16)
    return p


def kernel(x, mem, positions, mix_norm, w_in, b_forget, a_q_norm, a_k_norm, b_v_norm, b_spatial_w, b_spatial_b, c_q_lat_norm, c_w_uq, c_kv_lat_norm, c_w_ukv, c_q_nope_norm, c_q_rope_norm, c_k_nope_norm, c_k_rope_norm, out_norm_a, out_norm_b, out_norm_c, w_out, xattn_norm, mem_norm, w_mem_q, w_mem_kv, m_q_norm, m_k_norm, w_mem_out, ffn_norm, ffn_w_gate, ffn_w_up, ffn_w_down, w_router, b_router, moe_w_gate, moe_w_up, moe_w_down):
    nb, seq, d = x.shape
    assert d == D_MODEL and seq % (2 * ROW_TILE) == 0 and seq % ATTN_TILE == 0
    assert (nb * seq) % MOE_CHUNK == 0 and (nb * seq) % SC_WINDOW == 0 and (2 * nb * seq) % MOE_BLOCK == 0
    depth = w_in.shape[0]
    t = nb * seq
    x2d = x.reshape(t, d)
    cos, sin = _rope_tables(positions.reshape(t, 1).astype(F32))

    for l in range(depth):
        p = _layer_params(l, mix_norm, w_in, b_forget, a_q_norm, a_k_norm, b_v_norm, b_spatial_w,
                          b_spatial_b, c_q_lat_norm, c_w_uq, c_kv_lat_norm, c_w_ukv, c_q_nope_norm,
                          c_q_rope_norm, c_k_nope_norm, c_k_rope_norm, out_norm_a, out_norm_b,
                          out_norm_c, w_out, xattn_norm, mem_norm, w_mem_q, w_mem_kv, m_q_norm,
                          m_k_norm, w_mem_out)
        qa, ka, vta, bn, cq, ckv, misc = _inproj(x2d, seq, p)
        qc, kc, vtc = _mla_prep(cq, ckv, misc, cos, sin, seq, p)
        a = _attention(qa, ka, vta, unit=1, name="attn_fox")
        c = _attention(qc, kc, vtc, unit=CHUNK, name="attn_mla")
        km, vm = _mem_kv(mem, p)
        x2d = _outproj(x2d, a, bn, c, km, vm, seq, p)
        g = ffn_norm[l][None]
        if l % 2 == 0:
            m = l // 2
            ff = ffn_w_gate.shape[2]
            ff_pad = -(-ff // (2 * LANES)) * (2 * LANES)
            wg = _pad_cols(ffn_w_gate[m], ff_pad).astype(BF16)
            wu = _pad_cols(ffn_w_up[m], ff_pad).astype(BF16)
            wd = jnp.pad(ffn_w_down[m], ((0, ff_pad - ff), (0, 0))).astype(BF16)
            x2d = _ffn(x2d, g, wg, wu, wd)
        else:
            m = l // 2
            wr = _pad_cols(w_router[m], LANES)
            wr_hi = wr.astype(BF16)
            wr = jnp.stack([wr_hi, (wr - wr_hi.astype(F32)).astype(BF16)])
            br = _pad_cols(b_router[m][None], LANES)
            x2d = _moe_sorted(x2d, g, wr, br, moe_w_gate[m].astype(BF16), moe_w_up[m].astype(BF16),
                              moe_w_down[m].astype(BF16))
    return x2d.reshape(nb, seq, d)
```

```python
import functools

import numpy as np
import jax
import jax.numpy as jnp
from jax import lax
from jax.experimental import pallas as pl
from jax.experimental.pallas import tpu as pltpu
from jax.experimental.pallas import tpu_sc as plsc

F32 = jnp.float32
BF16 = jnp.bfloat16

D_MODEL = 1024
CHUNK = 64
EPS = 1e-6
NEG_INF = -1e30
A_HEADS, A_HEAD_DIM = 4, 64
B_GROUPS, B_GROUP_DIM, B_WINDOW = 4, 64, 128
C_HEADS, C_NOPE_DIM, C_ROPE_DIM, C_V_DIM = 8, 64, 32, 64
C_Q_RANK, C_KV_RANK = 256, 128
ROPE_THETA = 10000.0
M_HEADS, M_HEAD_DIM = 4, 128
N_EXPERTS = 8
A_W = A_HEADS * A_HEAD_DIM
B_W = B_GROUPS * B_GROUP_DIM
C_W = C_HEADS * C_V_DIM
M_W = M_HEADS * M_HEAD_DIM

LANES = 128

SEG_Q, SEG_K, SEG_V, SEG_U, SEG_VB, SEG_CQ, SEG_CKV, SEG_MISC = 0, 256, 512, 768, 1024, 1280, 1536, 1664
IN_PAD_W = SEG_MISC + LANES
ROPE_LANE = C_NOPE_DIM
FORGET_LANE = 96
HEAD_SLAB = LANES
VT_ROWS = 80
LOG2E = float(np.log2(np.e))

ROW_TILE = 512
ATTN_TILE = 512
ATTN_HEADS = 4
MOE_CHUNK = 1024
MOE_BLOCK = 512
VMEM_LIMIT = 56 * 1024 * 1024


def _cparams(*sem):
    return pltpu.CompilerParams(dimension_semantics=sem, vmem_limit_bytes=VMEM_LIMIT)


def _full(shape):
    n = len(shape)
    return pl.BlockSpec(shape, lambda *_: (0,) * n)


def _rms(x):
    return x * lax.rsqrt(jnp.mean(x * x, axis=-1, keepdims=True) + EPS)


def _dot_split(v, exact, pieces, lhs_is_exact=False):
    total = None
    rem = v
    for n in range(pieces):
        part = rem.astype(BF16)
        if n + 1 < pieces:
            rem = rem - part.astype(F32)
        term = (jnp.dot(exact, part, preferred_element_type=F32) if lhs_is_exact
                else jnp.dot(part, exact, preferred_element_type=F32))
        total = term if total is None else total + term
    return total


def _lane_iota(n=LANES):
    return lax.broadcasted_iota(jnp.int32, (1, n), 1)


def _rope_table_kernel(pos_ref, inv_ref, sgn_ref, cos_ref, sin_ref):
    ang = pos_ref[...] * inv_ref[...]
    cos_ref[...] = jnp.cos(ang)
    sin_ref[...] = jnp.sin(ang) * sgn_ref[...]


def _rope_tables(pos_col):
    t = pos_col.shape[0]
    half = C_ROPE_DIM // 2
    inv = ROPE_THETA ** (-jnp.arange(half, dtype=F32) / half)
    inv_l = jnp.zeros((1, LANES), F32).at[0, ROPE_LANE:ROPE_LANE + C_ROPE_DIM].set(jnp.tile(inv, 2))
    sgn = np.zeros((1, LANES), np.float32)
    sgn[0, ROPE_LANE:ROPE_LANE + half] = -1.0
    sgn[0, ROPE_LANE + half:ROPE_LANE + C_ROPE_DIM] = 1.0
    tm = ROW_TILE
    return pl.pallas_call(
        _rope_table_kernel,
        grid=(t // tm,),
        in_specs=[pl.BlockSpec((tm, 1), lambda i: (i, 0)), _full((1, LANES)), _full((1, LANES))],
        out_specs=[pl.BlockSpec((tm, LANES), lambda i: (i, 0))] * 2,
        out_shape=[jax.ShapeDtypeStruct((t, LANES), F32)] * 2,
        compiler_params=_cparams("parallel"),
        name="rope_tables",
    )(pos_col, inv_l, jnp.asarray(sgn))


def _rotate(x, cos, sin_signed, lane):
    half = C_ROPE_DIM // 2
    partner = jnp.where(lane < ROPE_LANE + half,
                        pltpu.roll(x, LANES - half, 1), pltpu.roll(x, half, 1))
    return x * cos + partner * sin_signed


def _store_v_transposed(vt_ref, v, n_heads, lanes=slice(None)):
    tm = v.shape[0]
    v_t = v.T
    tail = jnp.where(lax.broadcasted_iota(jnp.int32, (VT_ROWS - C_V_DIM, tm), 0) == 0, 1.0, 0.0).astype(BF16)
    for hd in range(n_heads):
        vt_ref[0, hd, 0:C_V_DIM, lanes] = v_t[hd * C_V_DIM:(hd + 1) * C_V_DIM, :].astype(BF16)
        vt_ref[0, hd, C_V_DIM:VT_ROWS, lanes] = tail


def _gelu(x):
    return 0.5 * x * (1.0 + lax.erf(x * np.float32(1.0 / np.sqrt(2.0))))


def _inproj_kernel(x0_ref, xa_ref, xb_ref, g_ref, w_ref, aq_ref, ak_ref, fb_ref, bvg_ref, ws_ref, bs_ref,
                   onb_ref, cqg_ref, ckvg_ref, gm_ref, tri_ref,
                   qa_ref, ka_ref, vt_ref, bn_ref, cq_ref, ckv_ref, misc_ref,
                   carry_ref, buf0_ref, buf1_ref, *, tiles_per_seq):
    i = pl.program_id(0)
    tm = xa_ref.shape[0]

    def projection_parts(x_ref, buf_ref):
        h = (_rms(x_ref[...]) * g_ref[...]).astype(BF16)

        def part(lo, hi):
            def run():
                buf_ref[:, lo:hi] = jnp.dot(h, w_ref[:, lo:hi], preferred_element_type=F32)
            return run
        return [part(SEG_Q, SEG_U), part(SEG_U, SEG_CQ), part(SEG_CQ, IN_PAD_W)]

    @pl.when(i == 0)
    def _():
        for run in projection_parts(x0_ref, buf0_ref):
            run()

    refs = (aq_ref, ak_ref, fb_ref, bvg_ref, ws_ref, bs_ref, onb_ref, cqg_ref, ckvg_ref, gm_ref, tri_ref,
            qa_ref, ka_ref, vt_ref, bn_ref, cq_ref, ckv_ref, misc_ref, carry_ref)
    _mixer_prologues(buf0_ref, 0, 2 * i, tm, tiles_per_seq, projection_parts(xa_ref, buf1_ref), *refs)
    _mixer_prologues(buf1_ref, 1, 2 * i + 1, tm, tiles_per_seq, projection_parts(xb_ref, buf0_ref), *refs)


def _mixer_prologues(proj, half, tile, tm, tiles_per_seq, between, aq_ref, ak_ref, fb_ref, bvg_ref, ws_ref,
                     bs_ref, onb_ref, cqg_ref, ckvg_ref, gm_ref, tri_ref, qa_ref, ka_ref, vt_ref, bn_ref,
                     cq_ref, ckv_ref, misc_ref, carry_ref):
    out_rows = slice(half * tm, (half + 1) * tm)
    gm = gm_ref[...]

    def group_mean(v):
        return _dot_split(v, gm, 2)

    misc = proj[:, SEG_MISC:SEG_MISC + LANES]
    misc_ref[out_rows, :] = misc
    z = misc + fb_ref[...]
    log_f = jnp.minimum(z, 0.0) - jnp.log1p(jnp.exp(-jnp.abs(z)))
    carry = jnp.where(tile % tiles_per_seq == 0, 0.0, carry_ref[...])
    cum = _dot_split(log_f, tri_ref[...], 3, lhs_is_exact=True) + carry
    carry_ref[...] = cum[tm - 1:tm, :]
    f_hi = (cum * LOG2E).astype(BF16).astype(F32)
    f_rem = cum * LOG2E - f_hi
    f_mid = f_rem.astype(BF16).astype(F32)
    f_lo = f_rem - f_mid

    between[0]()
    q = proj[:, SEG_Q:SEG_Q + A_W]
    qn = q * lax.rsqrt(group_mean(q * q) + EPS) * aq_ref[...]
    k = proj[:, SEG_K:SEG_K + A_W]
    kn = k * lax.rsqrt(group_mean(k * k) + EPS) * ak_ref[...]
    lane = _lane_iota()
    for hd in range(A_HEADS):
        pair = slice((hd // 2) * LANES, (hd // 2 + 1) * LANES)
        slab = slice(hd * HEAD_SLAB, (hd + 1) * HEAD_SLAB)
        data = (lane < A_HEAD_DIM) if hd % 2 == 0 else (lane >= A_HEAD_DIM)
        e0 = A_HEAD_DIM if hd % 2 == 0 else 0
        fl = FORGET_LANE + 8 * (hd // 2) + hd % 2
        ones = jnp.where((lane >= e0) & (lane < e0 + 3), 1.0, 0.0)
        qa_ref[out_rows, slab] = jnp.where(data, qn[:, pair], ones).astype(BF16)
        bias = jnp.where(lane == e0, -f_hi[:, fl:fl + 1],
                         jnp.where(lane == e0 + 1, -f_mid[:, fl:fl + 1],
                                   jnp.where(lane == e0 + 2, -f_lo[:, fl:fl + 1], 0.0)))
        ka_ref[out_rows, slab] = jnp.where(data, kn[:, pair], bias).astype(BF16)
    _store_v_transposed(vt_ref, proj[:, SEG_V:SEG_V + A_W], A_HEADS, out_rows)

    between[1]()
    u = _gelu(proj[:, SEG_U:SEG_U + B_W])
    v = _gelu(proj[:, SEG_VB:SEG_VB + B_W])
    dv = v - group_mean(v)
    vn = dv * lax.rsqrt(group_mean(dv * dv) + EPS) * bvg_ref[...]
    group = lax.broadcasted_iota(jnp.int32, (1, B_W), 1) // B_GROUP_DIM
    for w in range(tm // B_WINDOW):
        rows = slice(w * B_WINDOW, (w + 1) * B_WINDOW)
        y_all = jnp.dot(ws_ref[...], vn[rows].astype(BF16), preferred_element_type=F32)
        y = bs_ref[...]
        for g in range(B_GROUPS):
            y = y + jnp.where(group == g, y_all[g * B_WINDOW:(g + 1) * B_WINDOW], 0.0)
        b = u[rows] * y
        bn_ref[half * tm + w * B_WINDOW:half * tm + (w + 1) * B_WINDOW, :] = (_rms(b) * onb_ref[...]).astype(BF16)

    between[2]()
    cq_ref[out_rows, :] = (_rms(proj[:, SEG_CQ:SEG_CQ + C_Q_RANK]) * cqg_ref[...]).astype(BF16)
    ckv_ref[out_rows, :] = (_rms(proj[:, SEG_CKV:SEG_CKV + C_KV_RANK]) * ckvg_ref[...]).astype(BF16)


def _inproj(x2d, seq, p):
    t = x2d.shape[0]
    tm = ROW_TILE
    tps = seq // tm
    nb = t // seq
    n_tiles = t // tm
    steps_per_seq = tps // 2
    row = lambda w: pl.BlockSpec((2 * tm, w), lambda i: (i, 0))
    x_tile = lambda index: pl.BlockSpec((tm, D_MODEL), lambda i: (index(i), 0))
    qk_w = A_HEADS * HEAD_SLAB
    out_shape = [
        jax.ShapeDtypeStruct((t, qk_w), BF16), jax.ShapeDtypeStruct((t, qk_w), BF16),
        jax.ShapeDtypeStruct((nb, A_HEADS, VT_ROWS, seq), BF16),
        jax.ShapeDtypeStruct((t, B_W), BF16),
        jax.ShapeDtypeStruct((t, C_Q_RANK), BF16), jax.ShapeDtypeStruct((t, C_KV_RANK), BF16),
        jax.ShapeDtypeStruct((t, LANES), F32),
    ]
    out_specs = [row(qk_w), row(qk_w),
                 pl.BlockSpec((1, A_HEADS, VT_ROWS, 2 * tm),
                              lambda i: (i // steps_per_seq, 0, 0, i % steps_per_seq)),
                 row(B_W), row(C_Q_RANK), row(C_KV_RANK), row(LANES)]
    consts = [p["mix_g"], p["w_in"], p["aq"], p["ak"], p["fb"], p["bvg"], p["ws"], p["bs"], p["onb"],
              p["cqg"], p["ckvg"], p["gm"], p["tri"]]
    return pl.pallas_call(
        functools.partial(_inproj_kernel, tiles_per_seq=tps),
        grid=(n_tiles // 2,),
        in_specs=[x_tile(lambda i: 0), x_tile(lambda i: 2 * i + 1),
                  x_tile(lambda i: jnp.minimum(2 * i + 2, n_tiles - 1))] + [_full(c.shape) for c in consts],
        out_specs=out_specs,
        out_shape=out_shape,
        scratch_shapes=[pltpu.VMEM((1, LANES), F32), pltpu.VMEM((tm, IN_PAD_W), F32),
                        pltpu.VMEM((tm, IN_PAD_W), F32)],
        compiler_params=_cparams("arbitrary"),
        name="in_proj",
    )(x2d, x2d, x2d, *consts)


def _mla_prep_kernel(cq_ref, ckv_ref, misc_ref, cos_ref, sin_ref, wuq_ref, wuk_ref, wuv_ref,
                     gq_ref, gkn_ref, gkr_ref, seg_ref, qc_ref, kc_ref, vt_ref):
    tm = cq_ref.shape[0] // 2
    tiles = [slice(0, tm), slice(tm, 2 * tm)]
    lane = _lane_iota()
    rope = (lane >= ROPE_LANE) & (lane < ROPE_LANE + C_ROPE_DIM)
    seg = seg_ref[...]

    def inv_rms(v):
        return lax.rsqrt(jnp.dot((v * v).astype(BF16), seg, preferred_element_type=F32) + EPS)

    q, q_partner, kn, q_cos, q_sin, kr = [], [], [], [], [], []
    for rows in tiles:
        cos, sin = cos_ref[rows, :], sin_ref[rows, :]
        q.append(jnp.dot(cq_ref[rows, :], wuq_ref[0], preferred_element_type=F32))
        q_partner.append(jnp.dot(cq_ref[rows, :], wuq_ref[1], preferred_element_type=F32))
        q_cos.append(gq_ref[0:1, :] * cos)
        q_sin.append(gq_ref[1:2, :] * sin)
        kn.append(jnp.dot(ckv_ref[rows, :], wuk_ref[...], preferred_element_type=F32))
        _store_v_transposed(vt_ref, jnp.dot(ckv_ref[rows, :], wuv_ref[...], preferred_element_type=F32),
                            C_HEADS, rows)
        k_rot = jnp.where(rope, misc_ref[rows, :], 0.0)
        kr.append(_rotate(k_rot * inv_rms(k_rot) * gkr_ref[...], cos, sin, lane))

    for hd in range(C_HEADS):
        cols = slice(hd * HEAD_SLAB, (hd + 1) * HEAD_SLAB)
        for t, rows in enumerate(tiles):
            qh = q[t][:, cols]
            qc_ref[rows, cols] = (inv_rms(qh) * (qh * q_cos[t] + q_partner[t][:, cols] * q_sin[t])).astype(BF16)
            kh = kn[t][:, cols]
            kc_ref[rows, cols] = (kh * inv_rms(kh) * gkn_ref[...] + kr[t]).astype(BF16)


def _mla_prep(cq, ckv, misc, cos, sin, seq, p):
    t = cq.shape[0]
    tm = 2 * ROW_TILE
    tps = seq // tm
    row = lambda w: pl.BlockSpec((tm, w), lambda i: (i, 0))
    consts = [p["wuq"], p["wuk"], p["wuv"], p["gq"], p["gkn"], p["gkr"], p["seg"]]
    qk_w = C_HEADS * HEAD_SLAB
    return pl.pallas_call(
        _mla_prep_kernel,
        grid=(t // tm,),
        in_specs=[row(C_Q_RANK), row(C_KV_RANK), row(LANES), row(LANES), row(LANES)]
                 + [_full(c.shape) for c in consts],
        out_specs=[row(qk_w), row(qk_w),
                   pl.BlockSpec((1, C_HEADS, VT_ROWS, tm), lambda i: (i // tps, 0, 0, i % tps))],
        out_shape=[jax.ShapeDtypeStruct((t, qk_w), BF16), jax.ShapeDtypeStruct((t, qk_w), BF16),
                   jax.ShapeDtypeStruct((t // seq, C_HEADS, VT_ROWS, seq), BF16)],
        compiler_params=_cparams("parallel"),
        name="mla_prep",
    )(cq, ckv, misc, cos, sin, *consts)


def _attn_items(nq):
    return [(i, j) for i in range(nq) for j in range(i + 1)]


def _attn_kernel(q_ref, k_ref, vt_ref, mask_ref, o_ref, s0_ref, s1_ref, p0_ref, p1_ref,
                 mp0_ref, mp1_ref, mrun_ref, macc_ref, acc_ref):
    tk, tq = mask_ref.shape
    nq = q_ref.shape[0] // tq
    n_h = q_ref.shape[1] // HEAD_SLAB
    items = _attn_items(nq)
    n_items = len(items)
    s_bufs, p_bufs, mp_bufs = (s0_ref, s1_ref), (p0_ref, p1_ref), (mp0_ref, mp1_ref)
    mrun_ref[...] = jnp.full(mrun_ref.shape, NEG_INF, F32)
    macc_ref[...] = jnp.full(macc_ref.shape, NEG_INF, F32)
    acc_ref[...] = jnp.zeros(acc_ref.shape, F32)

    half = tk // 2
    nt_dims = (((1,), (1,)), ((), ()))
    top, bot, left, right = slice(0, half), slice(half, tk), slice(0, half), slice(half, tq)

    def scores(it, buf):
        qi, kj = items[it]
        k_rows, q_rows = kj * tk, qi * tq
        for hh in range(n_h):
            cols = slice(hh * HEAD_SLAB, (hh + 1) * HEAD_SLAB)
            if qi != kj:
                s_t = lax.dot_general(k_ref[k_rows:k_rows + tk, cols], q_ref[q_rows:q_rows + tq, cols],
                                      nt_dims, preferred_element_type=F32)
                s_bufs[buf][hh] = s_t
                col_max = jnp.max(s_t, axis=0, keepdims=True)
            else:
                s_top = lax.dot_general(k_ref[k_rows:k_rows + half, cols], q_ref[q_rows:q_rows + tq, cols],
                                        nt_dims, preferred_element_type=F32)
                s_top = jnp.concatenate([s_top[:, left] + mask_ref[top, left], s_top[:, right]], axis=1)
                s_br = lax.dot_general(k_ref[k_rows + half:k_rows + tk, cols],
                                       q_ref[q_rows + half:q_rows + tq, cols],
                                       nt_dims, preferred_element_type=F32) + mask_ref[bot, right]
                s_bufs[buf][hh, top, :] = s_top
                s_bufs[buf][hh, bot, right] = s_br
                max_top = jnp.max(s_top, axis=0, keepdims=True)
                col_max = jnp.concatenate(
                    [max_top[:, left], jnp.maximum(max_top[:, right], jnp.max(s_br, axis=0, keepdims=True))],
                    axis=1)
            mrun_ref[qi, hh] = jnp.maximum(mrun_ref[qi, hh], col_max)

    def exponentiate(it, buf):
        qi, kj = items[it]
        for hh in range(n_h):
            m = mrun_ref[qi, hh]
            if qi != kj:
                p_bufs[buf][hh] = jnp.exp2(s_bufs[buf][hh] - m).astype(BF16)
            else:
                p_bufs[buf][hh, top, :] = jnp.exp2(s_bufs[buf][hh, top, :] - m).astype(BF16)
                p_bufs[buf][hh, bot, right] = jnp.exp2(s_bufs[buf][hh, bot, right] - m[:, right]).astype(BF16)
            mp_bufs[buf][hh] = m

    def accumulate(it, buf):
        qi, kj = items[it]
        k_rows = kj * tk
        for hh in range(n_h):
            m = mp_bufs[buf][hh]
            if qi != kj:
                pv = jnp.dot(vt_ref[0, hh, :, k_rows:k_rows + tk], p_bufs[buf][hh],
                             preferred_element_type=F32)
            else:
                pv = jnp.concatenate(
                    [jnp.dot(vt_ref[0, hh, :, k_rows:k_rows + half], p_bufs[buf][hh, top, left],
                             preferred_element_type=F32),
                     jnp.dot(vt_ref[0, hh, :, k_rows:k_rows + tk], p_bufs[buf][hh, :, right],
                             preferred_element_type=F32)], axis=1)
            acc_ref[qi, hh] = jnp.exp2(macc_ref[qi, hh] - m) * acc_ref[qi, hh] + pv
            macc_ref[qi, hh] = m

    for it in range(n_items + 2):
        par = it % 2
        if 2 <= it:
            accumulate(it - 2, par)
            qi, kj = items[it - 2]
            if qi == kj:
                halves = [acc_ref[qi, hh, 0:C_V_DIM, :] / acc_ref[qi, hh, C_V_DIM:C_V_DIM + 1, :]
                          for hh in range(n_h)]
                o_ref[qi * tq:(qi + 1) * tq, :] = jnp.concatenate(halves, axis=0).T.astype(o_ref.dtype)
        if 1 <= it <= n_items:
            exponentiate(it - 1, 1 - par)
        if it < n_items:
            scores(it, par)


def _attention(q, k, vt, *, unit, name):
    t = q.shape[0]
    nb, n_heads, _, seq = vt.shape
    tq = ATTN_TILE
    nq = seq // tq
    assert (tq // 2) % unit == 0
    pos = np.arange(tq)
    diag_mask = np.where((pos[:, None] // unit) <= (pos[None, :] // unit), 0.0, NEG_INF)
    mask = jnp.asarray(diag_mask.astype(np.float32))
    n_h = ATTN_HEADS
    seq_blk = lambda w: pl.BlockSpec((seq, w), lambda b, p: (b, p))
    return pl.pallas_call(
        _attn_kernel,
        grid=(nb, n_heads // n_h),
        in_specs=[seq_blk(n_h * HEAD_SLAB), seq_blk(n_h * HEAD_SLAB),
                  pl.BlockSpec((1, n_h, VT_ROWS, seq), lambda b, p: (b, p, 0, 0)),
                  pl.BlockSpec((tq, tq), lambda b, p: (0, 0), pipeline_mode=pl.Buffered(1))],
        out_specs=seq_blk(n_h * C_V_DIM),
        out_shape=jax.ShapeDtypeStruct((t, n_heads * C_V_DIM), BF16),
        scratch_shapes=[pltpu.VMEM((n_h, tq, tq), F32), pltpu.VMEM((n_h, tq, tq), F32),
                        pltpu.VMEM((n_h, tq, tq), BF16), pltpu.VMEM((n_h, tq, tq), BF16),
                        pltpu.VMEM((n_h, 1, tq), F32), pltpu.VMEM((n_h, 1, tq), F32),
                        pltpu.VMEM((nq, n_h, 1, tq), F32), pltpu.VMEM((nq, n_h, 1, tq), F32),
                        pltpu.VMEM((nq, n_h, VT_ROWS, tq), F32)],
        compiler_params=_cparams("parallel", "parallel"),
        name=name,
    )(q, k, vt, mask)


def _mem_kv_kernel(mem_ref, g_ref, w_ref, kg_ref, k_ref, v_ref):
    mn = (_rms(mem_ref[0]) * g_ref[...]).astype(BF16)
    kv = jnp.dot(mn, w_ref[...], preferred_element_type=F32)
    for hd in range(M_HEADS):
        cols = slice(hd * M_HEAD_DIM, (hd + 1) * M_HEAD_DIM)
        k_ref[0, :, cols] = (_rms(kv[:, cols]) * kg_ref[...]).astype(BF16)
    v_ref[0] = kv[:, M_W:].astype(BF16)


def _mem_kv(mem, p):
    nb, ml, _ = mem.shape
    consts = [p["mem_g"], p["w_mem_kv"], p["mkg"]]
    blk = pl.BlockSpec((1, ml, M_W), lambda b: (b, 0, 0))
    return pl.pallas_call(
        _mem_kv_kernel,
        grid=(nb,),
        in_specs=[pl.BlockSpec((1, ml, D_MODEL), lambda b: (b, 0, 0))] + [_full(c.shape) for c in consts],
        out_specs=[blk, blk],
        out_shape=[jax.ShapeDtypeStruct((nb, ml, M_W), BF16)] * 2,
        compiler_params=_cparams("parallel"),
        name="mem_kv",
    )(mem, *consts)


def _outproj_kernel(x_ref, a_ref, bn_ref, c_ref, ona_ref, onc_ref, wo_ref, xg_ref, wq_ref, mqg_ref,
                    km_ref, vm_ref, wmo_ref, o_ref):
    tm = x_ref.shape[0] // 2
    tiles = [slice(0, tm), slice(tm, 2 * tm)]
    x1, q, outs = [], [], [[], []]
    for rows in tiles:
        a_n = (_rms(a_ref[rows, :].astype(F32)) * ona_ref[...]).astype(BF16)
        c_n = (_rms(c_ref[rows, :].astype(F32)) * onc_ref[...]).astype(BF16)
        mix = jnp.concatenate([a_n, bn_ref[rows, :], c_n], axis=-1)
        x1.append(x_ref[rows, :] + jnp.dot(mix, wo_ref[...], preferred_element_type=F32))
    for t in range(2):
        h = (_rms(x1[t]) * xg_ref[...]).astype(BF16)
        q.append(jnp.dot(h, wq_ref[...], preferred_element_type=F32))
    for hd in range(M_HEADS):
        cols = slice(hd * M_HEAD_DIM, (hd + 1) * M_HEAD_DIM)
        for t in range(2):
            qh = (_rms(q[t][:, cols]) * mqg_ref[...]).astype(BF16)
            s = lax.dot_general(qh, km_ref[0, :, cols], (((1,), (1,)), ((), ())), preferred_element_type=F32)
            e = jnp.exp(s - jnp.max(s, axis=-1, keepdims=True))
            pr = e / jnp.sum(e, axis=-1, keepdims=True)
            outs[t].append(jnp.dot(pr.astype(BF16), vm_ref[0, :, cols], preferred_element_type=F32).astype(BF16))
    for t, rows in enumerate(tiles):
        o_ref[rows, :] = x1[t] + jnp.dot(jnp.concatenate(outs[t], axis=-1), wmo_ref[...],
                                         preferred_element_type=F32)


def _outproj(x2d, a, bn, c, km, vm, seq, p):
    t = x2d.shape[0]
    tm = 2 * ROW_TILE
    tps = seq // tm
    ml = km.shape[1]
    row = lambda w: pl.BlockSpec((tm, w), lambda i: (i, 0))
    memblk = pl.BlockSpec((1, ml, M_W), lambda i: (i // tps, 0, 0))
    c1 = [p["ona"], p["onc"], p["w_out"], p["xg"], p["w_mem_q"], p["mqg"]]
    return pl.pallas_call(
        _outproj_kernel,
        grid=(t // tm,),
        in_specs=[row(D_MODEL), row(A_W), row(B_W), row(C_W)] + [_full(c_.shape) for c_ in c1]
                 + [memblk, memblk, _full(p["w_mem_out"].shape)],
        out_specs=row(D_MODEL),
        out_shape=jax.ShapeDtypeStruct((t, D_MODEL), F32),
        compiler_params=_cparams("parallel"),
        name="out_proj_mem_attn",
    )(x2d, a, bn, c, *c1, km, vm, p["w_mem_out"])


def _silu(x):
    return x * jax.nn.sigmoid(x)


def _ffn_kernel(x_ref, g_ref, wg_ref, wu_ref, wd_ref, o_ref, *, n_chunks):
    x = x_ref[...]
    h = (_rms(x) * g_ref[...]).astype(BF16)
    fc = wg_ref.shape[1] // n_chunks
    acc = x
    for c in range(n_chunks):
        cols = slice(c * fc, (c + 1) * fc)
        act = _silu(jnp.dot(h, wg_ref[:, cols], preferred_element_type=F32)) * \
            jnp.dot(h, wu_ref[:, cols], preferred_element_type=F32)
        acc = acc + jnp.dot(act.astype(BF16), wd_ref[cols, :], preferred_element_type=F32)
    o_ref[...] = acc


def _ffn(x2d, g, wg, wu, wd):
    t = x2d.shape[0]
    tm = ROW_TILE
    row = pl.BlockSpec((tm, D_MODEL), lambda i: (i, 0))
    resident = lambda a: pl.BlockSpec(a.shape, lambda i: (0, 0), pipeline_mode=pl.Buffered(1))
    return pl.pallas_call(
        functools.partial(_ffn_kernel, n_chunks=2),
        grid=(t // tm,),
        in_specs=[row, _full(g.shape), resident(wg), resident(wu), resident(wd)],
        out_specs=row,
        out_shape=jax.ShapeDtypeStruct((t, D_MODEL), F32),
        compiler_params=_cparams("parallel"),
        name="ffn_dense",
    )(x2d, g, wg, wu, wd)


def _pack_bf16_pairs(lo, hi):
    lo_bits = pltpu.bitcast(lo.astype(BF16).astype(F32), jnp.uint32)
    hi_bits = pltpu.bitcast(hi.astype(BF16).astype(F32), jnp.uint32)
    return lax.shift_right_logical(lo_bits, jnp.uint32(16)) | (hi_bits & jnp.uint32(0xFFFF0000))


def _unpack_bf16_pairs(words):
    lo = pltpu.bitcast(lax.shift_left(words, jnp.uint32(16)), F32)
    hi = pltpu.bitcast(words & jnp.uint32(0xFFFF0000), F32)
    return lo, hi


def _pack_rows(v):
    q = D_MODEL // 4
    return _pack_bf16_pairs(v[:, 0:q], v[:, q:2 * q]), _pack_bf16_pairs(v[:, 2 * q:3 * q], v[:, 3 * q:])


def _unpack_rows(a, b):
    return jnp.concatenate([*_unpack_bf16_pairs(a), *_unpack_bf16_pairs(b)], axis=-1)


def _route_kernel(x_ref, g_ref, wr_ref, br_ref, tri_ref, ha_ref, hb_ref, route_ref, cnt_ref):
    lane = _lane_iota()
    h = _rms(x_ref[...]) * g_ref[...]
    ha_ref[...], hb_ref[...] = _pack_rows(h)
    h_hi = h.astype(BF16)
    h_lo = (h - h_hi.astype(F32)).astype(BF16)
    logits = (jnp.dot(h_hi, wr_ref[0], preferred_element_type=F32)
              + jnp.dot(h_lo, wr_ref[0], preferred_element_type=F32)
              + jnp.dot(h_hi, wr_ref[1], preferred_element_type=F32)) + br_ref[...]
    logits = jnp.where(lane < N_EXPERTS, logits, -jnp.inf)
    v1 = jnp.max(logits, axis=-1, keepdims=True)
    i1 = jnp.min(jnp.where(logits == v1, lane, LANES), axis=-1, keepdims=True)
    rest = jnp.where(lane == i1, -jnp.inf, logits)
    v2 = jnp.max(rest, axis=-1, keepdims=True)
    i2 = jnp.min(jnp.where(rest == v2, lane, LANES), axis=-1, keepdims=True)
    e2 = jnp.exp(v2 - v1)
    g1 = 1.0 / (1.0 + e2)
    hit1, hit2 = lane == i1, lane == i2
    ones = jnp.where(hit1 | hit2, 1.0, 0.0)
    before = jnp.dot(tri_ref[...], ones.astype(BF16), preferred_element_type=F32)
    r1 = jnp.sum(jnp.where(hit1, before, 0.0), axis=-1, keepdims=True)
    r2 = jnp.sum(jnp.where(hit2, before, 0.0), axis=-1, keepdims=True)
    cols = [i1.astype(F32), i2.astype(F32), g1, e2 * g1, r1, r2]
    route = jnp.zeros(route_ref.shape, F32)
    for n, c in enumerate(cols):
        route = jnp.where(lane == n, c, route)
    route_ref[...] = route
    cnt_ref[0] = jnp.broadcast_to(jnp.sum(ones, axis=0, keepdims=True), cnt_ref.shape[1:])


def _route(x2d, g, wr, br):
    t = x2d.shape[0]
    tm = MOE_CHUNK
    tri = jnp.asarray(np.tril(np.ones((tm, tm), np.float32), -1)).astype(BF16)
    row = lambda w: pl.BlockSpec((tm, w), lambda i: (i, 0))
    q = D_MODEL // 4
    return pl.pallas_call(
        _route_kernel,
        grid=(t // tm,),
        in_specs=[row(D_MODEL), _full(g.shape), _full(wr.shape), _full(br.shape), _full(tri.shape)],
        out_specs=[row(q), row(q), row(LANES), pl.BlockSpec((1, 8, LANES), lambda i: (i, 0, 0))],
        out_shape=[jax.ShapeDtypeStruct((t, q), jnp.uint32), jax.ShapeDtypeStruct((t, q), jnp.uint32),
                   jax.ShapeDtypeStruct((t, LANES), F32), jax.ShapeDtypeStruct((t // tm, 8, LANES), F32)],
        compiler_params=_cparams("parallel"),
        name="moe_route",
    )(x2d, g, wr, br, tri)


def _dest_kernel(route_ref, base_ref, o_ref):
    lane = _lane_iota()
    r = route_ref[...]
    base = base_ref[0, 0:1, :]
    lane_f = lane.astype(F32)
    d1 = jnp.sum(jnp.where(lane_f == r[:, 0:1], base, 0.0), axis=-1, keepdims=True) + r[:, 4:5]
    d2 = jnp.sum(jnp.where(lane_f == r[:, 1:2], base, 0.0), axis=-1, keepdims=True) + r[:, 5:6]
    both = jnp.where(lane == 0, d1, jnp.where(lane == 1, d2, 0.0))
    o_ref[...] = both.T[0:8, :].astype(jnp.int32)


def _destinations(route, base):
    t = route.shape[0]
    tm = MOE_CHUNK
    out = pl.pallas_call(
        _dest_kernel,
        grid=(t // tm,),
        in_specs=[pl.BlockSpec((tm, LANES), lambda i: (i, 0)), pl.BlockSpec((1, 8, LANES), lambda i: (i, 0, 0))],
        out_specs=pl.BlockSpec((8, tm), lambda i: (0, i)),
        out_shape=jax.ShapeDtypeStruct((8, t), jnp.int32),
        compiler_params=_cparams("parallel"),
        name="moe_dest",
    )(route, base)
    return out[0:2].reshape(1, 2 * t)


SC_WINDOW = 128


def _sc_mesh():
    return plsc.VectorSubcoreMesh(core_axis_name="core", subcore_axis_name="subcore")


def _sc_scatter_rows(x, idx, n_out):
    n, d = x.shape
    m = idx.shape[1]
    nblk = n // SC_WINDOW

    @pl.kernel(out_type=jax.ShapeDtypeStruct((n_out, d), x.dtype), mesh=_sc_mesh(), name="moe_sc_scatter")
    def scatter(x_hbm, i_hbm, o_hbm):
        def body(x_vmem, i_vmem):
            pltpu.sync_copy(x_vmem, o_hbm.at[i_vmem.at[0]])

        half = m // SC_WINDOW // 2
        pltpu.emit_pipeline(
            body,
            grid=(2, half),
            in_specs=[pl.BlockSpec((SC_WINDOW, d), lambda c, j: ((c * half + j) % nblk, 0)),
                      pl.BlockSpec((1, SC_WINDOW), lambda c, j: (0, c * half + j))],
            out_specs=[],
            core_axis_name=("core", "subcore"),
            dimension_semantics=(pltpu.PARALLEL, pltpu.PARALLEL),
        )(x_hbm, i_hbm)

    return scatter(x, idx)


def _sc_gather_rows(table, idx):
    d = table.shape[1]
    m = idx.shape[1]

    @pl.kernel(out_type=jax.ShapeDtypeStruct((m, d), table.dtype), mesh=_sc_mesh(), name="moe_sc_gather")
    def gather(t_hbm, i_hbm, o_hbm):
        def body(i_vmem, o_vmem):
            pltpu.sync_copy(t_hbm.at[i_vmem.at[0]], o_vmem)

        half = m // SC_WINDOW // 2
        pltpu.emit_pipeline(
            body,
            grid=(2, half),
            in_specs=[pl.BlockSpec((1, SC_WINDOW), lambda c, j: (0, c * half + j))],
            out_specs=[pl.BlockSpec((SC_WINDOW, d), lambda c, j: (c * half + j, 0))],
            core_axis_name=("core", "subcore"),
            dimension_semantics=(pltpu.PARALLEL, pltpu.PARALLEL),
        )(i_hbm, o_hbm)

    return gather(table, idx)


def _expert_kernel(blk_expert_ref, n_used_ref, xa_ref, xb_ref, wg_ref, wu_ref, wd_ref, ya_ref, yb_ref):
    del blk_expert_ref

    @pl.when(pl.program_id(0) < n_used_ref[0])
    def _():
        xe = _unpack_rows(xa_ref[...], xb_ref[...]).astype(BF16)
        act = _silu(jnp.dot(xe, wg_ref[0], preferred_element_type=F32)) * \
            jnp.dot(xe, wu_ref[0], preferred_element_type=F32)
        y = jnp.dot(act.astype(BF16), wd_ref[0], preferred_element_type=F32)
        ya_ref[...], yb_ref[...] = _pack_rows(y)

    @pl.when(pl.program_id(0) >= n_used_ref[0])
    def _():
        ya_ref[...] = jnp.zeros(ya_ref.shape, ya_ref.dtype)
        yb_ref[...] = jnp.zeros(yb_ref.shape, yb_ref.dtype)


def _experts(blk_expert, n_used, xa, xb, wg, wu, wd):
    n_rows, q = xa.shape
    ff = wg.shape[2]
    blk = MOE_BLOCK
    row = pl.BlockSpec((blk, q), lambda b, be, nu: (b, 0))
    grid_spec = pltpu.PrefetchScalarGridSpec(
        num_scalar_prefetch=2,
        grid=(n_rows // blk,),
        in_specs=[row, row,
                  pl.BlockSpec((1, D_MODEL, ff), lambda b, be, nu: (be[b], 0, 0)),
                  pl.BlockSpec((1, D_MODEL, ff), lambda b, be, nu: (be[b], 0, 0)),
                  pl.BlockSpec((1, ff, D_MODEL), lambda b, be, nu: (be[b], 0, 0))],
        out_specs=[row, row])
    return pl.pallas_call(
        _expert_kernel,
        grid_spec=grid_spec,
        out_shape=[jax.ShapeDtypeStruct((n_rows, q), jnp.uint32)] * 2,
        compiler_params=_cparams("arbitrary"),
        name="moe_experts",
    )(blk_expert, n_used, xa, xb, wg, wu, wd)


def _combine_kernel(x_ref, route_ref, a1_ref, b1_ref, a2_ref, b2_ref, o_ref):
    g1 = route_ref[:, 2:3]
    g2 = route_ref[:, 3:4]
    o_ref[...] = x_ref[...] + g1 * _unpack_rows(a1_ref[...], b1_ref[...]) \
        + g2 * _unpack_rows(a2_ref[...], b2_ref[...])


def _combine(x2d, route, ya, yb):
    t = x2d.shape[0]
    tm = ROW_TILE
    nt = t // tm
    q = ya.shape[1]
    row = lambda w: pl.BlockSpec((tm, w), lambda i: (i, 0))
    first = pl.BlockSpec((tm, q), lambda i: (i, 0))
    second = pl.BlockSpec((tm, q), lambda i: (nt + i, 0))
    return pl.pallas_call(
        _combine_kernel,
        grid=(nt,),
        in_specs=[row(D_MODEL), row(LANES), first, first, second, second],
        out_specs=row(D_MODEL),
        out_shape=jax.ShapeDtypeStruct((t, D_MODEL), F32),
        compiler_params=_cparams("parallel"),
        name="moe_combine",
    )(x2d, route, ya, yb, ya, yb)


def _moe_sorted(x2d, g, wr, br, wg, wu, wd):
    t = x2d.shape[0]
    blk = MOE_BLOCK
    n_blocks = 2 * t // blk + N_EXPERTS
    ha, hb, route, cnt = _route(x2d, g, wr, br)

    cnt = cnt[:, 0, :N_EXPERTS].astype(jnp.int32)
    before_chunk = jnp.cumsum(cnt, axis=0) - cnt
    seg_blocks = (jnp.sum(cnt, axis=0) + blk - 1) // blk
    seg_end_blk = jnp.cumsum(seg_blocks)
    seg_start = (seg_end_blk - seg_blocks) * blk
    base = jnp.pad((seg_start[None, :] + before_chunk).astype(F32), ((0, 0), (0, LANES - N_EXPERTS)))
    idx = _destinations(route, jnp.broadcast_to(base[:, None, :], (base.shape[0], 8, LANES)))
    past_end = jnp.arange(n_blocks, dtype=jnp.int32)[:, None] >= seg_end_blk[None, :]
    blk_expert = jnp.minimum(jnp.sum(past_end, axis=1), N_EXPERTS - 1).astype(jnp.int32)
    n_used = seg_end_blk[-1:].astype(jnp.int32)

    xa = _sc_scatter_rows(ha, idx, n_blocks * blk)
    xb = _sc_scatter_rows(hb, idx, n_blocks * blk)
    ya, yb = _experts(blk_expert, n_used, xa, xb, wg, wu, wd)
    return _combine(x2d, route, _sc_gather_rows(ya, idx), _sc_gather_rows(yb, idx))


def _pad_cols(w, width):
    return jnp.pad(w, ((0, 0), (0, width - w.shape[1])))


def _layer_params(l, mix_norm, w_in, b_forget, a_q_norm, a_k_norm, b_v_norm, b_spatial_w, b_spatial_b,
                  c_q_lat_norm, c_w_uq, c_kv_lat_norm, c_w_ukv, c_q_nope_norm, c_q_rope_norm,
                  c_k_nope_norm, c_k_rope_norm, out_norm_a, out_norm_b, out_norm_c, w_out,
                  xattn_norm, mem_norm, w_mem_q, w_mem_kv, m_q_norm, m_k_norm, w_mem_out):
    p = {}
    o = np.cumsum((0, A_W, A_W, A_W, A_HEADS, B_W, B_W, C_Q_RANK, C_KV_RANK, C_ROPE_DIM))
    w = w_in[l]
    seg = lambda n: w[:, o[n]:o[n + 1]]
    fa = seg(3)
    misc = jnp.zeros((D_MODEL, LANES), F32)
    misc = misc.at[:, ROPE_LANE:ROPE_LANE + C_ROPE_DIM].set(seg(8))
    fb = jnp.zeros((1, LANES), F32)
    for hd in range(A_HEADS):
        ln = FORGET_LANE + 8 * (hd // 2) + hd % 2
        misc = misc.at[:, ln].set(fa[:, hd])
        fb = fb.at[0, ln].set(b_forget[l, hd])
    p["w_in"] = jnp.concatenate([seg(0), seg(1), seg(2), seg(4), seg(5), seg(6), seg(7), misc], axis=1).astype(BF16)
    p["fb"] = fb
    p["mix_g"] = mix_norm[l][None]
    p["aq"] = jnp.tile(a_q_norm[l], A_HEADS)[None] * (A_HEAD_DIM ** -0.5 * LOG2E)
    p["ak"] = jnp.tile(a_k_norm[l], A_HEADS)[None]
    p["bvg"] = b_v_norm[l][None]
    pos = np.arange(B_WINDOW)
    mask = (pos[None, :] // CHUNK) <= (pos[:, None] // CHUNK)
    p["ws"] = jnp.where(mask[None], b_spatial_w[l], 0.0).reshape(B_GROUPS * B_WINDOW, B_WINDOW).astype(BF16)
    p["bs"] = jnp.repeat(b_spatial_b[l].T, B_GROUP_DIM, axis=1)
    p["onb"] = out_norm_b[l][None]
    p["cqg"] = c_q_lat_norm[l][None]
    p["ckvg"] = c_kv_lat_norm[l][None]
    gidx = np.arange(A_W) // A_HEAD_DIM
    p["gm"] = jnp.asarray((gidx[:, None] == gidx[None, :]).astype(np.float32) / A_HEAD_DIM).astype(BF16)
    p["tri"] = jnp.asarray(np.tril(np.ones((ROW_TILE, ROW_TILE), np.float32))).astype(BF16)

    qd = C_NOPE_DIM + C_ROPE_DIM
    half = C_ROPE_DIM // 2
    wq = c_w_uq[l]
    wq_partner = jnp.concatenate([jnp.zeros_like(wq[:, :, :C_NOPE_DIM]), wq[:, :, C_NOPE_DIM + half:],
                                  wq[:, :, C_NOPE_DIM:C_NOPE_DIM + half]], axis=-1)
    p["wuq"] = jnp.pad(jnp.stack([wq, wq_partner]), ((0, 0), (0, 0), (0, 0), (0, HEAD_SLAB - qd))
                       ).reshape(2, C_Q_RANK, -1).astype(BF16)
    wukv = c_w_ukv[l]
    p["wuk"] = jnp.pad(wukv[:, :, :C_NOPE_DIM], ((0, 0), (0, 0), (0, HEAD_SLAB - C_NOPE_DIM))
                       ).reshape(C_KV_RANK, -1).astype(BF16)
    p["wuv"] = wukv[:, :, C_NOPE_DIM:].reshape(C_KV_RANK, C_W).astype(BF16)
    gq = jnp.concatenate([c_q_nope_norm[l], c_q_rope_norm[l]])
    gq_partner = jnp.concatenate([jnp.zeros_like(c_q_nope_norm[l]), c_q_rope_norm[l][half:],
                                  c_q_rope_norm[l][:half]])
    p["gq"] = _pad_cols(jnp.stack([gq, gq_partner]) * (qd ** -0.5 * LOG2E), LANES)
    p["gkn"] = _pad_cols(c_k_nope_norm[l][None], LANES)
    seg = np.zeros((LANES, LANES), np.float32)
    seg[:C_NOPE_DIM, :C_NOPE_DIM] = 1.0 / C_NOPE_DIM
    seg[ROPE_LANE:ROPE_LANE + C_ROPE_DIM, ROPE_LANE:ROPE_LANE + C_ROPE_DIM] = 1.0 / C_ROPE_DIM
    p["seg"] = jnp.asarray(seg).astype(BF16)
    p["gkr"] = jnp.zeros((1, LANES), F32).at[0, ROPE_LANE:ROPE_LANE + C_ROPE_DIM].set(c_k_rope_norm[l])

    p["ona"] = out_norm_a[l][None]
    p["onc"] = out_norm_c[l][None]
    p["w_out"] = w_out[l].astype(BF16)
    p["xg"] = xattn_norm[l][None]
    p["w_mem_q"] = w_mem_q[l].astype(BF16)
    p["mqg"] = m_q_norm[l][None] * (M_HEAD_DIM ** -0.5)
    p["mem_g"] = mem_norm[l][None]
    p["w_mem_kv"] = w_mem_kv[l].astype(BF16)
    p["mkg"] = m_k_norm[l][None]
    p["w_mem_out"] = w_mem_out[l].astype(BF16)
    return p


def kernel(x, mem, positions, mix_norm, w_in, b_forget, a_q_norm, a_k_norm, b_v_norm, b_spatial_w, b_spatial_b, c_q_lat_norm, c_w_uq, c_kv_lat_norm, c_w_ukv, c_q_nope_norm, c_q_rope_norm, c_k_nope_norm, c_k_rope_norm, out_norm_a, out_norm_b, out_norm_c, w_out, xattn_norm, mem_norm, w_mem_q, w_mem_kv, m_q_norm, m_k_norm, w_mem_out, ffn_norm, ffn_w_gate, ffn_w_up, ffn_w_down, w_router, b_router, moe_w_gate, moe_w_up, moe_w_down):
    nb, seq, d = x.shape
    assert d == D_MODEL and seq % (2 * ROW_TILE) == 0 and seq % ATTN_TILE == 0
    assert (nb * seq) % MOE_CHUNK == 0 and (nb * seq) % SC_WINDOW == 0 and (2 * nb * seq) % MOE_BLOCK == 0
    depth = w_in.shape[0]
    t = nb * seq
    x2d = x.reshape(t, d)
    cos, sin = _rope_tables(positions.reshape(t, 1).astype(F32))

    for l in range(depth):
        p = _layer_params(l, mix_norm, w_in, b_forget, a_q_norm, a_k_norm, b_v_norm, b_spatial_w,
                          b_spatial_b, c_q_lat_norm, c_w_uq, c_kv_lat_norm, c_w_ukv, c_q_nope_norm,
                          c_q_rope_norm, c_k_nope_norm, c_k_rope_norm, out_norm_a, out_norm_b,
                          out_norm_c, w_out, xattn_norm, mem_norm, w_mem_q, w_mem_kv, m_q_norm,
                          m_k_norm, w_mem_out)
        qa, ka, vta, bn, cq, ckv, misc = _inproj(x2d, seq, p)
        qc, kc, vtc = _mla_prep(cq, ckv, misc, cos, sin, seq, p)
        a = _attention(qa, ka, vta, unit=1, name="attn_fox")
        c = _attention(qc, kc, vtc, unit=CHUNK, name="attn_mla")
        km, vm = _mem_kv(mem, p)
        x2d = _outproj(x2d, a, bn, c, km, vm, seq, p)
        g = ffn_norm[l][None]
        if l % 2 == 0:
            m = l // 2
            ff = ffn_w_gate.shape[2]
            ff_pad = -(-ff // (2 * LANES)) * (2 * LANES)
            wg = _pad_cols(ffn_w_gate[m], ff_pad).astype(BF16)
            wu = _pad_cols(ffn_w_up[m], ff_pad).astype(BF16)
            wd = jnp.pad(ffn_w_down[m], ((0, ff_pad - ff), (0, 0))).astype(BF16)
            x2d = _ffn(x2d, g, wg, wu, wd)
        else:
            m = l // 2
            wr = _pad_cols(w_router[m], LANES)
            wr_hi = wr.astype(BF16)
            wr = jnp.stack([wr_hi, (wr - wr_hi.astype(F32)).astype(BF16)])
            br = _pad_cols(b_router[m][None], LANES)
            x2d = _moe_sorted(x2d, g, wr, br, moe_w_gate[m].astype(BF16), moe_w_up[m].astype(BF16),
                              moe_w_down[m].astype(BF16))
    return x2d.reshape(nb, seq, d)
```

```python
import functools

import numpy as np
import jax
import jax.numpy as jnp
from jax import lax
from jax.experimental import pallas as pl
from jax.experimental.pallas import tpu as pltpu
from jax.experimental.pallas import tpu_sc as plsc

F32 = jnp.float32
BF16 = jnp.bfloat16

D_MODEL = 1024
CHUNK = 64
EPS = 1e-6
NEG_INF = -1e30
A_HEADS, A_HEAD_DIM = 4, 64
B_GROUPS, B_GROUP_DIM, B_WINDOW = 4, 64, 128
C_HEADS, C_NOPE_DIM, C_ROPE_DIM, C_V_DIM = 8, 64, 32, 64
C_Q_RANK, C_KV_RANK = 256, 128
ROPE_THETA = 10000.0
M_HEADS, M_HEAD_DIM = 4, 128
N_EXPERTS = 8
A_W = A_HEADS * A_HEAD_DIM
B_W = B_GROUPS * B_GROUP_DIM
C_W = C_HEADS * C_V_DIM
M_W = M_HEADS * M_HEAD_DIM

LANES = 128

SEG_Q, SEG_K, SEG_V, SEG_U, SEG_VB, SEG_CQ, SEG_CKV, SEG_MISC = 0, 256, 512, 768, 1024, 1280, 1536, 1664
IN_PAD_W = SEG_MISC + LANES
ROPE_LANE = C_NOPE_DIM
FORGET_LANE = 96
HEAD_SLAB = LANES
VT_ROWS = 80
LOG2E = float(np.log2(np.e))

ROW_TILE = 512
ATTN_TILE = 512
ATTN_HEADS = 2
MOE_CHUNK = 1024
MOE_BLOCK = 512
VMEM_LIMIT = 56 * 1024 * 1024


def _cparams(*sem):
    return pltpu.CompilerParams(dimension_semantics=sem, vmem_limit_bytes=VMEM_LIMIT)


def _full(shape):
    n = len(shape)
    return pl.BlockSpec(shape, lambda *_: (0,) * n)


def _rms(x):
    return x * lax.rsqrt(jnp.mean(x * x, axis=-1, keepdims=True) + EPS)


def _dot_split(v, exact, pieces, lhs_is_exact=False):
    total = None
    rem = v
    for n in range(pieces):
        part = rem.astype(BF16)
        if n + 1 < pieces:
            rem = rem - part.astype(F32)
        term = (jnp.dot(exact, part, preferred_element_type=F32) if lhs_is_exact
                else jnp.dot(part, exact, preferred_element_type=F32))
        total = term if total is None else total + term
    return total


def _lane_iota(n=LANES):
    return lax.broadcasted_iota(jnp.int32, (1, n), 1)


def _rope_table_kernel(pos_ref, inv_ref, sgn_ref, cos_ref, sin_ref):
    ang = pos_ref[...] * inv_ref[...]
    cos_ref[...] = jnp.cos(ang)
    sin_ref[...] = jnp.sin(ang) * sgn_ref[...]


def _rope_tables(pos_col):
    t = pos_col.shape[0]
    half = C_ROPE_DIM // 2
    inv = ROPE_THETA ** (-jnp.arange(half, dtype=F32) / half)
    inv_l = jnp.zeros((1, LANES), F32).at[0, ROPE_LANE:ROPE_LANE + C_ROPE_DIM].set(jnp.tile(inv, 2))
    sgn = np.zeros((1, LANES), np.float32)
    sgn[0, ROPE_LANE:ROPE_LANE + half] = -1.0
    sgn[0, ROPE_LANE + half:ROPE_LANE + C_ROPE_DIM] = 1.0
    tm = ROW_TILE
    return pl.pallas_call(
        _rope_table_kernel,
        grid=(t // tm,),
        in_specs=[pl.BlockSpec((tm, 1), lambda i: (i, 0)), _full((1, LANES)), _full((1, LANES))],
        out_specs=[pl.BlockSpec((tm, LANES), lambda i: (i, 0))] * 2,
        out_shape=[jax.ShapeDtypeStruct((t, LANES), F32)] * 2,
        compiler_params=_cparams("parallel"),
        name="rope_tables",
    )(pos_col, inv_l, jnp.asarray(sgn))


def _rotate(x, cos, sin_signed, lane):
    half = C_ROPE_DIM // 2
    partner = jnp.where(lane < ROPE_LANE + half,
                        pltpu.roll(x, LANES - half, 1), pltpu.roll(x, half, 1))
    return x * cos + partner * sin_signed


def _store_v_transposed(vt_ref, v, n_heads, lanes=slice(None)):
    tm = v.shape[0]
    v_t = v.T
    tail = jnp.where(lax.broadcasted_iota(jnp.int32, (VT_ROWS - C_V_DIM, tm), 0) == 0, 1.0, 0.0).astype(BF16)
    for hd in range(n_heads):
        vt_ref[0, hd, 0:C_V_DIM, lanes] = v_t[hd * C_V_DIM:(hd + 1) * C_V_DIM, :].astype(BF16)
        vt_ref[0, hd, C_V_DIM:VT_ROWS, lanes] = tail


def _gelu(x):
    return 0.5 * x * (1.0 + lax.erf(x * np.float32(1.0 / np.sqrt(2.0))))


def _inproj_kernel(x0_ref, xa_ref, xb_ref, g_ref, w_ref, aq_ref, ak_ref, fb_ref, bvg_ref, ws_ref, bs_ref,
                   onb_ref, cqg_ref, ckvg_ref, gm_ref, tri_ref,
                   qa_ref, ka_ref, vt_ref, bn_ref, cq_ref, ckv_ref, misc_ref,
                   carry_ref, buf0_ref, buf1_ref, *, tiles_per_seq):
    i = pl.program_id(0)
    tm = xa_ref.shape[0]

    def projection_parts(x_ref, buf_ref):
        h = (_rms(x_ref[...]) * g_ref[...]).astype(BF16)

        def part(lo, hi):
            def run():
                buf_ref[:, lo:hi] = jnp.dot(h, w_ref[:, lo:hi], preferred_element_type=F32)
            return run
        return [part(SEG_Q, SEG_U), part(SEG_U, SEG_CQ), part(SEG_CQ, IN_PAD_W)]

    @pl.when(i == 0)
    def _():
        for run in projection_parts(x0_ref, buf0_ref):
            run()

    refs = (aq_ref, ak_ref, fb_ref, bvg_ref, ws_ref, bs_ref, onb_ref, cqg_ref, ckvg_ref, gm_ref, tri_ref,
            qa_ref, ka_ref, vt_ref, bn_ref, cq_ref, ckv_ref, misc_ref, carry_ref)
    _mixer_prologues(buf0_ref, 0, 2 * i, tm, tiles_per_seq, projection_parts(xa_ref, buf1_ref), *refs)
    _mixer_prologues(buf1_ref, 1, 2 * i + 1, tm, tiles_per_seq, projection_parts(xb_ref, buf0_ref), *refs)


def _mixer_prologues(proj, half, tile, tm, tiles_per_seq, between, aq_ref, ak_ref, fb_ref, bvg_ref, ws_ref,
                     bs_ref, onb_ref, cqg_ref, ckvg_ref, gm_ref, tri_ref, qa_ref, ka_ref, vt_ref, bn_ref,
                     cq_ref, ckv_ref, misc_ref, carry_ref):
    out_rows = slice(half * tm, (half + 1) * tm)
    gm = gm_ref[...]

    def group_mean(v):
        return _dot_split(v, gm, 2)

    misc = proj[:, SEG_MISC:SEG_MISC + LANES]
    misc_ref[out_rows, :] = misc
    z = misc + fb_ref[...]
    log_f = jnp.minimum(z, 0.0) - jnp.log1p(jnp.exp(-jnp.abs(z)))
    carry = jnp.where(tile % tiles_per_seq == 0, 0.0, carry_ref[...])
    cum = _dot_split(log_f, tri_ref[...], 3, lhs_is_exact=True) + carry
    carry_ref[...] = cum[tm - 1:tm, :]
    f_hi = (cum * LOG2E).astype(BF16).astype(F32)
    f_rem = cum * LOG2E - f_hi
    f_mid = f_rem.astype(BF16).astype(F32)
    f_lo = f_rem - f_mid

    between[0]()
    q = proj[:, SEG_Q:SEG_Q + A_W]
    qn = q * lax.rsqrt(group_mean(q * q) + EPS) * aq_ref[...]
    k = proj[:, SEG_K:SEG_K + A_W]
    kn = k * lax.rsqrt(group_mean(k * k) + EPS) * ak_ref[...]
    lane = _lane_iota()
    for hd in range(A_HEADS):
        pair = slice((hd // 2) * LANES, (hd // 2 + 1) * LANES)
        slab = slice(hd * HEAD_SLAB, (hd + 1) * HEAD_SLAB)
        data = (lane < A_HEAD_DIM) if hd % 2 == 0 else (lane >= A_HEAD_DIM)
        e0 = A_HEAD_DIM if hd % 2 == 0 else 0
        fl = FORGET_LANE + 8 * (hd // 2) + hd % 2
        ones = jnp.where((lane >= e0) & (lane < e0 + 3), 1.0, 0.0)
        qa_ref[out_rows, slab] = jnp.where(data, qn[:, pair], ones).astype(BF16)
        bias = jnp.where(lane == e0, -f_hi[:, fl:fl + 1],
                         jnp.where(lane == e0 + 1, -f_mid[:, fl:fl + 1],
                                   jnp.where(lane == e0 + 2, -f_lo[:, fl:fl + 1], 0.0)))
        ka_ref[out_rows, slab] = jnp.where(data, kn[:, pair], bias).astype(BF16)
    _store_v_transposed(vt_ref, proj[:, SEG_V:SEG_V + A_W], A_HEADS, out_rows)

    between[1]()
    u = _gelu(proj[:, SEG_U:SEG_U + B_W])
    v = _gelu(proj[:, SEG_VB:SEG_VB + B_W])
    dv = v - group_mean(v)
    vn = dv * lax.rsqrt(group_mean(dv * dv) + EPS) * bvg_ref[...]
    group = lax.broadcasted_iota(jnp.int32, (1, B_W), 1) // B_GROUP_DIM
    for w in range(tm // B_WINDOW):
        rows = slice(w * B_WINDOW, (w + 1) * B_WINDOW)
        y_all = jnp.dot(ws_ref[...], vn[rows].astype(BF16), preferred_element_type=F32)
        y = bs_ref[...]
        for g in range(B_GROUPS):
            y = y + jnp.where(group == g, y_all[g * B_WINDOW:(g + 1) * B_WINDOW], 0.0)
        b = u[rows] * y
        bn_ref[half * tm + w * B_WINDOW:half * tm + (w + 1) * B_WINDOW, :] = (_rms(b) * onb_ref[...]).astype(BF16)

    between[2]()
    cq_ref[out_rows, :] = (_rms(proj[:, SEG_CQ:SEG_CQ + C_Q_RANK]) * cqg_ref[...]).astype(BF16)
    ckv_ref[out_rows, :] = (_rms(proj[:, SEG_CKV:SEG_CKV + C_KV_RANK]) * ckvg_ref[...]).astype(BF16)


def _inproj(x2d, seq, p):
    t = x2d.shape[0]
    tm = ROW_TILE
    tps = seq // tm
    nb = t // seq
    n_tiles = t // tm
    steps_per_seq = tps // 2
    row = lambda w: pl.BlockSpec((2 * tm, w), lambda i: (i, 0))
    x_tile = lambda index: pl.BlockSpec((tm, D_MODEL), lambda i: (index(i), 0))
    qk_w = A_HEADS * HEAD_SLAB
    out_shape = [
        jax.ShapeDtypeStruct((t, qk_w), BF16), jax.ShapeDtypeStruct((t, qk_w), BF16),
        jax.ShapeDtypeStruct((nb, A_HEADS, VT_ROWS, seq), BF16),
        jax.ShapeDtypeStruct((t, B_W), BF16),
        jax.ShapeDtypeStruct((t, C_Q_RANK), BF16), jax.ShapeDtypeStruct((t, C_KV_RANK), BF16),
        jax.ShapeDtypeStruct((t, LANES), F32),
    ]
    out_specs = [row(qk_w), row(qk_w),
                 pl.BlockSpec((1, A_HEADS, VT_ROWS, 2 * tm),
                              lambda i: (i // steps_per_seq, 0, 0, i % steps_per_seq)),
                 row(B_W), row(C_Q_RANK), row(C_KV_RANK), row(LANES)]
    consts = [p["mix_g"], p["w_in"], p["aq"], p["ak"], p["fb"], p["bvg"], p["ws"], p["bs"], p["onb"],
              p["cqg"], p["ckvg"], p["gm"], p["tri"]]
    return pl.pallas_call(
        functools.partial(_inproj_kernel, tiles_per_seq=tps),
        grid=(n_tiles // 2,),
        in_specs=[x_tile(lambda i: 0), x_tile(lambda i: 2 * i + 1),
                  x_tile(lambda i: jnp.minimum(2 * i + 2, n_tiles - 1))] + [_full(c.shape) for c in consts],
        out_specs=out_specs,
        out_shape=out_shape,
        scratch_shapes=[pltpu.VMEM((1, LANES), F32), pltpu.VMEM((tm, IN_PAD_W), F32),
                        pltpu.VMEM((tm, IN_PAD_W), F32)],
        compiler_params=_cparams("arbitrary"),
        name="in_proj",
    )(x2d, x2d, x2d, *consts)


def _mla_prep_kernel(cq_ref, ckv_ref, misc_ref, cos_ref, sin_ref, wuq_ref, wuk_ref, wuv_ref,
                     gq_ref, gkn_ref, gkr_ref, seg_ref, qc_ref, kc_ref, vt_ref):
    tm = cq_ref.shape[0] // 2
    tiles = [slice(0, tm), slice(tm, 2 * tm)]
    lane = _lane_iota()
    rope = (lane >= ROPE_LANE) & (lane < ROPE_LANE + C_ROPE_DIM)
    seg = seg_ref[...]

    def inv_rms(v):
        return lax.rsqrt(jnp.dot((v * v).astype(BF16), seg, preferred_element_type=F32) + EPS)

    q, q_partner, kn, q_cos, q_sin, kr = [], [], [], [], [], []
    for rows in tiles:
        cos, sin = cos_ref[rows, :], sin_ref[rows, :]
        q.append(jnp.dot(cq_ref[rows, :], wuq_ref[0], preferred_element_type=F32))
        q_partner.append(jnp.dot(cq_ref[rows, :], wuq_ref[1], preferred_element_type=F32))
        q_cos.append(gq_ref[0:1, :] * cos)
        q_sin.append(gq_ref[1:2, :] * sin)
        kn.append(jnp.dot(ckv_ref[rows, :], wuk_ref[...], preferred_element_type=F32))
        _store_v_transposed(vt_ref, jnp.dot(ckv_ref[rows, :], wuv_ref[...], preferred_element_type=F32),
                            C_HEADS, rows)
        k_rot = jnp.where(rope, misc_ref[rows, :], 0.0)
        kr.append(_rotate(k_rot * inv_rms(k_rot) * gkr_ref[...], cos, sin, lane))

    for hd in range(C_HEADS):
        cols = slice(hd * HEAD_SLAB, (hd + 1) * HEAD_SLAB)
        for t, rows in enumerate(tiles):
            qh = q[t][:, cols]
            qc_ref[rows, cols] = (inv_rms(qh) * (qh * q_cos[t] + q_partner[t][:, cols] * q_sin[t])).astype(BF16)
            kh = kn[t][:, cols]
            kc_ref[rows, cols] = (kh * inv_rms(kh) * gkn_ref[...] + kr[t]).astype(BF16)


def _mla_prep(cq, ckv, misc, cos, sin, seq, p):
    t = cq.shape[0]
    tm = 2 * ROW_TILE
    tps = seq // tm
    row = lambda w: pl.BlockSpec((tm, w), lambda i: (i, 0))
    consts = [p["wuq"], p["wuk"], p["wuv"], p["gq"], p["gkn"], p["gkr"], p["seg"]]
    qk_w = C_HEADS * HEAD_SLAB
    return pl.pallas_call(
        _mla_prep_kernel,
        grid=(t // tm,),
        in_specs=[row(C_Q_RANK), row(C_KV_RANK), row(LANES), row(LANES), row(LANES)]
                 + [_full(c.shape) for c in consts],
        out_specs=[row(qk_w), row(qk_w),
                   pl.BlockSpec((1, C_HEADS, VT_ROWS, tm), lambda i: (i // tps, 0, 0, i % tps))],
        out_shape=[jax.ShapeDtypeStruct((t, qk_w), BF16), jax.ShapeDtypeStruct((t, qk_w), BF16),
                   jax.ShapeDtypeStruct((t // seq, C_HEADS, VT_ROWS, seq), BF16)],
        compiler_params=_cparams("parallel"),
        name="mla_prep",
    )(cq, ckv, misc, cos, sin, *consts)


def _attn_items(nq):
    return [(i, j) for i in range(nq) for j in range(i + 1)]


def _attn_kernel(q_ref, k_ref, vt_ref, mask_ref, o_ref, s0_ref, s1_ref, p0_ref, p1_ref,
                 mp0_ref, mp1_ref, mrun_ref, macc_ref, acc_ref):
    tk, tq = mask_ref.shape
    nq = q_ref.shape[0] // tq
    n_h = q_ref.shape[1] // HEAD_SLAB
    items = _attn_items(nq)
    n_items = len(items)
    s_bufs, p_bufs, mp_bufs = (s0_ref, s1_ref), (p0_ref, p1_ref), (mp0_ref, mp1_ref)
    mrun_ref[...] = jnp.full(mrun_ref.shape, NEG_INF, F32)
    macc_ref[...] = jnp.full(macc_ref.shape, NEG_INF, F32)
    acc_ref[...] = jnp.zeros(acc_ref.shape, F32)

    half = tk // 2
    nt_dims = (((1,), (1,)), ((), ()))
    top, bot, left, right = slice(0, half), slice(half, tk), slice(0, half), slice(half, tq)

    def scores(it, buf):
        qi, kj = items[it]
        k_rows, q_rows = kj * tk, qi * tq
        for hh in range(n_h):
            cols = slice(hh * HEAD_SLAB, (hh + 1) * HEAD_SLAB)
            if qi != kj:
                s_t = lax.dot_general(k_ref[k_rows:k_rows + tk, cols], q_ref[q_rows:q_rows + tq, cols],
                                      nt_dims, preferred_element_type=F32)
                s_bufs[buf][hh] = s_t
                col_max = jnp.max(s_t, axis=0, keepdims=True)
            else:
                s_top = lax.dot_general(k_ref[k_rows:k_rows + half, cols], q_ref[q_rows:q_rows + tq, cols],
                                        nt_dims, preferred_element_type=F32)
                s_top = jnp.concatenate([s_top[:, left] + mask_ref[top, left], s_top[:, right]], axis=1)
                s_br = lax.dot_general(k_ref[k_rows + half:k_rows + tk, cols],
                                       q_ref[q_rows + half:q_rows + tq, cols],
                                       nt_dims, preferred_element_type=F32) + mask_ref[bot, right]
                s_bufs[buf][hh, top, :] = s_top
                s_bufs[buf][hh, bot, right] = s_br
                max_top = jnp.max(s_top, axis=0, keepdims=True)
                col_max = jnp.concatenate(
                    [max_top[:, left], jnp.maximum(max_top[:, right], jnp.max(s_br, axis=0, keepdims=True))],
                    axis=1)
            mrun_ref[qi, hh] = jnp.maximum(mrun_ref[qi, hh], col_max)

    def exponentiate(it, buf):
        qi, kj = items[it]
        for hh in range(n_h):
            m = mrun_ref[qi, hh]
            if qi != kj:
                p_bufs[buf][hh] = jnp.exp2(s_bufs[buf][hh] - m).astype(BF16)
            else:
                p_bufs[buf][hh, top, :] = jnp.exp2(s_bufs[buf][hh, top, :] - m).astype(BF16)
                p_bufs[buf][hh, bot, right] = jnp.exp2(s_bufs[buf][hh, bot, right] - m[:, right]).astype(BF16)
            mp_bufs[buf][hh] = m

    def accumulate(it, buf):
        qi, kj = items[it]
        k_rows = kj * tk
        for hh in range(n_h):
            m = mp_bufs[buf][hh]
            if qi != kj:
                pv = jnp.dot(vt_ref[0, hh, :, k_rows:k_rows + tk], p_bufs[buf][hh],
                             preferred_element_type=F32)
            else:
                pv = jnp.concatenate(
                    [jnp.dot(vt_ref[0, hh, :, k_rows:k_rows + half], p_bufs[buf][hh, top, left],
                             preferred_element_type=F32),
                     jnp.dot(vt_ref[0, hh, :, k_rows:k_rows + tk], p_bufs[buf][hh, :, right],
                             preferred_element_type=F32)], axis=1)
            acc_ref[qi, hh] = jnp.exp2(macc_ref[qi, hh] - m) * acc_ref[qi, hh] + pv
            macc_ref[qi, hh] = m

    for it in range(n_items + 2):
        par = it % 2
        if 2 <= it:
            accumulate(it - 2, par)
            qi, kj = items[it - 2]
            if qi == kj:
                halves = [acc_ref[qi, hh, 0:C_V_DIM, :] / acc_ref[qi, hh, C_V_DIM:C_V_DIM + 1, :]
                          for hh in range(n_h)]
                o_ref[qi * tq:(qi + 1) * tq, :] = jnp.concatenate(halves, axis=0).T.astype(o_ref.dtype)
        if 1 <= it <= n_items:
            exponentiate(it - 1, 1 - par)
        if it < n_items:
            scores(it, par)


def _attention(q, k, vt, *, unit, name):
    t = q.shape[0]
    nb, n_heads, _, seq = vt.shape
    tq = ATTN_TILE
    nq = seq // tq
    assert (tq // 2) % unit == 0
    pos = np.arange(tq)
    diag_mask = np.where((pos[:, None] // unit) <= (pos[None, :] // unit), 0.0, NEG_INF)
    mask = jnp.asarray(diag_mask.astype(np.float32))
    n_h = ATTN_HEADS
    seq_blk = lambda w: pl.BlockSpec((seq, w), lambda b, p: (b, p))
    return pl.pallas_call(
        _attn_kernel,
        grid=(nb, n_heads // n_h),
        in_specs=[seq_blk(n_h * HEAD_SLAB), seq_blk(n_h * HEAD_SLAB),
                  pl.BlockSpec((1, n_h, VT_ROWS, seq), lambda b, p: (b, p, 0, 0)),
                  pl.BlockSpec((tq, tq), lambda b, p: (0, 0), pipeline_mode=pl.Buffered(1))],
        out_specs=seq_blk(n_h * C_V_DIM),
        out_shape=jax.ShapeDtypeStruct((t, n_heads * C_V_DIM), BF16),
        scratch_shapes=[pltpu.VMEM((n_h, tq, tq), F32), pltpu.VMEM((n_h, tq, tq), F32),
                        pltpu.VMEM((n_h, tq, tq), BF16), pltpu.VMEM((n_h, tq, tq), BF16),
                        pltpu.VMEM((n_h, 1, tq), F32), pltpu.VMEM((n_h, 1, tq), F32),
                        pltpu.VMEM((nq, n_h, 1, tq), F32), pltpu.VMEM((nq, n_h, 1, tq), F32),
                        pltpu.VMEM((nq, n_h, VT_ROWS, tq), F32)],
        compiler_params=_cparams("parallel", "parallel"),
        name=name,
    )(q, k, vt, mask)


def _mem_kv_kernel(mem_ref, g_ref, w_ref, kg_ref, k_ref, v_ref):
    mn = (_rms(mem_ref[0]) * g_ref[...]).astype(BF16)
    kv = jnp.dot(mn, w_ref[...], preferred_element_type=F32)
    for hd in range(M_HEADS):
        cols = slice(hd * M_HEAD_DIM, (hd + 1) * M_HEAD_DIM)
        k_ref[0, :, cols] = (_rms(kv[:, cols]) * kg_ref[...]).astype(BF16)
    v_ref[0] = kv[:, M_W:].astype(BF16)


def _mem_kv(mem, p):
    nb, ml, _ = mem.shape
    consts = [p["mem_g"], p["w_mem_kv"], p["mkg"]]
    blk = pl.BlockSpec((1, ml, M_W), lambda b: (b, 0, 0))
    return pl.pallas_call(
        _mem_kv_kernel,
        grid=(nb,),
        in_specs=[pl.BlockSpec((1, ml, D_MODEL), lambda b: (b, 0, 0))] + [_full(c.shape) for c in consts],
        out_specs=[blk, blk],
        out_shape=[jax.ShapeDtypeStruct((nb, ml, M_W), BF16)] * 2,
        compiler_params=_cparams("parallel"),
        name="mem_kv",
    )(mem, *consts)


def _outproj_kernel(x_ref, a_ref, bn_ref, c_ref, ona_ref, onc_ref, wo_ref, xg_ref, wq_ref, mqg_ref,
                    km_ref, vm_ref, wmo_ref, o_ref):
    tm = x_ref.shape[0] // 2
    tiles = [slice(0, tm), slice(tm, 2 * tm)]
    x1, q, outs = [], [], [[], []]
    for rows in tiles:
        a_n = (_rms(a_ref[rows, :].astype(F32)) * ona_ref[...]).astype(BF16)
        c_n = (_rms(c_ref[rows, :].astype(F32)) * onc_ref[...]).astype(BF16)
        mix = jnp.concatenate([a_n, bn_ref[rows, :], c_n], axis=-1)
        x1.append(x_ref[rows, :] + jnp.dot(mix, wo_ref[...], preferred_element_type=F32))
    for t in range(2):
        h = (_rms(x1[t]) * xg_ref[...]).astype(BF16)
        q.append(jnp.dot(h, wq_ref[...], preferred_element_type=F32))
    for hd in range(M_HEADS):
        cols = slice(hd * M_HEAD_DIM, (hd + 1) * M_HEAD_DIM)
        for t in range(2):
            qh = (_rms(q[t][:, cols]) * mqg_ref[...]).astype(BF16)
            s = lax.dot_general(qh, km_ref[0, :, cols], (((1,), (1,)), ((), ())), preferred_element_type=F32)
            e = jnp.exp(s - jnp.max(s, axis=-1, keepdims=True))
            pr = e / jnp.sum(e, axis=-1, keepdims=True)
            outs[t].append(jnp.dot(pr.astype(BF16), vm_ref[0, :, cols], preferred_element_type=F32).astype(BF16))
    for t, rows in enumerate(tiles):
        o_ref[rows, :] = x1[t] + jnp.dot(jnp.concatenate(outs[t], axis=-1), wmo_ref[...],
                                         preferred_element_type=F32)


def _outproj(x2d, a, bn, c, km, vm, seq, p):
    t = x2d.shape[0]
    tm = 2 * ROW_TILE
    tps = seq // tm
    ml = km.shape[1]
    row = lambda w: pl.BlockSpec((tm, w), lambda i: (i, 0))
    memblk = pl.BlockSpec((1, ml, M_W), lambda i: (i // tps, 0, 0))
    c1 = [p["ona"], p["onc"], p["w_out"], p["xg"], p["w_mem_q"], p["mqg"]]
    return pl.pallas_call(
        _outproj_kernel,
        grid=(t // tm,),
        in_specs=[row(D_MODEL), row(A_W), row(B_W), row(C_W)] + [_full(c_.shape) for c_ in c1]
                 + [memblk, memblk, _full(p["w_mem_out"].shape)],
        out_specs=row(D_MODEL),
        out_shape=jax.ShapeDtypeStruct((t, D_MODEL), F32),
        compiler_params=_cparams("parallel"),
        name="out_proj_mem_attn",
    )(x2d, a, bn, c, *c1, km, vm, p["w_mem_out"])


def _silu(x):
    return x * jax.nn.sigmoid(x)


def _ffn_kernel(x_ref, g_ref, wg_ref, wu_ref, wd_ref, o_ref, *, n_chunks):
    x = x_ref[...]
    h = (_rms(x) * g_ref[...]).astype(BF16)
    fc = wg_ref.shape[1] // n_chunks
    acc = x
    for c in range(n_chunks):
        cols = slice(c * fc, (c + 1) * fc)
        act = _silu(jnp.dot(h, wg_ref[:, cols], preferred_element_type=F32)) * \
            jnp.dot(h, wu_ref[:, cols], preferred_element_type=F32)
        acc = acc + jnp.dot(act.astype(BF16), wd_ref[cols, :], preferred_element_type=F32)
    o_ref[...] = acc


def _ffn(x2d, g, wg, wu, wd):
    t = x2d.shape[0]
    tm = ROW_TILE
    row = pl.BlockSpec((tm, D_MODEL), lambda i: (i, 0))
    resident = lambda a: pl.BlockSpec(a.shape, lambda i: (0, 0), pipeline_mode=pl.Buffered(1))
    return pl.pallas_call(
        functools.partial(_ffn_kernel, n_chunks=2),
        grid=(t // tm,),
        in_specs=[row, _full(g.shape), resident(wg), resident(wu), resident(wd)],
        out_specs=row,
        out_shape=jax.ShapeDtypeStruct((t, D_MODEL), F32),
        compiler_params=_cparams("parallel"),
        name="ffn_dense",
    )(x2d, g, wg, wu, wd)


def _pack_bf16_pairs(lo, hi):
    lo_bits = pltpu.bitcast(lo.astype(BF16).astype(F32), jnp.uint32)
    hi_bits = pltpu.bitcast(hi.astype(BF16).astype(F32), jnp.uint32)
    return lax.shift_right_logical(lo_bits, jnp.uint32(16)) | (hi_bits & jnp.uint32(0xFFFF0000))


def _unpack_bf16_pairs(words):
    lo = pltpu.bitcast(lax.shift_left(words, jnp.uint32(16)), F32)
    hi = pltpu.bitcast(words & jnp.uint32(0xFFFF0000), F32)
    return lo, hi


def _pack_rows(v):
    q = D_MODEL // 4
    return _pack_bf16_pairs(v[:, 0:q], v[:, q:2 * q]), _pack_bf16_pairs(v[:, 2 * q:3 * q], v[:, 3 * q:])


def _unpack_rows(a, b):
    return jnp.concatenate([*_unpack_bf16_pairs(a), *_unpack_bf16_pairs(b)], axis=-1)


def _route_kernel(x_ref, g_ref, wr_ref, br_ref, tri_ref, ha_ref, hb_ref, route_ref, cnt_ref):
    lane = _lane_iota()
    h = _rms(x_ref[...]) * g_ref[...]
    ha_ref[...], hb_ref[...] = _pack_rows(h)
    h_hi = h.astype(BF16)
    h_lo = (h - h_hi.astype(F32)).astype(BF16)
    logits = (jnp.dot(h_hi, wr_ref[0], preferred_element_type=F32)
              + jnp.dot(h_lo, wr_ref[0], preferred_element_type=F32)
              + jnp.dot(h_hi, wr_ref[1], preferred_element_type=F32)) + br_ref[...]
    logits = jnp.where(lane < N_EXPERTS, logits, -jnp.inf)
    v1 = jnp.max(logits, axis=-1, keepdims=True)
    i1 = jnp.min(jnp.where(logits == v1, lane, LANES), axis=-1, keepdims=True)
    rest = jnp.where(lane == i1, -jnp.inf, logits)
    v2 = jnp.max(rest, axis=-1, keepdims=True)
    i2 = jnp.min(jnp.where(rest == v2, lane, LANES), axis=-1, keepdims=True)
    e2 = jnp.exp(v2 - v1)
    g1 = 1.0 / (1.0 + e2)
    hit1, hit2 = lane == i1, lane == i2
    ones = jnp.where(hit1 | hit2, 1.0, 0.0)
    before = jnp.dot(tri_ref[...], ones.astype(BF16), preferred_element_type=F32)
    r1 = jnp.sum(jnp.where(hit1, before, 0.0), axis=-1, keepdims=True)
    r2 = jnp.sum(jnp.where(hit2, before, 0.0), axis=-1, keepdims=True)
    cols = [i1.astype(F32), i2.astype(F32), g1, e2 * g1, r1, r2]
    route = jnp.zeros(route_ref.shape, F32)
    for n, c in enumerate(cols):
        route = jnp.where(lane == n, c, route)
    route_ref[...] = route
    cnt_ref[0] = jnp.broadcast_to(jnp.sum(ones, axis=0, keepdims=True), cnt_ref.shape[1:])


def _route(x2d, g, wr, br):
    t = x2d.shape[0]
    tm = MOE_CHUNK
    tri = jnp.asarray(np.tril(np.ones((tm, tm), np.float32), -1)).astype(BF16)
    row = lambda w: pl.BlockSpec((tm, w), lambda i: (i, 0))
    q = D_MODEL // 4
    return pl.pallas_call(
        _route_kernel,
        grid=(t // tm,),
        in_specs=[row(D_MODEL), _full(g.shape), _full(wr.shape), _full(br.shape), _full(tri.shape)],
        out_specs=[row(q), row(q), row(LANES), pl.BlockSpec((1, 8, LANES), lambda i: (i, 0, 0))],
        out_shape=[jax.ShapeDtypeStruct((t, q), jnp.uint32), jax.ShapeDtypeStruct((t, q), jnp.uint32),
                   jax.ShapeDtypeStruct((t, LANES), F32), jax.ShapeDtypeStruct((t // tm, 8, LANES), F32)],
        compiler_params=_cparams("parallel"),
        name="moe_route",
    )(x2d, g, wr, br, tri)


def _dest_kernel(route_ref, base_ref, o_ref):
    lane = _lane_iota()
    r = route_ref[...]
    base = base_ref[0, 0:1, :]
    lane_f = lane.astype(F32)
    d1 = jnp.sum(jnp.where(lane_f == r[:, 0:1], base, 0.0), axis=-1, keepdims=True) + r[:, 4:5]
    d2 = jnp.sum(jnp.where(lane_f == r[:, 1:2], base, 0.0), axis=-1, keepdims=True) + r[:, 5:6]
    both = jnp.where(lane == 0, d1, jnp.where(lane == 1, d2, 0.0))
    o_ref[...] = both.T[0:8, :].astype(jnp.int32)


def _destinations(route, base):
    t = route.shape[0]
    tm = MOE_CHUNK
    out = pl.pallas_call(
        _dest_kernel,
        grid=(t // tm,),
        in_specs=[pl.BlockSpec((tm, LANES), lambda i: (i, 0)), pl.BlockSpec((1, 8, LANES), lambda i: (i, 0, 0))],
        out_specs=pl.BlockSpec((8, tm), lambda i: (0, i)),
        out_shape=jax.ShapeDtypeStruct((8, t), jnp.int32),
        compiler_params=_cparams("parallel"),
        name="moe_dest",
    )(route, base)
    return out[0:2].reshape(1, 2 * t)


SC_WINDOW = 128


def _sc_mesh():
    return plsc.VectorSubcoreMesh(core_axis_name="core", subcore_axis_name="subcore")


def _sc_scatter_rows(x, idx, n_out):
    n, d = x.shape
    m = idx.shape[1]
    nblk = n // SC_WINDOW

    @pl.kernel(out_type=jax.ShapeDtypeStruct((n_out, d), x.dtype), mesh=_sc_mesh(), name="moe_sc_scatter")
    def scatter(x_hbm, i_hbm, o_hbm):
        def body(x_vmem, i_vmem):
            pltpu.sync_copy(x_vmem, o_hbm.at[i_vmem.at[0]])

        half = m // SC_WINDOW // 2
        pltpu.emit_pipeline(
            body,
            grid=(2, half),
            in_specs=[pl.BlockSpec((SC_WINDOW, d), lambda c, j: ((c * half + j) % nblk, 0)),
                      pl.BlockSpec((1, SC_WINDOW), lambda c, j: (0, c * half + j))],
            out_specs=[],
            core_axis_name=("core", "subcore"),
            dimension_semantics=(pltpu.PARALLEL, pltpu.PARALLEL),
        )(x_hbm, i_hbm)

    return scatter(x, idx)


def _sc_gather_rows(table, idx):
    d = table.shape[1]
    m = idx.shape[1]

    @pl.kernel(out_type=jax.ShapeDtypeStruct((m, d), table.dtype), mesh=_sc_mesh(), name="moe_sc_gather")
    def gather(t_hbm, i_hbm, o_hbm):
        def body(i_vmem, o_vmem):
            pltpu.sync_copy(t_hbm.at[i_vmem.at[0]], o_vmem)

        half = m // SC_WINDOW // 2
        pltpu.emit_pipeline(
            body,
            grid=(2, half),
            in_specs=[pl.BlockSpec((1, SC_WINDOW), lambda c, j: (0, c * half + j))],
            out_specs=[pl.BlockSpec((SC_WINDOW, d), lambda c, j: (c * half + j, 0))],
            core_axis_name=("core", "subcore"),
            dimension_semantics=(pltpu.PARALLEL, pltpu.PARALLEL),
        )(i_hbm, o_hbm)

    return gather(table, idx)


def _expert_kernel(blk_expert_ref, n_used_ref, xa_ref, xb_ref, wg_ref, wu_ref, wd_ref, ya_ref, yb_ref):
    del blk_expert_ref

    @pl.when(pl.program_id(0) < n_used_ref[0])
    def _():
        xe = _unpack_rows(xa_ref[...], xb_ref[...]).astype(BF16)
        act = _silu(jnp.dot(xe, wg_ref[0], preferred_element_type=F32)) * \
            jnp.dot(xe, wu_ref[0], preferred_element_type=F32)
        y = jnp.dot(act.astype(BF16), wd_ref[0], preferred_element_type=F32)
        ya_ref[...], yb_ref[...] = _pack_rows(y)

    @pl.when(pl.program_id(0) >= n_used_ref[0])
    def _():
        ya_ref[...] = jnp.zeros(ya_ref.shape, ya_ref.dtype)
        yb_ref[...] = jnp.zeros(yb_ref.shape, yb_ref.dtype)


def _experts(blk_expert, n_used, xa, xb, wg, wu, wd):
    n_rows, q = xa.shape
    ff = wg.shape[2]
    blk = MOE_BLOCK
    row = pl.BlockSpec((blk, q), lambda b, be, nu: (b, 0))
    grid_spec = pltpu.PrefetchScalarGridSpec(
        num_scalar_prefetch=2,
        grid=(n_rows // blk,),
        in_specs=[row, row,
                  pl.BlockSpec((1, D_MODEL, ff), lambda b, be, nu: (be[b], 0, 0)),
                  pl.BlockSpec((1, D_MODEL, ff), lambda b, be, nu: (be[b], 0, 0)),
                  pl.BlockSpec((1, ff, D_MODEL), lambda b, be, nu: (be[b], 0, 0))],
        out_specs=[row, row])
    return pl.pallas_call(
        _expert_kernel,
        grid_spec=grid_spec,
        out_shape=[jax.ShapeDtypeStruct((n_rows, q), jnp.uint32)] * 2,
        compiler_params=_cparams("arbitrary"),
        name="moe_experts",
    )(blk_expert, n_used, xa, xb, wg, wu, wd)


def _combine_kernel(x_ref, route_ref, a1_ref, b1_ref, a2_ref, b2_ref, o_ref):
    g1 = route_ref[:, 2:3]
    g2 = route_ref[:, 3:4]
    o_ref[...] = x_ref[...] + g1 * _unpack_rows(a1_ref[...], b1_ref[...]) \
        + g2 * _unpack_rows(a2_ref[...], b2_ref[...])


def _combine(x2d, route, ya, yb):
    t = x2d.shape[0]
    tm = ROW_TILE
    nt = t // tm
    q = ya.shape[1]
    row = lambda w: pl.BlockSpec((tm, w), lambda i: (i, 0))
    first = pl.BlockSpec((tm, q), lambda i: (i, 0))
    second = pl.BlockSpec((tm, q), lambda i: (nt + i, 0))
    return pl.pallas_call(
        _combine_kernel,
        grid=(nt,),
        in_specs=[row(D_MODEL), row(LANES), first, first, second, second],
        out_specs=row(D_MODEL),
        out_shape=jax.ShapeDtypeStruct((t, D_MODEL), F32),
        compiler_params=_cparams("parallel"),
        name="moe_combine",
    )(x2d, route, ya, yb, ya, yb)


def _moe_sorted(x2d, g, wr, br, wg, wu, wd):
    t = x2d.shape[0]
    blk = MOE_BLOCK
    n_blocks = 2 * t // blk + N_EXPERTS
    ha, hb, route, cnt = _route(x2d, g, wr, br)

    cnt = cnt[:, 0, :N_EXPERTS].astype(jnp.int32)
    before_chunk = jnp.cumsum(cnt, axis=0) - cnt
    seg_blocks = (jnp.sum(cnt, axis=0) + blk - 1) // blk
    seg_end_blk = jnp.cumsum(seg_blocks)
    seg_start = (seg_end_blk - seg_blocks) * blk
    base = jnp.pad((seg_start[None, :] + before_chunk).astype(F32), ((0, 0), (0, LANES - N_EXPERTS)))
    idx = _destinations(route, jnp.broadcast_to(base[:, None, :], (base.shape[0], 8, LANES)))
    past_end = jnp.arange(n_blocks, dtype=jnp.int32)[:, None] >= seg_end_blk[None, :]
    blk_expert = jnp.minimum(jnp.sum(past_end, axis=1), N_EXPERTS - 1).astype(jnp.int32)
    n_used = seg_end_blk[-1:].astype(jnp.int32)

    xa = _sc_scatter_rows(ha, idx, n_blocks * blk)
    xb = _sc_scatter_rows(hb, idx, n_blocks * blk)
    ya, yb = _experts(blk_expert, n_used, xa, xb, wg, wu, wd)
    return _combine(x2d, route, _sc_gather_rows(ya, idx), _sc_gather_rows(yb, idx))


def _pad_cols(w, width):
    return jnp.pad(w, ((0, 0), (0, width - w.shape[1])))


def _layer_params(l, mix_norm, w_in, b_forget, a_q_norm, a_k_norm, b_v_norm, b_spatial_w, b_spatial_b,
                  c_q_lat_norm, c_w_uq, c_kv_lat_norm, c_w_ukv, c_q_nope_norm, c_q_rope_norm,
                  c_k_nope_norm, c_k_rope_norm, out_norm_a, out_norm_b, out_norm_c, w_out,
                  xattn_norm, mem_norm, w_mem_q, w_mem_kv, m_q_norm, m_k_norm, w_mem_out):
    p = {}
    o = np.cumsum((0, A_W, A_W, A_W, A_HEADS, B_W, B_W, C_Q_RANK, C_KV_RANK, C_ROPE_DIM))
    w = w_in[l]
    seg = lambda n: w[:, o[n]:o[n + 1]]
    fa = seg(3)
    misc = jnp.zeros((D_MODEL, LANES), F32)
    misc = misc.at[:, ROPE_LANE:ROPE_LANE + C_ROPE_DIM].set(seg(8))
    fb = jnp.zeros((1, LANES), F32)
    for hd in range(A_HEADS):
        ln = FORGET_LANE + 8 * (hd // 2) + hd % 2
        misc = misc.at[:, ln].set(fa[:, hd])
        fb = fb.at[0, ln].set(b_forget[l, hd])
    p["w_in"] = jnp.concatenate([seg(0), seg(1), seg(2), seg(4), seg(5), seg(6), seg(7), misc], axis=1).astype(BF16)
    p["fb"] = fb
    p["mix_g"] = mix_norm[l][None]
    p["aq"] = jnp.tile(a_q_norm[l], A_HEADS)[None] * (A_HEAD_DIM ** -0.5 * LOG2E)
    p["ak"] = jnp.tile(a_k_norm[l], A_HEADS)[None]
    p["bvg"] = b_v_norm[l][None]
    pos = np.arange(B_WINDOW)
    mask = (pos[None, :] // CHUNK) <= (pos[:, None] // CHUNK)
    p["ws"] = jnp.where(mask[None], b_spatial_w[l], 0.0).reshape(B_GROUPS * B_WINDOW, B_WINDOW).astype(BF16)
    p["bs"] = jnp.repeat(b_spatial_b[l].T, B_GROUP_DIM, axis=1)
    p["onb"] = out_norm_b[l][None]
    p["cqg"] = c_q_lat_norm[l][None]
    p["ckvg"] = c_kv_lat_norm[l][None]
    gidx = np.arange(A_W) // A_HEAD_DIM
    p["gm"] = jnp.asarray((gidx[:, None] == gidx[None, :]).astype(np.float32) / A_HEAD_DIM).astype(BF16)
    p["tri"] = jnp.asarray(np.tril(np.ones((ROW_TILE, ROW_TILE), np.float32))).astype(BF16)

    qd = C_NOPE_DIM + C_ROPE_DIM
    half = C_ROPE_DIM // 2
    wq = c_w_uq[l]
    wq_partner = jnp.concatenate([jnp.zeros_like(wq[:, :, :C_NOPE_DIM]), wq[:, :, C_NOPE_DIM + half:],
                                  wq[:, :, C_NOPE_DIM:C_NOPE_DIM + half]], axis=-1)
    p["wuq"] = jnp.pad(jnp.stack([wq, wq_partner]), ((0, 0), (0, 0), (0, 0), (0, HEAD_SLAB - qd))
                       ).reshape(2, C_Q_RANK, -1).astype(BF16)
    wukv = c_w_ukv[l]
    p["wuk"] = jnp.pad(wukv[:, :, :C_NOPE_DIM], ((0, 0), (0, 0), (0, HEAD_SLAB - C_NOPE_DIM))
                       ).reshape(C_KV_RANK, -1).astype(BF16)
    p["wuv"] = wukv[:, :, C_NOPE_DIM:].reshape(C_KV_RANK, C_W).astype(BF16)
    gq = jnp.concatenate([c_q_nope_norm[l], c_q_rope_norm[l]])
    gq_partner = jnp.concatenate([jnp.zeros_like(c_q_nope_norm[l]), c_q_rope_norm[l][half:],
                                  c_q_rope_norm[l][:half]])
    p["gq"] = _pad_cols(jnp.stack([gq, gq_partner]) * (qd ** -0.5 * LOG2E), LANES)
    p["gkn"] = _pad_cols(c_k_nope_norm[l][None], LANES)
    seg = np.zeros((LANES, LANES), np.float32)
    seg[:C_NOPE_DIM, :C_NOPE_DIM] = 1.0 / C_NOPE_DIM
    seg[ROPE_LANE:ROPE_LANE + C_ROPE_DIM, ROPE_LANE:ROPE_LANE + C_ROPE_DIM] = 1.0 / C_ROPE_DIM
    p["seg"] = jnp.asarray(seg).astype(BF16)
    p["gkr"] = jnp.zeros((1, LANES), F32).at[0, ROPE_LANE:ROPE_LANE + C_ROPE_DIM].set(c_k_rope_norm[l])

    p["ona"] = out_norm_a[l][None]
    p["onc"] = out_norm_c[l][None]
    p["w_out"] = w_out[l].astype(BF16)
    p["xg"] = xattn_norm[l][None]
    p["w_mem_q"] = w_mem_q[l].astype(BF16)
    p["mqg"] = m_q_norm[l][None] * (M_HEAD_DIM ** -0.5)
    p["mem_g"] = mem_norm[l][None]
    p["w_mem_kv"] = w_mem_kv[l].astype(BF16)
    p["mkg"] = m_k_norm[l][None]
    p["w_mem_out"] = w_mem_out[l].astype(BF16)
    return p


def kernel(x, mem, positions, mix_norm, w_in, b_forget, a_q_norm, a_k_norm, b_v_norm, b_spatial_w, b_spatial_b, c_q_lat_norm, c_w_uq, c_kv_lat_norm, c_w_ukv, c_q_nope_norm, c_q_rope_norm, c_k_nope_norm, c_k_rope_norm, out_norm_a, out_norm_b, out_norm_c, w_out, xattn_norm, mem_norm, w_mem_q, w_mem_kv, m_q_norm, m_k_norm, w_mem_out, ffn_norm, ffn_w_gate, ffn_w_up, ffn_w_down, w_router, b_router, moe_w_gate, moe_w_up, moe_w_down):
    nb, seq, d = x.shape
    assert d == D_MODEL and seq % (2 * ROW_TILE) == 0 and seq % ATTN_TILE == 0
    assert (nb * seq) % MOE_CHUNK == 0 and (nb * seq) % SC_WINDOW == 0 and (2 * nb * seq) % MOE_BLOCK == 0
    depth = w_in.shape[0]
    t = nb * seq
    x2d = x.reshape(t, d)
    cos, sin = _rope_tables(positions.reshape(t, 1).astype(F32))

    for l in range(depth):
        p = _layer_params(l, mix_norm, w_in, b_forget, a_q_norm, a_k_norm, b_v_norm, b_spatial_w,
                          b_spatial_b, c_q_lat_norm, c_w_uq, c_kv_lat_norm, c_w_ukv, c_q_nope_norm,
                          c_q_rope_norm, c_k_nope_norm, c_k_rope_norm, out_norm_a, out_norm_b,
                          out_norm_c, w_out, xattn_norm, mem_norm, w_mem_q, w_mem_kv, m_q_norm,
                          m_k_norm, w_mem_out)
        qa, ka, vta, bn, cq, ckv, misc = _inproj(x2d, seq, p)
        qc, kc, vtc = _mla_prep(cq, ckv, misc, cos, sin, seq, p)
        a = _attention(qa, ka, vta, unit=1, name="attn_fox")
        c = _attention(qc, kc, vtc, unit=CHUNK, name="attn_mla")
        km, vm = _mem_kv(mem, p)
        x2d = _outproj(x2d, a, bn, c, km, vm, seq, p)
        g = ffn_norm[l][None]
        if l % 2 == 0:
            m = l // 2
            ff = ffn_w_gate.shape[2]
            ff_pad = -(-ff // (2 * LANES)) * (2 * LANES)
            wg = _pad_cols(ffn_w_gate[m], ff_pad).astype(BF16)
            wu = _pad_cols(ffn_w_up[m], ff_pad).astype(BF16)
            wd = jnp.pad(ffn_w_down[m], ((0, ff_pad - ff), (0, 0))).astype(BF16)
            x2d = _ffn(x2d, g, wg, wu, wd)
        else:
            m = l // 2
            wr = _pad_cols(w_router[m], LANES)
            wr_hi = wr.astype(BF16)
            wr = jnp.stack([wr_hi, (wr - wr_hi.astype(F32)).astype(BF16)])
            br = _pad_cols(b_router[m][None], LANES)
            x2d = _moe_sorted(x2d, g, wr, br, moe_w_gate[m].astype(BF16), moe_w_up[m].astype(BF16),
                              moe_w_down[m].astype(BF16))
    return x2d.reshape(nb, seq, d)
```

```python
import functools

import numpy as np
import jax
import jax.numpy as jnp
from jax import lax
from jax.experimental import pallas as pl
from jax.experimental.pallas import tpu as pltpu
from jax.experimental.pallas import tpu_sc as plsc

F32 = jnp.float32
BF16 = jnp.bfloat16

D_MODEL = 1024
CHUNK = 64
EPS = 1e-6
NEG_INF = -1e30
A_HEADS, A_HEAD_DIM = 4, 64
B_GROUPS, B_GROUP_DIM, B_WINDOW = 4, 64, 128
C_HEADS, C_NOPE_DIM, C_ROPE_DIM, C_V_DIM = 8, 64, 32, 64
C_Q_RANK, C_KV_RANK = 256, 128
ROPE_THETA = 10000.0
M_HEADS, M_HEAD_DIM = 4, 128
N_EXPERTS = 8
A_W = A_HEADS * A_HEAD_DIM
B_W = B_GROUPS * B_GROUP_DIM
C_W = C_HEADS * C_V_DIM
M_W = M_HEADS * M_HEAD_DIM

LANES = 128

SEG_Q, SEG_K, SEG_V, SEG_U, SEG_VB, SEG_CQ, SEG_CKV, SEG_MISC = 0, 256, 512, 768, 1024, 1280, 1536, 1664
IN_PAD_W = SEG_MISC + LANES
ROPE_LANE = C_NOPE_DIM
FORGET_LANE = 96
HEAD_SLAB = LANES
VT_ROWS = 80
LOG2E = float(np.log2(np.e))

ROW_TILE = 512
ATTN_TILE = 512
ATTN_HEADS = 2
MOE_CHUNK = 1024
MOE_BLOCK = 512
VMEM_LIMIT = 56 * 1024 * 1024


def _cparams(*sem):
    return pltpu.CompilerParams(dimension_semantics=sem, vmem_limit_bytes=VMEM_LIMIT)


def _full(shape):
    n = len(shape)
    return pl.BlockSpec(shape, lambda *_: (0,) * n)


def _rms(x):
    return x * lax.rsqrt(jnp.mean(x * x, axis=-1, keepdims=True) + EPS)


def _dot_split(v, exact, pieces, lhs_is_exact=False):
    total = None
    rem = v
    for n in range(pieces):
        part = rem.astype(BF16)
        if n + 1 < pieces:
            rem = rem - part.astype(F32)
        term = (jnp.dot(exact, part, preferred_element_type=F32) if lhs_is_exact
                else jnp.dot(part, exact, preferred_element_type=F32))
        total = term if total is None else total + term
    return total


def _lane_iota(n=LANES):
    return lax.broadcasted_iota(jnp.int32, (1, n), 1)


def _rope_table_kernel(pos_ref, inv_ref, sgn_ref, cos_ref, sin_ref):
    ang = pos_ref[...] * inv_ref[...]
    cos_ref[...] = jnp.cos(ang)
    sin_ref[...] = jnp.sin(ang) * sgn_ref[...]


def _rope_tables(pos_col):
    t = pos_col.shape[0]
    half = C_ROPE_DIM // 2
    inv = ROPE_THETA ** (-jnp.arange(half, dtype=F32) / half)
    inv_l = jnp.zeros((1, LANES), F32).at[0, ROPE_LANE:ROPE_LANE + C_ROPE_DIM].set(jnp.tile(inv, 2))
    sgn = np.zeros((1, LANES), np.float32)
    sgn[0, ROPE_LANE:ROPE_LANE + half] = -1.0
    sgn[0, ROPE_LANE + half:ROPE_LANE + C_ROPE_DIM] = 1.0
    tm = ROW_TILE
    return pl.pallas_call(
        _rope_table_kernel,
        grid=(t // tm,),
        in_specs=[pl.BlockSpec((tm, 1), lambda i: (i, 0)), _full((1, LANES)), _full((1, LANES))],
        out_specs=[pl.BlockSpec((tm, LANES), lambda i: (i, 0))] * 2,
        out_shape=[jax.ShapeDtypeStruct((t, LANES), F32)] * 2,
        compiler_params=_cparams("parallel"),
        name="rope_tables",
    )(pos_col, inv_l, jnp.asarray(sgn))


def _rotate(x, cos, sin_signed, lane):
    half = C_ROPE_DIM // 2
    partner = jnp.where(lane < ROPE_LANE + half,
                        pltpu.roll(x, LANES - half, 1), pltpu.roll(x, half, 1))
    return x * cos + partner * sin_signed


def _store_v_transposed(vt_ref, v, n_heads, lanes=slice(None)):
    tm = v.shape[0]
    v_t = v.T
    tail = jnp.where(lax.broadcasted_iota(jnp.int32, (VT_ROWS - C_V_DIM, tm), 0) == 0, 1.0, 0.0).astype(BF16)
    for hd in range(n_heads):
        vt_ref[0, hd, 0:C_V_DIM, lanes] = v_t[hd * C_V_DIM:(hd + 1) * C_V_DIM, :].astype(BF16)
        vt_ref[0, hd, C_V_DIM:VT_ROWS, lanes] = tail


def _gelu(x):
    return 0.5 * x * (1.0 + lax.erf(x * np.float32(1.0 / np.sqrt(2.0))))


def _inproj_kernel(x0_ref, xa_ref, xb_ref, g_ref, w_ref, aq_ref, ak_ref, fb_ref, bvg_ref, ws_ref, bs_ref,
                   onb_ref, cqg_ref, ckvg_ref, gm_ref, tri_ref,
                   qa_ref, ka_ref, vt_ref, bn_ref, cq_ref, ckv_ref, misc_ref,
                   carry_ref, buf0_ref, buf1_ref, *, tiles_per_seq):
    i = pl.program_id(0)
    tm = xa_ref.shape[0]

    def projection_parts(x_ref, buf_ref):
        h = (_rms(x_ref[...]) * g_ref[...]).astype(BF16)

        def part(lo, hi):
            def run():
                buf_ref[:, lo:hi] = jnp.dot(h, w_ref[:, lo:hi], preferred_element_type=F32)
            return run
        return [part(SEG_Q, SEG_U), part(SEG_U, SEG_CQ), part(SEG_CQ, IN_PAD_W)]

    @pl.when(i == 0)
    def _():
        for run in projection_parts(x0_ref, buf0_ref):
            run()

    refs = (aq_ref, ak_ref, fb_ref, bvg_ref, ws_ref, bs_ref, onb_ref, cqg_ref, ckvg_ref, gm_ref, tri_ref,
            qa_ref, ka_ref, vt_ref, bn_ref, cq_ref, ckv_ref, misc_ref, carry_ref)
    _mixer_prologues(buf0_ref, 0, 2 * i, tm, tiles_per_seq, projection_parts(xa_ref, buf1_ref), *refs)
    _mixer_prologues(buf1_ref, 1, 2 * i + 1, tm, tiles_per_seq, projection_parts(xb_ref, buf0_ref), *refs)


def _mixer_prologues(proj, half, tile, tm, tiles_per_seq, between, aq_ref, ak_ref, fb_ref, bvg_ref, ws_ref,
                     bs_ref, onb_ref, cqg_ref, ckvg_ref, gm_ref, tri_ref, qa_ref, ka_ref, vt_ref, bn_ref,
                     cq_ref, ckv_ref, misc_ref, carry_ref):
    out_rows = slice(half * tm, (half + 1) * tm)
    gm = gm_ref[...]

    def group_mean(v):
        return _dot_split(v, gm, 2)

    misc = proj[:, SEG_MISC:SEG_MISC + LANES]
    misc_ref[out_rows, :] = misc
    z = misc + fb_ref[...]
    log_f = jnp.minimum(z, 0.0) - jnp.log1p(jnp.exp(-jnp.abs(z)))
    carry = jnp.where(tile % tiles_per_seq == 0, 0.0, carry_ref[...])
    cum = _dot_split(log_f, tri_ref[...], 3, lhs_is_exact=True) + carry
    carry_ref[...] = cum[tm - 1:tm, :]
    f_hi = (cum * LOG2E).astype(BF16).astype(F32)
    f_rem = cum * LOG2E - f_hi
    f_mid = f_rem.astype(BF16).astype(F32)
    f_lo = f_rem - f_mid

    between[0]()
    q = proj[:, SEG_Q:SEG_Q + A_W]
    qn = q * lax.rsqrt(group_mean(q * q) + EPS) * aq_ref[...]
    k = proj[:, SEG_K:SEG_K + A_W]
    kn = k * lax.rsqrt(group_mean(k * k) + EPS) * ak_ref[...]
    lane = _lane_iota()
    for hd in range(A_HEADS):
        pair = slice((hd // 2) * LANES, (hd // 2 + 1) * LANES)
        slab = slice(hd * HEAD_SLAB, (hd + 1) * HEAD_SLAB)
        data = (lane < A_HEAD_DIM) if hd % 2 == 0 else (lane >= A_HEAD_DIM)
        e0 = A_HEAD_DIM if hd % 2 == 0 else 0
        fl = FORGET_LANE + 8 * (hd // 2) + hd % 2
        ones = jnp.where((lane >= e0) & (lane < e0 + 3), 1.0, 0.0)
        qa_ref[out_rows, slab] = jnp.where(data, qn[:, pair], ones).astype(BF16)
        bias = jnp.where(lane == e0, -f_hi[:, fl:fl + 1],
                         jnp.where(lane == e0 + 1, -f_mid[:, fl:fl + 1],
                                   jnp.where(lane == e0 + 2, -f_lo[:, fl:fl + 1], 0.0)))
        ka_ref[out_rows, slab] = jnp.where(data, kn[:, pair], bias).astype(BF16)
    _store_v_transposed(vt_ref, proj[:, SEG_V:SEG_V + A_W], A_HEADS, out_rows)

    between[1]()
    u = _gelu(proj[:, SEG_U:SEG_U + B_W])
    v = _gelu(proj[:, SEG_VB:SEG_VB + B_W])
    dv = v - group_mean(v)
    vn = dv * lax.rsqrt(group_mean(dv * dv) + EPS) * bvg_ref[...]
    group = lax.broadcasted_iota(jnp.int32, (1, B_W), 1) // B_GROUP_DIM
    for w in range(tm // B_WINDOW):
        rows = slice(w * B_WINDOW, (w + 1) * B_WINDOW)
        y_all = jnp.dot(ws_ref[...], vn[rows].astype(BF16), preferred_element_type=F32)
        y = bs_ref[...]
        for g in range(B_GROUPS):
            y = y + jnp.where(group == g, y_all[g * B_WINDOW:(g + 1) * B_WINDOW], 0.0)
        b = u[rows] * y
        bn_ref[half * tm + w * B_WINDOW:half * tm + (w + 1) * B_WINDOW, :] = (_rms(b) * onb_ref[...]).astype(BF16)

    between[2]()
    cq_ref[out_rows, :] = (_rms(proj[:, SEG_CQ:SEG_CQ + C_Q_RANK]) * cqg_ref[...]).astype(BF16)
    ckv_ref[out_rows, :] = (_rms(proj[:, SEG_CKV:SEG_CKV + C_KV_RANK]) * ckvg_ref[...]).astype(BF16)


def _inproj(x2d, seq, p):
    t = x2d.shape[0]
    tm = ROW_TILE
    tps = seq // tm
    nb = t // seq
    n_tiles = t // tm
    steps_per_seq = tps // 2
    row = lambda w: pl.BlockSpec((2 * tm, w), lambda i: (i, 0))
    x_tile = lambda index: pl.BlockSpec((tm, D_MODEL), lambda i: (index(i), 0))
    qk_w = A_HEADS * HEAD_SLAB
    out_shape = [
        jax.ShapeDtypeStruct((t, qk_w), BF16), jax.ShapeDtypeStruct((t, qk_w), BF16),
        jax.ShapeDtypeStruct((nb, A_HEADS, VT_ROWS, seq), BF16),
        jax.ShapeDtypeStruct((t, B_W), BF16),
        jax.ShapeDtypeStruct((t, C_Q_RANK), BF16), jax.ShapeDtypeStruct((t, C_KV_RANK), BF16),
        jax.ShapeDtypeStruct((t, LANES), F32),
    ]
    out_specs = [row(qk_w), row(qk_w),
                 pl.BlockSpec((1, A_HEADS, VT_ROWS, 2 * tm),
                              lambda i: (i // steps_per_seq, 0, 0, i % steps_per_seq)),
                 row(B_W), row(C_Q_RANK), row(C_KV_RANK), row(LANES)]
    consts = [p["mix_g"], p["w_in"], p["aq"], p["ak"], p["fb"], p["bvg"], p["ws"], p["bs"], p["onb"],
              p["cqg"], p["ckvg"], p["gm"], p["tri"]]
    return pl.pallas_call(
        functools.partial(_inproj_kernel, tiles_per_seq=tps),
        grid=(n_tiles // 2,),
        in_specs=[x_tile(lambda i: 0), x_tile(lambda i: 2 * i + 1),
                  x_tile(lambda i: jnp.minimum(2 * i + 2, n_tiles - 1))] + [_full(c.shape) for c in consts],
        out_specs=out_specs,
        out_shape=out_shape,
        scratch_shapes=[pltpu.VMEM((1, LANES), F32), pltpu.VMEM((tm, IN_PAD_W), F32),
                        pltpu.VMEM((tm, IN_PAD_W), F32)],
        compiler_params=_cparams("arbitrary"),
        name="in_proj",
    )(x2d, x2d, x2d, *consts)


def _mla_prep_kernel(cq_ref, ckv_ref, misc_ref, cos_ref, sin_ref, wuq_ref, wuk_ref, wuv_ref,
                     gq_ref, gkn_ref, gkr_ref, seg_ref, qc_ref, kc_ref, vt_ref):
    tm = cq_ref.shape[0] // 2
    tiles = [slice(0, tm), slice(tm, 2 * tm)]
    lane = _lane_iota()
    rope = (lane >= ROPE_LANE) & (lane < ROPE_LANE + C_ROPE_DIM)
    seg = seg_ref[...]

    def inv_rms(v):
        return lax.rsqrt(jnp.dot((v * v).astype(BF16), seg, preferred_element_type=F32) + EPS)

    q, q_partner, kn, q_cos, q_sin, kr = [], [], [], [], [], []
    for rows in tiles:
        cos, sin = cos_ref[rows, :], sin_ref[rows, :]
        q.append(jnp.dot(cq_ref[rows, :], wuq_ref[0], preferred_element_type=F32))
        q_partner.append(jnp.dot(cq_ref[rows, :], wuq_ref[1], preferred_element_type=F32))
        q_cos.append(gq_ref[0:1, :] * cos)
        q_sin.append(gq_ref[1:2, :] * sin)
        kn.append(jnp.dot(ckv_ref[rows, :], wuk_ref[...], preferred_element_type=F32))
        _store_v_transposed(vt_ref, jnp.dot(ckv_ref[rows, :], wuv_ref[...], preferred_element_type=F32),
                            C_HEADS, rows)
        k_rot = jnp.where(rope, misc_ref[rows, :], 0.0)
        kr.append(_rotate(k_rot * inv_rms(k_rot) * gkr_ref[...], cos, sin, lane))

    for hd in range(C_HEADS):
        cols = slice(hd * HEAD_SLAB, (hd + 1) * HEAD_SLAB)
        for t, rows in enumerate(tiles):
            qh = q[t][:, cols]
            qc_ref[rows, cols] = (inv_rms(qh) * (qh * q_cos[t] + q_partner[t][:, cols] * q_sin[t])).astype(BF16)
            kh = kn[t][:, cols]
            kc_ref[rows, cols] = (kh * inv_rms(kh) * gkn_ref[...] + kr[t]).astype(BF16)


def _mla_prep(cq, ckv, misc, cos, sin, seq, p):
    t = cq.shape[0]
    tm = 2 * ROW_TILE
    tps = seq // tm
    row = lambda w: pl.BlockSpec((tm, w), lambda i: (i, 0))
    consts = [p["wuq"], p["wuk"], p["wuv"], p["gq"], p["gkn"], p["gkr"], p["seg"]]
    qk_w = C_HEADS * HEAD_SLAB
    return pl.pallas_call(
        _mla_prep_kernel,
        grid=(t // tm,),
        in_specs=[row(C_Q_RANK), row(C_KV_RANK), row(LANES), row(LANES), row(LANES)]
                 + [_full(c.shape) for c in consts],
        out_specs=[row(qk_w), row(qk_w),
                   pl.BlockSpec((1, C_HEADS, VT_ROWS, tm), lambda i: (i // tps, 0, 0, i % tps))],
        out_shape=[jax.ShapeDtypeStruct((t, qk_w), BF16), jax.ShapeDtypeStruct((t, qk_w), BF16),
                   jax.ShapeDtypeStruct((t // seq, C_HEADS, VT_ROWS, seq), BF16)],
        compiler_params=_cparams("parallel"),
        name="mla_prep",
    )(cq, ckv, misc, cos, sin, *consts)


def _attn_items(nq):
    return [(i, j) for i in range(nq) for j in range(i + 1)]


def _attn_kernel(q_ref, k_ref, vt_ref, mask_ref, o_ref, s0_ref, s1_ref, p0_ref, p1_ref,
                 mp0_ref, mp1_ref, mrun_ref, macc_ref, acc_ref):
    tk, tq = mask_ref.shape
    nq = q_ref.shape[0] // tq
    n_h = q_ref.shape[1] // HEAD_SLAB
    items = _attn_items(nq)
    n_items = len(items)
    s_bufs, p_bufs, mp_bufs = (s0_ref, s1_ref), (p0_ref, p1_ref), (mp0_ref, mp1_ref)
    mrun_ref[...] = jnp.full(mrun_ref.shape, NEG_INF, F32)
    macc_ref[...] = jnp.full(macc_ref.shape, NEG_INF, F32)
    acc_ref[...] = jnp.zeros(acc_ref.shape, F32)

    half = tk // 2
    nt_dims = (((1,), (1,)), ((), ()))
    top, bot, left, right = slice(0, half), slice(half, tk), slice(0, half), slice(half, tq)

    def scores(it, buf):
        qi, kj = items[it]
        k_rows, q_rows = kj * tk, qi * tq
        for hh in range(n_h):
            cols = slice(hh * HEAD_SLAB, (hh + 1) * HEAD_SLAB)
            if qi != kj:
                s_t = lax.dot_general(k_ref[k_rows:k_rows + tk, cols], q_ref[q_rows:q_rows + tq, cols],
                                      nt_dims, preferred_element_type=F32)
                s_bufs[buf][hh] = s_t
                col_max = jnp.max(s_t, axis=0, keepdims=True)
            else:
                s_top = lax.dot_general(k_ref[k_rows:k_rows + half, cols], q_ref[q_rows:q_rows + tq, cols],
                                        nt_dims, preferred_element_type=F32)
                s_top = jnp.concatenate([s_top[:, left] + mask_ref[top, left], s_top[:, right]], axis=1)
                s_br = lax.dot_general(k_ref[k_rows + half:k_rows + tk, cols],
                                       q_ref[q_rows + half:q_rows + tq, cols],
                                       nt_dims, preferred_element_type=F32) + mask_ref[bot, right]
                s_bufs[buf][hh, top, :] = s_top
                s_bufs[buf][hh, bot, right] = s_br
                max_top = jnp.max(s_top, axis=0, keepdims=True)
                col_max = jnp.concatenate(
                    [max_top[:, left], jnp.maximum(max_top[:, right], jnp.max(s_br, axis=0, keepdims=True))],
                    axis=1)
            mrun_ref[qi, hh] = jnp.maximum(mrun_ref[qi, hh], col_max)

    def exponentiate(it, buf):
        qi, kj = items[it]
        for hh in range(n_h):
            m = mrun_ref[qi, hh]
            if qi != kj:
                p_bufs[buf][hh] = jnp.exp2(s_bufs[buf][hh] - m).astype(BF16)
            else:
                p_bufs[buf][hh, top, :] = jnp.exp2(s_bufs[buf][hh, top, :] - m).astype(BF16)
                p_bufs[buf][hh, bot, right] = jnp.exp2(s_bufs[buf][hh, bot, right] - m[:, right]).astype(BF16)
            mp_bufs[buf][hh] = m

    def accumulate(it, buf):
        qi, kj = items[it]
        k_rows = kj * tk
        for hh in range(n_h):
            m = mp_bufs[buf][hh]
            if qi != kj:
                pv = jnp.dot(vt_ref[0, hh, :, k_rows:k_rows + tk], p_bufs[buf][hh],
                             preferred_element_type=F32)
            else:
                pv = jnp.concatenate(
                    [jnp.dot(vt_ref[0, hh, :, k_rows:k_rows + half], p_bufs[buf][hh, top, left],
                             preferred_element_type=F32),
                     jnp.dot(vt_ref[0, hh, :, k_rows:k_rows + tk], p_bufs[buf][hh, :, right],
                             preferred_element_type=F32)], axis=1)
            acc_ref[qi, hh] = jnp.exp2(macc_ref[qi, hh] - m) * acc_ref[qi, hh] + pv
            macc_ref[qi, hh] = m

    for it in range(n_items + 2):
        par = it % 2
        if 2 <= it:
            accumulate(it - 2, par)
            qi, kj = items[it - 2]
            if qi == kj:
                halves = [acc_ref[qi, hh, 0:C_V_DIM, :] / acc_ref[qi, hh, C_V_DIM:C_V_DIM + 1, :]
                          for hh in range(n_h)]
                o_ref[qi * tq:(qi + 1) * tq, :] = jnp.concatenate(halves, axis=0).T.astype(o_ref.dtype)
        if 1 <= it <= n_items:
            exponentiate(it - 1, 1 - par)
        if it < n_items:
            scores(it, par)


def _attention(q, k, vt, *, unit, name):
    t = q.shape[0]
    nb, n_heads, _, seq = vt.shape
    tq = ATTN_TILE
    nq = seq // tq
    assert (tq // 2) % unit == 0
    pos = np.arange(tq)
    diag_mask = np.where((pos[:, None] // unit) <= (pos[None, :] // unit), 0.0, NEG_INF)
    mask = jnp.asarray(diag_mask.astype(np.float32))
    n_h = ATTN_HEADS
    seq_blk = lambda w: pl.BlockSpec((seq, w), lambda b, p: (b, p))
    return pl.pallas_call(
        _attn_kernel,
        grid=(nb, n_heads // n_h),
        in_specs=[seq_blk(n_h * HEAD_SLAB), seq_blk(n_h * HEAD_SLAB),
                  pl.BlockSpec((1, n_h, VT_ROWS, seq), lambda b, p: (b, p, 0, 0)),
                  pl.BlockSpec((tq, tq), lambda b, p: (0, 0), pipeline_mode=pl.Buffered(1))],
        out_specs=seq_blk(n_h * C_V_DIM),
        out_shape=jax.ShapeDtypeStruct((t, n_heads * C_V_DIM), BF16),
        scratch_shapes=[pltpu.VMEM((n_h, tq, tq), F32), pltpu.VMEM((n_h, tq, tq), F32),
                        pltpu.VMEM((n_h, tq, tq), BF16), pltpu.VMEM((n_h, tq, tq), BF16),
                        pltpu.VMEM((n_h, 1, tq), F32), pltpu.VMEM((n_h, 1, tq), F32),
                        pltpu.VMEM((nq, n_h, 1, tq), F32), pltpu.VMEM((nq, n_h, 1, tq), F32),
                        pltpu.VMEM((nq, n_h, VT_ROWS, tq), F32)],
        compiler_params=_cparams("parallel", "parallel"),
        name=name,
    )(q, k, vt, mask)


def _mem_kv_kernel(mem_ref, g_ref, w_ref, kg_ref, k_ref, v_ref):
    mn = (_rms(mem_ref[0]) * g_ref[...]).astype(BF16)
    kv = jnp.dot(mn, w_ref[...], preferred_element_type=F32)
    for hd in range(M_HEADS):
        cols = slice(hd * M_HEAD_DIM, (hd + 1) * M_HEAD_DIM)
        k_ref[0, :, cols] = (_rms(kv[:, cols]) * kg_ref[...]).astype(BF16)
    v_ref[0] = kv[:, M_W:].astype(BF16)


def _mem_kv(mem, p):
    nb, ml, _ = mem.shape
    consts = [p["mem_g"], p["w_mem_kv"], p["mkg"]]
    blk = pl.BlockSpec((1, ml, M_W), lambda b: (b, 0, 0))
    return pl.pallas_call(
        _mem_kv_kernel,
        grid=(nb,),
        in_specs=[pl.BlockSpec((1, ml, D_MODEL), lambda b: (b, 0, 0))] + [_full(c.shape) for c in consts],
        out_specs=[blk, blk],
        out_shape=[jax.ShapeDtypeStruct((nb, ml, M_W), BF16)] * 2,
        compiler_params=_cparams("parallel"),
        name="mem_kv",
    )(mem, *consts)


def _outproj_kernel(x_ref, a_ref, bn_ref, c_ref, ona_ref, onc_ref, wo_ref, xg_ref, wq_ref, mqg_ref,
                    km_ref, vm_ref, wmo_ref, o_ref):
    tm = x_ref.shape[0] // 2
    tiles = [slice(0, tm), slice(tm, 2 * tm)]
    x1, q, outs = [], [], [[], []]
    for rows in tiles:
        a_n = (_rms(a_ref[rows, :].astype(F32)) * ona_ref[...]).astype(BF16)
        c_n = (_rms(c_ref[rows, :].astype(F32)) * onc_ref[...]).astype(BF16)
        mix = jnp.concatenate([a_n, bn_ref[rows, :], c_n], axis=-1)
        x1.append(x_ref[rows, :] + jnp.dot(mix, wo_ref[...], preferred_element_type=F32))
    for t in range(2):
        h = (_rms(x1[t]) * xg_ref[...]).astype(BF16)
        q.append(jnp.dot(h, wq_ref[...], preferred_element_type=F32))
    for hd in range(M_HEADS):
        cols = slice(hd * M_HEAD_DIM, (hd + 1) * M_HEAD_DIM)
        for t in range(2):
            qh = (_rms(q[t][:, cols]) * mqg_ref[...]).astype(BF16)
            s = lax.dot_general(qh, km_ref[0, :, cols], (((1,), (1,)), ((), ())), preferred_element_type=F32)
            e = jnp.exp(s - jnp.max(s, axis=-1, keepdims=True))
            pr = e / jnp.sum(e, axis=-1, keepdims=True)
            outs[t].append(jnp.dot(pr.astype(BF16), vm_ref[0, :, cols], preferred_element_type=F32).astype(BF16))
    for t, rows in enumerate(tiles):
        o_ref[rows, :] = x1[t] + jnp.dot(jnp.concatenate(outs[t], axis=-1), wmo_ref[...],
                                         preferred_element_type=F32)


def _outproj(x2d, a, bn, c, km, vm, seq, p):
    t = x2d.shape[0]
    tm = 2 * ROW_TILE
    tps = seq // tm
    ml = km.shape[1]
    row = lambda w: pl.BlockSpec((tm, w), lambda i: (i, 0))
    memblk = pl.BlockSpec((1, ml, M_W), lambda i: (i // tps, 0, 0))
    c1 = [p["ona"], p["onc"], p["w_out"], p["xg"], p["w_mem_q"], p["mqg"]]
    return pl.pallas_call(
        _outproj_kernel,
        grid=(t // tm,),
        in_specs=[row(D_MODEL), row(A_W), row(B_W), row(C_W)] + [_full(c_.shape) for c_ in c1]
                 + [memblk, memblk, _full(p["w_mem_out"].shape)],
        out_specs=row(D_MODEL),
        out_shape=jax.ShapeDtypeStruct((t, D_MODEL), F32),
        compiler_params=_cparams("parallel"),
        name="out_proj_mem_attn",
    )(x2d, a, bn, c, *c1, km, vm, p["w_mem_out"])


def _silu(x):
    return x * jax.nn.sigmoid(x)


def _ffn_kernel(x_ref, g_ref, wg_ref, wu_ref, wd_ref, o_ref, *, n_chunks):
    tm = x_ref.shape[0] // 2
    tiles = [slice(0, tm), slice(tm, 2 * tm)]
    acc = [x_ref[rows, :] for rows in tiles]
    h = [(_rms(a) * g_ref[...]).astype(BF16) for a in acc]
    fc = wg_ref.shape[1] // n_chunks
    for c in range(n_chunks):
        cols = slice(c * fc, (c + 1) * fc)
        for t in range(2):
            act = _silu(jnp.dot(h[t], wg_ref[:, cols], preferred_element_type=F32)) * \
                jnp.dot(h[t], wu_ref[:, cols], preferred_element_type=F32)
            acc[t] = acc[t] + jnp.dot(act.astype(BF16), wd_ref[cols, :], preferred_element_type=F32)
    for t, rows in enumerate(tiles):
        o_ref[rows, :] = acc[t]


def _ffn(x2d, g, wg, wu, wd):
    t = x2d.shape[0]
    tm = 2 * ROW_TILE
    row = pl.BlockSpec((tm, D_MODEL), lambda i: (i, 0))
    resident = lambda a: pl.BlockSpec(a.shape, lambda i: (0, 0), pipeline_mode=pl.Buffered(1))
    return pl.pallas_call(
        functools.partial(_ffn_kernel, n_chunks=2),
        grid=(t // tm,),
        in_specs=[row, _full(g.shape), resident(wg), resident(wu), resident(wd)],
        out_specs=row,
        out_shape=jax.ShapeDtypeStruct((t, D_MODEL), F32),
        compiler_params=_cparams("parallel"),
        name="ffn_dense",
    )(x2d, g, wg, wu, wd)


def _pack_bf16_pairs(lo, hi):
    lo_bits = pltpu.bitcast(lo.astype(BF16).astype(F32), jnp.uint32)
    hi_bits = pltpu.bitcast(hi.astype(BF16).astype(F32), jnp.uint32)
    return lax.shift_right_logical(lo_bits, jnp.uint32(16)) | (hi_bits & jnp.uint32(0xFFFF0000))


def _unpack_bf16_pairs(words):
    lo = pltpu.bitcast(lax.shift_left(words, jnp.uint32(16)), F32)
    hi = pltpu.bitcast(words & jnp.uint32(0xFFFF0000), F32)
    return lo, hi


def _pack_rows(v):
    q = D_MODEL // 4
    return _pack_bf16_pairs(v[:, 0:q], v[:, q:2 * q]), _pack_bf16_pairs(v[:, 2 * q:3 * q], v[:, 3 * q:])


def _unpack_rows(a, b):
    return jnp.concatenate([*_unpack_bf16_pairs(a), *_unpack_bf16_pairs(b)], axis=-1)


def _route_kernel(x_ref, g_ref, wr_ref, br_ref, tri_ref, ha_ref, hb_ref, route_ref, cnt_ref):
    lane = _lane_iota()
    h = _rms(x_ref[...]) * g_ref[...]
    ha_ref[...], hb_ref[...] = _pack_rows(h)
    h_hi = h.astype(BF16)
    h_lo = (h - h_hi.astype(F32)).astype(BF16)
    logits = (jnp.dot(h_hi, wr_ref[0], preferred_element_type=F32)
              + jnp.dot(h_lo, wr_ref[0], preferred_element_type=F32)
              + jnp.dot(h_hi, wr_ref[1], preferred_element_type=F32)) + br_ref[...]
    logits = jnp.where(lane < N_EXPERTS, logits, -jnp.inf)
    v1 = jnp.max(logits, axis=-1, keepdims=True)
    i1 = jnp.min(jnp.where(logits == v1, lane, LANES), axis=-1, keepdims=True)
    rest = jnp.where(lane == i1, -jnp.inf, logits)
    v2 = jnp.max(rest, axis=-1, keepdims=True)
    i2 = jnp.min(jnp.where(rest == v2, lane, LANES), axis=-1, keepdims=True)
    e2 = jnp.exp(v2 - v1)
    g1 = 1.0 / (1.0 + e2)
    hit1, hit2 = lane == i1, lane == i2
    ones = jnp.where(hit1 | hit2, 1.0, 0.0)
    before = jnp.dot(tri_ref[...], ones.astype(BF16), preferred_element_type=F32)
    r1 = jnp.sum(jnp.where(hit1, before, 0.0), axis=-1, keepdims=True)
    r2 = jnp.sum(jnp.where(hit2, before, 0.0), axis=-1, keepdims=True)
    cols = [i1.astype(F32), i2.astype(F32), g1, e2 * g1, r1, r2]
    route = jnp.zeros(route_ref.shape, F32)
    for n, c in enumerate(cols):
        route = jnp.where(lane == n, c, route)
    route_ref[...] = route
    cnt_ref[0] = jnp.broadcast_to(jnp.sum(ones, axis=0, keepdims=True), cnt_ref.shape[1:])


def _route(x2d, g, wr, br):
    t = x2d.shape[0]
    tm = MOE_CHUNK
    tri = jnp.asarray(np.tril(np.ones((tm, tm), np.float32), -1)).astype(BF16)
    row = lambda w: pl.BlockSpec((tm, w), lambda i: (i, 0))
    q = D_MODEL // 4
    return pl.pallas_call(
        _route_kernel,
        grid=(t // tm,),
        in_specs=[row(D_MODEL), _full(g.shape), _full(wr.shape), _full(br.shape), _full(tri.shape)],
        out_specs=[row(q), row(q), row(LANES), pl.BlockSpec((1, 8, LANES), lambda i: (i, 0, 0))],
        out_shape=[jax.ShapeDtypeStruct((t, q), jnp.uint32), jax.ShapeDtypeStruct((t, q), jnp.uint32),
                   jax.ShapeDtypeStruct((t, LANES), F32), jax.ShapeDtypeStruct((t // tm, 8, LANES), F32)],
        compiler_params=_cparams("parallel"),
        name="moe_route",
    )(x2d, g, wr, br, tri)


def _dest_kernel(route_ref, base_ref, o_ref):
    lane = _lane_iota()
    r = route_ref[...]
    base = base_ref[0, 0:1, :]
    lane_f = lane.astype(F32)
    d1 = jnp.sum(jnp.where(lane_f == r[:, 0:1], base, 0.0), axis=-1, keepdims=True) + r[:, 4:5]
    d2 = jnp.sum(jnp.where(lane_f == r[:, 1:2], base, 0.0), axis=-1, keepdims=True) + r[:, 5:6]
    both = jnp.where(lane == 0, d1, jnp.where(lane == 1, d2, 0.0))
    o_ref[...] = both.T[0:8, :].astype(jnp.int32)


def _destinations(route, base):
    t = route.shape[0]
    tm = MOE_CHUNK
    out = pl.pallas_call(
        _dest_kernel,
        grid=(t // tm,),
        in_specs=[pl.BlockSpec((tm, LANES), lambda i: (i, 0)), pl.BlockSpec((1, 8, LANES), lambda i: (i, 0, 0))],
        out_specs=pl.BlockSpec((8, tm), lambda i: (0, i)),
        out_shape=jax.ShapeDtypeStruct((8, t), jnp.int32),
        compiler_params=_cparams("parallel"),
        name="moe_dest",
    )(route, base)
    return out[0:2].reshape(1, 2 * t)


SC_WINDOW = 128


def _sc_mesh():
    return plsc.VectorSubcoreMesh(core_axis_name="core", subcore_axis_name="subcore")


def _sc_scatter_rows(x, idx, n_out):
    n, d = x.shape
    m = idx.shape[1]
    nblk = n // SC_WINDOW

    @pl.kernel(out_type=jax.ShapeDtypeStruct((n_out, d), x.dtype), mesh=_sc_mesh(), name="moe_sc_scatter")
    def scatter(x_hbm, i_hbm, o_hbm):
        def body(x_vmem, i_vmem):
            pltpu.sync_copy(x_vmem, o_hbm.at[i_vmem.at[0]])

        half = m // SC_WINDOW // 2
        pltpu.emit_pipeline(
            body,
            grid=(2, half),
            in_specs=[pl.BlockSpec((SC_WINDOW, d), lambda c, j: ((c * half + j) % nblk, 0)),
                      pl.BlockSpec((1, SC_WINDOW), lambda c, j: (0, c * half + j))],
            out_specs=[],
            core_axis_name=("core", "subcore"),
            dimension_semantics=(pltpu.PARALLEL, pltpu.PARALLEL),
        )(x_hbm, i_hbm)

    return scatter(x, idx)


def _sc_gather_rows(table, idx):
    d = table.shape[1]
    m = idx.shape[1]

    @pl.kernel(out_type=jax.ShapeDtypeStruct((m, d), table.dtype), mesh=_sc_mesh(), name="moe_sc_gather")
    def gather(t_hbm, i_hbm, o_hbm):
        def body(i_vmem, o_vmem):
            pltpu.sync_copy(t_hbm.at[i_vmem.at[0]], o_vmem)

        half = m // SC_WINDOW // 2
        pltpu.emit_pipeline(
            body,
            grid=(2, half),
            in_specs=[pl.BlockSpec((1, SC_WINDOW), lambda c, j: (0, c * half + j))],
            out_specs=[pl.BlockSpec((SC_WINDOW, d), lambda c, j: (c * half + j, 0))],
            core_axis_name=("core", "subcore"),
            dimension_semantics=(pltpu.PARALLEL, pltpu.PARALLEL),
        )(i_hbm, o_hbm)

    return gather(table, idx)


def _expert_kernel(blk_expert_ref, n_used_ref, xa_ref, xb_ref, wg_ref, wu_ref, wd_ref, ya_ref, yb_ref):
    del blk_expert_ref

    @pl.when(pl.program_id(0) < n_used_ref[0])
    def _():
        xe = _unpack_rows(xa_ref[...], xb_ref[...]).astype(BF16)
        ff = wg_ref.shape[2]
        split = (ff // (2 * LANES) + 1) * LANES
        y = None
        for cols in (slice(0, split), slice(split, ff)):
            act = _silu(jnp.dot(xe, wg_ref[0, :, cols], preferred_element_type=F32)) * \
                jnp.dot(xe, wu_ref[0, :, cols], preferred_element_type=F32)
            part = jnp.dot(act.astype(BF16), wd_ref[0, cols, :], preferred_element_type=F32)
            y = part if y is None else y + part
        ya_ref[...], yb_ref[...] = _pack_rows(y)

    @pl.when(pl.program_id(0) >= n_used_ref[0])
    def _():
        ya_ref[...] = jnp.zeros(ya_ref.shape, ya_ref.dtype)
        yb_ref[...] = jnp.zeros(yb_ref.shape, yb_ref.dtype)


def _experts(blk_expert, n_used, xa, xb, wg, wu, wd):
    n_rows, q = xa.shape
    ff = wg.shape[2]
    blk = MOE_BLOCK
    row = pl.BlockSpec((blk, q), lambda b, be, nu: (b, 0))
    grid_spec = pltpu.PrefetchScalarGridSpec(
        num_scalar_prefetch=2,
        grid=(n_rows // blk,),
        in_specs=[row, row,
                  pl.BlockSpec((1, D_MODEL, ff), lambda b, be, nu: (be[b], 0, 0)),
                  pl.BlockSpec((1, D_MODEL, ff), lambda b, be, nu: (be[b], 0, 0)),
                  pl.BlockSpec((1, ff, D_MODEL), lambda b, be, nu: (be[b], 0, 0))],
        out_specs=[row, row])
    return pl.pallas_call(
        _expert_kernel,
        grid_spec=grid_spec,
        out_shape=[jax.ShapeDtypeStruct((n_rows, q), jnp.uint32)] * 2,
        compiler_params=_cparams("arbitrary"),
        name="moe_experts",
    )(blk_expert, n_used, xa, xb, wg, wu, wd)


def _combine_kernel(x_ref, route_ref, a1_ref, b1_ref, a2_ref, b2_ref, o_ref):
    g1 = route_ref[:, 2:3]
    g2 = route_ref[:, 3:4]
    o_ref[...] = x_ref[...] + g1 * _unpack_rows(a1_ref[...], b1_ref[...]) \
        + g2 * _unpack_rows(a2_ref[...], b2_ref[...])


def _combine(x2d, route, ya, yb):
    t = x2d.shape[0]
    tm = ROW_TILE
    nt = t // tm
    q = ya.shape[1]
    row = lambda w: pl.BlockSpec((tm, w), lambda i: (i, 0))
    first = pl.BlockSpec((tm, q), lambda i: (i, 0))
    second = pl.BlockSpec((tm, q), lambda i: (nt + i, 0))
    return pl.pallas_call(
        _combine_kernel,
        grid=(nt,),
        in_specs=[row(D_MODEL), row(LANES), first, first, second, second],
        out_specs=row(D_MODEL),
        out_shape=jax.ShapeDtypeStruct((t, D_MODEL), F32),
        compiler_params=_cparams("parallel"),
        name="moe_combine",
    )(x2d, route, ya, yb, ya, yb)


def _moe_sorted(x2d, g, wr, br, wg, wu, wd):
    t = x2d.shape[0]
    blk = MOE_BLOCK
    n_blocks = 2 * t // blk + N_EXPERTS
    ha, hb, route, cnt = _route(x2d, g, wr, br)

    cnt = cnt[:, 0, :N_EXPERTS].astype(jnp.int32)
    before_chunk = jnp.cumsum(cnt, axis=0) - cnt
    seg_blocks = (jnp.sum(cnt, axis=0) + blk - 1) // blk
    seg_end_blk = jnp.cumsum(seg_blocks)
    seg_start = (seg_end_blk - seg_blocks) * blk
    base = jnp.pad((seg_start[None, :] + before_chunk).astype(F32), ((0, 0), (0, LANES - N_EXPERTS)))
    idx = _destinations(route, jnp.broadcast_to(base[:, None, :], (base.shape[0], 8, LANES)))
    past_end = jnp.arange(n_blocks, dtype=jnp.int32)[:, None] >= seg_end_blk[None, :]
    blk_expert = jnp.minimum(jnp.sum(past_end, axis=1), N_EXPERTS - 1).astype(jnp.int32)
    n_used = seg_end_blk[-1:].astype(jnp.int32)

    xa = _sc_scatter_rows(ha, idx, n_blocks * blk)
    xb = _sc_scatter_rows(hb, idx, n_blocks * blk)
    ya, yb = _experts(blk_expert, n_used, xa, xb, wg, wu, wd)
    return _combine(x2d, route, _sc_gather_rows(ya, idx), _sc_gather_rows(yb, idx))


def _pad_cols(w, width):
    return jnp.pad(w, ((0, 0), (0, width - w.shape[1])))


def _layer_params(l, mix_norm, w_in, b_forget, a_q_norm, a_k_norm, b_v_norm, b_spatial_w, b_spatial_b,
                  c_q_lat_norm, c_w_uq, c_kv_lat_norm, c_w_ukv, c_q_nope_norm, c_q_rope_norm,
                  c_k_nope_norm, c_k_rope_norm, out_norm_a, out_norm_b, out_norm_c, w_out,
                  xattn_norm, mem_norm, w_mem_q, w_mem_kv, m_q_norm, m_k_norm, w_mem_out):
    p = {}
    o = np.cumsum((0, A_W, A_W, A_W, A_HEADS, B_W, B_W, C_Q_RANK, C_KV_RANK, C_ROPE_DIM))
    w = w_in[l]
    seg = lambda n: w[:, o[n]:o[n + 1]]
    fa = seg(3)
    misc = jnp.zeros((D_MODEL, LANES), F32)
    misc = misc.at[:, ROPE_LANE:ROPE_LANE + C_ROPE_DIM].set(seg(8))
    fb = jnp.zeros((1, LANES), F32)
    for hd in range(A_HEADS):
        ln = FORGET_LANE + 8 * (hd // 2) + hd % 2
        misc = misc.at[:, ln].set(fa[:, hd])
        fb = fb.at[0, ln].set(b_forget[l, hd])
    p["w_in"] = jnp.concatenate([seg(0), seg(1), seg(2), seg(4), seg(5), seg(6), seg(7), misc], axis=1).astype(BF16)
    p["fb"] = fb
    p["mix_g"] = mix_norm[l][None]
    p["aq"] = jnp.tile(a_q_norm[l], A_HEADS)[None] * (A_HEAD_DIM ** -0.5 * LOG2E)
    p["ak"] = jnp.tile(a_k_norm[l], A_HEADS)[None]
    p["bvg"] = b_v_norm[l][None]
    pos = np.arange(B_WINDOW)
    mask = (pos[None, :] // CHUNK) <= (pos[:, None] // CHUNK)
    p["ws"] = jnp.where(mask[None], b_spatial_w[l], 0.0).reshape(B_GROUPS * B_WINDOW, B_WINDOW).astype(BF16)
    p["bs"] = jnp.repeat(b_spatial_b[l].T, B_GROUP_DIM, axis=1)
    p["onb"] = out_norm_b[l][None]
    p["cqg"] = c_q_lat_norm[l][None]
    p["ckvg"] = c_kv_lat_norm[l][None]
    gidx = np.arange(A_W) // A_HEAD_DIM
    p["gm"] = jnp.asarray((gidx[:, None] == gidx[None, :]).astype(np.float32) / A_HEAD_DIM).astype(BF16)
    p["tri"] = jnp.asarray(np.tril(np.ones((ROW_TILE, ROW_TILE), np.float32))).astype(BF16)

    qd = C_NOPE_DIM + C_ROPE_DIM
    half = C_ROPE_DIM // 2
    wq = c_w_uq[l]
    wq_partner = jnp.concatenate([jnp.zeros_like(wq[:, :, :C_NOPE_DIM]), wq[:, :, C_NOPE_DIM + half:],
                                  wq[:, :, C_NOPE_DIM:C_NOPE_DIM + half]], axis=-1)
    p["wuq"] = jnp.pad(jnp.stack([wq, wq_partner]), ((0, 0), (0, 0), (0, 0), (0, HEAD_SLAB - qd))
                       ).reshape(2, C_Q_RANK, -1).astype(BF16)
    wukv = c_w_ukv[l]
    p["wuk"] = jnp.pad(wukv[:, :, :C_NOPE_DIM], ((0, 0), (0, 0), (0, HEAD_SLAB - C_NOPE_DIM))
                       ).reshape(C_KV_RANK, -1).astype(BF16)
    p["wuv"] = wukv[:, :, C_NOPE_DIM:].reshape(C_KV_RANK, C_W).astype(BF16)
    gq = jnp.concatenate([c_q_nope_norm[l], c_q_rope_norm[l]])
    gq_partner = jnp.concatenate([jnp.zeros_like(c_q_nope_norm[l]), c_q_rope_norm[l][half:],
                                  c_q_rope_norm[l][:half]])
    p["gq"] = _pad_cols(jnp.stack([gq, gq_partner]) * (qd ** -0.5 * LOG2E), LANES)
    p["gkn"] = _pad_cols(c_k_nope_norm[l][None], LANES)
    seg = np.zeros((LANES, LANES), np.float32)
    seg[:C_NOPE_DIM, :C_NOPE_DIM] = 1.0 / C_NOPE_DIM
    seg[ROPE_LANE:ROPE_LANE + C_ROPE_DIM, ROPE_LANE:ROPE_LANE + C_ROPE_DIM] = 1.0 / C_ROPE_DIM
    p["seg"] = jnp.asarray(seg).astype(BF16)
    p["gkr"] = jnp.zeros((1, LANES), F32).at[0, ROPE_LANE:ROPE_LANE + C_ROPE_DIM].set(c_k_rope_norm[l])

    p["ona"] = out_norm_a[l][None]
    p["onc"] = out_norm_c[l][None]
    p["w_out"] = w_out[l].astype(BF16)
    p["xg"] = xattn_norm[l][None]
    p["w_mem_q"] = w_mem_q[l].astype(BF16)
    p["mqg"] = m_q_norm[l][None] * (M_HEAD_DIM ** -0.5)
    p["mem_g"] = mem_norm[l][None]
    p["w_mem_kv"] = w_mem_kv[l].astype(BF16)
    p["mkg"] = m_k_norm[l][None]
    p["w_mem_out"] = w_mem_out[l].astype(BF16)
    return p


def kernel(x, mem, positions, mix_norm, w_in, b_forget, a_q_norm, a_k_norm, b_v_norm, b_spatial_w, b_spatial_b, c_q_lat_norm, c_w_uq, c_kv_lat_norm, c_w_ukv, c_q_nope_norm, c_q_rope_norm, c_k_nope_norm, c_k_rope_norm, out_norm_a, out_norm_b, out_norm_c, w_out, xattn_norm, mem_norm, w_mem_q, w_mem_kv, m_q_norm, m_k_norm, w_mem_out, ffn_norm, ffn_w_gate, ffn_w_up, ffn_w_down, w_router, b_router, moe_w_gate, moe_w_up, moe_w_down):
    nb, seq, d = x.shape
    assert d == D_MODEL and seq % (2 * ROW_TILE) == 0 and seq % ATTN_TILE == 0
    assert (nb * seq) % MOE_CHUNK == 0 and (nb * seq) % SC_WINDOW == 0 and (2 * nb * seq) % MOE_BLOCK == 0
    depth = w_in.shape[0]
    t = nb * seq
    x2d = x.reshape(t, d)
    cos, sin = _rope_tables(positions.reshape(t, 1).astype(F32))

    for l in range(depth):
        p = _layer_params(l, mix_norm, w_in, b_forget, a_q_norm, a_k_norm, b_v_norm, b_spatial_w,
                          b_spatial_b, c_q_lat_norm, c_w_uq, c_kv_lat_norm, c_w_ukv, c_q_nope_norm,
                          c_q_rope_norm, c_k_nope_norm, c_k_rope_norm, out_norm_a, out_norm_b,
                          out_norm_c, w_out, xattn_norm, mem_norm, w_mem_q, w_mem_kv, m_q_norm,
                          m_k_norm, w_mem_out)
        qa, ka, vta, bn, cq, ckv, misc = _inproj(x2d, seq, p)
        qc, kc, vtc = _mla_prep(cq, ckv, misc, cos, sin, seq, p)
        a = _attention(qa, ka, vta, unit=1, name="attn_fox")
        c = _attention(qc, kc, vtc, unit=CHUNK, name="attn_mla")
        km, vm = _mem_kv(mem, p)
        x2d = _outproj(x2d, a, bn, c, km, vm, seq, p)
        g = ffn_norm[l][None]
        if l % 2 == 0:
            m = l // 2
            ff = ffn_w_gate.shape[2]
            ff_pad = -(-ff // (2 * LANES)) * (2 * LANES)
            wg = _pad_cols(ffn_w_gate[m], ff_pad).astype(BF16)
            wu = _pad_cols(ffn_w_up[m], ff_pad).astype(BF16)
            wd = jnp.pad(ffn_w_down[m], ((0, ff_pad - ff), (0, 0))).astype(BF16)
            x2d = _ffn(x2d, g, wg, wu, wd)
        else:
            m = l // 2
            wr = _pad_cols(w_router[m], LANES)
            wr_hi = wr.astype(BF16)
            wr = jnp.stack([wr_hi, (wr - wr_hi.astype(F32)).astype(BF16)])
            br = _pad_cols(b_router[m][None], LANES)
            x2d = _moe_sorted(x2d, g, wr, br, moe_w_gate[m].astype(BF16), moe_w_up[m].astype(BF16),
                              moe_w_down[m].astype(BF16))
    return x2d.reshape(nb, seq, d)
```

```python
import functools

import numpy as np
import jax
import jax.numpy as jnp
from jax import lax
from jax.experimental import pallas as pl
from jax.experimental.pallas import tpu as pltpu
from jax.experimental.pallas import tpu_sc as plsc

F32 = jnp.float32
BF16 = jnp.bfloat16

D_MODEL = 1024
CHUNK = 64
EPS = 1e-6
NEG_INF = -1e30
A_HEADS, A_HEAD_DIM = 4, 64
B_GROUPS, B_GROUP_DIM, B_WINDOW = 4, 64, 128
C_HEADS, C_NOPE_DIM, C_ROPE_DIM, C_V_DIM = 8, 64, 32, 64
C_Q_RANK, C_KV_RANK = 256, 128
ROPE_THETA = 10000.0
M_HEADS, M_HEAD_DIM = 4, 128
N_EXPERTS = 8
A_W = A_HEADS * A_HEAD_DIM
B_W = B_GROUPS * B_GROUP_DIM
C_W = C_HEADS * C_V_DIM
M_W = M_HEADS * M_HEAD_DIM

LANES = 128

SEG_Q, SEG_K, SEG_V, SEG_U, SEG_VB, SEG_CQ, SEG_CKV, SEG_MISC = 0, 256, 512, 768, 1024, 1280, 1536, 1664
IN_PAD_W = SEG_MISC + LANES
ROPE_LANE = C_NOPE_DIM
FORGET_LANE = 96
HEAD_SLAB = LANES
VT_ROWS = 80
LOG2E = float(np.log2(np.e))

ROW_TILE = 512
ATTN_TILE = 512
ATTN_HEADS = 2
MOE_CHUNK = 1024
MOE_BLOCK = 512
VMEM_LIMIT = 56 * 1024 * 1024


def _cparams(*sem):
    return pltpu.CompilerParams(dimension_semantics=sem, vmem_limit_bytes=VMEM_LIMIT)


def _full(shape):
    n = len(shape)
    return pl.BlockSpec(shape, lambda *_: (0,) * n)


def _rms(x):
    return x * lax.rsqrt(jnp.mean(x * x, axis=-1, keepdims=True) + EPS)


def _dot_split(v, exact, pieces, lhs_is_exact=False):
    total = None
    rem = v
    for n in range(pieces):
        part = rem.astype(BF16)
        if n + 1 < pieces:
            rem = rem - part.astype(F32)
        term = (jnp.dot(exact, part, preferred_element_type=F32) if lhs_is_exact
                else jnp.dot(part, exact, preferred_element_type=F32))
        total = term if total is None else total + term
    return total


def _lane_iota(n=LANES):
    return lax.broadcasted_iota(jnp.int32, (1, n), 1)


def _rope_table_kernel(pos_ref, inv_ref, sgn_ref, cos_ref, sin_ref):
    ang = pos_ref[...] * inv_ref[...]
    cos_ref[...] = jnp.cos(ang)
    sin_ref[...] = jnp.sin(ang) * sgn_ref[...]


def _rope_tables(pos_col):
    t = pos_col.shape[0]
    half = C_ROPE_DIM // 2
    inv = ROPE_THETA ** (-jnp.arange(half, dtype=F32) / half)
    inv_l = jnp.zeros((1, LANES), F32).at[0, ROPE_LANE:ROPE_LANE + C_ROPE_DIM].set(jnp.tile(inv, 2))
    sgn = np.zeros((1, LANES), np.float32)
    sgn[0, ROPE_LANE:ROPE_LANE + half] = -1.0
    sgn[0, ROPE_LANE + half:ROPE_LANE + C_ROPE_DIM] = 1.0
    tm = ROW_TILE
    return pl.pallas_call(
        _rope_table_kernel,
        grid=(t // tm,),
        in_specs=[pl.BlockSpec((tm, 1), lambda i: (i, 0)), _full((1, LANES)), _full((1, LANES))],
        out_specs=[pl.BlockSpec((tm, LANES), lambda i: (i, 0))] * 2,
        out_shape=[jax.ShapeDtypeStruct((t, LANES), F32)] * 2,
        compiler_params=_cparams("parallel"),
        name="rope_tables",
    )(pos_col, inv_l, jnp.asarray(sgn))


def _rotate(x, cos, sin_signed, lane):
    half = C_ROPE_DIM // 2
    partner = jnp.where(lane < ROPE_LANE + half,
                        pltpu.roll(x, LANES - half, 1), pltpu.roll(x, half, 1))
    return x * cos + partner * sin_signed


def _store_v_transposed(vt_ref, v, n_heads, lanes=slice(None)):
    tm = v.shape[0]
    v_t = v.T
    tail = jnp.where(lax.broadcasted_iota(jnp.int32, (VT_ROWS - C_V_DIM, tm), 0) == 0, 1.0, 0.0).astype(BF16)
    for hd in range(n_heads):
        vt_ref[0, hd, 0:C_V_DIM, lanes] = v_t[hd * C_V_DIM:(hd + 1) * C_V_DIM, :].astype(BF16)
        vt_ref[0, hd, C_V_DIM:VT_ROWS, lanes] = tail


def _gelu(x):
    return 0.5 * x * (1.0 + lax.erf(x * np.float32(1.0 / np.sqrt(2.0))))


def _inproj_kernel(x0_ref, xa_ref, xb_ref, g_ref, w_ref, aq_ref, ak_ref, fb_ref, bvg_ref, ws_ref, bs_ref,
                   onb_ref, cqg_ref, ckvg_ref, gm_ref, tri_ref,
                   qa_ref, ka_ref, vt_ref, bn_ref, cq_ref, ckv_ref, misc_ref,
                   carry_ref, buf0_ref, buf1_ref, *, tiles_per_seq):
    i = pl.program_id(0)
    tm = xa_ref.shape[0]

    def projection_parts(x_ref, buf_ref):
        h = (_rms(x_ref[...]) * g_ref[...]).astype(BF16)

        def part(lo, hi):
            def run():
                buf_ref[:, lo:hi] = jnp.dot(h, w_ref[:, lo:hi], preferred_element_type=F32)
            return run
        return [part(SEG_Q, SEG_U), part(SEG_U, SEG_CQ), part(SEG_CQ, IN_PAD_W)]

    @pl.when(i == 0)
    def _():
        for run in projection_parts(x0_ref, buf0_ref):
            run()

    refs = (aq_ref, ak_ref, fb_ref, bvg_ref, ws_ref, bs_ref, onb_ref, cqg_ref, ckvg_ref, gm_ref, tri_ref,
            qa_ref, ka_ref, vt_ref, bn_ref, cq_ref, ckv_ref, misc_ref, carry_ref)
    _mixer_prologues(buf0_ref, 0, 2 * i, tm, tiles_per_seq, projection_parts(xa_ref, buf1_ref), *refs)
    _mixer_prologues(buf1_ref, 1, 2 * i + 1, tm, tiles_per_seq, projection_parts(xb_ref, buf0_ref), *refs)


def _mixer_prologues(proj, half, tile, tm, tiles_per_seq, between, aq_ref, ak_ref, fb_ref, bvg_ref, ws_ref,
                     bs_ref, onb_ref, cqg_ref, ckvg_ref, gm_ref, tri_ref, qa_ref, ka_ref, vt_ref, bn_ref,
                     cq_ref, ckv_ref, misc_ref, carry_ref):
    out_rows = slice(half * tm, (half + 1) * tm)
    gm = gm_ref[...]

    def group_mean(v):
        return _dot_split(v, gm, 2)

    misc = proj[:, SEG_MISC:SEG_MISC + LANES]
    misc_ref[out_rows, :] = misc
    z = misc + fb_ref[...]
    log_f = jnp.minimum(z, 0.0) - jnp.log1p(jnp.exp(-jnp.abs(z)))
    carry = jnp.where(tile % tiles_per_seq == 0, 0.0, carry_ref[...])
    cum = _dot_split(log_f, tri_ref[...], 3, lhs_is_exact=True) + carry
    carry_ref[...] = cum[tm - 1:tm, :]
    f_hi = (cum * LOG2E).astype(BF16).astype(F32)
    f_rem = cum * LOG2E - f_hi
    f_mid = f_rem.astype(BF16).astype(F32)
    f_lo = f_rem - f_mid

    between[0]()
    q = proj[:, SEG_Q:SEG_Q + A_W]
    qn = q * lax.rsqrt(group_mean(q * q) + EPS) * aq_ref[...]
    k = proj[:, SEG_K:SEG_K + A_W]
    kn = k * lax.rsqrt(group_mean(k * k) + EPS) * ak_ref[...]
    lane = _lane_iota()
    for hd in range(A_HEADS):
        pair = slice((hd // 2) * LANES, (hd // 2 + 1) * LANES)
        slab = slice(hd * HEAD_SLAB, (hd + 1) * HEAD_SLAB)
        data = (lane < A_HEAD_DIM) if hd % 2 == 0 else (lane >= A_HEAD_DIM)
        e0 = A_HEAD_DIM if hd % 2 == 0 else 0
        fl = FORGET_LANE + 8 * (hd // 2) + hd % 2
        ones = jnp.where((lane >= e0) & (lane < e0 + 3), 1.0, 0.0)
        qa_ref[0, hd, :, out_rows] = jnp.where(data, qn[:, pair], ones).T.astype(BF16)
        bias = jnp.where(lane == e0, -f_hi[:, fl:fl + 1],
                         jnp.where(lane == e0 + 1, -f_mid[:, fl:fl + 1],
                                   jnp.where(lane == e0 + 2, -f_lo[:, fl:fl + 1], 0.0)))
        ka_ref[out_rows, slab] = jnp.where(data, kn[:, pair], bias).astype(BF16)
    _store_v_transposed(vt_ref, proj[:, SEG_V:SEG_V + A_W], A_HEADS, out_rows)

    between[1]()
    u = _gelu(proj[:, SEG_U:SEG_U + B_W])
    v = _gelu(proj[:, SEG_VB:SEG_VB + B_W])
    dv = v - group_mean(v)
    vn = dv * lax.rsqrt(group_mean(dv * dv) + EPS) * bvg_ref[...]
    group = lax.broadcasted_iota(jnp.int32, (1, B_W), 1) // B_GROUP_DIM
    for w in range(tm // B_WINDOW):
        rows = slice(w * B_WINDOW, (w + 1) * B_WINDOW)
        y_all = jnp.dot(ws_ref[...], vn[rows].astype(BF16), preferred_element_type=F32)
        y = bs_ref[...]
        for g in range(B_GROUPS):
            y = y + jnp.where(group == g, y_all[g * B_WINDOW:(g + 1) * B_WINDOW], 0.0)
        b = u[rows] * y
        bn_ref[half * tm + w * B_WINDOW:half * tm + (w + 1) * B_WINDOW, :] = (_rms(b) * onb_ref[...]).astype(BF16)

    between[2]()
    cq_ref[out_rows, :] = (_rms(proj[:, SEG_CQ:SEG_CQ + C_Q_RANK]) * cqg_ref[...]).astype(BF16)
    ckv_ref[out_rows, :] = (_rms(proj[:, SEG_CKV:SEG_CKV + C_KV_RANK]) * ckvg_ref[...]).astype(BF16)


def _inproj(x2d, seq, p):
    t = x2d.shape[0]
    tm = ROW_TILE
    tps = seq // tm
    nb = t // seq
    n_tiles = t // tm
    steps_per_seq = tps // 2
    row = lambda w: pl.BlockSpec((2 * tm, w), lambda i: (i, 0))
    x_tile = lambda index: pl.BlockSpec((tm, D_MODEL), lambda i: (index(i), 0))
    qk_w = A_HEADS * HEAD_SLAB
    per_head = lambda rows: pl.BlockSpec((1, A_HEADS, rows, 2 * tm),
                                         lambda i: (i // steps_per_seq, 0, 0, i % steps_per_seq))
    out_shape = [
        jax.ShapeDtypeStruct((nb, A_HEADS, HEAD_SLAB, seq), BF16), jax.ShapeDtypeStruct((t, qk_w), BF16),
        jax.ShapeDtypeStruct((nb, A_HEADS, VT_ROWS, seq), BF16),
        jax.ShapeDtypeStruct((t, B_W), BF16),
        jax.ShapeDtypeStruct((t, C_Q_RANK), BF16), jax.ShapeDtypeStruct((t, C_KV_RANK), BF16),
        jax.ShapeDtypeStruct((t, LANES), F32),
    ]
    out_specs = [per_head(HEAD_SLAB), row(qk_w), per_head(VT_ROWS),
                 row(B_W), row(C_Q_RANK), row(C_KV_RANK), row(LANES)]
    consts = [p["mix_g"], p["w_in"], p["aq"], p["ak"], p["fb"], p["bvg"], p["ws"], p["bs"], p["onb"],
              p["cqg"], p["ckvg"], p["gm"], p["tri"]]
    return pl.pallas_call(
        functools.partial(_inproj_kernel, tiles_per_seq=tps),
        grid=(n_tiles // 2,),
        in_specs=[x_tile(lambda i: 0), x_tile(lambda i: 2 * i + 1),
                  x_tile(lambda i: jnp.minimum(2 * i + 2, n_tiles - 1))] + [_full(c.shape) for c in consts],
        out_specs=out_specs,
        out_shape=out_shape,
        scratch_shapes=[pltpu.VMEM((1, LANES), F32), pltpu.VMEM((tm, IN_PAD_W), F32),
                        pltpu.VMEM((tm, IN_PAD_W), F32)],
        compiler_params=_cparams("arbitrary"),
        name="in_proj",
    )(x2d, x2d, x2d, *consts)


def _mla_prep_kernel(cq_ref, ckv_ref, misc_ref, cos_ref, sin_ref, wuq_ref, wuk_ref, wuv_ref,
                     gq_ref, gkn_ref, gkr_ref, seg_ref, qc_ref, kc_ref, vt_ref):
    tm = cq_ref.shape[0] // 2
    tiles = [slice(0, tm), slice(tm, 2 * tm)]
    lane = _lane_iota()
    rope = (lane >= ROPE_LANE) & (lane < ROPE_LANE + C_ROPE_DIM)
    seg = seg_ref[...]

    def inv_rms(v):
        return lax.rsqrt(jnp.dot((v * v).astype(BF16), seg, preferred_element_type=F32) + EPS)

    q, q_partner, kn, q_cos, q_sin, kr = [], [], [], [], [], []
    for rows in tiles:
        cos, sin = cos_ref[rows, :], sin_ref[rows, :]
        q.append(jnp.dot(cq_ref[rows, :], wuq_ref[0], preferred_element_type=F32))
        q_partner.append(jnp.dot(cq_ref[rows, :], wuq_ref[1], preferred_element_type=F32))
        q_cos.append(gq_ref[0:1, :] * cos)
        q_sin.append(gq_ref[1:2, :] * sin)
        kn.append(jnp.dot(ckv_ref[rows, :], wuk_ref[...], preferred_element_type=F32))
        _store_v_transposed(vt_ref, jnp.dot(ckv_ref[rows, :], wuv_ref[...], preferred_element_type=F32),
                            C_HEADS, rows)
        k_rot = jnp.where(rope, misc_ref[rows, :], 0.0)
        kr.append(_rotate(k_rot * inv_rms(k_rot) * gkr_ref[...], cos, sin, lane))

    for hd in range(C_HEADS):
        cols = slice(hd * HEAD_SLAB, (hd + 1) * HEAD_SLAB)
        for t, rows in enumerate(tiles):
            qh = q[t][:, cols]
            qc_ref[0, hd, :, rows] = (inv_rms(qh) * (qh * q_cos[t] + q_partner[t][:, cols] * q_sin[t])
                                      ).T.astype(BF16)
            kh = kn[t][:, cols]
            kc_ref[rows, cols] = (kh * inv_rms(kh) * gkn_ref[...] + kr[t]).astype(BF16)


def _mla_prep(cq, ckv, misc, cos, sin, seq, p):
    t = cq.shape[0]
    tm = 2 * ROW_TILE
    tps = seq // tm
    row = lambda w: pl.BlockSpec((tm, w), lambda i: (i, 0))
    consts = [p["wuq"], p["wuk"], p["wuv"], p["gq"], p["gkn"], p["gkr"], p["seg"]]
    qk_w = C_HEADS * HEAD_SLAB
    return pl.pallas_call(
        _mla_prep_kernel,
        grid=(t // tm,),
        in_specs=[row(C_Q_RANK), row(C_KV_RANK), row(LANES), row(LANES), row(LANES)]
                 + [_full(c.shape) for c in consts],
        out_specs=[pl.BlockSpec((1, C_HEADS, HEAD_SLAB, tm), lambda i: (i // tps, 0, 0, i % tps)), row(qk_w),
                   pl.BlockSpec((1, C_HEADS, VT_ROWS, tm), lambda i: (i // tps, 0, 0, i % tps))],
        out_shape=[jax.ShapeDtypeStruct((t // seq, C_HEADS, HEAD_SLAB, seq), BF16),
                   jax.ShapeDtypeStruct((t, qk_w), BF16),
                   jax.ShapeDtypeStruct((t // seq, C_HEADS, VT_ROWS, seq), BF16)],
        compiler_params=_cparams("parallel"),
        name="mla_prep",
    )(cq, ckv, misc, cos, sin, *consts)


def _attn_items(nq):
    return [(i, j) for i in range(nq) for j in range(i + 1)]


def _attn_kernel(qt_ref, k_ref, vt_ref, mask_ref, o_ref, s0_ref, s1_ref, p0_ref, p1_ref,
                 mp0_ref, mp1_ref, mrun_ref, macc_ref, acc_ref):
    tk, tq = mask_ref.shape
    nq = k_ref.shape[0] // tq
    n_h = k_ref.shape[1] // HEAD_SLAB
    items = _attn_items(nq)
    n_items = len(items)
    s_bufs, p_bufs, mp_bufs = (s0_ref, s1_ref), (p0_ref, p1_ref), (mp0_ref, mp1_ref)
    mrun_ref[...] = jnp.full(mrun_ref.shape, NEG_INF, F32)
    macc_ref[...] = jnp.full(macc_ref.shape, NEG_INF, F32)
    acc_ref[...] = jnp.zeros(acc_ref.shape, F32)

    half = tk // 2
    top, bot, left, right = slice(0, half), slice(half, tk), slice(0, half), slice(half, tq)

    def scores(it, buf):
        qi, kj = items[it]
        k_rows, q_rows = kj * tk, qi * tq
        for hh in range(n_h):
            cols = slice(hh * HEAD_SLAB, (hh + 1) * HEAD_SLAB)
            if qi != kj:
                s_t = jnp.dot(k_ref[k_rows:k_rows + tk, cols], qt_ref[0, hh, :, q_rows:q_rows + tq],
                              preferred_element_type=F32)
                s_bufs[buf][hh] = s_t
                col_max = jnp.max(s_t, axis=0, keepdims=True)
            else:
                s_top = jnp.dot(k_ref[k_rows:k_rows + half, cols], qt_ref[0, hh, :, q_rows:q_rows + tq],
                                preferred_element_type=F32)
                s_top = jnp.concatenate([s_top[:, left] + mask_ref[top, left], s_top[:, right]], axis=1)
                s_br = jnp.dot(k_ref[k_rows + half:k_rows + tk, cols],
                               qt_ref[0, hh, :, q_rows + half:q_rows + tq],
                               preferred_element_type=F32) + mask_ref[bot, right]
                s_bufs[buf][hh, top, :] = s_top
                s_bufs[buf][hh, bot, right] = s_br
                max_top = jnp.max(s_top, axis=0, keepdims=True)
                col_max = jnp.concatenate(
                    [max_top[:, left], jnp.maximum(max_top[:, right], jnp.max(s_br, axis=0, keepdims=True))],
                    axis=1)
            mrun_ref[qi, hh] = jnp.maximum(mrun_ref[qi, hh], col_max)

    def exponentiate(it, buf):
        qi, kj = items[it]
        for hh in range(n_h):
            m = mrun_ref[qi, hh]
            if qi != kj:
                p_bufs[buf][hh] = jnp.exp2(s_bufs[buf][hh] - m).astype(BF16)
            else:
                p_bufs[buf][hh, top, :] = jnp.exp2(s_bufs[buf][hh, top, :] - m).astype(BF16)
                p_bufs[buf][hh, bot, right] = jnp.exp2(s_bufs[buf][hh, bot, right] - m[:, right]).astype(BF16)
            mp_bufs[buf][hh] = m

    def accumulate(it, buf):
        qi, kj = items[it]
        k_rows = kj * tk
        for hh in range(n_h):
            m = mp_bufs[buf][hh]
            if qi != kj:
                pv = jnp.dot(vt_ref[0, hh, :, k_rows:k_rows + tk], p_bufs[buf][hh],
                             preferred_element_type=F32)
            else:
                pv = jnp.concatenate(
                    [jnp.dot(vt_ref[0, hh, :, k_rows:k_rows + half], p_bufs[buf][hh, top, left],
                             preferred_element_type=F32),
                     jnp.dot(vt_ref[0, hh, :, k_rows:k_rows + tk], p_bufs[buf][hh, :, right],
                             preferred_element_type=F32)], axis=1)
            acc_ref[qi, hh] = jnp.exp2(macc_ref[qi, hh] - m) * acc_ref[qi, hh] + pv
            macc_ref[qi, hh] = m

    for it in range(n_items + 2):
        par = it % 2
        if 2 <= it:
            accumulate(it - 2, par)
            qi, kj = items[it - 2]
            if qi == kj:
                halves = [acc_ref[qi, hh, 0:C_V_DIM, :] / acc_ref[qi, hh, C_V_DIM:C_V_DIM + 1, :]
                          for hh in range(n_h)]
                o_ref[qi * tq:(qi + 1) * tq, :] = jnp.concatenate(halves, axis=0).T.astype(o_ref.dtype)
        if 1 <= it <= n_items:
            exponentiate(it - 1, 1 - par)
        if it < n_items:
            scores(it, par)


def _attention(q, k, vt, *, unit, name):
    t = k.shape[0]
    nb, n_heads, _, seq = vt.shape
    tq = ATTN_TILE
    nq = seq // tq
    assert (tq // 2) % unit == 0
    pos = np.arange(tq)
    diag_mask = np.where((pos[:, None] // unit) <= (pos[None, :] // unit), 0.0, NEG_INF)
    mask = jnp.asarray(diag_mask.astype(np.float32))
    n_h = ATTN_HEADS
    seq_blk = lambda w: pl.BlockSpec((seq, w), lambda b, p: (b, p))
    return pl.pallas_call(
        _attn_kernel,
        grid=(nb, n_heads // n_h),
        in_specs=[pl.BlockSpec((1, n_h, HEAD_SLAB, seq), lambda b, p: (b, p, 0, 0)), seq_blk(n_h * HEAD_SLAB),
                  pl.BlockSpec((1, n_h, VT_ROWS, seq), lambda b, p: (b, p, 0, 0)),
                  pl.BlockSpec((tq, tq), lambda b, p: (0, 0), pipeline_mode=pl.Buffered(1))],
        out_specs=seq_blk(n_h * C_V_DIM),
        out_shape=jax.ShapeDtypeStruct((t, n_heads * C_V_DIM), BF16),
        scratch_shapes=[pltpu.VMEM((n_h, tq, tq), F32), pltpu.VMEM((n_h, tq, tq), F32),
                        pltpu.VMEM((n_h, tq, tq), BF16), pltpu.VMEM((n_h, tq, tq), BF16),
                        pltpu.VMEM((n_h, 1, tq), F32), pltpu.VMEM((n_h, 1, tq), F32),
                        pltpu.VMEM((nq, n_h, 1, tq), F32), pltpu.VMEM((nq, n_h, 1, tq), F32),
                        pltpu.VMEM((nq, n_h, VT_ROWS, tq), F32)],
        compiler_params=_cparams("parallel", "parallel"),
        name=name,
    )(q, k, vt, mask)


def _mem_kv_kernel(mem_ref, g_ref, w_ref, kg_ref, k_ref, v_ref):
    mn = (_rms(mem_ref[0]) * g_ref[...]).astype(BF16)
    kv = jnp.dot(mn, w_ref[...], preferred_element_type=F32)
    for hd in range(M_HEADS):
        cols = slice(hd * M_HEAD_DIM, (hd + 1) * M_HEAD_DIM)
        k_ref[0, :, cols] = (_rms(kv[:, cols]) * kg_ref[...]).astype(BF16)
    v_ref[0] = kv[:, M_W:].astype(BF16)


def _mem_kv(mem, p):
    nb, ml, _ = mem.shape
    consts = [p["mem_g"], p["w_mem_kv"], p["mkg"]]
    blk = pl.BlockSpec((1, ml, M_W), lambda b: (b, 0, 0))
    return pl.pallas_call(
        _mem_kv_kernel,
        grid=(nb,),
        in_specs=[pl.BlockSpec((1, ml, D_MODEL), lambda b: (b, 0, 0))] + [_full(c.shape) for c in consts],
        out_specs=[blk, blk],
        out_shape=[jax.ShapeDtypeStruct((nb, ml, M_W), BF16)] * 2,
        compiler_params=_cparams("parallel"),
        name="mem_kv",
    )(mem, *consts)


def _outproj_kernel(x_ref, a_ref, bn_ref, c_ref, ona_ref, onc_ref, wo_ref, xg_ref, wq_ref, mqg_ref,
                    km_ref, vm_ref, wmo_ref, o_ref):
    tm = x_ref.shape[0] // 2
    tiles = [slice(0, tm), slice(tm, 2 * tm)]
    x1, q, outs = [], [], [[], []]
    for rows in tiles:
        a_n = (_rms(a_ref[rows, :].astype(F32)) * ona_ref[...]).astype(BF16)
        c_n = (_rms(c_ref[rows, :].astype(F32)) * onc_ref[...]).astype(BF16)
        mix = jnp.concatenate([a_n, bn_ref[rows, :], c_n], axis=-1)
        x1.append(x_ref[rows, :] + jnp.dot(mix, wo_ref[...], preferred_element_type=F32))
    for t in range(2):
        h = (_rms(x1[t]) * xg_ref[...]).astype(BF16)
        q.append(jnp.dot(h, wq_ref[...], preferred_element_type=F32))
    for hd in range(M_HEADS):
        cols = slice(hd * M_HEAD_DIM, (hd + 1) * M_HEAD_DIM)
        for t in range(2):
            qh = (_rms(q[t][:, cols]) * mqg_ref[...]).astype(BF16)
            s = lax.dot_general(qh, km_ref[0, :, cols], (((1,), (1,)), ((), ())), preferred_element_type=F32)
            e = jnp.exp(s - jnp.max(s, axis=-1, keepdims=True))
            pr = e / jnp.sum(e, axis=-1, keepdims=True)
            outs[t].append(jnp.dot(pr.astype(BF16), vm_ref[0, :, cols], preferred_element_type=F32).astype(BF16))
    for t, rows in enumerate(tiles):
        o_ref[rows, :] = x1[t] + jnp.dot(jnp.concatenate(outs[t], axis=-1), wmo_ref[...],
                                         preferred_element_type=F32)


def _outproj(x2d, a, bn, c, km, vm, seq, p):
    t = x2d.shape[0]
    tm = 2 * ROW_TILE
    tps = seq // tm
    ml = km.shape[1]
    row = lambda w: pl.BlockSpec((tm, w), lambda i: (i, 0))
    memblk = pl.BlockSpec((1, ml, M_W), lambda i: (i // tps, 0, 0))
    c1 = [p["ona"], p["onc"], p["w_out"], p["xg"], p["w_mem_q"], p["mqg"]]
    return pl.pallas_call(
        _outproj_kernel,
        grid=(t // tm,),
        in_specs=[row(D_MODEL), row(A_W), row(B_W), row(C_W)] + [_full(c_.shape) for c_ in c1]
                 + [memblk, memblk, _full(p["w_mem_out"].shape)],
        out_specs=row(D_MODEL),
        out_shape=jax.ShapeDtypeStruct((t, D_MODEL), F32),
        compiler_params=_cparams("parallel"),
        name="out_proj_mem_attn",
    )(x2d, a, bn, c, *c1, km, vm, p["w_mem_out"])


def _silu(x):
    return x * jax.nn.sigmoid(x)


def _ffn_kernel(x_ref, g_ref, wg_ref, wu_ref, wd_ref, o_ref, *, n_chunks):
    x = x_ref[...]
    h = (_rms(x) * g_ref[...]).astype(BF16)
    fc = wg_ref.shape[1] // n_chunks
    acc = x
    for c in range(n_chunks):
        cols = slice(c * fc, (c + 1) * fc)
        act = _silu(jnp.dot(h, wg_ref[:, cols], preferred_element_type=F32)) * \
            jnp.dot(h, wu_ref[:, cols], preferred_element_type=F32)
        acc = acc + jnp.dot(act.astype(BF16), wd_ref[cols, :], preferred_element_type=F32)
    o_ref[...] = acc


def _ffn(x2d, g, wg, wu, wd):
    t = x2d.shape[0]
    tm = ROW_TILE
    row = pl.BlockSpec((tm, D_MODEL), lambda i: (i, 0))
    resident = lambda a: pl.BlockSpec(a.shape, lambda i: (0, 0), pipeline_mode=pl.Buffered(1))
    return pl.pallas_call(
        functools.partial(_ffn_kernel, n_chunks=2),
        grid=(t // tm,),
        in_specs=[row, _full(g.shape), resident(wg), resident(wu), resident(wd)],
        out_specs=row,
        out_shape=jax.ShapeDtypeStruct((t, D_MODEL), F32),
        compiler_params=_cparams("parallel"),
        name="ffn_dense",
    )(x2d, g, wg, wu, wd)


def _pack_bf16_pairs(lo, hi):
    lo_bits = pltpu.bitcast(lo.astype(BF16).astype(F32), jnp.uint32)
    hi_bits = pltpu.bitcast(hi.astype(BF16).astype(F32), jnp.uint32)
    return lax.shift_right_logical(lo_bits, jnp.uint32(16)) | (hi_bits & jnp.uint32(0xFFFF0000))


def _unpack_bf16_pairs(words):
    lo = pltpu.bitcast(lax.shift_left(words, jnp.uint32(16)), F32)
    hi = pltpu.bitcast(words & jnp.uint32(0xFFFF0000), F32)
    return lo, hi


def _pack_rows(v):
    q = D_MODEL // 4
    return _pack_bf16_pairs(v[:, 0:q], v[:, q:2 * q]), _pack_bf16_pairs(v[:, 2 * q:3 * q], v[:, 3 * q:])


def _unpack_rows(a, b):
    return jnp.concatenate([*_unpack_bf16_pairs(a), *_unpack_bf16_pairs(b)], axis=-1)


def _route_kernel(x_ref, g_ref, wr_ref, br_ref, tri_ref, ha_ref, hb_ref, route_ref, cnt_ref):
    lane = _lane_iota()
    h = _rms(x_ref[...]) * g_ref[...]
    ha_ref[...], hb_ref[...] = _pack_rows(h)
    h_hi = h.astype(BF16)
    h_lo = (h - h_hi.astype(F32)).astype(BF16)
    logits = (jnp.dot(h_hi, wr_ref[0], preferred_element_type=F32)
              + jnp.dot(h_lo, wr_ref[0], preferred_element_type=F32)
              + jnp.dot(h_hi, wr_ref[1], preferred_element_type=F32)) + br_ref[...]
    logits = jnp.where(lane < N_EXPERTS, logits, -jnp.inf)
    v1 = jnp.max(logits, axis=-1, keepdims=True)
    i1 = jnp.min(jnp.where(logits == v1, lane, LANES), axis=-1, keepdims=True)
    rest = jnp.where(lane == i1, -jnp.inf, logits)
    v2 = jnp.max(rest, axis=-1, keepdims=True)
    i2 = jnp.min(jnp.where(rest == v2, lane, LANES), axis=-1, keepdims=True)
    e2 = jnp.exp(v2 - v1)
    g1 = 1.0 / (1.0 + e2)
    hit1, hit2 = lane == i1, lane == i2
    ones = jnp.where(hit1 | hit2, 1.0, 0.0)
    before = jnp.dot(tri_ref[...], ones.astype(BF16), preferred_element_type=F32)
    r1 = jnp.sum(jnp.where(hit1, before, 0.0), axis=-1, keepdims=True)
    r2 = jnp.sum(jnp.where(hit2, before, 0.0), axis=-1, keepdims=True)
    cols = [i1.astype(F32), i2.astype(F32), g1, e2 * g1, r1, r2]
    route = jnp.zeros(route_ref.shape, F32)
    for n, c in enumerate(cols):
        route = jnp.where(lane == n, c, route)
    route_ref[...] = route
    cnt_ref[0] = jnp.broadcast_to(jnp.sum(ones, axis=0, keepdims=True), cnt_ref.shape[1:])


def _route(x2d, g, wr, br):
    t = x2d.shape[0]
    tm = MOE_CHUNK
    tri = jnp.asarray(np.tril(np.ones((tm, tm), np.float32), -1)).astype(BF16)
    row = lambda w: pl.BlockSpec((tm, w), lambda i: (i, 0))
    q = D_MODEL // 4
    return pl.pallas_call(
        _route_kernel,
        grid=(t // tm,),
        in_specs=[row(D_MODEL), _full(g.shape), _full(wr.shape), _full(br.shape), _full(tri.shape)],
        out_specs=[row(q), row(q), row(LANES), pl.BlockSpec((1, 8, LANES), lambda i: (i, 0, 0))],
        out_shape=[jax.ShapeDtypeStruct((t, q), jnp.uint32), jax.ShapeDtypeStruct((t, q), jnp.uint32),
                   jax.ShapeDtypeStruct((t, LANES), F32), jax.ShapeDtypeStruct((t // tm, 8, LANES), F32)],
        compiler_params=_cparams("parallel"),
        name="moe_route",
    )(x2d, g, wr, br, tri)


def _dest_kernel(route_ref, base_ref, o_ref):
    lane = _lane_iota()
    r = route_ref[...]
    base = base_ref[0, 0:1, :]
    lane_f = lane.astype(F32)
    d1 = jnp.sum(jnp.where(lane_f == r[:, 0:1], base, 0.0), axis=-1, keepdims=True) + r[:, 4:5]
    d2 = jnp.sum(jnp.where(lane_f == r[:, 1:2], base, 0.0), axis=-1, keepdims=True) + r[:, 5:6]
    both = jnp.where(lane == 0, d1, jnp.where(lane == 1, d2, 0.0))
    o_ref[...] = both.T[0:8, :].astype(jnp.int32)


def _destinations(route, base):
    t = route.shape[0]
    tm = MOE_CHUNK
    out = pl.pallas_call(
        _dest_kernel,
        grid=(t // tm,),
        in_specs=[pl.BlockSpec((tm, LANES), lambda i: (i, 0)), pl.BlockSpec((1, 8, LANES), lambda i: (i, 0, 0))],
        out_specs=pl.BlockSpec((8, tm), lambda i: (0, i)),
        out_shape=jax.ShapeDtypeStruct((8, t), jnp.int32),
        compiler_params=_cparams("parallel"),
        name="moe_dest",
    )(route, base)
    return out[0:2].reshape(1, 2 * t)


SC_WINDOW = 128


def _sc_mesh():
    return plsc.VectorSubcoreMesh(core_axis_name="core", subcore_axis_name="subcore")


def _sc_scatter_rows(x, idx, n_out):
    n, d = x.shape
    m = idx.shape[1]
    nblk = n // SC_WINDOW

    @pl.kernel(out_type=jax.ShapeDtypeStruct((n_out, d), x.dtype), mesh=_sc_mesh(), name="moe_sc_scatter")
    def scatter(x_hbm, i_hbm, o_hbm):
        def body(x_vmem, i_vmem):
            pltpu.sync_copy(x_vmem, o_hbm.at[i_vmem.at[0]])

        half = m // SC_WINDOW // 2
        pltpu.emit_pipeline(
            body,
            grid=(2, half),
            in_specs=[pl.BlockSpec((SC_WINDOW, d), lambda c, j: ((c * half + j) % nblk, 0)),
                      pl.BlockSpec((1, SC_WINDOW), lambda c, j: (0, c * half + j))],
            out_specs=[],
            core_axis_name=("core", "subcore"),
            dimension_semantics=(pltpu.PARALLEL, pltpu.PARALLEL),
        )(x_hbm, i_hbm)

    return scatter(x, idx)


def _sc_gather_rows(table, idx):
    d = table.shape[1]
    m = idx.shape[1]

    @pl.kernel(out_type=jax.ShapeDtypeStruct((m, d), table.dtype), mesh=_sc_mesh(), name="moe_sc_gather")
    def gather(t_hbm, i_hbm, o_hbm):
        def body(i_vmem, o_vmem):
            pltpu.sync_copy(t_hbm.at[i_vmem.at[0]], o_vmem)

        half = m // SC_WINDOW // 2
        pltpu.emit_pipeline(
            body,
            grid=(2, half),
            in_specs=[pl.BlockSpec((1, SC_WINDOW), lambda c, j: (0, c * half + j))],
            out_specs=[pl.BlockSpec((SC_WINDOW, d), lambda c, j: (c * half + j, 0))],
            core_axis_name=("core", "subcore"),
            dimension_semantics=(pltpu.PARALLEL, pltpu.PARALLEL),
        )(i_hbm, o_hbm)

    return gather(table, idx)


def _expert_kernel(blk_expert_ref, n_used_ref, xa_ref, xb_ref, wg_ref, wu_ref, wd_ref, ya_ref, yb_ref):
    del blk_expert_ref

    @pl.when(pl.program_id(0) < n_used_ref[0])
    def _():
        xe = _unpack_rows(xa_ref[...], xb_ref[...]).astype(BF16)
        act = _silu(jnp.dot(xe, wg_ref[0], preferred_element_type=F32)) * \
            jnp.dot(xe, wu_ref[0], preferred_element_type=F32)
        y = jnp.dot(act.astype(BF16), wd_ref[0], preferred_element_type=F32)
        ya_ref[...], yb_ref[...] = _pack_rows(y)

    @pl.when(pl.program_id(0) >= n_used_ref[0])
    def _():
        ya_ref[...] = jnp.zeros(ya_ref.shape, ya_ref.dtype)
        yb_ref[...] = jnp.zeros(yb_ref.shape, yb_ref.dtype)


def _experts(blk_expert, n_used, xa, xb, wg, wu, wd):
    n_rows, q = xa.shape
    ff = wg.shape[2]
    blk = MOE_BLOCK
    row = pl.BlockSpec((blk, q), lambda b, be, nu: (b, 0))
    grid_spec = pltpu.PrefetchScalarGridSpec(
        num_scalar_prefetch=2,
        grid=(n_rows // blk,),
        in_specs=[row, row,
                  pl.BlockSpec((1, D_MODEL, ff), lambda b, be, nu: (be[b], 0, 0)),
                  pl.BlockSpec((1, D_MODEL, ff), lambda b, be, nu: (be[b], 0, 0)),
                  pl.BlockSpec((1, ff, D_MODEL), lambda b, be, nu: (be[b], 0, 0))],
        out_specs=[row, row])
    return pl.pallas_call(
        _expert_kernel,
        grid_spec=grid_spec,
        out_shape=[jax.ShapeDtypeStruct((n_rows, q), jnp.uint32)] * 2,
        compiler_params=_cparams("arbitrary"),
        name="moe_experts",
    )(blk_expert, n_used, xa, xb, wg, wu, wd)


def _combine_kernel(x_ref, route_ref, a1_ref, b1_ref, a2_ref, b2_ref, o_ref):
    g1 = route_ref[:, 2:3]
    g2 = route_ref[:, 3:4]
    o_ref[...] = x_ref[...] + g1 * _unpack_rows(a1_ref[...], b1_ref[...]) \
        + g2 * _unpack_rows(a2_ref[...], b2_ref[...])


def _combine(x2d, route, ya, yb):
    t = x2d.shape[0]
    tm = ROW_TILE
    nt = t // tm
    q = ya.shape[1]
    row = lambda w: pl.BlockSpec((tm, w), lambda i: (i, 0))
    first = pl.BlockSpec((tm, q), lambda i: (i, 0))
    second = pl.BlockSpec((tm, q), lambda i: (nt + i, 0))
    return pl.pallas_call(
        _combine_kernel,
        grid=(nt,),
        in_specs=[row(D_MODEL), row(LANES), first, first, second, second],
        out_specs=row(D_MODEL),
        out_shape=jax.ShapeDtypeStruct((t, D_MODEL), F32),
        compiler_params=_cparams("parallel"),
        name="moe_combine",
    )(x2d, route, ya, yb, ya, yb)


def _moe_sorted(x2d, g, wr, br, wg, wu, wd):
    t = x2d.shape[0]
    blk = MOE_BLOCK
    n_blocks = 2 * t // blk + N_EXPERTS
    ha, hb, route, cnt = _route(x2d, g, wr, br)

    cnt = cnt[:, 0, :N_EXPERTS].astype(jnp.int32)
    before_chunk = jnp.cumsum(cnt, axis=0) - cnt
    seg_blocks = (jnp.sum(cnt, axis=0) + blk - 1) // blk
    seg_end_blk = jnp.cumsum(seg_blocks)
    seg_start = (seg_end_blk - seg_blocks) * blk
    base = jnp.pad((seg_start[None, :] + before_chunk).astype(F32), ((0, 0), (0, LANES - N_EXPERTS)))
    idx = _destinations(route, jnp.broadcast_to(base[:, None, :], (base.shape[0], 8, LANES)))
    past_end = jnp.arange(n_blocks, dtype=jnp.int32)[:, None] >= seg_end_blk[None, :]
    blk_expert = jnp.minimum(jnp.sum(past_end, axis=1), N_EXPERTS - 1).astype(jnp.int32)
    n_used = seg_end_blk[-1:].astype(jnp.int32)

    xa = _sc_scatter_rows(ha, idx, n_blocks * blk)
    xb = _sc_scatter_rows(hb, idx, n_blocks * blk)
    ya, yb = _experts(blk_expert, n_used, xa, xb, wg, wu, wd)
    return _combine(x2d, route, _sc_gather_rows(ya, idx), _sc_gather_rows(yb, idx))


def _pad_cols(w, width):
    return jnp.pad(w, ((0, 0), (0, width - w.shape[1])))


def _layer_params(l, mix_norm, w_in, b_forget, a_q_norm, a_k_norm, b_v_norm, b_spatial_w, b_spatial_b,
                  c_q_lat_norm, c_w_uq, c_kv_lat_norm, c_w_ukv, c_q_nope_norm, c_q_rope_norm,
                  c_k_nope_norm, c_k_rope_norm, out_norm_a, out_norm_b, out_norm_c, w_out,
                  xattn_norm, mem_norm, w_mem_q, w_mem_kv, m_q_norm, m_k_norm, w_mem_out):
    p = {}
    o = np.cumsum((0, A_W, A_W, A_W, A_HEADS, B_W, B_W, C_Q_RANK, C_KV_RANK, C_ROPE_DIM))
    w = w_in[l]
    seg = lambda n: w[:, o[n]:o[n + 1]]
    fa = seg(3)
    misc = jnp.zeros((D_MODEL, LANES), F32)
    misc = misc.at[:, ROPE_LANE:ROPE_LANE + C_ROPE_DIM].set(seg(8))
    fb = jnp.zeros((1, LANES), F32)
    for hd in range(A_HEADS):
        ln = FORGET_LANE + 8 * (hd // 2) + hd % 2
        misc = misc.at[:, ln].set(fa[:, hd])
        fb = fb.at[0, ln].set(b_forget[l, hd])
    p["w_in"] = jnp.concatenate([seg(0), seg(1), seg(2), seg(4), seg(5), seg(6), seg(7), misc], axis=1).astype(BF16)
    p["fb"] = fb
    p["mix_g"] = mix_norm[l][None]
    p["aq"] = jnp.tile(a_q_norm[l], A_HEADS)[None] * (A_HEAD_DIM ** -0.5 * LOG2E)
    p["ak"] = jnp.tile(a_k_norm[l], A_HEADS)[None]
    p["bvg"] = b_v_norm[l][None]
    pos = np.arange(B_WINDOW)
    mask = (pos[None, :] // CHUNK) <= (pos[:, None] // CHUNK)
    p["ws"] = jnp.where(mask[None], b_spatial_w[l], 0.0).reshape(B_GROUPS * B_WINDOW, B_WINDOW).astype(BF16)
    p["bs"] = jnp.repeat(b_spatial_b[l].T, B_GROUP_DIM, axis=1)
    p["onb"] = out_norm_b[l][None]
    p["cqg"] = c_q_lat_norm[l][None]
    p["ckvg"] = c_kv_lat_norm[l][None]
    gidx = np.arange(A_W) // A_HEAD_DIM
    p["gm"] = jnp.asarray((gidx[:, None] == gidx[None, :]).astype(np.float32) / A_HEAD_DIM).astype(BF16)
    p["tri"] = jnp.asarray(np.tril(np.ones((ROW_TILE, ROW_TILE), np.float32))).astype(BF16)

    qd = C_NOPE_DIM + C_ROPE_DIM
    half = C_ROPE_DIM // 2
    wq = c_w_uq[l]
    wq_partner = jnp.concatenate([jnp.zeros_like(wq[:, :, :C_NOPE_DIM]), wq[:, :, C_NOPE_DIM + half:],
                                  wq[:, :, C_NOPE_DIM:C_NOPE_DIM + half]], axis=-1)
    p["wuq"] = jnp.pad(jnp.stack([wq, wq_partner]), ((0, 0), (0, 0), (0, 0), (0, HEAD_SLAB - qd))
                       ).reshape(2, C_Q_RANK, -1).astype(BF16)
    wukv = c_w_ukv[l]
    p["wuk"] = jnp.pad(wukv[:, :, :C_NOPE_DIM], ((0, 0), (0, 0), (0, HEAD_SLAB - C_NOPE_DIM))
                       ).reshape(C_KV_RANK, -1).astype(BF16)
    p["wuv"] = wukv[:, :, C_NOPE_DIM:].reshape(C_KV_RANK, C_W).astype(BF16)
    gq = jnp.concatenate([c_q_nope_norm[l], c_q_rope_norm[l]])
    gq_partner = jnp.concatenate([jnp.zeros_like(c_q_nope_norm[l]), c_q_rope_norm[l][half:],
                                  c_q_rope_norm[l][:half]])
    p["gq"] = _pad_cols(jnp.stack([gq, gq_partner]) * (qd ** -0.5 * LOG2E), LANES)
    p["gkn"] = _pad_cols(c_k_nope_norm[l][None], LANES)
    seg = np.zeros((LANES, LANES), np.float32)
    seg[:C_NOPE_DIM, :C_NOPE_DIM] = 1.0 / C_NOPE_DIM
    seg[ROPE_LANE:ROPE_LANE + C_ROPE_DIM, ROPE_LANE:ROPE_LANE + C_ROPE_DIM] = 1.0 / C_ROPE_DIM
    p["seg"] = jnp.asarray(seg).astype(BF16)
    p["gkr"] = jnp.zeros((1, LANES), F32).at[0, ROPE_LANE:ROPE_LANE + C_ROPE_DIM].set(c_k_rope_norm[l])

    p["ona"] = out_norm_a[l][None]
    p["onc"] = out_norm_c[l][None]
    p["w_out"] = w_out[l].astype(BF16)
    p["xg"] = xattn_norm[l][None]
    p["w_mem_q"] = w_mem_q[l].astype(BF16)
    p["mqg"] = m_q_norm[l][None] * (M_HEAD_DIM ** -0.5)
    p["mem_g"] = mem_norm[l][None]
    p["w_mem_kv"] = w_mem_kv[l].astype(BF16)
    p["mkg"] = m_k_norm[l][None]
    p["w_mem_out"] = w_mem_out[l].astype(BF16)
    return p


def kernel(x, mem, positions, mix_norm, w_in, b_forget, a_q_norm, a_k_norm, b_v_norm, b_spatial_w, b_spatial_b, c_q_lat_norm, c_w_uq, c_kv_lat_norm, c_w_ukv, c_q_nope_norm, c_q_rope_norm, c_k_nope_norm, c_k_rope_norm, out_norm_a, out_norm_b, out_norm_c, w_out, xattn_norm, mem_norm, w_mem_q, w_mem_kv, m_q_norm, m_k_norm, w_mem_out, ffn_norm, ffn_w_gate, ffn_w_up, ffn_w_down, w_router, b_router, moe_w_gate, moe_w_up, moe_w_down):
    nb, seq, d = x.shape
    assert d == D_MODEL and seq % (2 * ROW_TILE) == 0 and seq % ATTN_TILE == 0
    assert (nb * seq) % MOE_CHUNK == 0 and (nb * seq) % SC_WINDOW == 0 and (2 * nb * seq) % MOE_BLOCK == 0
    depth = w_in.shape[0]
    t = nb * seq
    x2d = x.reshape(t, d)
    cos, sin = _rope_tables(positions.reshape(t, 1).astype(F32))

    for l in range(depth):
        p = _layer_params(l, mix_norm, w_in, b_forget, a_q_norm, a_k_norm, b_v_norm, b_spatial_w,
                          b_spatial_b, c_q_lat_norm, c_w_uq, c_kv_lat_norm, c_w_ukv, c_q_nope_norm,
                          c_q_rope_norm, c_k_nope_norm, c_k_rope_norm, out_norm_a, out_norm_b,
                          out_norm_c, w_out, xattn_norm, mem_norm, w_mem_q, w_mem_kv, m_q_norm,
                          m_k_norm, w_mem_out)
        qa, ka, vta, bn, cq, ckv, misc = _inproj(x2d, seq, p)
        qc, kc, vtc = _mla_prep(cq, ckv, misc, cos, sin, seq, p)
        a = _attention(qa, ka, vta, unit=1, name="attn_fox")
        c = _attention(qc, kc, vtc, unit=CHUNK, name="attn_mla")
        km, vm = _mem_kv(mem, p)
        x2d = _outproj(x2d, a, bn, c, km, vm, seq, p)
        g = ffn_norm[l][None]
        if l % 2 == 0:
            m = l // 2
            ff = ffn_w_gate.shape[2]
            ff_pad = -(-ff // (2 * LANES)) * (2 * LANES)
            wg = _pad_cols(ffn_w_gate[m], ff_pad).astype(BF16)
            wu = _pad_cols(ffn_w_up[m], ff_pad).astype(BF16)
            wd = jnp.pad(ffn_w_down[m], ((0, ff_pad - ff), (0, 0))).astype(BF16)
            x2d = _ffn(x2d, g, wg, wu, wd)
        else:
            m = l // 2
            wr = _pad_cols(w_router[m], LANES)
            wr_hi = wr.astype(BF16)
            wr = jnp.stack([wr_hi, (wr - wr_hi.astype(F32)).astype(BF16)])
            br = _pad_cols(b_router[m][None], LANES)
            x2d = _moe_sorted(x2d, g, wr, br, moe_w_gate[m].astype(BF16), moe_w_up[m].astype(BF16),
                              moe_w_down[m].astype(BF16))
    return x2d.reshape(nb, seq, d)
```
